```python
import math
import jax, jax.numpy as jnp
from jax import lax
import numpy as np

D_MODEL = 1024
BATCH = 8
SEQ = 4096
DEPTH = 4

EPS = 1e-6
D_MIX = D_MODEL

SSD_HEADS = 8
SSD_HEAD_DIM = 64
SSD_INNER = SSD_HEADS * SSD_HEAD_DIM
SSD_GROUPS = 2
SSD_HPG = SSD_HEADS // SSD_GROUPS
SSD_STATE = 128
SSD_CONV = 4
SSD_CHUNK = 128
SSD_XBC = SSD_INNER + 2 * SSD_GROUPS * SSD_STATE
DT_MIN = 0.001
DT_MAX = 0.1

MLA_HEADS = 4
MLA_NOPE = 64
MLA_ROPE = 32
MLA_V = 64
MLA_Q_RANK = 256
MLA_KV_RANK = 128
ROPE_THETA = 10000.0
MAX_POS_OFFSET = 1024

SWA_HEADS = 4
SWA_KV_HEADS = 2
SWA_HEAD_DIM = 64
WINDOW = 128

ATTN_BLOCK = 128

SSD_IN = SSD_INNER + SSD_XBC + SSD_HEADS
MLA_IN = MLA_Q_RANK + MLA_KV_RANK + MLA_ROPE
SWA_IN = (SWA_HEADS + 2 * SWA_KV_HEADS) * SWA_HEAD_DIM
D_IN = SSD_IN + MLA_IN + SWA_IN

D_FF = 2816
FFN_CONV = 3

kernel_name = "hybrid_ssd_mla_swa_convffn_adaln"


def rmsnorm(x, g):
    xf = x.astype(jnp.float32)
    y = xf * lax.rsqrt(jnp.mean(xf * xf, axis=-1, keepdims=True) + EPS)
    return (y * g.astype(jnp.float32)).astype(x.dtype)


def causal_dwconv(x, w, b):
    k_width = w.shape[0]
    s = x.shape[1]
    xp = jnp.pad(x, ((0, 0), (k_width - 1, 0), (0, 0)))
    y = b
    for k in range(k_width):
        y = y + xp[:, k:k + s] * w[k]
    return y


def apply_rope(x, cos, sin):
    xf = x.astype(jnp.float32)
    x1, x2 = jnp.split(xf, 2, axis=-1)
    return jnp.concatenate([x1 * cos - x2 * sin, x1 * sin + x2 * cos], axis=-1).astype(x.dtype)


def ssd_mixer(p, conv_w, conv_b, dt_bias, a_log, d_skip, norm_g):
    f32 = jnp.float32
    bsz, s, _ = p.shape
    nc, q = s // SSD_CHUNK, SSD_CHUNK
    z, xbc, dt = jnp.split(p, [SSD_INNER, SSD_INNER + SSD_XBC], axis=-1)
    xbc = jax.nn.silu(causal_dwconv(xbc, conv_w, conv_b))
    xs, bm, cm = jnp.split(xbc, [SSD_INNER, SSD_INNER + SSD_GROUPS * SSD_STATE], axis=-1)
    xs = xs.astype(f32).reshape(bsz, nc, q, SSD_GROUPS, SSD_HPG, SSD_HEAD_DIM)
    bm = bm.astype(f32).reshape(bsz, nc, q, SSD_GROUPS, SSD_STATE)
    cm = cm.astype(f32).reshape(bsz, nc, q, SSD_GROUPS, SSD_STATE)
    dt = jax.nn.softplus(dt.astype(f32) + dt_bias.astype(f32))
    a = -jnp.exp(a_log.astype(f32))
    dt_c = dt.reshape(bsz, nc, q, SSD_GROUPS, SSD_HPG)
    xdt = xs * dt_c[..., None]
    da = (dt * a).reshape(bsz, nc, q, SSD_HEADS).transpose(0, 1, 3, 2)
    a_h = jnp.cumsum(da, axis=-1).reshape(bsz, nc, SSD_GROUPS, SSD_HPG, q)
    causal = jnp.tril(jnp.ones((q, q), dtype=bool))
    seg = a_h[..., :, None] - a_h[..., None, :]
    decay = jnp.exp(jnp.where(causal, seg, -jnp.inf))
    cb = jnp.einsum('bclgn,bcsgn->bcgls', cm, bm)
    y_diag = jnp.einsum('bcghls,bcsghp->bclghp', cb[:, :, :, None] * decay, xdt)
    decay_to_end = jnp.exp(a_h[..., -1:] - a_h)
    states = jnp.einsum('bcsgn,bcghs,bcsghp->bcghpn', bm, decay_to_end, xdt)
    chunk_decay = jnp.exp(a_h[..., -1])

    def step(h, inp):
        st, dec = inp
        return h * dec[..., None, None] + st, h

    h0 = jnp.zeros((bsz, SSD_GROUPS, SSD_HPG, SSD_HEAD_DIM, SSD_STATE), f32)
    _, h_in = lax.scan(step, h0, (states.swapaxes(0, 1), chunk_decay.swapaxes(0, 1)))
    h_in = h_in.swapaxes(0, 1)
    y_off = jnp.einsum('bclgn,bcghpn,bcghl->bclghp', cm, h_in, jnp.exp(a_h))
    y = y_diag + y_off + xs * d_skip.astype(f32).reshape(SSD_GROUPS, SSD_HPG, 1)
    y = y.reshape(bsz, s, SSD_INNER) * jax.nn.silu(z.astype(f32))
    yg = y.reshape(bsz, s, SSD_GROUPS, SSD_INNER // SSD_GROUPS)
    yg = yg * lax.rsqrt(jnp.mean(yg * yg, axis=-1, keepdims=True) + EPS)
    y = yg.reshape(bsz, s, SSD_INNER) * norm_g.astype(f32)
    return y.astype(p.dtype)


def causal_block_attention(q, k, v, scale):
    bsz, s, h, dk = q.shape
    nb = s // ATTN_BLOCK
    qb = q.reshape(bsz, nb, ATTN_BLOCK, h, dk).swapaxes(0, 1)
    kpos = jnp.arange(s)

    def one_block(args):
        qi, i = args
        sc = jnp.einsum('bqhd,bkhd->bhqk', qi, k, preferred_element_type=jnp.float32) * scale
        qpos = i * ATTN_BLOCK + jnp.arange(ATTN_BLOCK)
        sc = jnp.where(kpos[None, :] <= qpos[:, None], sc, -jnp.inf)
        pr = jax.nn.softmax(sc, axis=-1).astype(v.dtype)
        return jnp.einsum('bhqk,bkhd->bqhd', pr, v)

    out = lax.map(one_block, (qb, jnp.arange(nb)))
    return out.swapaxes(0, 1).reshape(bsz, s, h * v.shape[-1])


def mla_mixer(p, q_norm_g, w_uq, kv_norm_g, w_ukv, cos, sin):
    bsz, s, _ = p.shape
    cq, ckv, k_rope = jnp.split(p, [MLA_Q_RANK, MLA_Q_RANK + MLA_KV_RANK], axis=-1)
    qh = (rmsnorm(cq, q_norm_g) @ w_uq).reshape(bsz, s, MLA_HEADS, MLA_NOPE + MLA_ROPE)
    q_nope, q_rope = jnp.split(qh, [MLA_NOPE], axis=-1)
    q_rope = apply_rope(q_rope, cos, sin)
    kv = (rmsnorm(ckv, kv_norm_g) @ w_ukv).reshape(bsz, s, MLA_HEADS, MLA_NOPE + MLA_V)
    k_nope, v = jnp.split(kv, [MLA_NOPE], axis=-1)
    k_rope = apply_rope(k_rope[:, :, None, :], cos, sin)
    qf = jnp.concatenate([q_nope, q_rope], axis=-1)
    kf = jnp.concatenate([k_nope, jnp.broadcast_to(k_rope, (bsz, s, MLA_HEADS, MLA_ROPE))], axis=-1)
    return causal_block_attention(qf, kf, v, 1.0 / math.sqrt(MLA_NOPE + MLA_ROPE))


def swa_sink_mixer(p, sinks):
    bsz, s, _ = p.shape
    grp = SWA_HEADS // SWA_KV_HEADS
    nb = s // WINDOW
    qd = SWA_HEADS * SWA_HEAD_DIM
    kd = SWA_KV_HEADS * SWA_HEAD_DIM
    q, k, v = jnp.split(p, [qd, qd + kd], axis=-1)
    qb = q.reshape(bsz, nb, WINDOW, SWA_KV_HEADS, grp, SWA_HEAD_DIM)

    def band(t):
        tp = jnp.pad(t.reshape(bsz, s, SWA_KV_HEADS, SWA_HEAD_DIM), ((0, 0), (WINDOW, 0), (0, 0), (0, 0)))
        tp = tp.reshape(bsz, nb + 1, WINDOW, SWA_KV_HEADS, SWA_HEAD_DIM)
        return jnp.concatenate([tp[:, :-1], tp[:, 1:]], axis=2)

    kb, vb = band(k), band(v)
    sc = jnp.einsum('bnqhgd,bnkhd->bnhgqk', qb, kb, preferred_element_type=jnp.float32)
    sc = sc * (1.0 / math.sqrt(SWA_HEAD_DIM))
    i = jnp.arange(WINDOW)[:, None]
    j = jnp.arange(2 * WINDOW)[None, :]
    rel = WINDOW + i - j
    blk = jnp.arange(nb)[:, None, None]
    valid = (rel >= 0) & (rel < WINDOW) & ((blk - 1) * WINDOW + j >= 0)
    sc = jnp.where(valid[None, :, None, None], sc, -jnp.inf)
    sink = jnp.broadcast_to(sinks.astype(jnp.float32).reshape(1, 1, SWA_KV_HEADS, grp, 1, 1),
                            sc.shape[:-1] + (1,))
    pr = jax.nn.softmax(jnp.concatenate([sc, sink], axis=-1), axis=-1)[..., :-1]
    o = jnp.einsum('bnhgqk,bnkhd->bnqhgd', pr.astype(vb.dtype), vb)
    return o.reshape(bsz, s, qd)


def _fwd_setup_inputs(seed: int = 0) -> dict:
    key = jax.random.key(seed)
    ks = iter(jax.random.split(key, 32))
    f32 = jnp.float32
    L = DEPTH

    def nrm(shape, scale):
        return jax.random.normal(next(ks), shape, f32) * scale

    def gain(shape):
        return 1.0 + nrm(shape, 0.02)

    x = nrm((BATCH, SEQ, D_MODEL), 1.0)
    c = nrm((BATCH, D_MODEL), 1.0)
    offs = jax.random.randint(next(ks), (BATCH, 1), 0, MAX_POS_OFFSET, dtype=jnp.int32)
    positions = offs + jnp.arange(SEQ, dtype=jnp.int32)[None, :]
    ada_w = nrm((L, D_MODEL, 6 * D_MODEL), 0.5 * D_MODEL ** -0.5)
    ada_b = nrm((L, 6 * D_MODEL), 0.02)
    norm1_g = gain((L, D_MODEL))
    norm2_g = gain((L, D_MODEL))
    w_in = nrm((L, D_MODEL, D_IN), D_MODEL ** -0.5)
    ssd_conv_w = nrm((L, SSD_CONV, SSD_XBC), SSD_CONV ** -0.5)
    ssd_conv_b = nrm((L, SSD_XBC), 0.02)
    dt0 = jnp.exp(jax.random.uniform(next(ks), (L, SSD_HEADS), f32)
                  * (math.log(DT_MAX) - math.log(DT_MIN)) + math.log(DT_MIN))
    ssd_dt_bias = dt0 + jnp.log(-jnp.expm1(-dt0))
    ssd_a_log = jnp.log(jax.random.uniform(next(ks), (L, SSD_HEADS), f32, 1.0, 16.0))
    ssd_d = 1.0 + nrm((L, SSD_HEADS), 0.1)
    ssd_norm_g = gain((L, SSD_INNER))
    mla_q_norm_g = gain((L, MLA_Q_RANK))
    mla_w_uq = nrm((L, MLA_Q_RANK, MLA_HEADS * (MLA_NOPE + MLA_ROPE)), MLA_Q_RANK ** -0.5)
    mla_kv_norm_g = gain((L, MLA_KV_RANK))
    mla_w_ukv = nrm((L, MLA_KV_RANK, MLA_HEADS * (MLA_NOPE + MLA_V)), MLA_KV_RANK ** -0.5)
    swa_sinks = nrm((L, SWA_HEADS), 1.0)
    w_out = nrm((L, D_MIX, D_MODEL), D_MIX ** -0.5)
    ffn_w_up = nrm((L, D_MODEL, 2 * D_FF), D_MODEL ** -0.5)
    ffn_conv_w = nrm((L, FFN_CONV, 2 * D_FF), FFN_CONV ** -0.5)
    ffn_conv_b = nrm((L, 2 * D_FF), 0.02)
    ffn_w_down = nrm((L, D_FF, D_MODEL), D_FF ** -0.5)
    final_norm_g = gain((D_MODEL,))
    return {"x": x, "c": c, "positions": positions,
            "ada_w": ada_w, "ada_b": ada_b, "norm1_g": norm1_g, "norm2_g": norm2_g,
            "w_in": w_in, "ssd_conv_w": ssd_conv_w, "ssd_conv_b": ssd_conv_b,
            "ssd_dt_bias": ssd_dt_bias, "ssd_a_log": ssd_a_log, "ssd_d": ssd_d,
            "ssd_norm_g": ssd_norm_g, "mla_q_norm_g": mla_q_norm_g, "mla_w_uq": mla_w_uq,
            "mla_kv_norm_g": mla_kv_norm_g, "mla_w_ukv": mla_w_ukv, "swa_sinks": swa_sinks,
            "w_out": w_out, "ffn_w_up": ffn_w_up, "ffn_conv_w": ffn_conv_w,
            "ffn_conv_b": ffn_conv_b, "ffn_w_down": ffn_w_down, "final_norm_g": final_norm_g}


def _fwd_reference(x, c, positions, ada_w, ada_b, norm1_g, norm2_g, w_in, ssd_conv_w, ssd_conv_b,
              ssd_dt_bias, ssd_a_log, ssd_d, ssd_norm_g, mla_q_norm_g, mla_w_uq, mla_kv_norm_g,
              mla_w_ukv, swa_sinks, w_out, ffn_w_up, ffn_conv_w, ffn_conv_b, ffn_w_down,
              final_norm_g):
    inv_freq = ROPE_THETA ** (-jnp.arange(0, MLA_ROPE, 2, dtype=jnp.float32) / MLA_ROPE)
    ang = positions.astype(jnp.float32)[..., None] * inv_freq
    cos = jnp.cos(ang)[:, :, None, :]
    sin = jnp.sin(ang)[:, :, None, :]
    c_act = jax.nn.silu(c)
    for l in range(DEPTH):
        mod = c_act @ ada_w[l] + ada_b[l]
        sh1, sc1, g1, sh2, sc2, g2 = [m[:, None, :] for m in jnp.split(mod, 6, axis=-1)]
        h = rmsnorm(x, norm1_g[l]) * (1.0 + sc1) + sh1
        proj = h @ w_in[l]
        p_ssd, p_mla, p_swa = jnp.split(proj, [SSD_IN, SSD_IN + MLA_IN], axis=-1)
        y_ssd = ssd_mixer(p_ssd, ssd_conv_w[l], ssd_conv_b[l], ssd_dt_bias[l], ssd_a_log[l],
                          ssd_d[l], ssd_norm_g[l])
        y_mla = mla_mixer(p_mla, mla_q_norm_g[l], mla_w_uq[l], mla_kv_norm_g[l], mla_w_ukv[l],
                          cos, sin)
        y_swa = swa_sink_mixer(p_swa, swa_sinks[l])
        y = jnp.concatenate([y_ssd, y_mla, y_swa], axis=-1) @ w_out[l]
        x = x + g1 * y
        h = rmsnorm(x, norm2_g[l]) * (1.0 + sc2) + sh2
        u = causal_dwconv(h @ ffn_w_up[l], ffn_conv_w[l], ffn_conv_b[l])
        a, b = jnp.split(u, 2, axis=-1)
        x = x + g2 * ((jax.nn.silu(a) * b) @ ffn_w_down[l])
    return rmsnorm(x, final_norm_g)


import jax as _jax
import jax.numpy as _jnp

TWIN_FORMAT = 'train_step'
FWD_PARAMS = ['x', 'c', 'positions', 'ada_w', 'ada_b', 'norm1_g', 'norm2_g', 'w_in', 'ssd_conv_w', 'ssd_conv_b', 'ssd_dt_bias', 'ssd_a_log', 'ssd_d', 'ssd_norm_g', 'mla_q_norm_g', 'mla_w_uq', 'mla_kv_norm_g', 'mla_w_ukv', 'swa_sinks', 'w_out', 'ffn_w_up', 'ffn_conv_w', 'ffn_conv_b', 'ffn_w_down', 'final_norm_g']
TWIN_WEIGHTS = ['ada_w', 'ada_b', 'norm1_g', 'norm2_g', 'w_in', 'ssd_conv_w', 'ssd_conv_b', 'ssd_dt_bias', 'ssd_a_log', 'ssd_d', 'ssd_norm_g', 'mla_q_norm_g', 'mla_w_uq', 'mla_kv_norm_g', 'mla_w_ukv', 'swa_sinks', 'w_out', 'ffn_w_up', 'ffn_conv_w', 'ffn_conv_b', 'ffn_w_down', 'final_norm_g']
TWIN_DIFF_INPUT = 'x'
TWIN_INPUTS = ['x', 'c', 'positions', 'ada_w', 'ada_b', 'norm1_g', 'norm2_g', 'w_in', 'ssd_conv_w', 'ssd_conv_b', 'ssd_dt_bias', 'ssd_a_log', 'ssd_d', 'ssd_norm_g', 'mla_q_norm_g', 'mla_w_uq', 'mla_kv_norm_g', 'mla_w_ukv', 'swa_sinks', 'w_out', 'ffn_w_up', 'ffn_conv_w', 'ffn_conv_b', 'ffn_w_down', 'final_norm_g', 'loss_target', 'm_ada_w', 'm_ada_b', 'm_norm1_g', 'm_norm2_g', 'm_w_in', 'm_ssd_conv_w', 'm_ssd_conv_b', 'm_ssd_dt_bias', 'm_ssd_a_log', 'm_ssd_d', 'm_ssd_norm_g', 'm_mla_q_norm_g', 'm_mla_w_uq', 'm_mla_kv_norm_g', 'm_mla_w_ukv', 'm_swa_sinks', 'm_w_out', 'm_ffn_w_up', 'm_ffn_conv_w', 'm_ffn_conv_b', 'm_ffn_w_down', 'm_final_norm_g', 'v_ada_w', 'v_ada_b', 'v_norm1_g', 'v_norm2_g', 'v_w_in', 'v_ssd_conv_w', 'v_ssd_conv_b', 'v_ssd_dt_bias', 'v_ssd_a_log', 'v_ssd_d', 'v_ssd_norm_g', 'v_mla_q_norm_g', 'v_mla_w_uq', 'v_mla_kv_norm_g', 'v_mla_w_ukv', 'v_swa_sinks', 'v_w_out', 'v_ffn_w_up', 'v_ffn_conv_w', 'v_ffn_conv_b', 'v_ffn_w_down', 'v_final_norm_g']
TWIN_OUTPUTS = ['loss', 'grad_x', 'grad_ada_w', 'grad_ada_b', 'grad_norm1_g', 'grad_norm2_g', 'grad_w_in', 'grad_ssd_conv_w', 'grad_ssd_conv_b', 'grad_ssd_dt_bias', 'grad_ssd_a_log', 'grad_ssd_d', 'grad_ssd_norm_g', 'grad_mla_q_norm_g', 'grad_mla_w_uq', 'grad_mla_kv_norm_g', 'grad_mla_w_ukv', 'grad_swa_sinks', 'grad_w_out', 'grad_ffn_w_up', 'grad_ffn_conv_w', 'grad_ffn_conv_b', 'grad_ffn_w_down', 'grad_final_norm_g', 'delta_ada_w', 'delta_ada_b', 'delta_norm1_g', 'delta_norm2_g', 'delta_w_in', 'delta_ssd_conv_w', 'delta_ssd_conv_b', 'delta_ssd_dt_bias', 'delta_ssd_a_log', 'delta_ssd_d', 'delta_ssd_norm_g', 'delta_mla_q_norm_g', 'delta_mla_w_uq', 'delta_mla_kv_norm_g', 'delta_mla_w_ukv', 'delta_swa_sinks', 'delta_w_out', 'delta_ffn_w_up', 'delta_ffn_conv_w', 'delta_ffn_conv_b', 'delta_ffn_w_down', 'delta_final_norm_g', 'new_m_ada_w', 'new_m_ada_b', 'new_m_norm1_g', 'new_m_norm2_g', 'new_m_w_in', 'new_m_ssd_conv_w', 'new_m_ssd_conv_b', 'new_m_ssd_dt_bias', 'new_m_ssd_a_log', 'new_m_ssd_d', 'new_m_ssd_norm_g', 'new_m_mla_q_norm_g', 'new_m_mla_w_uq', 'new_m_mla_kv_norm_g', 'new_m_mla_w_ukv', 'new_m_swa_sinks', 'new_m_w_out', 'new_m_ffn_w_up', 'new_m_ffn_conv_w', 'new_m_ffn_conv_b', 'new_m_ffn_w_down', 'new_m_final_norm_g', 'new_v_ada_w', 'new_v_ada_b', 'new_v_norm1_g', 'new_v_norm2_g', 'new_v_w_in', 'new_v_ssd_conv_w', 'new_v_ssd_conv_b', 'new_v_ssd_dt_bias', 'new_v_ssd_a_log', 'new_v_ssd_d', 'new_v_ssd_norm_g', 'new_v_mla_q_norm_g', 'new_v_mla_w_uq', 'new_v_mla_kv_norm_g', 'new_v_mla_w_ukv', 'new_v_swa_sinks', 'new_v_w_out', 'new_v_ffn_w_up', 'new_v_ffn_conv_w', 'new_v_ffn_conv_b', 'new_v_ffn_w_down', 'new_v_final_norm_g']
TWIN_LEAF_KINDS = {'loss': 'loss', 'grad_x': 'grad_x', 'grad_ada_w': 'grad_w', 'grad_ada_b': 'grad_w', 'grad_norm1_g': 'grad_w', 'grad_norm2_g': 'grad_w', 'grad_w_in': 'grad_w', 'grad_ssd_conv_w': 'grad_w', 'grad_ssd_conv_b': 'grad_w', 'grad_ssd_dt_bias': 'grad_w', 'grad_ssd_a_log': 'grad_w', 'grad_ssd_d': 'grad_w', 'grad_ssd_norm_g': 'grad_w', 'grad_mla_q_norm_g': 'grad_w', 'grad_mla_w_uq': 'grad_w', 'grad_mla_kv_norm_g': 'grad_w', 'grad_mla_w_ukv': 'grad_w', 'grad_swa_sinks': 'grad_w', 'grad_w_out': 'grad_w', 'grad_ffn_w_up': 'grad_w', 'grad_ffn_conv_w': 'grad_w', 'grad_ffn_conv_b': 'grad_w', 'grad_ffn_w_down': 'grad_w', 'grad_final_norm_g': 'grad_w', 'delta_ada_w': 'delta_w', 'delta_ada_b': 'delta_w', 'delta_norm1_g': 'delta_w', 'delta_norm2_g': 'delta_w', 'delta_w_in': 'delta_w', 'delta_ssd_conv_w': 'delta_w', 'delta_ssd_conv_b': 'delta_w', 'delta_ssd_dt_bias': 'delta_w', 'delta_ssd_a_log': 'delta_w', 'delta_ssd_d': 'delta_w', 'delta_ssd_norm_g': 'delta_w', 'delta_mla_q_norm_g': 'delta_w', 'delta_mla_w_uq': 'delta_w', 'delta_mla_kv_norm_g': 'delta_w', 'delta_mla_w_ukv': 'delta_w', 'delta_swa_sinks': 'delta_w', 'delta_w_out': 'delta_w', 'delta_ffn_w_up': 'delta_w', 'delta_ffn_conv_w': 'delta_w', 'delta_ffn_conv_b': 'delta_w', 'delta_ffn_w_down': 'delta_w', 'delta_final_norm_g': 'delta_w', 'new_m_ada_w': 'new_m', 'new_m_ada_b': 'new_m', 'new_m_norm1_g': 'new_m', 'new_m_norm2_g': 'new_m', 'new_m_w_in': 'new_m', 'new_m_ssd_conv_w': 'new_m', 'new_m_ssd_conv_b': 'new_m', 'new_m_ssd_dt_bias': 'new_m', 'new_m_ssd_a_log': 'new_m', 'new_m_ssd_d': 'new_m', 'new_m_ssd_norm_g': 'new_m', 'new_m_mla_q_norm_g': 'new_m', 'new_m_mla_w_uq': 'new_m', 'new_m_mla_kv_norm_g': 'new_m', 'new_m_mla_w_ukv': 'new_m', 'new_m_swa_sinks': 'new_m', 'new_m_w_out': 'new_m', 'new_m_ffn_w_up': 'new_m', 'new_m_ffn_conv_w': 'new_m', 'new_m_ffn_conv_b': 'new_m', 'new_m_ffn_w_down': 'new_m', 'new_m_final_norm_g': 'new_m', 'new_v_ada_w': 'new_v', 'new_v_ada_b': 'new_v', 'new_v_norm1_g': 'new_v', 'new_v_norm2_g': 'new_v', 'new_v_w_in': 'new_v', 'new_v_ssd_conv_w': 'new_v', 'new_v_ssd_conv_b': 'new_v', 'new_v_ssd_dt_bias': 'new_v', 'new_v_ssd_a_log': 'new_v', 'new_v_ssd_d': 'new_v', 'new_v_ssd_norm_g': 'new_v', 'new_v_mla_q_norm_g': 'new_v', 'new_v_mla_w_uq': 'new_v', 'new_v_mla_kv_norm_g': 'new_v', 'new_v_mla_w_ukv': 'new_v', 'new_v_swa_sinks': 'new_v', 'new_v_w_out': 'new_v', 'new_v_ffn_w_up': 'new_v', 'new_v_ffn_conv_w': 'new_v', 'new_v_ffn_conv_b': 'new_v', 'new_v_ffn_w_down': 'new_v', 'new_v_final_norm_g': 'new_v'}


def _forward(args):
    return _fwd_reference(*[args[k] for k in FWD_PARAMS])


def _output_shape():
    out = _jax.eval_shape(lambda: _forward(_fwd_setup_inputs(0)))
    return out.shape, out.dtype

N_MICROBATCH = 1
ADAM_LR = 0.001
ADAM_B1 = 0.9
ADAM_B2 = 0.999
ADAM_EPS = 1e-08
ADAM_WD = 0.01
ADAM_STEP = 10
PER_EXAMPLE_BATCH_AXIS = {'x': 0, 'c': 0, 'positions': 0, 'loss_target': 0}
SHARED_INPUTS = []
_WEIGHT_DTYPES = {'ada_w': _jnp.float32, 'ada_b': _jnp.float32, 'norm1_g': _jnp.float32, 'norm2_g': _jnp.float32, 'w_in': _jnp.float32, 'ssd_conv_w': _jnp.float32, 'ssd_conv_b': _jnp.float32, 'ssd_dt_bias': _jnp.float32, 'ssd_a_log': _jnp.float32, 'ssd_d': _jnp.float32, 'ssd_norm_g': _jnp.float32, 'mla_q_norm_g': _jnp.float32, 'mla_w_uq': _jnp.float32, 'mla_kv_norm_g': _jnp.float32, 'mla_w_ukv': _jnp.float32, 'swa_sinks': _jnp.float32, 'w_out': _jnp.float32, 'ffn_w_up': _jnp.float32, 'ffn_conv_w': _jnp.float32, 'ffn_conv_b': _jnp.float32, 'ffn_w_down': _jnp.float32, 'final_norm_g': _jnp.float32}
MOMENT_SCALE = {'ada_w': 5.628498e-02, 'ada_b': 9.183427e-02, 'norm1_g': 5.881183e-02, 'norm2_g': 5.463367e-02, 'w_in': 3.981632e-02, 'ssd_conv_w': 4.268221e-02, 'ssd_conv_b': 5.140457e-02, 'ssd_dt_bias': 7.740074e-02, 'ssd_a_log': 1.850552e-01, 'ssd_d': 2.579329e-01, 'ssd_norm_g': 5.322301e-02, 'mla_q_norm_g': 9.587922e-03, 'mla_w_uq': 7.921466e-03, 'mla_kv_norm_g': 3.162949e-02, 'mla_w_ukv': 1.567756e-02, 'swa_sinks': 1.038150e-02, 'w_out': 4.108485e-02, 'ffn_w_up': 2.418685e-02, 'ffn_conv_w': 2.397287e-02, 'ffn_conv_b': 2.169361e-02, 'ffn_w_down': 3.933589e-02, 'final_norm_g': 3.206146e+01}


def _to_microbatches(a, axis):
    t = _jnp.moveaxis(a, axis, 0)
    t = t.reshape((N_MICROBATCH, t.shape[0] // N_MICROBATCH) + t.shape[1:])
    return _jnp.moveaxis(t, 1, axis + 1)


def setup_inputs(seed: int = 0) -> dict:
    inp = _fwd_setup_inputs(seed)
    key = _jax.random.fold_in(_jax.random.key(seed), 7919)
    shape, _ = _output_shape()
    out = dict(inp)
    out["loss_target"] = _jax.random.normal(_jax.random.fold_in(key, 0), shape, _jnp.float32)
    for i, name in enumerate(TWIN_WEIGHTS):
        w = inp[name].astype(_jnp.float32)
        if MOMENT_SCALE is None:
            s = _jnp.sqrt(_jnp.mean(_jnp.square(w)) + 1e-30)
        else:
            s = MOMENT_SCALE[name]
        km, kv = _jax.random.split(_jax.random.fold_in(key, i + 1))
        out[name] = w
        out["m_" + name] = s * _jax.random.normal(km, w.shape, _jnp.float32)
        out["v_" + name] = (s * s) * _jax.random.uniform(kv, w.shape, _jnp.float32, 0.5, 1.5)
    if N_MICROBATCH > 1:
        for name, axis in PER_EXAMPLE_BATCH_AXIS.items():
            out[name] = _to_microbatches(out[name], axis)
    return {'x': out['x'], 'c': out['c'], 'positions': out['positions'], 'ada_w': out['ada_w'], 'ada_b': out['ada_b'], 'norm1_g': out['norm1_g'], 'norm2_g': out['norm2_g'], 'w_in': out['w_in'], 'ssd_conv_w': out['ssd_conv_w'], 'ssd_conv_b': out['ssd_conv_b'], 'ssd_dt_bias': out['ssd_dt_bias'], 'ssd_a_log': out['ssd_a_log'], 'ssd_d': out['ssd_d'], 'ssd_norm_g': out['ssd_norm_g'], 'mla_q_norm_g': out['mla_q_norm_g'], 'mla_w_uq': out['mla_w_uq'], 'mla_kv_norm_g': out['mla_kv_norm_g'], 'mla_w_ukv': out['mla_w_ukv'], 'swa_sinks': out['swa_sinks'], 'w_out': out['w_out'], 'ffn_w_up': out['ffn_w_up'], 'ffn_conv_w': out['ffn_conv_w'], 'ffn_conv_b': out['ffn_conv_b'], 'ffn_w_down': out['ffn_w_down'], 'final_norm_g': out['final_norm_g'], 'loss_target': out['loss_target'], 'm_ada_w': out['m_ada_w'], 'm_ada_b': out['m_ada_b'], 'm_norm1_g': out['m_norm1_g'], 'm_norm2_g': out['m_norm2_g'], 'm_w_in': out['m_w_in'], 'm_ssd_conv_w': out['m_ssd_conv_w'], 'm_ssd_conv_b': out['m_ssd_conv_b'], 'm_ssd_dt_bias': out['m_ssd_dt_bias'], 'm_ssd_a_log': out['m_ssd_a_log'], 'm_ssd_d': out['m_ssd_d'], 'm_ssd_norm_g': out['m_ssd_norm_g'], 'm_mla_q_norm_g': out['m_mla_q_norm_g'], 'm_mla_w_uq': out['m_mla_w_uq'], 'm_mla_kv_norm_g': out['m_mla_kv_norm_g'], 'm_mla_w_ukv': out['m_mla_w_ukv'], 'm_swa_sinks': out['m_swa_sinks'], 'm_w_out': out['m_w_out'], 'm_ffn_w_up': out['m_ffn_w_up'], 'm_ffn_conv_w': out['m_ffn_conv_w'], 'm_ffn_conv_b': out['m_ffn_conv_b'], 'm_ffn_w_down': out['m_ffn_w_down'], 'm_final_norm_g': out['m_final_norm_g'], 'v_ada_w': out['v_ada_w'], 'v_ada_b': out['v_ada_b'], 'v_norm1_g': out['v_norm1_g'], 'v_norm2_g': out['v_norm2_g'], 'v_w_in': out['v_w_in'], 'v_ssd_conv_w': out['v_ssd_conv_w'], 'v_ssd_conv_b': out['v_ssd_conv_b'], 'v_ssd_dt_bias': out['v_ssd_dt_bias'], 'v_ssd_a_log': out['v_ssd_a_log'], 'v_ssd_d': out['v_ssd_d'], 'v_ssd_norm_g': out['v_ssd_norm_g'], 'v_mla_q_norm_g': out['v_mla_q_norm_g'], 'v_mla_w_uq': out['v_mla_w_uq'], 'v_mla_kv_norm_g': out['v_mla_kv_norm_g'], 'v_mla_w_ukv': out['v_mla_w_ukv'], 'v_swa_sinks': out['v_swa_sinks'], 'v_w_out': out['v_w_out'], 'v_ffn_w_up': out['v_ffn_w_up'], 'v_ffn_conv_w': out['v_ffn_conv_w'], 'v_ffn_conv_b': out['v_ffn_conv_b'], 'v_ffn_w_down': out['v_ffn_w_down'], 'v_final_norm_g': out['v_final_norm_g']}


def _loss(weights, diff, rest, loss_target):
    with _jax.named_scope("forward"):
        args = {**rest, TWIN_DIFF_INPUT: diff, **{k: w.astype(_WEIGHT_DTYPES[k]) for k, w in weights.items()}}
        y = _forward(args)
    with _jax.named_scope("loss_head"):
        err = _jnp.square(y.astype(_jnp.float32) - loss_target)
        return 0.5 * _jnp.sum(_jnp.mean(err, axis=-1)) if err.ndim else 0.5 * err


def _adamw(w, g, m, v):
    m = ADAM_B1 * m + (1.0 - ADAM_B1) * g
    v = ADAM_B2 * v + (1.0 - ADAM_B2) * _jnp.square(g)
    m_hat = m / (1.0 - ADAM_B1 ** ADAM_STEP)
    v_hat = v / (1.0 - ADAM_B2 ** ADAM_STEP)
    delta = -ADAM_LR * (m_hat / (_jnp.sqrt(v_hat) + ADAM_EPS) + ADAM_WD * w)
    return delta, m, v


def reference(x, c, positions, ada_w, ada_b, norm1_g, norm2_g, w_in, ssd_conv_w, ssd_conv_b, ssd_dt_bias, ssd_a_log, ssd_d, ssd_norm_g, mla_q_norm_g, mla_w_uq, mla_kv_norm_g, mla_w_ukv, swa_sinks, w_out, ffn_w_up, ffn_conv_w, ffn_conv_b, ffn_w_down, final_norm_g, loss_target, m_ada_w, m_ada_b, m_norm1_g, m_norm2_g, m_w_in, m_ssd_conv_w, m_ssd_conv_b, m_ssd_dt_bias, m_ssd_a_log, m_ssd_d, m_ssd_norm_g, m_mla_q_norm_g, m_mla_w_uq, m_mla_kv_norm_g, m_mla_w_ukv, m_swa_sinks, m_w_out, m_ffn_w_up, m_ffn_conv_w, m_ffn_conv_b, m_ffn_w_down, m_final_norm_g, v_ada_w, v_ada_b, v_norm1_g, v_norm2_g, v_w_in, v_ssd_conv_w, v_ssd_conv_b, v_ssd_dt_bias, v_ssd_a_log, v_ssd_d, v_ssd_norm_g, v_mla_q_norm_g, v_mla_w_uq, v_mla_kv_norm_g, v_mla_w_ukv, v_swa_sinks, v_w_out, v_ffn_w_up, v_ffn_conv_w, v_ffn_conv_b, v_ffn_w_down, v_final_norm_g):
    given = dict(x=x, c=c, positions=positions, ada_w=ada_w, ada_b=ada_b, norm1_g=norm1_g, norm2_g=norm2_g, w_in=w_in, ssd_conv_w=ssd_conv_w, ssd_conv_b=ssd_conv_b, ssd_dt_bias=ssd_dt_bias, ssd_a_log=ssd_a_log, ssd_d=ssd_d, ssd_norm_g=ssd_norm_g, mla_q_norm_g=mla_q_norm_g, mla_w_uq=mla_w_uq, mla_kv_norm_g=mla_kv_norm_g, mla_w_ukv=mla_w_ukv, swa_sinks=swa_sinks, w_out=w_out, ffn_w_up=ffn_w_up, ffn_conv_w=ffn_conv_w, ffn_conv_b=ffn_conv_b, ffn_w_down=ffn_w_down, final_norm_g=final_norm_g, loss_target=loss_target, m_ada_w=m_ada_w, m_ada_b=m_ada_b, m_norm1_g=m_norm1_g, m_norm2_g=m_norm2_g, m_w_in=m_w_in, m_ssd_conv_w=m_ssd_conv_w, m_ssd_conv_b=m_ssd_conv_b, m_ssd_dt_bias=m_ssd_dt_bias, m_ssd_a_log=m_ssd_a_log, m_ssd_d=m_ssd_d, m_ssd_norm_g=m_ssd_norm_g, m_mla_q_norm_g=m_mla_q_norm_g, m_mla_w_uq=m_mla_w_uq, m_mla_kv_norm_g=m_mla_kv_norm_g, m_mla_w_ukv=m_mla_w_ukv, m_swa_sinks=m_swa_sinks, m_w_out=m_w_out, m_ffn_w_up=m_ffn_w_up, m_ffn_conv_w=m_ffn_conv_w, m_ffn_conv_b=m_ffn_conv_b, m_ffn_w_down=m_ffn_w_down, m_final_norm_g=m_final_norm_g, v_ada_w=v_ada_w, v_ada_b=v_ada_b, v_norm1_g=v_norm1_g, v_norm2_g=v_norm2_g, v_w_in=v_w_in, v_ssd_conv_w=v_ssd_conv_w, v_ssd_conv_b=v_ssd_conv_b, v_ssd_dt_bias=v_ssd_dt_bias, v_ssd_a_log=v_ssd_a_log, v_ssd_d=v_ssd_d, v_ssd_norm_g=v_ssd_norm_g, v_mla_q_norm_g=v_mla_q_norm_g, v_mla_w_uq=v_mla_w_uq, v_mla_kv_norm_g=v_mla_kv_norm_g, v_mla_w_ukv=v_mla_w_ukv, v_swa_sinks=v_swa_sinks, v_w_out=v_w_out, v_ffn_w_up=v_ffn_w_up, v_ffn_conv_w=v_ffn_conv_w, v_ffn_conv_b=v_ffn_conv_b, v_ffn_w_down=v_ffn_w_down, v_final_norm_g=v_final_norm_g)
    weights = {n: given[n] for n in TWIN_WEIGHTS}
    shared = {n: given[n] for n in SHARED_INPUTS}
    per_example = {n: given[n] for n in ['x', 'c', 'positions']}
    grad_fn = _jax.value_and_grad(_loss, argnums=(0, 1))

    def one_microbatch(ex, loss_target):
        ex = dict(ex)
        diff = ex.pop(TWIN_DIFF_INPUT)
        return grad_fn(weights, diff, {**shared, **ex}, loss_target)

    if N_MICROBATCH == 1:
        loss, (grad_w, grad_x) = one_microbatch(per_example, given["loss_target"])
    else:
        def body(carry, xs):
            loss_sum, grad_sum = carry
            l_k, (gw_k, gx_k) = one_microbatch(xs[0], xs[1])
            with _jax.named_scope("update"):
                return (loss_sum + l_k, _jax.tree.map(_jnp.add, grad_sum, gw_k)), gx_k

        init = (_jnp.zeros((), _jnp.float32), _jax.tree.map(_jnp.zeros_like, weights))
        (loss, grad_w), grad_x = _jax.lax.scan(body, init, (per_example, given["loss_target"]))
    with _jax.named_scope("update"):
        delta_w, new_m, new_v = {}, {}, {}
        for n in TWIN_WEIGHTS:
            delta_w[n], new_m[n], new_v[n] = _adamw(weights[n], grad_w[n], given["m_" + n], given["v_" + n])
    return (loss, grad_x, *[grad_w[n] for n in TWIN_WEIGHTS], *[delta_w[n] for n in TWIN_WEIGHTS],
            *[new_m[n] for n in TWIN_WEIGHTS], *[new_v[n] for n in TWIN_WEIGHTS])
```

```python
import functools
import math

import jax
import jax.numpy as jnp
from jax import lax
from jax.experimental import pallas as pl
from jax.experimental.pallas import tpu as pltpu

F32 = jnp.float32
_MXU = jnp.bfloat16
_ACT = jnp.bfloat16
_HI = lax.Precision.HIGHEST
EPS = 1e-6
NDEV = 8
DEPTH = 4
D = 1024
LANE = 128
SUB = 8
VMEM_LIMIT = 56 * 1024 * 1024

SSD_INNER, SSD_STATE, SSD_HEADS, SSD_GROUPS, SSD_CHUNK, SSD_CONV = 512, 128, 8, 2, 128, 4
SSD_XBC = SSD_INNER + 2 * SSD_GROUPS * SSD_STATE
MLA_HEADS, MLA_NOPE, MLA_ROPE, MLA_V, MLA_QR, MLA_KVR = 4, 64, 32, 64, 256, 128
SWA_HEADS, SWA_KV, SWA_HD, WINDOW = 4, 2, 64, 128
D_FF, FFN_CONV = 2816, 3
D_IN = 2472
ROPE_THETA = 10000.0
C_XBC, C_Z, C_CQ, C_CKV, C_MISC, C_SQ, C_SK, C_SV, D_INP = 0, 1024, 1536, 1792, 1920, 2048, 2560, 2816, 3072
ROPE_LANE = 64
D_MIXP = 1536

ADAM_LR, ADAM_B1, ADAM_B2, ADAM_EPS, ADAM_WD, ADAM_STEP = 0.001, 0.9, 0.999, 1e-08, 0.01, 10

TS_ROW = 512
TS_FFN = 256
TQ_ATT = 512
TS_SWA = 512


def _tile(n, cap, q=LANE):
    best = None
    for t in range(q, min(n, cap) + 1, q):
        if n % t == 0:
            best = t
    return n if best is None else best


def _cp(ngrid):
    return pltpu.CompilerParams(dimension_semantics=("arbitrary",) * ngrid, vmem_limit_bytes=VMEM_LIMIT)


def _dot(a, b):
    return jnp.dot(a.astype(_MXU), b.astype(_MXU), preferred_element_type=F32)


def _dot_nt(a, b):
    return lax.dot_general(a.astype(_MXU), b.astype(_MXU), (((1,), (1,)), ((), ())), preferred_element_type=F32)


def _dot_tn(a, b):
    return jnp.dot(a.T.astype(_MXU), b.astype(_MXU), preferred_element_type=F32)


def _sigmoid(x):
    return 1.0 / (1.0 + jnp.exp(-x))


def _silu(x):
    return x * _sigmoid(x)


def _dsilu(x):
    s = _sigmoid(x)
    return s * (1.0 + x * (1.0 - s))


def _softplus(x):
    u = jnp.exp(-jnp.abs(x))
    w = 1.0 + u
    log1p = jnp.where(w == 1.0, u, jnp.log(w) * u / jnp.where(w == 1.0, 1.0, w - 1.0))
    return jnp.maximum(x, 0.0) + log1p


def _colsum(x):
    return jnp.sum(x, axis=0, keepdims=True)


def _rowsum(x):
    return jnp.sum(x, axis=1, keepdims=True)


def _shift_down(t, halo, j):
    if j == 0:
        return t
    n = t.shape[0]
    rolled = pltpu.roll(t, j, 0)
    row = lax.broadcasted_iota(jnp.int32, (SUB, t.shape[1]), 0)
    first = jnp.where(row < j, pltpu.roll(halo, j, 0), rolled[:SUB])
    return jnp.concatenate([first, rolled[SUB:]], axis=0) if n > SUB else first


def _shift_up(t, halo, j):
    if j == 0:
        return t
    n = t.shape[0]
    rolled = pltpu.roll(t, n - j, 0)
    row = lax.broadcasted_iota(jnp.int32, (SUB, t.shape[1]), 0)
    last = jnp.where(row >= SUB - j, pltpu.roll(halo, SUB - j, 0), rolled[n - SUB:])
    return jnp.concatenate([rolled[:n - SUB], last], axis=0) if n > SUB else last


def _mm(a, b, *, ta=False, tb=False, out_dtype=F32, name):
    if ta:
        K, M = a.shape
    else:
        M, K = a.shape
    if tb:
        N, K2 = b.shape
    else:
        K2, N = b.shape
    assert K == K2, (a.shape, b.shape, ta, tb)
    tm, tn, tk = _tile(M, 1024), _tile(N, 1408), _tile(K, 1024)
    nk = K // tk
    dn = (((0 if ta else 1,), (1 if tb else 0,)), ((), ()))

    def body(a_ref, b_ref, o_ref, acc_ref):
        k = pl.program_id(2)
        part = lax.dot_general(a_ref[...].astype(_MXU), b_ref[...].astype(_MXU), dn, preferred_element_type=F32)

        @pl.when(k == 0)
        def _():
            acc_ref[...] = part

        @pl.when(k > 0)
        def _():
            acc_ref[...] += part

        @pl.when(k == nk - 1)
        def _():
            o_ref[...] = acc_ref[...].astype(out_dtype)

    a_spec = pl.BlockSpec((tk, tm), lambda i, j, k: (k, i)) if ta else pl.BlockSpec((tm, tk), lambda i, j, k: (i, k))
    b_spec = pl.BlockSpec((tn, tk), lambda i, j, k: (j, k)) if tb else pl.BlockSpec((tk, tn), lambda i, j, k: (k, j))
    return pl.pallas_call(
        body, grid=(M // tm, N // tn, nk), in_specs=[a_spec, b_spec],
        out_specs=pl.BlockSpec((tm, tn), lambda i, j, k: (i, j)),
        out_shape=jax.ShapeDtypeStruct((M, N), out_dtype),
        scratch_shapes=[pltpu.VMEM((tm, tn), F32)], compiler_params=_cp(3), name=name)(a, b)


def _row(ts, w, col=0):
    return pl.BlockSpec((ts, w), lambda i: (i, col))


def _vec(w, r=1):
    return pl.BlockSpec((r, w), lambda i: (0, 0))


def _silu_call(x, name):
    def body(x_ref, o_ref):
        o_ref[...] = _silu(x_ref[...])
    return pl.pallas_call(body, out_shape=jax.ShapeDtypeStruct(x.shape, F32), name=name)(x)


def _norm_fwd(x, g, sc, sh, *, f=None, gate=None, name):
    S, dm = x.shape
    ts = _tile(S, TS_ROW, SUB)
    res = f is not None

    def body(*refs):
        if res:
            x_ref, f_ref, gate_ref, g_ref, sc_ref, sh_ref, xo_ref, h_ref = refs
            xv = x_ref[...] + gate_ref[...] * f_ref[...]
            xo_ref[...] = xv
        else:
            x_ref, g_ref, sc_ref, sh_ref, h_ref = refs
            xv = x_ref[...]
        rstd = lax.rsqrt(jnp.mean(xv * xv, axis=-1, keepdims=True) + EPS)
        h_ref[...] = ((xv * rstd) * g_ref[...] * (1.0 + sc_ref[...]) + sh_ref[...]).astype(_ACT)

    ins = [x] + ([f, gate] if res else []) + [g, sc, sh]
    in_specs = [_row(ts, dm)] + ([_row(ts, dm), _vec(dm)] if res else []) + [_vec(dm)] * 3
    h_shape = jax.ShapeDtypeStruct((S, dm), _ACT)
    if res:
        out_shape, out_specs = (jax.ShapeDtypeStruct((S, dm), F32), h_shape), (_row(ts, dm), _row(ts, dm))
    else:
        out_shape, out_specs = h_shape, _row(ts, dm)
    return pl.pallas_call(body, grid=(S // ts,), in_specs=in_specs, out_specs=out_specs, out_shape=out_shape,
                          compiler_params=_cp(1), name=name)(*ins)


def _norm_bwd(dh, x, dres, g, sc, *, name):
    S, dm = x.shape
    ts = _tile(S, TS_ROW, SUB)

    def body(dh_ref, x_ref, dres_ref, g_ref, sc_ref, dx_ref, dg_ref, dsc_ref, dsh_ref):
        i = pl.program_id(0)
        xv = x_ref[...]
        dhv = dh_ref[...]
        rstd = lax.rsqrt(jnp.mean(xv * xv, axis=-1, keepdims=True) + EPS)
        xhat = xv * rstd
        hn = xhat * g_ref[...]
        dhn = dhv * (1.0 + sc_ref[...])
        dxh = dhn * g_ref[...]
        dx_ref[...] = dres_ref[...] + rstd * (dxh - xhat * jnp.mean(dxh * xhat, axis=-1, keepdims=True))

        @pl.when(i == 0)
        def _():
            dg_ref[...] = jnp.zeros_like(dg_ref)
            dsc_ref[...] = jnp.zeros_like(dsc_ref)
            dsh_ref[...] = jnp.zeros_like(dsh_ref)

        dg_ref[...] += _colsum(dhn * xhat)
        dsc_ref[...] += _colsum(dhv * hn)
        dsh_ref[...] += _colsum(dhv)

    vshape = jax.ShapeDtypeStruct((1, dm), F32)
    return pl.pallas_call(
        body, grid=(S // ts,), in_specs=[_row(ts, dm)] * 3 + [_vec(dm)] * 2,
        out_specs=(_row(ts, dm), _vec(dm), _vec(dm), _vec(dm)),
        out_shape=(jax.ShapeDtypeStruct((S, dm), F32), vshape, vshape, vshape),
        compiler_params=_cp(1), name=name)(dh, x, dres, g, sc)


def _gate_bwd(dxo, f, gate, *, name):
    S, dm = f.shape
    ts = _tile(S, TS_ROW, SUB)

    def body(dxo_ref, f_ref, gate_ref, df_ref, dgate_ref):
        i = pl.program_id(0)
        dv = dxo_ref[...]
        df_ref[...] = (gate_ref[...] * dv).astype(_ACT)

        @pl.when(i == 0)
        def _():
            dgate_ref[...] = jnp.zeros_like(dgate_ref)

        dgate_ref[...] += _colsum(dv * f_ref[...])

    return pl.pallas_call(
        body, grid=(S // ts,), in_specs=[_row(ts, dm), _row(ts, dm), _vec(dm)],
        out_specs=(_row(ts, dm), _vec(dm)),
        out_shape=(jax.ShapeDtypeStruct((S, dm), _ACT), jax.ShapeDtypeStruct((1, dm), F32)),
        compiler_params=_cp(1), name=name)(dxo, f, gate)


def _final_loss(x, f, gate, g, tgt, *, name):
    S, dm = x.shape
    ts = _tile(S, TS_ROW, SUB)

    def body(x_ref, f_ref, gate_ref, g_ref, t_ref, loss_ref, dx_ref, dg_ref):
        i = pl.program_id(0)
        xv = x_ref[...] + gate_ref[...] * f_ref[...]
        rstd = lax.rsqrt(jnp.mean(xv * xv, axis=-1, keepdims=True) + EPS)
        xhat = xv * rstd
        err = xhat * g_ref[...] - t_ref[...]
        dy = err * (1.0 / dm)
        dxh = dy * g_ref[...]
        dx_ref[...] = rstd * (dxh - xhat * jnp.mean(dxh * xhat, axis=-1, keepdims=True))

        @pl.when(i == 0)
        def _():
            loss_ref[...] = jnp.zeros_like(loss_ref)
            dg_ref[...] = jnp.zeros_like(dg_ref)

        loss_ref[...] += jnp.full((1, LANE), 0.5 * jnp.sum(jnp.mean(err * err, axis=-1, keepdims=True)), F32)
        dg_ref[...] += _colsum(dy * xhat)

    return pl.pallas_call(
        body, grid=(S // ts,), in_specs=[_row(ts, dm), _row(ts, dm), _vec(dm), _vec(dm), _row(ts, dm)],
        out_specs=(_vec(LANE), _row(ts, dm), _vec(dm)),
        out_shape=(jax.ShapeDtypeStruct((1, LANE), F32), jax.ShapeDtypeStruct((S, dm), F32),
                   jax.ShapeDtypeStruct((1, dm), F32)),
        compiler_params=_cp(1), name=name)(x, f, gate, g, tgt)


def _ffn_conv(t, halo, cw_ref, cb_ref):
    return ((cb_ref[...] + _shift_down(t, halo, 2) * cw_ref[0:1, :]) + _shift_down(t, halo, 1) * cw_ref[1:2, :]) \
        + t * cw_ref[2:3, :]


def _prev_halo_spec(ts, w, col=0):
    return pl.BlockSpec((SUB, w), lambda i: (jnp.maximum(i * (ts // SUB) - 1, 0), col))


def _ffn_act_fwd(up, cw, cb, *, name):
    S, w2 = up.shape
    ff = w2 // 2
    ts = _tile(S, TS_FFN, SUB)

    def body(up_ref, halo_ref, cw_ref, cb_ref, act_ref):
        i = pl.program_id(0)
        t = up_ref[...]
        halo = jnp.where(i > 0, halo_ref[...], 0.0)
        u = _ffn_conv(t, halo, cw_ref, cb_ref)
        act_ref[...] = (_silu(u[:, :ff]) * u[:, ff:]).astype(_ACT)

    return pl.pallas_call(
        body, grid=(S // ts,), in_specs=[_row(ts, w2), _prev_halo_spec(ts, w2), _vec(w2, FFN_CONV), _vec(w2)],
        out_specs=_row(ts, ff), out_shape=jax.ShapeDtypeStruct((S, ff), _ACT),
        compiler_params=_cp(1), name=name)(up, up, cw, cb)


def _ffn_act_bwd(up, dact, cw, cb, *, name):
    S, w2 = up.shape
    ff = w2 // 2
    ts = _tile(S, TS_FFN, SUB)

    def body(up_ref, halo_ref, dact_ref, cw_ref, cb_ref, du_ref):
        i = pl.program_id(0)
        t = up_ref[...]
        halo = jnp.where(i > 0, halo_ref[...], 0.0)
        u = _ffn_conv(t, halo, cw_ref, cb_ref)
        a, b = u[:, :ff], u[:, ff:]
        da = dact_ref[...]
        du_ref[:, :ff] = da * b * _dsilu(a)
        du_ref[:, ff:] = da * _silu(a)

    return pl.pallas_call(
        body, grid=(S // ts,),
        in_specs=[_row(ts, w2), _prev_halo_spec(ts, w2), _row(ts, ff), _vec(w2, FFN_CONV), _vec(w2)],
        out_specs=_row(ts, w2), out_shape=jax.ShapeDtypeStruct((S, w2), F32),
        compiler_params=_cp(1), name=name)(up, up, dact, cw, cb)


def _ffn_conv_bwd(du, up, cw, *, name):
    S, w2 = up.shape
    ts = _tile(S, TS_FFN, SUB)
    n = S // ts

    def body(du_ref, nxt_ref, up_ref, halo_ref, cw_ref, dup_ref, dcw_ref, dcb_ref):
        i = pl.program_id(0)
        dv = du_ref[...]
        nxt = jnp.where(i < n - 1, nxt_ref[...], 0.0)
        t = up_ref[...]
        halo = jnp.where(i > 0, halo_ref[...], 0.0)
        dup = (dv * cw_ref[2:3, :] + _shift_up(dv, nxt, 1) * cw_ref[1:2, :]) + _shift_up(dv, nxt, 2) * cw_ref[0:1, :]
        dup_ref[...] = dup.astype(_ACT)

        @pl.when(i == 0)
        def _():
            dcw_ref[...] = jnp.zeros_like(dcw_ref)
            dcb_ref[...] = jnp.zeros_like(dcb_ref)

        dcb_ref[...] += _colsum(dv)
        dcw_ref[2:3, :] += _colsum(dv * t)
        dcw_ref[1:2, :] += _colsum(dv * _shift_down(t, halo, 1))
        dcw_ref[0:1, :] += _colsum(dv * _shift_down(t, halo, 2))

    nxt_spec = pl.BlockSpec((SUB, w2), lambda i: (jnp.minimum((i + 1) * (ts // SUB), S // SUB - 1), 0))
    return pl.pallas_call(
        body, grid=(n,),
        in_specs=[_row(ts, w2), nxt_spec, _row(ts, w2), _prev_halo_spec(ts, w2), _vec(w2, FFN_CONV)],
        out_specs=(_row(ts, w2), _vec(w2, FFN_CONV), _vec(w2)),
        out_shape=(jax.ShapeDtypeStruct((S, w2), _ACT), jax.ShapeDtypeStruct((FFN_CONV, w2), F32),
                   jax.ShapeDtypeStruct((1, w2), F32)),
        compiler_params=_cp(1), name=name)(du, du, up, up, cw)


def _ssd_core(pre, halo, misc, cw_ref, cb_ref, dtb, alog):
    q = pre.shape[0]
    conv = cb_ref[...]
    for k in range(SSD_CONV):
        conv = conv + _shift_down(pre, halo, SSD_CONV - 1 - k) * cw_ref[k:k + 1, :]
    xbc = _silu(conv)
    raw = misc + dtb
    dt = _softplus(raw)
    a = -jnp.exp(alog)
    r = lax.broadcasted_iota(jnp.int32, (q, q), 0)
    c = lax.broadcasted_iota(jnp.int32, (q, q), 1)
    tri = r >= c
    acum = jnp.dot(tri.astype(F32), dt * a, precision=_HI, preferred_element_type=F32)
    return conv, xbc, raw, dt, a, acum, acum.T, tri


def _sel(v, j, lo):
    return jnp.where(lo, v[:, 2 * j:2 * j + 1], v[:, 2 * j + 1:2 * j + 2])


def _ssd_pair_fwd(xbc, dt, acum, acum_t, tri, dsk, g_mat, b_mat, c_mat, h_pair, j, lo, lo1, sub_lo):
    q = xbc.shape[0]
    x = xbc[:, LANE * j:LANE * (j + 1)]
    dtp = _sel(dt, j, lo)
    ap = _sel(acum, j, lo)
    xd = x * dtp
    ls, ms = [], []
    for h in (2 * j, 2 * j + 1):
        seg = acum[:, h:h + 1] - acum_t[h:h + 1, :]
        l_mat = jnp.exp(jnp.where(tri, seg, -jnp.inf))
        ls.append(l_mat)
        ms.append(g_mat * l_mat)
    yd = jnp.where(lo, _dot(ms[0], xd), _dot(ms[1], xd))
    ea = jnp.exp(ap)
    yo = _dot_nt(c_mat, h_pair) * ea
    dp = _sel(dsk, j, lo1)
    alast = acum[q - 1:q, :]
    e = jnp.exp(_sel(alast, j, lo1) - ap)
    cd = jnp.where(sub_lo, jnp.exp(alast[:, 2 * j:2 * j + 1]), jnp.exp(alast[:, 2 * j + 1:2 * j + 2]))
    return dict(x=x, dtp=dtp, ap=ap, xd=xd, ls=ls, ms=ms, ea=ea, yo=yo, dp=dp, e=e, cd=cd, y=yd + yo + x * dp)


def _gnorm(yg):
    half = SSD_INNER // SSD_GROUPS
    rstds, yns = [], []
    for g in range(SSD_GROUPS):
        part = yg[:, half * g:half * (g + 1)]
        rstd = lax.rsqrt(jnp.mean(part * part, axis=-1, keepdims=True) + EPS)
        rstds.append(rstd)
        yns.append(part * rstd)
    return rstds, yns


def _ssd_specs(nc, rev):
    q = SSD_CHUNK
    cidx = (lambda i: nc - 1 - i) if rev else (lambda i: i)
    return [
        pl.BlockSpec((q, SSD_XBC), lambda i: (cidx(i), C_XBC // SSD_XBC)),
        pl.BlockSpec((SUB, SSD_XBC), lambda i: (jnp.maximum(cidx(i) * (q // SUB) - 1, 0), C_XBC // SSD_XBC)),
        pl.BlockSpec((q, SSD_INNER), lambda i: (cidx(i), C_Z // SSD_INNER)),
        pl.BlockSpec((q, LANE), lambda i: (cidx(i), C_MISC // LANE)),
    ]


def _ssd_param_specs():
    return [_vec(SSD_XBC, SSD_CONV), _vec(SSD_XBC), _vec(LANE), _vec(LANE), _vec(LANE), _vec(SSD_INNER)]


def _ssd_fwd(proj, cw, cb, dtb, alog, dsk, ng, *, name):
    S = proj.shape[0]
    q = SSD_CHUNK
    nc = S // q
    npair = SSD_HEADS // 2

    def body(xbc_ref, halo_ref, z_ref, misc_ref, cw_ref, cb_ref, dtb_ref, alog_ref, dsk_ref, ng_ref,
             y_ref, hin_ref, h_ref):
        c = pl.program_id(0)

        @pl.when(c == 0)
        def _():
            h_ref[...] = jnp.zeros_like(h_ref)

        pre = xbc_ref[...]
        halo = jnp.where(c > 0, halo_ref[...], 0.0)
        conv, xbc, raw, dt, a, acum, acum_t, tri = _ssd_core(pre, halo, misc_ref[...], cw_ref, cb_ref,
                                                             dtb_ref[...], alog_ref[...])
        lo = lax.broadcasted_iota(jnp.int32, (q, LANE), 1) < LANE // 2
        lo1 = lo[:1]
        sub_lo = lax.broadcasted_iota(jnp.int32, (LANE, LANE), 0) < LANE // 2
        ys = []
        for g in range(SSD_GROUPS):
            b_mat = xbc[:, SSD_INNER + SSD_STATE * g:SSD_INNER + SSD_STATE * (g + 1)]
            c_mat = xbc[:, SSD_INNER + SSD_STATE * (SSD_GROUPS + g):SSD_INNER + SSD_STATE * (SSD_GROUPS + g + 1)]
            g_mat = _dot_nt(c_mat, b_mat)
            for jj in range(npair // SSD_GROUPS):
                j = g * (npair // SSD_GROUPS) + jj
                hj = h_ref[j]
                p = _ssd_pair_fwd(xbc, dt, acum, acum_t, tri, dsk_ref[...], g_mat, b_mat, c_mat, hj, j, lo, lo1, sub_lo)
                ys.append(p["y"])
                hin_ref[0, j] = hj
                h_ref[j] = p["cd"] * hj + _dot_tn(p["xd"] * p["e"], b_mat)
        yg = jnp.concatenate(ys, axis=1) * _silu(z_ref[...])
        _, yns = _gnorm(yg)
        y_ref[...] = jnp.concatenate(yns, axis=1) * ng_ref[...]

    return pl.pallas_call(
        body, grid=(nc,), in_specs=_ssd_specs(nc, False) + _ssd_param_specs(),
        out_specs=(pl.BlockSpec((q, SSD_INNER), lambda i: (i, 0)),
                   pl.BlockSpec((1, npair, LANE, LANE), lambda i: (i, 0, 0, 0))),
        out_shape=(jax.ShapeDtypeStruct((S, SSD_INNER), F32), jax.ShapeDtypeStruct((nc, npair, LANE, LANE), F32)),
        scratch_shapes=[pltpu.VMEM((npair, LANE, LANE), F32)], compiler_params=_cp(1), name=name,
    )(proj, proj, proj, proj, cw, cb, dtb, alog, dsk, ng)


def _ssd_bwd(proj, dycat, hin, cw, cb, dtb, alog, dsk, ng, *, name):
    S = proj.shape[0]
    q = SSD_CHUNK
    nc = S // q
    npair = SSD_HEADS // 2
    ppg = npair // SSD_GROUPS

    def body(xbc_ref, halo_ref, z_ref, misc_ref, dy_ref, hin_ref, cw_ref, cb_ref, dtb_ref, alog_ref, dsk_ref, ng_ref,
             dpre_ref, dz_ref, dmisc_ref, dcw_ref, dcb_ref, ddtb_ref, dalog_ref, ddsk_ref, dng_ref,
             dh_ref, carry_ref):
        i = pl.program_id(0)
        c = nc - 1 - i

        @pl.when(i == 0)
        def _():
            dh_ref[...] = jnp.zeros_like(dh_ref)
            carry_ref[...] = jnp.zeros_like(carry_ref)
            for r in (dcw_ref, dcb_ref, ddtb_ref, dalog_ref, ddsk_ref, dng_ref):
                r[...] = jnp.zeros_like(r)

        pre = xbc_ref[...]
        halo = jnp.where(c > 0, halo_ref[...], 0.0)
        conv, xbc, raw, dt, a, acum, acum_t, tri = _ssd_core(pre, halo, misc_ref[...], cw_ref, cb_ref,
                                                             dtb_ref[...], alog_ref[...])
        lane = lax.broadcasted_iota(jnp.int32, (q, LANE), 1)
        lane1 = lane[:1]
        rowi = lax.broadcasted_iota(jnp.int32, (q, LANE), 0)
        lastrow = rowi == q - 1
        lo = lane < LANE // 2
        lo1 = lo[:1]
        sub_lo = lax.broadcasted_iota(jnp.int32, (LANE, LANE), 0) < LANE // 2
        dsk = dsk_ref[...]
        alast = acum[q - 1:q, :]

        def halves(t):
            return _rowsum(jnp.where(lo, t, 0.0)), _rowsum(jnp.where(lo, 0.0, t))

        def put(ha, va, vb):
            ln = lane if va.shape[0] == q else lane1
            return jnp.where(ln == ha, va, 0.0) + jnp.where(ln == ha + 1, vb, 0.0)

        mats, pairs = [], []
        for g in range(SSD_GROUPS):
            b_mat = xbc[:, SSD_INNER + SSD_STATE * g:SSD_INNER + SSD_STATE * (g + 1)]
            c_mat = xbc[:, SSD_INNER + SSD_STATE * (SSD_GROUPS + g):SSD_INNER + SSD_STATE * (SSD_GROUPS + g + 1)]
            g_mat = _dot_nt(c_mat, b_mat)
            mats.append((b_mat, c_mat, g_mat))
            for jj in range(ppg):
                j = g * ppg + jj
                pairs.append(_ssd_pair_fwd(xbc, dt, acum, acum_t, tri, dsk, g_mat, b_mat, c_mat, hin_ref[0, j],
                                           j, lo, lo1, sub_lo))
        z = z_ref[...]
        sz = _silu(z)
        y = jnp.concatenate([p["y"] for p in pairs], axis=1)
        rstds, yns = _gnorm(y * sz)
        dout = dy_ref[...]
        dng_ref[...] += _colsum(dout * jnp.concatenate(yns, axis=1))
        dyn = dout * ng_ref[...]
        half = SSD_INNER // SSD_GROUPS
        dygs = []
        for g in range(SSD_GROUPS):
            dyn_g = dyn[:, half * g:half * (g + 1)]
            dygs.append(rstds[g] * (dyn_g - yns[g] * jnp.mean(dyn_g * yns[g], axis=-1, keepdims=True)))
        dyg = jnp.concatenate(dygs, axis=1)
        dyv = dyg * sz
        dz_ref[...] = (dyg * y * _dsilu(z)).astype(_ACT)

        da_acc = jnp.zeros((q, LANE), F32)
        ddt = jnp.zeros((q, LANE), F32)
        dds = jnp.zeros((1, LANE), F32)
        dxs, dbs, dcs = [], [], []
        for g in range(SSD_GROUPS):
            b_mat, c_mat, g_mat = mats[g]
            dg_mat = jnp.zeros((q, q), F32)
            db = jnp.zeros((q, SSD_STATE), F32)
            dc = jnp.zeros((q, SSD_STATE), F32)
            for jj in range(ppg):
                j = g * ppg + jj
                ha = 2 * j
                p = pairs[j]
                hj = hin_ref[0, j]
                dyp = dyv[:, LANE * j:LANE * (j + 1)]
                dsum = _colsum(dyp * p["x"])
                dds = dds + put(ha, _rowsum(jnp.where(lo1, dsum, 0.0)), _rowsum(jnp.where(lo1, 0.0, dsum)))
                dx = dyp * p["dp"]
                dw = dyp * p["ea"]
                dc = dc + _dot(dw, hj)
                dh_yo = _dot_tn(dw, c_mat)
                ra, rb = halves(dyp * p["yo"])
                da_acc = da_acc + put(ha, ra, rb)
                dxd = jnp.zeros((q, LANE), F32)
                for idx in range(2):
                    dyh = jnp.where(lo, dyp, 0.0) if idx == 0 else jnp.where(lo, 0.0, dyp)
                    dm = _dot_nt(dyh, p["xd"])
                    dxd = dxd + _dot_tn(p["ms"][idx], dyh)
                    dg_mat = dg_mat + dm * p["ls"][idx]
                    t = dm * p["ms"][idx]
                    da_h = _rowsum(t) - _rowsum(t.T)
                    da_acc = da_acc + jnp.where(lane == ha + idx, da_h, 0.0)
                dhn = dh_ref[j]
                s = _rowsum(dhn * hj)
                sa = jnp.sum(jnp.where(sub_lo[:, :1], s, 0.0), keepdims=True)
                sb = jnp.sum(jnp.where(sub_lo[:, :1], 0.0, s), keepdims=True)
                cda, cdb = jnp.exp(alast[:, ha:ha + 1]), jnp.exp(alast[:, ha + 1:ha + 2])
                db = db + _dot(p["xd"] * p["e"], dhn)
                r = _dot_nt(b_mat, dhn)
                dxd = dxd + r * p["e"]
                qa, qb = halves(r * p["xd"] * p["e"])
                da_acc = da_acc - put(ha, qa, qb)
                tot_a = sa * cda + jnp.sum(qa, keepdims=True)
                tot_b = sb * cdb + jnp.sum(qb, keepdims=True)
                da_acc = da_acc + jnp.where(lastrow, put(ha, tot_a, tot_b), 0.0)
                dh_ref[j] = p["cd"] * dhn + dh_yo
                dx = dx + dxd * p["dtp"]
                ua, ub = halves(dxd * p["x"])
                ddt = ddt + put(ha, ua, ub)
                dxs.append(dx)
            dc = dc + _dot(dg_mat, b_mat)
            db = db + _dot_tn(dg_mat, c_mat)
            dbs.append(db)
            dcs.append(dc)
        r2 = lax.broadcasted_iota(jnp.int32, (q, q), 0)
        c2 = lax.broadcasted_iota(jnp.int32, (q, q), 1)
        dda = jnp.dot((c2 >= r2).astype(F32), da_acc, precision=_HI, preferred_element_type=F32)
        ddt = ddt + dda * a
        dalog_ref[...] += _colsum(dda * dt) * a
        ddsk_ref[...] += dds
        draw = jnp.where(lane < SSD_HEADS, ddt * _sigmoid(raw), 0.0)
        ddtb_ref[...] += _colsum(draw)
        dmisc_ref[...] = draw
        dconv = jnp.concatenate(dxs + dbs + dcs, axis=1) * _dsilu(conv)
        dcb_ref[...] += _colsum(dconv)
        nxt = carry_ref[...]
        dpre = jnp.zeros_like(dconv)
        for k in range(SSD_CONV):
            dcw_ref[k:k + 1, :] += _colsum(dconv * _shift_down(pre, halo, SSD_CONV - 1 - k))
            dpre = dpre + _shift_up(dconv, nxt, SSD_CONV - 1 - k) * cw_ref[k:k + 1, :]
        dpre_ref[...] = dpre.astype(_ACT)
        carry_ref[...] = dconv[:SUB]

    rev = lambda i: (nc - 1 - i, 0)
    vshape = lambda w, r=1: jax.ShapeDtypeStruct((r, w), F32)
    return pl.pallas_call(
        body, grid=(nc,),
        in_specs=_ssd_specs(nc, True) + [pl.BlockSpec((q, SSD_INNER), rev),
                                         pl.BlockSpec((1, npair, LANE, LANE), lambda i: (nc - 1 - i, 0, 0, 0))]
        + _ssd_param_specs(),
        out_specs=(pl.BlockSpec((q, SSD_XBC), rev), pl.BlockSpec((q, SSD_INNER), rev), pl.BlockSpec((q, LANE), rev),
                   _vec(SSD_XBC, SSD_CONV), _vec(SSD_XBC), _vec(LANE), _vec(LANE), _vec(LANE), _vec(SSD_INNER)),
        out_shape=(jax.ShapeDtypeStruct((S, SSD_XBC), _ACT), jax.ShapeDtypeStruct((S, SSD_INNER), _ACT),
                   jax.ShapeDtypeStruct((S, LANE), F32),
                   vshape(SSD_XBC, SSD_CONV), vshape(SSD_XBC), vshape(LANE), vshape(LANE), vshape(LANE),
                   vshape(SSD_INNER)),
        scratch_shapes=[pltpu.VMEM((npair, LANE, LANE), F32), pltpu.VMEM((SUB, SSD_XBC), F32)],
        compiler_params=_cp(1), name=name,
    )(proj, proj, proj, proj, dycat, hin, cw, cb, dtb, alog, dsk, ng)


def _rope(x, cosf, sina, sinb):
    return x * cosf + pltpu.roll(x, LANE - MLA_ROPE // 2, 1) * sina + pltpu.roll(x, MLA_ROPE // 2, 1) * sinb


def _rope_t(dy, cosf, sina, sinb):
    return dy * cosf + pltpu.roll(dy * sina, MLA_ROPE // 2, 1) + pltpu.roll(dy * sinb, LANE - MLA_ROPE // 2, 1)


def _rope_lanes(shape):
    lane = lax.broadcasted_iota(jnp.int32, shape, 1)
    return (lane >= ROPE_LANE) & (lane < ROPE_LANE + MLA_ROPE)


def _mla_prep_fwd(proj, cosf, sina, sinb, gq, gkv, wuq, wukv, *, name):
    S = proj.shape[0]
    ts = _tile(S, TS_ROW, SUB)
    hw = MLA_HEADS * LANE

    def body(cq_ref, ckv_ref, misc_ref, cos_ref, sa_ref, sb_ref, gq_ref, gkv_ref, wuq_ref, wukv_ref,
             q_ref, k_ref, v_ref):
        cosv, sav, sbv = cos_ref[...], sa_ref[...], sb_ref[...]
        cq = cq_ref[...]
        qn = cq * lax.rsqrt(jnp.mean(cq * cq, axis=-1, keepdims=True) + EPS) * gq_ref[...]
        qh = _dot(qn, wuq_ref[...])
        ckv = ckv_ref[...]
        kvn = ckv * lax.rsqrt(jnp.mean(ckv * ckv, axis=-1, keepdims=True) + EPS) * gkv_ref[...]
        kv = _dot(kvn, wukv_ref[...])
        kr = _rope(jnp.where(_rope_lanes((ts, LANE)), misc_ref[...], 0.0), cosv, sav, sbv)
        for h in range(MLA_HEADS):
            sl = slice(LANE * h, LANE * (h + 1))
            q_ref[:, sl] = _rope(qh[:, sl], cosv, sav, sbv).astype(_ACT)
            k_ref[:, sl] = (kv[:, sl] + kr).astype(_ACT)
        v_ref[...] = kv[:, hw:].astype(_ACT)

    oshape = jax.ShapeDtypeStruct((S, hw), _ACT)
    return pl.pallas_call(
        body, grid=(S // ts,),
        in_specs=[_row(ts, MLA_QR, C_CQ // MLA_QR), _row(ts, MLA_KVR, C_CKV // MLA_KVR), _row(ts, LANE, C_MISC // LANE),
                  _row(ts, LANE), _row(ts, LANE), _row(ts, LANE), _vec(MLA_QR), _vec(MLA_KVR),
                  _vec(hw, MLA_QR), _vec(2 * hw, MLA_KVR)],
        out_specs=(_row(ts, hw),) * 3, out_shape=(oshape,) * 3, compiler_params=_cp(1), name=name,
    )(proj, proj, proj, cosf, sina, sinb, gq, gkv, wuq, wukv)


def _mla_prep_bwd(proj, dq, dk, dv, dmisc_ssd, cosf, sina, sinb, gq, gkv, wuq, wukv, *, name):
    S = proj.shape[0]
    ts = _tile(S, TS_ROW, SUB)
    hw = MLA_HEADS * LANE

    def body(cq_ref, ckv_ref, dq_ref, dk_ref, dv_ref, dms_ref, cos_ref, sa_ref, sb_ref, gq_ref, gkv_ref,
             wuq_ref, wukv_ref, dcq_ref, dckv_ref, dmisc_ref, dqh_ref, dkv_ref, qn_ref, kvn_ref, dgq_ref, dgkv_ref):
        i = pl.program_id(0)
        cosv, sav, sbv = cos_ref[...], sa_ref[...], sb_ref[...]

        @pl.when(i == 0)
        def _():
            dgq_ref[...] = jnp.zeros_like(dgq_ref)
            dgkv_ref[...] = jnp.zeros_like(dgkv_ref)

        dqh = jnp.concatenate([_rope_t(dq_ref[:, LANE * h:LANE * (h + 1)], cosv, sav, sbv)
                               for h in range(MLA_HEADS)], axis=1)
        dqh_ref[...] = dqh.astype(_ACT)
        dkv = jnp.concatenate([dk_ref[...], dv_ref[...]], axis=1)
        dkv_ref[...] = dkv.astype(_ACT)

        def norm_bwd(x, g, dn, dg_ref, n_ref):
            rstd = lax.rsqrt(jnp.mean(x * x, axis=-1, keepdims=True) + EPS)
            xhat = x * rstd
            n_ref[...] = (xhat * g).astype(_ACT)
            dg_ref[...] += _colsum(dn * xhat)
            dxh = dn * g
            return rstd * (dxh - xhat * jnp.mean(dxh * xhat, axis=-1, keepdims=True))

        dcq_ref[...] = norm_bwd(cq_ref[...], gq_ref[...], _dot_nt(dqh, wuq_ref[...]), dgq_ref, qn_ref).astype(_ACT)
        dckv_ref[...] = norm_bwd(ckv_ref[...], gkv_ref[...], _dot_nt(dkv, wukv_ref[...]), dgkv_ref, kvn_ref).astype(_ACT)
        dks = dk_ref[:, 0:LANE]
        for h in range(1, MLA_HEADS):
            dks = dks + dk_ref[:, LANE * h:LANE * (h + 1)]
        rl = _rope_lanes((ts, LANE))
        dkr = _rope_t(jnp.where(rl, dks, 0.0), cosv, sav, sbv)
        dmisc_ref[...] = (dms_ref[...] + jnp.where(rl, dkr, 0.0)).astype(_ACT)

    act = lambda w: jax.ShapeDtypeStruct((S, w), _ACT)
    return pl.pallas_call(
        body, grid=(S // ts,),
        in_specs=[_row(ts, MLA_QR, C_CQ // MLA_QR), _row(ts, MLA_KVR, C_CKV // MLA_KVR),
                  _row(ts, hw), _row(ts, hw), _row(ts, hw), _row(ts, LANE),
                  _row(ts, LANE), _row(ts, LANE), _row(ts, LANE), _vec(MLA_QR), _vec(MLA_KVR),
                  _vec(hw, MLA_QR), _vec(2 * hw, MLA_KVR)],
        out_specs=(_row(ts, MLA_QR), _row(ts, MLA_KVR), _row(ts, LANE), _row(ts, hw), _row(ts, 2 * hw),
                   _row(ts, MLA_QR), _row(ts, MLA_KVR), _vec(MLA_QR), _vec(MLA_KVR)),
        out_shape=(act(MLA_QR), act(MLA_KVR), act(LANE), act(hw), act(2 * hw), act(MLA_QR), act(MLA_KVR),
                   jax.ShapeDtypeStruct((1, MLA_QR), F32), jax.ShapeDtypeStruct((1, MLA_KVR), F32)),
        compiler_params=_cp(1), name=name,
    )(proj, proj, dq, dk, dv, dmisc_ssd, cosf, sina, sinb, gq, gkv, wuq, wukv)


_MLA_SCALE = 1.0 / math.sqrt(MLA_NOPE + MLA_ROPE)


def _causal_scores(q, k, i, j, tq):
    s = _dot_nt(q, k) * _MLA_SCALE
    rows = i * tq + lax.broadcasted_iota(jnp.int32, (tq, tq), 0)
    cols = j * tq + lax.broadcasted_iota(jnp.int32, (tq, tq), 1)
    return jnp.where(cols <= rows, s, -jnp.inf)


def _attn_fwd(q, k, v, *, name):
    S = q.shape[0]
    tq = _tile(S, TQ_ATT)
    nq = S // tq

    def body(q_ref, k_ref, v_ref, o_ref, lse_ref, m_ref, l_ref, acc_ref):
        i, j = pl.program_id(1), pl.program_id(2)

        @pl.when(j == 0)
        def _():
            m_ref[...] = jnp.full_like(m_ref, -jnp.inf)
            l_ref[...] = jnp.zeros_like(l_ref)
            acc_ref[...] = jnp.zeros_like(acc_ref)

        @pl.when(j <= i)
        def _():
            s = _causal_scores(q_ref[...], k_ref[...], i, j, tq)
            m_prev = m_ref[...]
            m_new = jnp.maximum(m_prev, jnp.max(s, axis=1, keepdims=True))
            p = jnp.exp(s - m_new)
            alpha = jnp.exp(m_prev - m_new)
            l_ref[...] = alpha * l_ref[...] + _rowsum(p)
            acc_ref[...] = alpha * acc_ref[...] + _dot(p, v_ref[...])
            m_ref[...] = m_new

        @pl.when(j == nq - 1)
        def _():
            o_ref[...] = acc_ref[...] / l_ref[...]
            lse_ref[...] = jnp.broadcast_to(m_ref[...] + jnp.log(l_ref[...]), (tq, LANE))

    qspec = pl.BlockSpec((tq, LANE), lambda h, i, j: (i, h))
    kspec = pl.BlockSpec((tq, LANE), lambda h, i, j: (jnp.minimum(j, i), h))
    oshape = jax.ShapeDtypeStruct((S, MLA_HEADS * LANE), F32)
    return pl.pallas_call(
        body, grid=(MLA_HEADS, nq, nq), in_specs=[qspec, kspec, kspec], out_specs=(qspec, qspec),
        out_shape=(oshape, oshape),
        scratch_shapes=[pltpu.VMEM((tq, 1), F32), pltpu.VMEM((tq, 1), F32), pltpu.VMEM((tq, LANE), F32)],
        compiler_params=_cp(3), name=name)(q, k, v)


def _attn_bwd_dq(q, k, v, o, lse, dycat, *, name):
    S = q.shape[0]
    tq = _tile(S, TQ_ATT)
    nq = S // tq

    def body(q_ref, k_ref, v_ref, o_ref, lse_ref, do_ref, dq_ref, acc_ref):
        i, j = pl.program_id(1), pl.program_id(2)

        @pl.when(j == 0)
        def _():
            acc_ref[...] = jnp.zeros_like(acc_ref)

        @pl.when(j <= i)
        def _():
            kv = k_ref[...]
            p = jnp.exp(_causal_scores(q_ref[...], kv, i, j, tq) - lse_ref[:, 0:1])
            dov = do_ref[...]
            delta = _rowsum(dov * o_ref[...])
            ds = p * (_dot_nt(dov, v_ref[...]) - delta) * _MLA_SCALE
            acc_ref[...] += _dot(ds, kv)

        @pl.when(j == nq - 1)
        def _():
            dq_ref[...] = acc_ref[...]

    qspec = pl.BlockSpec((tq, LANE), lambda h, i, j: (i, h))
    kspec = pl.BlockSpec((tq, LANE), lambda h, i, j: (jnp.minimum(j, i), h))
    dospec = pl.BlockSpec((tq, LANE), lambda h, i, j: (i, SSD_INNER // LANE + h))
    return pl.pallas_call(
        body, grid=(MLA_HEADS, nq, nq), in_specs=[qspec, kspec, kspec, qspec, qspec, dospec], out_specs=qspec,
        out_shape=jax.ShapeDtypeStruct((S, MLA_HEADS * LANE), F32),
        scratch_shapes=[pltpu.VMEM((tq, LANE), F32)], compiler_params=_cp(3), name=name)(q, k, v, o, lse, dycat)


def _attn_bwd_dkv(q, k, v, o, lse, dycat, *, name):
    S = q.shape[0]
    tq = _tile(S, TQ_ATT)
    nq = S // tq

    def body(q_ref, k_ref, v_ref, o_ref, lse_ref, do_ref, dk_ref, dv_ref, dk_acc, dv_acc):
        j, i = pl.program_id(1), pl.program_id(2)

        @pl.when(i == 0)
        def _():
            dk_acc[...] = jnp.zeros_like(dk_acc)
            dv_acc[...] = jnp.zeros_like(dv_acc)

        @pl.when(i >= j)
        def _():
            qv = q_ref[...]
            p = jnp.exp(_causal_scores(qv, k_ref[...], i, j, tq) - lse_ref[:, 0:1])
            dov = do_ref[...]
            delta = _rowsum(dov * o_ref[...])
            dv_acc[...] += _dot_tn(p, dov)
            ds = p * (_dot_nt(dov, v_ref[...]) - delta) * _MLA_SCALE
            dk_acc[...] += _dot_tn(ds, qv)

        @pl.when(i == nq - 1)
        def _():
            dk_ref[...] = dk_acc[...]
            dv_ref[...] = dv_acc[...]

    qspec = pl.BlockSpec((tq, LANE), lambda h, j, i: (jnp.maximum(i, j), h))
    kspec = pl.BlockSpec((tq, LANE), lambda h, j, i: (j, h))
    dospec = pl.BlockSpec((tq, LANE), lambda h, j, i: (jnp.maximum(i, j), SSD_INNER // LANE + h))
    oshape = jax.ShapeDtypeStruct((S, MLA_HEADS * LANE), F32)
    return pl.pallas_call(
        body, grid=(MLA_HEADS, nq, nq), in_specs=[qspec, kspec, kspec, qspec, qspec, dospec],
        out_specs=(kspec, kspec), out_shape=(oshape, oshape),
        scratch_shapes=[pltpu.VMEM((tq, LANE), F32), pltpu.VMEM((tq, LANE), F32)],
        compiler_params=_cp(3), name=name)(q, k, v, o, lse, dycat)


_SWA_SCALE = 1.0 / math.sqrt(SWA_HD)
_SWA_KW = SWA_KV * LANE


def _swa_specs(S, ts, rev):
    n = S // ts
    t = (lambda i: n - 1 - i) if rev else (lambda i: i)
    hb = lambda i: jnp.maximum(t(i) * (ts // WINDOW) - 1, 0)
    return [
        pl.BlockSpec((ts, SWA_HEADS * LANE), lambda i: (t(i), C_SQ // (SWA_HEADS * LANE))),
        pl.BlockSpec((ts, _SWA_KW), lambda i: (t(i), C_SK // _SWA_KW)),
        pl.BlockSpec((WINDOW, _SWA_KW), lambda i: (hb(i), C_SK // _SWA_KW)),
        pl.BlockSpec((ts, _SWA_KW), lambda i: (t(i), C_SV // _SWA_KW)),
        pl.BlockSpec((WINDOW, _SWA_KW), lambda i: (hb(i), C_SV // _SWA_KW)),
    ]


def _swa_scores(qh, kk, t, b, ts):
    s = _dot_nt(qh, kk) * _SWA_SCALE
    row = lax.broadcasted_iota(jnp.int32, (WINDOW, 2 * WINDOW), 0)
    col = lax.broadcasted_iota(jnp.int32, (WINDOW, 2 * WINDOW), 1)
    rel = WINDOW + row - col
    kpos = t * ts + (b - 1) * WINDOW + col
    return jnp.where((rel >= 0) & (rel < WINDOW) & (kpos >= 0), s, -jnp.inf)


def _swa_fwd(proj, sinks, *, name):
    S = proj.shape[0]
    ts = _tile(S, TS_SWA)
    nb = ts // WINDOW

    def body(q_ref, k_ref, kh_ref, v_ref, vh_ref, sink_ref, o_ref, lse_ref):
        t = pl.program_id(0)
        kext = jnp.concatenate([kh_ref[...], k_ref[...]], axis=0)
        vext = jnp.concatenate([vh_ref[...], v_ref[...]], axis=0)
        for b in range(nb):
            rows = slice(WINDOW * b, WINDOW * (b + 1))
            for h in range(SWA_HEADS):
                kvl = slice(LANE * (h // (SWA_HEADS // SWA_KV)), LANE * (h // (SWA_HEADS // SWA_KV) + 1))
                hl = slice(LANE * h, LANE * (h + 1))
                kk = kext[WINDOW * b:WINDOW * (b + 2), kvl]
                vv = vext[WINDOW * b:WINDOW * (b + 2), kvl]
                s = _swa_scores(q_ref[rows, hl], kk, t, b, ts)
                sk = sink_ref[:, h:h + 1]
                m = jnp.maximum(jnp.max(s, axis=1, keepdims=True), sk)
                p = jnp.exp(s - m)
                den = _rowsum(p) + jnp.exp(sk - m)
                o_ref[rows, hl] = _dot(p, vv) / den
                lse_ref[rows, hl] = jnp.broadcast_to(m + jnp.log(den), (WINDOW, LANE))

    oshape = jax.ShapeDtypeStruct((S, SWA_HEADS * LANE), F32)
    ospec = pl.BlockSpec((ts, SWA_HEADS * LANE), lambda i: (i, 0))
    return pl.pallas_call(
        body, grid=(S // ts,), in_specs=_swa_specs(S, ts, False) + [_vec(LANE)], out_specs=(ospec, ospec),
        out_shape=(oshape, oshape), compiler_params=_cp(1), name=name)(proj, proj, proj, proj, proj, sinks)


def _swa_bwd(proj, o, lse, dycat, sinks, *, name):
    S = proj.shape[0]
    ts = _tile(S, TS_SWA)
    nb = ts // WINDOW
    n = S // ts
    grp = SWA_HEADS // SWA_KV

    def body(q_ref, k_ref, kh_ref, v_ref, vh_ref, o_ref, lse_ref, do_ref, sink_ref,
             dq_ref, dk_ref, dv_ref, dsink_ref, dk_carry, dv_carry):
        i = pl.program_id(0)
        t = n - 1 - i

        @pl.when(i == 0)
        def _():
            dk_carry[...] = jnp.zeros_like(dk_carry)
            dv_carry[...] = jnp.zeros_like(dv_carry)
            dsink_ref[...] = jnp.zeros_like(dsink_ref)

        kext = jnp.concatenate([kh_ref[...], k_ref[...]], axis=0)
        vext = jnp.concatenate([vh_ref[...], v_ref[...]], axis=0)
        lane1 = lax.broadcasted_iota(jnp.int32, (1, LANE), 1)
        dkb = [[jnp.zeros((WINDOW, LANE), F32) for _ in range(SWA_KV)] for _ in range(nb + 1)]
        dvb = [[jnp.zeros((WINDOW, LANE), F32) for _ in range(SWA_KV)] for _ in range(nb + 1)]
        dsink = jnp.zeros((1, LANE), F32)
        for b in range(nb):
            rows = slice(WINDOW * b, WINDOW * (b + 1))
            for h in range(SWA_HEADS):
                kvh = h // grp
                kvl = slice(LANE * kvh, LANE * (kvh + 1))
                hl = slice(LANE * h, LANE * (h + 1))
                kk = kext[WINDOW * b:WINDOW * (b + 2), kvl]
                vv = vext[WINDOW * b:WINDOW * (b + 2), kvl]
                qh = q_ref[rows, hl]
                lse_h = lse_ref[rows, LANE * h:LANE * h + 1]
                p = jnp.exp(_swa_scores(qh, kk, t, b, ts) - lse_h)
                doh = do_ref[rows, hl]
                delta = _rowsum(doh * o_ref[rows, hl])
                ds = p * (_dot_nt(doh, vv) - delta)
                sk = sink_ref[:, h:h + 1]
                dsink = dsink + jnp.where(lane1 == h, -jnp.sum(jnp.exp(sk - lse_h) * delta, keepdims=True), 0.0)
                dq_ref[rows, hl] = (_dot(ds, kk) * _SWA_SCALE).astype(_ACT)
                dkk = _dot_tn(ds, qh) * _SWA_SCALE
                dvv = _dot_tn(p, doh)
                dkb[b][kvh] = dkb[b][kvh] + dkk[:WINDOW]
                dkb[b + 1][kvh] = dkb[b + 1][kvh] + dkk[WINDOW:]
                dvb[b][kvh] = dvb[b][kvh] + dvv[:WINDOW]
                dvb[b + 1][kvh] = dvb[b + 1][kvh] + dvv[WINDOW:]
        dsink_ref[...] += dsink
        for dref, blocks, carry in ((dk_ref, dkb, dk_carry), (dv_ref, dvb, dv_carry)):
            old = carry[...]
            for b in range(1, nb + 1):
                blk = jnp.concatenate(blocks[b], axis=1)
                if b == nb:
                    blk = blk + old
                dref[WINDOW * (b - 1):WINDOW * b, :] = blk.astype(_ACT)
            carry[...] = jnp.concatenate(blocks[0], axis=1)

    hw = SWA_HEADS * LANE
    rev = lambda i: (n - 1 - i, 0)
    mix = lambda i: (n - 1 - i, (SSD_INNER + MLA_HEADS * LANE) // hw)
    return pl.pallas_call(
        body, grid=(n,),
        in_specs=_swa_specs(S, ts, True) + [pl.BlockSpec((ts, hw), rev), pl.BlockSpec((ts, hw), rev),
                                            pl.BlockSpec((ts, hw), mix), _vec(LANE)],
        out_specs=(pl.BlockSpec((ts, hw), rev), pl.BlockSpec((ts, _SWA_KW), rev), pl.BlockSpec((ts, _SWA_KW), rev),
                   _vec(LANE)),
        out_shape=(jax.ShapeDtypeStruct((S, hw), _ACT), jax.ShapeDtypeStruct((S, _SWA_KW), _ACT),
                   jax.ShapeDtypeStruct((S, _SWA_KW), _ACT), jax.ShapeDtypeStruct((1, LANE), F32)),
        scratch_shapes=[pltpu.VMEM((WINDOW, _SWA_KW), F32), pltpu.VMEM((WINDOW, _SWA_KW), F32)],
        compiler_params=_cp(1), name=name)(proj, proj, proj, proj, proj, o, lse, dycat, sinks)


def _exchange(arrays, *, scatter, name):
    n = len(arrays)

    def body(*refs):
        ins, outs = refs[:n], refs[n:2 * n]
        send_sems, recv_sems, loc_sems = refs[2 * n:]
        x, y, c = lax.axis_index("x"), lax.axis_index("y"), lax.axis_index("c")
        me = 4 * x + 2 * y + c

        def src(i, dest):
            return ins[i].at[dest] if scatter else ins[i]

        local = [pltpu.make_async_copy(src(i, me), outs[i].at[me], loc_sems.at[i]) for i in range(n)]
        for cp in local:
            cp.start()
        sends, recvs = [], []
        for k in range(1, NDEV):
            px = 1 - x if k & 4 else x
            py = 1 - y if k & 2 else y
            pc = 1 - c if k & 1 else c
            peer = 4 * px + 2 * py + pc
            for i in range(n):
                common = dict(send_sem=send_sems.at[i, k - 1], recv_sem=recv_sems.at[i, k - 1],
                              device_id=(px, py, pc), device_id_type=pl.DeviceIdType.MESH)
                sends.append(pltpu.make_async_remote_copy(src_ref=src(i, peer), dst_ref=outs[i].at[me], **common))
                recvs.append(pltpu.make_async_remote_copy(src_ref=src(i, peer), dst_ref=outs[i].at[peer], **common))
        for cp in sends:
            cp.start()
        for cp in recvs:
            cp.wait_recv()
        for cp in sends:
            cp.wait_send()
        for cp in local:
            cp.wait()

    hbm = pl.BlockSpec(memory_space=pl.ANY)
    out_shape = tuple(jax.ShapeDtypeStruct(a.shape if scatter else (NDEV,) + a.shape, a.dtype) for a in arrays)
    return pl.pallas_call(
        body, in_specs=[hbm] * n, out_specs=tuple([hbm] * n), out_shape=out_shape,
        scratch_shapes=[pltpu.SemaphoreType.DMA((n, NDEV - 1)), pltpu.SemaphoreType.DMA((n, NDEV - 1)),
                        pltpu.SemaphoreType.DMA((n,))],
        name=name)(*arrays)


def _adamw(w, m, v, parts, *, name):
    R, C = w.shape
    npart = parts.shape[0]
    cap = max(SUB, ((1 << 18) // C) // SUB * SUB)
    tr = _tile(R, cap, SUB)

    def body(w_ref, m_ref, v_ref, p_ref, g_ref, d_ref, mo_ref, vo_ref):
        g = p_ref[0]
        for k in range(1, npart):
            g = g + p_ref[k]
        mn = ADAM_B1 * m_ref[...] + (1.0 - ADAM_B1) * g
        vn = ADAM_B2 * v_ref[...] + (1.0 - ADAM_B2) * (g * g)
        m_hat = mn / (1.0 - ADAM_B1 ** ADAM_STEP)
        v_hat = vn / (1.0 - ADAM_B2 ** ADAM_STEP)
        g_ref[...] = g
        d_ref[...] = -ADAM_LR * (m_hat / (jnp.sqrt(v_hat) + ADAM_EPS) + ADAM_WD * w_ref[...])
        mo_ref[...] = mn
        vo_ref[...] = vn

    spec = pl.BlockSpec((tr, C), lambda i: (i, 0))
    oshape = jax.ShapeDtypeStruct((R, C), F32)
    return pl.pallas_call(
        body, grid=(R // tr,), in_specs=[spec] * 3 + [pl.BlockSpec((npart, tr, C), lambda i: (0, i, 0))],
        out_specs=(spec,) * 4, out_shape=(oshape,) * 4, compiler_params=_cp(1), name=name)(w, m, v, parts)


def _pad_heads(w, nh, hd, axis=-1):
    axis = axis % w.ndim
    shp = w.shape
    w = w.reshape(shp[:axis] + (nh, hd) + shp[axis + 1:])
    pads = [(0, 0)] * w.ndim
    pads[axis + 1] = (0, LANE - hd)
    return jnp.pad(w, pads).reshape(shp[:axis] + (nh * LANE,) + shp[axis + 1:])


def _unpad_heads(w, nh, hd, axis=-1):
    axis = axis % w.ndim
    shp = w.shape
    w = w.reshape(shp[:axis] + (nh, LANE) + shp[axis + 1:])
    w = lax.slice_in_dim(w, 0, hd, axis=axis + 1)
    return w.reshape(shp[:axis] + (nh * hd,) + shp[axis + 1:])


_O_DT = SSD_INNER + SSD_XBC
_O_CQ = _O_DT + SSD_HEADS
_O_CKV = _O_CQ + MLA_QR
_O_KR = _O_CKV + MLA_KVR
_O_SQ = _O_KR + MLA_ROPE
_O_SK = _O_SQ + SWA_HEADS * SWA_HD
_O_SV = _O_SK + SWA_KV * SWA_HD


def _w_in_to_padded(w):
    z, xbc, dt = w[..., :SSD_INNER], w[..., SSD_INNER:_O_DT], w[..., _O_DT:_O_CQ]
    cq, ckv, kr = w[..., _O_CQ:_O_CKV], w[..., _O_CKV:_O_KR], w[..., _O_KR:_O_SQ]
    sq, sk, sv = w[..., _O_SQ:_O_SK], w[..., _O_SK:_O_SV], w[..., _O_SV:]
    zeros = lambda n: jnp.zeros(w.shape[:-1] + (n,), w.dtype)
    misc = jnp.concatenate([dt, zeros(ROPE_LANE - SSD_HEADS), kr, zeros(LANE - ROPE_LANE - MLA_ROPE)], axis=-1)
    return jnp.concatenate([xbc, z, cq, ckv, misc, _pad_heads(sq, SWA_HEADS, SWA_HD),
                            _pad_heads(sk, SWA_KV, SWA_HD), _pad_heads(sv, SWA_KV, SWA_HD)], axis=-1)


def _w_in_from_padded(g):
    xbc, z, cq, ckv = g[..., C_XBC:C_Z], g[..., C_Z:C_CQ], g[..., C_CQ:C_CKV], g[..., C_CKV:C_MISC]
    dt, kr = g[..., C_MISC:C_MISC + SSD_HEADS], g[..., C_MISC + ROPE_LANE:C_MISC + ROPE_LANE + MLA_ROPE]
    sq = _unpad_heads(g[..., C_SQ:C_SK], SWA_HEADS, SWA_HD)
    sk = _unpad_heads(g[..., C_SK:C_SV], SWA_KV, SWA_HD)
    sv = _unpad_heads(g[..., C_SV:], SWA_KV, SWA_HD)
    return jnp.concatenate([z, xbc, dt, cq, ckv, kr, sq, sk, sv], axis=-1)


def _w_out_to_padded(w):
    a = SSD_INNER
    b = a + MLA_HEADS * MLA_V
    return jnp.concatenate([w[..., :a, :], _pad_heads(w[..., a:b, :], MLA_HEADS, MLA_V, axis=-2),
                            _pad_heads(w[..., b:, :], SWA_HEADS, SWA_HD, axis=-2)], axis=-2)


def _w_out_from_padded(g):
    a = SSD_INNER
    b = a + MLA_HEADS * LANE
    return jnp.concatenate([g[..., :a, :], _unpad_heads(g[..., a:b, :], MLA_HEADS, MLA_V, axis=-2),
                            _unpad_heads(g[..., b:, :], SWA_HEADS, SWA_HD, axis=-2)], axis=-2)


def _w_ukv_to_padded(w):
    w4 = w.reshape(w.shape[:-1] + (MLA_HEADS, MLA_NOPE + MLA_V))
    flat = lambda t: t.reshape(w.shape[:-1] + (MLA_HEADS * t.shape[-1],))
    return jnp.concatenate([_pad_heads(flat(w4[..., :MLA_NOPE]), MLA_HEADS, MLA_NOPE),
                            _pad_heads(flat(w4[..., MLA_NOPE:]), MLA_HEADS, MLA_V)], axis=-1)


def _w_ukv_from_padded(g):
    hw = MLA_HEADS * LANE
    gk = _unpad_heads(g[..., :hw], MLA_HEADS, MLA_NOPE).reshape(g.shape[:-1] + (MLA_HEADS, MLA_NOPE))
    gv = _unpad_heads(g[..., hw:], MLA_HEADS, MLA_V).reshape(g.shape[:-1] + (MLA_HEADS, MLA_V))
    return jnp.concatenate([gk, gv], axis=-1).reshape(g.shape[:-1] + (MLA_HEADS * (MLA_NOPE + MLA_V),))


def _pad_lane(v):
    return jnp.pad(v, [(0, 0)] * (v.ndim - 1) + [(0, LANE - v.shape[-1])])


def _rope_tables(positions):
    inv_freq = ROPE_THETA ** (-jnp.arange(0, MLA_ROPE, 2, dtype=F32) / MLA_ROPE)
    ang = positions.astype(F32).reshape(-1, 1) * inv_freq
    cos, sin = jnp.cos(ang), jnp.sin(ang)
    S = ang.shape[0]
    one, zero = jnp.ones((S, ROPE_LANE), F32), jnp.zeros((S, ROPE_LANE), F32)
    tail1, tail0 = jnp.ones((S, LANE - ROPE_LANE - MLA_ROPE), F32), jnp.zeros((S, LANE - ROPE_LANE - MLA_ROPE), F32)
    z16 = jnp.zeros_like(sin)
    return (jnp.concatenate([one, cos, cos, tail1], axis=1), jnp.concatenate([zero, -sin, z16, tail0], axis=1),
            jnp.concatenate([zero, z16, sin, tail0], axis=1))


def _layer_fwd(l, x_in, f_prev, gate_prev, mod, P, tabs):
    sh1, sc1, g1, sh2, sc2, g2 = [mod[k:k + 1] for k in range(6)]
    tag = f"l{l}_"
    if f_prev is None:
        x0 = x_in
        h1 = _norm_fwd(x0, P["n1g"][l], sc1, sh1, name=tag + "norm1")
    else:
        x0, h1 = _norm_fwd(x_in, P["n1g"][l], sc1, sh1, f=f_prev, gate=gate_prev, name=tag + "norm1")
    proj = _mm(h1, P["w_in"][l], name=tag + "proj")
    y_ssd, hin = _ssd_fwd(proj, P["ssd_cw"][l], P["ssd_cb"][l], P["dtb"][l], P["alog"][l], P["dsk"][l],
                          P["ssd_ng"][l], name=tag + "ssd")
    q, k, v = _mla_prep_fwd(proj, *tabs, P["gq"][l], P["gkv"][l], P["w_uq"][l], P["w_ukv"][l], name=tag + "mla_prep")
    o_mla, lse_mla = _attn_fwd(q, k, v, name=tag + "mla_attn")
    o_swa, lse_swa = _swa_fwd(proj, P["sinks"][l], name=tag + "swa")
    ycat = jnp.concatenate([y_ssd.astype(_ACT), o_mla.astype(_ACT), o_swa.astype(_ACT)], axis=1)
    y = _mm(ycat, P["w_out"][l], name=tag + "out")
    x1, h2 = _norm_fwd(x0, P["n2g"][l], sc2, sh2, f=y, gate=g1, name=tag + "norm2")
    up = _mm(h2, P["w_up"][l], name=tag + "up")
    act = _ffn_act_fwd(up, P["fcw"][l], P["fcb"][l], name=tag + "ffn_act")
    f = _mm(act, P["w_down"][l], name=tag + "down")
    saved = dict(x0=x0, h1=h1, proj=proj, hin=hin, q=q, k=k, v=v, o_mla=o_mla, lse_mla=lse_mla, o_swa=o_swa,
                 lse_swa=lse_swa, ycat=ycat, y=y, x1=x1, h2=h2, up=up, act=act, f=f, mod=mod)
    return x1, f, g2, saved


def _layer_bwd(l, dxo, sv, P, tabs):
    mod = sv["mod"]
    sh1, sc1, g1, sh2, sc2, g2 = [mod[k:k + 1] for k in range(6)]
    tag = f"l{l}_b_"
    G = {}
    df, dg2 = _gate_bwd(dxo, sv["f"], g2, name=tag + "gate2")
    dact = _mm(df, P["w_down"][l], tb=True, name=tag + "dact")
    G["w_down"] = _mm(sv["act"], df, ta=True, name=tag + "dw_down")
    du = _ffn_act_bwd(sv["up"], dact, P["fcw"][l], P["fcb"][l], name=tag + "ffn_act")
    dup, G["fcw"], G["fcb"] = _ffn_conv_bwd(du, sv["up"], P["fcw"][l], name=tag + "ffn_conv")
    dh2 = _mm(dup, P["w_up"][l], tb=True, name=tag + "dh2")
    G["w_up"] = _mm(sv["h2"], dup, ta=True, name=tag + "dw_up")
    dx1, G["n2g"], dsc2, dsh2 = _norm_bwd(dh2, sv["x1"], dxo, P["n2g"][l], sc2, name=tag + "norm2")
    dy, dg1 = _gate_bwd(dx1, sv["y"], g1, name=tag + "gate1")
    dycat = _mm(dy, P["w_out"][l], tb=True, name=tag + "dycat")
    G["w_out"] = _mm(sv["ycat"], dy, ta=True, name=tag + "dw_out")
    proj = sv["proj"]
    (dpre, dz, dmisc_ssd, G["ssd_cw"], G["ssd_cb"], G["dtb"], G["alog"], G["dsk"], G["ssd_ng"]) = _ssd_bwd(
        proj, dycat, sv["hin"], P["ssd_cw"][l], P["ssd_cb"][l], P["dtb"][l], P["alog"][l], P["dsk"][l],
        P["ssd_ng"][l], name=tag + "ssd")
    att = (sv["q"], sv["k"], sv["v"], sv["o_mla"], sv["lse_mla"], dycat)
    dq = _attn_bwd_dq(*att, name=tag + "mla_dq")
    dk, dv = _attn_bwd_dkv(*att, name=tag + "mla_dkv")
    dcq, dckv, dmisc, dqh, dkv, qn, kvn, G["gq"], G["gkv"] = _mla_prep_bwd(
        proj, dq, dk, dv, dmisc_ssd, *tabs, P["gq"][l], P["gkv"][l], P["w_uq"][l], P["w_ukv"][l], name=tag + "mla_prep")
    G["w_uq"] = _mm(qn, dqh, ta=True, name=tag + "dw_uq")
    G["w_ukv"] = _mm(kvn, dkv, ta=True, name=tag + "dw_ukv")
    dsq, dsk_, dsv_, G["sinks"] = _swa_bwd(proj, sv["o_swa"], sv["lse_swa"], dycat, P["sinks"][l], name=tag + "swa")
    dproj = jnp.concatenate([dpre, dz, dcq, dckv, dmisc, dsq, dsk_, dsv_], axis=1)
    dh1 = _mm(dproj, P["w_in"][l], tb=True, name=tag + "dh1")
    G["w_in"] = _mm(sv["h1"], dproj, ta=True, name=tag + "dw_in")
    dx0, G["n1g"], dsc1, dsh1 = _norm_bwd(dh1, sv["x0"], dx1, P["n1g"][l], sc1, name=tag + "norm1")
    G["mod"] = jnp.concatenate([dsh1, dsc1, dg1, dsh2, dsc2, dg2], axis=0)
    return dx0, G


def _local_step(x, tgt, mods, P, tabs):
    saved = []
    xin, f, gate = x, None, None
    for l in range(DEPTH):
        xin, f, gate, sv = _layer_fwd(l, xin, f, gate, mods[l], P, tabs)
        saved.append(sv)
    loss, dx, dfinal = _final_loss(xin, f, gate, P["final_g"], tgt, name="final_loss")
    grads = [None] * DEPTH
    for l in reversed(range(DEPTH)):
        dx, grads[l] = _layer_bwd(l, dx, saved[l], P, tabs)
    return loss[0, 0], dx, dfinal, grads


_WEIGHTS = ['ada_w', 'ada_b', 'norm1_g', 'norm2_g', 'w_in', 'ssd_conv_w', 'ssd_conv_b', 'ssd_dt_bias', 'ssd_a_log',
            'ssd_d', 'ssd_norm_g', 'mla_q_norm_g', 'mla_w_uq', 'mla_kv_norm_g', 'mla_w_ukv', 'swa_sinks', 'w_out',
            'ffn_w_up', 'ffn_conv_w', 'ffn_conv_b', 'ffn_w_down', 'final_norm_g']
_INPUTS = ['x', 'c', 'positions'] + _WEIGHTS + ['loss_target'] + ['m_' + n for n in _WEIGHTS] + ['v_' + n for n in _WEIGHTS]
_SMALL = [('ada_b', 'mod'), ('norm1_g', 'n1g'), ('norm2_g', 'n2g'), ('ssd_conv_b', 'ssd_cb'), ('ssd_dt_bias', 'dtb'),
          ('ssd_a_log', 'alog'), ('ssd_d', 'dsk'), ('ssd_norm_g', 'ssd_ng'), ('mla_q_norm_g', 'gq'),
          ('mla_kv_norm_g', 'gkv'), ('swa_sinks', 'sinks'), ('ffn_conv_b', 'fcb')]
_SHARDED = [('w_in', 'w_in', 2), ('ssd_conv_w', 'ssd_cw', 2), ('mla_w_uq', 'w_uq', 2), ('mla_w_ukv', 'w_ukv', 2),
            ('w_out', 'w_out', 1), ('ffn_w_up', 'w_up', 2), ('ffn_conv_w', 'fcw', 2), ('ffn_w_down', 'w_down', 1)]


def _pack_small(per_layer, final):
    parts = []
    for name, _ in _SMALL:
        v = per_layer[name]
        v = v.reshape(DEPTH, -1)
        pad = (-v.shape[1]) % LANE
        parts.append(jnp.pad(v, ((0, 0), (0, pad))).reshape(-1))
    parts.append(final.reshape(-1))
    return jnp.concatenate(parts).reshape(-1, LANE)


def _unpack_small(packed, shapes):
    flat = packed.reshape(-1)
    out, off = {}, 0
    for name, _ in _SMALL:
        n = math.prod(shapes[name][1:])
        npad = n + (-n) % LANE
        out[name] = flat[off:off + DEPTH * npad].reshape(DEPTH, npad)[:, :n].reshape(shapes[name])
        off += DEPTH * npad
    out['final_norm_g'] = flat[off:off + D]
    return out


def _shard_major(g, axis):
    shp = g.shape
    g = g.reshape(shp[:axis] + (NDEV, shp[axis] // NDEV) + shp[axis + 1:])
    return jnp.moveaxis(g, axis, 0)


def _unshard(g, axis):
    g = jnp.moveaxis(g, 0, axis)
    shp = g.shape
    return g.reshape(shp[:axis] + (shp[axis] * shp[axis + 1],) + shp[axis + 2:])


def kernel(x, c, positions, ada_w, ada_b, norm1_g, norm2_g, w_in, ssd_conv_w, ssd_conv_b, ssd_dt_bias, ssd_a_log, ssd_d, ssd_norm_g, mla_q_norm_g, mla_w_uq, mla_kv_norm_g, mla_w_ukv, swa_sinks, w_out, ffn_w_up, ffn_conv_w, ffn_conv_b, ffn_w_down, final_norm_g, loss_target, m_ada_w, m_ada_b, m_norm1_g, m_norm2_g, m_w_in, m_ssd_conv_w, m_ssd_conv_b, m_ssd_dt_bias, m_ssd_a_log, m_ssd_d, m_ssd_norm_g, m_mla_q_norm_g, m_mla_w_uq, m_mla_kv_norm_g, m_mla_w_ukv, m_swa_sinks, m_w_out, m_ffn_w_up, m_ffn_conv_w, m_ffn_conv_b, m_ffn_w_down, m_final_norm_g, v_ada_w, v_ada_b, v_norm1_g, v_norm2_g, v_w_in, v_ssd_conv_w, v_ssd_conv_b, v_ssd_dt_bias, v_ssd_a_log, v_ssd_d, v_ssd_norm_g, v_mla_q_norm_g, v_mla_w_uq, v_mla_kv_norm_g, v_mla_w_ukv, v_swa_sinks, v_w_out, v_ffn_w_up, v_ffn_conv_w, v_ffn_conv_b, v_ffn_w_down, v_final_norm_g):
    a = dict(zip(_INPUTS, (x, c, positions, ada_w, ada_b, norm1_g, norm2_g, w_in, ssd_conv_w, ssd_conv_b, ssd_dt_bias, ssd_a_log, ssd_d, ssd_norm_g, mla_q_norm_g, mla_w_uq, mla_kv_norm_g, mla_w_ukv, swa_sinks, w_out, ffn_w_up, ffn_conv_w, ffn_conv_b, ffn_w_down, final_norm_g, loss_target, m_ada_w, m_ada_b, m_norm1_g, m_norm2_g, m_w_in, m_ssd_conv_w, m_ssd_conv_b, m_ssd_dt_bias, m_ssd_a_log, m_ssd_d, m_ssd_norm_g, m_mla_q_norm_g, m_mla_w_uq, m_mla_kv_norm_g, m_mla_w_ukv, m_swa_sinks, m_w_out, m_ffn_w_up, m_ffn_conv_w, m_ffn_conv_b, m_ffn_w_down, m_final_norm_g, v_ada_w, v_ada_b, v_norm1_g, v_norm2_g, v_w_in, v_ssd_conv_w, v_ssd_conv_b, v_ssd_dt_bias, v_ssd_a_log, v_ssd_d, v_ssd_norm_g, v_mla_q_norm_g, v_mla_w_uq, v_mla_kv_norm_g, v_mla_w_ukv, v_swa_sinks, v_w_out, v_ffn_w_up, v_ffn_conv_w, v_ffn_conv_b, v_ffn_w_down, v_final_norm_g)))
    axes = ("x", "y", "c")
    me = 4 * lax.axis_index("x") + 2 * lax.axis_index("y") + lax.axis_index("c")
    ncol = ada_w.shape[-1]

    mxu_names = ['w_in', 'mla_w_uq', 'mla_w_ukv', 'w_out', 'ffn_w_up', 'ffn_w_down']
    sent = [a[n].astype(_MXU) for n in mxu_names] + [ssd_conv_w, ffn_conv_w, c]
    got = _exchange(sent, scatter=False, name="gather_weights")
    full = {n: _unshard(g, ax) for (n, _, ax), g in
            zip([s for s in _SHARDED if s[0] in mxu_names] + [s for s in _SHARDED if s[0] not in mxu_names], got[:8])}
    c_act = _silu_call(got[8].reshape(NDEV, D), name="c_act")
    mod_part = jnp.stack([_mm(c_act, ada_w[l], name=f"mod{l}") for l in range(DEPTH)])
    mod_all = _exchange([mod_part], scatter=False, name="gather_mod")[0]
    mod_mine = lax.dynamic_index_in_dim(mod_all, me, axis=2, keepdims=False)
    mods = (jnp.moveaxis(mod_mine, 0, 1).reshape(DEPTH, 6 * D) + ada_b).reshape(DEPTH, 6, D)

    vec = lambda v: v.reshape(DEPTH, 1, -1)
    P = dict(
        w_in=_w_in_to_padded(full['w_in']), w_out=_w_out_to_padded(full['w_out']), w_up=full['ffn_w_up'],
        w_down=full['ffn_w_down'], w_uq=_pad_heads(full['mla_w_uq'], MLA_HEADS, MLA_NOPE + MLA_ROPE),
        w_ukv=_w_ukv_to_padded(full['mla_w_ukv']), ssd_cw=full['ssd_conv_w'], fcw=full['ffn_conv_w'],
        ssd_cb=vec(ssd_conv_b), dtb=vec(_pad_lane(ssd_dt_bias)), alog=vec(_pad_lane(ssd_a_log)),
        dsk=vec(_pad_lane(ssd_d)), ssd_ng=vec(ssd_norm_g), gq=vec(mla_q_norm_g), gkv=vec(mla_kv_norm_g),
        sinks=vec(_pad_lane(swa_sinks)), fcb=vec(ffn_conv_b), n1g=vec(norm1_g), n2g=vec(norm2_g),
        final_g=final_norm_g.reshape(1, D))
    tabs = _rope_tables(positions)

    loss, dx, dfinal, grads = _local_step(x[0], loss_target[0], mods, P, tabs)
    loss = lax.psum(loss, axes)

    stack = lambda key: jnp.stack([grads[l][key] for l in range(DEPTH)])
    unpad = dict(w_in=_w_in_from_padded, w_out=_w_out_from_padded, w_ukv=_w_ukv_from_padded,
                 w_uq=lambda g: _unpad_heads(g, MLA_HEADS, MLA_NOPE + MLA_ROPE))
    big = [_shard_major(unpad.get(key, lambda g: g)(stack(key)), ax) for _, key, ax in _SHARDED]
    parts = _exchange(big, scatter=True, name="scatter_grads")
    small_g = {name: stack(key).reshape(DEPTH, -1) for name, key in _SMALL}
    small_parts = _exchange([_pack_small(small_g, dfinal)], scatter=False, name="gather_small")[0]

    out_g, out_d, out_m, out_v = {}, {}, {}, {}

    def update(name, wv, mv, vv, pv):
        shp = wv.shape
        r = lambda t: t.reshape((-1, shp[-1]))
        res = _adamw(r(wv), r(mv), r(vv), pv.reshape((pv.shape[0], -1, shp[-1])), name="adamw_" + name)
        out_g[name], out_d[name], out_m[name], out_v[name] = [t.reshape(shp) for t in res]

    for (name, _, _), pv in zip(_SHARDED, parts):
        update(name, a[name], a['m_' + name], a['v_' + name], pv)
    n_ada = DEPTH * 6 * D // LANE
    dmod_all = small_parts[:, :n_ada].reshape(NDEV, DEPTH, 6 * D)
    dmod_mine = lax.dynamic_slice_in_dim(dmod_all, me * ncol, ncol, axis=2)
    g_ada = jnp.stack([_mm(c_act, dmod_mine[:, l], ta=True, name=f"dw_ada{l}") for l in range(DEPTH)])
    update('ada_w', ada_w, m_ada_w, v_ada_w, g_ada[None])
    shapes = {n: a[n].shape for n, _ in _SMALL}
    pk = lambda pre: _pack_small({n: a[pre + n] for n, _ in _SMALL}, a[pre + 'final_norm_g'])
    res = _adamw(pk(''), pk('m_'), pk('v_'), small_parts, name="adamw_small")
    for dst, t in zip((out_g, out_d, out_m, out_v), res):
        dst.update(_unpack_small(t, shapes))

    outs = [loss, dx[None]]
    for dct in (out_g, out_d, out_m, out_v):
        outs += [dct[n] for n in _WEIGHTS]
    return tuple(outs)
```

```python
import functools
import math

import jax
import jax.numpy as jnp
from jax import lax
from jax.experimental import pallas as pl
from jax.experimental.pallas import tpu as pltpu

F32 = jnp.float32
_MXU = jnp.bfloat16
_ACT = jnp.bfloat16
_HI = lax.Precision.HIGHEST
EPS = 1e-6
NDEV = 8
DEPTH = 4
D = 1024
LANE = 128
SUB = 8
VMEM_LIMIT = 56 * 1024 * 1024

SSD_INNER, SSD_STATE, SSD_HEADS, SSD_GROUPS, SSD_CHUNK, SSD_CONV = 512, 128, 8, 2, 128, 4
SSD_XBC = SSD_INNER + 2 * SSD_GROUPS * SSD_STATE
MLA_HEADS, MLA_NOPE, MLA_ROPE, MLA_V, MLA_QR, MLA_KVR = 4, 64, 32, 64, 256, 128
SWA_HEADS, SWA_KV, SWA_HD, WINDOW = 4, 2, 64, 128
D_FF, FFN_CONV = 2816, 3
D_IN = 2472
ROPE_THETA = 10000.0
C_XBC, C_Z, C_CQ, C_CKV, C_MISC, C_SQ, C_SK, C_SV, D_INP = 0, 1024, 1536, 1792, 1920, 2048, 2560, 2816, 3072
ROPE_LANE = 64
D_MIXP = 1536

ADAM_LR, ADAM_B1, ADAM_B2, ADAM_EPS, ADAM_WD, ADAM_STEP = 0.001, 0.9, 0.999, 1e-08, 0.01, 10

TS_ROW = 512
TS_FFN = 256
TQ_ATT = 512
TS_SWA = 512


def _tile(n, cap, q=LANE):
    best = None
    for t in range(q, min(n, cap) + 1, q):
        if n % t == 0:
            best = t
    return n if best is None else best


def _cp(ngrid):
    return pltpu.CompilerParams(dimension_semantics=("arbitrary",) * ngrid, vmem_limit_bytes=VMEM_LIMIT)


def _dot(a, b):
    return jnp.dot(a.astype(_MXU), b.astype(_MXU), preferred_element_type=F32)


def _dot_nt(a, b):
    return lax.dot_general(a.astype(_MXU), b.astype(_MXU), (((1,), (1,)), ((), ())), preferred_element_type=F32)


def _dot_tn(a, b):
    return jnp.dot(a.T.astype(_MXU), b.astype(_MXU), preferred_element_type=F32)


def _sigmoid(x):
    return 1.0 / (1.0 + jnp.exp(-x))


def _silu(x):
    return x * _sigmoid(x)


def _dsilu(x):
    s = _sigmoid(x)
    return s * (1.0 + x * (1.0 - s))


def _softplus(x):
    u = jnp.exp(-jnp.abs(x))
    w = 1.0 + u
    log1p = jnp.where(w == 1.0, u, jnp.log(w) * u / jnp.where(w == 1.0, 1.0, w - 1.0))
    return jnp.maximum(x, 0.0) + log1p


def _colsum(x):
    return jnp.sum(x, axis=0, keepdims=True)


def _rowsum(x):
    return jnp.sum(x, axis=1, keepdims=True)


def _shift_down(t, halo, j):
    if j == 0:
        return t
    n = t.shape[0]
    rolled = pltpu.roll(t, j, 0)
    row = lax.broadcasted_iota(jnp.int32, (SUB, t.shape[1]), 0)
    first = jnp.where(row < j, pltpu.roll(halo, j, 0), rolled[:SUB])
    return jnp.concatenate([first, rolled[SUB:]], axis=0) if n > SUB else first


def _shift_up(t, halo, j):
    if j == 0:
        return t
    n = t.shape[0]
    rolled = pltpu.roll(t, n - j, 0)
    row = lax.broadcasted_iota(jnp.int32, (SUB, t.shape[1]), 0)
    last = jnp.where(row >= SUB - j, pltpu.roll(halo, SUB - j, 0), rolled[n - SUB:])
    return jnp.concatenate([rolled[:n - SUB], last], axis=0) if n > SUB else last


def _mm(a, b, *, ta=False, tb=False, out_dtype=F32, name):
    if ta:
        K, M = a.shape
    else:
        M, K = a.shape
    if tb:
        N, K2 = b.shape
    else:
        K2, N = b.shape
    assert K == K2, (a.shape, b.shape, ta, tb)
    tm, tn, tk = _tile(M, 1024), _tile(N, 1408), _tile(K, 1024)
    nk = K // tk
    dn = (((0 if ta else 1,), (1 if tb else 0,)), ((), ()))

    def body(a_ref, b_ref, o_ref, acc_ref):
        k = pl.program_id(2)
        part = lax.dot_general(a_ref[...].astype(_MXU), b_ref[...].astype(_MXU), dn, preferred_element_type=F32)

        @pl.when(k == 0)
        def _():
            acc_ref[...] = part

        @pl.when(k > 0)
        def _():
            acc_ref[...] += part

        @pl.when(k == nk - 1)
        def _():
            o_ref[...] = acc_ref[...].astype(out_dtype)

    a_spec = pl.BlockSpec((tk, tm), lambda i, j, k: (k, i)) if ta else pl.BlockSpec((tm, tk), lambda i, j, k: (i, k))
    b_spec = pl.BlockSpec((tn, tk), lambda i, j, k: (j, k)) if tb else pl.BlockSpec((tk, tn), lambda i, j, k: (k, j))
    return pl.pallas_call(
        body, grid=(M // tm, N // tn, nk), in_specs=[a_spec, b_spec],
        out_specs=pl.BlockSpec((tm, tn), lambda i, j, k: (i, j)),
        out_shape=jax.ShapeDtypeStruct((M, N), out_dtype),
        scratch_shapes=[pltpu.VMEM((tm, tn), F32)], compiler_params=_cp(3), name=name)(a, b)


def _row(ts, w, col=0):
    return pl.BlockSpec((ts, w), lambda i: (i, col))


def _vec(w, r=1):
    return pl.BlockSpec((r, w), lambda i: (0, 0))


def _silu_call(x, name):
    def body(x_ref, o_ref):
        o_ref[...] = _silu(x_ref[...])
    return pl.pallas_call(body, out_shape=jax.ShapeDtypeStruct(x.shape, F32), name=name)(x)


def _norm_fwd(x, g, sc, sh, *, f=None, gate=None, name):
    S, dm = x.shape
    ts = _tile(S, TS_ROW, SUB)
    res = f is not None

    def body(*refs):
        if res:
            x_ref, f_ref, gate_ref, g_ref, sc_ref, sh_ref, xo_ref, h_ref = refs
            xv = x_ref[...] + gate_ref[...] * f_ref[...]
            xo_ref[...] = xv
        else:
            x_ref, g_ref, sc_ref, sh_ref, h_ref = refs
            xv = x_ref[...]
        rstd = lax.rsqrt(jnp.mean(xv * xv, axis=-1, keepdims=True) + EPS)
        h_ref[...] = ((xv * rstd) * g_ref[...] * (1.0 + sc_ref[...]) + sh_ref[...]).astype(_ACT)

    ins = [x] + ([f, gate] if res else []) + [g, sc, sh]
    in_specs = [_row(ts, dm)] + ([_row(ts, dm), _vec(dm)] if res else []) + [_vec(dm)] * 3
    h_shape = jax.ShapeDtypeStruct((S, dm), _ACT)
    if res:
        out_shape, out_specs = (jax.ShapeDtypeStruct((S, dm), F32), h_shape), (_row(ts, dm), _row(ts, dm))
    else:
        out_shape, out_specs = h_shape, _row(ts, dm)
    return pl.pallas_call(body, grid=(S // ts,), in_specs=in_specs, out_specs=out_specs, out_shape=out_shape,
                          compiler_params=_cp(1), name=name)(*ins)


def _norm_bwd(dh, x, dres, g, sc, *, name):
    S, dm = x.shape
    ts = _tile(S, TS_ROW, SUB)

    def body(dh_ref, x_ref, dres_ref, g_ref, sc_ref, dx_ref, dg_ref, dsc_ref, dsh_ref):
        i = pl.program_id(0)
        xv = x_ref[...]
        dhv = dh_ref[...]
        rstd = lax.rsqrt(jnp.mean(xv * xv, axis=-1, keepdims=True) + EPS)
        xhat = xv * rstd
        hn = xhat * g_ref[...]
        dhn = dhv * (1.0 + sc_ref[...])
        dxh = dhn * g_ref[...]
        dx_ref[...] = dres_ref[...] + rstd * (dxh - xhat * jnp.mean(dxh * xhat, axis=-1, keepdims=True))

        @pl.when(i == 0)
        def _():
            dg_ref[...] = jnp.zeros_like(dg_ref)
            dsc_ref[...] = jnp.zeros_like(dsc_ref)
            dsh_ref[...] = jnp.zeros_like(dsh_ref)

        dg_ref[...] += _colsum(dhn * xhat)
        dsc_ref[...] += _colsum(dhv * hn)
        dsh_ref[...] += _colsum(dhv)

    vshape = jax.ShapeDtypeStruct((1, dm), F32)
    return pl.pallas_call(
        body, grid=(S // ts,), in_specs=[_row(ts, dm)] * 3 + [_vec(dm)] * 2,
        out_specs=(_row(ts, dm), _vec(dm), _vec(dm), _vec(dm)),
        out_shape=(jax.ShapeDtypeStruct((S, dm), F32), vshape, vshape, vshape),
        compiler_params=_cp(1), name=name)(dh, x, dres, g, sc)


def _gate_bwd(dxo, f, gate, *, name):
    S, dm = f.shape
    ts = _tile(S, TS_ROW, SUB)

    def body(dxo_ref, f_ref, gate_ref, df_ref, dgate_ref):
        i = pl.program_id(0)
        dv = dxo_ref[...]
        df_ref[...] = (gate_ref[...] * dv).astype(_ACT)

        @pl.when(i == 0)
        def _():
            dgate_ref[...] = jnp.zeros_like(dgate_ref)

        dgate_ref[...] += _colsum(dv * f_ref[...])

    return pl.pallas_call(
        body, grid=(S // ts,), in_specs=[_row(ts, dm), _row(ts, dm), _vec(dm)],
        out_specs=(_row(ts, dm), _vec(dm)),
        out_shape=(jax.ShapeDtypeStruct((S, dm), _ACT), jax.ShapeDtypeStruct((1, dm), F32)),
        compiler_params=_cp(1), name=name)(dxo, f, gate)


def _final_loss(x, f, gate, g, tgt, *, name):
    S, dm = x.shape
    ts = _tile(S, TS_ROW, SUB)

    def body(x_ref, f_ref, gate_ref, g_ref, t_ref, loss_ref, dx_ref, dg_ref):
        i = pl.program_id(0)
        xv = x_ref[...] + gate_ref[...] * f_ref[...]
        rstd = lax.rsqrt(jnp.mean(xv * xv, axis=-1, keepdims=True) + EPS)
        xhat = xv * rstd
        err = xhat * g_ref[...] - t_ref[...]
        dy = err * (1.0 / dm)
        dxh = dy * g_ref[...]
        dx_ref[...] = rstd * (dxh - xhat * jnp.mean(dxh * xhat, axis=-1, keepdims=True))

        @pl.when(i == 0)
        def _():
            loss_ref[...] = jnp.zeros_like(loss_ref)
            dg_ref[...] = jnp.zeros_like(dg_ref)

        loss_ref[...] += jnp.full((1, LANE), 0.5 * jnp.sum(jnp.mean(err * err, axis=-1, keepdims=True)), F32)
        dg_ref[...] += _colsum(dy * xhat)

    return pl.pallas_call(
        body, grid=(S // ts,), in_specs=[_row(ts, dm), _row(ts, dm), _vec(dm), _vec(dm), _row(ts, dm)],
        out_specs=(_vec(LANE), _row(ts, dm), _vec(dm)),
        out_shape=(jax.ShapeDtypeStruct((1, LANE), F32), jax.ShapeDtypeStruct((S, dm), F32),
                   jax.ShapeDtypeStruct((1, dm), F32)),
        compiler_params=_cp(1), name=name)(x, f, gate, g, tgt)


def _ffn_conv(t, halo, cw_ref, cb_ref):
    return ((cb_ref[...] + _shift_down(t, halo, 2) * cw_ref[0:1, :]) + _shift_down(t, halo, 1) * cw_ref[1:2, :]) \
        + t * cw_ref[2:3, :]


def _prev_halo_spec(ts, w, col=0):
    return pl.BlockSpec((SUB, w), lambda i: (jnp.maximum(i * (ts // SUB) - 1, 0), col))


def _ffn_act_fwd(up, cw, cb, *, name):
    S, w2 = up.shape
    ff = w2 // 2
    ts = _tile(S, TS_FFN, SUB)

    def body(up_ref, halo_ref, cw_ref, cb_ref, act_ref):
        i = pl.program_id(0)
        t = up_ref[...]
        halo = jnp.where(i > 0, halo_ref[...], 0.0)
        u = _ffn_conv(t, halo, cw_ref, cb_ref)
        act_ref[...] = (_silu(u[:, :ff]) * u[:, ff:]).astype(_ACT)

    return pl.pallas_call(
        body, grid=(S // ts,), in_specs=[_row(ts, w2), _prev_halo_spec(ts, w2), _vec(w2, FFN_CONV), _vec(w2)],
        out_specs=_row(ts, ff), out_shape=jax.ShapeDtypeStruct((S, ff), _ACT),
        compiler_params=_cp(1), name=name)(up, up, cw, cb)


def _ffn_act_bwd(up, dact, cw, cb, *, name):
    S, w2 = up.shape
    ff = w2 // 2
    ts = _tile(S, TS_FFN, SUB)

    def body(up_ref, halo_ref, dact_ref, cw_ref, cb_ref, du_ref):
        i = pl.program_id(0)
        t = up_ref[...]
        halo = jnp.where(i > 0, halo_ref[...], 0.0)
        u = _ffn_conv(t, halo, cw_ref, cb_ref)
        a, b = u[:, :ff], u[:, ff:]
        da = dact_ref[...]
        du_ref[:, :ff] = da * b * _dsilu(a)
        du_ref[:, ff:] = da * _silu(a)

    return pl.pallas_call(
        body, grid=(S // ts,),
        in_specs=[_row(ts, w2), _prev_halo_spec(ts, w2), _row(ts, ff), _vec(w2, FFN_CONV), _vec(w2)],
        out_specs=_row(ts, w2), out_shape=jax.ShapeDtypeStruct((S, w2), F32),
        compiler_params=_cp(1), name=name)(up, up, dact, cw, cb)


def _ffn_conv_bwd(du, up, cw, *, name):
    S, w2 = up.shape
    ts = _tile(S, TS_FFN, SUB)
    n = S // ts

    def body(du_ref, nxt_ref, up_ref, halo_ref, cw_ref, dup_ref, dcw_ref, dcb_ref):
        i = pl.program_id(0)
        dv = du_ref[...]
        nxt = jnp.where(i < n - 1, nxt_ref[...], 0.0)
        t = up_ref[...]
        halo = jnp.where(i > 0, halo_ref[...], 0.0)
        dup = (dv * cw_ref[2:3, :] + _shift_up(dv, nxt, 1) * cw_ref[1:2, :]) + _shift_up(dv, nxt, 2) * cw_ref[0:1, :]
        dup_ref[...] = dup.astype(_ACT)

        @pl.when(i == 0)
        def _():
            dcw_ref[...] = jnp.zeros_like(dcw_ref)
            dcb_ref[...] = jnp.zeros_like(dcb_ref)

        dcb_ref[...] += _colsum(dv)
        dcw_ref[2:3, :] += _colsum(dv * t)
        dcw_ref[1:2, :] += _colsum(dv * _shift_down(t, halo, 1))
        dcw_ref[0:1, :] += _colsum(dv * _shift_down(t, halo, 2))

    nxt_spec = pl.BlockSpec((SUB, w2), lambda i: (jnp.minimum((i + 1) * (ts // SUB), S // SUB - 1), 0))
    return pl.pallas_call(
        body, grid=(n,),
        in_specs=[_row(ts, w2), nxt_spec, _row(ts, w2), _prev_halo_spec(ts, w2), _vec(w2, FFN_CONV)],
        out_specs=(_row(ts, w2), _vec(w2, FFN_CONV), _vec(w2)),
        out_shape=(jax.ShapeDtypeStruct((S, w2), _ACT), jax.ShapeDtypeStruct((FFN_CONV, w2), F32),
                   jax.ShapeDtypeStruct((1, w2), F32)),
        compiler_params=_cp(1), name=name)(du, du, up, up, cw)


def _ssd_core(pre, halo, misc, cw_ref, cb_ref, dtb, alog):
    q = pre.shape[0]
    conv = cb_ref[...]
    for k in range(SSD_CONV):
        conv = conv + _shift_down(pre, halo, SSD_CONV - 1 - k) * cw_ref[k:k + 1, :]
    xbc = _silu(conv)
    raw = misc + dtb
    dt = _softplus(raw)
    a = -jnp.exp(alog)
    r = lax.broadcasted_iota(jnp.int32, (q, q), 0)
    c = lax.broadcasted_iota(jnp.int32, (q, q), 1)
    tri = r >= c
    acum = jnp.dot(tri.astype(F32), dt * a, precision=_HI, preferred_element_type=F32)
    return conv, xbc, raw, dt, a, acum, acum.T, tri


def _sel(v, j, lo):
    return jnp.where(lo, v[:, 2 * j:2 * j + 1], v[:, 2 * j + 1:2 * j + 2])


def _ssd_pair_fwd(xbc, dt, acum, acum_t, tri, dsk, g_mat, b_mat, c_mat, h_pair, j, lo, lo1, sub_lo):
    q = xbc.shape[0]
    x = xbc[:, LANE * j:LANE * (j + 1)]
    dtp = _sel(dt, j, lo)
    ap = _sel(acum, j, lo)
    xd = x * dtp
    ls, ms = [], []
    for h in (2 * j, 2 * j + 1):
        seg = acum[:, h:h + 1] - acum_t[h:h + 1, :]
        l_mat = jnp.exp(jnp.where(tri, seg, -jnp.inf))
        ls.append(l_mat)
        ms.append(g_mat * l_mat)
    yd = jnp.where(lo, _dot(ms[0], xd), _dot(ms[1], xd))
    ea = jnp.exp(ap)
    yo = _dot_nt(c_mat, h_pair) * ea
    dp = _sel(dsk, j, lo1)
    alast = acum[q - 1:q, :]
    e = jnp.exp(_sel(alast, j, lo1) - ap)
    cd = jnp.where(sub_lo, jnp.exp(alast[:, 2 * j:2 * j + 1]), jnp.exp(alast[:, 2 * j + 1:2 * j + 2]))
    return dict(x=x, dtp=dtp, ap=ap, xd=xd, ls=ls, ms=ms, ea=ea, yo=yo, dp=dp, e=e, cd=cd, y=yd + yo + x * dp)


def _gnorm(yg):
    half = SSD_INNER // SSD_GROUPS
    rstds, yns = [], []
    for g in range(SSD_GROUPS):
        part = yg[:, half * g:half * (g + 1)]
        rstd = lax.rsqrt(jnp.mean(part * part, axis=-1, keepdims=True) + EPS)
        rstds.append(rstd)
        yns.append(part * rstd)
    return rstds, yns


def _ssd_specs(nc, rev):
    q = SSD_CHUNK
    cidx = (lambda i: nc - 1 - i) if rev else (lambda i: i)
    return [
        pl.BlockSpec((q, SSD_XBC), lambda i: (cidx(i), C_XBC // SSD_XBC)),
        pl.BlockSpec((SUB, SSD_XBC), lambda i: (jnp.maximum(cidx(i) * (q // SUB) - 1, 0), C_XBC // SSD_XBC)),
        pl.BlockSpec((q, SSD_INNER), lambda i: (cidx(i), C_Z // SSD_INNER)),
        pl.BlockSpec((q, LANE), lambda i: (cidx(i), C_MISC // LANE)),
    ]


def _ssd_param_specs():
    return [_vec(SSD_XBC, SSD_CONV), _vec(SSD_XBC), _vec(LANE), _vec(LANE), _vec(LANE), _vec(SSD_INNER)]


def _ssd_fwd(proj, cw, cb, dtb, alog, dsk, ng, *, name):
    S = proj.shape[0]
    q = SSD_CHUNK
    nc = S // q
    npair = SSD_HEADS // 2

    def body(xbc_ref, halo_ref, z_ref, misc_ref, cw_ref, cb_ref, dtb_ref, alog_ref, dsk_ref, ng_ref,
             y_ref, hin_ref, h_ref):
        c = pl.program_id(0)

        @pl.when(c == 0)
        def _():
            h_ref[...] = jnp.zeros_like(h_ref)

        pre = xbc_ref[...]
        halo = jnp.where(c > 0, halo_ref[...], 0.0)
        conv, xbc, raw, dt, a, acum, acum_t, tri = _ssd_core(pre, halo, misc_ref[...], cw_ref, cb_ref,
                                                             dtb_ref[...], alog_ref[...])
        lo = lax.broadcasted_iota(jnp.int32, (q, LANE), 1) < LANE // 2
        lo1 = lo[:1]
        sub_lo = lax.broadcasted_iota(jnp.int32, (LANE, LANE), 0) < LANE // 2
        ys = []
        for g in range(SSD_GROUPS):
            b_mat = xbc[:, SSD_INNER + SSD_STATE * g:SSD_INNER + SSD_STATE * (g + 1)]
            c_mat = xbc[:, SSD_INNER + SSD_STATE * (SSD_GROUPS + g):SSD_INNER + SSD_STATE * (SSD_GROUPS + g + 1)]
            g_mat = _dot_nt(c_mat, b_mat)
            for jj in range(npair // SSD_GROUPS):
                j = g * (npair // SSD_GROUPS) + jj
                hj = h_ref[j]
                p = _ssd_pair_fwd(xbc, dt, acum, acum_t, tri, dsk_ref[...], g_mat, b_mat, c_mat, hj, j, lo, lo1, sub_lo)
                ys.append(p["y"])
                hin_ref[0, j] = hj
                h_ref[j] = p["cd"] * hj + _dot_tn(p["xd"] * p["e"], b_mat)
        yg = jnp.concatenate(ys, axis=1) * _silu(z_ref[...])
        _, yns = _gnorm(yg)
        y_ref[...] = jnp.concatenate(yns, axis=1) * ng_ref[...]

    return pl.pallas_call(
        body, grid=(nc,), in_specs=_ssd_specs(nc, False) + _ssd_param_specs(),
        out_specs=(pl.BlockSpec((q, SSD_INNER), lambda i: (i, 0)),
                   pl.BlockSpec((1, npair, LANE, LANE), lambda i: (i, 0, 0, 0))),
        out_shape=(jax.ShapeDtypeStruct((S, SSD_INNER), F32), jax.ShapeDtypeStruct((nc, npair, LANE, LANE), F32)),
        scratch_shapes=[pltpu.VMEM((npair, LANE, LANE), F32)], compiler_params=_cp(1), name=name,
    )(proj, proj, proj, proj, cw, cb, dtb, alog, dsk, ng)


def _ssd_bwd(proj, dycat, hin, cw, cb, dtb, alog, dsk, ng, *, name):
    S = proj.shape[0]
    q = SSD_CHUNK
    nc = S // q
    npair = SSD_HEADS // 2
    ppg = npair // SSD_GROUPS

    def body(xbc_ref, halo_ref, z_ref, misc_ref, dy_ref, hin_ref, cw_ref, cb_ref, dtb_ref, alog_ref, dsk_ref, ng_ref,
             dpre_ref, dz_ref, dmisc_ref, dcw_ref, dcb_ref, ddtb_ref, dalog_ref, ddsk_ref, dng_ref,
             dh_ref, carry_ref):
        i = pl.program_id(0)
        c = nc - 1 - i

        @pl.when(i == 0)
        def _():
            dh_ref[...] = jnp.zeros_like(dh_ref)
            carry_ref[...] = jnp.zeros_like(carry_ref)
            for r in (dcw_ref, dcb_ref, ddtb_ref, dalog_ref, ddsk_ref, dng_ref):
                r[...] = jnp.zeros_like(r)

        pre = xbc_ref[...]
        halo = jnp.where(c > 0, halo_ref[...], 0.0)
        conv, xbc, raw, dt, a, acum, acum_t, tri = _ssd_core(pre, halo, misc_ref[...], cw_ref, cb_ref,
                                                             dtb_ref[...], alog_ref[...])
        lane = lax.broadcasted_iota(jnp.int32, (q, LANE), 1)
        lane1 = lane[:1]
        rowi = lax.broadcasted_iota(jnp.int32, (q, LANE), 0)
        lastrow = rowi == q - 1
        lo = lane < LANE // 2
        lo1 = lo[:1]
        sub_lo = lax.broadcasted_iota(jnp.int32, (LANE, LANE), 0) < LANE // 2
        dsk = dsk_ref[...]
        alast = acum[q - 1:q, :]

        def halves(t):
            return _rowsum(jnp.where(lo, t, 0.0)), _rowsum(jnp.where(lo, 0.0, t))

        def put(ha, va, vb):
            ln = lane if va.shape[0] == q else lane1
            return jnp.where(ln == ha, va, 0.0) + jnp.where(ln == ha + 1, vb, 0.0)

        mats, pairs = [], []
        for g in range(SSD_GROUPS):
            b_mat = xbc[:, SSD_INNER + SSD_STATE * g:SSD_INNER + SSD_STATE * (g + 1)]
            c_mat = xbc[:, SSD_INNER + SSD_STATE * (SSD_GROUPS + g):SSD_INNER + SSD_STATE * (SSD_GROUPS + g + 1)]
            g_mat = _dot_nt(c_mat, b_mat)
            mats.append((b_mat, c_mat, g_mat))
            for jj in range(ppg):
                j = g * ppg + jj
                pairs.append(_ssd_pair_fwd(xbc, dt, acum, acum_t, tri, dsk, g_mat, b_mat, c_mat, hin_ref[0, j],
                                           j, lo, lo1, sub_lo))
        z = z_ref[...]
        sz = _silu(z)
        y = jnp.concatenate([p["y"] for p in pairs], axis=1)
        rstds, yns = _gnorm(y * sz)
        dout = dy_ref[...]
        dng_ref[...] += _colsum(dout * jnp.concatenate(yns, axis=1))
        dyn = dout * ng_ref[...]
        half = SSD_INNER // SSD_GROUPS
        dygs = []
        for g in range(SSD_GROUPS):
            dyn_g = dyn[:, half * g:half * (g + 1)]
            dygs.append(rstds[g] * (dyn_g - yns[g] * jnp.mean(dyn_g * yns[g], axis=-1, keepdims=True)))
        dyg = jnp.concatenate(dygs, axis=1)
        dyv = dyg * sz
        dz_ref[...] = (dyg * y * _dsilu(z)).astype(_ACT)

        da_acc = jnp.zeros((q, LANE), F32)
        ddt = jnp.zeros((q, LANE), F32)
        dds = jnp.zeros((1, LANE), F32)
        dxs, dbs, dcs = [], [], []
        for g in range(SSD_GROUPS):
            b_mat, c_mat, g_mat = mats[g]
            dg_mat = jnp.zeros((q, q), F32)
            db = jnp.zeros((q, SSD_STATE), F32)
            dc = jnp.zeros((q, SSD_STATE), F32)
            for jj in range(ppg):
                j = g * ppg + jj
                ha = 2 * j
                p = pairs[j]
                hj = hin_ref[0, j]
                dyp = dyv[:, LANE * j:LANE * (j + 1)]
                dsum = _colsum(dyp * p["x"])
                dds = dds + put(ha, _rowsum(jnp.where(lo1, dsum, 0.0)), _rowsum(jnp.where(lo1, 0.0, dsum)))
                dx = dyp * p["dp"]
                dw = dyp * p["ea"]
                dc = dc + _dot(dw, hj)
                dh_yo = _dot_tn(dw, c_mat)
                ra, rb = halves(dyp * p["yo"])
                da_acc = da_acc + put(ha, ra, rb)
                dxd = jnp.zeros((q, LANE), F32)
                for idx in range(2):
                    dyh = jnp.where(lo, dyp, 0.0) if idx == 0 else jnp.where(lo, 0.0, dyp)
                    dm = _dot_nt(dyh, p["xd"])
                    dxd = dxd + _dot_tn(p["ms"][idx], dyh)
                    dg_mat = dg_mat + dm * p["ls"][idx]
                    t = dm * p["ms"][idx]
                    da_h = _rowsum(t) - _rowsum(t.T)
                    da_acc = da_acc + jnp.where(lane == ha + idx, da_h, 0.0)
                dhn = dh_ref[j]
                s = _rowsum(dhn * hj)
                sa = jnp.sum(jnp.where(sub_lo[:, :1], s, 0.0), keepdims=True)
                sb = jnp.sum(jnp.where(sub_lo[:, :1], 0.0, s), keepdims=True)
                cda, cdb = jnp.exp(alast[:, ha:ha + 1]), jnp.exp(alast[:, ha + 1:ha + 2])
                db = db + _dot(p["xd"] * p["e"], dhn)
                r = _dot_nt(b_mat, dhn)
                dxd = dxd + r * p["e"]
                qa, qb = halves(r * p["xd"] * p["e"])
                da_acc = da_acc - put(ha, qa, qb)
                tot_a = sa * cda + jnp.sum(qa, keepdims=True)
                tot_b = sb * cdb + jnp.sum(qb, keepdims=True)
                da_acc = da_acc + jnp.where(lastrow, put(ha, tot_a, tot_b), 0.0)
                dh_ref[j] = p["cd"] * dhn + dh_yo
                dx = dx + dxd * p["dtp"]
                ua, ub = halves(dxd * p["x"])
                ddt = ddt + put(ha, ua, ub)
                dxs.append(dx)
            dc = dc + _dot(dg_mat, b_mat)
            db = db + _dot_tn(dg_mat, c_mat)
            dbs.append(db)
            dcs.append(dc)
        r2 = lax.broadcasted_iota(jnp.int32, (q, q), 0)
        c2 = lax.broadcasted_iota(jnp.int32, (q, q), 1)
        dda = jnp.dot((c2 >= r2).astype(F32), da_acc, precision=_HI, preferred_element_type=F32)
        ddt = ddt + dda * a
        dalog_ref[...] += _colsum(dda * dt) * a
        ddsk_ref[...] += dds
        draw = jnp.where(lane < SSD_HEADS, ddt * _sigmoid(raw), 0.0)
        ddtb_ref[...] += _colsum(draw)
        dmisc_ref[...] = draw
        dconv = jnp.concatenate(dxs + dbs + dcs, axis=1) * _dsilu(conv)
        dcb_ref[...] += _colsum(dconv)
        nxt = carry_ref[...]
        dpre = jnp.zeros_like(dconv)
        for k in range(SSD_CONV):
            dcw_ref[k:k + 1, :] += _colsum(dconv * _shift_down(pre, halo, SSD_CONV - 1 - k))
            dpre = dpre + _shift_up(dconv, nxt, SSD_CONV - 1 - k) * cw_ref[k:k + 1, :]
        dpre_ref[...] = dpre.astype(_ACT)
        carry_ref[...] = dconv[:SUB]

    rev = lambda i: (nc - 1 - i, 0)
    vshape = lambda w, r=1: jax.ShapeDtypeStruct((r, w), F32)
    return pl.pallas_call(
        body, grid=(nc,),
        in_specs=_ssd_specs(nc, True) + [pl.BlockSpec((q, SSD_INNER), rev),
                                         pl.BlockSpec((1, npair, LANE, LANE), lambda i: (nc - 1 - i, 0, 0, 0))]
        + _ssd_param_specs(),
        out_specs=(pl.BlockSpec((q, SSD_XBC), rev), pl.BlockSpec((q, SSD_INNER), rev), pl.BlockSpec((q, LANE), rev),
                   _vec(SSD_XBC, SSD_CONV), _vec(SSD_XBC), _vec(LANE), _vec(LANE), _vec(LANE), _vec(SSD_INNER)),
        out_shape=(jax.ShapeDtypeStruct((S, SSD_XBC), _ACT), jax.ShapeDtypeStruct((S, SSD_INNER), _ACT),
                   jax.ShapeDtypeStruct((S, LANE), F32),
                   vshape(SSD_XBC, SSD_CONV), vshape(SSD_XBC), vshape(LANE), vshape(LANE), vshape(LANE),
                   vshape(SSD_INNER)),
        scratch_shapes=[pltpu.VMEM((npair, LANE, LANE), F32), pltpu.VMEM((SUB, SSD_XBC), F32)],
        compiler_params=_cp(1), name=name,
    )(proj, proj, proj, proj, dycat, hin, cw, cb, dtb, alog, dsk, ng)


def _rope(x, cosf, sina, sinb):
    return x * cosf + pltpu.roll(x, LANE - MLA_ROPE // 2, 1) * sina + pltpu.roll(x, MLA_ROPE // 2, 1) * sinb


def _rope_t(dy, cosf, sina, sinb):
    return dy * cosf + pltpu.roll(dy * sina, MLA_ROPE // 2, 1) + pltpu.roll(dy * sinb, LANE - MLA_ROPE // 2, 1)


def _rope_lanes(shape):
    lane = lax.broadcasted_iota(jnp.int32, shape, 1)
    return (lane >= ROPE_LANE) & (lane < ROPE_LANE + MLA_ROPE)


def _mla_prep_fwd(proj, cosf, sina, sinb, gq, gkv, wuq, wukv, *, name):
    S = proj.shape[0]
    ts = _tile(S, TS_ROW, SUB)
    hw = MLA_HEADS * LANE

    def body(cq_ref, ckv_ref, misc_ref, cos_ref, sa_ref, sb_ref, gq_ref, gkv_ref, wuq_ref, wukv_ref,
             q_ref, k_ref, v_ref):
        cosv, sav, sbv = cos_ref[...], sa_ref[...], sb_ref[...]
        cq = cq_ref[...]
        qn = cq * lax.rsqrt(jnp.mean(cq * cq, axis=-1, keepdims=True) + EPS) * gq_ref[...]
        qh = _dot(qn, wuq_ref[...])
        ckv = ckv_ref[...]
        kvn = ckv * lax.rsqrt(jnp.mean(ckv * ckv, axis=-1, keepdims=True) + EPS) * gkv_ref[...]
        kv = _dot(kvn, wukv_ref[...])
        kr = _rope(jnp.where(_rope_lanes((ts, LANE)), misc_ref[...], 0.0), cosv, sav, sbv)
        for h in range(MLA_HEADS):
            sl = slice(LANE * h, LANE * (h + 1))
            q_ref[:, sl] = _rope(qh[:, sl], cosv, sav, sbv).astype(_ACT)
            k_ref[:, sl] = (kv[:, sl] + kr).astype(_ACT)
        v_ref[...] = kv[:, hw:].astype(_ACT)

    oshape = jax.ShapeDtypeStruct((S, hw), _ACT)
    return pl.pallas_call(
        body, grid=(S // ts,),
        in_specs=[_row(ts, MLA_QR, C_CQ // MLA_QR), _row(ts, MLA_KVR, C_CKV // MLA_KVR), _row(ts, LANE, C_MISC // LANE),
                  _row(ts, LANE), _row(ts, LANE), _row(ts, LANE), _vec(MLA_QR), _vec(MLA_KVR),
                  _vec(hw, MLA_QR), _vec(2 * hw, MLA_KVR)],
        out_specs=(_row(ts, hw),) * 3, out_shape=(oshape,) * 3, compiler_params=_cp(1), name=name,
    )(proj, proj, proj, cosf, sina, sinb, gq, gkv, wuq, wukv)


def _mla_prep_bwd(proj, dq, dk, dv, dmisc_ssd, cosf, sina, sinb, gq, gkv, wuq, wukv, *, name):
    S = proj.shape[0]
    ts = _tile(S, TS_ROW, SUB)
    hw = MLA_HEADS * LANE

    def body(cq_ref, ckv_ref, dq_ref, dk_ref, dv_ref, dms_ref, cos_ref, sa_ref, sb_ref, gq_ref, gkv_ref,
             wuq_ref, wukv_ref, dcq_ref, dckv_ref, dmisc_ref, dqh_ref, dkv_ref, qn_ref, kvn_ref, dgq_ref, dgkv_ref):
        i = pl.program_id(0)
        cosv, sav, sbv = cos_ref[...], sa_ref[...], sb_ref[...]

        @pl.when(i == 0)
        def _():
            dgq_ref[...] = jnp.zeros_like(dgq_ref)
            dgkv_ref[...] = jnp.zeros_like(dgkv_ref)

        dqh = jnp.concatenate([_rope_t(dq_ref[:, LANE * h:LANE * (h + 1)], cosv, sav, sbv)
                               for h in range(MLA_HEADS)], axis=1)
        dqh_ref[...] = dqh.astype(_ACT)
        dkv = jnp.concatenate([dk_ref[...], dv_ref[...]], axis=1)
        dkv_ref[...] = dkv.astype(_ACT)

        def norm_bwd(x, g, dn, dg_ref, n_ref):
            rstd = lax.rsqrt(jnp.mean(x * x, axis=-1, keepdims=True) + EPS)
            xhat = x * rstd
            n_ref[...] = (xhat * g).astype(_ACT)
            dg_ref[...] += _colsum(dn * xhat)
            dxh = dn * g
            return rstd * (dxh - xhat * jnp.mean(dxh * xhat, axis=-1, keepdims=True))

        dcq_ref[...] = norm_bwd(cq_ref[...], gq_ref[...], _dot_nt(dqh, wuq_ref[...]), dgq_ref, qn_ref).astype(_ACT)
        dckv_ref[...] = norm_bwd(ckv_ref[...], gkv_ref[...], _dot_nt(dkv, wukv_ref[...]), dgkv_ref, kvn_ref).astype(_ACT)
        dks = dk_ref[:, 0:LANE]
        for h in range(1, MLA_HEADS):
            dks = dks + dk_ref[:, LANE * h:LANE * (h + 1)]
        rl = _rope_lanes((ts, LANE))
        dkr = _rope_t(jnp.where(rl, dks, 0.0), cosv, sav, sbv)
        dmisc_ref[...] = (dms_ref[...] + jnp.where(rl, dkr, 0.0)).astype(_ACT)

    act = lambda w: jax.ShapeDtypeStruct((S, w), _ACT)
    return pl.pallas_call(
        body, grid=(S // ts,),
        in_specs=[_row(ts, MLA_QR, C_CQ // MLA_QR), _row(ts, MLA_KVR, C_CKV // MLA_KVR),
                  _row(ts, hw), _row(ts, hw), _row(ts, hw), _row(ts, LANE),
                  _row(ts, LANE), _row(ts, LANE), _row(ts, LANE), _vec(MLA_QR), _vec(MLA_KVR),
                  _vec(hw, MLA_QR), _vec(2 * hw, MLA_KVR)],
        out_specs=(_row(ts, MLA_QR), _row(ts, MLA_KVR), _row(ts, LANE), _row(ts, hw), _row(ts, 2 * hw),
                   _row(ts, MLA_QR), _row(ts, MLA_KVR), _vec(MLA_QR), _vec(MLA_KVR)),
        out_shape=(act(MLA_QR), act(MLA_KVR), act(LANE), act(hw), act(2 * hw), act(MLA_QR), act(MLA_KVR),
                   jax.ShapeDtypeStruct((1, MLA_QR), F32), jax.ShapeDtypeStruct((1, MLA_KVR), F32)),
        compiler_params=_cp(1), name=name,
    )(proj, proj, dq, dk, dv, dmisc_ssd, cosf, sina, sinb, gq, gkv, wuq, wukv)


_MLA_SCALE = 1.0 / math.sqrt(MLA_NOPE + MLA_ROPE)


def _causal_scores(q, k, i, j, tq):
    s = _dot_nt(q, k) * _MLA_SCALE
    rows = i * tq + lax.broadcasted_iota(jnp.int32, (tq, tq), 0)
    cols = j * tq + lax.broadcasted_iota(jnp.int32, (tq, tq), 1)
    return jnp.where(cols <= rows, s, -jnp.inf)


def _attn_fwd(q, k, v, *, name):
    S = q.shape[0]
    tq = _tile(S, TQ_ATT)
    nq = S // tq

    def body(q_ref, k_ref, v_ref, o_ref, lse_ref, m_ref, l_ref, acc_ref):
        i, j = pl.program_id(1), pl.program_id(2)

        @pl.when(j == 0)
        def _():
            m_ref[...] = jnp.full_like(m_ref, -jnp.inf)
            l_ref[...] = jnp.zeros_like(l_ref)
            acc_ref[...] = jnp.zeros_like(acc_ref)

        @pl.when(j <= i)
        def _():
            s = _causal_scores(q_ref[...], k_ref[...], i, j, tq)
            m_prev = m_ref[...]
            m_new = jnp.maximum(m_prev, jnp.max(s, axis=1, keepdims=True))
            p = jnp.exp(s - m_new)
            alpha = jnp.exp(m_prev - m_new)
            l_ref[...] = alpha * l_ref[...] + _rowsum(p)
            acc_ref[...] = alpha * acc_ref[...] + _dot(p, v_ref[...])
            m_ref[...] = m_new

        @pl.when(j == nq - 1)
        def _():
            o_ref[...] = acc_ref[...] / l_ref[...]
            lse_ref[...] = jnp.broadcast_to(m_ref[...] + jnp.log(l_ref[...]), (tq, LANE))

    qspec = pl.BlockSpec((tq, LANE), lambda h, i, j: (i, h))
    kspec = pl.BlockSpec((tq, LANE), lambda h, i, j: (jnp.minimum(j, i), h))
    oshape = jax.ShapeDtypeStruct((S, MLA_HEADS * LANE), F32)
    return pl.pallas_call(
        body, grid=(MLA_HEADS, nq, nq), in_specs=[qspec, kspec, kspec], out_specs=(qspec, qspec),
        out_shape=(oshape, oshape),
        scratch_shapes=[pltpu.VMEM((tq, 1), F32), pltpu.VMEM((tq, 1), F32), pltpu.VMEM((tq, LANE), F32)],
        compiler_params=_cp(3), name=name)(q, k, v)


def _attn_bwd_dq(q, k, v, o, lse, dycat, *, name):
    S = q.shape[0]
    tq = _tile(S, TQ_ATT)
    nq = S // tq

    def body(q_ref, k_ref, v_ref, o_ref, lse_ref, do_ref, dq_ref, acc_ref):
        i, j = pl.program_id(1), pl.program_id(2)

        @pl.when(j == 0)
        def _():
            acc_ref[...] = jnp.zeros_like(acc_ref)

        @pl.when(j <= i)
        def _():
            kv = k_ref[...]
            p = jnp.exp(_causal_scores(q_ref[...], kv, i, j, tq) - lse_ref[:, 0:1])
            dov = do_ref[...]
            delta = _rowsum(dov * o_ref[...])
            ds = p * (_dot_nt(dov, v_ref[...]) - delta) * _MLA_SCALE
            acc_ref[...] += _dot(ds, kv)

        @pl.when(j == nq - 1)
        def _():
            dq_ref[...] = acc_ref[...]

    qspec = pl.BlockSpec((tq, LANE), lambda h, i, j: (i, h))
    kspec = pl.BlockSpec((tq, LANE), lambda h, i, j: (jnp.minimum(j, i), h))
    dospec = pl.BlockSpec((tq, LANE), lambda h, i, j: (i, SSD_INNER // LANE + h))
    return pl.pallas_call(
        body, grid=(MLA_HEADS, nq, nq), in_specs=[qspec, kspec, kspec, qspec, qspec, dospec], out_specs=qspec,
        out_shape=jax.ShapeDtypeStruct((S, MLA_HEADS * LANE), F32),
        scratch_shapes=[pltpu.VMEM((tq, LANE), F32)], compiler_params=_cp(3), name=name)(q, k, v, o, lse, dycat)


def _attn_bwd_dkv(q, k, v, o, lse, dycat, *, name):
    S = q.shape[0]
    tq = _tile(S, TQ_ATT)
    nq = S // tq

    def body(q_ref, k_ref, v_ref, o_ref, lse_ref, do_ref, dk_ref, dv_ref, dk_acc, dv_acc):
        j, i = pl.program_id(1), pl.program_id(2)

        @pl.when(i == 0)
        def _():
            dk_acc[...] = jnp.zeros_like(dk_acc)
            dv_acc[...] = jnp.zeros_like(dv_acc)

        @pl.when(i >= j)
        def _():
            qv = q_ref[...]
            p = jnp.exp(_causal_scores(qv, k_ref[...], i, j, tq) - lse_ref[:, 0:1])
            dov = do_ref[...]
            delta = _rowsum(dov * o_ref[...])
            dv_acc[...] += _dot_tn(p, dov)
            ds = p * (_dot_nt(dov, v_ref[...]) - delta) * _MLA_SCALE
            dk_acc[...] += _dot_tn(ds, qv)

        @pl.when(i == nq - 1)
        def _():
            dk_ref[...] = dk_acc[...]
            dv_ref[...] = dv_acc[...]

    qspec = pl.BlockSpec((tq, LANE), lambda h, j, i: (jnp.maximum(i, j), h))
    kspec = pl.BlockSpec((tq, LANE), lambda h, j, i: (j, h))
    dospec = pl.BlockSpec((tq, LANE), lambda h, j, i: (jnp.maximum(i, j), SSD_INNER // LANE + h))
    oshape = jax.ShapeDtypeStruct((S, MLA_HEADS * LANE), F32)
    return pl.pallas_call(
        body, grid=(MLA_HEADS, nq, nq), in_specs=[qspec, kspec, kspec, qspec, qspec, dospec],
        out_specs=(kspec, kspec), out_shape=(oshape, oshape),
        scratch_shapes=[pltpu.VMEM((tq, LANE), F32), pltpu.VMEM((tq, LANE), F32)],
        compiler_params=_cp(3), name=name)(q, k, v, o, lse, dycat)


_SWA_SCALE = 1.0 / math.sqrt(SWA_HD)
_SWA_KW = SWA_KV * LANE


def _swa_specs(S, ts, rev):
    n = S // ts
    t = (lambda i: n - 1 - i) if rev else (lambda i: i)
    hb = lambda i: jnp.maximum(t(i) * (ts // WINDOW) - 1, 0)
    return [
        pl.BlockSpec((ts, SWA_HEADS * LANE), lambda i: (t(i), C_SQ // (SWA_HEADS * LANE))),
        pl.BlockSpec((ts, _SWA_KW), lambda i: (t(i), C_SK // _SWA_KW)),
        pl.BlockSpec((WINDOW, _SWA_KW), lambda i: (hb(i), C_SK // _SWA_KW)),
        pl.BlockSpec((ts, _SWA_KW), lambda i: (t(i), C_SV // _SWA_KW)),
        pl.BlockSpec((WINDOW, _SWA_KW), lambda i: (hb(i), C_SV // _SWA_KW)),
    ]


def _swa_scores(qh, kk, t, b, ts):
    s = _dot_nt(qh, kk) * _SWA_SCALE
    row = lax.broadcasted_iota(jnp.int32, (WINDOW, 2 * WINDOW), 0)
    col = lax.broadcasted_iota(jnp.int32, (WINDOW, 2 * WINDOW), 1)
    rel = WINDOW + row - col
    kpos = t * ts + (b - 1) * WINDOW + col
    return jnp.where((rel >= 0) & (rel < WINDOW) & (kpos >= 0), s, -jnp.inf)


def _swa_fwd(proj, sinks, *, name):
    S = proj.shape[0]
    ts = _tile(S, TS_SWA)
    nb = ts // WINDOW

    def body(q_ref, k_ref, kh_ref, v_ref, vh_ref, sink_ref, o_ref, lse_ref):
        t = pl.program_id(0)
        kext = jnp.concatenate([kh_ref[...], k_ref[...]], axis=0)
        vext = jnp.concatenate([vh_ref[...], v_ref[...]], axis=0)
        for b in range(nb):
            rows = slice(WINDOW * b, WINDOW * (b + 1))
            for h in range(SWA_HEADS):
                kvl = slice(LANE * (h // (SWA_HEADS // SWA_KV)), LANE * (h // (SWA_HEADS // SWA_KV) + 1))
                hl = slice(LANE * h, LANE * (h + 1))
                kk = kext[WINDOW * b:WINDOW * (b + 2), kvl]
                vv = vext[WINDOW * b:WINDOW * (b + 2), kvl]
                s = _swa_scores(q_ref[rows, hl], kk, t, b, ts)
                sk = sink_ref[:, h:h + 1]
                m = jnp.maximum(jnp.max(s, axis=1, keepdims=True), sk)
                p = jnp.exp(s - m)
                den = _rowsum(p) + jnp.exp(sk - m)
                o_ref[rows, hl] = _dot(p, vv) / den
                lse_ref[rows, hl] = jnp.broadcast_to(m + jnp.log(den), (WINDOW, LANE))

    oshape = jax.ShapeDtypeStruct((S, SWA_HEADS * LANE), F32)
    ospec = pl.BlockSpec((ts, SWA_HEADS * LANE), lambda i: (i, 0))
    return pl.pallas_call(
        body, grid=(S // ts,), in_specs=_swa_specs(S, ts, False) + [_vec(LANE)], out_specs=(ospec, ospec),
        out_shape=(oshape, oshape), compiler_params=_cp(1), name=name)(proj, proj, proj, proj, proj, sinks)


def _swa_bwd(proj, o, lse, dycat, sinks, *, name):
    S = proj.shape[0]
    ts = _tile(S, TS_SWA)
    nb = ts // WINDOW
    n = S // ts
    grp = SWA_HEADS // SWA_KV

    def body(q_ref, k_ref, kh_ref, v_ref, vh_ref, o_ref, lse_ref, do_ref, sink_ref,
             dq_ref, dk_ref, dv_ref, dsink_ref, dk_carry, dv_carry):
        i = pl.program_id(0)
        t = n - 1 - i

        @pl.when(i == 0)
        def _():
            dk_carry[...] = jnp.zeros_like(dk_carry)
            dv_carry[...] = jnp.zeros_like(dv_carry)
            dsink_ref[...] = jnp.zeros_like(dsink_ref)

        kext = jnp.concatenate([kh_ref[...], k_ref[...]], axis=0)
        vext = jnp.concatenate([vh_ref[...], v_ref[...]], axis=0)
        lane1 = lax.broadcasted_iota(jnp.int32, (1, LANE), 1)
        dkb = [[jnp.zeros((WINDOW, LANE), F32) for _ in range(SWA_KV)] for _ in range(nb + 1)]
        dvb = [[jnp.zeros((WINDOW, LANE), F32) for _ in range(SWA_KV)] for _ in range(nb + 1)]
        dsink = jnp.zeros((1, LANE), F32)
        for b in range(nb):
            rows = slice(WINDOW * b, WINDOW * (b + 1))
            for h in range(SWA_HEADS):
                kvh = h // grp
                kvl = slice(LANE * kvh, LANE * (kvh + 1))
                hl = slice(LANE * h, LANE * (h + 1))
                kk = kext[WINDOW * b:WINDOW * (b + 2), kvl]
                vv = vext[WINDOW * b:WINDOW * (b + 2), kvl]
                qh = q_ref[rows, hl]
                lse_h = lse_ref[rows, LANE * h:LANE * h + 1]
                p = jnp.exp(_swa_scores(qh, kk, t, b, ts) - lse_h)
                doh = do_ref[rows, hl]
                delta = _rowsum(doh * o_ref[rows, hl])
                ds = p * (_dot_nt(doh, vv) - delta)
                sk = sink_ref[:, h:h + 1]
                dsink = dsink + jnp.where(lane1 == h, -jnp.sum(jnp.exp(sk - lse_h) * delta, keepdims=True), 0.0)
                dq_ref[rows, hl] = (_dot(ds, kk) * _SWA_SCALE).astype(_ACT)
                dkk = _dot_tn(ds, qh) * _SWA_SCALE
                dvv = _dot_tn(p, doh)
                dkb[b][kvh] = dkb[b][kvh] + dkk[:WINDOW]
                dkb[b + 1][kvh] = dkb[b + 1][kvh] + dkk[WINDOW:]
                dvb[b][kvh] = dvb[b][kvh] + dvv[:WINDOW]
                dvb[b + 1][kvh] = dvb[b + 1][kvh] + dvv[WINDOW:]
        dsink_ref[...] += dsink
        for dref, blocks, carry in ((dk_ref, dkb, dk_carry), (dv_ref, dvb, dv_carry)):
            old = carry[...]
            for b in range(1, nb + 1):
                blk = jnp.concatenate(blocks[b], axis=1)
                if b == nb:
                    blk = blk + old
                dref[WINDOW * (b - 1):WINDOW * b, :] = blk.astype(_ACT)
            carry[...] = jnp.concatenate(blocks[0], axis=1)

    hw = SWA_HEADS * LANE
    rev = lambda i: (n - 1 - i, 0)
    mix = lambda i: (n - 1 - i, (SSD_INNER + MLA_HEADS * LANE) // hw)
    return pl.pallas_call(
        body, grid=(n,),
        in_specs=_swa_specs(S, ts, True) + [pl.BlockSpec((ts, hw), rev), pl.BlockSpec((ts, hw), rev),
                                            pl.BlockSpec((ts, hw), mix), _vec(LANE)],
        out_specs=(pl.BlockSpec((ts, hw), rev), pl.BlockSpec((ts, _SWA_KW), rev), pl.BlockSpec((ts, _SWA_KW), rev),
                   _vec(LANE)),
        out_shape=(jax.ShapeDtypeStruct((S, hw), _ACT), jax.ShapeDtypeStruct((S, _SWA_KW), _ACT),
                   jax.ShapeDtypeStruct((S, _SWA_KW), _ACT), jax.ShapeDtypeStruct((1, LANE), F32)),
        scratch_shapes=[pltpu.VMEM((WINDOW, _SWA_KW), F32), pltpu.VMEM((WINDOW, _SWA_KW), F32)],
        compiler_params=_cp(1), name=name)(proj, proj, proj, proj, proj, o, lse, dycat, sinks)


def _exchange(arrays, *, scatter, name):
    n = len(arrays)

    def body(*refs):
        ins, outs = refs[:n], refs[n:2 * n]
        send_sems, recv_sems, loc_sems = refs[2 * n:]
        x, y, c = lax.axis_index("x"), lax.axis_index("y"), lax.axis_index("c")
        me = 4 * x + 2 * y + c

        def src(i, dest):
            return ins[i].at[dest] if scatter else ins[i]

        local = [pltpu.make_async_copy(src(i, me), outs[i].at[me], loc_sems.at[i]) for i in range(n)]
        for cp in local:
            cp.start()
        sends, recvs = [], []
        for k in range(1, NDEV):
            px = 1 - x if k & 4 else x
            py = 1 - y if k & 2 else y
            pc = 1 - c if k & 1 else c
            peer = 4 * px + 2 * py + pc
            for i in range(n):
                common = dict(send_sem=send_sems.at[i, k - 1], recv_sem=recv_sems.at[i, k - 1],
                              device_id=(px, py, pc), device_id_type=pl.DeviceIdType.MESH)
                sends.append(pltpu.make_async_remote_copy(src_ref=src(i, peer), dst_ref=outs[i].at[me], **common))
                recvs.append(pltpu.make_async_remote_copy(src_ref=src(i, peer), dst_ref=outs[i].at[peer], **common))
        for cp in sends:
            cp.start()
        for cp in recvs:
            cp.wait_recv()
        for cp in sends:
            cp.wait_send()
        for cp in local:
            cp.wait()

    hbm = pl.BlockSpec(memory_space=pl.ANY)
    out_shape = tuple(jax.ShapeDtypeStruct(a.shape if scatter else (NDEV,) + a.shape, a.dtype) for a in arrays)
    return pl.pallas_call(
        body, in_specs=[hbm] * n, out_specs=tuple([hbm] * n), out_shape=out_shape,
        scratch_shapes=[pltpu.SemaphoreType.DMA((n, NDEV - 1)), pltpu.SemaphoreType.DMA((n, NDEV - 1)),
                        pltpu.SemaphoreType.DMA((n,))],
        name=name)(*arrays)


def _adamw(w, m, v, parts, *, name):
    R, C = w.shape
    npart = parts.shape[0]
    cap = max(SUB, ((1 << 18) // C) // SUB * SUB)
    tr = _tile(R, cap, SUB)

    def body(w_ref, m_ref, v_ref, p_ref, g_ref, d_ref, mo_ref, vo_ref):
        g = p_ref[0]
        for k in range(1, npart):
            g = g + p_ref[k]
        mn = ADAM_B1 * m_ref[...] + (1.0 - ADAM_B1) * g
        vn = ADAM_B2 * v_ref[...] + (1.0 - ADAM_B2) * (g * g)
        m_hat = mn / (1.0 - ADAM_B1 ** ADAM_STEP)
        v_hat = vn / (1.0 - ADAM_B2 ** ADAM_STEP)
        g_ref[...] = g
        d_ref[...] = -ADAM_LR * (m_hat / (jnp.sqrt(v_hat) + ADAM_EPS) + ADAM_WD * w_ref[...])
        mo_ref[...] = mn
        vo_ref[...] = vn

    spec = pl.BlockSpec((tr, C), lambda i: (i, 0))
    oshape = jax.ShapeDtypeStruct((R, C), F32)
    return pl.pallas_call(
        body, grid=(R // tr,), in_specs=[spec] * 3 + [pl.BlockSpec((npart, tr, C), lambda i: (0, i, 0))],
        out_specs=(spec,) * 4, out_shape=(oshape,) * 4, compiler_params=_cp(1), name=name)(w, m, v, parts)


def _adamw_layer(l, w, m, v, parts, prev, *, name):
    L, R, C = w.shape
    npart = parts.shape[0]
    cap = max(SUB, ((1 << 18) // C) // SUB * SUB)
    tr = _tile(R, cap, SUB)
    nprev = 0 if prev is None else 4

    def body(*refs):
        w_ref, m_ref, v_ref, p_ref = refs[:4]
        g_ref, d_ref, mo_ref, vo_ref = refs[4 + nprev:]
        g = p_ref[0]
        for k in range(1, npart):
            g = g + p_ref[k]
        mn = ADAM_B1 * m_ref[...] + (1.0 - ADAM_B1) * g
        vn = ADAM_B2 * v_ref[...] + (1.0 - ADAM_B2) * (g * g)
        m_hat = mn / (1.0 - ADAM_B1 ** ADAM_STEP)
        v_hat = vn / (1.0 - ADAM_B2 ** ADAM_STEP)
        g_ref[...] = g
        d_ref[...] = -ADAM_LR * (m_hat / (jnp.sqrt(v_hat) + ADAM_EPS) + ADAM_WD * w_ref[...])
        mo_ref[...] = mn
        vo_ref[...] = vn

    spec = pl.BlockSpec((None, tr, C), lambda i: (l, i, 0))
    oshape = jax.ShapeDtypeStruct((L, R, C), F32)
    return pl.pallas_call(
        body, grid=(R // tr,),
        in_specs=[spec] * 3 + [pl.BlockSpec((npart, tr, C), lambda i: (0, i, 0))]
        + [pl.BlockSpec(memory_space=pl.ANY)] * nprev,
        out_specs=(spec,) * 4, out_shape=(oshape,) * 4,
        input_output_aliases={4 + k: k for k in range(nprev)},
        compiler_params=_cp(1), name=name)(w, m, v, parts, *(prev or ()))


_HBM = pl.BlockSpec(memory_space=pltpu.HBM)
_SEM = pl.BlockSpec(memory_space=pltpu.SEMAPHORE)
_EFFECT = pltpu.SideEffectType.DATAFLOW_SIDE_EFFECTING


def _peers():
    x, y, c = lax.axis_index("x"), lax.axis_index("y"), lax.axis_index("c")
    out = []
    for k in range(1, NDEV):
        px = 1 - x if k & 4 else x
        py = 1 - y if k & 2 else y
        pc = 1 - c if k & 1 else c
        out.append((k - 1, (px, py, pc), 4 * px + 2 * py + pc))
    return 4 * x + 2 * y + c, out


def _xchg_start(arrays, *, scatter, name):
    n = len(arrays)
    lands = [lax.empty(a.shape if scatter else (NDEV,) + a.shape, a.dtype) for a in arrays]

    def body(*refs):
        ins, lnd = refs[:n], refs[n:2 * n]
        send_sems, recv_sems = refs[2 * n], refs[2 * n + 1]
        token = refs[-1]
        me, peers = _peers()
        for k, dev, peer in peers:
            for i in range(n):
                pltpu.make_async_remote_copy(
                    src_ref=ins[i].at[peer] if scatter else ins[i], dst_ref=lnd[i].at[me],
                    send_sem=send_sems.at[i * (NDEV - 1) + k], recv_sem=recv_sems.at[i * (NDEV - 1) + k],
                    device_id=dev, device_id_type=pl.DeviceIdType.MESH).start()
        token[...] = jnp.zeros_like(token)

    sems = pltpu.SemaphoreType.DMA((n * (NDEV - 1),))
    res = pl.pallas_call(
        body, name=name,
        out_shape=(sems, sems) + tuple(pltpu.HBM(t.shape, t.dtype) for t in list(arrays) + lands)
        + (jax.ShapeDtypeStruct((SUB, LANE), F32),),
        in_specs=[_HBM] * (2 * n), out_specs=(_SEM, _SEM) + (_HBM,) * (2 * n) + (pl.BlockSpec(memory_space=pltpu.VMEM),),
        input_output_aliases={i: 2 + i for i in range(2 * n)},
        compiler_params=pltpu.CompilerParams(has_side_effects=_EFFECT),
    )(*[pltpu.with_memory_space_constraint(t, pltpu.HBM) for t in list(arrays) + lands])
    return dict(send=res[0], recv=res[1], thru=list(res[2:2 + 2 * n]), token=res[-1], scatter=scatter, n=n)


def _xchg_wait(handle, after, *, name):
    n, scatter = handle["n"], handle["scatter"]
    thru = handle["thru"]

    def body(*refs):
        ins, lnd = refs[:n], refs[n:2 * n]
        send_sems, recv_sems = refs[2 * n], refs[2 * n + 1]
        me, peers = _peers()
        for k, dev, peer in peers:
            for i in range(n):
                cp = pltpu.make_async_remote_copy(
                    src_ref=ins[i].at[peer] if scatter else ins[i], dst_ref=lnd[i].at[peer],
                    send_sem=send_sems.at[i * (NDEV - 1) + k], recv_sem=recv_sems.at[i * (NDEV - 1) + k],
                    device_id=dev, device_id_type=pl.DeviceIdType.MESH)
                cp.wait_send()
                cp.wait_recv()

    res = pl.pallas_call(
        body, name=name, out_shape=tuple(pltpu.HBM(t.shape, t.dtype) for t in thru),
        in_specs=[_HBM] * (2 * n) + [_SEM, _SEM, pl.BlockSpec(memory_space=pl.ANY)], out_specs=(_HBM,) * (2 * n),
        input_output_aliases={i: i for i in range(2 * n)},
        compiler_params=pltpu.CompilerParams(has_side_effects=_EFFECT),
    )(*thru, handle["send"], handle["recv"], after)
    return list(res[n:])


def _pad_heads(w, nh, hd, axis=-1):
    axis = axis % w.ndim
    shp = w.shape
    w = w.reshape(shp[:axis] + (nh, hd) + shp[axis + 1:])
    pads = [(0, 0)] * w.ndim
    pads[axis + 1] = (0, LANE - hd)
    return jnp.pad(w, pads).reshape(shp[:axis] + (nh * LANE,) + shp[axis + 1:])


def _unpad_heads(w, nh, hd, axis=-1):
    axis = axis % w.ndim
    shp = w.shape
    w = w.reshape(shp[:axis] + (nh, LANE) + shp[axis + 1:])
    w = lax.slice_in_dim(w, 0, hd, axis=axis + 1)
    return w.reshape(shp[:axis] + (nh * hd,) + shp[axis + 1:])


_O_DT = SSD_INNER + SSD_XBC
_O_CQ = _O_DT + SSD_HEADS
_O_CKV = _O_CQ + MLA_QR
_O_KR = _O_CKV + MLA_KVR
_O_SQ = _O_KR + MLA_ROPE
_O_SK = _O_SQ + SWA_HEADS * SWA_HD
_O_SV = _O_SK + SWA_KV * SWA_HD


def _w_in_to_padded(w):
    z, xbc, dt = w[..., :SSD_INNER], w[..., SSD_INNER:_O_DT], w[..., _O_DT:_O_CQ]
    cq, ckv, kr = w[..., _O_CQ:_O_CKV], w[..., _O_CKV:_O_KR], w[..., _O_KR:_O_SQ]
    sq, sk, sv = w[..., _O_SQ:_O_SK], w[..., _O_SK:_O_SV], w[..., _O_SV:]
    zeros = lambda n: jnp.zeros(w.shape[:-1] + (n,), w.dtype)
    misc = jnp.concatenate([dt, zeros(ROPE_LANE - SSD_HEADS), kr, zeros(LANE - ROPE_LANE - MLA_ROPE)], axis=-1)
    return jnp.concatenate([xbc, z, cq, ckv, misc, _pad_heads(sq, SWA_HEADS, SWA_HD),
                            _pad_heads(sk, SWA_KV, SWA_HD), _pad_heads(sv, SWA_KV, SWA_HD)], axis=-1)


def _w_in_from_padded(g):
    xbc, z, cq, ckv = g[..., C_XBC:C_Z], g[..., C_Z:C_CQ], g[..., C_CQ:C_CKV], g[..., C_CKV:C_MISC]
    dt, kr = g[..., C_MISC:C_MISC + SSD_HEADS], g[..., C_MISC + ROPE_LANE:C_MISC + ROPE_LANE + MLA_ROPE]
    sq = _unpad_heads(g[..., C_SQ:C_SK], SWA_HEADS, SWA_HD)
    sk = _unpad_heads(g[..., C_SK:C_SV], SWA_KV, SWA_HD)
    sv = _unpad_heads(g[..., C_SV:], SWA_KV, SWA_HD)
    return jnp.concatenate([z, xbc, dt, cq, ckv, kr, sq, sk, sv], axis=-1)


def _w_out_to_padded(w):
    a = SSD_INNER
    b = a + MLA_HEADS * MLA_V
    return jnp.concatenate([w[..., :a, :], _pad_heads(w[..., a:b, :], MLA_HEADS, MLA_V, axis=-2),
                            _pad_heads(w[..., b:, :], SWA_HEADS, SWA_HD, axis=-2)], axis=-2)


def _w_out_from_padded(g):
    a = SSD_INNER
    b = a + MLA_HEADS * LANE
    return jnp.concatenate([g[..., :a, :], _unpad_heads(g[..., a:b, :], MLA_HEADS, MLA_V, axis=-2),
                            _unpad_heads(g[..., b:, :], SWA_HEADS, SWA_HD, axis=-2)], axis=-2)


def _w_ukv_to_padded(w):
    w4 = w.reshape(w.shape[:-1] + (MLA_HEADS, MLA_NOPE + MLA_V))
    flat = lambda t: t.reshape(w.shape[:-1] + (MLA_HEADS * t.shape[-1],))
    return jnp.concatenate([_pad_heads(flat(w4[..., :MLA_NOPE]), MLA_HEADS, MLA_NOPE),
                            _pad_heads(flat(w4[..., MLA_NOPE:]), MLA_HEADS, MLA_V)], axis=-1)


def _w_ukv_from_padded(g):
    hw = MLA_HEADS * LANE
    gk = _unpad_heads(g[..., :hw], MLA_HEADS, MLA_NOPE).reshape(g.shape[:-1] + (MLA_HEADS, MLA_NOPE))
    gv = _unpad_heads(g[..., hw:], MLA_HEADS, MLA_V).reshape(g.shape[:-1] + (MLA_HEADS, MLA_V))
    return jnp.concatenate([gk, gv], axis=-1).reshape(g.shape[:-1] + (MLA_HEADS * (MLA_NOPE + MLA_V),))


def _pad_lane(v):
    return jnp.pad(v, [(0, 0)] * (v.ndim - 1) + [(0, LANE - v.shape[-1])])


def _rope_tables(positions):
    inv_freq = ROPE_THETA ** (-jnp.arange(0, MLA_ROPE, 2, dtype=F32) / MLA_ROPE)
    ang = positions.astype(F32).reshape(-1, 1) * inv_freq
    cos, sin = jnp.cos(ang), jnp.sin(ang)
    S = ang.shape[0]
    one, zero = jnp.ones((S, ROPE_LANE), F32), jnp.zeros((S, ROPE_LANE), F32)
    tail1, tail0 = jnp.ones((S, LANE - ROPE_LANE - MLA_ROPE), F32), jnp.zeros((S, LANE - ROPE_LANE - MLA_ROPE), F32)
    z16 = jnp.zeros_like(sin)
    return (jnp.concatenate([one, cos, cos, tail1], axis=1), jnp.concatenate([zero, -sin, z16, tail0], axis=1),
            jnp.concatenate([zero, z16, sin, tail0], axis=1))


def _layer_fwd(l, x_in, f_prev, gate_prev, mod, P, tabs):
    sh1, sc1, g1, sh2, sc2, g2 = [mod[k:k + 1] for k in range(6)]
    tag = f"l{l}_"
    if f_prev is None:
        x0 = x_in
        h1 = _norm_fwd(x0, P["n1g"], sc1, sh1, name=tag + "norm1")
    else:
        x0, h1 = _norm_fwd(x_in, P["n1g"], sc1, sh1, f=f_prev, gate=gate_prev, name=tag + "norm1")
    proj = _mm(h1, P["w_in"], name=tag + "proj")
    y_ssd, hin = _ssd_fwd(proj, P["ssd_cw"], P["ssd_cb"], P["dtb"], P["alog"], P["dsk"],
                          P["ssd_ng"], name=tag + "ssd")
    q, k, v = _mla_prep_fwd(proj, *tabs, P["gq"], P["gkv"], P["w_uq"], P["w_ukv"], name=tag + "mla_prep")
    o_mla, lse_mla = _attn_fwd(q, k, v, name=tag + "mla_attn")
    o_swa, lse_swa = _swa_fwd(proj, P["sinks"], name=tag + "swa")
    ycat = jnp.concatenate([y_ssd.astype(_ACT), o_mla.astype(_ACT), o_swa.astype(_ACT)], axis=1)
    y = _mm(ycat, P["w_out"], name=tag + "out")
    x1, h2 = _norm_fwd(x0, P["n2g"], sc2, sh2, f=y, gate=g1, name=tag + "norm2")
    up = _mm(h2, P["w_up"], name=tag + "up")
    act = _ffn_act_fwd(up, P["fcw"], P["fcb"], name=tag + "ffn_act")
    f = _mm(act, P["w_down"], name=tag + "down")
    saved = dict(x0=x0, h1=h1, proj=proj, hin=hin, q=q, k=k, v=v, o_mla=o_mla, lse_mla=lse_mla, o_swa=o_swa,
                 lse_swa=lse_swa, ycat=ycat, y=y, x1=x1, h2=h2, up=up, act=act, f=f, mod=mod)
    return x1, f, g2, saved


def _layer_bwd(l, dxo, sv, P, tabs):
    mod = sv["mod"]
    sh1, sc1, g1, sh2, sc2, g2 = [mod[k:k + 1] for k in range(6)]
    tag = f"l{l}_b_"
    G = {}
    df, dg2 = _gate_bwd(dxo, sv["f"], g2, name=tag + "gate2")
    dact = _mm(df, P["w_down"], tb=True, name=tag + "dact")
    G["w_down"] = _mm(sv["act"], df, ta=True, name=tag + "dw_down")
    du = _ffn_act_bwd(sv["up"], dact, P["fcw"], P["fcb"], name=tag + "ffn_act")
    dup, G["fcw"], G["fcb"] = _ffn_conv_bwd(du, sv["up"], P["fcw"], name=tag + "ffn_conv")
    dh2 = _mm(dup, P["w_up"], tb=True, name=tag + "dh2")
    G["w_up"] = _mm(sv["h2"], dup, ta=True, name=tag + "dw_up")
    dx1, G["n2g"], dsc2, dsh2 = _norm_bwd(dh2, sv["x1"], dxo, P["n2g"], sc2, name=tag + "norm2")
    dy, dg1 = _gate_bwd(dx1, sv["y"], g1, name=tag + "gate1")
    dycat = _mm(dy, P["w_out"], tb=True, name=tag + "dycat")
    G["w_out"] = _mm(sv["ycat"], dy, ta=True, name=tag + "dw_out")
    proj = sv["proj"]
    (dpre, dz, dmisc_ssd, G["ssd_cw"], G["ssd_cb"], G["dtb"], G["alog"], G["dsk"], G["ssd_ng"]) = _ssd_bwd(
        proj, dycat, sv["hin"], P["ssd_cw"], P["ssd_cb"], P["dtb"], P["alog"], P["dsk"],
        P["ssd_ng"], name=tag + "ssd")
    att = (sv["q"], sv["k"], sv["v"], sv["o_mla"], sv["lse_mla"], dycat)
    dq = _attn_bwd_dq(*att, name=tag + "mla_dq")
    dk, dv = _attn_bwd_dkv(*att, name=tag + "mla_dkv")
    dcq, dckv, dmisc, dqh, dkv, qn, kvn, G["gq"], G["gkv"] = _mla_prep_bwd(
        proj, dq, dk, dv, dmisc_ssd, *tabs, P["gq"], P["gkv"], P["w_uq"], P["w_ukv"], name=tag + "mla_prep")
    G["w_uq"] = _mm(qn, dqh, ta=True, name=tag + "dw_uq")
    G["w_ukv"] = _mm(kvn, dkv, ta=True, name=tag + "dw_ukv")
    dsq, dsk_, dsv_, G["sinks"] = _swa_bwd(proj, sv["o_swa"], sv["lse_swa"], dycat, P["sinks"], name=tag + "swa")
    dproj = jnp.concatenate([dpre, dz, dcq, dckv, dmisc, dsq, dsk_, dsv_], axis=1)
    dh1 = _mm(dproj, P["w_in"], tb=True, name=tag + "dh1")
    G["w_in"] = _mm(sv["h1"], dproj, ta=True, name=tag + "dw_in")
    dx0, G["n1g"], dsc1, dsh1 = _norm_bwd(dh1, sv["x0"], dx1, P["n1g"], sc1, name=tag + "norm1")
    G["mod"] = jnp.concatenate([dsh1, dsc1, dg1, dsh2, dsc2, dg2], axis=0)
    return dx0, G


def _local_step(x, tgt, mods, get_params, tabs, final_g, on_grads):
    saved, params = [], []
    xin, f, gate = x, None, None
    for l in range(DEPTH):
        params.append(get_params(l, x if f is None else f))
        xin, f, gate, sv = _layer_fwd(l, xin, f, gate, mods[l], params[l], tabs)
        saved.append(sv)
    loss, dx, dfinal = _final_loss(xin, f, gate, final_g, tgt, name="final_loss")
    for l in reversed(range(DEPTH)):
        dx, G = _layer_bwd(l, dx, saved[l], params[l], tabs)
        on_grads(l, G)
    return loss[0, 0], dx, dfinal


_WEIGHTS = ['ada_w', 'ada_b', 'norm1_g', 'norm2_g', 'w_in', 'ssd_conv_w', 'ssd_conv_b', 'ssd_dt_bias', 'ssd_a_log',
            'ssd_d', 'ssd_norm_g', 'mla_q_norm_g', 'mla_w_uq', 'mla_kv_norm_g', 'mla_w_ukv', 'swa_sinks', 'w_out',
            'ffn_w_up', 'ffn_conv_w', 'ffn_conv_b', 'ffn_w_down', 'final_norm_g']
_INPUTS = ['x', 'c', 'positions'] + _WEIGHTS + ['loss_target'] + ['m_' + n for n in _WEIGHTS] + ['v_' + n for n in _WEIGHTS]
_SMALL = [('ada_b', 'mod'), ('norm1_g', 'n1g'), ('norm2_g', 'n2g'), ('ssd_conv_b', 'ssd_cb'), ('ssd_dt_bias', 'dtb'),
          ('ssd_a_log', 'alog'), ('ssd_d', 'dsk'), ('ssd_norm_g', 'ssd_ng'), ('mla_q_norm_g', 'gq'),
          ('mla_kv_norm_g', 'gkv'), ('swa_sinks', 'sinks'), ('ffn_conv_b', 'fcb')]
_SHARDED = [('w_in', 'w_in', 2), ('ssd_conv_w', 'ssd_cw', 2), ('mla_w_uq', 'w_uq', 2), ('mla_w_ukv', 'w_ukv', 2),
            ('w_out', 'w_out', 1), ('ffn_w_up', 'w_up', 2), ('ffn_conv_w', 'fcw', 2), ('ffn_w_down', 'w_down', 1)]


def _pack_small(per_layer, final):
    parts = []
    for name, _ in _SMALL:
        v = per_layer[name]
        v = v.reshape(DEPTH, -1)
        pad = (-v.shape[1]) % LANE
        parts.append(jnp.pad(v, ((0, 0), (0, pad))).reshape(-1))
    parts.append(final.reshape(-1))
    return jnp.concatenate(parts).reshape(-1, LANE)


def _unpack_small(packed, shapes):
    flat = packed.reshape(-1)
    out, off = {}, 0
    for name, _ in _SMALL:
        n = math.prod(shapes[name][1:])
        npad = n + (-n) % LANE
        out[name] = flat[off:off + DEPTH * npad].reshape(DEPTH, npad)[:, :n].reshape(shapes[name])
        off += DEPTH * npad
    out['final_norm_g'] = flat[off:off + D]
    return out


def _shard_major(g, axis):
    shp = g.shape
    g = g.reshape(shp[:axis] + (NDEV, shp[axis] // NDEV) + shp[axis + 1:])
    return jnp.moveaxis(g, axis, 0)


def _unshard(g, axis):
    g = jnp.moveaxis(g, 0, axis)
    shp = g.shape
    return g.reshape(shp[:axis] + (shp[axis] * shp[axis + 1],) + shp[axis + 2:])


def kernel(x, c, positions, ada_w, ada_b, norm1_g, norm2_g, w_in, ssd_conv_w, ssd_conv_b, ssd_dt_bias, ssd_a_log, ssd_d, ssd_norm_g, mla_q_norm_g, mla_w_uq, mla_kv_norm_g, mla_w_ukv, swa_sinks, w_out, ffn_w_up, ffn_conv_w, ffn_conv_b, ffn_w_down, final_norm_g, loss_target, m_ada_w, m_ada_b, m_norm1_g, m_norm2_g, m_w_in, m_ssd_conv_w, m_ssd_conv_b, m_ssd_dt_bias, m_ssd_a_log, m_ssd_d, m_ssd_norm_g, m_mla_q_norm_g, m_mla_w_uq, m_mla_kv_norm_g, m_mla_w_ukv, m_swa_sinks, m_w_out, m_ffn_w_up, m_ffn_conv_w, m_ffn_conv_b, m_ffn_w_down, m_final_norm_g, v_ada_w, v_ada_b, v_norm1_g, v_norm2_g, v_w_in, v_ssd_conv_w, v_ssd_conv_b, v_ssd_dt_bias, v_ssd_a_log, v_ssd_d, v_ssd_norm_g, v_mla_q_norm_g, v_mla_w_uq, v_mla_kv_norm_g, v_mla_w_ukv, v_swa_sinks, v_w_out, v_ffn_w_up, v_ffn_conv_w, v_ffn_conv_b, v_ffn_w_down, v_final_norm_g):
    a = dict(zip(_INPUTS, (x, c, positions, ada_w, ada_b, norm1_g, norm2_g, w_in, ssd_conv_w, ssd_conv_b, ssd_dt_bias, ssd_a_log, ssd_d, ssd_norm_g, mla_q_norm_g, mla_w_uq, mla_kv_norm_g, mla_w_ukv, swa_sinks, w_out, ffn_w_up, ffn_conv_w, ffn_conv_b, ffn_w_down, final_norm_g, loss_target, m_ada_w, m_ada_b, m_norm1_g, m_norm2_g, m_w_in, m_ssd_conv_w, m_ssd_conv_b, m_ssd_dt_bias, m_ssd_a_log, m_ssd_d, m_ssd_norm_g, m_mla_q_norm_g, m_mla_w_uq, m_mla_kv_norm_g, m_mla_w_ukv, m_swa_sinks, m_w_out, m_ffn_w_up, m_ffn_conv_w, m_ffn_conv_b, m_ffn_w_down, m_final_norm_g, v_ada_w, v_ada_b, v_norm1_g, v_norm2_g, v_w_in, v_ssd_conv_w, v_ssd_conv_b, v_ssd_dt_bias, v_ssd_a_log, v_ssd_d, v_ssd_norm_g, v_mla_q_norm_g, v_mla_w_uq, v_mla_kv_norm_g, v_mla_w_ukv, v_swa_sinks, v_w_out, v_ffn_w_up, v_ffn_conv_w, v_ffn_conv_b, v_ffn_w_down, v_final_norm_g)))
    axes = ("x", "y", "c")
    me = 4 * lax.axis_index("x") + 2 * lax.axis_index("y") + lax.axis_index("c")
    ncol = ada_w.shape[-1]

    mxu_names = ('w_in', 'mla_w_uq', 'mla_w_ukv', 'w_out', 'ffn_w_up', 'ffn_w_down')
    own = [[a[n][l].astype(_MXU) if n in mxu_names else a[n][l] for n, _, _ in _SHARDED] for l in range(DEPTH)]
    gathers = [_xchg_start(own[l], scatter=False, name=f"gather_start{l}") for l in range(DEPTH)]

    c_all = _exchange([c], scatter=False, name="gather_c")[0]
    c_act = _silu_call(c_all.reshape(NDEV, D), name="c_act")
    mod_part = jnp.stack([_mm(c_act, ada_w[l], name=f"mod{l}") for l in range(DEPTH)])
    mod_all = _exchange([mod_part], scatter=False, name="gather_mod")[0]
    mod_mine = lax.dynamic_index_in_dim(mod_all, me, axis=2, keepdims=False)
    mods = (jnp.moveaxis(mod_mine, 0, 1).reshape(DEPTH, 6 * D) + ada_b).reshape(DEPTH, 6, D)
    tabs = _rope_tables(positions)

    def place_own(landed, mine):
        return [lax.dynamic_update_index_in_dim(t, o, me, 0) for t, o in zip(landed, mine)]

    def get_params(l, after):
        landed = place_own(_xchg_wait(gathers[l], after, name=f"gather_wait{l}"), own[l])
        full = {n: _unshard(g, ax - 1) for (n, _, ax), g in zip(_SHARDED, landed)}
        vec = lambda t: t[l].reshape(1, -1)
        return dict(
            w_in=_w_in_to_padded(full['w_in']), w_out=_w_out_to_padded(full['w_out']), w_up=full['ffn_w_up'],
            w_down=full['ffn_w_down'], w_uq=_pad_heads(full['mla_w_uq'], MLA_HEADS, MLA_NOPE + MLA_ROPE),
            w_ukv=_w_ukv_to_padded(full['mla_w_ukv']), ssd_cw=full['ssd_conv_w'], fcw=full['ffn_conv_w'],
            ssd_cb=vec(ssd_conv_b), dtb=vec(_pad_lane(ssd_dt_bias)), alog=vec(_pad_lane(ssd_a_log)),
            dsk=vec(_pad_lane(ssd_d)), ssd_ng=vec(ssd_norm_g), gq=vec(mla_q_norm_g), gkv=vec(mla_kv_norm_g),
            sinks=vec(_pad_lane(swa_sinks)), fcb=vec(ffn_conv_b), n1g=vec(norm1_g), n2g=vec(norm2_g))

    unpad = dict(w_in=_w_in_from_padded, w_out=_w_out_from_padded, w_ukv=_w_ukv_from_padded,
                 w_uq=lambda g: _unpad_heads(g, MLA_HEADS, MLA_NOPE + MLA_ROPE))
    grads, sent, scatters = [None] * DEPTH, [None] * DEPTH, [None] * DEPTH

    def on_grads(l, G):
        grads[l] = G
        sent[l] = [_shard_major(unpad.get(key, lambda g: g)(G[key]), ax - 1) for _, key, ax in _SHARDED]
        scatters[l] = _xchg_start(sent[l], scatter=True, name=f"scatter_start{l}")

    loss, dx, dfinal = _local_step(x[0], loss_target[0], mods, get_params, tabs, final_norm_g.reshape(1, D), on_grads)
    loss = lax.psum(loss, axes)

    stack = lambda key: jnp.stack([grads[l][key] for l in range(DEPTH)])
    small_g = {name: stack(key).reshape(DEPTH, -1) for name, key in _SMALL}
    small_parts = _exchange([_pack_small(small_g, dfinal)], scatter=False, name="gather_small")[0]

    out_g, out_d, out_m, out_v = {}, {}, {}, {}
    chain = {name: None for name, _, _ in _SHARDED}
    for l in reversed(range(DEPTH)):
        landed = _xchg_wait(scatters[l], dx, name=f"scatter_wait{l}")
        parts = place_own(landed, [lax.dynamic_index_in_dim(t, me, 0, keepdims=False) for t in sent[l]])
        for (name, _, _), pv in zip(_SHARDED, parts):
            chain[name] = _adamw_layer(l, a[name], a['m_' + name], a['v_' + name], pv, chain[name],
                                       name=f"adamw_{name}{l}")
    for name, _, _ in _SHARDED:
        out_g[name], out_d[name], out_m[name], out_v[name] = chain[name]

    def update(name, wv, mv, vv, pv):
        shp = wv.shape
        r = lambda t: t.reshape((-1, shp[-1]))
        res = _adamw(r(wv), r(mv), r(vv), pv.reshape((pv.shape[0], -1, shp[-1])), name="adamw_" + name)
        out_g[name], out_d[name], out_m[name], out_v[name] = [t.reshape(shp) for t in res]

    n_ada = DEPTH * 6 * D // LANE
    dmod_all = small_parts[:, :n_ada].reshape(NDEV, DEPTH, 6 * D)
    dmod_mine = lax.dynamic_slice_in_dim(dmod_all, me * ncol, ncol, axis=2)
    g_ada = jnp.stack([_mm(c_act, dmod_mine[:, l], ta=True, name=f"dw_ada{l}") for l in range(DEPTH)])
    update('ada_w', ada_w, m_ada_w, v_ada_w, g_ada[None])
    shapes = {n: a[n].shape for n, _ in _SMALL}
    pk = lambda pre: _pack_small({n: a[pre + n] for n, _ in _SMALL}, a[pre + 'final_norm_g'])
    res = _adamw(pk(''), pk('m_'), pk('v_'), small_parts, name="adamw_small")
    for dst, t in zip((out_g, out_d, out_m, out_v), res):
        dst.update(_unpack_small(t, shapes))

    outs = [loss, dx[None]]
    for dct in (out_g, out_d, out_m, out_v):
        outs += [dct[n] for n in _WEIGHTS]
    return tuple(outs)
```

```python
import functools
import math

import jax
import jax.numpy as jnp
from jax import lax
from jax.experimental import pallas as pl
from jax.experimental.pallas import tpu as pltpu

F32 = jnp.float32
_MXU = jnp.bfloat16
_ACT = jnp.bfloat16
_HI = lax.Precision.HIGHEST
EPS = 1e-6
NDEV = 8
DEPTH = 4
D = 1024
LANE = 128
SUB = 8
VMEM_LIMIT = 56 * 1024 * 1024

SSD_INNER, SSD_STATE, SSD_HEADS, SSD_GROUPS, SSD_CHUNK, SSD_CONV = 512, 128, 8, 2, 128, 4
SSD_XBC = SSD_INNER + 2 * SSD_GROUPS * SSD_STATE
MLA_HEADS, MLA_NOPE, MLA_ROPE, MLA_V, MLA_QR, MLA_KVR = 4, 64, 32, 64, 256, 128
SWA_HEADS, SWA_KV, SWA_HD, WINDOW = 4, 2, 64, 128
D_FF, FFN_CONV = 2816, 3
D_IN = 2472
ROPE_THETA = 10000.0
C_XBC, C_Z, C_CQ, C_CKV, C_MISC, C_SQ, C_SK, C_SV, D_INP = 0, 1024, 1536, 1792, 1920, 2048, 2560, 2816, 3072
ROPE_LANE = 64
D_MIXP = 1536

ADAM_LR, ADAM_B1, ADAM_B2, ADAM_EPS, ADAM_WD, ADAM_STEP = 0.001, 0.9, 0.999, 1e-08, 0.01, 10

TS_ROW = 512
TS_FFN = 256
TQ_ATT = 512
TS_SWA = 512


def _tile(n, cap, q=LANE):
    best = None
    for t in range(q, min(n, cap) + 1, q):
        if n % t == 0:
            best = t
    return n if best is None else best


def _cp(ngrid):
    return pltpu.CompilerParams(dimension_semantics=("arbitrary",) * ngrid, vmem_limit_bytes=VMEM_LIMIT)


def _dot(a, b):
    return jnp.dot(a.astype(_MXU), b.astype(_MXU), preferred_element_type=F32)


def _dot_nt(a, b):
    return lax.dot_general(a.astype(_MXU), b.astype(_MXU), (((1,), (1,)), ((), ())), preferred_element_type=F32)


def _dot_tn(a, b):
    return jnp.dot(a.T.astype(_MXU), b.astype(_MXU), preferred_element_type=F32)


def _sigmoid(x):
    return 1.0 / (1.0 + jnp.exp(-x))


def _silu(x):
    return x * _sigmoid(x)


def _dsilu(x):
    s = _sigmoid(x)
    return s * (1.0 + x * (1.0 - s))


def _softplus(x):
    u = jnp.exp(-jnp.abs(x))
    w = 1.0 + u
    log1p = jnp.where(w == 1.0, u, jnp.log(w) * u / jnp.where(w == 1.0, 1.0, w - 1.0))
    return jnp.maximum(x, 0.0) + log1p


def _colsum(x):
    return jnp.sum(x, axis=0, keepdims=True)


def _rowsum(x):
    return jnp.sum(x, axis=1, keepdims=True)


def _shift_down(t, halo, j):
    if j == 0:
        return t
    n = t.shape[0]
    rolled = pltpu.roll(t, j, 0)
    row = lax.broadcasted_iota(jnp.int32, (SUB, t.shape[1]), 0)
    first = jnp.where(row < j, pltpu.roll(halo, j, 0), rolled[:SUB])
    return jnp.concatenate([first, rolled[SUB:]], axis=0) if n > SUB else first


def _shift_up(t, halo, j):
    if j == 0:
        return t
    n = t.shape[0]
    rolled = pltpu.roll(t, n - j, 0)
    row = lax.broadcasted_iota(jnp.int32, (SUB, t.shape[1]), 0)
    last = jnp.where(row >= SUB - j, pltpu.roll(halo, SUB - j, 0), rolled[n - SUB:])
    return jnp.concatenate([rolled[:n - SUB], last], axis=0) if n > SUB else last


def _mm(a, b, *, ta=False, tb=False, out_dtype=F32, name):
    if ta:
        K, M = a.shape
    else:
        M, K = a.shape
    if tb:
        N, K2 = b.shape
    else:
        K2, N = b.shape
    assert K == K2, (a.shape, b.shape, ta, tb)
    tm, tn, tk = _tile(M, 1024), _tile(N, 1408), _tile(K, 1024)
    nk = K // tk
    dn = (((0 if ta else 1,), (1 if tb else 0,)), ((), ()))

    def body(a_ref, b_ref, o_ref, acc_ref):
        k = pl.program_id(2)
        part = lax.dot_general(a_ref[...].astype(_MXU), b_ref[...].astype(_MXU), dn, preferred_element_type=F32)

        @pl.when(k == 0)
        def _():
            acc_ref[...] = part

        @pl.when(k > 0)
        def _():
            acc_ref[...] += part

        @pl.when(k == nk - 1)
        def _():
            o_ref[...] = acc_ref[...].astype(out_dtype)

    a_spec = pl.BlockSpec((tk, tm), lambda i, j, k: (k, i)) if ta else pl.BlockSpec((tm, tk), lambda i, j, k: (i, k))
    b_spec = pl.BlockSpec((tn, tk), lambda i, j, k: (j, k)) if tb else pl.BlockSpec((tk, tn), lambda i, j, k: (k, j))
    return pl.pallas_call(
        body, grid=(M // tm, N // tn, nk), in_specs=[a_spec, b_spec],
        out_specs=pl.BlockSpec((tm, tn), lambda i, j, k: (i, j)),
        out_shape=jax.ShapeDtypeStruct((M, N), out_dtype),
        scratch_shapes=[pltpu.VMEM((tm, tn), F32)], compiler_params=_cp(3), name=name)(a, b)


def _row(ts, w, col=0):
    return pl.BlockSpec((ts, w), lambda i: (i, col))


def _vec(w, r=1):
    return pl.BlockSpec((r, w), lambda i: (0, 0))


def _silu_call(x, name):
    def body(x_ref, o_ref):
        o_ref[...] = _silu(x_ref[...])
    return pl.pallas_call(body, out_shape=jax.ShapeDtypeStruct(x.shape, F32), name=name)(x)


def _norm_fwd(x, g, sc, sh, *, f=None, gate=None, name):
    S, dm = x.shape
    ts = _tile(S, TS_ROW, SUB)
    res = f is not None

    def body(*refs):
        if res:
            x_ref, f_ref, gate_ref, g_ref, sc_ref, sh_ref, xo_ref, h_ref = refs
            xv = x_ref[...] + gate_ref[...] * f_ref[...]
            xo_ref[...] = xv
        else:
            x_ref, g_ref, sc_ref, sh_ref, h_ref = refs
            xv = x_ref[...]
        rstd = lax.rsqrt(jnp.mean(xv * xv, axis=-1, keepdims=True) + EPS)
        h_ref[...] = ((xv * rstd) * g_ref[...] * (1.0 + sc_ref[...]) + sh_ref[...]).astype(_ACT)

    ins = [x] + ([f, gate] if res else []) + [g, sc, sh]
    in_specs = [_row(ts, dm)] + ([_row(ts, dm), _vec(dm)] if res else []) + [_vec(dm)] * 3
    h_shape = jax.ShapeDtypeStruct((S, dm), _ACT)
    if res:
        out_shape, out_specs = (jax.ShapeDtypeStruct((S, dm), F32), h_shape), (_row(ts, dm), _row(ts, dm))
    else:
        out_shape, out_specs = h_shape, _row(ts, dm)
    return pl.pallas_call(body, grid=(S // ts,), in_specs=in_specs, out_specs=out_specs, out_shape=out_shape,
                          compiler_params=_cp(1), name=name)(*ins)


def _norm_bwd(dh, x, dres, g, sc, *, name):
    S, dm = x.shape
    ts = _tile(S, TS_ROW, SUB)

    def body(dh_ref, x_ref, dres_ref, g_ref, sc_ref, dx_ref, dg_ref, dsc_ref, dsh_ref):
        i = pl.program_id(0)
        xv = x_ref[...]
        dhv = dh_ref[...]
        rstd = lax.rsqrt(jnp.mean(xv * xv, axis=-1, keepdims=True) + EPS)
        xhat = xv * rstd
        hn = xhat * g_ref[...]
        dhn = dhv * (1.0 + sc_ref[...])
        dxh = dhn * g_ref[...]
        dx_ref[...] = dres_ref[...] + rstd * (dxh - xhat * jnp.mean(dxh * xhat, axis=-1, keepdims=True))

        @pl.when(i == 0)
        def _():
            dg_ref[...] = jnp.zeros_like(dg_ref)
            dsc_ref[...] = jnp.zeros_like(dsc_ref)
            dsh_ref[...] = jnp.zeros_like(dsh_ref)

        dg_ref[...] += _colsum(dhn * xhat)
        dsc_ref[...] += _colsum(dhv * hn)
        dsh_ref[...] += _colsum(dhv)

    vshape = jax.ShapeDtypeStruct((1, dm), F32)
    return pl.pallas_call(
        body, grid=(S // ts,), in_specs=[_row(ts, dm)] * 3 + [_vec(dm)] * 2,
        out_specs=(_row(ts, dm), _vec(dm), _vec(dm), _vec(dm)),
        out_shape=(jax.ShapeDtypeStruct((S, dm), F32), vshape, vshape, vshape),
        compiler_params=_cp(1), name=name)(dh, x, dres, g, sc)


def _gate_bwd(dxo, f, gate, *, name):
    S, dm = f.shape
    ts = _tile(S, TS_ROW, SUB)

    def body(dxo_ref, f_ref, gate_ref, df_ref, dgate_ref):
        i = pl.program_id(0)
        dv = dxo_ref[...]
        df_ref[...] = (gate_ref[...] * dv).astype(_ACT)

        @pl.when(i == 0)
        def _():
            dgate_ref[...] = jnp.zeros_like(dgate_ref)

        dgate_ref[...] += _colsum(dv * f_ref[...])

    return pl.pallas_call(
        body, grid=(S // ts,), in_specs=[_row(ts, dm), _row(ts, dm), _vec(dm)],
        out_specs=(_row(ts, dm), _vec(dm)),
        out_shape=(jax.ShapeDtypeStruct((S, dm), _ACT), jax.ShapeDtypeStruct((1, dm), F32)),
        compiler_params=_cp(1), name=name)(dxo, f, gate)


def _final_loss(x, f, gate, g, tgt, *, name):
    S, dm = x.shape
    ts = _tile(S, TS_ROW, SUB)

    def body(x_ref, f_ref, gate_ref, g_ref, t_ref, loss_ref, dx_ref, dg_ref):
        i = pl.program_id(0)
        xv = x_ref[...] + gate_ref[...] * f_ref[...]
        rstd = lax.rsqrt(jnp.mean(xv * xv, axis=-1, keepdims=True) + EPS)
        xhat = xv * rstd
        err = xhat * g_ref[...] - t_ref[...]
        dy = err * (1.0 / dm)
        dxh = dy * g_ref[...]
        dx_ref[...] = rstd * (dxh - xhat * jnp.mean(dxh * xhat, axis=-1, keepdims=True))

        @pl.when(i == 0)
        def _():
            loss_ref[...] = jnp.zeros_like(loss_ref)
            dg_ref[...] = jnp.zeros_like(dg_ref)

        loss_ref[...] += jnp.full((1, LANE), 0.5 * jnp.sum(jnp.mean(err * err, axis=-1, keepdims=True)), F32)
        dg_ref[...] += _colsum(dy * xhat)

    return pl.pallas_call(
        body, grid=(S // ts,), in_specs=[_row(ts, dm), _row(ts, dm), _vec(dm), _vec(dm), _row(ts, dm)],
        out_specs=(_vec(LANE), _row(ts, dm), _vec(dm)),
        out_shape=(jax.ShapeDtypeStruct((1, LANE), F32), jax.ShapeDtypeStruct((S, dm), F32),
                   jax.ShapeDtypeStruct((1, dm), F32)),
        compiler_params=_cp(1), name=name)(x, f, gate, g, tgt)


def _ffn_conv(t, halo, cw_ref, cb_ref):
    return ((cb_ref[...] + _shift_down(t, halo, 2) * cw_ref[0:1, :]) + _shift_down(t, halo, 1) * cw_ref[1:2, :]) \
        + t * cw_ref[2:3, :]


def _prev_halo_spec(ts, w, col=0):
    return pl.BlockSpec((SUB, w), lambda i: (jnp.maximum(i * (ts // SUB) - 1, 0), col))


def _ffn_act_fwd(up, cw, cb, *, name):
    S, w2 = up.shape
    ff = w2 // 2
    ts = _tile(S, TS_FFN, SUB)

    def body(up_ref, halo_ref, cw_ref, cb_ref, act_ref):
        i = pl.program_id(0)
        t = up_ref[...]
        halo = jnp.where(i > 0, halo_ref[...], 0.0)
        u = _ffn_conv(t, halo, cw_ref, cb_ref)
        act_ref[...] = (_silu(u[:, :ff]) * u[:, ff:]).astype(_ACT)

    return pl.pallas_call(
        body, grid=(S // ts,), in_specs=[_row(ts, w2), _prev_halo_spec(ts, w2), _vec(w2, FFN_CONV), _vec(w2)],
        out_specs=_row(ts, ff), out_shape=jax.ShapeDtypeStruct((S, ff), _ACT),
        compiler_params=_cp(1), name=name)(up, up, cw, cb)


def _ffn_act_bwd(up, dact, cw, cb, *, name):
    S, w2 = up.shape
    ff = w2 // 2
    ts = _tile(S, TS_FFN, SUB)

    def body(up_ref, halo_ref, dact_ref, cw_ref, cb_ref, du_ref):
        i = pl.program_id(0)
        t = up_ref[...]
        halo = jnp.where(i > 0, halo_ref[...], 0.0)
        u = _ffn_conv(t, halo, cw_ref, cb_ref)
        a, b = u[:, :ff], u[:, ff:]
        da = dact_ref[...]
        du_ref[:, :ff] = da * b * _dsilu(a)
        du_ref[:, ff:] = da * _silu(a)

    return pl.pallas_call(
        body, grid=(S // ts,),
        in_specs=[_row(ts, w2), _prev_halo_spec(ts, w2), _row(ts, ff), _vec(w2, FFN_CONV), _vec(w2)],
        out_specs=_row(ts, w2), out_shape=jax.ShapeDtypeStruct((S, w2), F32),
        compiler_params=_cp(1), name=name)(up, up, dact, cw, cb)


def _ffn_conv_bwd(du, up, cw, *, name):
    S, w2 = up.shape
    ts = _tile(S, TS_FFN, SUB)
    n = S // ts

    def body(du_ref, nxt_ref, up_ref, halo_ref, cw_ref, dup_ref, dcw_ref, dcb_ref):
        i = pl.program_id(0)
        dv = du_ref[...]
        nxt = jnp.where(i < n - 1, nxt_ref[...], 0.0)
        t = up_ref[...]
        halo = jnp.where(i > 0, halo_ref[...], 0.0)
        dup = (dv * cw_ref[2:3, :] + _shift_up(dv, nxt, 1) * cw_ref[1:2, :]) + _shift_up(dv, nxt, 2) * cw_ref[0:1, :]
        dup_ref[...] = dup.astype(_ACT)

        @pl.when(i == 0)
        def _():
            dcw_ref[...] = jnp.zeros_like(dcw_ref)
            dcb_ref[...] = jnp.zeros_like(dcb_ref)

        dcb_ref[...] += _colsum(dv)
        dcw_ref[2:3, :] += _colsum(dv * t)
        dcw_ref[1:2, :] += _colsum(dv * _shift_down(t, halo, 1))
        dcw_ref[0:1, :] += _colsum(dv * _shift_down(t, halo, 2))

    nxt_spec = pl.BlockSpec((SUB, w2), lambda i: (jnp.minimum((i + 1) * (ts // SUB), S // SUB - 1), 0))
    return pl.pallas_call(
        body, grid=(n,),
        in_specs=[_row(ts, w2), nxt_spec, _row(ts, w2), _prev_halo_spec(ts, w2), _vec(w2, FFN_CONV)],
        out_specs=(_row(ts, w2), _vec(w2, FFN_CONV), _vec(w2)),
        out_shape=(jax.ShapeDtypeStruct((S, w2), _ACT), jax.ShapeDtypeStruct((FFN_CONV, w2), F32),
                   jax.ShapeDtypeStruct((1, w2), F32)),
        compiler_params=_cp(1), name=name)(du, du, up, up, cw)


def _ssd_core(pre, halo, misc, cw_ref, cb_ref, dtb, alog):
    q = pre.shape[0]
    conv = cb_ref[...]
    for k in range(SSD_CONV):
        conv = conv + _shift_down(pre, halo, SSD_CONV - 1 - k) * cw_ref[k:k + 1, :]
    xbc = _silu(conv)
    raw = misc + dtb
    dt = _softplus(raw)
    a = -jnp.exp(alog)
    r = lax.broadcasted_iota(jnp.int32, (q, q), 0)
    c = lax.broadcasted_iota(jnp.int32, (q, q), 1)
    tri = r >= c
    acum = jnp.dot(tri.astype(F32), dt * a, precision=_HI, preferred_element_type=F32)
    return conv, xbc, raw, dt, a, acum, acum.T, tri


def _sel(v, j, lo):
    return jnp.where(lo, v[:, 2 * j:2 * j + 1], v[:, 2 * j + 1:2 * j + 2])


def _ssd_pair_fwd(xbc, dt, acum, acum_t, tri, dsk, g_mat, b_mat, c_mat, h_pair, j, lo, lo1, sub_lo):
    q = xbc.shape[0]
    x = xbc[:, LANE * j:LANE * (j + 1)]
    dtp = _sel(dt, j, lo)
    ap = _sel(acum, j, lo)
    xd = x * dtp
    ls, ms = [], []
    for h in (2 * j, 2 * j + 1):
        seg = acum[:, h:h + 1] - acum_t[h:h + 1, :]
        l_mat = jnp.exp(jnp.where(tri, seg, -jnp.inf))
        ls.append(l_mat)
        ms.append(g_mat * l_mat)
    yd = jnp.where(lo, _dot(ms[0], xd), _dot(ms[1], xd))
    ea = jnp.exp(ap)
    yo = _dot_nt(c_mat, h_pair) * ea
    dp = _sel(dsk, j, lo1)
    alast = acum[q - 1:q, :]
    e = jnp.exp(_sel(alast, j, lo1) - ap)
    cd = jnp.where(sub_lo, jnp.exp(alast[:, 2 * j:2 * j + 1]), jnp.exp(alast[:, 2 * j + 1:2 * j + 2]))
    return dict(x=x, dtp=dtp, ap=ap, xd=xd, ls=ls, ms=ms, ea=ea, yo=yo, dp=dp, e=e, cd=cd, y=yd + yo + x * dp)


def _gnorm(yg):
    half = SSD_INNER // SSD_GROUPS
    rstds, yns = [], []
    for g in range(SSD_GROUPS):
        part = yg[:, half * g:half * (g + 1)]
        rstd = lax.rsqrt(jnp.mean(part * part, axis=-1, keepdims=True) + EPS)
        rstds.append(rstd)
        yns.append(part * rstd)
    return rstds, yns


def _ssd_specs(nc, rev):
    q = SSD_CHUNK
    cidx = (lambda i: nc - 1 - i) if rev else (lambda i: i)
    return [
        pl.BlockSpec((q, SSD_XBC), lambda i: (cidx(i), C_XBC // SSD_XBC)),
        pl.BlockSpec((SUB, SSD_XBC), lambda i: (jnp.maximum(cidx(i) * (q // SUB) - 1, 0), C_XBC // SSD_XBC)),
        pl.BlockSpec((q, SSD_INNER), lambda i: (cidx(i), C_Z // SSD_INNER)),
        pl.BlockSpec((q, LANE), lambda i: (cidx(i), C_MISC // LANE)),
    ]


def _ssd_param_specs():
    return [_vec(SSD_XBC, SSD_CONV), _vec(SSD_XBC), _vec(LANE), _vec(LANE), _vec(LANE), _vec(SSD_INNER)]


def _ssd_fwd(proj, cw, cb, dtb, alog, dsk, ng, *, name):
    S = proj.shape[0]
    q = SSD_CHUNK
    nc = S // q
    npair = SSD_HEADS // 2

    def body(xbc_ref, halo_ref, z_ref, misc_ref, cw_ref, cb_ref, dtb_ref, alog_ref, dsk_ref, ng_ref,
             y_ref, hin_ref, h_ref):
        c = pl.program_id(0)

        @pl.when(c == 0)
        def _():
            h_ref[...] = jnp.zeros_like(h_ref)

        pre = xbc_ref[...]
        halo = jnp.where(c > 0, halo_ref[...], 0.0)
        conv, xbc, raw, dt, a, acum, acum_t, tri = _ssd_core(pre, halo, misc_ref[...], cw_ref, cb_ref,
                                                             dtb_ref[...], alog_ref[...])
        lo = lax.broadcasted_iota(jnp.int32, (q, LANE), 1) < LANE // 2
        lo1 = lo[:1]
        sub_lo = lax.broadcasted_iota(jnp.int32, (LANE, LANE), 0) < LANE // 2
        ys = []
        for g in range(SSD_GROUPS):
            b_mat = xbc[:, SSD_INNER + SSD_STATE * g:SSD_INNER + SSD_STATE * (g + 1)]
            c_mat = xbc[:, SSD_INNER + SSD_STATE * (SSD_GROUPS + g):SSD_INNER + SSD_STATE * (SSD_GROUPS + g + 1)]
            g_mat = _dot_nt(c_mat, b_mat)
            for jj in range(npair // SSD_GROUPS):
                j = g * (npair // SSD_GROUPS) + jj
                hj = h_ref[j]
                p = _ssd_pair_fwd(xbc, dt, acum, acum_t, tri, dsk_ref[...], g_mat, b_mat, c_mat, hj, j, lo, lo1, sub_lo)
                ys.append(p["y"])
                hin_ref[0, j] = hj
                h_ref[j] = p["cd"] * hj + _dot_tn(p["xd"] * p["e"], b_mat)
        yg = jnp.concatenate(ys, axis=1) * _silu(z_ref[...])
        _, yns = _gnorm(yg)
        y_ref[...] = jnp.concatenate(yns, axis=1) * ng_ref[...]

    return pl.pallas_call(
        body, grid=(nc,), in_specs=_ssd_specs(nc, False) + _ssd_param_specs(),
        out_specs=(pl.BlockSpec((q, SSD_INNER), lambda i: (i, 0)),
                   pl.BlockSpec((1, npair, LANE, LANE), lambda i: (i, 0, 0, 0))),
        out_shape=(jax.ShapeDtypeStruct((S, SSD_INNER), F32), jax.ShapeDtypeStruct((nc, npair, LANE, LANE), F32)),
        scratch_shapes=[pltpu.VMEM((npair, LANE, LANE), F32)], compiler_params=_cp(1), name=name,
    )(proj, proj, proj, proj, cw, cb, dtb, alog, dsk, ng)


def _ssd_bwd(proj, dycat, hin, cw, cb, dtb, alog, dsk, ng, *, name):
    S = proj.shape[0]
    q = SSD_CHUNK
    nc = S // q
    npair = SSD_HEADS // 2
    ppg = npair // SSD_GROUPS

    def body(xbc_ref, halo_ref, z_ref, misc_ref, dy_ref, hin_ref, cw_ref, cb_ref, dtb_ref, alog_ref, dsk_ref, ng_ref,
             dpre_ref, dz_ref, dmisc_ref, dcw_ref, dcb_ref, ddtb_ref, dalog_ref, ddsk_ref, dng_ref,
             dh_ref, carry_ref):
        i = pl.program_id(0)
        c = nc - 1 - i

        @pl.when(i == 0)
        def _():
            dh_ref[...] = jnp.zeros_like(dh_ref)
            carry_ref[...] = jnp.zeros_like(carry_ref)
            for r in (dcw_ref, dcb_ref, ddtb_ref, dalog_ref, ddsk_ref, dng_ref):
                r[...] = jnp.zeros_like(r)

        pre = xbc_ref[...]
        halo = jnp.where(c > 0, halo_ref[...], 0.0)
        conv, xbc, raw, dt, a, acum, acum_t, tri = _ssd_core(pre, halo, misc_ref[...], cw_ref, cb_ref,
                                                             dtb_ref[...], alog_ref[...])
        lane = lax.broadcasted_iota(jnp.int32, (q, LANE), 1)
        lane1 = lane[:1]
        rowi = lax.broadcasted_iota(jnp.int32, (q, LANE), 0)
        lastrow = rowi == q - 1
        lo = lane < LANE // 2
        lo1 = lo[:1]
        sub_lo = lax.broadcasted_iota(jnp.int32, (LANE, LANE), 0) < LANE // 2
        dsk = dsk_ref[...]
        alast = acum[q - 1:q, :]

        def halves(t):
            return _rowsum(jnp.where(lo, t, 0.0)), _rowsum(jnp.where(lo, 0.0, t))

        def put(ha, va, vb):
            ln = lane if va.shape[0] == q else lane1
            return jnp.where(ln == ha, va, 0.0) + jnp.where(ln == ha + 1, vb, 0.0)

        mats, pairs = [], []
        for g in range(SSD_GROUPS):
            b_mat = xbc[:, SSD_INNER + SSD_STATE * g:SSD_INNER + SSD_STATE * (g + 1)]
            c_mat = xbc[:, SSD_INNER + SSD_STATE * (SSD_GROUPS + g):SSD_INNER + SSD_STATE * (SSD_GROUPS + g + 1)]
            g_mat = _dot_nt(c_mat, b_mat)
            mats.append((b_mat, c_mat, g_mat))
            for jj in range(ppg):
                j = g * ppg + jj
                pairs.append(_ssd_pair_fwd(xbc, dt, acum, acum_t, tri, dsk, g_mat, b_mat, c_mat, hin_ref[0, j],
                                           j, lo, lo1, sub_lo))
        z = z_ref[...]
        sz = _silu(z)
        y = jnp.concatenate([p["y"] for p in pairs], axis=1)
        rstds, yns = _gnorm(y * sz)
        dout = dy_ref[...]
        dng_ref[...] += _colsum(dout * jnp.concatenate(yns, axis=1))
        dyn = dout * ng_ref[...]
        half = SSD_INNER // SSD_GROUPS
        dygs = []
        for g in range(SSD_GROUPS):
            dyn_g = dyn[:, half * g:half * (g + 1)]
            dygs.append(rstds[g] * (dyn_g - yns[g] * jnp.mean(dyn_g * yns[g], axis=-1, keepdims=True)))
        dyg = jnp.concatenate(dygs, axis=1)
        dyv = dyg * sz
        dz_ref[...] = (dyg * y * _dsilu(z)).astype(_ACT)

        da_acc = jnp.zeros((q, LANE), F32)
        ddt = jnp.zeros((q, LANE), F32)
        dds = jnp.zeros((1, LANE), F32)
        dxs, dbs, dcs = [], [], []
        for g in range(SSD_GROUPS):
            b_mat, c_mat, g_mat = mats[g]
            dg_mat = jnp.zeros((q, q), F32)
            db = jnp.zeros((q, SSD_STATE), F32)
            dc = jnp.zeros((q, SSD_STATE), F32)
            for jj in range(ppg):
                j = g * ppg + jj
                ha = 2 * j
                p = pairs[j]
                hj = hin_ref[0, j]
                dyp = dyv[:, LANE * j:LANE * (j + 1)]
                dsum = _colsum(dyp * p["x"])
                dds = dds + put(ha, _rowsum(jnp.where(lo1, dsum, 0.0)), _rowsum(jnp.where(lo1, 0.0, dsum)))
                dx = dyp * p["dp"]
                dw = dyp * p["ea"]
                dc = dc + _dot(dw, hj)
                dh_yo = _dot_tn(dw, c_mat)
                ra, rb = halves(dyp * p["yo"])
                da_acc = da_acc + put(ha, ra, rb)
                dxd = jnp.zeros((q, LANE), F32)
                for idx in range(2):
                    dyh = jnp.where(lo, dyp, 0.0) if idx == 0 else jnp.where(lo, 0.0, dyp)
                    dm = _dot_nt(dyh, p["xd"])
                    dxd = dxd + _dot_tn(p["ms"][idx], dyh)
                    dg_mat = dg_mat + dm * p["ls"][idx]
                    t = dm * p["ms"][idx]
                    da_h = _rowsum(t) - _rowsum(t.T)
                    da_acc = da_acc + jnp.where(lane == ha + idx, da_h, 0.0)
                dhn = dh_ref[j]
                s = _rowsum(dhn * hj)
                sa = jnp.sum(jnp.where(sub_lo[:, :1], s, 0.0), keepdims=True)
                sb = jnp.sum(jnp.where(sub_lo[:, :1], 0.0, s), keepdims=True)
                cda, cdb = jnp.exp(alast[:, ha:ha + 1]), jnp.exp(alast[:, ha + 1:ha + 2])
                db = db + _dot(p["xd"] * p["e"], dhn)
                r = _dot_nt(b_mat, dhn)
                dxd = dxd + r * p["e"]
                qa, qb = halves(r * p["xd"] * p["e"])
                da_acc = da_acc - put(ha, qa, qb)
                tot_a = sa * cda + jnp.sum(qa, keepdims=True)
                tot_b = sb * cdb + jnp.sum(qb, keepdims=True)
                da_acc = da_acc + jnp.where(lastrow, put(ha, tot_a, tot_b), 0.0)
                dh_ref[j] = p["cd"] * dhn + dh_yo
                dx = dx + dxd * p["dtp"]
                ua, ub = halves(dxd * p["x"])
                ddt = ddt + put(ha, ua, ub)
                dxs.append(dx)
            dc = dc + _dot(dg_mat, b_mat)
            db = db + _dot_tn(dg_mat, c_mat)
            dbs.append(db)
            dcs.append(dc)
        r2 = lax.broadcasted_iota(jnp.int32, (q, q), 0)
        c2 = lax.broadcasted_iota(jnp.int32, (q, q), 1)
        dda = jnp.dot((c2 >= r2).astype(F32), da_acc, precision=_HI, preferred_element_type=F32)
        ddt = ddt + dda * a
        dalog_ref[...] += _colsum(dda * dt) * a
        ddsk_ref[...] += dds
        draw = jnp.where(lane < SSD_HEADS, ddt * _sigmoid(raw), 0.0)
        ddtb_ref[...] += _colsum(draw)
        dmisc_ref[...] = draw
        dconv = jnp.concatenate(dxs + dbs + dcs, axis=1) * _dsilu(conv)
        dcb_ref[...] += _colsum(dconv)
        nxt = carry_ref[...]
        dpre = jnp.zeros_like(dconv)
        for k in range(SSD_CONV):
            dcw_ref[k:k + 1, :] += _colsum(dconv * _shift_down(pre, halo, SSD_CONV - 1 - k))
            dpre = dpre + _shift_up(dconv, nxt, SSD_CONV - 1 - k) * cw_ref[k:k + 1, :]
        dpre_ref[...] = dpre.astype(_ACT)
        carry_ref[...] = dconv[:SUB]

    rev = lambda i: (nc - 1 - i, 0)
    vshape = lambda w, r=1: jax.ShapeDtypeStruct((r, w), F32)
    return pl.pallas_call(
        body, grid=(nc,),
        in_specs=_ssd_specs(nc, True) + [pl.BlockSpec((q, SSD_INNER), rev),
                                         pl.BlockSpec((1, npair, LANE, LANE), lambda i: (nc - 1 - i, 0, 0, 0))]
        + _ssd_param_specs(),
        out_specs=(pl.BlockSpec((q, SSD_XBC), rev), pl.BlockSpec((q, SSD_INNER), rev), pl.BlockSpec((q, LANE), rev),
                   _vec(SSD_XBC, SSD_CONV), _vec(SSD_XBC), _vec(LANE), _vec(LANE), _vec(LANE), _vec(SSD_INNER)),
        out_shape=(jax.ShapeDtypeStruct((S, SSD_XBC), _ACT), jax.ShapeDtypeStruct((S, SSD_INNER), _ACT),
                   jax.ShapeDtypeStruct((S, LANE), F32),
                   vshape(SSD_XBC, SSD_CONV), vshape(SSD_XBC), vshape(LANE), vshape(LANE), vshape(LANE),
                   vshape(SSD_INNER)),
        scratch_shapes=[pltpu.VMEM((npair, LANE, LANE), F32), pltpu.VMEM((SUB, SSD_XBC), F32)],
        compiler_params=_cp(1), name=name,
    )(proj, proj, proj, proj, dycat, hin, cw, cb, dtb, alog, dsk, ng)


def _rope(x, cosf, sina, sinb):
    return x * cosf + pltpu.roll(x, LANE - MLA_ROPE // 2, 1) * sina + pltpu.roll(x, MLA_ROPE // 2, 1) * sinb


def _rope_t(dy, cosf, sina, sinb):
    return dy * cosf + pltpu.roll(dy * sina, MLA_ROPE // 2, 1) + pltpu.roll(dy * sinb, LANE - MLA_ROPE // 2, 1)


def _rope_lanes(shape):
    lane = lax.broadcasted_iota(jnp.int32, shape, 1)
    return (lane >= ROPE_LANE) & (lane < ROPE_LANE + MLA_ROPE)


def _mla_prep_fwd(proj, cosf, sina, sinb, gq, gkv, wuq, wukv, *, name):
    S = proj.shape[0]
    ts = _tile(S, TS_ROW, SUB)
    hw = MLA_HEADS * LANE

    def body(cq_ref, ckv_ref, misc_ref, cos_ref, sa_ref, sb_ref, gq_ref, gkv_ref, wuq_ref, wukv_ref,
             q_ref, k_ref, v_ref):
        cosv, sav, sbv = cos_ref[...], sa_ref[...], sb_ref[...]
        cq = cq_ref[...]
        qn = cq * lax.rsqrt(jnp.mean(cq * cq, axis=-1, keepdims=True) + EPS) * gq_ref[...]
        qh = _dot(qn, wuq_ref[...])
        ckv = ckv_ref[...]
        kvn = ckv * lax.rsqrt(jnp.mean(ckv * ckv, axis=-1, keepdims=True) + EPS) * gkv_ref[...]
        kv = _dot(kvn, wukv_ref[...])
        kr = _rope(jnp.where(_rope_lanes((ts, LANE)), misc_ref[...], 0.0), cosv, sav, sbv)
        for h in range(MLA_HEADS):
            sl = slice(LANE * h, LANE * (h + 1))
            q_ref[:, sl] = _rope(qh[:, sl], cosv, sav, sbv).astype(_ACT)
            k_ref[:, sl] = (kv[:, sl] + kr).astype(_ACT)
        v_ref[...] = kv[:, hw:].astype(_ACT)

    oshape = jax.ShapeDtypeStruct((S, hw), _ACT)
    return pl.pallas_call(
        body, grid=(S // ts,),
        in_specs=[_row(ts, MLA_QR, C_CQ // MLA_QR), _row(ts, MLA_KVR, C_CKV // MLA_KVR), _row(ts, LANE, C_MISC // LANE),
                  _row(ts, LANE), _row(ts, LANE), _row(ts, LANE), _vec(MLA_QR), _vec(MLA_KVR),
                  _vec(hw, MLA_QR), _vec(2 * hw, MLA_KVR)],
        out_specs=(_row(ts, hw),) * 3, out_shape=(oshape,) * 3, compiler_params=_cp(1), name=name,
    )(proj, proj, proj, cosf, sina, sinb, gq, gkv, wuq, wukv)


def _mla_prep_bwd(proj, dq, dk, dv, dmisc_ssd, cosf, sina, sinb, gq, gkv, wuq, wukv, *, name):
    S = proj.shape[0]
    ts = _tile(S, TS_ROW, SUB)
    hw = MLA_HEADS * LANE

    def body(cq_ref, ckv_ref, dq_ref, dk_ref, dv_ref, dms_ref, cos_ref, sa_ref, sb_ref, gq_ref, gkv_ref,
             wuq_ref, wukv_ref, dcq_ref, dckv_ref, dmisc_ref, dqh_ref, dkv_ref, qn_ref, kvn_ref, dgq_ref, dgkv_ref):
        i = pl.program_id(0)
        cosv, sav, sbv = cos_ref[...], sa_ref[...], sb_ref[...]

        @pl.when(i == 0)
        def _():
            dgq_ref[...] = jnp.zeros_like(dgq_ref)
            dgkv_ref[...] = jnp.zeros_like(dgkv_ref)

        dqh = jnp.concatenate([_rope_t(dq_ref[:, LANE * h:LANE * (h + 1)], cosv, sav, sbv)
                               for h in range(MLA_HEADS)], axis=1)
        dqh_ref[...] = dqh.astype(_ACT)
        dkv = jnp.concatenate([dk_ref[...], dv_ref[...]], axis=1)
        dkv_ref[...] = dkv.astype(_ACT)

        def norm_bwd(x, g, dn, dg_ref, n_ref):
            rstd = lax.rsqrt(jnp.mean(x * x, axis=-1, keepdims=True) + EPS)
            xhat = x * rstd
            n_ref[...] = (xhat * g).astype(_ACT)
            dg_ref[...] += _colsum(dn * xhat)
            dxh = dn * g
            return rstd * (dxh - xhat * jnp.mean(dxh * xhat, axis=-1, keepdims=True))

        dcq_ref[...] = norm_bwd(cq_ref[...], gq_ref[...], _dot_nt(dqh, wuq_ref[...]), dgq_ref, qn_ref).astype(_ACT)
        dckv_ref[...] = norm_bwd(ckv_ref[...], gkv_ref[...], _dot_nt(dkv, wukv_ref[...]), dgkv_ref, kvn_ref).astype(_ACT)
        dks = dk_ref[:, 0:LANE]
        for h in range(1, MLA_HEADS):
            dks = dks + dk_ref[:, LANE * h:LANE * (h + 1)]
        rl = _rope_lanes((ts, LANE))
        dkr = _rope_t(jnp.where(rl, dks, 0.0), cosv, sav, sbv)
        dmisc_ref[...] = (dms_ref[...] + jnp.where(rl, dkr, 0.0)).astype(_ACT)

    act = lambda w: jax.ShapeDtypeStruct((S, w), _ACT)
    return pl.pallas_call(
        body, grid=(S // ts,),
        in_specs=[_row(ts, MLA_QR, C_CQ // MLA_QR), _row(ts, MLA_KVR, C_CKV // MLA_KVR),
                  _row(ts, hw), _row(ts, hw), _row(ts, hw), _row(ts, LANE),
                  _row(ts, LANE), _row(ts, LANE), _row(ts, LANE), _vec(MLA_QR), _vec(MLA_KVR),
                  _vec(hw, MLA_QR), _vec(2 * hw, MLA_KVR)],
        out_specs=(_row(ts, MLA_QR), _row(ts, MLA_KVR), _row(ts, LANE), _row(ts, hw), _row(ts, 2 * hw),
                   _row(ts, MLA_QR), _row(ts, MLA_KVR), _vec(MLA_QR), _vec(MLA_KVR)),
        out_shape=(act(MLA_QR), act(MLA_KVR), act(LANE), act(hw), act(2 * hw), act(MLA_QR), act(MLA_KVR),
                   jax.ShapeDtypeStruct((1, MLA_QR), F32), jax.ShapeDtypeStruct((1, MLA_KVR), F32)),
        compiler_params=_cp(1), name=name,
    )(proj, proj, dq, dk, dv, dmisc_ssd, cosf, sina, sinb, gq, gkv, wuq, wukv)


_MLA_SCALE = 1.0 / math.sqrt(MLA_NOPE + MLA_ROPE)


def _causal_scores(q, k, i, j, tq):
    s = _dot_nt(q, k) * _MLA_SCALE
    rows = i * tq + lax.broadcasted_iota(jnp.int32, (tq, tq), 0)
    cols = j * tq + lax.broadcasted_iota(jnp.int32, (tq, tq), 1)
    return jnp.where(cols <= rows, s, -jnp.inf)


def _attn_fwd(q, k, v, *, name):
    S = q.shape[0]
    tq = _tile(S, TQ_ATT)
    nq = S // tq

    def body(q_ref, k_ref, v_ref, o_ref, lse_ref, m_ref, l_ref, acc_ref):
        i, j = pl.program_id(1), pl.program_id(2)

        @pl.when(j == 0)
        def _():
            m_ref[...] = jnp.full_like(m_ref, -jnp.inf)
            l_ref[...] = jnp.zeros_like(l_ref)
            acc_ref[...] = jnp.zeros_like(acc_ref)

        @pl.when(j <= i)
        def _():
            s = _causal_scores(q_ref[...], k_ref[...], i, j, tq)
            m_prev = m_ref[...]
            m_new = jnp.maximum(m_prev, jnp.max(s, axis=1, keepdims=True))
            p = jnp.exp(s - m_new)
            alpha = jnp.exp(m_prev - m_new)
            l_ref[...] = alpha * l_ref[...] + _rowsum(p)
            acc_ref[...] = alpha * acc_ref[...] + _dot(p, v_ref[...])
            m_ref[...] = m_new

        @pl.when(j == nq - 1)
        def _():
            o_ref[...] = acc_ref[...] / l_ref[...]
            lse_ref[...] = jnp.broadcast_to(m_ref[...] + jnp.log(l_ref[...]), (tq, LANE))

    qspec = pl.BlockSpec((tq, LANE), lambda h, i, j: (i, h))
    kspec = pl.BlockSpec((tq, LANE), lambda h, i, j: (jnp.minimum(j, i), h))
    oshape = jax.ShapeDtypeStruct((S, MLA_HEADS * LANE), F32)
    return pl.pallas_call(
        body, grid=(MLA_HEADS, nq, nq), in_specs=[qspec, kspec, kspec], out_specs=(qspec, qspec),
        out_shape=(oshape, oshape),
        scratch_shapes=[pltpu.VMEM((tq, 1), F32), pltpu.VMEM((tq, 1), F32), pltpu.VMEM((tq, LANE), F32)],
        compiler_params=_cp(3), name=name)(q, k, v)


def _attn_bwd_dq(q, k, v, o, lse, dycat, *, name):
    S = q.shape[0]
    tq = _tile(S, TQ_ATT)
    nq = S // tq

    def body(q_ref, k_ref, v_ref, o_ref, lse_ref, do_ref, dq_ref, acc_ref):
        i, j = pl.program_id(1), pl.program_id(2)

        @pl.when(j == 0)
        def _():
            acc_ref[...] = jnp.zeros_like(acc_ref)

        @pl.when(j <= i)
        def _():
            kv = k_ref[...]
            p = jnp.exp(_causal_scores(q_ref[...], kv, i, j, tq) - lse_ref[:, 0:1])
            dov = do_ref[...]
            delta = _rowsum(dov * o_ref[...])
            ds = p * (_dot_nt(dov, v_ref[...]) - delta) * _MLA_SCALE
            acc_ref[...] += _dot(ds, kv)

        @pl.when(j == nq - 1)
        def _():
            dq_ref[...] = acc_ref[...]

    qspec = pl.BlockSpec((tq, LANE), lambda h, i, j: (i, h))
    kspec = pl.BlockSpec((tq, LANE), lambda h, i, j: (jnp.minimum(j, i), h))
    dospec = pl.BlockSpec((tq, LANE), lambda h, i, j: (i, SSD_INNER // LANE + h))
    return pl.pallas_call(
        body, grid=(MLA_HEADS, nq, nq), in_specs=[qspec, kspec, kspec, qspec, qspec, dospec], out_specs=qspec,
        out_shape=jax.ShapeDtypeStruct((S, MLA_HEADS * LANE), F32),
        scratch_shapes=[pltpu.VMEM((tq, LANE), F32)], compiler_params=_cp(3), name=name)(q, k, v, o, lse, dycat)


def _attn_bwd_dkv(q, k, v, o, lse, dycat, *, name):
    S = q.shape[0]
    tq = _tile(S, TQ_ATT)
    nq = S // tq

    def body(q_ref, k_ref, v_ref, o_ref, lse_ref, do_ref, dk_ref, dv_ref, dk_acc, dv_acc):
        j, i = pl.program_id(1), pl.program_id(2)

        @pl.when(i == 0)
        def _():
            dk_acc[...] = jnp.zeros_like(dk_acc)
            dv_acc[...] = jnp.zeros_like(dv_acc)

        @pl.when(i >= j)
        def _():
            qv = q_ref[...]
            p = jnp.exp(_causal_scores(qv, k_ref[...], i, j, tq) - lse_ref[:, 0:1])
            dov = do_ref[...]
            delta = _rowsum(dov * o_ref[...])
            dv_acc[...] += _dot_tn(p, dov)
            ds = p * (_dot_nt(dov, v_ref[...]) - delta) * _MLA_SCALE
            dk_acc[...] += _dot_tn(ds, qv)

        @pl.when(i == nq - 1)
        def _():
            dk_ref[...] = dk_acc[...]
            dv_ref[...] = dv_acc[...]

    qspec = pl.BlockSpec((tq, LANE), lambda h, j, i: (jnp.maximum(i, j), h))
    kspec = pl.BlockSpec((tq, LANE), lambda h, j, i: (j, h))
    dospec = pl.BlockSpec((tq, LANE), lambda h, j, i: (jnp.maximum(i, j), SSD_INNER // LANE + h))
    oshape = jax.ShapeDtypeStruct((S, MLA_HEADS * LANE), F32)
    return pl.pallas_call(
        body, grid=(MLA_HEADS, nq, nq), in_specs=[qspec, kspec, kspec, qspec, qspec, dospec],
        out_specs=(kspec, kspec), out_shape=(oshape, oshape),
        scratch_shapes=[pltpu.VMEM((tq, LANE), F32), pltpu.VMEM((tq, LANE), F32)],
        compiler_params=_cp(3), name=name)(q, k, v, o, lse, dycat)


_SWA_SCALE = 1.0 / math.sqrt(SWA_HD)
_SWA_KW = SWA_KV * LANE


def _swa_specs(S, ts, rev):
    n = S // ts
    t = (lambda i: n - 1 - i) if rev else (lambda i: i)
    hb = lambda i: jnp.maximum(t(i) * (ts // WINDOW) - 1, 0)
    return [
        pl.BlockSpec((ts, SWA_HEADS * LANE), lambda i: (t(i), C_SQ // (SWA_HEADS * LANE))),
        pl.BlockSpec((ts, _SWA_KW), lambda i: (t(i), C_SK // _SWA_KW)),
        pl.BlockSpec((WINDOW, _SWA_KW), lambda i: (hb(i), C_SK // _SWA_KW)),
        pl.BlockSpec((ts, _SWA_KW), lambda i: (t(i), C_SV // _SWA_KW)),
        pl.BlockSpec((WINDOW, _SWA_KW), lambda i: (hb(i), C_SV // _SWA_KW)),
    ]


def _swa_scores(qh, kk, t, b, ts):
    s = _dot_nt(qh, kk) * _SWA_SCALE
    row = lax.broadcasted_iota(jnp.int32, (WINDOW, 2 * WINDOW), 0)
    col = lax.broadcasted_iota(jnp.int32, (WINDOW, 2 * WINDOW), 1)
    rel = WINDOW + row - col
    kpos = t * ts + (b - 1) * WINDOW + col
    return jnp.where((rel >= 0) & (rel < WINDOW) & (kpos >= 0), s, -jnp.inf)


def _swa_fwd(proj, sinks, *, name):
    S = proj.shape[0]
    ts = _tile(S, TS_SWA)
    nb = ts // WINDOW

    def body(q_ref, k_ref, kh_ref, v_ref, vh_ref, sink_ref, o_ref, lse_ref):
        t = pl.program_id(0)
        kext = jnp.concatenate([kh_ref[...], k_ref[...]], axis=0)
        vext = jnp.concatenate([vh_ref[...], v_ref[...]], axis=0)
        for b in range(nb):
            rows = slice(WINDOW * b, WINDOW * (b + 1))
            for h in range(SWA_HEADS):
                kvl = slice(LANE * (h // (SWA_HEADS // SWA_KV)), LANE * (h // (SWA_HEADS // SWA_KV) + 1))
                hl = slice(LANE * h, LANE * (h + 1))
                kk = kext[WINDOW * b:WINDOW * (b + 2), kvl]
                vv = vext[WINDOW * b:WINDOW * (b + 2), kvl]
                s = _swa_scores(q_ref[rows, hl], kk, t, b, ts)
                sk = sink_ref[:, h:h + 1]
                m = jnp.maximum(jnp.max(s, axis=1, keepdims=True), sk)
                p = jnp.exp(s - m)
                den = _rowsum(p) + jnp.exp(sk - m)
                o_ref[rows, hl] = _dot(p, vv) / den
                lse_ref[rows, hl] = jnp.broadcast_to(m + jnp.log(den), (WINDOW, LANE))

    oshape = jax.ShapeDtypeStruct((S, SWA_HEADS * LANE), F32)
    ospec = pl.BlockSpec((ts, SWA_HEADS * LANE), lambda i: (i, 0))
    return pl.pallas_call(
        body, grid=(S // ts,), in_specs=_swa_specs(S, ts, False) + [_vec(LANE)], out_specs=(ospec, ospec),
        out_shape=(oshape, oshape), compiler_params=_cp(1), name=name)(proj, proj, proj, proj, proj, sinks)


def _swa_bwd(proj, o, lse, dycat, sinks, *, name):
    S = proj.shape[0]
    ts = _tile(S, TS_SWA)
    nb = ts // WINDOW
    n = S // ts
    grp = SWA_HEADS // SWA_KV

    def body(q_ref, k_ref, kh_ref, v_ref, vh_ref, o_ref, lse_ref, do_ref, sink_ref,
             dq_ref, dk_ref, dv_ref, dsink_ref, dk_carry, dv_carry):
        i = pl.program_id(0)
        t = n - 1 - i

        @pl.when(i == 0)
        def _():
            dk_carry[...] = jnp.zeros_like(dk_carry)
            dv_carry[...] = jnp.zeros_like(dv_carry)
            dsink_ref[...] = jnp.zeros_like(dsink_ref)

        kext = jnp.concatenate([kh_ref[...], k_ref[...]], axis=0)
        vext = jnp.concatenate([vh_ref[...], v_ref[...]], axis=0)
        lane1 = lax.broadcasted_iota(jnp.int32, (1, LANE), 1)
        dkb = [[jnp.zeros((WINDOW, LANE), F32) for _ in range(SWA_KV)] for _ in range(nb + 1)]
        dvb = [[jnp.zeros((WINDOW, LANE), F32) for _ in range(SWA_KV)] for _ in range(nb + 1)]
        dsink = jnp.zeros((1, LANE), F32)
        for b in range(nb):
            rows = slice(WINDOW * b, WINDOW * (b + 1))
            for h in range(SWA_HEADS):
                kvh = h // grp
                kvl = slice(LANE * kvh, LANE * (kvh + 1))
                hl = slice(LANE * h, LANE * (h + 1))
                kk = kext[WINDOW * b:WINDOW * (b + 2), kvl]
                vv = vext[WINDOW * b:WINDOW * (b + 2), kvl]
                qh = q_ref[rows, hl]
                lse_h = lse_ref[rows, LANE * h:LANE * h + 1]
                p = jnp.exp(_swa_scores(qh, kk, t, b, ts) - lse_h)
                doh = do_ref[rows, hl]
                delta = _rowsum(doh * o_ref[rows, hl])
                ds = p * (_dot_nt(doh, vv) - delta)
                sk = sink_ref[:, h:h + 1]
                dsink = dsink + jnp.where(lane1 == h, -jnp.sum(jnp.exp(sk - lse_h) * delta, keepdims=True), 0.0)
                dq_ref[rows, hl] = (_dot(ds, kk) * _SWA_SCALE).astype(_ACT)
                dkk = _dot_tn(ds, qh) * _SWA_SCALE
                dvv = _dot_tn(p, doh)
                dkb[b][kvh] = dkb[b][kvh] + dkk[:WINDOW]
                dkb[b + 1][kvh] = dkb[b + 1][kvh] + dkk[WINDOW:]
                dvb[b][kvh] = dvb[b][kvh] + dvv[:WINDOW]
                dvb[b + 1][kvh] = dvb[b + 1][kvh] + dvv[WINDOW:]
        dsink_ref[...] += dsink
        for dref, blocks, carry in ((dk_ref, dkb, dk_carry), (dv_ref, dvb, dv_carry)):
            old = carry[...]
            for b in range(1, nb + 1):
                blk = jnp.concatenate(blocks[b], axis=1)
                if b == nb:
                    blk = blk + old
                dref[WINDOW * (b - 1):WINDOW * b, :] = blk.astype(_ACT)
            carry[...] = jnp.concatenate(blocks[0], axis=1)

    hw = SWA_HEADS * LANE
    rev = lambda i: (n - 1 - i, 0)
    mix = lambda i: (n - 1 - i, (SSD_INNER + MLA_HEADS * LANE) // hw)
    return pl.pallas_call(
        body, grid=(n,),
        in_specs=_swa_specs(S, ts, True) + [pl.BlockSpec((ts, hw), rev), pl.BlockSpec((ts, hw), rev),
                                            pl.BlockSpec((ts, hw), mix), _vec(LANE)],
        out_specs=(pl.BlockSpec((ts, hw), rev), pl.BlockSpec((ts, _SWA_KW), rev), pl.BlockSpec((ts, _SWA_KW), rev),
                   _vec(LANE)),
        out_shape=(jax.ShapeDtypeStruct((S, hw), _ACT), jax.ShapeDtypeStruct((S, _SWA_KW), _ACT),
                   jax.ShapeDtypeStruct((S, _SWA_KW), _ACT), jax.ShapeDtypeStruct((1, LANE), F32)),
        scratch_shapes=[pltpu.VMEM((WINDOW, _SWA_KW), F32), pltpu.VMEM((WINDOW, _SWA_KW), F32)],
        compiler_params=_cp(1), name=name)(proj, proj, proj, proj, proj, o, lse, dycat, sinks)


def _exchange(arrays, *, scatter, name):
    n = len(arrays)

    def body(*refs):
        ins, outs = refs[:n], refs[n:2 * n]
        send_sems, recv_sems, loc_sems = refs[2 * n:]
        x, y, c = lax.axis_index("x"), lax.axis_index("y"), lax.axis_index("c")
        me = 4 * x + 2 * y + c

        def src(i, dest):
            return ins[i].at[dest] if scatter else ins[i]

        local = [pltpu.make_async_copy(src(i, me), outs[i].at[me], loc_sems.at[i]) for i in range(n)]
        for cp in local:
            cp.start()
        sends, recvs = [], []
        for k in range(1, NDEV):
            px = 1 - x if k & 4 else x
            py = 1 - y if k & 2 else y
            pc = 1 - c if k & 1 else c
            peer = 4 * px + 2 * py + pc
            for i in range(n):
                common = dict(send_sem=send_sems.at[i, k - 1], recv_sem=recv_sems.at[i, k - 1],
                              device_id=(px, py, pc), device_id_type=pl.DeviceIdType.MESH)
                sends.append(pltpu.make_async_remote_copy(src_ref=src(i, peer), dst_ref=outs[i].at[me], **common))
                recvs.append(pltpu.make_async_remote_copy(src_ref=src(i, peer), dst_ref=outs[i].at[peer], **common))
        for cp in sends:
            cp.start()
        for cp in recvs:
            cp.wait_recv()
        for cp in sends:
            cp.wait_send()
        for cp in local:
            cp.wait()

    hbm = pl.BlockSpec(memory_space=pl.ANY)
    out_shape = tuple(jax.ShapeDtypeStruct(a.shape if scatter else (NDEV,) + a.shape, a.dtype) for a in arrays)
    return pl.pallas_call(
        body, in_specs=[hbm] * n, out_specs=tuple([hbm] * n), out_shape=out_shape,
        scratch_shapes=[pltpu.SemaphoreType.DMA((n, NDEV - 1)), pltpu.SemaphoreType.DMA((n, NDEV - 1)),
                        pltpu.SemaphoreType.DMA((n,))],
        name=name)(*arrays)


def _adamw(w, m, v, parts, *, name):
    R, C = w.shape
    npart = parts.shape[0]
    cap = max(SUB, ((1 << 18) // C) // SUB * SUB)
    tr = _tile(R, cap, SUB)

    def body(w_ref, m_ref, v_ref, p_ref, g_ref, d_ref, mo_ref, vo_ref):
        g = p_ref[0]
        for k in range(1, npart):
            g = g + p_ref[k]
        mn = ADAM_B1 * m_ref[...] + (1.0 - ADAM_B1) * g
        vn = ADAM_B2 * v_ref[...] + (1.0 - ADAM_B2) * (g * g)
        m_hat = mn / (1.0 - ADAM_B1 ** ADAM_STEP)
        v_hat = vn / (1.0 - ADAM_B2 ** ADAM_STEP)
        g_ref[...] = g
        d_ref[...] = -ADAM_LR * (m_hat / (jnp.sqrt(v_hat) + ADAM_EPS) + ADAM_WD * w_ref[...])
        mo_ref[...] = mn
        vo_ref[...] = vn

    spec = pl.BlockSpec((tr, C), lambda i: (i, 0))
    oshape = jax.ShapeDtypeStruct((R, C), F32)
    return pl.pallas_call(
        body, grid=(R // tr,), in_specs=[spec] * 3 + [pl.BlockSpec((npart, tr, C), lambda i: (0, i, 0))],
        out_specs=(spec,) * 4, out_shape=(oshape,) * 4, compiler_params=_cp(1), name=name)(w, m, v, parts)


def _adamw_layer(l, w, m, v, parts, prev, *, name):
    L, R, C = w.shape
    npart = parts.shape[0]
    cap = max(SUB, ((1 << 18) // C) // SUB * SUB)
    tr = _tile(R, cap, SUB)
    nprev = 0 if prev is None else 4

    def body(*refs):
        w_ref, m_ref, v_ref, p_ref = refs[:4]
        g_ref, d_ref, mo_ref, vo_ref = refs[4 + nprev:]
        g = p_ref[0]
        for k in range(1, npart):
            g = g + p_ref[k]
        mn = ADAM_B1 * m_ref[...] + (1.0 - ADAM_B1) * g
        vn = ADAM_B2 * v_ref[...] + (1.0 - ADAM_B2) * (g * g)
        m_hat = mn / (1.0 - ADAM_B1 ** ADAM_STEP)
        v_hat = vn / (1.0 - ADAM_B2 ** ADAM_STEP)
        g_ref[...] = g
        d_ref[...] = -ADAM_LR * (m_hat / (jnp.sqrt(v_hat) + ADAM_EPS) + ADAM_WD * w_ref[...])
        mo_ref[...] = mn
        vo_ref[...] = vn

    spec = pl.BlockSpec((None, tr, C), lambda i: (l, i, 0))
    oshape = jax.ShapeDtypeStruct((L, R, C), F32)
    return pl.pallas_call(
        body, grid=(R // tr,),
        in_specs=[spec] * 3 + [pl.BlockSpec((npart, tr, C), lambda i: (0, i, 0))]
        + [pl.BlockSpec(memory_space=pl.ANY)] * nprev,
        out_specs=(spec,) * 4, out_shape=(oshape,) * 4,
        input_output_aliases={4 + k: k for k in range(nprev)},
        compiler_params=_cp(1), name=name)(w, m, v, parts, *(prev or ()))


_HBM = pl.BlockSpec(memory_space=pltpu.HBM)
_SEM = pl.BlockSpec(memory_space=pltpu.SEMAPHORE)
_EFFECT = pltpu.SideEffectType.DATAFLOW_SIDE_EFFECTING


def _peers():
    x, y, c = lax.axis_index("x"), lax.axis_index("y"), lax.axis_index("c")
    out = []
    for k in range(1, NDEV):
        px = 1 - x if k & 4 else x
        py = 1 - y if k & 2 else y
        pc = 1 - c if k & 1 else c
        out.append((k - 1, (px, py, pc), 4 * px + 2 * py + pc))
    return 4 * x + 2 * y + c, out


def _xchg_start(arrays, *, scatter, name):
    n = len(arrays)
    lands = [lax.empty(a.shape if scatter else (NDEV,) + a.shape, a.dtype) for a in arrays]

    def body(*refs):
        ins, lnd = refs[:n], refs[n:2 * n]
        send_sems, recv_sems = refs[2 * n], refs[2 * n + 1]
        token = refs[-1]
        me, peers = _peers()
        for k, dev, peer in peers:
            for i in range(n):
                pltpu.make_async_remote_copy(
                    src_ref=ins[i].at[peer] if scatter else ins[i], dst_ref=lnd[i].at[me],
                    send_sem=send_sems.at[i * (NDEV - 1) + k], recv_sem=recv_sems.at[i * (NDEV - 1) + k],
                    device_id=dev, device_id_type=pl.DeviceIdType.MESH).start()
        token[...] = jnp.zeros_like(token)

    sems = pltpu.SemaphoreType.DMA((n * (NDEV - 1),))
    res = pl.pallas_call(
        body, name=name,
        out_shape=(sems, sems) + tuple(pltpu.HBM(t.shape, t.dtype) for t in list(arrays) + lands)
        + (jax.ShapeDtypeStruct((SUB, LANE), F32),),
        in_specs=[_HBM] * (2 * n), out_specs=(_SEM, _SEM) + (_HBM,) * (2 * n) + (pl.BlockSpec(memory_space=pltpu.VMEM),),
        input_output_aliases={i: 2 + i for i in range(2 * n)},
        compiler_params=pltpu.CompilerParams(has_side_effects=_EFFECT),
    )(*[pltpu.with_memory_space_constraint(t, pltpu.HBM) for t in list(arrays) + lands])
    return dict(send=res[0], recv=res[1], thru=list(res[2:2 + 2 * n]), token=res[-1], scatter=scatter, n=n)


def _xchg_wait(handle, after, *, name):
    n, scatter = handle["n"], handle["scatter"]
    thru = handle["thru"]

    def body(*refs):
        ins, lnd = refs[:n], refs[n:2 * n]
        send_sems, recv_sems = refs[2 * n], refs[2 * n + 1]
        me, peers = _peers()
        for k, dev, peer in peers:
            for i in range(n):
                cp = pltpu.make_async_remote_copy(
                    src_ref=ins[i].at[peer] if scatter else ins[i], dst_ref=lnd[i].at[peer],
                    send_sem=send_sems.at[i * (NDEV - 1) + k], recv_sem=recv_sems.at[i * (NDEV - 1) + k],
                    device_id=dev, device_id_type=pl.DeviceIdType.MESH)
                cp.wait_send()
                cp.wait_recv()

    res = pl.pallas_call(
        body, name=name, out_shape=tuple(pltpu.HBM(t.shape, t.dtype) for t in thru),
        in_specs=[_HBM] * (2 * n) + [_SEM, _SEM, pl.BlockSpec(memory_space=pl.ANY)], out_specs=(_HBM,) * (2 * n),
        input_output_aliases={i: i for i in range(2 * n)},
        compiler_params=pltpu.CompilerParams(has_side_effects=_EFFECT),
    )(*thru, handle["send"], handle["recv"], after)
    return list(res[n:])


def _pad_heads(w, nh, hd, axis=-1):
    axis = axis % w.ndim
    shp = w.shape
    w = w.reshape(shp[:axis] + (nh, hd) + shp[axis + 1:])
    pads = [(0, 0)] * w.ndim
    pads[axis + 1] = (0, LANE - hd)
    return jnp.pad(w, pads).reshape(shp[:axis] + (nh * LANE,) + shp[axis + 1:])


def _unpad_heads(w, nh, hd, axis=-1):
    axis = axis % w.ndim
    shp = w.shape
    w = w.reshape(shp[:axis] + (nh, LANE) + shp[axis + 1:])
    w = lax.slice_in_dim(w, 0, hd, axis=axis + 1)
    return w.reshape(shp[:axis] + (nh * hd,) + shp[axis + 1:])


_O_DT = SSD_INNER + SSD_XBC
_O_CQ = _O_DT + SSD_HEADS
_O_CKV = _O_CQ + MLA_QR
_O_KR = _O_CKV + MLA_KVR
_O_SQ = _O_KR + MLA_ROPE
_O_SK = _O_SQ + SWA_HEADS * SWA_HD
_O_SV = _O_SK + SWA_KV * SWA_HD


def _w_in_to_padded(w):
    z, xbc, dt = w[..., :SSD_INNER], w[..., SSD_INNER:_O_DT], w[..., _O_DT:_O_CQ]
    cq, ckv, kr = w[..., _O_CQ:_O_CKV], w[..., _O_CKV:_O_KR], w[..., _O_KR:_O_SQ]
    sq, sk, sv = w[..., _O_SQ:_O_SK], w[..., _O_SK:_O_SV], w[..., _O_SV:]
    zeros = lambda n: jnp.zeros(w.shape[:-1] + (n,), w.dtype)
    misc = jnp.concatenate([dt, zeros(ROPE_LANE - SSD_HEADS), kr, zeros(LANE - ROPE_LANE - MLA_ROPE)], axis=-1)
    return jnp.concatenate([xbc, z, cq, ckv, misc, _pad_heads(sq, SWA_HEADS, SWA_HD),
                            _pad_heads(sk, SWA_KV, SWA_HD), _pad_heads(sv, SWA_KV, SWA_HD)], axis=-1)


def _w_in_from_padded(g):
    xbc, z, cq, ckv = g[..., C_XBC:C_Z], g[..., C_Z:C_CQ], g[..., C_CQ:C_CKV], g[..., C_CKV:C_MISC]
    dt, kr = g[..., C_MISC:C_MISC + SSD_HEADS], g[..., C_MISC + ROPE_LANE:C_MISC + ROPE_LANE + MLA_ROPE]
    sq = _unpad_heads(g[..., C_SQ:C_SK], SWA_HEADS, SWA_HD)
    sk = _unpad_heads(g[..., C_SK:C_SV], SWA_KV, SWA_HD)
    sv = _unpad_heads(g[..., C_SV:], SWA_KV, SWA_HD)
    return jnp.concatenate([z, xbc, dt, cq, ckv, kr, sq, sk, sv], axis=-1)


def _w_out_to_padded(w):
    a = SSD_INNER
    b = a + MLA_HEADS * MLA_V
    return jnp.concatenate([w[..., :a, :], _pad_heads(w[..., a:b, :], MLA_HEADS, MLA_V, axis=-2),
                            _pad_heads(w[..., b:, :], SWA_HEADS, SWA_HD, axis=-2)], axis=-2)


def _w_out_from_padded(g):
    a = SSD_INNER
    b = a + MLA_HEADS * LANE
    return jnp.concatenate([g[..., :a, :], _unpad_heads(g[..., a:b, :], MLA_HEADS, MLA_V, axis=-2),
                            _unpad_heads(g[..., b:, :], SWA_HEADS, SWA_HD, axis=-2)], axis=-2)


def _w_ukv_to_padded(w):
    w4 = w.reshape(w.shape[:-1] + (MLA_HEADS, MLA_NOPE + MLA_V))
    flat = lambda t: t.reshape(w.shape[:-1] + (MLA_HEADS * t.shape[-1],))
    return jnp.concatenate([_pad_heads(flat(w4[..., :MLA_NOPE]), MLA_HEADS, MLA_NOPE),
                            _pad_heads(flat(w4[..., MLA_NOPE:]), MLA_HEADS, MLA_V)], axis=-1)


def _w_ukv_from_padded(g):
    hw = MLA_HEADS * LANE
    gk = _unpad_heads(g[..., :hw], MLA_HEADS, MLA_NOPE).reshape(g.shape[:-1] + (MLA_HEADS, MLA_NOPE))
    gv = _unpad_heads(g[..., hw:], MLA_HEADS, MLA_V).reshape(g.shape[:-1] + (MLA_HEADS, MLA_V))
    return jnp.concatenate([gk, gv], axis=-1).reshape(g.shape[:-1] + (MLA_HEADS * (MLA_NOPE + MLA_V),))


def _pad_lane(v):
    return jnp.pad(v, [(0, 0)] * (v.ndim - 1) + [(0, LANE - v.shape[-1])])


def _rope_tables(positions):
    inv_freq = ROPE_THETA ** (-jnp.arange(0, MLA_ROPE, 2, dtype=F32) / MLA_ROPE)
    ang = positions.astype(F32).reshape(-1, 1) * inv_freq
    cos, sin = jnp.cos(ang), jnp.sin(ang)
    S = ang.shape[0]
    one, zero = jnp.ones((S, ROPE_LANE), F32), jnp.zeros((S, ROPE_LANE), F32)
    tail1, tail0 = jnp.ones((S, LANE - ROPE_LANE - MLA_ROPE), F32), jnp.zeros((S, LANE - ROPE_LANE - MLA_ROPE), F32)
    z16 = jnp.zeros_like(sin)
    return (jnp.concatenate([one, cos, cos, tail1], axis=1), jnp.concatenate([zero, -sin, z16, tail0], axis=1),
            jnp.concatenate([zero, z16, sin, tail0], axis=1))


def _layer_fwd(l, x_in, f_prev, gate_prev, mod, P, tabs):
    sh1, sc1, g1, sh2, sc2, g2 = [mod[k:k + 1] for k in range(6)]
    tag = f"l{l}_"
    if f_prev is None:
        x0 = x_in
        h1 = _norm_fwd(x0, P["n1g"], sc1, sh1, name=tag + "norm1")
    else:
        x0, h1 = _norm_fwd(x_in, P["n1g"], sc1, sh1, f=f_prev, gate=gate_prev, name=tag + "norm1")
    proj = _mm(h1, P["w_in"], name=tag + "proj")
    y_ssd, hin = _ssd_fwd(proj, P["ssd_cw"], P["ssd_cb"], P["dtb"], P["alog"], P["dsk"],
                          P["ssd_ng"], name=tag + "ssd")
    q, k, v = _mla_prep_fwd(proj, *tabs, P["gq"], P["gkv"], P["w_uq"], P["w_ukv"], name=tag + "mla_prep")
    o_mla, lse_mla = _attn_fwd(q, k, v, name=tag + "mla_attn")
    o_swa, lse_swa = _swa_fwd(proj, P["sinks"], name=tag + "swa")
    ycat = jnp.concatenate([y_ssd.astype(_ACT), o_mla.astype(_ACT), o_swa.astype(_ACT)], axis=1)
    y = _mm(ycat, P["w_out"], name=tag + "out")
    x1, h2 = _norm_fwd(x0, P["n2g"], sc2, sh2, f=y, gate=g1, name=tag + "norm2")
    up = _mm(h2, P["w_up"], name=tag + "up")
    act = _ffn_act_fwd(up, P["fcw"], P["fcb"], name=tag + "ffn_act")
    f = _mm(act, P["w_down"], name=tag + "down")
    saved = dict(x0=x0, h1=h1, proj=proj, hin=hin, q=q, k=k, v=v, o_mla=o_mla, lse_mla=lse_mla, o_swa=o_swa,
                 lse_swa=lse_swa, ycat=ycat, y=y, x1=x1, h2=h2, up=up, act=act, f=f, mod=mod)
    return x1, f, g2, saved


def _layer_bwd(l, dxo, sv, P, tabs):
    mod = sv["mod"]
    sh1, sc1, g1, sh2, sc2, g2 = [mod[k:k + 1] for k in range(6)]
    tag = f"l{l}_b_"
    G = {}
    df, dg2 = _gate_bwd(dxo, sv["f"], g2, name=tag + "gate2")
    dact = _mm(df, P["w_down"], tb=True, name=tag + "dact")
    G["w_down"] = _mm(sv["act"], df, ta=True, name=tag + "dw_down")
    du = _ffn_act_bwd(sv["up"], dact, P["fcw"], P["fcb"], name=tag + "ffn_act")
    dup, G["fcw"], G["fcb"] = _ffn_conv_bwd(du, sv["up"], P["fcw"], name=tag + "ffn_conv")
    dh2 = _mm(dup, P["w_up"], tb=True, name=tag + "dh2")
    G["w_up"] = _mm(sv["h2"], dup, ta=True, name=tag + "dw_up")
    dx1, G["n2g"], dsc2, dsh2 = _norm_bwd(dh2, sv["x1"], dxo, P["n2g"], sc2, name=tag + "norm2")
    dy, dg1 = _gate_bwd(dx1, sv["y"], g1, name=tag + "gate1")
    dycat = _mm(dy, P["w_out"], tb=True, name=tag + "dycat")
    G["w_out"] = _mm(sv["ycat"], dy, ta=True, name=tag + "dw_out")
    proj = sv["proj"]
    (dpre, dz, dmisc_ssd, G["ssd_cw"], G["ssd_cb"], G["dtb"], G["alog"], G["dsk"], G["ssd_ng"]) = _ssd_bwd(
        proj, dycat, sv["hin"], P["ssd_cw"], P["ssd_cb"], P["dtb"], P["alog"], P["dsk"],
        P["ssd_ng"], name=tag + "ssd")
    att = (sv["q"], sv["k"], sv["v"], sv["o_mla"], sv["lse_mla"], dycat)
    dq = _attn_bwd_dq(*att, name=tag + "mla_dq")
    dk, dv = _attn_bwd_dkv(*att, name=tag + "mla_dkv")
    dcq, dckv, dmisc, dqh, dkv, qn, kvn, G["gq"], G["gkv"] = _mla_prep_bwd(
        proj, dq, dk, dv, dmisc_ssd, *tabs, P["gq"], P["gkv"], P["w_uq"], P["w_ukv"], name=tag + "mla_prep")
    G["w_uq"] = _mm(qn, dqh, ta=True, name=tag + "dw_uq")
    G["w_ukv"] = _mm(kvn, dkv, ta=True, name=tag + "dw_ukv")
    dsq, dsk_, dsv_, G["sinks"] = _swa_bwd(proj, sv["o_swa"], sv["lse_swa"], dycat, P["sinks"], name=tag + "swa")
    dproj = jnp.concatenate([dpre, dz, dcq, dckv, dmisc, dsq, dsk_, dsv_], axis=1)
    dh1 = _mm(dproj, P["w_in"], tb=True, name=tag + "dh1")
    G["w_in"] = _mm(sv["h1"], dproj, ta=True, name=tag + "dw_in")
    dx0, G["n1g"], dsc1, dsh1 = _norm_bwd(dh1, sv["x0"], dx1, P["n1g"], sc1, name=tag + "norm1")
    G["mod"] = jnp.concatenate([dsh1, dsc1, dg1, dsh2, dsc2, dg2], axis=0)
    return dx0, G


def _local_step(x, tgt, mods, get_params, tabs, final_g, on_grads):
    saved, params = [], []
    xin, f, gate = x, None, None
    for l in range(DEPTH):
        params.append(get_params(l, x if f is None else f))
        xin, f, gate, sv = _layer_fwd(l, xin, f, gate, mods[l], params[l], tabs)
        saved.append(sv)
    loss, dx, dfinal = _final_loss(xin, f, gate, final_g, tgt, name="final_loss")
    for l in reversed(range(DEPTH)):
        dx, G = _layer_bwd(l, dx, saved[l], params[l], tabs)
        token = on_grads(l, G)
        if token is not None and l > 0:
            saved[l - 1]["mod"] = saved[l - 1]["mod"] + token
    return loss[0, 0], dx, dfinal


_WEIGHTS = ['ada_w', 'ada_b', 'norm1_g', 'norm2_g', 'w_in', 'ssd_conv_w', 'ssd_conv_b', 'ssd_dt_bias', 'ssd_a_log',
            'ssd_d', 'ssd_norm_g', 'mla_q_norm_g', 'mla_w_uq', 'mla_kv_norm_g', 'mla_w_ukv', 'swa_sinks', 'w_out',
            'ffn_w_up', 'ffn_conv_w', 'ffn_conv_b', 'ffn_w_down', 'final_norm_g']
_INPUTS = ['x', 'c', 'positions'] + _WEIGHTS + ['loss_target'] + ['m_' + n for n in _WEIGHTS] + ['v_' + n for n in _WEIGHTS]
_SMALL = [('ada_b', 'mod'), ('norm1_g', 'n1g'), ('norm2_g', 'n2g'), ('ssd_conv_b', 'ssd_cb'), ('ssd_dt_bias', 'dtb'),
          ('ssd_a_log', 'alog'), ('ssd_d', 'dsk'), ('ssd_norm_g', 'ssd_ng'), ('mla_q_norm_g', 'gq'),
          ('mla_kv_norm_g', 'gkv'), ('swa_sinks', 'sinks'), ('ffn_conv_b', 'fcb')]
_SHARDED = [('w_in', 'w_in', 2), ('ssd_conv_w', 'ssd_cw', 2), ('mla_w_uq', 'w_uq', 2), ('mla_w_ukv', 'w_ukv', 2),
            ('w_out', 'w_out', 1), ('ffn_w_up', 'w_up', 2), ('ffn_conv_w', 'fcw', 2), ('ffn_w_down', 'w_down', 1)]


def _pack_small(per_layer, final):
    parts = []
    for name, _ in _SMALL:
        v = per_layer[name]
        v = v.reshape(DEPTH, -1)
        pad = (-v.shape[1]) % LANE
        parts.append(jnp.pad(v, ((0, 0), (0, pad))).reshape(-1))
    parts.append(final.reshape(-1))
    return jnp.concatenate(parts).reshape(-1, LANE)


def _unpack_small(packed, shapes):
    flat = packed.reshape(-1)
    out, off = {}, 0
    for name, _ in _SMALL:
        n = math.prod(shapes[name][1:])
        npad = n + (-n) % LANE
        out[name] = flat[off:off + DEPTH * npad].reshape(DEPTH, npad)[:, :n].reshape(shapes[name])
        off += DEPTH * npad
    out['final_norm_g'] = flat[off:off + D]
    return out


def _shard_major(g, axis):
    shp = g.shape
    g = g.reshape(shp[:axis] + (NDEV, shp[axis] // NDEV) + shp[axis + 1:])
    return jnp.moveaxis(g, axis, 0)


def _unshard(g, axis):
    g = jnp.moveaxis(g, 0, axis)
    shp = g.shape
    return g.reshape(shp[:axis] + (shp[axis] * shp[axis + 1],) + shp[axis + 2:])


def kernel(x, c, positions, ada_w, ada_b, norm1_g, norm2_g, w_in, ssd_conv_w, ssd_conv_b, ssd_dt_bias, ssd_a_log, ssd_d, ssd_norm_g, mla_q_norm_g, mla_w_uq, mla_kv_norm_g, mla_w_ukv, swa_sinks, w_out, ffn_w_up, ffn_conv_w, ffn_conv_b, ffn_w_down, final_norm_g, loss_target, m_ada_w, m_ada_b, m_norm1_g, m_norm2_g, m_w_in, m_ssd_conv_w, m_ssd_conv_b, m_ssd_dt_bias, m_ssd_a_log, m_ssd_d, m_ssd_norm_g, m_mla_q_norm_g, m_mla_w_uq, m_mla_kv_norm_g, m_mla_w_ukv, m_swa_sinks, m_w_out, m_ffn_w_up, m_ffn_conv_w, m_ffn_conv_b, m_ffn_w_down, m_final_norm_g, v_ada_w, v_ada_b, v_norm1_g, v_norm2_g, v_w_in, v_ssd_conv_w, v_ssd_conv_b, v_ssd_dt_bias, v_ssd_a_log, v_ssd_d, v_ssd_norm_g, v_mla_q_norm_g, v_mla_w_uq, v_mla_kv_norm_g, v_mla_w_ukv, v_swa_sinks, v_w_out, v_ffn_w_up, v_ffn_conv_w, v_ffn_conv_b, v_ffn_w_down, v_final_norm_g):
    a = dict(zip(_INPUTS, (x, c, positions, ada_w, ada_b, norm1_g, norm2_g, w_in, ssd_conv_w, ssd_conv_b, ssd_dt_bias, ssd_a_log, ssd_d, ssd_norm_g, mla_q_norm_g, mla_w_uq, mla_kv_norm_g, mla_w_ukv, swa_sinks, w_out, ffn_w_up, ffn_conv_w, ffn_conv_b, ffn_w_down, final_norm_g, loss_target, m_ada_w, m_ada_b, m_norm1_g, m_norm2_g, m_w_in, m_ssd_conv_w, m_ssd_conv_b, m_ssd_dt_bias, m_ssd_a_log, m_ssd_d, m_ssd_norm_g, m_mla_q_norm_g, m_mla_w_uq, m_mla_kv_norm_g, m_mla_w_ukv, m_swa_sinks, m_w_out, m_ffn_w_up, m_ffn_conv_w, m_ffn_conv_b, m_ffn_w_down, m_final_norm_g, v_ada_w, v_ada_b, v_norm1_g, v_norm2_g, v_w_in, v_ssd_conv_w, v_ssd_conv_b, v_ssd_dt_bias, v_ssd_a_log, v_ssd_d, v_ssd_norm_g, v_mla_q_norm_g, v_mla_w_uq, v_mla_kv_norm_g, v_mla_w_ukv, v_swa_sinks, v_w_out, v_ffn_w_up, v_ffn_conv_w, v_ffn_conv_b, v_ffn_w_down, v_final_norm_g)))
    axes = ("x", "y", "c")
    me = 4 * lax.axis_index("x") + 2 * lax.axis_index("y") + lax.axis_index("c")
    ncol = ada_w.shape[-1]

    mxu_names = ('w_in', 'mla_w_uq', 'mla_w_ukv', 'w_out', 'ffn_w_up', 'ffn_w_down')
    own = [[a[n][l].astype(_MXU) if n in mxu_names else a[n][l] for n, _, _ in _SHARDED] for l in range(DEPTH)]
    gathers = [_xchg_start(own[l], scatter=False, name=f"gather_start{l}") for l in range(DEPTH)]

    c_all = _exchange([c], scatter=False, name="gather_c")[0]
    c_act = _silu_call(c_all.reshape(NDEV, D), name="c_act")
    mod_part = jnp.stack([_mm(c_act, ada_w[l], name=f"mod{l}") for l in range(DEPTH)])
    mod_all = _exchange([mod_part], scatter=False, name="gather_mod")[0]
    mod_mine = lax.dynamic_index_in_dim(mod_all, me, axis=2, keepdims=False)
    mods = (jnp.moveaxis(mod_mine, 0, 1).reshape(DEPTH, 6 * D) + ada_b).reshape(DEPTH, 6, D)
    tabs = _rope_tables(positions)

    def place_own(landed, mine):
        return [lax.dynamic_update_index_in_dim(t, o, me, 0) for t, o in zip(landed, mine)]

    def get_params(l, after):
        landed = place_own(_xchg_wait(gathers[l], after, name=f"gather_wait{l}"), own[l])
        full = {n: _unshard(g, ax - 1) for (n, _, ax), g in zip(_SHARDED, landed)}
        vec = lambda t: t[l].reshape(1, -1)
        return dict(
            w_in=_w_in_to_padded(full['w_in']), w_out=_w_out_to_padded(full['w_out']), w_up=full['ffn_w_up'],
            w_down=full['ffn_w_down'], w_uq=_pad_heads(full['mla_w_uq'], MLA_HEADS, MLA_NOPE + MLA_ROPE),
            w_ukv=_w_ukv_to_padded(full['mla_w_ukv']), ssd_cw=full['ssd_conv_w'], fcw=full['ffn_conv_w'],
            ssd_cb=vec(ssd_conv_b), dtb=vec(_pad_lane(ssd_dt_bias)), alog=vec(_pad_lane(ssd_a_log)),
            dsk=vec(_pad_lane(ssd_d)), ssd_ng=vec(ssd_norm_g), gq=vec(mla_q_norm_g), gkv=vec(mla_kv_norm_g),
            sinks=vec(_pad_lane(swa_sinks)), fcb=vec(ffn_conv_b), n1g=vec(norm1_g), n2g=vec(norm2_g))

    unpad = dict(w_in=_w_in_from_padded, w_out=_w_out_from_padded, w_ukv=_w_ukv_from_padded,
                 w_uq=lambda g: _unpad_heads(g, MLA_HEADS, MLA_NOPE + MLA_ROPE))
    grads, sent, scatters = [None] * DEPTH, [None] * DEPTH, [None] * DEPTH

    def on_grads(l, G):
        grads[l] = G
        sent[l] = [_shard_major(unpad.get(key, lambda g: g)(G[key]), ax - 1) for _, key, ax in _SHARDED]
        scatters[l] = _xchg_start(sent[l], scatter=True, name=f"scatter_start{l}")
        return scatters[l]["token"][0, 0]

    mods = mods + sum(g["token"][0, 0] for g in gathers)
    loss, dx, dfinal = _local_step(x[0], loss_target[0], mods, get_params, tabs, final_norm_g.reshape(1, D), on_grads)
    loss = lax.psum(loss, axes)

    stack = lambda key: jnp.stack([grads[l][key] for l in range(DEPTH)])
    small_g = {name: stack(key).reshape(DEPTH, -1) for name, key in _SMALL}
    small_parts = _exchange([_pack_small(small_g, dfinal)], scatter=False, name="gather_small")[0]

    out_g, out_d, out_m, out_v = {}, {}, {}, {}
    chain = {name: None for name, _, _ in _SHARDED}
    for l in reversed(range(DEPTH)):
        landed = _xchg_wait(scatters[l], dx, name=f"scatter_wait{l}")
        parts = place_own(landed, [lax.dynamic_index_in_dim(t, me, 0, keepdims=False) for t in sent[l]])
        for (name, _, _), pv in zip(_SHARDED, parts):
            chain[name] = _adamw_layer(l, a[name], a['m_' + name], a['v_' + name], pv, chain[name],
                                       name=f"adamw_{name}{l}")
    for name, _, _ in _SHARDED:
        out_g[name], out_d[name], out_m[name], out_v[name] = chain[name]

    def update(name, wv, mv, vv, pv):
        shp = wv.shape
        r = lambda t: t.reshape((-1, shp[-1]))
        res = _adamw(r(wv), r(mv), r(vv), pv.reshape((pv.shape[0], -1, shp[-1])), name="adamw_" + name)
        out_g[name], out_d[name], out_m[name], out_v[name] = [t.reshape(shp) for t in res]

    n_ada = DEPTH * 6 * D // LANE
    dmod_all = small_parts[:, :n_ada].reshape(NDEV, DEPTH, 6 * D)
    dmod_mine = lax.dynamic_slice_in_dim(dmod_all, me * ncol, ncol, axis=2)
    g_ada = jnp.stack([_mm(c_act, dmod_mine[:, l], ta=True, name=f"dw_ada{l}") for l in range(DEPTH)])
    update('ada_w', ada_w, m_ada_w, v_ada_w, g_ada[None])
    shapes = {n: a[n].shape for n, _ in _SMALL}
    pk = lambda pre: _pack_small({n: a[pre + n] for n, _ in _SMALL}, a[pre + 'final_norm_g'])
    res = _adamw(pk(''), pk('m_'), pk('v_'), small_parts, name="adamw_small")
    for dst, t in zip((out_g, out_d, out_m, out_v), res):
        dst.update(_unpack_small(t, shapes))

    outs = [loss, dx[None]]
    for dct in (out_g, out_d, out_m, out_v):
        outs += [dct[n] for n in _WEIGHTS]
    return tuple(outs)
```

```python
import functools
import math

import jax
import jax.numpy as jnp
from jax import lax
from jax.experimental import pallas as pl
from jax.experimental.pallas import tpu as pltpu

F32 = jnp.float32
_MXU = jnp.bfloat16
_ACT = jnp.bfloat16
_HI = lax.Precision.HIGHEST
EPS = 1e-6
NDEV = 8
DEPTH = 4
D = 1024
LANE = 128
SUB = 8
VMEM_LIMIT = 56 * 1024 * 1024

SSD_INNER, SSD_STATE, SSD_HEADS, SSD_GROUPS, SSD_CHUNK, SSD_CONV = 512, 128, 8, 2, 128, 4
SSD_XBC = SSD_INNER + 2 * SSD_GROUPS * SSD_STATE
MLA_HEADS, MLA_NOPE, MLA_ROPE, MLA_V, MLA_QR, MLA_KVR = 4, 64, 32, 64, 256, 128
SWA_HEADS, SWA_KV, SWA_HD, WINDOW = 4, 2, 64, 128
D_FF, FFN_CONV = 2816, 3
D_IN = 2472
ROPE_THETA = 10000.0
C_XBC, C_Z, C_CQ, C_CKV, C_MISC, C_SQ, C_SK, C_SV, D_INP = 0, 1024, 1536, 1792, 1920, 2048, 2560, 2816, 3072
ROPE_LANE = 64
D_MIXP = 1536

ADAM_LR, ADAM_B1, ADAM_B2, ADAM_EPS, ADAM_WD, ADAM_STEP = 0.001, 0.9, 0.999, 1e-08, 0.01, 10

TS_ROW = 512
TS_FFN = 256
TQ_ATT = 512
TS_SWA = 512


def _tile(n, cap, q=LANE):
    best = None
    for t in range(q, min(n, cap) + 1, q):
        if n % t == 0:
            best = t
    return n if best is None else best


def _cp(ngrid):
    return pltpu.CompilerParams(dimension_semantics=("arbitrary",) * ngrid, vmem_limit_bytes=VMEM_LIMIT)


def _dot(a, b):
    return jnp.dot(a.astype(_MXU), b.astype(_MXU), preferred_element_type=F32)


def _dot_nt(a, b):
    return lax.dot_general(a.astype(_MXU), b.astype(_MXU), (((1,), (1,)), ((), ())), preferred_element_type=F32)


def _dot_tn(a, b):
    return jnp.dot(a.T.astype(_MXU), b.astype(_MXU), preferred_element_type=F32)


def _sigmoid(x):
    return 1.0 / (1.0 + jnp.exp(-x))


def _silu(x):
    return x * _sigmoid(x)


def _dsilu(x):
    s = _sigmoid(x)
    return s * (1.0 + x * (1.0 - s))


def _softplus(x):
    u = jnp.exp(-jnp.abs(x))
    w = 1.0 + u
    log1p = jnp.where(w == 1.0, u, jnp.log(w) * u / jnp.where(w == 1.0, 1.0, w - 1.0))
    return jnp.maximum(x, 0.0) + log1p


def _colsum(x):
    return jnp.sum(x, axis=0, keepdims=True)


def _rowsum(x):
    return jnp.sum(x, axis=1, keepdims=True)


def _shift_down(t, halo, j):
    if j == 0:
        return t
    n = t.shape[0]
    rolled = pltpu.roll(t, j, 0)
    row = lax.broadcasted_iota(jnp.int32, (SUB, t.shape[1]), 0)
    first = jnp.where(row < j, pltpu.roll(halo, j, 0), rolled[:SUB])
    return jnp.concatenate([first, rolled[SUB:]], axis=0) if n > SUB else first


def _shift_up(t, halo, j):
    if j == 0:
        return t
    n = t.shape[0]
    rolled = pltpu.roll(t, n - j, 0)
    row = lax.broadcasted_iota(jnp.int32, (SUB, t.shape[1]), 0)
    last = jnp.where(row >= SUB - j, pltpu.roll(halo, SUB - j, 0), rolled[n - SUB:])
    return jnp.concatenate([rolled[:n - SUB], last], axis=0) if n > SUB else last


def _mm(a, b, *, ta=False, tb=False, out_dtype=F32, name):
    if ta:
        K, M = a.shape
    else:
        M, K = a.shape
    if tb:
        N, K2 = b.shape
    else:
        K2, N = b.shape
    assert K == K2, (a.shape, b.shape, ta, tb)
    tm, tn, tk = _tile(M, 1024), _tile(N, 1408), _tile(K, 1024)
    nk = K // tk
    dn = (((0 if ta else 1,), (1 if tb else 0,)), ((), ()))

    def body(a_ref, b_ref, o_ref, acc_ref):
        k = pl.program_id(2)
        part = lax.dot_general(a_ref[...].astype(_MXU), b_ref[...].astype(_MXU), dn, preferred_element_type=F32)

        @pl.when(k == 0)
        def _():
            acc_ref[...] = part

        @pl.when(k > 0)
        def _():
            acc_ref[...] += part

        @pl.when(k == nk - 1)
        def _():
            o_ref[...] = acc_ref[...].astype(out_dtype)

    a_spec = pl.BlockSpec((tk, tm), lambda i, j, k: (k, i)) if ta else pl.BlockSpec((tm, tk), lambda i, j, k: (i, k))
    b_spec = pl.BlockSpec((tn, tk), lambda i, j, k: (j, k)) if tb else pl.BlockSpec((tk, tn), lambda i, j, k: (k, j))
    return pl.pallas_call(
        body, grid=(M // tm, N // tn, nk), in_specs=[a_spec, b_spec],
        out_specs=pl.BlockSpec((tm, tn), lambda i, j, k: (i, j)),
        out_shape=jax.ShapeDtypeStruct((M, N), out_dtype),
        scratch_shapes=[pltpu.VMEM((tm, tn), F32)], compiler_params=_cp(3), name=name)(a, b)


def _row(ts, w, col=0):
    return pl.BlockSpec((ts, w), lambda i: (i, col))


def _vec(w, r=1):
    return pl.BlockSpec((r, w), lambda i: (0, 0))


def _silu_call(x, name):
    def body(x_ref, o_ref):
        o_ref[...] = _silu(x_ref[...])
    return pl.pallas_call(body, out_shape=jax.ShapeDtypeStruct(x.shape, F32), name=name)(x)


def _norm_fwd(x, g, sc, sh, *, f=None, gate=None, name):
    S, dm = x.shape
    ts = _tile(S, TS_ROW, SUB)
    res = f is not None

    def body(*refs):
        if res:
            x_ref, f_ref, gate_ref, g_ref, sc_ref, sh_ref, xo_ref, h_ref = refs
            xv = x_ref[...] + gate_ref[...] * f_ref[...]
            xo_ref[...] = xv
        else:
            x_ref, g_ref, sc_ref, sh_ref, h_ref = refs
            xv = x_ref[...]
        rstd = lax.rsqrt(jnp.mean(xv * xv, axis=-1, keepdims=True) + EPS)
        h_ref[...] = ((xv * rstd) * g_ref[...] * (1.0 + sc_ref[...]) + sh_ref[...]).astype(_ACT)

    ins = [x] + ([f, gate] if res else []) + [g, sc, sh]
    in_specs = [_row(ts, dm)] + ([_row(ts, dm), _vec(dm)] if res else []) + [_vec(dm)] * 3
    h_shape = jax.ShapeDtypeStruct((S, dm), _ACT)
    if res:
        out_shape, out_specs = (jax.ShapeDtypeStruct((S, dm), F32), h_shape), (_row(ts, dm), _row(ts, dm))
    else:
        out_shape, out_specs = h_shape, _row(ts, dm)
    return pl.pallas_call(body, grid=(S // ts,), in_specs=in_specs, out_specs=out_specs, out_shape=out_shape,
                          compiler_params=_cp(1), name=name)(*ins)


def _norm_bwd(dh, x, dres, g, sc, *, name):
    S, dm = x.shape
    ts = _tile(S, TS_ROW, SUB)

    def body(dh_ref, x_ref, dres_ref, g_ref, sc_ref, dx_ref, dg_ref, dsc_ref, dsh_ref):
        i = pl.program_id(0)
        xv = x_ref[...]
        dhv = dh_ref[...]
        rstd = lax.rsqrt(jnp.mean(xv * xv, axis=-1, keepdims=True) + EPS)
        xhat = xv * rstd
        hn = xhat * g_ref[...]
        dhn = dhv * (1.0 + sc_ref[...])
        dxh = dhn * g_ref[...]
        dx_ref[...] = dres_ref[...] + rstd * (dxh - xhat * jnp.mean(dxh * xhat, axis=-1, keepdims=True))

        @pl.when(i == 0)
        def _():
            dg_ref[...] = jnp.zeros_like(dg_ref)
            dsc_ref[...] = jnp.zeros_like(dsc_ref)
            dsh_ref[...] = jnp.zeros_like(dsh_ref)

        dg_ref[...] += _colsum(dhn * xhat)
        dsc_ref[...] += _colsum(dhv * hn)
        dsh_ref[...] += _colsum(dhv)

    vshape = jax.ShapeDtypeStruct((1, dm), F32)
    return pl.pallas_call(
        body, grid=(S // ts,), in_specs=[_row(ts, dm)] * 3 + [_vec(dm)] * 2,
        out_specs=(_row(ts, dm), _vec(dm), _vec(dm), _vec(dm)),
        out_shape=(jax.ShapeDtypeStruct((S, dm), F32), vshape, vshape, vshape),
        compiler_params=_cp(1), name=name)(dh, x, dres, g, sc)


def _gate_bwd(dxo, f, gate, *, name):
    S, dm = f.shape
    ts = _tile(S, TS_ROW, SUB)

    def body(dxo_ref, f_ref, gate_ref, df_ref, dgate_ref):
        i = pl.program_id(0)
        dv = dxo_ref[...]
        df_ref[...] = (gate_ref[...] * dv).astype(_ACT)

        @pl.when(i == 0)
        def _():
            dgate_ref[...] = jnp.zeros_like(dgate_ref)

        dgate_ref[...] += _colsum(dv * f_ref[...])

    return pl.pallas_call(
        body, grid=(S // ts,), in_specs=[_row(ts, dm), _row(ts, dm), _vec(dm)],
        out_specs=(_row(ts, dm), _vec(dm)),
        out_shape=(jax.ShapeDtypeStruct((S, dm), _ACT), jax.ShapeDtypeStruct((1, dm), F32)),
        compiler_params=_cp(1), name=name)(dxo, f, gate)


def _final_loss(x, f, gate, g, tgt, *, name):
    S, dm = x.shape
    ts = _tile(S, TS_ROW, SUB)

    def body(x_ref, f_ref, gate_ref, g_ref, t_ref, loss_ref, dx_ref, dg_ref):
        i = pl.program_id(0)
        xv = x_ref[...] + gate_ref[...] * f_ref[...]
        rstd = lax.rsqrt(jnp.mean(xv * xv, axis=-1, keepdims=True) + EPS)
        xhat = xv * rstd
        err = xhat * g_ref[...] - t_ref[...]
        dy = err * (1.0 / dm)
        dxh = dy * g_ref[...]
        dx_ref[...] = rstd * (dxh - xhat * jnp.mean(dxh * xhat, axis=-1, keepdims=True))

        @pl.when(i == 0)
        def _():
            loss_ref[...] = jnp.zeros_like(loss_ref)
            dg_ref[...] = jnp.zeros_like(dg_ref)

        loss_ref[...] += jnp.full((1, LANE), 0.5 * jnp.sum(jnp.mean(err * err, axis=-1, keepdims=True)), F32)
        dg_ref[...] += _colsum(dy * xhat)

    return pl.pallas_call(
        body, grid=(S // ts,), in_specs=[_row(ts, dm), _row(ts, dm), _vec(dm), _vec(dm), _row(ts, dm)],
        out_specs=(_vec(LANE), _row(ts, dm), _vec(dm)),
        out_shape=(jax.ShapeDtypeStruct((1, LANE), F32), jax.ShapeDtypeStruct((S, dm), F32),
                   jax.ShapeDtypeStruct((1, dm), F32)),
        compiler_params=_cp(1), name=name)(x, f, gate, g, tgt)


def _ffn_conv(t, halo, cw_ref, cb_ref):
    return ((cb_ref[...] + _shift_down(t, halo, 2) * cw_ref[0:1, :]) + _shift_down(t, halo, 1) * cw_ref[1:2, :]) \
        + t * cw_ref[2:3, :]


def _prev_halo_spec(ts, w, col=0):
    return pl.BlockSpec((SUB, w), lambda i: (jnp.maximum(i * (ts // SUB) - 1, 0), col))


def _ffn_act_fwd(up, cw, cb, *, name):
    S, w2 = up.shape
    ff = w2 // 2
    ts = _tile(S, TS_FFN, SUB)

    def body(up_ref, halo_ref, cw_ref, cb_ref, act_ref):
        i = pl.program_id(0)
        t = up_ref[...]
        halo = jnp.where(i > 0, halo_ref[...], 0.0)
        u = _ffn_conv(t, halo, cw_ref, cb_ref)
        act_ref[...] = (_silu(u[:, :ff]) * u[:, ff:]).astype(_ACT)

    return pl.pallas_call(
        body, grid=(S // ts,), in_specs=[_row(ts, w2), _prev_halo_spec(ts, w2), _vec(w2, FFN_CONV), _vec(w2)],
        out_specs=_row(ts, ff), out_shape=jax.ShapeDtypeStruct((S, ff), _ACT),
        compiler_params=_cp(1), name=name)(up, up, cw, cb)


def _ffn_act_bwd(up, dact, cw, cb, *, name):
    S, w2 = up.shape
    ff = w2 // 2
    ts = _tile(S, TS_FFN, SUB)

    def body(up_ref, halo_ref, dact_ref, cw_ref, cb_ref, du_ref):
        i = pl.program_id(0)
        t = up_ref[...]
        halo = jnp.where(i > 0, halo_ref[...], 0.0)
        u = _ffn_conv(t, halo, cw_ref, cb_ref)
        a, b = u[:, :ff], u[:, ff:]
        da = dact_ref[...]
        du_ref[:, :ff] = da * b * _dsilu(a)
        du_ref[:, ff:] = da * _silu(a)

    return pl.pallas_call(
        body, grid=(S // ts,),
        in_specs=[_row(ts, w2), _prev_halo_spec(ts, w2), _row(ts, ff), _vec(w2, FFN_CONV), _vec(w2)],
        out_specs=_row(ts, w2), out_shape=jax.ShapeDtypeStruct((S, w2), F32),
        compiler_params=_cp(1), name=name)(up, up, dact, cw, cb)


def _ffn_conv_bwd(du, up, cw, *, name):
    S, w2 = up.shape
    ts = _tile(S, TS_FFN, SUB)
    n = S // ts

    def body(du_ref, nxt_ref, up_ref, halo_ref, cw_ref, dup_ref, dcw_ref, dcb_ref):
        i = pl.program_id(0)
        dv = du_ref[...]
        nxt = jnp.where(i < n - 1, nxt_ref[...], 0.0)
        t = up_ref[...]
        halo = jnp.where(i > 0, halo_ref[...], 0.0)
        dup = (dv * cw_ref[2:3, :] + _shift_up(dv, nxt, 1) * cw_ref[1:2, :]) + _shift_up(dv, nxt, 2) * cw_ref[0:1, :]
        dup_ref[...] = dup.astype(_ACT)

        @pl.when(i == 0)
        def _():
            dcw_ref[...] = jnp.zeros_like(dcw_ref)
            dcb_ref[...] = jnp.zeros_like(dcb_ref)

        dcb_ref[...] += _colsum(dv)
        dcw_ref[2:3, :] += _colsum(dv * t)
        dcw_ref[1:2, :] += _colsum(dv * _shift_down(t, halo, 1))
        dcw_ref[0:1, :] += _colsum(dv * _shift_down(t, halo, 2))

    nxt_spec = pl.BlockSpec((SUB, w2), lambda i: (jnp.minimum((i + 1) * (ts // SUB), S // SUB - 1), 0))
    return pl.pallas_call(
        body, grid=(n,),
        in_specs=[_row(ts, w2), nxt_spec, _row(ts, w2), _prev_halo_spec(ts, w2), _vec(w2, FFN_CONV)],
        out_specs=(_row(ts, w2), _vec(w2, FFN_CONV), _vec(w2)),
        out_shape=(jax.ShapeDtypeStruct((S, w2), _ACT), jax.ShapeDtypeStruct((FFN_CONV, w2), F32),
                   jax.ShapeDtypeStruct((1, w2), F32)),
        compiler_params=_cp(1), name=name)(du, du, up, up, cw)


def _ssd_core(pre, halo, misc, cw_ref, cb_ref, dtb, alog):
    q = pre.shape[0]
    conv = cb_ref[...]
    for k in range(SSD_CONV):
        conv = conv + _shift_down(pre, halo, SSD_CONV - 1 - k) * cw_ref[k:k + 1, :]
    xbc = _silu(conv)
    raw = misc + dtb
    dt = _softplus(raw)
    a = -jnp.exp(alog)
    r = lax.broadcasted_iota(jnp.int32, (q, q), 0)
    c = lax.broadcasted_iota(jnp.int32, (q, q), 1)
    tri = r >= c
    acum = jnp.dot(tri.astype(F32), dt * a, precision=_HI, preferred_element_type=F32)
    return conv, xbc, raw, dt, a, acum, acum.T, tri


def _sel(v, j, lo):
    return jnp.where(lo, v[:, 2 * j:2 * j + 1], v[:, 2 * j + 1:2 * j + 2])


def _ssd_pair_fwd(xbc, dt, acum, acum_t, tri, dsk, g_mat, b_mat, c_mat, h_pair, j, lo, lo1, sub_lo):
    q = xbc.shape[0]
    x = xbc[:, LANE * j:LANE * (j + 1)]
    dtp = _sel(dt, j, lo)
    ap = _sel(acum, j, lo)
    xd = x * dtp
    ls, ms = [], []
    for h in (2 * j, 2 * j + 1):
        seg = acum[:, h:h + 1] - acum_t[h:h + 1, :]
        l_mat = jnp.exp(jnp.where(tri, seg, -jnp.inf))
        ls.append(l_mat)
        ms.append(g_mat * l_mat)
    yd = jnp.where(lo, _dot(ms[0], xd), _dot(ms[1], xd))
    ea = jnp.exp(ap)
    yo = _dot_nt(c_mat, h_pair) * ea
    dp = _sel(dsk, j, lo1)
    alast = acum[q - 1:q, :]
    e = jnp.exp(_sel(alast, j, lo1) - ap)
    cd = jnp.where(sub_lo, jnp.exp(alast[:, 2 * j:2 * j + 1]), jnp.exp(alast[:, 2 * j + 1:2 * j + 2]))
    return dict(x=x, dtp=dtp, ap=ap, xd=xd, ls=ls, ms=ms, ea=ea, yo=yo, dp=dp, e=e, cd=cd, y=yd + yo + x * dp)


def _gnorm(yg):
    half = SSD_INNER // SSD_GROUPS
    rstds, yns = [], []
    for g in range(SSD_GROUPS):
        part = yg[:, half * g:half * (g + 1)]
        rstd = lax.rsqrt(jnp.mean(part * part, axis=-1, keepdims=True) + EPS)
        rstds.append(rstd)
        yns.append(part * rstd)
    return rstds, yns


def _ssd_specs(nc, rev):
    q = SSD_CHUNK
    cidx = (lambda i: nc - 1 - i) if rev else (lambda i: i)
    return [
        pl.BlockSpec((q, SSD_XBC), lambda i: (cidx(i), C_XBC // SSD_XBC)),
        pl.BlockSpec((SUB, SSD_XBC), lambda i: (jnp.maximum(cidx(i) * (q // SUB) - 1, 0), C_XBC // SSD_XBC)),
        pl.BlockSpec((q, SSD_INNER), lambda i: (cidx(i), C_Z // SSD_INNER)),
        pl.BlockSpec((q, LANE), lambda i: (cidx(i), C_MISC // LANE)),
    ]


def _ssd_param_specs():
    return [_vec(SSD_XBC, SSD_CONV), _vec(SSD_XBC), _vec(LANE), _vec(LANE), _vec(LANE), _vec(SSD_INNER)]


def _ssd_fwd(proj, cw, cb, dtb, alog, dsk, ng, *, name):
    S = proj.shape[0]
    q = SSD_CHUNK
    nc = S // q
    npair = SSD_HEADS // 2

    def body(xbc_ref, halo_ref, z_ref, misc_ref, cw_ref, cb_ref, dtb_ref, alog_ref, dsk_ref, ng_ref,
             y_ref, hin_ref, h_ref):
        c = pl.program_id(0)

        @pl.when(c == 0)
        def _():
            h_ref[...] = jnp.zeros_like(h_ref)

        pre = xbc_ref[...]
        halo = jnp.where(c > 0, halo_ref[...], 0.0)
        conv, xbc, raw, dt, a, acum, acum_t, tri = _ssd_core(pre, halo, misc_ref[...], cw_ref, cb_ref,
                                                             dtb_ref[...], alog_ref[...])
        lo = lax.broadcasted_iota(jnp.int32, (q, LANE), 1) < LANE // 2
        lo1 = lo[:1]
        sub_lo = lax.broadcasted_iota(jnp.int32, (LANE, LANE), 0) < LANE // 2
        ys = []
        for g in range(SSD_GROUPS):
            b_mat = xbc[:, SSD_INNER + SSD_STATE * g:SSD_INNER + SSD_STATE * (g + 1)]
            c_mat = xbc[:, SSD_INNER + SSD_STATE * (SSD_GROUPS + g):SSD_INNER + SSD_STATE * (SSD_GROUPS + g + 1)]
            g_mat = _dot_nt(c_mat, b_mat)
            for jj in range(npair // SSD_GROUPS):
                j = g * (npair // SSD_GROUPS) + jj
                hj = h_ref[j]
                p = _ssd_pair_fwd(xbc, dt, acum, acum_t, tri, dsk_ref[...], g_mat, b_mat, c_mat, hj, j, lo, lo1, sub_lo)
                ys.append(p["y"])
                hin_ref[0, j] = hj
                h_ref[j] = p["cd"] * hj + _dot_tn(p["xd"] * p["e"], b_mat)
        yg = jnp.concatenate(ys, axis=1) * _silu(z_ref[...])
        _, yns = _gnorm(yg)
        y_ref[...] = jnp.concatenate(yns, axis=1) * ng_ref[...]

    return pl.pallas_call(
        body, grid=(nc,), in_specs=_ssd_specs(nc, False) + _ssd_param_specs(),
        out_specs=(pl.BlockSpec((q, SSD_INNER), lambda i: (i, 0)),
                   pl.BlockSpec((1, npair, LANE, LANE), lambda i: (i, 0, 0, 0))),
        out_shape=(jax.ShapeDtypeStruct((S, SSD_INNER), F32), jax.ShapeDtypeStruct((nc, npair, LANE, LANE), F32)),
        scratch_shapes=[pltpu.VMEM((npair, LANE, LANE), F32)], compiler_params=_cp(1), name=name,
    )(proj, proj, proj, proj, cw, cb, dtb, alog, dsk, ng)


def _ssd_bwd(proj, dycat, hin, cw, cb, dtb, alog, dsk, ng, *, name):
    S = proj.shape[0]
    q = SSD_CHUNK
    nc = S // q
    npair = SSD_HEADS // 2
    ppg = npair // SSD_GROUPS

    def body(xbc_ref, halo_ref, z_ref, misc_ref, dy_ref, hin_ref, cw_ref, cb_ref, dtb_ref, alog_ref, dsk_ref, ng_ref,
             dpre_ref, dz_ref, dmisc_ref, dcw_ref, dcb_ref, ddtb_ref, dalog_ref, ddsk_ref, dng_ref,
             dh_ref, carry_ref):
        i = pl.program_id(0)
        c = nc - 1 - i

        @pl.when(i == 0)
        def _():
            dh_ref[...] = jnp.zeros_like(dh_ref)
            carry_ref[...] = jnp.zeros_like(carry_ref)
            for r in (dcw_ref, dcb_ref, ddtb_ref, dalog_ref, ddsk_ref, dng_ref):
                r[...] = jnp.zeros_like(r)

        pre = xbc_ref[...]
        halo = jnp.where(c > 0, halo_ref[...], 0.0)
        conv, xbc, raw, dt, a, acum, acum_t, tri = _ssd_core(pre, halo, misc_ref[...], cw_ref, cb_ref,
                                                             dtb_ref[...], alog_ref[...])
        lane = lax.broadcasted_iota(jnp.int32, (q, LANE), 1)
        lane1 = lane[:1]
        rowi = lax.broadcasted_iota(jnp.int32, (q, LANE), 0)
        lastrow = rowi == q - 1
        lo = lane < LANE // 2
        lo1 = lo[:1]
        sub_lo = lax.broadcasted_iota(jnp.int32, (LANE, LANE), 0) < LANE // 2
        dsk = dsk_ref[...]
        alast = acum[q - 1:q, :]

        def halves(t):
            return _rowsum(jnp.where(lo, t, 0.0)), _rowsum(jnp.where(lo, 0.0, t))

        def put(ha, va, vb):
            ln = lane if va.shape[0] == q else lane1
            return jnp.where(ln == ha, va, 0.0) + jnp.where(ln == ha + 1, vb, 0.0)

        mats, pairs = [], []
        for g in range(SSD_GROUPS):
            b_mat = xbc[:, SSD_INNER + SSD_STATE * g:SSD_INNER + SSD_STATE * (g + 1)]
            c_mat = xbc[:, SSD_INNER + SSD_STATE * (SSD_GROUPS + g):SSD_INNER + SSD_STATE * (SSD_GROUPS + g + 1)]
            g_mat = _dot_nt(c_mat, b_mat)
            mats.append((b_mat, c_mat, g_mat))
            for jj in range(ppg):
                j = g * ppg + jj
                pairs.append(_ssd_pair_fwd(xbc, dt, acum, acum_t, tri, dsk, g_mat, b_mat, c_mat, hin_ref[0, j],
                                           j, lo, lo1, sub_lo))
        z = z_ref[...]
        sz = _silu(z)
        y = jnp.concatenate([p["y"] for p in pairs], axis=1)
        rstds, yns = _gnorm(y * sz)
        dout = dy_ref[...]
        dng_ref[...] += _colsum(dout * jnp.concatenate(yns, axis=1))
        dyn = dout * ng_ref[...]
        half = SSD_INNER // SSD_GROUPS
        dygs = []
        for g in range(SSD_GROUPS):
            dyn_g = dyn[:, half * g:half * (g + 1)]
            dygs.append(rstds[g] * (dyn_g - yns[g] * jnp.mean(dyn_g * yns[g], axis=-1, keepdims=True)))
        dyg = jnp.concatenate(dygs, axis=1)
        dyv = dyg * sz
        dz_ref[...] = (dyg * y * _dsilu(z)).astype(_ACT)

        da_acc = jnp.zeros((q, LANE), F32)
        ddt = jnp.zeros((q, LANE), F32)
        dds = jnp.zeros((1, LANE), F32)
        dxs, dbs, dcs = [], [], []
        for g in range(SSD_GROUPS):
            b_mat, c_mat, g_mat = mats[g]
            dg_mat = jnp.zeros((q, q), F32)
            db = jnp.zeros((q, SSD_STATE), F32)
            dc = jnp.zeros((q, SSD_STATE), F32)
            for jj in range(ppg):
                j = g * ppg + jj
                ha = 2 * j
                p = pairs[j]
                hj = hin_ref[0, j]
                dyp = dyv[:, LANE * j:LANE * (j + 1)]
                dsum = _colsum(dyp * p["x"])
                dds = dds + put(ha, _rowsum(jnp.where(lo1, dsum, 0.0)), _rowsum(jnp.where(lo1, 0.0, dsum)))
                dx = dyp * p["dp"]
                dw = dyp * p["ea"]
                dc = dc + _dot(dw, hj)
                dh_yo = _dot_tn(dw, c_mat)
                ra, rb = halves(dyp * p["yo"])
                da_acc = da_acc + put(ha, ra, rb)
                dxd = jnp.zeros((q, LANE), F32)
                for idx in range(2):
                    dyh = jnp.where(lo, dyp, 0.0) if idx == 0 else jnp.where(lo, 0.0, dyp)
                    dm = _dot_nt(dyh, p["xd"])
                    dxd = dxd + _dot_tn(p["ms"][idx], dyh)
                    dg_mat = dg_mat + dm * p["ls"][idx]
                    t = dm * p["ms"][idx]
                    da_h = _rowsum(t) - _rowsum(t.T)
                    da_acc = da_acc + jnp.where(lane == ha + idx, da_h, 0.0)
                dhn = dh_ref[j]
                s = _rowsum(dhn * hj)
                sa = jnp.sum(jnp.where(sub_lo[:, :1], s, 0.0), keepdims=True)
                sb = jnp.sum(jnp.where(sub_lo[:, :1], 0.0, s), keepdims=True)
                cda, cdb = jnp.exp(alast[:, ha:ha + 1]), jnp.exp(alast[:, ha + 1:ha + 2])
                db = db + _dot(p["xd"] * p["e"], dhn)
                r = _dot_nt(b_mat, dhn)
                dxd = dxd + r * p["e"]
                qa, qb = halves(r * p["xd"] * p["e"])
                da_acc = da_acc - put(ha, qa, qb)
                tot_a = sa * cda + jnp.sum(qa, keepdims=True)
                tot_b = sb * cdb + jnp.sum(qb, keepdims=True)
                da_acc = da_acc + jnp.where(lastrow, put(ha, tot_a, tot_b), 0.0)
                dh_ref[j] = p["cd"] * dhn + dh_yo
                dx = dx + dxd * p["dtp"]
                ua, ub = halves(dxd * p["x"])
                ddt = ddt + put(ha, ua, ub)
                dxs.append(dx)
            dc = dc + _dot(dg_mat, b_mat)
            db = db + _dot_tn(dg_mat, c_mat)
            dbs.append(db)
            dcs.append(dc)
        r2 = lax.broadcasted_iota(jnp.int32, (q, q), 0)
        c2 = lax.broadcasted_iota(jnp.int32, (q, q), 1)
        dda = jnp.dot((c2 >= r2).astype(F32), da_acc, precision=_HI, preferred_element_type=F32)
        ddt = ddt + dda * a
        dalog_ref[...] += _colsum(dda * dt) * a
        ddsk_ref[...] += dds
        draw = jnp.where(lane < SSD_HEADS, ddt * _sigmoid(raw), 0.0)
        ddtb_ref[...] += _colsum(draw)
        dmisc_ref[...] = draw
        dconv = jnp.concatenate(dxs + dbs + dcs, axis=1) * _dsilu(conv)
        dcb_ref[...] += _colsum(dconv)
        nxt = carry_ref[...]
        dpre = jnp.zeros_like(dconv)
        for k in range(SSD_CONV):
            dcw_ref[k:k + 1, :] += _colsum(dconv * _shift_down(pre, halo, SSD_CONV - 1 - k))
            dpre = dpre + _shift_up(dconv, nxt, SSD_CONV - 1 - k) * cw_ref[k:k + 1, :]
        dpre_ref[...] = dpre.astype(_ACT)
        carry_ref[...] = dconv[:SUB]

    rev = lambda i: (nc - 1 - i, 0)
    vshape = lambda w, r=1: jax.ShapeDtypeStruct((r, w), F32)
    return pl.pallas_call(
        body, grid=(nc,),
        in_specs=_ssd_specs(nc, True) + [pl.BlockSpec((q, SSD_INNER), rev),
                                         pl.BlockSpec((1, npair, LANE, LANE), lambda i: (nc - 1 - i, 0, 0, 0))]
        + _ssd_param_specs(),
        out_specs=(pl.BlockSpec((q, SSD_XBC), rev), pl.BlockSpec((q, SSD_INNER), rev), pl.BlockSpec((q, LANE), rev),
                   _vec(SSD_XBC, SSD_CONV), _vec(SSD_XBC), _vec(LANE), _vec(LANE), _vec(LANE), _vec(SSD_INNER)),
        out_shape=(jax.ShapeDtypeStruct((S, SSD_XBC), _ACT), jax.ShapeDtypeStruct((S, SSD_INNER), _ACT),
                   jax.ShapeDtypeStruct((S, LANE), F32),
                   vshape(SSD_XBC, SSD_CONV), vshape(SSD_XBC), vshape(LANE), vshape(LANE), vshape(LANE),
                   vshape(SSD_INNER)),
        scratch_shapes=[pltpu.VMEM((npair, LANE, LANE), F32), pltpu.VMEM((SUB, SSD_XBC), F32)],
        compiler_params=_cp(1), name=name,
    )(proj, proj, proj, proj, dycat, hin, cw, cb, dtb, alog, dsk, ng)


def _rope(x, cosf, sina, sinb):
    return x * cosf + pltpu.roll(x, LANE - MLA_ROPE // 2, 1) * sina + pltpu.roll(x, MLA_ROPE // 2, 1) * sinb


def _rope_t(dy, cosf, sina, sinb):
    return dy * cosf + pltpu.roll(dy * sina, MLA_ROPE // 2, 1) + pltpu.roll(dy * sinb, LANE - MLA_ROPE // 2, 1)


def _rope_lanes(shape):
    lane = lax.broadcasted_iota(jnp.int32, shape, 1)
    return (lane >= ROPE_LANE) & (lane < ROPE_LANE + MLA_ROPE)


def _mla_prep_fwd(proj, cosf, sina, sinb, gq, gkv, wuq, wukv, *, name):
    S = proj.shape[0]
    ts = _tile(S, TS_ROW, SUB)
    hw = MLA_HEADS * LANE

    def body(cq_ref, ckv_ref, misc_ref, cos_ref, sa_ref, sb_ref, gq_ref, gkv_ref, wuq_ref, wukv_ref,
             q_ref, k_ref, v_ref):
        cosv, sav, sbv = cos_ref[...], sa_ref[...], sb_ref[...]
        cq = cq_ref[...]
        qn = cq * lax.rsqrt(jnp.mean(cq * cq, axis=-1, keepdims=True) + EPS) * gq_ref[...]
        qh = _dot(qn, wuq_ref[...])
        ckv = ckv_ref[...]
        kvn = ckv * lax.rsqrt(jnp.mean(ckv * ckv, axis=-1, keepdims=True) + EPS) * gkv_ref[...]
        kv = _dot(kvn, wukv_ref[...])
        kr = _rope(jnp.where(_rope_lanes((ts, LANE)), misc_ref[...], 0.0), cosv, sav, sbv)
        for h in range(MLA_HEADS):
            sl = slice(LANE * h, LANE * (h + 1))
            q_ref[:, sl] = _rope(qh[:, sl], cosv, sav, sbv).astype(_ACT)
            k_ref[:, sl] = (kv[:, sl] + kr).astype(_ACT)
        v_ref[...] = kv[:, hw:].astype(_ACT)

    oshape = jax.ShapeDtypeStruct((S, hw), _ACT)
    return pl.pallas_call(
        body, grid=(S // ts,),
        in_specs=[_row(ts, MLA_QR, C_CQ // MLA_QR), _row(ts, MLA_KVR, C_CKV // MLA_KVR), _row(ts, LANE, C_MISC // LANE),
                  _row(ts, LANE), _row(ts, LANE), _row(ts, LANE), _vec(MLA_QR), _vec(MLA_KVR),
                  _vec(hw, MLA_QR), _vec(2 * hw, MLA_KVR)],
        out_specs=(_row(ts, hw),) * 3, out_shape=(oshape,) * 3, compiler_params=_cp(1), name=name,
    )(proj, proj, proj, cosf, sina, sinb, gq, gkv, wuq, wukv)


def _mla_prep_bwd(proj, dq, dk, dv, dmisc_ssd, cosf, sina, sinb, gq, gkv, wuq, wukv, *, name):
    S = proj.shape[0]
    ts = _tile(S, TS_ROW, SUB)
    hw = MLA_HEADS * LANE

    def body(cq_ref, ckv_ref, dq_ref, dk_ref, dv_ref, dms_ref, cos_ref, sa_ref, sb_ref, gq_ref, gkv_ref,
             wuq_ref, wukv_ref, dcq_ref, dckv_ref, dmisc_ref, dqh_ref, dkv_ref, qn_ref, kvn_ref, dgq_ref, dgkv_ref):
        i = pl.program_id(0)
        cosv, sav, sbv = cos_ref[...], sa_ref[...], sb_ref[...]

        @pl.when(i == 0)
        def _():
            dgq_ref[...] = jnp.zeros_like(dgq_ref)
            dgkv_ref[...] = jnp.zeros_like(dgkv_ref)

        dqh = jnp.concatenate([_rope_t(dq_ref[:, LANE * h:LANE * (h + 1)], cosv, sav, sbv)
                               for h in range(MLA_HEADS)], axis=1)
        dqh_ref[...] = dqh.astype(_ACT)
        dkv = jnp.concatenate([dk_ref[...], dv_ref[...]], axis=1)
        dkv_ref[...] = dkv.astype(_ACT)

        def norm_bwd(x, g, dn, dg_ref, n_ref):
            rstd = lax.rsqrt(jnp.mean(x * x, axis=-1, keepdims=True) + EPS)
            xhat = x * rstd
            n_ref[...] = (xhat * g).astype(_ACT)
            dg_ref[...] += _colsum(dn * xhat)
            dxh = dn * g
            return rstd * (dxh - xhat * jnp.mean(dxh * xhat, axis=-1, keepdims=True))

        dcq_ref[...] = norm_bwd(cq_ref[...], gq_ref[...], _dot_nt(dqh, wuq_ref[...]), dgq_ref, qn_ref).astype(_ACT)
        dckv_ref[...] = norm_bwd(ckv_ref[...], gkv_ref[...], _dot_nt(dkv, wukv_ref[...]), dgkv_ref, kvn_ref).astype(_ACT)
        dks = dk_ref[:, 0:LANE]
        for h in range(1, MLA_HEADS):
            dks = dks + dk_ref[:, LANE * h:LANE * (h + 1)]
        rl = _rope_lanes((ts, LANE))
        dkr = _rope_t(jnp.where(rl, dks, 0.0), cosv, sav, sbv)
        dmisc_ref[...] = (dms_ref[...] + jnp.where(rl, dkr, 0.0)).astype(_ACT)

    act = lambda w: jax.ShapeDtypeStruct((S, w), _ACT)
    return pl.pallas_call(
        body, grid=(S // ts,),
        in_specs=[_row(ts, MLA_QR, C_CQ // MLA_QR), _row(ts, MLA_KVR, C_CKV // MLA_KVR),
                  _row(ts, hw), _row(ts, hw), _row(ts, hw), _row(ts, LANE),
                  _row(ts, LANE), _row(ts, LANE), _row(ts, LANE), _vec(MLA_QR), _vec(MLA_KVR),
                  _vec(hw, MLA_QR), _vec(2 * hw, MLA_KVR)],
        out_specs=(_row(ts, MLA_QR), _row(ts, MLA_KVR), _row(ts, LANE), _row(ts, hw), _row(ts, 2 * hw),
                   _row(ts, MLA_QR), _row(ts, MLA_KVR), _vec(MLA_QR), _vec(MLA_KVR)),
        out_shape=(act(MLA_QR), act(MLA_KVR), act(LANE), act(hw), act(2 * hw), act(MLA_QR), act(MLA_KVR),
                   jax.ShapeDtypeStruct((1, MLA_QR), F32), jax.ShapeDtypeStruct((1, MLA_KVR), F32)),
        compiler_params=_cp(1), name=name,
    )(proj, proj, dq, dk, dv, dmisc_ssd, cosf, sina, sinb, gq, gkv, wuq, wukv)


_MLA_SCALE = 1.0 / math.sqrt(MLA_NOPE + MLA_ROPE)


def _causal_scores(q, k, i, j, tq):
    s = _dot_nt(q, k) * _MLA_SCALE
    rows = i * tq + lax.broadcasted_iota(jnp.int32, (tq, tq), 0)
    cols = j * tq + lax.broadcasted_iota(jnp.int32, (tq, tq), 1)
    return jnp.where(cols <= rows, s, -jnp.inf)


def _attn_fwd(q, k, v, *, name):
    S = q.shape[0]
    tq = _tile(S, TQ_ATT)
    nq = S // tq

    def body(q_ref, k_ref, v_ref, o_ref, lse_ref, m_ref, l_ref, acc_ref):
        i, j = pl.program_id(1), pl.program_id(2)

        @pl.when(j == 0)
        def _():
            m_ref[...] = jnp.full_like(m_ref, -jnp.inf)
            l_ref[...] = jnp.zeros_like(l_ref)
            acc_ref[...] = jnp.zeros_like(acc_ref)

        @pl.when(j <= i)
        def _():
            s = _causal_scores(q_ref[...], k_ref[...], i, j, tq)
            m_prev = m_ref[...]
            m_new = jnp.maximum(m_prev, jnp.max(s, axis=1, keepdims=True))
            p = jnp.exp(s - m_new)
            alpha = jnp.exp(m_prev - m_new)
            l_ref[...] = alpha * l_ref[...] + _rowsum(p)
            acc_ref[...] = alpha * acc_ref[...] + _dot(p, v_ref[...])
            m_ref[...] = m_new

        @pl.when(j == nq - 1)
        def _():
            o_ref[...] = acc_ref[...] / l_ref[...]
            lse_ref[...] = jnp.broadcast_to(m_ref[...] + jnp.log(l_ref[...]), (tq, LANE))

    qspec = pl.BlockSpec((tq, LANE), lambda h, i, j: (i, h))
    kspec = pl.BlockSpec((tq, LANE), lambda h, i, j: (jnp.minimum(j, i), h))
    oshape = jax.ShapeDtypeStruct((S, MLA_HEADS * LANE), F32)
    return pl.pallas_call(
        body, grid=(MLA_HEADS, nq, nq), in_specs=[qspec, kspec, kspec], out_specs=(qspec, qspec),
        out_shape=(oshape, oshape),
        scratch_shapes=[pltpu.VMEM((tq, 1), F32), pltpu.VMEM((tq, 1), F32), pltpu.VMEM((tq, LANE), F32)],
        compiler_params=_cp(3), name=name)(q, k, v)


def _attn_bwd_dq(q, k, v, o, lse, dycat, *, name):
    S = q.shape[0]
    tq = _tile(S, TQ_ATT)
    nq = S // tq

    def body(q_ref, k_ref, v_ref, o_ref, lse_ref, do_ref, dq_ref, acc_ref):
        i, j = pl.program_id(1), pl.program_id(2)

        @pl.when(j == 0)
        def _():
            acc_ref[...] = jnp.zeros_like(acc_ref)

        @pl.when(j <= i)
        def _():
            kv = k_ref[...]
            p = jnp.exp(_causal_scores(q_ref[...], kv, i, j, tq) - lse_ref[:, 0:1])
            dov = do_ref[...]
            delta = _rowsum(dov * o_ref[...])
            ds = p * (_dot_nt(dov, v_ref[...]) - delta) * _MLA_SCALE
            acc_ref[...] += _dot(ds, kv)

        @pl.when(j == nq - 1)
        def _():
            dq_ref[...] = acc_ref[...]

    qspec = pl.BlockSpec((tq, LANE), lambda h, i, j: (i, h))
    kspec = pl.BlockSpec((tq, LANE), lambda h, i, j: (jnp.minimum(j, i), h))
    dospec = pl.BlockSpec((tq, LANE), lambda h, i, j: (i, SSD_INNER // LANE + h))
    return pl.pallas_call(
        body, grid=(MLA_HEADS, nq, nq), in_specs=[qspec, kspec, kspec, qspec, qspec, dospec], out_specs=qspec,
        out_shape=jax.ShapeDtypeStruct((S, MLA_HEADS * LANE), F32),
        scratch_shapes=[pltpu.VMEM((tq, LANE), F32)], compiler_params=_cp(3), name=name)(q, k, v, o, lse, dycat)


def _attn_bwd_dkv(q, k, v, o, lse, dycat, *, name):
    S = q.shape[0]
    tq = _tile(S, TQ_ATT)
    nq = S // tq

    def body(q_ref, k_ref, v_ref, o_ref, lse_ref, do_ref, dk_ref, dv_ref, dk_acc, dv_acc):
        j, i = pl.program_id(1), pl.program_id(2)

        @pl.when(i == 0)
        def _():
            dk_acc[...] = jnp.zeros_like(dk_acc)
            dv_acc[...] = jnp.zeros_like(dv_acc)

        @pl.when(i >= j)
        def _():
            qv = q_ref[...]
            p = jnp.exp(_causal_scores(qv, k_ref[...], i, j, tq) - lse_ref[:, 0:1])
            dov = do_ref[...]
            delta = _rowsum(dov * o_ref[...])
            dv_acc[...] += _dot_tn(p, dov)
            ds = p * (_dot_nt(dov, v_ref[...]) - delta) * _MLA_SCALE
            dk_acc[...] += _dot_tn(ds, qv)

        @pl.when(i == nq - 1)
        def _():
            dk_ref[...] = dk_acc[...]
            dv_ref[...] = dv_acc[...]

    qspec = pl.BlockSpec((tq, LANE), lambda h, j, i: (jnp.maximum(i, j), h))
    kspec = pl.BlockSpec((tq, LANE), lambda h, j, i: (j, h))
    dospec = pl.BlockSpec((tq, LANE), lambda h, j, i: (jnp.maximum(i, j), SSD_INNER // LANE + h))
    oshape = jax.ShapeDtypeStruct((S, MLA_HEADS * LANE), F32)
    return pl.pallas_call(
        body, grid=(MLA_HEADS, nq, nq), in_specs=[qspec, kspec, kspec, qspec, qspec, dospec],
        out_specs=(kspec, kspec), out_shape=(oshape, oshape),
        scratch_shapes=[pltpu.VMEM((tq, LANE), F32), pltpu.VMEM((tq, LANE), F32)],
        compiler_params=_cp(3), name=name)(q, k, v, o, lse, dycat)


_SWA_SCALE = 1.0 / math.sqrt(SWA_HD)
_SWA_KW = SWA_KV * LANE


def _swa_specs(S, ts, rev):
    n = S // ts
    t = (lambda i: n - 1 - i) if rev else (lambda i: i)
    hb = lambda i: jnp.maximum(t(i) * (ts // WINDOW) - 1, 0)
    return [
        pl.BlockSpec((ts, SWA_HEADS * LANE), lambda i: (t(i), C_SQ // (SWA_HEADS * LANE))),
        pl.BlockSpec((ts, _SWA_KW), lambda i: (t(i), C_SK // _SWA_KW)),
        pl.BlockSpec((WINDOW, _SWA_KW), lambda i: (hb(i), C_SK // _SWA_KW)),
        pl.BlockSpec((ts, _SWA_KW), lambda i: (t(i), C_SV // _SWA_KW)),
        pl.BlockSpec((WINDOW, _SWA_KW), lambda i: (hb(i), C_SV // _SWA_KW)),
    ]


def _swa_scores(qh, kk, t, b, ts):
    s = _dot_nt(qh, kk) * _SWA_SCALE
    row = lax.broadcasted_iota(jnp.int32, (WINDOW, 2 * WINDOW), 0)
    col = lax.broadcasted_iota(jnp.int32, (WINDOW, 2 * WINDOW), 1)
    rel = WINDOW + row - col
    kpos = t * ts + (b - 1) * WINDOW + col
    return jnp.where((rel >= 0) & (rel < WINDOW) & (kpos >= 0), s, -jnp.inf)


def _swa_fwd(proj, sinks, *, name):
    S = proj.shape[0]
    ts = _tile(S, TS_SWA)
    nb = ts // WINDOW

    def body(q_ref, k_ref, kh_ref, v_ref, vh_ref, sink_ref, o_ref, lse_ref):
        t = pl.program_id(0)
        kext = jnp.concatenate([kh_ref[...], k_ref[...]], axis=0)
        vext = jnp.concatenate([vh_ref[...], v_ref[...]], axis=0)
        for b in range(nb):
            rows = slice(WINDOW * b, WINDOW * (b + 1))
            for h in range(SWA_HEADS):
                kvl = slice(LANE * (h // (SWA_HEADS // SWA_KV)), LANE * (h // (SWA_HEADS // SWA_KV) + 1))
                hl = slice(LANE * h, LANE * (h + 1))
                kk = kext[WINDOW * b:WINDOW * (b + 2), kvl]
                vv = vext[WINDOW * b:WINDOW * (b + 2), kvl]
                s = _swa_scores(q_ref[rows, hl], kk, t, b, ts)
                sk = sink_ref[:, h:h + 1]
                m = jnp.maximum(jnp.max(s, axis=1, keepdims=True), sk)
                p = jnp.exp(s - m)
                den = _rowsum(p) + jnp.exp(sk - m)
                o_ref[rows, hl] = _dot(p, vv) / den
                lse_ref[rows, hl] = jnp.broadcast_to(m + jnp.log(den), (WINDOW, LANE))

    oshape = jax.ShapeDtypeStruct((S, SWA_HEADS * LANE), F32)
    ospec = pl.BlockSpec((ts, SWA_HEADS * LANE), lambda i: (i, 0))
    return pl.pallas_call(
        body, grid=(S // ts,), in_specs=_swa_specs(S, ts, False) + [_vec(LANE)], out_specs=(ospec, ospec),
        out_shape=(oshape, oshape), compiler_params=_cp(1), name=name)(proj, proj, proj, proj, proj, sinks)


def _swa_bwd(proj, o, lse, dycat, sinks, *, name):
    S = proj.shape[0]
    ts = _tile(S, TS_SWA)
    nb = ts // WINDOW
    n = S // ts
    grp = SWA_HEADS // SWA_KV

    def body(q_ref, k_ref, kh_ref, v_ref, vh_ref, o_ref, lse_ref, do_ref, sink_ref,
             dq_ref, dk_ref, dv_ref, dsink_ref, dk_carry, dv_carry):
        i = pl.program_id(0)
        t = n - 1 - i

        @pl.when(i == 0)
        def _():
            dk_carry[...] = jnp.zeros_like(dk_carry)
            dv_carry[...] = jnp.zeros_like(dv_carry)
            dsink_ref[...] = jnp.zeros_like(dsink_ref)

        kext = jnp.concatenate([kh_ref[...], k_ref[...]], axis=0)
        vext = jnp.concatenate([vh_ref[...], v_ref[...]], axis=0)
        lane1 = lax.broadcasted_iota(jnp.int32, (1, LANE), 1)
        dkb = [[jnp.zeros((WINDOW, LANE), F32) for _ in range(SWA_KV)] for _ in range(nb + 1)]
        dvb = [[jnp.zeros((WINDOW, LANE), F32) for _ in range(SWA_KV)] for _ in range(nb + 1)]
        dsink = jnp.zeros((1, LANE), F32)
        for b in range(nb):
            rows = slice(WINDOW * b, WINDOW * (b + 1))
            for h in range(SWA_HEADS):
                kvh = h // grp
                kvl = slice(LANE * kvh, LANE * (kvh + 1))
                hl = slice(LANE * h, LANE * (h + 1))
                kk = kext[WINDOW * b:WINDOW * (b + 2), kvl]
                vv = vext[WINDOW * b:WINDOW * (b + 2), kvl]
                qh = q_ref[rows, hl]
                lse_h = lse_ref[rows, LANE * h:LANE * h + 1]
                p = jnp.exp(_swa_scores(qh, kk, t, b, ts) - lse_h)
                doh = do_ref[rows, hl]
                delta = _rowsum(doh * o_ref[rows, hl])
                ds = p * (_dot_nt(doh, vv) - delta)
                sk = sink_ref[:, h:h + 1]
                dsink = dsink + jnp.where(lane1 == h, -jnp.sum(jnp.exp(sk - lse_h) * delta, keepdims=True), 0.0)
                dq_ref[rows, hl] = (_dot(ds, kk) * _SWA_SCALE).astype(_ACT)
                dkk = _dot_tn(ds, qh) * _SWA_SCALE
                dvv = _dot_tn(p, doh)
                dkb[b][kvh] = dkb[b][kvh] + dkk[:WINDOW]
                dkb[b + 1][kvh] = dkb[b + 1][kvh] + dkk[WINDOW:]
                dvb[b][kvh] = dvb[b][kvh] + dvv[:WINDOW]
                dvb[b + 1][kvh] = dvb[b + 1][kvh] + dvv[WINDOW:]
        dsink_ref[...] += dsink
        for dref, blocks, carry in ((dk_ref, dkb, dk_carry), (dv_ref, dvb, dv_carry)):
            old = carry[...]
            for b in range(1, nb + 1):
                blk = jnp.concatenate(blocks[b], axis=1)
                if b == nb:
                    blk = blk + old
                dref[WINDOW * (b - 1):WINDOW * b, :] = blk.astype(_ACT)
            carry[...] = jnp.concatenate(blocks[0], axis=1)

    hw = SWA_HEADS * LANE
    rev = lambda i: (n - 1 - i, 0)
    mix = lambda i: (n - 1 - i, (SSD_INNER + MLA_HEADS * LANE) // hw)
    return pl.pallas_call(
        body, grid=(n,),
        in_specs=_swa_specs(S, ts, True) + [pl.BlockSpec((ts, hw), rev), pl.BlockSpec((ts, hw), rev),
                                            pl.BlockSpec((ts, hw), mix), _vec(LANE)],
        out_specs=(pl.BlockSpec((ts, hw), rev), pl.BlockSpec((ts, _SWA_KW), rev), pl.BlockSpec((ts, _SWA_KW), rev),
                   _vec(LANE)),
        out_shape=(jax.ShapeDtypeStruct((S, hw), _ACT), jax.ShapeDtypeStruct((S, _SWA_KW), _ACT),
                   jax.ShapeDtypeStruct((S, _SWA_KW), _ACT), jax.ShapeDtypeStruct((1, LANE), F32)),
        scratch_shapes=[pltpu.VMEM((WINDOW, _SWA_KW), F32), pltpu.VMEM((WINDOW, _SWA_KW), F32)],
        compiler_params=_cp(1), name=name)(proj, proj, proj, proj, proj, o, lse, dycat, sinks)


def _exchange(arrays, *, scatter, name):
    n = len(arrays)

    def body(*refs):
        ins, outs = refs[:n], refs[n:2 * n]
        send_sems, recv_sems, loc_sems = refs[2 * n:]
        x, y, c = lax.axis_index("x"), lax.axis_index("y"), lax.axis_index("c")
        me = 4 * x + 2 * y + c

        def src(i, dest):
            return ins[i].at[dest] if scatter else ins[i]

        local = [pltpu.make_async_copy(src(i, me), outs[i].at[me], loc_sems.at[i]) for i in range(n)]
        for cp in local:
            cp.start()
        sends, recvs = [], []
        for k in range(1, NDEV):
            px = 1 - x if k & 4 else x
            py = 1 - y if k & 2 else y
            pc = 1 - c if k & 1 else c
            peer = 4 * px + 2 * py + pc
            for i in range(n):
                common = dict(send_sem=send_sems.at[i, k - 1], recv_sem=recv_sems.at[i, k - 1],
                              device_id=(px, py, pc), device_id_type=pl.DeviceIdType.MESH)
                sends.append(pltpu.make_async_remote_copy(src_ref=src(i, peer), dst_ref=outs[i].at[me], **common))
                recvs.append(pltpu.make_async_remote_copy(src_ref=src(i, peer), dst_ref=outs[i].at[peer], **common))
        for cp in sends:
            cp.start()
        for cp in recvs:
            cp.wait_recv()
        for cp in sends:
            cp.wait_send()
        for cp in local:
            cp.wait()

    hbm = pl.BlockSpec(memory_space=pl.ANY)
    out_shape = tuple(jax.ShapeDtypeStruct(a.shape if scatter else (NDEV,) + a.shape, a.dtype) for a in arrays)
    return pl.pallas_call(
        body, in_specs=[hbm] * n, out_specs=tuple([hbm] * n), out_shape=out_shape,
        scratch_shapes=[pltpu.SemaphoreType.DMA((n, NDEV - 1)), pltpu.SemaphoreType.DMA((n, NDEV - 1)),
                        pltpu.SemaphoreType.DMA((n,))],
        name=name)(*arrays)


def _adamw(w, m, v, parts, *, name):
    R, C = w.shape
    npart = parts.shape[0]
    cap = max(SUB, ((1 << 18) // C) // SUB * SUB)
    tr = _tile(R, cap, SUB)

    def body(w_ref, m_ref, v_ref, p_ref, g_ref, d_ref, mo_ref, vo_ref):
        g = p_ref[0]
        for k in range(1, npart):
            g = g + p_ref[k]
        mn = ADAM_B1 * m_ref[...] + (1.0 - ADAM_B1) * g
        vn = ADAM_B2 * v_ref[...] + (1.0 - ADAM_B2) * (g * g)
        m_hat = mn / (1.0 - ADAM_B1 ** ADAM_STEP)
        v_hat = vn / (1.0 - ADAM_B2 ** ADAM_STEP)
        g_ref[...] = g
        d_ref[...] = -ADAM_LR * (m_hat / (jnp.sqrt(v_hat) + ADAM_EPS) + ADAM_WD * w_ref[...])
        mo_ref[...] = mn
        vo_ref[...] = vn

    spec = pl.BlockSpec((tr, C), lambda i: (i, 0))
    oshape = jax.ShapeDtypeStruct((R, C), F32)
    return pl.pallas_call(
        body, grid=(R // tr,), in_specs=[spec] * 3 + [pl.BlockSpec((npart, tr, C), lambda i: (0, i, 0))],
        out_specs=(spec,) * 4, out_shape=(oshape,) * 4, compiler_params=_cp(1), name=name)(w, m, v, parts)


def _adamw_layer(l, w, m, v, parts, prev, *, name):
    L, R, C = w.shape
    npart = parts.shape[0]
    cap = max(SUB, ((1 << 18) // C) // SUB * SUB)
    tr = _tile(R, cap, SUB)
    nprev = 0 if prev is None else 4

    def body(*refs):
        w_ref, m_ref, v_ref, p_ref = refs[:4]
        g_ref, d_ref, mo_ref, vo_ref = refs[4 + nprev:]
        g = p_ref[0]
        for k in range(1, npart):
            g = g + p_ref[k]
        mn = ADAM_B1 * m_ref[...] + (1.0 - ADAM_B1) * g
        vn = ADAM_B2 * v_ref[...] + (1.0 - ADAM_B2) * (g * g)
        m_hat = mn / (1.0 - ADAM_B1 ** ADAM_STEP)
        v_hat = vn / (1.0 - ADAM_B2 ** ADAM_STEP)
        g_ref[...] = g
        d_ref[...] = -ADAM_LR * (m_hat / (jnp.sqrt(v_hat) + ADAM_EPS) + ADAM_WD * w_ref[...])
        mo_ref[...] = mn
        vo_ref[...] = vn

    spec = pl.BlockSpec((None, tr, C), lambda i: (l, i, 0))
    oshape = jax.ShapeDtypeStruct((L, R, C), F32)
    return pl.pallas_call(
        body, grid=(R // tr,),
        in_specs=[spec] * 3 + [pl.BlockSpec((npart, tr, C), lambda i: (0, i, 0))]
        + [pl.BlockSpec(memory_space=pl.ANY)] * nprev,
        out_specs=(spec,) * 4, out_shape=(oshape,) * 4,
        input_output_aliases={4 + k: k for k in range(nprev)},
        compiler_params=_cp(1), name=name)(w, m, v, parts, *(prev or ()))


_HBM = pl.BlockSpec(memory_space=pltpu.HBM)
_SEM = pl.BlockSpec(memory_space=pltpu.SEMAPHORE)
_EFFECT = pltpu.SideEffectType.DATAFLOW_SIDE_EFFECTING


def _peers():
    x, y, c = lax.axis_index("x"), lax.axis_index("y"), lax.axis_index("c")
    out = []
    for k in range(1, NDEV):
        px = 1 - x if k & 4 else x
        py = 1 - y if k & 2 else y
        pc = 1 - c if k & 1 else c
        out.append((k - 1, (px, py, pc), 4 * px + 2 * py + pc))
    return 4 * x + 2 * y + c, out


def _xchg_start(arrays, *, scatter, name):
    n = len(arrays)
    lands = [lax.empty(a.shape if scatter else (NDEV,) + a.shape, a.dtype) for a in arrays]

    def body(*refs):
        ins, lnd = refs[:n], refs[n:2 * n]
        send_sems, recv_sems = refs[2 * n], refs[2 * n + 1]
        token = refs[-1]
        me, peers = _peers()
        for k, dev, peer in peers:
            for i in range(n):
                pltpu.make_async_remote_copy(
                    src_ref=ins[i].at[peer] if scatter else ins[i], dst_ref=lnd[i].at[me],
                    send_sem=send_sems.at[i * (NDEV - 1) + k], recv_sem=recv_sems.at[i * (NDEV - 1) + k],
                    device_id=dev, device_id_type=pl.DeviceIdType.MESH).start()
        token[...] = jnp.zeros_like(token)

    sems = pltpu.SemaphoreType.DMA((n * (NDEV - 1),))
    res = pl.pallas_call(
        body, name=name,
        out_shape=(sems, sems) + tuple(pltpu.HBM(t.shape, t.dtype) for t in list(arrays) + lands)
        + (jax.ShapeDtypeStruct((SUB, LANE), F32),),
        in_specs=[_HBM] * (2 * n), out_specs=(_SEM, _SEM) + (_HBM,) * (2 * n) + (pl.BlockSpec(memory_space=pltpu.VMEM),),
        input_output_aliases={i: 2 + i for i in range(2 * n)},
        compiler_params=pltpu.CompilerParams(has_side_effects=_EFFECT),
    )(*[pltpu.with_memory_space_constraint(t, pltpu.HBM) for t in list(arrays) + lands])
    return dict(send=res[0], recv=res[1], thru=list(res[2:2 + 2 * n]), token=res[-1], scatter=scatter, n=n)


def _xchg_wait(handle, after, *, name):
    n, scatter = handle["n"], handle["scatter"]
    thru = handle["thru"]

    def body(*refs):
        ins, lnd = refs[:n], refs[n:2 * n]
        send_sems, recv_sems = refs[2 * n], refs[2 * n + 1]
        me, peers = _peers()
        for k, dev, peer in peers:
            for i in range(n):
                cp = pltpu.make_async_remote_copy(
                    src_ref=ins[i].at[peer] if scatter else ins[i], dst_ref=lnd[i].at[peer],
                    send_sem=send_sems.at[i * (NDEV - 1) + k], recv_sem=recv_sems.at[i * (NDEV - 1) + k],
                    device_id=dev, device_id_type=pl.DeviceIdType.MESH)
                cp.wait_send()
                cp.wait_recv()

    res = pl.pallas_call(
        body, name=name, out_shape=tuple(pltpu.HBM(t.shape, t.dtype) for t in thru),
        in_specs=[_HBM] * (2 * n) + [_SEM, _SEM, pl.BlockSpec(memory_space=pl.ANY)], out_specs=(_HBM,) * (2 * n),
        input_output_aliases={i: i for i in range(2 * n)},
        compiler_params=pltpu.CompilerParams(has_side_effects=_EFFECT),
    )(*thru, handle["send"], handle["recv"], after)
    return list(res[n:])


def _pad_heads(w, nh, hd, axis=-1):
    axis = axis % w.ndim
    shp = w.shape
    w = w.reshape(shp[:axis] + (nh, hd) + shp[axis + 1:])
    pads = [(0, 0)] * w.ndim
    pads[axis + 1] = (0, LANE - hd)
    return jnp.pad(w, pads).reshape(shp[:axis] + (nh * LANE,) + shp[axis + 1:])


def _unpad_heads(w, nh, hd, axis=-1):
    axis = axis % w.ndim
    shp = w.shape
    w = w.reshape(shp[:axis] + (nh, LANE) + shp[axis + 1:])
    w = lax.slice_in_dim(w, 0, hd, axis=axis + 1)
    return w.reshape(shp[:axis] + (nh * hd,) + shp[axis + 1:])


_O_DT = SSD_INNER + SSD_XBC
_O_CQ = _O_DT + SSD_HEADS
_O_CKV = _O_CQ + MLA_QR
_O_KR = _O_CKV + MLA_KVR
_O_SQ = _O_KR + MLA_ROPE
_O_SK = _O_SQ + SWA_HEADS * SWA_HD
_O_SV = _O_SK + SWA_KV * SWA_HD


def _w_in_to_padded(w):
    z, xbc, dt = w[..., :SSD_INNER], w[..., SSD_INNER:_O_DT], w[..., _O_DT:_O_CQ]
    cq, ckv, kr = w[..., _O_CQ:_O_CKV], w[..., _O_CKV:_O_KR], w[..., _O_KR:_O_SQ]
    sq, sk, sv = w[..., _O_SQ:_O_SK], w[..., _O_SK:_O_SV], w[..., _O_SV:]
    zeros = lambda n: jnp.zeros(w.shape[:-1] + (n,), w.dtype)
    misc = jnp.concatenate([dt, zeros(ROPE_LANE - SSD_HEADS), kr, zeros(LANE - ROPE_LANE - MLA_ROPE)], axis=-1)
    return jnp.concatenate([xbc, z, cq, ckv, misc, _pad_heads(sq, SWA_HEADS, SWA_HD),
                            _pad_heads(sk, SWA_KV, SWA_HD), _pad_heads(sv, SWA_KV, SWA_HD)], axis=-1)


def _w_in_from_padded(g):
    xbc, z, cq, ckv = g[..., C_XBC:C_Z], g[..., C_Z:C_CQ], g[..., C_CQ:C_CKV], g[..., C_CKV:C_MISC]
    dt, kr = g[..., C_MISC:C_MISC + SSD_HEADS], g[..., C_MISC + ROPE_LANE:C_MISC + ROPE_LANE + MLA_ROPE]
    sq = _unpad_heads(g[..., C_SQ:C_SK], SWA_HEADS, SWA_HD)
    sk = _unpad_heads(g[..., C_SK:C_SV], SWA_KV, SWA_HD)
    sv = _unpad_heads(g[..., C_SV:], SWA_KV, SWA_HD)
    return jnp.concatenate([z, xbc, dt, cq, ckv, kr, sq, sk, sv], axis=-1)


def _w_out_to_padded(w):
    a = SSD_INNER
    b = a + MLA_HEADS * MLA_V
    return jnp.concatenate([w[..., :a, :], _pad_heads(w[..., a:b, :], MLA_HEADS, MLA_V, axis=-2),
                            _pad_heads(w[..., b:, :], SWA_HEADS, SWA_HD, axis=-2)], axis=-2)


def _w_out_from_padded(g):
    a = SSD_INNER
    b = a + MLA_HEADS * LANE
    return jnp.concatenate([g[..., :a, :], _unpad_heads(g[..., a:b, :], MLA_HEADS, MLA_V, axis=-2),
                            _unpad_heads(g[..., b:, :], SWA_HEADS, SWA_HD, axis=-2)], axis=-2)


def _w_ukv_to_padded(w):
    w4 = w.reshape(w.shape[:-1] + (MLA_HEADS, MLA_NOPE + MLA_V))
    flat = lambda t: t.reshape(w.shape[:-1] + (MLA_HEADS * t.shape[-1],))
    return jnp.concatenate([_pad_heads(flat(w4[..., :MLA_NOPE]), MLA_HEADS, MLA_NOPE),
                            _pad_heads(flat(w4[..., MLA_NOPE:]), MLA_HEADS, MLA_V)], axis=-1)


def _w_ukv_from_padded(g):
    hw = MLA_HEADS * LANE
    gk = _unpad_heads(g[..., :hw], MLA_HEADS, MLA_NOPE).reshape(g.shape[:-1] + (MLA_HEADS, MLA_NOPE))
    gv = _unpad_heads(g[..., hw:], MLA_HEADS, MLA_V).reshape(g.shape[:-1] + (MLA_HEADS, MLA_V))
    return jnp.concatenate([gk, gv], axis=-1).reshape(g.shape[:-1] + (MLA_HEADS * (MLA_NOPE + MLA_V),))


def _pad_lane(v):
    return jnp.pad(v, [(0, 0)] * (v.ndim - 1) + [(0, LANE - v.shape[-1])])


def _rope_tables(positions):
    inv_freq = ROPE_THETA ** (-jnp.arange(0, MLA_ROPE, 2, dtype=F32) / MLA_ROPE)
    ang = positions.astype(F32).reshape(-1, 1) * inv_freq
    cos, sin = jnp.cos(ang), jnp.sin(ang)
    S = ang.shape[0]
    one, zero = jnp.ones((S, ROPE_LANE), F32), jnp.zeros((S, ROPE_LANE), F32)
    tail1, tail0 = jnp.ones((S, LANE - ROPE_LANE - MLA_ROPE), F32), jnp.zeros((S, LANE - ROPE_LANE - MLA_ROPE), F32)
    z16 = jnp.zeros_like(sin)
    return (jnp.concatenate([one, cos, cos, tail1], axis=1), jnp.concatenate([zero, -sin, z16, tail0], axis=1),
            jnp.concatenate([zero, z16, sin, tail0], axis=1))


def _layer_fwd(l, x_in, f_prev, gate_prev, mod, P, tabs):
    sh1, sc1, g1, sh2, sc2, g2 = [mod[k:k + 1] for k in range(6)]
    tag = f"l{l}_"
    if f_prev is None:
        x0 = x_in
        h1 = _norm_fwd(x0, P["n1g"], sc1, sh1, name=tag + "norm1")
    else:
        x0, h1 = _norm_fwd(x_in, P["n1g"], sc1, sh1, f=f_prev, gate=gate_prev, name=tag + "norm1")
    proj = _mm(h1, P["w_in"], name=tag + "proj")
    P.update(P.pop("late")(proj))
    y_ssd, hin = _ssd_fwd(proj, P["ssd_cw"], P["ssd_cb"], P["dtb"], P["alog"], P["dsk"],
                          P["ssd_ng"], name=tag + "ssd")
    q, k, v = _mla_prep_fwd(proj, *tabs, P["gq"], P["gkv"], P["w_uq"], P["w_ukv"], name=tag + "mla_prep")
    o_mla, lse_mla = _attn_fwd(q, k, v, name=tag + "mla_attn")
    o_swa, lse_swa = _swa_fwd(proj, P["sinks"], name=tag + "swa")
    ycat = jnp.concatenate([y_ssd.astype(_ACT), o_mla.astype(_ACT), o_swa.astype(_ACT)], axis=1)
    y = _mm(ycat, P["w_out"], name=tag + "out")
    x1, h2 = _norm_fwd(x0, P["n2g"], sc2, sh2, f=y, gate=g1, name=tag + "norm2")
    up = _mm(h2, P["w_up"], name=tag + "up")
    act = _ffn_act_fwd(up, P["fcw"], P["fcb"], name=tag + "ffn_act")
    f = _mm(act, P["w_down"], name=tag + "down")
    saved = dict(x0=x0, h1=h1, proj=proj, hin=hin, q=q, k=k, v=v, o_mla=o_mla, lse_mla=lse_mla, o_swa=o_swa,
                 lse_swa=lse_swa, ycat=ycat, y=y, x1=x1, h2=h2, up=up, act=act, f=f, mod=mod)
    return x1, f, g2, saved


def _layer_bwd(l, dxo, sv, P, tabs, on_ffn_grads):
    mod = sv["mod"]
    sh1, sc1, g1, sh2, sc2, g2 = [mod[k:k + 1] for k in range(6)]
    tag = f"l{l}_b_"
    G = {}
    df, dg2 = _gate_bwd(dxo, sv["f"], g2, name=tag + "gate2")
    dact = _mm(df, P["w_down"], tb=True, name=tag + "dact")
    G["w_down"] = _mm(sv["act"], df, ta=True, name=tag + "dw_down")
    du = _ffn_act_bwd(sv["up"], dact, P["fcw"], P["fcb"], name=tag + "ffn_act")
    dup, G["fcw"], G["fcb"] = _ffn_conv_bwd(du, sv["up"], P["fcw"], name=tag + "ffn_conv")
    dh2 = _mm(dup, P["w_up"], tb=True, name=tag + "dh2")
    G["w_up"] = _mm(sv["h2"], dup, ta=True, name=tag + "dw_up")
    token = on_ffn_grads(l, G)
    if token is not None:
        sc2 = sc2 + token
    dx1, G["n2g"], dsc2, dsh2 = _norm_bwd(dh2, sv["x1"], dxo, P["n2g"], sc2, name=tag + "norm2")
    dy, dg1 = _gate_bwd(dx1, sv["y"], g1, name=tag + "gate1")
    dycat = _mm(dy, P["w_out"], tb=True, name=tag + "dycat")
    G["w_out"] = _mm(sv["ycat"], dy, ta=True, name=tag + "dw_out")
    proj = sv["proj"]
    (dpre, dz, dmisc_ssd, G["ssd_cw"], G["ssd_cb"], G["dtb"], G["alog"], G["dsk"], G["ssd_ng"]) = _ssd_bwd(
        proj, dycat, sv["hin"], P["ssd_cw"], P["ssd_cb"], P["dtb"], P["alog"], P["dsk"],
        P["ssd_ng"], name=tag + "ssd")
    att = (sv["q"], sv["k"], sv["v"], sv["o_mla"], sv["lse_mla"], dycat)
    dq = _attn_bwd_dq(*att, name=tag + "mla_dq")
    dk, dv = _attn_bwd_dkv(*att, name=tag + "mla_dkv")
    dcq, dckv, dmisc, dqh, dkv, qn, kvn, G["gq"], G["gkv"] = _mla_prep_bwd(
        proj, dq, dk, dv, dmisc_ssd, *tabs, P["gq"], P["gkv"], P["w_uq"], P["w_ukv"], name=tag + "mla_prep")
    G["w_uq"] = _mm(qn, dqh, ta=True, name=tag + "dw_uq")
    G["w_ukv"] = _mm(kvn, dkv, ta=True, name=tag + "dw_ukv")
    dsq, dsk_, dsv_, G["sinks"] = _swa_bwd(proj, sv["o_swa"], sv["lse_swa"], dycat, P["sinks"], name=tag + "swa")
    dproj = jnp.concatenate([dpre, dz, dcq, dckv, dmisc, dsq, dsk_, dsv_], axis=1)
    dh1 = _mm(dproj, P["w_in"], tb=True, name=tag + "dh1")
    G["w_in"] = _mm(sv["h1"], dproj, ta=True, name=tag + "dw_in")
    dx0, G["n1g"], dsc1, dsh1 = _norm_bwd(dh1, sv["x0"], dx1, P["n1g"], sc1, name=tag + "norm1")
    G["mod"] = jnp.concatenate([dsh1, dsc1, dg1, dsh2, dsc2, dg2], axis=0)
    return dx0, G


def _local_step(x, tgt, mods, get_params, tabs, final_g, on_grads, on_ffn_grads):
    saved, params = [], []
    xin, f, gate = x, None, None
    for l in range(DEPTH):
        params.append(get_params(l, x if f is None else f))
        xin, f, gate, sv = _layer_fwd(l, xin, f, gate, mods[l], params[l], tabs)
        saved.append(sv)
    loss, dx, dfinal = _final_loss(xin, f, gate, final_g, tgt, name="final_loss")
    for l in reversed(range(DEPTH)):
        dx, G = _layer_bwd(l, dx, saved[l], params[l], tabs, on_ffn_grads)
        token = on_grads(l, G)
        if token is not None and l > 0:
            saved[l - 1]["mod"] = saved[l - 1]["mod"] + token
    return loss[0, 0], dx, dfinal


_WEIGHTS = ['ada_w', 'ada_b', 'norm1_g', 'norm2_g', 'w_in', 'ssd_conv_w', 'ssd_conv_b', 'ssd_dt_bias', 'ssd_a_log',
            'ssd_d', 'ssd_norm_g', 'mla_q_norm_g', 'mla_w_uq', 'mla_kv_norm_g', 'mla_w_ukv', 'swa_sinks', 'w_out',
            'ffn_w_up', 'ffn_conv_w', 'ffn_conv_b', 'ffn_w_down', 'final_norm_g']
_INPUTS = ['x', 'c', 'positions'] + _WEIGHTS + ['loss_target'] + ['m_' + n for n in _WEIGHTS] + ['v_' + n for n in _WEIGHTS]
_SMALL = [('ada_b', 'mod'), ('norm1_g', 'n1g'), ('norm2_g', 'n2g'), ('ssd_conv_b', 'ssd_cb'), ('ssd_dt_bias', 'dtb'),
          ('ssd_a_log', 'alog'), ('ssd_d', 'dsk'), ('ssd_norm_g', 'ssd_ng'), ('mla_q_norm_g', 'gq'),
          ('mla_kv_norm_g', 'gkv'), ('swa_sinks', 'sinks'), ('ffn_conv_b', 'fcb')]
_SHARDED = [('w_in', 'w_in', 2), ('ssd_conv_w', 'ssd_cw', 2), ('mla_w_uq', 'w_uq', 2), ('mla_w_ukv', 'w_ukv', 2),
            ('w_out', 'w_out', 1), ('ffn_w_up', 'w_up', 2), ('ffn_conv_w', 'fcw', 2), ('ffn_w_down', 'w_down', 1)]
_SHARDED_NAMES = [n for n, _, _ in _SHARDED]


def _pack_small(per_layer, final):
    parts = []
    for name, _ in _SMALL:
        v = per_layer[name]
        v = v.reshape(DEPTH, -1)
        pad = (-v.shape[1]) % LANE
        parts.append(jnp.pad(v, ((0, 0), (0, pad))).reshape(-1))
    parts.append(final.reshape(-1))
    return jnp.concatenate(parts).reshape(-1, LANE)


def _unpack_small(packed, shapes):
    flat = packed.reshape(-1)
    out, off = {}, 0
    for name, _ in _SMALL:
        n = math.prod(shapes[name][1:])
        npad = n + (-n) % LANE
        out[name] = flat[off:off + DEPTH * npad].reshape(DEPTH, npad)[:, :n].reshape(shapes[name])
        off += DEPTH * npad
    out['final_norm_g'] = flat[off:off + D]
    return out


def _shard_major(g, axis):
    shp = g.shape
    g = g.reshape(shp[:axis] + (NDEV, shp[axis] // NDEV) + shp[axis + 1:])
    return jnp.moveaxis(g, axis, 0)


def _unshard(g, axis):
    g = jnp.moveaxis(g, 0, axis)
    shp = g.shape
    return g.reshape(shp[:axis] + (shp[axis] * shp[axis + 1],) + shp[axis + 2:])


def kernel(x, c, positions, ada_w, ada_b, norm1_g, norm2_g, w_in, ssd_conv_w, ssd_conv_b, ssd_dt_bias, ssd_a_log, ssd_d, ssd_norm_g, mla_q_norm_g, mla_w_uq, mla_kv_norm_g, mla_w_ukv, swa_sinks, w_out, ffn_w_up, ffn_conv_w, ffn_conv_b, ffn_w_down, final_norm_g, loss_target, m_ada_w, m_ada_b, m_norm1_g, m_norm2_g, m_w_in, m_ssd_conv_w, m_ssd_conv_b, m_ssd_dt_bias, m_ssd_a_log, m_ssd_d, m_ssd_norm_g, m_mla_q_norm_g, m_mla_w_uq, m_mla_kv_norm_g, m_mla_w_ukv, m_swa_sinks, m_w_out, m_ffn_w_up, m_ffn_conv_w, m_ffn_conv_b, m_ffn_w_down, m_final_norm_g, v_ada_w, v_ada_b, v_norm1_g, v_norm2_g, v_w_in, v_ssd_conv_w, v_ssd_conv_b, v_ssd_dt_bias, v_ssd_a_log, v_ssd_d, v_ssd_norm_g, v_mla_q_norm_g, v_mla_w_uq, v_mla_kv_norm_g, v_mla_w_ukv, v_swa_sinks, v_w_out, v_ffn_w_up, v_ffn_conv_w, v_ffn_conv_b, v_ffn_w_down, v_final_norm_g):
    a = dict(zip(_INPUTS, (x, c, positions, ada_w, ada_b, norm1_g, norm2_g, w_in, ssd_conv_w, ssd_conv_b, ssd_dt_bias, ssd_a_log, ssd_d, ssd_norm_g, mla_q_norm_g, mla_w_uq, mla_kv_norm_g, mla_w_ukv, swa_sinks, w_out, ffn_w_up, ffn_conv_w, ffn_conv_b, ffn_w_down, final_norm_g, loss_target, m_ada_w, m_ada_b, m_norm1_g, m_norm2_g, m_w_in, m_ssd_conv_w, m_ssd_conv_b, m_ssd_dt_bias, m_ssd_a_log, m_ssd_d, m_ssd_norm_g, m_mla_q_norm_g, m_mla_w_uq, m_mla_kv_norm_g, m_mla_w_ukv, m_swa_sinks, m_w_out, m_ffn_w_up, m_ffn_conv_w, m_ffn_conv_b, m_ffn_w_down, m_final_norm_g, v_ada_w, v_ada_b, v_norm1_g, v_norm2_g, v_w_in, v_ssd_conv_w, v_ssd_conv_b, v_ssd_dt_bias, v_ssd_a_log, v_ssd_d, v_ssd_norm_g, v_mla_q_norm_g, v_mla_w_uq, v_mla_kv_norm_g, v_mla_w_ukv, v_swa_sinks, v_w_out, v_ffn_w_up, v_ffn_conv_w, v_ffn_conv_b, v_ffn_w_down, v_final_norm_g)))
    axes = ("x", "y", "c")
    me = 4 * lax.axis_index("x") + 2 * lax.axis_index("y") + lax.axis_index("c")
    ncol = ada_w.shape[-1]

    c_all = _exchange([c], scatter=False, name="gather_c")[0]
    c_act = _silu_call(c_all.reshape(NDEV, D), name="c_act")
    mod_part = jnp.stack([_mm(c_act, ada_w[l], name=f"mod{l}") for l in range(DEPTH)])
    mod_all = _exchange([mod_part], scatter=False, name="gather_mod")[0]
    mod_mine = lax.dynamic_index_in_dim(mod_all, me, axis=2, keepdims=False)
    mods = (jnp.moveaxis(mod_mine, 0, 1).reshape(DEPTH, 6 * D) + ada_b).reshape(DEPTH, 6, D)
    tabs = _rope_tables(positions)

    mxu_names = ('w_in', 'mla_w_uq', 'mla_w_ukv', 'w_out', 'ffn_w_up', 'ffn_w_down')
    shard_of = {n: (key, ax) for n, key, ax in _SHARDED}
    early_w, late_w = _SHARDED_NAMES[:4], _SHARDED_NAMES[4:]
    mods, raw = lax.optimization_barrier((mods, {n: a[n] for n in _SHARDED_NAMES}))
    own_of = lambda names, l: [raw[n][l].astype(_MXU) if n in mxu_names else raw[n][l] for n in names]
    own = [{"early": own_of(early_w, l), "late": own_of(late_w, l)} for l in range(DEPTH)]
    gathers = [{grp: _xchg_start(own[l][grp], scatter=False, name=f"gather_start_{grp}{l}") for grp in ("early", "late")}
               for l in range(DEPTH)]

    def place_own(landed, mine):
        return [lax.dynamic_update_index_in_dim(t, o, me, 0) for t, o in zip(landed, mine)]

    def gathered(l, grp, names, after):
        landed = place_own(_xchg_wait(gathers[l][grp], after, name=f"gather_wait_{grp}{l}"), own[l][grp])
        return {n: _unshard(g, shard_of[n][1] - 1) for n, g in zip(names, landed)}

    def get_params(l, after):
        full = gathered(l, "early", early_w, after)
        vec = lambda t: t[l].reshape(1, -1)

        def late(after2):
            rest = gathered(l, "late", late_w, after2)
            return dict(w_out=_w_out_to_padded(rest['w_out']), w_up=rest['ffn_w_up'], w_down=rest['ffn_w_down'],
                        fcw=rest['ffn_conv_w'])

        return dict(
            w_in=_w_in_to_padded(full['w_in']), w_uq=_pad_heads(full['mla_w_uq'], MLA_HEADS, MLA_NOPE + MLA_ROPE),
            w_ukv=_w_ukv_to_padded(full['mla_w_ukv']), ssd_cw=full['ssd_conv_w'], late=late,
            ssd_cb=vec(ssd_conv_b), dtb=vec(_pad_lane(ssd_dt_bias)), alog=vec(_pad_lane(ssd_a_log)),
            dsk=vec(_pad_lane(ssd_d)), ssd_ng=vec(ssd_norm_g), gq=vec(mla_q_norm_g), gkv=vec(mla_kv_norm_g),
            sinks=vec(_pad_lane(swa_sinks)), fcb=vec(ffn_conv_b), n1g=vec(norm1_g), n2g=vec(norm2_g))

    unpad = dict(w_in=_w_in_from_padded, w_out=_w_out_from_padded, w_ukv=_w_ukv_from_padded,
                 w_uq=lambda g: _unpad_heads(g, MLA_HEADS, MLA_NOPE + MLA_ROPE))
    ffn_w, mixer_w = _SHARDED_NAMES[5:], _SHARDED_NAMES[:5]
    grads = [None] * DEPTH
    sent = [dict() for _ in range(DEPTH)]
    scatters = [dict() for _ in range(DEPTH)]

    def send(l, grp, names, G):
        sent[l][grp] = [_shard_major(unpad.get(shard_of[n][0], lambda g: g)(G[shard_of[n][0]]), shard_of[n][1] - 1)
                        for n in names]
        scatters[l][grp] = _xchg_start(sent[l][grp], scatter=True, name=f"scatter_start_{grp}{l}")
        return scatters[l][grp]["token"][0, 0]

    def on_ffn_grads(l, G):
        return send(l, "ffn", ffn_w, G)

    def on_grads(l, G):
        grads[l] = G
        return send(l, "mixer", mixer_w, G)

    mods = mods + sum(g[grp]["token"][0, 0] for g in gathers for grp in ("early", "late"))
    loss, dx, dfinal = _local_step(x[0], loss_target[0], mods, get_params, tabs, final_norm_g.reshape(1, D),
                                   on_grads, on_ffn_grads)
    loss = lax.psum(loss, axes)

    stack = lambda key: jnp.stack([grads[l][key] for l in range(DEPTH)])
    small_g = {name: stack(key).reshape(DEPTH, -1) for name, key in _SMALL}
    small_parts = _exchange([_pack_small(small_g, dfinal)], scatter=False, name="gather_small")[0]

    out_g, out_d, out_m, out_v = {}, {}, {}, {}
    chain = {name: None for name in _SHARDED_NAMES}
    for l in reversed(range(DEPTH)):
        for grp, names in (("ffn", ffn_w), ("mixer", mixer_w)):
            landed = _xchg_wait(scatters[l][grp], dx, name=f"scatter_wait_{grp}{l}")
            parts = place_own(landed, [lax.dynamic_index_in_dim(t, me, 0, keepdims=False) for t in sent[l][grp]])
            for name, pv in zip(names, parts):
                chain[name] = _adamw_layer(l, a[name], a['m_' + name], a['v_' + name], pv, chain[name],
                                           name=f"adamw_{name}{l}")
    for name in _SHARDED_NAMES:
        out_g[name], out_d[name], out_m[name], out_v[name] = chain[name]

    def update(name, wv, mv, vv, pv):
        shp = wv.shape
        r = lambda t: t.reshape((-1, shp[-1]))
        res = _adamw(r(wv), r(mv), r(vv), pv.reshape((pv.shape[0], -1, shp[-1])), name="adamw_" + name)
        out_g[name], out_d[name], out_m[name], out_v[name] = [t.reshape(shp) for t in res]

    n_ada = DEPTH * 6 * D // LANE
    dmod_all = small_parts[:, :n_ada].reshape(NDEV, DEPTH, 6 * D)
    dmod_mine = lax.dynamic_slice_in_dim(dmod_all, me * ncol, ncol, axis=2)
    g_ada = jnp.stack([_mm(c_act, dmod_mine[:, l], ta=True, name=f"dw_ada{l}") for l in range(DEPTH)])
    update('ada_w', ada_w, m_ada_w, v_ada_w, g_ada[None])
    shapes = {n: a[n].shape for n, _ in _SMALL}
    pk = lambda pre: _pack_small({n: a[pre + n] for n, _ in _SMALL}, a[pre + 'final_norm_g'])
    res = _adamw(pk(''), pk('m_'), pk('v_'), small_parts, name="adamw_small")
    for dst, t in zip((out_g, out_d, out_m, out_v), res):
        dst.update(_unpack_small(t, shapes))

    outs = [loss, dx[None]]
    for dct in (out_g, out_d, out_m, out_v):
        outs += [dct[n] for n in _WEIGHTS]
    return tuple(outs)
```

```python
import functools
import math

import jax
import jax.numpy as jnp
from jax import lax
from jax.experimental import pallas as pl
from jax.experimental.pallas import tpu as pltpu

F32 = jnp.float32
_MXU = jnp.bfloat16
_ACT = jnp.bfloat16
_HI = lax.Precision.HIGHEST
EPS = 1e-6
NDEV = 8
DEPTH = 4
D = 1024
LANE = 128
SUB = 8
VMEM_LIMIT = 56 * 1024 * 1024

SSD_INNER, SSD_STATE, SSD_HEADS, SSD_GROUPS, SSD_CHUNK, SSD_CONV = 512, 128, 8, 2, 128, 4
SSD_XBC = SSD_INNER + 2 * SSD_GROUPS * SSD_STATE
MLA_HEADS, MLA_NOPE, MLA_ROPE, MLA_V, MLA_QR, MLA_KVR = 4, 64, 32, 64, 256, 128
SWA_HEADS, SWA_KV, SWA_HD, WINDOW = 4, 2, 64, 128
D_FF, FFN_CONV = 2816, 3
D_IN = 2472
ROPE_THETA = 10000.0
C_XBC, C_Z, C_CQ, C_CKV, C_MISC, C_SQ, C_SK, C_SV, D_INP = 0, 1024, 1536, 1792, 1920, 2048, 2560, 2816, 3072
ROPE_LANE = 64
D_MIXP = 1536

ADAM_LR, ADAM_B1, ADAM_B2, ADAM_EPS, ADAM_WD, ADAM_STEP = 0.001, 0.9, 0.999, 1e-08, 0.01, 10

TS_ROW = 512
TS_FFN = 256
TQ_ATT = 512
TS_SWA = 512


def _tile(n, cap, q=LANE):
    best = None
    for t in range(q, min(n, cap) + 1, q):
        if n % t == 0:
            best = t
    return n if best is None else best


def _cp(ngrid):
    return pltpu.CompilerParams(dimension_semantics=("arbitrary",) * ngrid, vmem_limit_bytes=VMEM_LIMIT)


def _dot(a, b):
    return jnp.dot(a.astype(_MXU), b.astype(_MXU), preferred_element_type=F32)


def _dot_nt(a, b):
    return lax.dot_general(a.astype(_MXU), b.astype(_MXU), (((1,), (1,)), ((), ())), preferred_element_type=F32)


def _dot_tn(a, b):
    return jnp.dot(a.T.astype(_MXU), b.astype(_MXU), preferred_element_type=F32)


def _sigmoid(x):
    return 1.0 / (1.0 + jnp.exp(-x))


def _silu(x):
    return x * _sigmoid(x)


def _dsilu(x):
    s = _sigmoid(x)
    return s * (1.0 + x * (1.0 - s))


def _softplus(x):
    u = jnp.exp(-jnp.abs(x))
    w = 1.0 + u
    log1p = jnp.where(w == 1.0, u, jnp.log(w) * u / jnp.where(w == 1.0, 1.0, w - 1.0))
    return jnp.maximum(x, 0.0) + log1p


def _colsum(x):
    return jnp.sum(x, axis=0, keepdims=True)


def _rowsum(x):
    return jnp.sum(x, axis=1, keepdims=True)


def _shift_down(t, halo, j):
    if j == 0:
        return t
    n = t.shape[0]
    rolled = pltpu.roll(t, j, 0)
    row = lax.broadcasted_iota(jnp.int32, (SUB, t.shape[1]), 0)
    first = jnp.where(row < j, pltpu.roll(halo, j, 0), rolled[:SUB])
    return jnp.concatenate([first, rolled[SUB:]], axis=0) if n > SUB else first


def _shift_up(t, halo, j):
    if j == 0:
        return t
    n = t.shape[0]
    rolled = pltpu.roll(t, n - j, 0)
    row = lax.broadcasted_iota(jnp.int32, (SUB, t.shape[1]), 0)
    last = jnp.where(row >= SUB - j, pltpu.roll(halo, SUB - j, 0), rolled[n - SUB:])
    return jnp.concatenate([rolled[:n - SUB], last], axis=0) if n > SUB else last


def _mm(a, b, *, ta=False, tb=False, out_dtype=F32, name):
    if ta:
        K, M = a.shape
    else:
        M, K = a.shape
    if tb:
        N, K2 = b.shape
    else:
        K2, N = b.shape
    assert K == K2, (a.shape, b.shape, ta, tb)
    tm, tn, tk = _tile(M, 1536), _tile(N, 1408), _tile(K, 1536)
    nk = K // tk
    dn = (((0 if ta else 1,), (1 if tb else 0,)), ((), ()))

    def body(a_ref, b_ref, o_ref, acc_ref):
        k = pl.program_id(2)
        part = lax.dot_general(a_ref[...].astype(_MXU), b_ref[...].astype(_MXU), dn, preferred_element_type=F32)

        @pl.when(k == 0)
        def _():
            acc_ref[...] = part

        @pl.when(k > 0)
        def _():
            acc_ref[...] += part

        @pl.when(k == nk - 1)
        def _():
            o_ref[...] = acc_ref[...].astype(out_dtype)

    a_spec = pl.BlockSpec((tk, tm), lambda i, j, k: (k, i)) if ta else pl.BlockSpec((tm, tk), lambda i, j, k: (i, k))
    b_spec = pl.BlockSpec((tn, tk), lambda i, j, k: (j, k)) if tb else pl.BlockSpec((tk, tn), lambda i, j, k: (k, j))
    return pl.pallas_call(
        body, grid=(M // tm, N // tn, nk), in_specs=[a_spec, b_spec],
        out_specs=pl.BlockSpec((tm, tn), lambda i, j, k: (i, j)),
        out_shape=jax.ShapeDtypeStruct((M, N), out_dtype),
        scratch_shapes=[pltpu.VMEM((tm, tn), F32)], compiler_params=_cp(3), name=name)(a, b)


def _row(ts, w, col=0):
    return pl.BlockSpec((ts, w), lambda i: (i, col))


def _vec(w, r=1):
    return pl.BlockSpec((r, w), lambda i: (0, 0))


def _silu_call(x, name):
    def body(x_ref, o_ref):
        o_ref[...] = _silu(x_ref[...])
    return pl.pallas_call(body, out_shape=jax.ShapeDtypeStruct(x.shape, F32), name=name)(x)


def _norm_fwd(x, g, sc, sh, *, f=None, gate=None, name):
    S, dm = x.shape
    ts = _tile(S, TS_ROW, SUB)
    res = f is not None

    def body(*refs):
        if res:
            x_ref, f_ref, gate_ref, g_ref, sc_ref, sh_ref, xo_ref, h_ref = refs
            xv = x_ref[...] + gate_ref[...] * f_ref[...]
            xo_ref[...] = xv
        else:
            x_ref, g_ref, sc_ref, sh_ref, h_ref = refs
            xv = x_ref[...]
        rstd = lax.rsqrt(jnp.mean(xv * xv, axis=-1, keepdims=True) + EPS)
        h_ref[...] = ((xv * rstd) * g_ref[...] * (1.0 + sc_ref[...]) + sh_ref[...]).astype(_ACT)

    ins = [x] + ([f, gate] if res else []) + [g, sc, sh]
    in_specs = [_row(ts, dm)] + ([_row(ts, dm), _vec(dm)] if res else []) + [_vec(dm)] * 3
    h_shape = jax.ShapeDtypeStruct((S, dm), _ACT)
    if res:
        out_shape, out_specs = (jax.ShapeDtypeStruct((S, dm), F32), h_shape), (_row(ts, dm), _row(ts, dm))
    else:
        out_shape, out_specs = h_shape, _row(ts, dm)
    return pl.pallas_call(body, grid=(S // ts,), in_specs=in_specs, out_specs=out_specs, out_shape=out_shape,
                          compiler_params=_cp(1), name=name)(*ins)


def _norm_bwd(dh, x, dres, g, sc, *, name):
    S, dm = x.shape
    ts = _tile(S, TS_ROW, SUB)

    def body(dh_ref, x_ref, dres_ref, g_ref, sc_ref, dx_ref, dg_ref, dsc_ref, dsh_ref):
        i = pl.program_id(0)
        xv = x_ref[...]
        dhv = dh_ref[...]
        rstd = lax.rsqrt(jnp.mean(xv * xv, axis=-1, keepdims=True) + EPS)
        xhat = xv * rstd
        hn = xhat * g_ref[...]
        dhn = dhv * (1.0 + sc_ref[...])
        dxh = dhn * g_ref[...]
        dx_ref[...] = dres_ref[...] + rstd * (dxh - xhat * jnp.mean(dxh * xhat, axis=-1, keepdims=True))

        @pl.when(i == 0)
        def _():
            dg_ref[...] = jnp.zeros_like(dg_ref)
            dsc_ref[...] = jnp.zeros_like(dsc_ref)
            dsh_ref[...] = jnp.zeros_like(dsh_ref)

        dg_ref[...] += _colsum(dhn * xhat)
        dsc_ref[...] += _colsum(dhv * hn)
        dsh_ref[...] += _colsum(dhv)

    vshape = jax.ShapeDtypeStruct((1, dm), F32)
    return pl.pallas_call(
        body, grid=(S // ts,), in_specs=[_row(ts, dm)] * 3 + [_vec(dm)] * 2,
        out_specs=(_row(ts, dm), _vec(dm), _vec(dm), _vec(dm)),
        out_shape=(jax.ShapeDtypeStruct((S, dm), F32), vshape, vshape, vshape),
        compiler_params=_cp(1), name=name)(dh, x, dres, g, sc)


def _gate_bwd(dxo, f, gate, *, name):
    S, dm = f.shape
    ts = _tile(S, TS_ROW, SUB)

    def body(dxo_ref, f_ref, gate_ref, df_ref, dgate_ref):
        i = pl.program_id(0)
        dv = dxo_ref[...]
        df_ref[...] = (gate_ref[...] * dv).astype(_ACT)

        @pl.when(i == 0)
        def _():
            dgate_ref[...] = jnp.zeros_like(dgate_ref)

        dgate_ref[...] += _colsum(dv * f_ref[...])

    return pl.pallas_call(
        body, grid=(S // ts,), in_specs=[_row(ts, dm), _row(ts, dm), _vec(dm)],
        out_specs=(_row(ts, dm), _vec(dm)),
        out_shape=(jax.ShapeDtypeStruct((S, dm), _ACT), jax.ShapeDtypeStruct((1, dm), F32)),
        compiler_params=_cp(1), name=name)(dxo, f, gate)


def _final_loss(x, f, gate, g, tgt, *, name):
    S, dm = x.shape
    ts = _tile(S, TS_ROW, SUB)

    def body(x_ref, f_ref, gate_ref, g_ref, t_ref, loss_ref, dx_ref, dg_ref):
        i = pl.program_id(0)
        xv = x_ref[...] + gate_ref[...] * f_ref[...]
        rstd = lax.rsqrt(jnp.mean(xv * xv, axis=-1, keepdims=True) + EPS)
        xhat = xv * rstd
        err = xhat * g_ref[...] - t_ref[...]
        dy = err * (1.0 / dm)
        dxh = dy * g_ref[...]
        dx_ref[...] = rstd * (dxh - xhat * jnp.mean(dxh * xhat, axis=-1, keepdims=True))

        @pl.when(i == 0)
        def _():
            loss_ref[...] = jnp.zeros_like(loss_ref)
            dg_ref[...] = jnp.zeros_like(dg_ref)

        loss_ref[...] += jnp.full((1, LANE), 0.5 * jnp.sum(jnp.mean(err * err, axis=-1, keepdims=True)), F32)
        dg_ref[...] += _colsum(dy * xhat)

    return pl.pallas_call(
        body, grid=(S // ts,), in_specs=[_row(ts, dm), _row(ts, dm), _vec(dm), _vec(dm), _row(ts, dm)],
        out_specs=(_vec(LANE), _row(ts, dm), _vec(dm)),
        out_shape=(jax.ShapeDtypeStruct((1, LANE), F32), jax.ShapeDtypeStruct((S, dm), F32),
                   jax.ShapeDtypeStruct((1, dm), F32)),
        compiler_params=_cp(1), name=name)(x, f, gate, g, tgt)


def _ffn_conv(t, halo, cw_ref, cb_ref):
    return ((cb_ref[...] + _shift_down(t, halo, 2) * cw_ref[0:1, :]) + _shift_down(t, halo, 1) * cw_ref[1:2, :]) \
        + t * cw_ref[2:3, :]


def _prev_halo_spec(ts, w, col=0):
    return pl.BlockSpec((SUB, w), lambda i: (jnp.maximum(i * (ts // SUB) - 1, 0), col))


def _ffn_act_fwd(up, cw, cb, *, name):
    S, w2 = up.shape
    ff = w2 // 2
    ts = _tile(S, TS_FFN, SUB)

    def body(up_ref, halo_ref, cw_ref, cb_ref, act_ref):
        i = pl.program_id(0)
        t = up_ref[...]
        halo = jnp.where(i > 0, halo_ref[...], 0.0)
        u = _ffn_conv(t, halo, cw_ref, cb_ref)
        act_ref[...] = (_silu(u[:, :ff]) * u[:, ff:]).astype(_ACT)

    return pl.pallas_call(
        body, grid=(S // ts,), in_specs=[_row(ts, w2), _prev_halo_spec(ts, w2), _vec(w2, FFN_CONV), _vec(w2)],
        out_specs=_row(ts, ff), out_shape=jax.ShapeDtypeStruct((S, ff), _ACT),
        compiler_params=_cp(1), name=name)(up, up, cw, cb)


def _ffn_act_bwd(up, dact, cw, cb, *, name):
    S, w2 = up.shape
    ff = w2 // 2
    ts = _tile(S, TS_FFN, SUB)

    def body(up_ref, halo_ref, dact_ref, cw_ref, cb_ref, du_ref):
        i = pl.program_id(0)
        t = up_ref[...]
        halo = jnp.where(i > 0, halo_ref[...], 0.0)
        u = _ffn_conv(t, halo, cw_ref, cb_ref)
        a, b = u[:, :ff], u[:, ff:]
        da = dact_ref[...]
        du_ref[:, :ff] = da * b * _dsilu(a)
        du_ref[:, ff:] = da * _silu(a)

    return pl.pallas_call(
        body, grid=(S // ts,),
        in_specs=[_row(ts, w2), _prev_halo_spec(ts, w2), _row(ts, ff), _vec(w2, FFN_CONV), _vec(w2)],
        out_specs=_row(ts, w2), out_shape=jax.ShapeDtypeStruct((S, w2), F32),
        compiler_params=_cp(1), name=name)(up, up, dact, cw, cb)


def _ffn_conv_bwd(du, up, cw, *, name):
    S, w2 = up.shape
    ts = _tile(S, TS_FFN, SUB)
    n = S // ts

    def body(du_ref, nxt_ref, up_ref, halo_ref, cw_ref, dup_ref, dcw_ref, dcb_ref):
        i = pl.program_id(0)
        dv = du_ref[...]
        nxt = jnp.where(i < n - 1, nxt_ref[...], 0.0)
        t = up_ref[...]
        halo = jnp.where(i > 0, halo_ref[...], 0.0)
        dup = (dv * cw_ref[2:3, :] + _shift_up(dv, nxt, 1) * cw_ref[1:2, :]) + _shift_up(dv, nxt, 2) * cw_ref[0:1, :]
        dup_ref[...] = dup.astype(_ACT)

        @pl.when(i == 0)
        def _():
            dcw_ref[...] = jnp.zeros_like(dcw_ref)
            dcb_ref[...] = jnp.zeros_like(dcb_ref)

        dcb_ref[...] += _colsum(dv)
        dcw_ref[2:3, :] += _colsum(dv * t)
        dcw_ref[1:2, :] += _colsum(dv * _shift_down(t, halo, 1))
        dcw_ref[0:1, :] += _colsum(dv * _shift_down(t, halo, 2))

    nxt_spec = pl.BlockSpec((SUB, w2), lambda i: (jnp.minimum((i + 1) * (ts // SUB), S // SUB - 1), 0))
    return pl.pallas_call(
        body, grid=(n,),
        in_specs=[_row(ts, w2), nxt_spec, _row(ts, w2), _prev_halo_spec(ts, w2), _vec(w2, FFN_CONV)],
        out_specs=(_row(ts, w2), _vec(w2, FFN_CONV), _vec(w2)),
        out_shape=(jax.ShapeDtypeStruct((S, w2), _ACT), jax.ShapeDtypeStruct((FFN_CONV, w2), F32),
                   jax.ShapeDtypeStruct((1, w2), F32)),
        compiler_params=_cp(1), name=name)(du, du, up, up, cw)


def _ssd_core(pre, halo, misc, cw_ref, cb_ref, dtb, alog):
    q = pre.shape[0]
    conv = cb_ref[...]
    for k in range(SSD_CONV):
        conv = conv + _shift_down(pre, halo, SSD_CONV - 1 - k) * cw_ref[k:k + 1, :]
    xbc = _silu(conv)
    raw = misc + dtb
    dt = _softplus(raw)
    a = -jnp.exp(alog)
    r = lax.broadcasted_iota(jnp.int32, (q, q), 0)
    c = lax.broadcasted_iota(jnp.int32, (q, q), 1)
    tri = r >= c
    acum = jnp.dot(tri.astype(F32), dt * a, precision=_HI, preferred_element_type=F32)
    return conv, xbc, raw, dt, a, acum, acum.T, tri


def _sel(v, j, lo):
    return jnp.where(lo, v[:, 2 * j:2 * j + 1], v[:, 2 * j + 1:2 * j + 2])


def _ssd_pair_fwd(xbc, dt, acum, acum_t, tri, dsk, g_mat, b_mat, c_mat, h_pair, j, lo, lo1, sub_lo):
    q = xbc.shape[0]
    x = xbc[:, LANE * j:LANE * (j + 1)]
    dtp = _sel(dt, j, lo)
    ap = _sel(acum, j, lo)
    xd = x * dtp
    ls, ms = [], []
    for h in (2 * j, 2 * j + 1):
        seg = acum[:, h:h + 1] - acum_t[h:h + 1, :]
        l_mat = jnp.exp(jnp.where(tri, seg, -jnp.inf))
        ls.append(l_mat)
        ms.append(g_mat * l_mat)
    yd = jnp.where(lo, _dot(ms[0], xd), _dot(ms[1], xd))
    ea = jnp.exp(ap)
    yo = _dot_nt(c_mat, h_pair) * ea
    dp = _sel(dsk, j, lo1)
    alast = acum[q - 1:q, :]
    e = jnp.exp(_sel(alast, j, lo1) - ap)
    cd = jnp.where(sub_lo, jnp.exp(alast[:, 2 * j:2 * j + 1]), jnp.exp(alast[:, 2 * j + 1:2 * j + 2]))
    return dict(x=x, dtp=dtp, ap=ap, xd=xd, ls=ls, ms=ms, ea=ea, yo=yo, dp=dp, e=e, cd=cd, y=yd + yo + x * dp)


def _gnorm(yg):
    half = SSD_INNER // SSD_GROUPS
    rstds, yns = [], []
    for g in range(SSD_GROUPS):
        part = yg[:, half * g:half * (g + 1)]
        rstd = lax.rsqrt(jnp.mean(part * part, axis=-1, keepdims=True) + EPS)
        rstds.append(rstd)
        yns.append(part * rstd)
    return rstds, yns


def _ssd_specs(nc, rev):
    q = SSD_CHUNK
    cidx = (lambda i: nc - 1 - i) if rev else (lambda i: i)
    return [
        pl.BlockSpec((q, SSD_XBC), lambda i: (cidx(i), C_XBC // SSD_XBC)),
        pl.BlockSpec((SUB, SSD_XBC), lambda i: (jnp.maximum(cidx(i) * (q // SUB) - 1, 0), C_XBC // SSD_XBC)),
        pl.BlockSpec((q, SSD_INNER), lambda i: (cidx(i), C_Z // SSD_INNER)),
        pl.BlockSpec((q, LANE), lambda i: (cidx(i), C_MISC // LANE)),
    ]


def _ssd_param_specs():
    return [_vec(SSD_XBC, SSD_CONV), _vec(SSD_XBC), _vec(LANE), _vec(LANE), _vec(LANE), _vec(SSD_INNER)]


def _ssd_fwd(proj, cw, cb, dtb, alog, dsk, ng, *, name):
    S = proj.shape[0]
    q = SSD_CHUNK
    nc = S // q
    npair = SSD_HEADS // 2

    def body(xbc_ref, halo_ref, z_ref, misc_ref, cw_ref, cb_ref, dtb_ref, alog_ref, dsk_ref, ng_ref,
             y_ref, hin_ref, h_ref):
        c = pl.program_id(0)

        @pl.when(c == 0)
        def _():
            h_ref[...] = jnp.zeros_like(h_ref)

        pre = xbc_ref[...]
        halo = jnp.where(c > 0, halo_ref[...], 0.0)
        conv, xbc, raw, dt, a, acum, acum_t, tri = _ssd_core(pre, halo, misc_ref[...], cw_ref, cb_ref,
                                                             dtb_ref[...], alog_ref[...])
        lo = lax.broadcasted_iota(jnp.int32, (q, LANE), 1) < LANE // 2
        lo1 = lo[:1]
        sub_lo = lax.broadcasted_iota(jnp.int32, (LANE, LANE), 0) < LANE // 2
        ys = []
        for g in range(SSD_GROUPS):
            b_mat = xbc[:, SSD_INNER + SSD_STATE * g:SSD_INNER + SSD_STATE * (g + 1)]
            c_mat = xbc[:, SSD_INNER + SSD_STATE * (SSD_GROUPS + g):SSD_INNER + SSD_STATE * (SSD_GROUPS + g + 1)]
            g_mat = _dot_nt(c_mat, b_mat)
            for jj in range(npair // SSD_GROUPS):
                j = g * (npair // SSD_GROUPS) + jj
                hj = h_ref[j]
                p = _ssd_pair_fwd(xbc, dt, acum, acum_t, tri, dsk_ref[...], g_mat, b_mat, c_mat, hj, j, lo, lo1, sub_lo)
                ys.append(p["y"])
                hin_ref[0, j] = hj
                h_ref[j] = p["cd"] * hj + _dot_tn(p["xd"] * p["e"], b_mat)
        yg = jnp.concatenate(ys, axis=1) * _silu(z_ref[...])
        _, yns = _gnorm(yg)
        y_ref[...] = jnp.concatenate(yns, axis=1) * ng_ref[...]

    return pl.pallas_call(
        body, grid=(nc,), in_specs=_ssd_specs(nc, False) + _ssd_param_specs(),
        out_specs=(pl.BlockSpec((q, SSD_INNER), lambda i: (i, 0)),
                   pl.BlockSpec((1, npair, LANE, LANE), lambda i: (i, 0, 0, 0))),
        out_shape=(jax.ShapeDtypeStruct((S, SSD_INNER), F32), jax.ShapeDtypeStruct((nc, npair, LANE, LANE), F32)),
        scratch_shapes=[pltpu.VMEM((npair, LANE, LANE), F32)], compiler_params=_cp(1), name=name,
    )(proj, proj, proj, proj, cw, cb, dtb, alog, dsk, ng)


def _ssd_bwd(proj, dycat, hin, cw, cb, dtb, alog, dsk, ng, *, name):
    S = proj.shape[0]
    q = SSD_CHUNK
    nc = S // q
    npair = SSD_HEADS // 2
    ppg = npair // SSD_GROUPS

    def body(xbc_ref, halo_ref, z_ref, misc_ref, dy_ref, hin_ref, cw_ref, cb_ref, dtb_ref, alog_ref, dsk_ref, ng_ref,
             dpre_ref, dz_ref, dmisc_ref, dcw_ref, dcb_ref, ddtb_ref, dalog_ref, ddsk_ref, dng_ref,
             dh_ref, carry_ref):
        i = pl.program_id(0)
        c = nc - 1 - i

        @pl.when(i == 0)
        def _():
            dh_ref[...] = jnp.zeros_like(dh_ref)
            carry_ref[...] = jnp.zeros_like(carry_ref)
            for r in (dcw_ref, dcb_ref, ddtb_ref, dalog_ref, ddsk_ref, dng_ref):
                r[...] = jnp.zeros_like(r)

        pre = xbc_ref[...]
        halo = jnp.where(c > 0, halo_ref[...], 0.0)
        conv, xbc, raw, dt, a, acum, acum_t, tri = _ssd_core(pre, halo, misc_ref[...], cw_ref, cb_ref,
                                                             dtb_ref[...], alog_ref[...])
        lane = lax.broadcasted_iota(jnp.int32, (q, LANE), 1)
        lane1 = lane[:1]
        rowi = lax.broadcasted_iota(jnp.int32, (q, LANE), 0)
        lastrow = rowi == q - 1
        lo = lane < LANE // 2
        lo1 = lo[:1]
        sub_lo = lax.broadcasted_iota(jnp.int32, (LANE, LANE), 0) < LANE // 2
        dsk = dsk_ref[...]
        alast = acum[q - 1:q, :]

        def halves(t):
            return _rowsum(jnp.where(lo, t, 0.0)), _rowsum(jnp.where(lo, 0.0, t))

        def put(ha, va, vb):
            ln = lane if va.shape[0] == q else lane1
            return jnp.where(ln == ha, va, 0.0) + jnp.where(ln == ha + 1, vb, 0.0)

        mats, pairs = [], []
        for g in range(SSD_GROUPS):
            b_mat = xbc[:, SSD_INNER + SSD_STATE * g:SSD_INNER + SSD_STATE * (g + 1)]
            c_mat = xbc[:, SSD_INNER + SSD_STATE * (SSD_GROUPS + g):SSD_INNER + SSD_STATE * (SSD_GROUPS + g + 1)]
            g_mat = _dot_nt(c_mat, b_mat)
            mats.append((b_mat, c_mat, g_mat))
            for jj in range(ppg):
                j = g * ppg + jj
                pairs.append(_ssd_pair_fwd(xbc, dt, acum, acum_t, tri, dsk, g_mat, b_mat, c_mat, hin_ref[0, j],
                                           j, lo, lo1, sub_lo))
        z = z_ref[...]
        sz = _silu(z)
        y = jnp.concatenate([p["y"] for p in pairs], axis=1)
        rstds, yns = _gnorm(y * sz)
        dout = dy_ref[...]
        dng_ref[...] += _colsum(dout * jnp.concatenate(yns, axis=1))
        dyn = dout * ng_ref[...]
        half = SSD_INNER // SSD_GROUPS
        dygs = []
        for g in range(SSD_GROUPS):
            dyn_g = dyn[:, half * g:half * (g + 1)]
            dygs.append(rstds[g] * (dyn_g - yns[g] * jnp.mean(dyn_g * yns[g], axis=-1, keepdims=True)))
        dyg = jnp.concatenate(dygs, axis=1)
        dyv = dyg * sz
        dz_ref[...] = (dyg * y * _dsilu(z)).astype(_ACT)

        da_acc = jnp.zeros((q, LANE), F32)
        ddt = jnp.zeros((q, LANE), F32)
        dds = jnp.zeros((1, LANE), F32)
        dxs, dbs, dcs = [], [], []
        for g in range(SSD_GROUPS):
            b_mat, c_mat, g_mat = mats[g]
            dg_mat = jnp.zeros((q, q), F32)
            db = jnp.zeros((q, SSD_STATE), F32)
            dc = jnp.zeros((q, SSD_STATE), F32)
            for jj in range(ppg):
                j = g * ppg + jj
                ha = 2 * j
                p = pairs[j]
                hj = hin_ref[0, j]
                dyp = dyv[:, LANE * j:LANE * (j + 1)]
                dsum = _colsum(dyp * p["x"])
                dds = dds + put(ha, _rowsum(jnp.where(lo1, dsum, 0.0)), _rowsum(jnp.where(lo1, 0.0, dsum)))
                dx = dyp * p["dp"]
                dw = dyp * p["ea"]
                dc = dc + _dot(dw, hj)
                dh_yo = _dot_tn(dw, c_mat)
                ra, rb = halves(dyp * p["yo"])
                da_acc = da_acc + put(ha, ra, rb)
                dxd = jnp.zeros((q, LANE), F32)
                for idx in range(2):
                    dyh = jnp.where(lo, dyp, 0.0) if idx == 0 else jnp.where(lo, 0.0, dyp)
                    dm = _dot_nt(dyh, p["xd"])
                    dxd = dxd + _dot_tn(p["ms"][idx], dyh)
                    dg_mat = dg_mat + dm * p["ls"][idx]
                    t = dm * p["ms"][idx]
                    da_h = _rowsum(t) - _rowsum(t.T)
                    da_acc = da_acc + jnp.where(lane == ha + idx, da_h, 0.0)
                dhn = dh_ref[j]
                s = _rowsum(dhn * hj)
                sa = jnp.sum(jnp.where(sub_lo[:, :1], s, 0.0), keepdims=True)
                sb = jnp.sum(jnp.where(sub_lo[:, :1], 0.0, s), keepdims=True)
                cda, cdb = jnp.exp(alast[:, ha:ha + 1]), jnp.exp(alast[:, ha + 1:ha + 2])
                db = db + _dot(p["xd"] * p["e"], dhn)
                r = _dot_nt(b_mat, dhn)
                dxd = dxd + r * p["e"]
                qa, qb = halves(r * p["xd"] * p["e"])
                da_acc = da_acc - put(ha, qa, qb)
                tot_a = sa * cda + jnp.sum(qa, keepdims=True)
                tot_b = sb * cdb + jnp.sum(qb, keepdims=True)
                da_acc = da_acc + jnp.where(lastrow, put(ha, tot_a, tot_b), 0.0)
                dh_ref[j] = p["cd"] * dhn + dh_yo
                dx = dx + dxd * p["dtp"]
                ua, ub = halves(dxd * p["x"])
                ddt = ddt + put(ha, ua, ub)
                dxs.append(dx)
            dc = dc + _dot(dg_mat, b_mat)
            db = db + _dot_tn(dg_mat, c_mat)
            dbs.append(db)
            dcs.append(dc)
        r2 = lax.broadcasted_iota(jnp.int32, (q, q), 0)
        c2 = lax.broadcasted_iota(jnp.int32, (q, q), 1)
        dda = jnp.dot((c2 >= r2).astype(F32), da_acc, precision=_HI, preferred_element_type=F32)
        ddt = ddt + dda * a
        dalog_ref[...] += _colsum(dda * dt) * a
        ddsk_ref[...] += dds
        draw = jnp.where(lane < SSD_HEADS, ddt * _sigmoid(raw), 0.0)
        ddtb_ref[...] += _colsum(draw)
        dmisc_ref[...] = draw
        dconv = jnp.concatenate(dxs + dbs + dcs, axis=1) * _dsilu(conv)
        dcb_ref[...] += _colsum(dconv)
        nxt = carry_ref[...]
        dpre = jnp.zeros_like(dconv)
        for k in range(SSD_CONV):
            dcw_ref[k:k + 1, :] += _colsum(dconv * _shift_down(pre, halo, SSD_CONV - 1 - k))
            dpre = dpre + _shift_up(dconv, nxt, SSD_CONV - 1 - k) * cw_ref[k:k + 1, :]
        dpre_ref[...] = dpre.astype(_ACT)
        carry_ref[...] = dconv[:SUB]

    rev = lambda i: (nc - 1 - i, 0)
    vshape = lambda w, r=1: jax.ShapeDtypeStruct((r, w), F32)
    return pl.pallas_call(
        body, grid=(nc,),
        in_specs=_ssd_specs(nc, True) + [pl.BlockSpec((q, SSD_INNER), rev),
                                         pl.BlockSpec((1, npair, LANE, LANE), lambda i: (nc - 1 - i, 0, 0, 0))]
        + _ssd_param_specs(),
        out_specs=(pl.BlockSpec((q, SSD_XBC), rev), pl.BlockSpec((q, SSD_INNER), rev), pl.BlockSpec((q, LANE), rev),
                   _vec(SSD_XBC, SSD_CONV), _vec(SSD_XBC), _vec(LANE), _vec(LANE), _vec(LANE), _vec(SSD_INNER)),
        out_shape=(jax.ShapeDtypeStruct((S, SSD_XBC), _ACT), jax.ShapeDtypeStruct((S, SSD_INNER), _ACT),
                   jax.ShapeDtypeStruct((S, LANE), F32),
                   vshape(SSD_XBC, SSD_CONV), vshape(SSD_XBC), vshape(LANE), vshape(LANE), vshape(LANE),
                   vshape(SSD_INNER)),
        scratch_shapes=[pltpu.VMEM((npair, LANE, LANE), F32), pltpu.VMEM((SUB, SSD_XBC), F32)],
        compiler_params=_cp(1), name=name,
    )(proj, proj, proj, proj, dycat, hin, cw, cb, dtb, alog, dsk, ng)


def _rope(x, cosf, sina, sinb):
    return x * cosf + pltpu.roll(x, LANE - MLA_ROPE // 2, 1) * sina + pltpu.roll(x, MLA_ROPE // 2, 1) * sinb


def _rope_t(dy, cosf, sina, sinb):
    return dy * cosf + pltpu.roll(dy * sina, MLA_ROPE // 2, 1) + pltpu.roll(dy * sinb, LANE - MLA_ROPE // 2, 1)


def _rope_lanes(shape):
    lane = lax.broadcasted_iota(jnp.int32, shape, 1)
    return (lane >= ROPE_LANE) & (lane < ROPE_LANE + MLA_ROPE)


def _mla_prep_fwd(proj, cosf, sina, sinb, gq, gkv, wuq, wukv, *, name):
    S = proj.shape[0]
    ts = _tile(S, TS_ROW, SUB)
    hw = MLA_HEADS * LANE

    def body(cq_ref, ckv_ref, misc_ref, cos_ref, sa_ref, sb_ref, gq_ref, gkv_ref, wuq_ref, wukv_ref,
             q_ref, k_ref, v_ref):
        cosv, sav, sbv = cos_ref[...], sa_ref[...], sb_ref[...]
        cq = cq_ref[...]
        qn = cq * lax.rsqrt(jnp.mean(cq * cq, axis=-1, keepdims=True) + EPS) * gq_ref[...]
        qh = _dot(qn, wuq_ref[...])
        ckv = ckv_ref[...]
        kvn = ckv * lax.rsqrt(jnp.mean(ckv * ckv, axis=-1, keepdims=True) + EPS) * gkv_ref[...]
        kv = _dot(kvn, wukv_ref[...])
        kr = _rope(jnp.where(_rope_lanes((ts, LANE)), misc_ref[...], 0.0), cosv, sav, sbv)
        for h in range(MLA_HEADS):
            sl = slice(LANE * h, LANE * (h + 1))
            q_ref[:, sl] = _rope(qh[:, sl], cosv, sav, sbv).astype(_ACT)
            k_ref[:, sl] = (kv[:, sl] + kr).astype(_ACT)
        v_ref[...] = kv[:, hw:].astype(_ACT)

    oshape = jax.ShapeDtypeStruct((S, hw), _ACT)
    return pl.pallas_call(
        body, grid=(S // ts,),
        in_specs=[_row(ts, MLA_QR, C_CQ // MLA_QR), _row(ts, MLA_KVR, C_CKV // MLA_KVR), _row(ts, LANE, C_MISC // LANE),
                  _row(ts, LANE), _row(ts, LANE), _row(ts, LANE), _vec(MLA_QR), _vec(MLA_KVR),
                  _vec(hw, MLA_QR), _vec(2 * hw, MLA_KVR)],
        out_specs=(_row(ts, hw),) * 3, out_shape=(oshape,) * 3, compiler_params=_cp(1), name=name,
    )(proj, proj, proj, cosf, sina, sinb, gq, gkv, wuq, wukv)


def _mla_prep_bwd(proj, dq, dk, dv, dmisc_ssd, cosf, sina, sinb, gq, gkv, wuq, wukv, *, name):
    S = proj.shape[0]
    ts = _tile(S, TS_ROW, SUB)
    hw = MLA_HEADS * LANE

    def body(cq_ref, ckv_ref, dq_ref, dk_ref, dv_ref, dms_ref, cos_ref, sa_ref, sb_ref, gq_ref, gkv_ref,
             wuq_ref, wukv_ref, dcq_ref, dckv_ref, dmisc_ref, dqh_ref, dkv_ref, qn_ref, kvn_ref, dgq_ref, dgkv_ref):
        i = pl.program_id(0)
        cosv, sav, sbv = cos_ref[...], sa_ref[...], sb_ref[...]

        @pl.when(i == 0)
        def _():
            dgq_ref[...] = jnp.zeros_like(dgq_ref)
            dgkv_ref[...] = jnp.zeros_like(dgkv_ref)

        dqh = jnp.concatenate([_rope_t(dq_ref[:, LANE * h:LANE * (h + 1)], cosv, sav, sbv)
                               for h in range(MLA_HEADS)], axis=1)
        dqh_ref[...] = dqh.astype(_ACT)
        dkv = jnp.concatenate([dk_ref[...], dv_ref[...]], axis=1)
        dkv_ref[...] = dkv.astype(_ACT)

        def norm_bwd(x, g, dn, dg_ref, n_ref):
            rstd = lax.rsqrt(jnp.mean(x * x, axis=-1, keepdims=True) + EPS)
            xhat = x * rstd
            n_ref[...] = (xhat * g).astype(_ACT)
            dg_ref[...] += _colsum(dn * xhat)
            dxh = dn * g
            return rstd * (dxh - xhat * jnp.mean(dxh * xhat, axis=-1, keepdims=True))

        dcq_ref[...] = norm_bwd(cq_ref[...], gq_ref[...], _dot_nt(dqh, wuq_ref[...]), dgq_ref, qn_ref).astype(_ACT)
        dckv_ref[...] = norm_bwd(ckv_ref[...], gkv_ref[...], _dot_nt(dkv, wukv_ref[...]), dgkv_ref, kvn_ref).astype(_ACT)
        dks = dk_ref[:, 0:LANE]
        for h in range(1, MLA_HEADS):
            dks = dks + dk_ref[:, LANE * h:LANE * (h + 1)]
        rl = _rope_lanes((ts, LANE))
        dkr = _rope_t(jnp.where(rl, dks, 0.0), cosv, sav, sbv)
        dmisc_ref[...] = (dms_ref[...] + jnp.where(rl, dkr, 0.0)).astype(_ACT)

    act = lambda w: jax.ShapeDtypeStruct((S, w), _ACT)
    return pl.pallas_call(
        body, grid=(S // ts,),
        in_specs=[_row(ts, MLA_QR, C_CQ // MLA_QR), _row(ts, MLA_KVR, C_CKV // MLA_KVR),
                  _row(ts, hw), _row(ts, hw), _row(ts, hw), _row(ts, LANE),
                  _row(ts, LANE), _row(ts, LANE), _row(ts, LANE), _vec(MLA_QR), _vec(MLA_KVR),
                  _vec(hw, MLA_QR), _vec(2 * hw, MLA_KVR)],
        out_specs=(_row(ts, MLA_QR), _row(ts, MLA_KVR), _row(ts, LANE), _row(ts, hw), _row(ts, 2 * hw),
                   _row(ts, MLA_QR), _row(ts, MLA_KVR), _vec(MLA_QR), _vec(MLA_KVR)),
        out_shape=(act(MLA_QR), act(MLA_KVR), act(LANE), act(hw), act(2 * hw), act(MLA_QR), act(MLA_KVR),
                   jax.ShapeDtypeStruct((1, MLA_QR), F32), jax.ShapeDtypeStruct((1, MLA_KVR), F32)),
        compiler_params=_cp(1), name=name,
    )(proj, proj, dq, dk, dv, dmisc_ssd, cosf, sina, sinb, gq, gkv, wuq, wukv)


_MLA_SCALE = 1.0 / math.sqrt(MLA_NOPE + MLA_ROPE)


def _causal_scores(q, k, i, j, tq):
    s = _dot_nt(q, k) * _MLA_SCALE
    rows = i * tq + lax.broadcasted_iota(jnp.int32, (tq, tq), 0)
    cols = j * tq + lax.broadcasted_iota(jnp.int32, (tq, tq), 1)
    return jnp.where(cols <= rows, s, -jnp.inf)


def _attn_fwd(q, k, v, *, name):
    S = q.shape[0]
    tq = _tile(S, TQ_ATT)
    nq = S // tq

    def body(q_ref, k_ref, v_ref, o_ref, lse_ref, m_ref, l_ref, acc_ref):
        i, j = pl.program_id(1), pl.program_id(2)

        @pl.when(j == 0)
        def _():
            m_ref[...] = jnp.full_like(m_ref, -jnp.inf)
            l_ref[...] = jnp.zeros_like(l_ref)
            acc_ref[...] = jnp.zeros_like(acc_ref)

        @pl.when(j <= i)
        def _():
            s = _causal_scores(q_ref[...], k_ref[...], i, j, tq)
            m_prev = m_ref[...]
            m_new = jnp.maximum(m_prev, jnp.max(s, axis=1, keepdims=True))
            p = jnp.exp(s - m_new)
            alpha = jnp.exp(m_prev - m_new)
            l_ref[...] = alpha * l_ref[...] + _rowsum(p)
            acc_ref[...] = alpha * acc_ref[...] + _dot(p, v_ref[...])
            m_ref[...] = m_new

        @pl.when(j == nq - 1)
        def _():
            o_ref[...] = acc_ref[...] / l_ref[...]
            lse_ref[...] = jnp.broadcast_to(m_ref[...] + jnp.log(l_ref[...]), (tq, LANE))

    qspec = pl.BlockSpec((tq, LANE), lambda h, i, j: (i, h))
    kspec = pl.BlockSpec((tq, LANE), lambda h, i, j: (jnp.minimum(j, i), h))
    oshape = jax.ShapeDtypeStruct((S, MLA_HEADS * LANE), F32)
    return pl.pallas_call(
        body, grid=(MLA_HEADS, nq, nq), in_specs=[qspec, kspec, kspec], out_specs=(qspec, qspec),
        out_shape=(oshape, oshape),
        scratch_shapes=[pltpu.VMEM((tq, 1), F32), pltpu.VMEM((tq, 1), F32), pltpu.VMEM((tq, LANE), F32)],
        compiler_params=_cp(3), name=name)(q, k, v)


def _attn_bwd_dq(q, k, v, o, lse, dycat, *, name):
    S = q.shape[0]
    tq = _tile(S, TQ_ATT)
    nq = S // tq

    def body(q_ref, k_ref, v_ref, o_ref, lse_ref, do_ref, dq_ref, acc_ref):
        i, j = pl.program_id(1), pl.program_id(2)

        @pl.when(j == 0)
        def _():
            acc_ref[...] = jnp.zeros_like(acc_ref)

        @pl.when(j <= i)
        def _():
            kv = k_ref[...]
            p = jnp.exp(_causal_scores(q_ref[...], kv, i, j, tq) - lse_ref[:, 0:1])
            dov = do_ref[...]
            delta = _rowsum(dov * o_ref[...])
            ds = p * (_dot_nt(dov, v_ref[...]) - delta) * _MLA_SCALE
            acc_ref[...] += _dot(ds, kv)

        @pl.when(j == nq - 1)
        def _():
            dq_ref[...] = acc_ref[...]

    qspec = pl.BlockSpec((tq, LANE), lambda h, i, j: (i, h))
    kspec = pl.BlockSpec((tq, LANE), lambda h, i, j: (jnp.minimum(j, i), h))
    dospec = pl.BlockSpec((tq, LANE), lambda h, i, j: (i, SSD_INNER // LANE + h))
    return pl.pallas_call(
        body, grid=(MLA_HEADS, nq, nq), in_specs=[qspec, kspec, kspec, qspec, qspec, dospec], out_specs=qspec,
        out_shape=jax.ShapeDtypeStruct((S, MLA_HEADS * LANE), F32),
        scratch_shapes=[pltpu.VMEM((tq, LANE), F32)], compiler_params=_cp(3), name=name)(q, k, v, o, lse, dycat)


def _attn_bwd_dkv(q, k, v, o, lse, dycat, *, name):
    S = q.shape[0]
    tq = _tile(S, TQ_ATT)
    nq = S // tq

    def body(q_ref, k_ref, v_ref, o_ref, lse_ref, do_ref, dk_ref, dv_ref, dk_acc, dv_acc):
        j, i = pl.program_id(1), pl.program_id(2)

        @pl.when(i == 0)
        def _():
            dk_acc[...] = jnp.zeros_like(dk_acc)
            dv_acc[...] = jnp.zeros_like(dv_acc)

        @pl.when(i >= j)
        def _():
            qv = q_ref[...]
            p = jnp.exp(_causal_scores(qv, k_ref[...], i, j, tq) - lse_ref[:, 0:1])
            dov = do_ref[...]
            delta = _rowsum(dov * o_ref[...])
            dv_acc[...] += _dot_tn(p, dov)
            ds = p * (_dot_nt(dov, v_ref[...]) - delta) * _MLA_SCALE
            dk_acc[...] += _dot_tn(ds, qv)

        @pl.when(i == nq - 1)
        def _():
            dk_ref[...] = dk_acc[...]
            dv_ref[...] = dv_acc[...]

    qspec = pl.BlockSpec((tq, LANE), lambda h, j, i: (jnp.maximum(i, j), h))
    kspec = pl.BlockSpec((tq, LANE), lambda h, j, i: (j, h))
    dospec = pl.BlockSpec((tq, LANE), lambda h, j, i: (jnp.maximum(i, j), SSD_INNER // LANE + h))
    oshape = jax.ShapeDtypeStruct((S, MLA_HEADS * LANE), F32)
    return pl.pallas_call(
        body, grid=(MLA_HEADS, nq, nq), in_specs=[qspec, kspec, kspec, qspec, qspec, dospec],
        out_specs=(kspec, kspec), out_shape=(oshape, oshape),
        scratch_shapes=[pltpu.VMEM((tq, LANE), F32), pltpu.VMEM((tq, LANE), F32)],
        compiler_params=_cp(3), name=name)(q, k, v, o, lse, dycat)


_SWA_SCALE = 1.0 / math.sqrt(SWA_HD)
_SWA_KW = SWA_KV * LANE


def _swa_specs(S, ts, rev):
    n = S // ts
    t = (lambda i: n - 1 - i) if rev else (lambda i: i)
    hb = lambda i: jnp.maximum(t(i) * (ts // WINDOW) - 1, 0)
    return [
        pl.BlockSpec((ts, SWA_HEADS * LANE), lambda i: (t(i), C_SQ // (SWA_HEADS * LANE))),
        pl.BlockSpec((ts, _SWA_KW), lambda i: (t(i), C_SK // _SWA_KW)),
        pl.BlockSpec((WINDOW, _SWA_KW), lambda i: (hb(i), C_SK // _SWA_KW)),
        pl.BlockSpec((ts, _SWA_KW), lambda i: (t(i), C_SV // _SWA_KW)),
        pl.BlockSpec((WINDOW, _SWA_KW), lambda i: (hb(i), C_SV // _SWA_KW)),
    ]


def _swa_scores(qh, kk, t, b, ts):
    s = _dot_nt(qh, kk) * _SWA_SCALE
    row = lax.broadcasted_iota(jnp.int32, (WINDOW, 2 * WINDOW), 0)
    col = lax.broadcasted_iota(jnp.int32, (WINDOW, 2 * WINDOW), 1)
    rel = WINDOW + row - col
    kpos = t * ts + (b - 1) * WINDOW + col
    return jnp.where((rel >= 0) & (rel < WINDOW) & (kpos >= 0), s, -jnp.inf)


def _swa_fwd(proj, sinks, *, name):
    S = proj.shape[0]
    ts = _tile(S, TS_SWA)
    nb = ts // WINDOW

    def body(q_ref, k_ref, kh_ref, v_ref, vh_ref, sink_ref, o_ref, lse_ref):
        t = pl.program_id(0)
        kext = jnp.concatenate([kh_ref[...], k_ref[...]], axis=0)
        vext = jnp.concatenate([vh_ref[...], v_ref[...]], axis=0)
        for b in range(nb):
            rows = slice(WINDOW * b, WINDOW * (b + 1))
            for h in range(SWA_HEADS):
                kvl = slice(LANE * (h // (SWA_HEADS // SWA_KV)), LANE * (h // (SWA_HEADS // SWA_KV) + 1))
                hl = slice(LANE * h, LANE * (h + 1))
                kk = kext[WINDOW * b:WINDOW * (b + 2), kvl]
                vv = vext[WINDOW * b:WINDOW * (b + 2), kvl]
                s = _swa_scores(q_ref[rows, hl], kk, t, b, ts)
                sk = sink_ref[:, h:h + 1]
                m = jnp.maximum(jnp.max(s, axis=1, keepdims=True), sk)
                p = jnp.exp(s - m)
                den = _rowsum(p) + jnp.exp(sk - m)
                o_ref[rows, hl] = _dot(p, vv) / den
                lse_ref[rows, hl] = jnp.broadcast_to(m + jnp.log(den), (WINDOW, LANE))

    oshape = jax.ShapeDtypeStruct((S, SWA_HEADS * LANE), F32)
    ospec = pl.BlockSpec((ts, SWA_HEADS * LANE), lambda i: (i, 0))
    return pl.pallas_call(
        body, grid=(S // ts,), in_specs=_swa_specs(S, ts, False) + [_vec(LANE)], out_specs=(ospec, ospec),
        out_shape=(oshape, oshape), compiler_params=_cp(1), name=name)(proj, proj, proj, proj, proj, sinks)


def _swa_bwd(proj, o, lse, dycat, sinks, *, name):
    S = proj.shape[0]
    ts = _tile(S, TS_SWA)
    nb = ts // WINDOW
    n = S // ts
    grp = SWA_HEADS // SWA_KV

    def body(q_ref, k_ref, kh_ref, v_ref, vh_ref, o_ref, lse_ref, do_ref, sink_ref,
             dq_ref, dk_ref, dv_ref, dsink_ref, dk_carry, dv_carry):
        i = pl.program_id(0)
        t = n - 1 - i

        @pl.when(i == 0)
        def _():
            dk_carry[...] = jnp.zeros_like(dk_carry)
            dv_carry[...] = jnp.zeros_like(dv_carry)
            dsink_ref[...] = jnp.zeros_like(dsink_ref)

        kext = jnp.concatenate([kh_ref[...], k_ref[...]], axis=0)
        vext = jnp.concatenate([vh_ref[...], v_ref[...]], axis=0)
        lane1 = lax.broadcasted_iota(jnp.int32, (1, LANE), 1)
        dkb = [[jnp.zeros((WINDOW, LANE), F32) for _ in range(SWA_KV)] for _ in range(nb + 1)]
        dvb = [[jnp.zeros((WINDOW, LANE), F32) for _ in range(SWA_KV)] for _ in range(nb + 1)]
        dsink = jnp.zeros((1, LANE), F32)
        for b in range(nb):
            rows = slice(WINDOW * b, WINDOW * (b + 1))
            for h in range(SWA_HEADS):
                kvh = h // grp
                kvl = slice(LANE * kvh, LANE * (kvh + 1))
                hl = slice(LANE * h, LANE * (h + 1))
                kk = kext[WINDOW * b:WINDOW * (b + 2), kvl]
                vv = vext[WINDOW * b:WINDOW * (b + 2), kvl]
                qh = q_ref[rows, hl]
                lse_h = lse_ref[rows, LANE * h:LANE * h + 1]
                p = jnp.exp(_swa_scores(qh, kk, t, b, ts) - lse_h)
                doh = do_ref[rows, hl]
                delta = _rowsum(doh * o_ref[rows, hl])
                ds = p * (_dot_nt(doh, vv) - delta)
                sk = sink_ref[:, h:h + 1]
                dsink = dsink + jnp.where(lane1 == h, -jnp.sum(jnp.exp(sk - lse_h) * delta, keepdims=True), 0.0)
                dq_ref[rows, hl] = (_dot(ds, kk) * _SWA_SCALE).astype(_ACT)
                dkk = _dot_tn(ds, qh) * _SWA_SCALE
                dvv = _dot_tn(p, doh)
                dkb[b][kvh] = dkb[b][kvh] + dkk[:WINDOW]
                dkb[b + 1][kvh] = dkb[b + 1][kvh] + dkk[WINDOW:]
                dvb[b][kvh] = dvb[b][kvh] + dvv[:WINDOW]
                dvb[b + 1][kvh] = dvb[b + 1][kvh] + dvv[WINDOW:]
        dsink_ref[...] += dsink
        for dref, blocks, carry in ((dk_ref, dkb, dk_carry), (dv_ref, dvb, dv_carry)):
            old = carry[...]
            for b in range(1, nb + 1):
                blk = jnp.concatenate(blocks[b], axis=1)
                if b == nb:
                    blk = blk + old
                dref[WINDOW * (b - 1):WINDOW * b, :] = blk.astype(_ACT)
            carry[...] = jnp.concatenate(blocks[0], axis=1)

    hw = SWA_HEADS * LANE
    rev = lambda i: (n - 1 - i, 0)
    mix = lambda i: (n - 1 - i, (SSD_INNER + MLA_HEADS * LANE) // hw)
    return pl.pallas_call(
        body, grid=(n,),
        in_specs=_swa_specs(S, ts, True) + [pl.BlockSpec((ts, hw), rev), pl.BlockSpec((ts, hw), rev),
                                            pl.BlockSpec((ts, hw), mix), _vec(LANE)],
        out_specs=(pl.BlockSpec((ts, hw), rev), pl.BlockSpec((ts, _SWA_KW), rev), pl.BlockSpec((ts, _SWA_KW), rev),
                   _vec(LANE)),
        out_shape=(jax.ShapeDtypeStruct((S, hw), _ACT), jax.ShapeDtypeStruct((S, _SWA_KW), _ACT),
                   jax.ShapeDtypeStruct((S, _SWA_KW), _ACT), jax.ShapeDtypeStruct((1, LANE), F32)),
        scratch_shapes=[pltpu.VMEM((WINDOW, _SWA_KW), F32), pltpu.VMEM((WINDOW, _SWA_KW), F32)],
        compiler_params=_cp(1), name=name)(proj, proj, proj, proj, proj, o, lse, dycat, sinks)


def _exchange(arrays, *, scatter, name):
    n = len(arrays)

    def body(*refs):
        ins, outs = refs[:n], refs[n:2 * n]
        send_sems, recv_sems, loc_sems = refs[2 * n:]
        x, y, c = lax.axis_index("x"), lax.axis_index("y"), lax.axis_index("c")
        me = 4 * x + 2 * y + c

        def src(i, dest):
            return ins[i].at[dest] if scatter else ins[i]

        local = [pltpu.make_async_copy(src(i, me), outs[i].at[me], loc_sems.at[i]) for i in range(n)]
        for cp in local:
            cp.start()
        sends, recvs = [], []
        for k in range(1, NDEV):
            px = 1 - x if k & 4 else x
            py = 1 - y if k & 2 else y
            pc = 1 - c if k & 1 else c
            peer = 4 * px + 2 * py + pc
            for i in range(n):
                common = dict(send_sem=send_sems.at[i, k - 1], recv_sem=recv_sems.at[i, k - 1],
                              device_id=(px, py, pc), device_id_type=pl.DeviceIdType.MESH)
                sends.append(pltpu.make_async_remote_copy(src_ref=src(i, peer), dst_ref=outs[i].at[me], **common))
                recvs.append(pltpu.make_async_remote_copy(src_ref=src(i, peer), dst_ref=outs[i].at[peer], **common))
        for cp in sends:
            cp.start()
        for cp in recvs:
            cp.wait_recv()
        for cp in sends:
            cp.wait_send()
        for cp in local:
            cp.wait()

    hbm = pl.BlockSpec(memory_space=pl.ANY)
    out_shape = tuple(jax.ShapeDtypeStruct(a.shape if scatter else (NDEV,) + a.shape, a.dtype) for a in arrays)
    return pl.pallas_call(
        body, in_specs=[hbm] * n, out_specs=tuple([hbm] * n), out_shape=out_shape,
        scratch_shapes=[pltpu.SemaphoreType.DMA((n, NDEV - 1)), pltpu.SemaphoreType.DMA((n, NDEV - 1)),
                        pltpu.SemaphoreType.DMA((n,))],
        name=name)(*arrays)


def _adamw(w, m, v, parts, *, name):
    R, C = w.shape
    npart = parts.shape[0]
    cap = max(SUB, ((1 << 18) // C) // SUB * SUB)
    tr = _tile(R, cap, SUB)

    def body(w_ref, m_ref, v_ref, p_ref, g_ref, d_ref, mo_ref, vo_ref):
        g = p_ref[0]
        for k in range(1, npart):
            g = g + p_ref[k]
        mn = ADAM_B1 * m_ref[...] + (1.0 - ADAM_B1) * g
        vn = ADAM_B2 * v_ref[...] + (1.0 - ADAM_B2) * (g * g)
        m_hat = mn / (1.0 - ADAM_B1 ** ADAM_STEP)
        v_hat = vn / (1.0 - ADAM_B2 ** ADAM_STEP)
        g_ref[...] = g
        d_ref[...] = -ADAM_LR * (m_hat / (jnp.sqrt(v_hat) + ADAM_EPS) + ADAM_WD * w_ref[...])
        mo_ref[...] = mn
        vo_ref[...] = vn

    spec = pl.BlockSpec((tr, C), lambda i: (i, 0))
    oshape = jax.ShapeDtypeStruct((R, C), F32)
    return pl.pallas_call(
        body, grid=(R // tr,), in_specs=[spec] * 3 + [pl.BlockSpec((npart, tr, C), lambda i: (0, i, 0))],
        out_specs=(spec,) * 4, out_shape=(oshape,) * 4, compiler_params=_cp(1), name=name)(w, m, v, parts)


def _adamw_layer(l, w, m, v, parts, prev, *, name):
    L, R, C = w.shape
    npart = parts.shape[0]
    cap = max(SUB, ((1 << 18) // C) // SUB * SUB)
    tr = _tile(R, cap, SUB)
    nprev = 0 if prev is None else 4

    def body(*refs):
        w_ref, m_ref, v_ref, p_ref = refs[:4]
        g_ref, d_ref, mo_ref, vo_ref = refs[4 + nprev:]
        g = p_ref[0]
        for k in range(1, npart):
            g = g + p_ref[k]
        mn = ADAM_B1 * m_ref[...] + (1.0 - ADAM_B1) * g
        vn = ADAM_B2 * v_ref[...] + (1.0 - ADAM_B2) * (g * g)
        m_hat = mn / (1.0 - ADAM_B1 ** ADAM_STEP)
        v_hat = vn / (1.0 - ADAM_B2 ** ADAM_STEP)
        g_ref[...] = g
        d_ref[...] = -ADAM_LR * (m_hat / (jnp.sqrt(v_hat) + ADAM_EPS) + ADAM_WD * w_ref[...])
        mo_ref[...] = mn
        vo_ref[...] = vn

    spec = pl.BlockSpec((None, tr, C), lambda i: (l, i, 0))
    oshape = jax.ShapeDtypeStruct((L, R, C), F32)
    return pl.pallas_call(
        body, grid=(R // tr,),
        in_specs=[spec] * 3 + [pl.BlockSpec((npart, tr, C), lambda i: (0, i, 0))]
        + [pl.BlockSpec(memory_space=pl.ANY)] * nprev,
        out_specs=(spec,) * 4, out_shape=(oshape,) * 4,
        input_output_aliases={4 + k: k for k in range(nprev)},
        compiler_params=_cp(1), name=name)(w, m, v, parts, *(prev or ()))


_HBM = pl.BlockSpec(memory_space=pltpu.HBM)
_SEM = pl.BlockSpec(memory_space=pltpu.SEMAPHORE)
_EFFECT = pltpu.SideEffectType.DATAFLOW_SIDE_EFFECTING


def _peers():
    x, y, c = lax.axis_index("x"), lax.axis_index("y"), lax.axis_index("c")
    out = []
    for k in range(1, NDEV):
        px = 1 - x if k & 4 else x
        py = 1 - y if k & 2 else y
        pc = 1 - c if k & 1 else c
        out.append((k - 1, (px, py, pc), 4 * px + 2 * py + pc))
    return 4 * x + 2 * y + c, out


def _xchg_start(arrays, *, scatter, name):
    n = len(arrays)
    lands = [lax.empty(a.shape if scatter else (NDEV,) + a.shape, a.dtype) for a in arrays]

    def body(*refs):
        ins, lnd = refs[:n], refs[n:2 * n]
        send_sems, recv_sems = refs[2 * n], refs[2 * n + 1]
        token = refs[-1]
        me, peers = _peers()
        for k, dev, peer in peers:
            for i in range(n):
                pltpu.make_async_remote_copy(
                    src_ref=ins[i].at[peer] if scatter else ins[i], dst_ref=lnd[i].at[me],
                    send_sem=send_sems.at[i * (NDEV - 1) + k], recv_sem=recv_sems.at[i * (NDEV - 1) + k],
                    device_id=dev, device_id_type=pl.DeviceIdType.MESH).start()
        token[...] = jnp.zeros_like(token)

    sems = pltpu.SemaphoreType.DMA((n * (NDEV - 1),))
    res = pl.pallas_call(
        body, name=name,
        out_shape=(sems, sems) + tuple(pltpu.HBM(t.shape, t.dtype) for t in list(arrays) + lands)
        + (jax.ShapeDtypeStruct((SUB, LANE), F32),),
        in_specs=[_HBM] * (2 * n), out_specs=(_SEM, _SEM) + (_HBM,) * (2 * n) + (pl.BlockSpec(memory_space=pltpu.VMEM),),
        input_output_aliases={i: 2 + i for i in range(2 * n)},
        compiler_params=pltpu.CompilerParams(has_side_effects=_EFFECT),
    )(*[pltpu.with_memory_space_constraint(t, pltpu.HBM) for t in list(arrays) + lands])
    return dict(send=res[0], recv=res[1], thru=list(res[2:2 + 2 * n]), token=res[-1], scatter=scatter, n=n)


def _xchg_wait(handle, after, *, name):
    n, scatter = handle["n"], handle["scatter"]
    thru = handle["thru"]

    def body(*refs):
        ins, lnd = refs[:n], refs[n:2 * n]
        send_sems, recv_sems = refs[2 * n], refs[2 * n + 1]
        me, peers = _peers()
        for k, dev, peer in peers:
            for i in range(n):
                cp = pltpu.make_async_remote_copy(
                    src_ref=ins[i].at[peer] if scatter else ins[i], dst_ref=lnd[i].at[peer],
                    send_sem=send_sems.at[i * (NDEV - 1) + k], recv_sem=recv_sems.at[i * (NDEV - 1) + k],
                    device_id=dev, device_id_type=pl.DeviceIdType.MESH)
                cp.wait_send()
                cp.wait_recv()

    res = pl.pallas_call(
        body, name=name, out_shape=tuple(pltpu.HBM(t.shape, t.dtype) for t in thru),
        in_specs=[_HBM] * (2 * n) + [_SEM, _SEM, pl.BlockSpec(memory_space=pl.ANY)], out_specs=(_HBM,) * (2 * n),
        input_output_aliases={i: i for i in range(2 * n)},
        compiler_params=pltpu.CompilerParams(has_side_effects=_EFFECT),
    )(*thru, handle["send"], handle["recv"], after)
    return list(res[:n]), list(res[n:])


def _pad_heads(w, nh, hd, axis=-1):
    axis = axis % w.ndim
    shp = w.shape
    w = w.reshape(shp[:axis] + (nh, hd) + shp[axis + 1:])
    pads = [(0, 0)] * w.ndim
    pads[axis + 1] = (0, LANE - hd)
    return jnp.pad(w, pads).reshape(shp[:axis] + (nh * LANE,) + shp[axis + 1:])


def _unpad_heads(w, nh, hd, axis=-1):
    axis = axis % w.ndim
    shp = w.shape
    w = w.reshape(shp[:axis] + (nh, LANE) + shp[axis + 1:])
    w = lax.slice_in_dim(w, 0, hd, axis=axis + 1)
    return w.reshape(shp[:axis] + (nh * hd,) + shp[axis + 1:])


_O_DT = SSD_INNER + SSD_XBC
_O_CQ = _O_DT + SSD_HEADS
_O_CKV = _O_CQ + MLA_QR
_O_KR = _O_CKV + MLA_KVR
_O_SQ = _O_KR + MLA_ROPE
_O_SK = _O_SQ + SWA_HEADS * SWA_HD
_O_SV = _O_SK + SWA_KV * SWA_HD


def _w_in_to_padded(w, axis=-1):
    axis = axis % w.ndim
    cut = lambda a, b: lax.slice_in_dim(w, a, b, axis=axis)
    z, xbc, dt = cut(0, SSD_INNER), cut(SSD_INNER, _O_DT), cut(_O_DT, _O_CQ)
    cq, ckv, kr = cut(_O_CQ, _O_CKV), cut(_O_CKV, _O_KR), cut(_O_KR, _O_SQ)
    sq, sk, sv = cut(_O_SQ, _O_SK), cut(_O_SK, _O_SV), cut(_O_SV, D_IN)
    zeros = lambda n: jnp.zeros(w.shape[:axis] + (n,) + w.shape[axis + 1:], w.dtype)
    return jnp.concatenate([xbc, z, cq, ckv, dt, zeros(ROPE_LANE - SSD_HEADS), kr, zeros(LANE - ROPE_LANE - MLA_ROPE),
                            _pad_heads(sq, SWA_HEADS, SWA_HD, axis), _pad_heads(sk, SWA_KV, SWA_HD, axis),
                            _pad_heads(sv, SWA_KV, SWA_HD, axis)], axis=axis)


def _w_in_from_padded(g, axis=-1):
    axis = axis % g.ndim
    cut = lambda a, b: lax.slice_in_dim(g, a, b, axis=axis)
    xbc, z, cq, ckv = cut(C_XBC, C_Z), cut(C_Z, C_CQ), cut(C_CQ, C_CKV), cut(C_CKV, C_MISC)
    dt, kr = cut(C_MISC, C_MISC + SSD_HEADS), cut(C_MISC + ROPE_LANE, C_MISC + ROPE_LANE + MLA_ROPE)
    sq = _unpad_heads(cut(C_SQ, C_SK), SWA_HEADS, SWA_HD, axis)
    sk = _unpad_heads(cut(C_SK, C_SV), SWA_KV, SWA_HD, axis)
    sv = _unpad_heads(cut(C_SV, D_INP), SWA_KV, SWA_HD, axis)
    return jnp.concatenate([z, xbc, dt, cq, ckv, kr, sq, sk, sv], axis=axis)


def _w_out_to_padded(w):
    a = SSD_INNER
    b = a + MLA_HEADS * MLA_V
    return jnp.concatenate([w[..., :a, :], _pad_heads(w[..., a:b, :], MLA_HEADS, MLA_V, axis=-2),
                            _pad_heads(w[..., b:, :], SWA_HEADS, SWA_HD, axis=-2)], axis=-2)


def _w_out_from_padded(g):
    a = SSD_INNER
    b = a + MLA_HEADS * LANE
    return jnp.concatenate([g[..., :a, :], _unpad_heads(g[..., a:b, :], MLA_HEADS, MLA_V, axis=-2),
                            _unpad_heads(g[..., b:, :], SWA_HEADS, SWA_HD, axis=-2)], axis=-2)


def _w_ukv_to_padded(w):
    w4 = w.reshape(w.shape[:-1] + (MLA_HEADS, MLA_NOPE + MLA_V))
    flat = lambda t: t.reshape(w.shape[:-1] + (MLA_HEADS * t.shape[-1],))
    return jnp.concatenate([_pad_heads(flat(w4[..., :MLA_NOPE]), MLA_HEADS, MLA_NOPE),
                            _pad_heads(flat(w4[..., MLA_NOPE:]), MLA_HEADS, MLA_V)], axis=-1)


def _w_ukv_from_padded(g):
    hw = MLA_HEADS * LANE
    gk = _unpad_heads(g[..., :hw], MLA_HEADS, MLA_NOPE).reshape(g.shape[:-1] + (MLA_HEADS, MLA_NOPE))
    gv = _unpad_heads(g[..., hw:], MLA_HEADS, MLA_V).reshape(g.shape[:-1] + (MLA_HEADS, MLA_V))
    return jnp.concatenate([gk, gv], axis=-1).reshape(g.shape[:-1] + (MLA_HEADS * (MLA_NOPE + MLA_V),))


def _pad_lane(v):
    return jnp.pad(v, [(0, 0)] * (v.ndim - 1) + [(0, LANE - v.shape[-1])])


def _rope_tables(positions):
    inv_freq = ROPE_THETA ** (-jnp.arange(0, MLA_ROPE, 2, dtype=F32) / MLA_ROPE)
    ang = positions.astype(F32).reshape(-1, 1) * inv_freq
    cos, sin = jnp.cos(ang), jnp.sin(ang)
    S = ang.shape[0]
    one, zero = jnp.ones((S, ROPE_LANE), F32), jnp.zeros((S, ROPE_LANE), F32)
    tail1, tail0 = jnp.ones((S, LANE - ROPE_LANE - MLA_ROPE), F32), jnp.zeros((S, LANE - ROPE_LANE - MLA_ROPE), F32)
    z16 = jnp.zeros_like(sin)
    return (jnp.concatenate([one, cos, cos, tail1], axis=1), jnp.concatenate([zero, -sin, z16, tail0], axis=1),
            jnp.concatenate([zero, z16, sin, tail0], axis=1))


def _layer_fwd(l, x_in, f_prev, gate_prev, mod, P, tabs):
    sh1, sc1, g1, sh2, sc2, g2 = [mod[k:k + 1] for k in range(6)]
    tag = f"l{l}_"
    if f_prev is None:
        x0 = x_in
        h1 = _norm_fwd(x0, P["n1g"], sc1, sh1, name=tag + "norm1")
    else:
        x0, h1 = _norm_fwd(x_in, P["n1g"], sc1, sh1, f=f_prev, gate=gate_prev, name=tag + "norm1")
    proj = _mm(h1, P["w_in"], tb=True, name=tag + "proj")
    P.update(P.pop("late")(proj))
    y_ssd, hin = _ssd_fwd(proj, P["ssd_cw"], P["ssd_cb"], P["dtb"], P["alog"], P["dsk"],
                          P["ssd_ng"], name=tag + "ssd")
    q, k, v = _mla_prep_fwd(proj, *tabs, P["gq"], P["gkv"], P["w_uq"], P["w_ukv"], name=tag + "mla_prep")
    o_mla, lse_mla = _attn_fwd(q, k, v, name=tag + "mla_attn")
    o_swa, lse_swa = _swa_fwd(proj, P["sinks"], name=tag + "swa")
    ycat = jnp.concatenate([y_ssd.astype(_ACT), o_mla.astype(_ACT), o_swa.astype(_ACT)], axis=1)
    y = _mm(ycat, P["w_out"], name=tag + "out")
    x1, h2 = _norm_fwd(x0, P["n2g"], sc2, sh2, f=y, gate=g1, name=tag + "norm2")
    up = _mm(h2, P["w_up"], tb=True, name=tag + "up")
    act = _ffn_act_fwd(up, P["fcw"], P["fcb"], name=tag + "ffn_act")
    f = _mm(act, P["w_down"], name=tag + "down")
    saved = dict(x0=x0, h1=h1, proj=proj, hin=hin, q=q, k=k, v=v, o_mla=o_mla, lse_mla=lse_mla, o_swa=o_swa,
                 lse_swa=lse_swa, ycat=ycat, y=y, x1=x1, h2=h2, up=up, act=act, f=f, mod=mod)
    return x1, f, g2, saved


def _layer_bwd(l, dxo, sv, P, tabs, on_ffn_grads):
    mod = sv["mod"]
    sh1, sc1, g1, sh2, sc2, g2 = [mod[k:k + 1] for k in range(6)]
    tag = f"l{l}_b_"
    G = {}
    df, dg2 = _gate_bwd(dxo, sv["f"], g2, name=tag + "gate2")
    dact = _mm(df, P["w_down"], tb=True, name=tag + "dact")
    G["w_down"] = _mm(sv["act"], df, ta=True, name=tag + "dw_down")
    du = _ffn_act_bwd(sv["up"], dact, P["fcw"], P["fcb"], name=tag + "ffn_act")
    dup, G["fcw"], G["fcb"] = _ffn_conv_bwd(du, sv["up"], P["fcw"], name=tag + "ffn_conv")
    dh2 = _mm(dup, P["w_up"], name=tag + "dh2")
    G["w_up"] = _mm(dup, sv["h2"], ta=True, name=tag + "dw_up")
    token = on_ffn_grads(l, G)
    if token is not None:
        sc2 = sc2 + token
    dx1, G["n2g"], dsc2, dsh2 = _norm_bwd(dh2, sv["x1"], dxo, P["n2g"], sc2, name=tag + "norm2")
    dy, dg1 = _gate_bwd(dx1, sv["y"], g1, name=tag + "gate1")
    dycat = _mm(dy, P["w_out"], tb=True, name=tag + "dycat")
    G["w_out"] = _mm(sv["ycat"], dy, ta=True, name=tag + "dw_out")
    proj = sv["proj"]
    (dpre, dz, dmisc_ssd, G["ssd_cw"], G["ssd_cb"], G["dtb"], G["alog"], G["dsk"], G["ssd_ng"]) = _ssd_bwd(
        proj, dycat, sv["hin"], P["ssd_cw"], P["ssd_cb"], P["dtb"], P["alog"], P["dsk"],
        P["ssd_ng"], name=tag + "ssd")
    att = (sv["q"], sv["k"], sv["v"], sv["o_mla"], sv["lse_mla"], dycat)
    dq = _attn_bwd_dq(*att, name=tag + "mla_dq")
    dk, dv = _attn_bwd_dkv(*att, name=tag + "mla_dkv")
    dcq, dckv, dmisc, dqh, dkv, qn, kvn, G["gq"], G["gkv"] = _mla_prep_bwd(
        proj, dq, dk, dv, dmisc_ssd, *tabs, P["gq"], P["gkv"], P["w_uq"], P["w_ukv"], name=tag + "mla_prep")
    G["w_uq"] = _mm(qn, dqh, ta=True, name=tag + "dw_uq")
    G["w_ukv"] = _mm(kvn, dkv, ta=True, name=tag + "dw_ukv")
    dsq, dsk_, dsv_, G["sinks"] = _swa_bwd(proj, sv["o_swa"], sv["lse_swa"], dycat, P["sinks"], name=tag + "swa")
    dproj = jnp.concatenate([dpre, dz, dcq, dckv, dmisc, dsq, dsk_, dsv_], axis=1)
    dh1 = _mm(dproj, P["w_in"], name=tag + "dh1")
    G["w_in"] = _mm(dproj, sv["h1"], ta=True, name=tag + "dw_in")
    dx0, G["n1g"], dsc1, dsh1 = _norm_bwd(dh1, sv["x0"], dx1, P["n1g"], sc1, name=tag + "norm1")
    G["mod"] = jnp.concatenate([dsh1, dsc1, dg1, dsh2, dsc2, dg2], axis=0)
    return dx0, G


def _local_step(x, tgt, mods, get_params, tabs, final_g, on_grads, on_ffn_grads):
    saved, params = [], []
    xin, f, gate = x, None, None
    for l in range(DEPTH):
        params.append(get_params(l, x if f is None else f))
        xin, f, gate, sv = _layer_fwd(l, xin, f, gate, mods[l], params[l], tabs)
        saved.append(sv)
    loss, dx, dfinal = _final_loss(xin, f, gate, final_g, tgt, name="final_loss")
    for l in reversed(range(DEPTH)):
        dx, G = _layer_bwd(l, dx, saved[l], params[l], tabs, on_ffn_grads)
        token = on_grads(l, G)
        if token is not None and l > 0:
            saved[l - 1]["mod"] = saved[l - 1]["mod"] + token
    return loss[0, 0], dx, dfinal


_WEIGHTS = ['ada_w', 'ada_b', 'norm1_g', 'norm2_g', 'w_in', 'ssd_conv_w', 'ssd_conv_b', 'ssd_dt_bias', 'ssd_a_log',
            'ssd_d', 'ssd_norm_g', 'mla_q_norm_g', 'mla_w_uq', 'mla_kv_norm_g', 'mla_w_ukv', 'swa_sinks', 'w_out',
            'ffn_w_up', 'ffn_conv_w', 'ffn_conv_b', 'ffn_w_down', 'final_norm_g']
_INPUTS = ['x', 'c', 'positions'] + _WEIGHTS + ['loss_target'] + ['m_' + n for n in _WEIGHTS] + ['v_' + n for n in _WEIGHTS]
_SMALL = [('ada_b', 'mod'), ('norm1_g', 'n1g'), ('norm2_g', 'n2g'), ('ssd_conv_b', 'ssd_cb'), ('ssd_dt_bias', 'dtb'),
          ('ssd_a_log', 'alog'), ('ssd_d', 'dsk'), ('ssd_norm_g', 'ssd_ng'), ('mla_q_norm_g', 'gq'),
          ('mla_kv_norm_g', 'gkv'), ('swa_sinks', 'sinks'), ('ffn_conv_b', 'fcb')]
_SHARDED = [('w_in', 'w_in', 2), ('ssd_conv_w', 'ssd_cw', 2), ('mla_w_uq', 'w_uq', 2), ('mla_w_ukv', 'w_ukv', 2),
            ('w_out', 'w_out', 1), ('ffn_w_up', 'w_up', 2), ('ffn_conv_w', 'fcw', 2), ('ffn_w_down', 'w_down', 1)]
_SHARDED_NAMES = [n for n, _, _ in _SHARDED]
_TRANSPOSED = ('w_in', 'ffn_w_up')


def _pack_small(per_layer, final):
    parts = []
    for name, _ in _SMALL:
        v = per_layer[name]
        v = v.reshape(DEPTH, -1)
        pad = (-v.shape[1]) % LANE
        parts.append(jnp.pad(v, ((0, 0), (0, pad))).reshape(-1))
    parts.append(final.reshape(-1))
    return jnp.concatenate(parts).reshape(-1, LANE)


def _unpack_small(packed, shapes):
    flat = packed.reshape(-1)
    out, off = {}, 0
    for name, _ in _SMALL:
        n = math.prod(shapes[name][1:])
        npad = n + (-n) % LANE
        out[name] = flat[off:off + DEPTH * npad].reshape(DEPTH, npad)[:, :n].reshape(shapes[name])
        off += DEPTH * npad
    out['final_norm_g'] = flat[off:off + D]
    return out


def _shard_major(g, axis):
    shp = g.shape
    g = g.reshape(shp[:axis] + (NDEV, shp[axis] // NDEV) + shp[axis + 1:])
    return jnp.moveaxis(g, axis, 0)


def _unshard(g, axis):
    g = jnp.moveaxis(g, 0, axis)
    shp = g.shape
    return g.reshape(shp[:axis] + (shp[axis] * shp[axis + 1],) + shp[axis + 2:])


def kernel(x, c, positions, ada_w, ada_b, norm1_g, norm2_g, w_in, ssd_conv_w, ssd_conv_b, ssd_dt_bias, ssd_a_log, ssd_d, ssd_norm_g, mla_q_norm_g, mla_w_uq, mla_kv_norm_g, mla_w_ukv, swa_sinks, w_out, ffn_w_up, ffn_conv_w, ffn_conv_b, ffn_w_down, final_norm_g, loss_target, m_ada_w, m_ada_b, m_norm1_g, m_norm2_g, m_w_in, m_ssd_conv_w, m_ssd_conv_b, m_ssd_dt_bias, m_ssd_a_log, m_ssd_d, m_ssd_norm_g, m_mla_q_norm_g, m_mla_w_uq, m_mla_kv_norm_g, m_mla_w_ukv, m_swa_sinks, m_w_out, m_ffn_w_up, m_ffn_conv_w, m_ffn_conv_b, m_ffn_w_down, m_final_norm_g, v_ada_w, v_ada_b, v_norm1_g, v_norm2_g, v_w_in, v_ssd_conv_w, v_ssd_conv_b, v_ssd_dt_bias, v_ssd_a_log, v_ssd_d, v_ssd_norm_g, v_mla_q_norm_g, v_mla_w_uq, v_mla_kv_norm_g, v_mla_w_ukv, v_swa_sinks, v_w_out, v_ffn_w_up, v_ffn_conv_w, v_ffn_conv_b, v_ffn_w_down, v_final_norm_g):
    a = dict(zip(_INPUTS, (x, c, positions, ada_w, ada_b, norm1_g, norm2_g, w_in, ssd_conv_w, ssd_conv_b, ssd_dt_bias, ssd_a_log, ssd_d, ssd_norm_g, mla_q_norm_g, mla_w_uq, mla_kv_norm_g, mla_w_ukv, swa_sinks, w_out, ffn_w_up, ffn_conv_w, ffn_conv_b, ffn_w_down, final_norm_g, loss_target, m_ada_w, m_ada_b, m_norm1_g, m_norm2_g, m_w_in, m_ssd_conv_w, m_ssd_conv_b, m_ssd_dt_bias, m_ssd_a_log, m_ssd_d, m_ssd_norm_g, m_mla_q_norm_g, m_mla_w_uq, m_mla_kv_norm_g, m_mla_w_ukv, m_swa_sinks, m_w_out, m_ffn_w_up, m_ffn_conv_w, m_ffn_conv_b, m_ffn_w_down, m_final_norm_g, v_ada_w, v_ada_b, v_norm1_g, v_norm2_g, v_w_in, v_ssd_conv_w, v_ssd_conv_b, v_ssd_dt_bias, v_ssd_a_log, v_ssd_d, v_ssd_norm_g, v_mla_q_norm_g, v_mla_w_uq, v_mla_kv_norm_g, v_mla_w_ukv, v_swa_sinks, v_w_out, v_ffn_w_up, v_ffn_conv_w, v_ffn_conv_b, v_ffn_w_down, v_final_norm_g)))
    axes = ("x", "y", "c")
    me = 4 * lax.axis_index("x") + 2 * lax.axis_index("y") + lax.axis_index("c")
    ncol = ada_w.shape[-1]

    c_all = _exchange([c], scatter=False, name="gather_c")[0]
    c_act = _silu_call(c_all.reshape(NDEV, D), name="c_act")
    mod_part = jnp.stack([_mm(c_act, ada_w[l], name=f"mod{l}") for l in range(DEPTH)])
    mod_all = _exchange([mod_part], scatter=False, name="gather_mod")[0]
    mod_mine = lax.dynamic_index_in_dim(mod_all, me, axis=2, keepdims=False)
    mods = (jnp.moveaxis(mod_mine, 0, 1).reshape(DEPTH, 6 * D) + ada_b).reshape(DEPTH, 6, D)
    tabs = _rope_tables(positions)

    mxu_names = ('w_in', 'mla_w_uq', 'mla_w_ukv', 'w_out', 'ffn_w_up', 'ffn_w_down')
    kform = lambda n, t: jnp.swapaxes(t, -1, -2) if n in _TRANSPOSED else t
    shard_of = {n: (key, 1 if n in _TRANSPOSED else ax) for n, key, ax in _SHARDED}
    early_w, late_w = _SHARDED_NAMES[:4], _SHARDED_NAMES[4:]
    mods, raw = lax.optimization_barrier((mods, {n: a[n] for n in _SHARDED_NAMES}))
    own_of = lambda names, l: [kform(n, raw[n][l]).astype(_MXU) if n in mxu_names else raw[n][l] for n in names]
    gathers, prev = [], None
    for l in range(DEPTH):
        gathers.append({})
        for grp, names in (("early", early_w), ("late", late_w)):
            srcs = own_of(names, l)
            if prev is not None:
                srcs, _ = lax.optimization_barrier((srcs, prev))
            gathers[l][grp] = _xchg_start(srcs, scatter=False, name=f"gather_start_{grp}{l}")
            prev = gathers[l][grp]["token"]

    def place_own(landed, mine):
        return [lax.dynamic_update_index_in_dim(t, o, me, 0) for t, o in zip(landed, mine)]

    def gathered(l, grp, names, after):
        mine, landed = _xchg_wait(gathers[l][grp], after, name=f"gather_wait_{grp}{l}")
        return {n: _unshard(g, shard_of[n][1] - 1) for n, g in zip(names, place_own(landed, mine))}

    def get_params(l, after):
        full = gathered(l, "early", early_w, after)
        vec = lambda t: t[l].reshape(1, -1)

        def late(after2):
            rest = gathered(l, "late", late_w, after2)
            return dict(w_out=_w_out_to_padded(rest['w_out']), w_up=rest['ffn_w_up'], w_down=rest['ffn_w_down'],
                        fcw=rest['ffn_conv_w'])

        return dict(
            w_in=_w_in_to_padded(full['w_in'], axis=0), w_uq=_pad_heads(full['mla_w_uq'], MLA_HEADS, MLA_NOPE + MLA_ROPE),
            w_ukv=_w_ukv_to_padded(full['mla_w_ukv']), ssd_cw=full['ssd_conv_w'], late=late,
            ssd_cb=vec(ssd_conv_b), dtb=vec(_pad_lane(ssd_dt_bias)), alog=vec(_pad_lane(ssd_a_log)),
            dsk=vec(_pad_lane(ssd_d)), ssd_ng=vec(ssd_norm_g), gq=vec(mla_q_norm_g), gkv=vec(mla_kv_norm_g),
            sinks=vec(_pad_lane(swa_sinks)), fcb=vec(ffn_conv_b), n1g=vec(norm1_g), n2g=vec(norm2_g))

    unpad = dict(w_in=functools.partial(_w_in_from_padded, axis=0), w_out=_w_out_from_padded, w_ukv=_w_ukv_from_padded,
                 w_uq=lambda g: _unpad_heads(g, MLA_HEADS, MLA_NOPE + MLA_ROPE))
    ffn_w, mixer_w = _SHARDED_NAMES[5:], _SHARDED_NAMES[:5]
    grads = [None] * DEPTH
    scatters = [dict() for _ in range(DEPTH)]

    def send(l, grp, names, G):
        parts = [_shard_major(unpad.get(shard_of[n][0], lambda g: g)(G[shard_of[n][0]]), shard_of[n][1] - 1)
                 for n in names]
        scatters[l][grp] = _xchg_start(parts, scatter=True, name=f"scatter_start_{grp}{l}")
        return scatters[l][grp]["token"][0, 0]

    def on_ffn_grads(l, G):
        return send(l, "ffn", ffn_w, G)

    def on_grads(l, G):
        grads[l] = G
        return send(l, "mixer", mixer_w, G)

    mods = mods + sum(g[grp]["token"][0, 0] for g in gathers for grp in ("early", "late"))
    loss, dx, dfinal = _local_step(x[0], loss_target[0], mods, get_params, tabs, final_norm_g.reshape(1, D),
                                   on_grads, on_ffn_grads)
    loss = lax.psum(loss, axes)

    stack = lambda key: jnp.stack([grads[l][key] for l in range(DEPTH)])
    small_g = {name: stack(key).reshape(DEPTH, -1) for name, key in _SMALL}
    small_parts = _exchange([_pack_small(small_g, dfinal)], scatter=False, name="gather_small")[0]

    out_g, out_d, out_m, out_v = {}, {}, {}, {}
    chain = {name: None for name in _SHARDED_NAMES}
    for l in reversed(range(DEPTH)):
        for grp, names in (("ffn", ffn_w), ("mixer", mixer_w)):
            mine, landed = _xchg_wait(scatters[l][grp], dx, name=f"scatter_wait_{grp}{l}")
            parts = place_own(landed, [lax.dynamic_index_in_dim(t, me, 0, keepdims=False) for t in mine])
            for name, pv in zip(names, parts):
                chain[name] = _adamw_layer(l, kform(name, a[name]), kform(name, a['m_' + name]),
                                           kform(name, a['v_' + name]), pv, chain[name], name=f"adamw_{name}{l}")
    for name in _SHARDED_NAMES:
        out_g[name], out_d[name], out_m[name], out_v[name] = [kform(name, t) for t in chain[name]]

    def update(name, wv, mv, vv, pv):
        shp = wv.shape
        r = lambda t: t.reshape((-1, shp[-1]))
        res = _adamw(r(wv), r(mv), r(vv), pv.reshape((pv.shape[0], -1, shp[-1])), name="adamw_" + name)
        out_g[name], out_d[name], out_m[name], out_v[name] = [t.reshape(shp) for t in res]

    n_ada = DEPTH * 6 * D // LANE
    dmod_all = small_parts[:, :n_ada].reshape(NDEV, DEPTH, 6 * D)
    dmod_mine = lax.dynamic_slice_in_dim(dmod_all, me * ncol, ncol, axis=2)
    g_ada = jnp.stack([_mm(c_act, dmod_mine[:, l], ta=True, name=f"dw_ada{l}") for l in range(DEPTH)])
    update('ada_w', ada_w, m_ada_w, v_ada_w, g_ada[None])
    shapes = {n: a[n].shape for n, _ in _SMALL}
    pk = lambda pre: _pack_small({n: a[pre + n] for n, _ in _SMALL}, a[pre + 'final_norm_g'])
    res = _adamw(pk(''), pk('m_'), pk('v_'), small_parts, name="adamw_small")
    for dst, t in zip((out_g, out_d, out_m, out_v), res):
        dst.update(_unpack_small(t, shapes))

    outs = [loss, dx[None]]
    for dct in (out_g, out_d, out_m, out_v):
        outs += [dct[n] for n in _WEIGHTS]
    return tuple(outs)
```

```python
import functools
import math

import jax
import jax.numpy as jnp
from jax import lax
from jax.experimental import pallas as pl
from jax.experimental.pallas import tpu as pltpu

F32 = jnp.float32
_MXU = jnp.bfloat16
_ACT = jnp.bfloat16
_HI = lax.Precision.HIGHEST
EPS = 1e-6
NDEV = 8
DEPTH = 4
D = 1024
LANE = 128
SUB = 8
VMEM_LIMIT = 56 * 1024 * 1024

SSD_INNER, SSD_STATE, SSD_HEADS, SSD_GROUPS, SSD_CHUNK, SSD_CONV = 512, 128, 8, 2, 128, 4
SSD_XBC = SSD_INNER + 2 * SSD_GROUPS * SSD_STATE
MLA_HEADS, MLA_NOPE, MLA_ROPE, MLA_V, MLA_QR, MLA_KVR = 4, 64, 32, 64, 256, 128
SWA_HEADS, SWA_KV, SWA_HD, WINDOW = 4, 2, 64, 128
D_FF, FFN_CONV = 2816, 3
D_IN = 2472
ROPE_THETA = 10000.0
C_XBC, C_Z, C_CQ, C_CKV, C_MISC, C_SQ, C_SK, C_SV, D_INP = 0, 1024, 1536, 1792, 1920, 2048, 2560, 2816, 3072
ROPE_LANE = 64
D_MIXP = 1536

ADAM_LR, ADAM_B1, ADAM_B2, ADAM_EPS, ADAM_WD, ADAM_STEP = 0.001, 0.9, 0.999, 1e-08, 0.01, 10

TS_ROW = 512
TS_FFN = 256
TQ_ATT = 512
TS_SWA = 512


def _tile(n, cap, q=LANE):
    best = None
    for t in range(q, min(n, cap) + 1, q):
        if n % t == 0:
            best = t
    return n if best is None else best


def _cp(ngrid):
    return pltpu.CompilerParams(dimension_semantics=("arbitrary",) * ngrid, vmem_limit_bytes=VMEM_LIMIT)


def _dot(a, b):
    return jnp.dot(a.astype(_MXU), b.astype(_MXU), preferred_element_type=F32)


def _dot_nt(a, b):
    return lax.dot_general(a.astype(_MXU), b.astype(_MXU), (((1,), (1,)), ((), ())), preferred_element_type=F32)


def _dot_tn(a, b):
    return jnp.dot(a.T.astype(_MXU), b.astype(_MXU), preferred_element_type=F32)


def _sigmoid(x):
    return 1.0 / (1.0 + jnp.exp(-x))


def _silu(x):
    return x * _sigmoid(x)


def _dsilu(x):
    s = _sigmoid(x)
    return s * (1.0 + x * (1.0 - s))


def _softplus(x):
    u = jnp.exp(-jnp.abs(x))
    w = 1.0 + u
    log1p = jnp.where(w == 1.0, u, jnp.log(w) * u / jnp.where(w == 1.0, 1.0, w - 1.0))
    return jnp.maximum(x, 0.0) + log1p


def _colsum(x):
    return jnp.sum(x, axis=0, keepdims=True)


def _rowsum(x):
    return jnp.sum(x, axis=1, keepdims=True)


def _shift_down(t, halo, j):
    if j == 0:
        return t
    n = t.shape[0]
    rolled = pltpu.roll(t, j, 0)
    row = lax.broadcasted_iota(jnp.int32, (SUB, t.shape[1]), 0)
    first = jnp.where(row < j, pltpu.roll(halo, j, 0), rolled[:SUB])
    return jnp.concatenate([first, rolled[SUB:]], axis=0) if n > SUB else first


def _shift_up(t, halo, j):
    if j == 0:
        return t
    n = t.shape[0]
    rolled = pltpu.roll(t, n - j, 0)
    row = lax.broadcasted_iota(jnp.int32, (SUB, t.shape[1]), 0)
    last = jnp.where(row >= SUB - j, pltpu.roll(halo, SUB - j, 0), rolled[n - SUB:])
    return jnp.concatenate([rolled[:n - SUB], last], axis=0) if n > SUB else last


def _mm(a, b, *, ta=False, tb=False, out_dtype=F32, name):
    if ta:
        K, M = a.shape
    else:
        M, K = a.shape
    if tb:
        N, K2 = b.shape
    else:
        K2, N = b.shape
    assert K == K2, (a.shape, b.shape, ta, tb)
    tm, tn, tk = _tile(M, 1536), _tile(N, 1408), _tile(K, 1536)
    nk = K // tk
    dn = (((0 if ta else 1,), (1 if tb else 0,)), ((), ()))

    def body(a_ref, b_ref, o_ref, acc_ref):
        k = pl.program_id(2)
        part = lax.dot_general(a_ref[...].astype(_MXU), b_ref[...].astype(_MXU), dn, preferred_element_type=F32)

        @pl.when(k == 0)
        def _():
            acc_ref[...] = part

        @pl.when(k > 0)
        def _():
            acc_ref[...] += part

        @pl.when(k == nk - 1)
        def _():
            o_ref[...] = acc_ref[...].astype(out_dtype)

    a_spec = pl.BlockSpec((tk, tm), lambda i, j, k: (k, i)) if ta else pl.BlockSpec((tm, tk), lambda i, j, k: (i, k))
    b_spec = pl.BlockSpec((tn, tk), lambda i, j, k: (j, k)) if tb else pl.BlockSpec((tk, tn), lambda i, j, k: (k, j))
    return pl.pallas_call(
        body, grid=(M // tm, N // tn, nk), in_specs=[a_spec, b_spec],
        out_specs=pl.BlockSpec((tm, tn), lambda i, j, k: (i, j)),
        out_shape=jax.ShapeDtypeStruct((M, N), out_dtype),
        scratch_shapes=[pltpu.VMEM((tm, tn), F32)], compiler_params=_cp(3), name=name)(a, b)


def _row(ts, w, col=0):
    return pl.BlockSpec((ts, w), lambda i: (i, col))


def _vec(w, r=1):
    return pl.BlockSpec((r, w), lambda i: (0, 0))


def _silu_call(x, name):
    def body(x_ref, o_ref):
        o_ref[...] = _silu(x_ref[...])
    return pl.pallas_call(body, out_shape=jax.ShapeDtypeStruct(x.shape, F32), name=name)(x)


def _norm_fwd(x, g, sc, sh, *, f=None, gate=None, name):
    S, dm = x.shape
    ts = _tile(S, TS_ROW, SUB)
    res = f is not None

    def body(*refs):
        if res:
            x_ref, f_ref, gate_ref, g_ref, sc_ref, sh_ref, xo_ref, h_ref = refs
            xv = x_ref[...] + gate_ref[...] * f_ref[...]
            xo_ref[...] = xv
        else:
            x_ref, g_ref, sc_ref, sh_ref, h_ref = refs
            xv = x_ref[...]
        rstd = lax.rsqrt(jnp.mean(xv * xv, axis=-1, keepdims=True) + EPS)
        h_ref[...] = ((xv * rstd) * g_ref[...] * (1.0 + sc_ref[...]) + sh_ref[...]).astype(_ACT)

    ins = [x] + ([f, gate] if res else []) + [g, sc, sh]
    in_specs = [_row(ts, dm)] + ([_row(ts, dm), _vec(dm)] if res else []) + [_vec(dm)] * 3
    h_shape = jax.ShapeDtypeStruct((S, dm), _ACT)
    if res:
        out_shape, out_specs = (jax.ShapeDtypeStruct((S, dm), F32), h_shape), (_row(ts, dm), _row(ts, dm))
    else:
        out_shape, out_specs = h_shape, _row(ts, dm)
    return pl.pallas_call(body, grid=(S // ts,), in_specs=in_specs, out_specs=out_specs, out_shape=out_shape,
                          compiler_params=_cp(1), name=name)(*ins)


def _norm_bwd(dh, x, dres, g, sc, *, name):
    S, dm = x.shape
    ts = _tile(S, TS_ROW, SUB)

    def body(dh_ref, x_ref, dres_ref, g_ref, sc_ref, dx_ref, dg_ref, dsc_ref, dsh_ref):
        i = pl.program_id(0)
        xv = x_ref[...]
        dhv = dh_ref[...]
        rstd = lax.rsqrt(jnp.mean(xv * xv, axis=-1, keepdims=True) + EPS)
        xhat = xv * rstd
        hn = xhat * g_ref[...]
        dhn = dhv * (1.0 + sc_ref[...])
        dxh = dhn * g_ref[...]
        dx_ref[...] = dres_ref[...] + rstd * (dxh - xhat * jnp.mean(dxh * xhat, axis=-1, keepdims=True))

        @pl.when(i == 0)
        def _():
            dg_ref[...] = jnp.zeros_like(dg_ref)
            dsc_ref[...] = jnp.zeros_like(dsc_ref)
            dsh_ref[...] = jnp.zeros_like(dsh_ref)

        dg_ref[...] += _colsum(dhn * xhat)
        dsc_ref[...] += _colsum(dhv * hn)
        dsh_ref[...] += _colsum(dhv)

    vshape = jax.ShapeDtypeStruct((1, dm), F32)
    return pl.pallas_call(
        body, grid=(S // ts,), in_specs=[_row(ts, dm)] * 3 + [_vec(dm)] * 2,
        out_specs=(_row(ts, dm), _vec(dm), _vec(dm), _vec(dm)),
        out_shape=(jax.ShapeDtypeStruct((S, dm), F32), vshape, vshape, vshape),
        compiler_params=_cp(1), name=name)(dh, x, dres, g, sc)


def _gate_bwd(dxo, f, gate, *, name):
    S, dm = f.shape
    ts = _tile(S, TS_ROW, SUB)

    def body(dxo_ref, f_ref, gate_ref, df_ref, dgate_ref):
        i = pl.program_id(0)
        dv = dxo_ref[...]
        df_ref[...] = (gate_ref[...] * dv).astype(_ACT)

        @pl.when(i == 0)
        def _():
            dgate_ref[...] = jnp.zeros_like(dgate_ref)

        dgate_ref[...] += _colsum(dv * f_ref[...])

    return pl.pallas_call(
        body, grid=(S // ts,), in_specs=[_row(ts, dm), _row(ts, dm), _vec(dm)],
        out_specs=(_row(ts, dm), _vec(dm)),
        out_shape=(jax.ShapeDtypeStruct((S, dm), _ACT), jax.ShapeDtypeStruct((1, dm), F32)),
        compiler_params=_cp(1), name=name)(dxo, f, gate)


def _final_loss(x, f, gate, g, tgt, *, name):
    S, dm = x.shape
    ts = _tile(S, TS_ROW, SUB)

    def body(x_ref, f_ref, gate_ref, g_ref, t_ref, loss_ref, dx_ref, dg_ref):
        i = pl.program_id(0)
        xv = x_ref[...] + gate_ref[...] * f_ref[...]
        rstd = lax.rsqrt(jnp.mean(xv * xv, axis=-1, keepdims=True) + EPS)
        xhat = xv * rstd
        err = xhat * g_ref[...] - t_ref[...]
        dy = err * (1.0 / dm)
        dxh = dy * g_ref[...]
        dx_ref[...] = rstd * (dxh - xhat * jnp.mean(dxh * xhat, axis=-1, keepdims=True))

        @pl.when(i == 0)
        def _():
            loss_ref[...] = jnp.zeros_like(loss_ref)
            dg_ref[...] = jnp.zeros_like(dg_ref)

        loss_ref[...] += jnp.full((1, LANE), 0.5 * jnp.sum(jnp.mean(err * err, axis=-1, keepdims=True)), F32)
        dg_ref[...] += _colsum(dy * xhat)

    return pl.pallas_call(
        body, grid=(S // ts,), in_specs=[_row(ts, dm), _row(ts, dm), _vec(dm), _vec(dm), _row(ts, dm)],
        out_specs=(_vec(LANE), _row(ts, dm), _vec(dm)),
        out_shape=(jax.ShapeDtypeStruct((1, LANE), F32), jax.ShapeDtypeStruct((S, dm), F32),
                   jax.ShapeDtypeStruct((1, dm), F32)),
        compiler_params=_cp(1), name=name)(x, f, gate, g, tgt)


def _ffn_conv(t, halo, cw_ref, cb_ref):
    return ((cb_ref[...] + _shift_down(t, halo, 2) * cw_ref[0:1, :]) + _shift_down(t, halo, 1) * cw_ref[1:2, :]) \
        + t * cw_ref[2:3, :]


def _prev_halo_spec(ts, w, col=0):
    return pl.BlockSpec((SUB, w), lambda i: (jnp.maximum(i * (ts // SUB) - 1, 0), col))


def _ffn_act_fwd(up, cw, cb, *, name):
    S, w2 = up.shape
    ff = w2 // 2
    ts = _tile(S, TS_FFN, SUB)

    def body(up_ref, halo_ref, cw_ref, cb_ref, act_ref):
        i = pl.program_id(0)
        t = up_ref[...]
        halo = jnp.where(i > 0, halo_ref[...], 0.0)
        u = _ffn_conv(t, halo, cw_ref, cb_ref)
        act_ref[...] = (_silu(u[:, :ff]) * u[:, ff:]).astype(_ACT)

    return pl.pallas_call(
        body, grid=(S // ts,), in_specs=[_row(ts, w2), _prev_halo_spec(ts, w2), _vec(w2, FFN_CONV), _vec(w2)],
        out_specs=_row(ts, ff), out_shape=jax.ShapeDtypeStruct((S, ff), _ACT),
        compiler_params=_cp(1), name=name)(up, up, cw, cb)


def _ffn_act_bwd(up, dact, cw, cb, *, name):
    S, w2 = up.shape
    ff = w2 // 2
    ts = _tile(S, TS_FFN, SUB)

    def body(up_ref, halo_ref, dact_ref, cw_ref, cb_ref, du_ref):
        i = pl.program_id(0)
        t = up_ref[...]
        halo = jnp.where(i > 0, halo_ref[...], 0.0)
        u = _ffn_conv(t, halo, cw_ref, cb_ref)
        a, b = u[:, :ff], u[:, ff:]
        da = dact_ref[...]
        du_ref[:, :ff] = da * b * _dsilu(a)
        du_ref[:, ff:] = da * _silu(a)

    return pl.pallas_call(
        body, grid=(S // ts,),
        in_specs=[_row(ts, w2), _prev_halo_spec(ts, w2), _row(ts, ff), _vec(w2, FFN_CONV), _vec(w2)],
        out_specs=_row(ts, w2), out_shape=jax.ShapeDtypeStruct((S, w2), F32),
        compiler_params=_cp(1), name=name)(up, up, dact, cw, cb)


def _ffn_conv_bwd(du, up, cw, *, name):
    S, w2 = up.shape
    ts = _tile(S, TS_FFN, SUB)
    n = S // ts

    def body(du_ref, nxt_ref, up_ref, halo_ref, cw_ref, dup_ref, dcw_ref, dcb_ref):
        i = pl.program_id(0)
        dv = du_ref[...]
        nxt = jnp.where(i < n - 1, nxt_ref[...], 0.0)
        t = up_ref[...]
        halo = jnp.where(i > 0, halo_ref[...], 0.0)
        dup = (dv * cw_ref[2:3, :] + _shift_up(dv, nxt, 1) * cw_ref[1:2, :]) + _shift_up(dv, nxt, 2) * cw_ref[0:1, :]
        dup_ref[...] = dup.astype(_ACT)

        @pl.when(i == 0)
        def _():
            dcw_ref[...] = jnp.zeros_like(dcw_ref)
            dcb_ref[...] = jnp.zeros_like(dcb_ref)

        dcb_ref[...] += _colsum(dv)
        dcw_ref[2:3, :] += _colsum(dv * t)
        dcw_ref[1:2, :] += _colsum(dv * _shift_down(t, halo, 1))
        dcw_ref[0:1, :] += _colsum(dv * _shift_down(t, halo, 2))

    nxt_spec = pl.BlockSpec((SUB, w2), lambda i: (jnp.minimum((i + 1) * (ts // SUB), S // SUB - 1), 0))
    return pl.pallas_call(
        body, grid=(n,),
        in_specs=[_row(ts, w2), nxt_spec, _row(ts, w2), _prev_halo_spec(ts, w2), _vec(w2, FFN_CONV)],
        out_specs=(_row(ts, w2), _vec(w2, FFN_CONV), _vec(w2)),
        out_shape=(jax.ShapeDtypeStruct((S, w2), _ACT), jax.ShapeDtypeStruct((FFN_CONV, w2), F32),
                   jax.ShapeDtypeStruct((1, w2), F32)),
        compiler_params=_cp(1), name=name)(du, du, up, up, cw)


def _ssd_core(pre, halo, misc, cw_ref, cb_ref, dtb, alog):
    q = pre.shape[0]
    conv = cb_ref[...]
    for k in range(SSD_CONV):
        conv = conv + _shift_down(pre, halo, SSD_CONV - 1 - k) * cw_ref[k:k + 1, :]
    xbc = _silu(conv)
    raw = misc + dtb
    dt = _softplus(raw)
    a = -jnp.exp(alog)
    r = lax.broadcasted_iota(jnp.int32, (q, q), 0)
    c = lax.broadcasted_iota(jnp.int32, (q, q), 1)
    tri = r >= c
    acum = jnp.dot(tri.astype(F32), dt * a, precision=_HI, preferred_element_type=F32)
    return conv, xbc, raw, dt, a, acum, acum.T, tri


def _sel(v, j, lo):
    return jnp.where(lo, v[:, 2 * j:2 * j + 1], v[:, 2 * j + 1:2 * j + 2])


def _ssd_pair_fwd(xbc, dt, acum, acum_t, tri, dsk, g_mat, b_mat, c_mat, h_pair, j, lo, lo1, sub_lo):
    q = xbc.shape[0]
    x = xbc[:, LANE * j:LANE * (j + 1)]
    dtp = _sel(dt, j, lo)
    ap = _sel(acum, j, lo)
    xd = x * dtp
    ls, ms = [], []
    for h in (2 * j, 2 * j + 1):
        seg = acum[:, h:h + 1] - acum_t[h:h + 1, :]
        l_mat = jnp.exp(jnp.where(tri, seg, -jnp.inf))
        ls.append(l_mat)
        ms.append(g_mat * l_mat)
    yd = jnp.where(lo, _dot(ms[0], xd), _dot(ms[1], xd))
    ea = jnp.exp(ap)
    yo = _dot_nt(c_mat, h_pair) * ea
    dp = _sel(dsk, j, lo1)
    alast = acum[q - 1:q, :]
    e = jnp.exp(_sel(alast, j, lo1) - ap)
    cd = jnp.where(sub_lo, jnp.exp(alast[:, 2 * j:2 * j + 1]), jnp.exp(alast[:, 2 * j + 1:2 * j + 2]))
    return dict(x=x, dtp=dtp, ap=ap, xd=xd, ls=ls, ms=ms, ea=ea, yo=yo, dp=dp, e=e, cd=cd, y=yd + yo + x * dp)


def _gnorm(yg):
    half = SSD_INNER // SSD_GROUPS
    rstds, yns = [], []
    for g in range(SSD_GROUPS):
        part = yg[:, half * g:half * (g + 1)]
        rstd = lax.rsqrt(jnp.mean(part * part, axis=-1, keepdims=True) + EPS)
        rstds.append(rstd)
        yns.append(part * rstd)
    return rstds, yns


def _ssd_specs(nc, rev):
    q = SSD_CHUNK
    cidx = (lambda i: nc - 1 - i) if rev else (lambda i: i)
    return [
        pl.BlockSpec((q, SSD_XBC), lambda i: (cidx(i), C_XBC // SSD_XBC)),
        pl.BlockSpec((SUB, SSD_XBC), lambda i: (jnp.maximum(cidx(i) * (q // SUB) - 1, 0), C_XBC // SSD_XBC)),
        pl.BlockSpec((q, SSD_INNER), lambda i: (cidx(i), C_Z // SSD_INNER)),
        pl.BlockSpec((q, LANE), lambda i: (cidx(i), C_MISC // LANE)),
    ]


def _ssd_param_specs():
    return [_vec(SSD_XBC, SSD_CONV), _vec(SSD_XBC), _vec(LANE), _vec(LANE), _vec(LANE), _vec(SSD_INNER)]


def _ssd_fwd(proj, cw, cb, dtb, alog, dsk, ng, *, name):
    S = proj.shape[0]
    q = SSD_CHUNK
    nc = S // q
    npair = SSD_HEADS // 2

    def body(xbc_ref, halo_ref, z_ref, misc_ref, cw_ref, cb_ref, dtb_ref, alog_ref, dsk_ref, ng_ref,
             y_ref, hin_ref, h_ref):
        c = pl.program_id(0)

        @pl.when(c == 0)
        def _():
            h_ref[...] = jnp.zeros_like(h_ref)

        pre = xbc_ref[...]
        halo = jnp.where(c > 0, halo_ref[...], 0.0)
        conv, xbc, raw, dt, a, acum, acum_t, tri = _ssd_core(pre, halo, misc_ref[...], cw_ref, cb_ref,
                                                             dtb_ref[...], alog_ref[...])
        lo = lax.broadcasted_iota(jnp.int32, (q, LANE), 1) < LANE // 2
        lo1 = lo[:1]
        sub_lo = lax.broadcasted_iota(jnp.int32, (LANE, LANE), 0) < LANE // 2
        ys = []
        for g in range(SSD_GROUPS):
            b_mat = xbc[:, SSD_INNER + SSD_STATE * g:SSD_INNER + SSD_STATE * (g + 1)]
            c_mat = xbc[:, SSD_INNER + SSD_STATE * (SSD_GROUPS + g):SSD_INNER + SSD_STATE * (SSD_GROUPS + g + 1)]
            g_mat = _dot_nt(c_mat, b_mat)
            for jj in range(npair // SSD_GROUPS):
                j = g * (npair // SSD_GROUPS) + jj
                hj = h_ref[j]
                p = _ssd_pair_fwd(xbc, dt, acum, acum_t, tri, dsk_ref[...], g_mat, b_mat, c_mat, hj, j, lo, lo1, sub_lo)
                ys.append(p["y"])
                hin_ref[0, j] = hj
                h_ref[j] = p["cd"] * hj + _dot_tn(p["xd"] * p["e"], b_mat)
        yg = jnp.concatenate(ys, axis=1) * _silu(z_ref[...])
        _, yns = _gnorm(yg)
        y_ref[...] = jnp.concatenate(yns, axis=1) * ng_ref[...]

    return pl.pallas_call(
        body, grid=(nc,), in_specs=_ssd_specs(nc, False) + _ssd_param_specs(),
        out_specs=(pl.BlockSpec((q, SSD_INNER), lambda i: (i, 0)),
                   pl.BlockSpec((1, npair, LANE, LANE), lambda i: (i, 0, 0, 0))),
        out_shape=(jax.ShapeDtypeStruct((S, SSD_INNER), F32), jax.ShapeDtypeStruct((nc, npair, LANE, LANE), F32)),
        scratch_shapes=[pltpu.VMEM((npair, LANE, LANE), F32)], compiler_params=_cp(1), name=name,
    )(proj, proj, proj, proj, cw, cb, dtb, alog, dsk, ng)


def _ssd_bwd(proj, dycat, hin, cw, cb, dtb, alog, dsk, ng, *, name):
    S = proj.shape[0]
    q = SSD_CHUNK
    nc = S // q
    npair = SSD_HEADS // 2
    ppg = npair // SSD_GROUPS

    def body(xbc_ref, halo_ref, z_ref, misc_ref, dy_ref, hin_ref, cw_ref, cb_ref, dtb_ref, alog_ref, dsk_ref, ng_ref,
             dpre_ref, dz_ref, dmisc_ref, dcw_ref, dcb_ref, ddtb_ref, dalog_ref, ddsk_ref, dng_ref,
             dh_ref, carry_ref):
        i = pl.program_id(0)
        c = nc - 1 - i

        @pl.when(i == 0)
        def _():
            dh_ref[...] = jnp.zeros_like(dh_ref)
            carry_ref[...] = jnp.zeros_like(carry_ref)
            for r in (dcw_ref, dcb_ref, ddtb_ref, dalog_ref, ddsk_ref, dng_ref):
                r[...] = jnp.zeros_like(r)

        pre = xbc_ref[...]
        halo = jnp.where(c > 0, halo_ref[...], 0.0)
        conv, xbc, raw, dt, a, acum, acum_t, tri = _ssd_core(pre, halo, misc_ref[...], cw_ref, cb_ref,
                                                             dtb_ref[...], alog_ref[...])
        lane = lax.broadcasted_iota(jnp.int32, (q, LANE), 1)
        lane1 = lane[:1]
        rowi = lax.broadcasted_iota(jnp.int32, (q, LANE), 0)
        lastrow = rowi == q - 1
        lo = lane < LANE // 2
        lo1 = lo[:1]
        sub_lo = lax.broadcasted_iota(jnp.int32, (LANE, LANE), 0) < LANE // 2
        dsk = dsk_ref[...]
        alast = acum[q - 1:q, :]

        def halves(t):
            return _rowsum(jnp.where(lo, t, 0.0)), _rowsum(jnp.where(lo, 0.0, t))

        def put(ha, va, vb):
            ln = lane if va.shape[0] == q else lane1
            return jnp.where(ln == ha, va, 0.0) + jnp.where(ln == ha + 1, vb, 0.0)

        mats, pairs = [], []
        for g in range(SSD_GROUPS):
            b_mat = xbc[:, SSD_INNER + SSD_STATE * g:SSD_INNER + SSD_STATE * (g + 1)]
            c_mat = xbc[:, SSD_INNER + SSD_STATE * (SSD_GROUPS + g):SSD_INNER + SSD_STATE * (SSD_GROUPS + g + 1)]
            g_mat = _dot_nt(c_mat, b_mat)
            mats.append((b_mat, c_mat, g_mat))
            for jj in range(ppg):
                j = g * ppg + jj
                pairs.append(_ssd_pair_fwd(xbc, dt, acum, acum_t, tri, dsk, g_mat, b_mat, c_mat, hin_ref[0, j],
                                           j, lo, lo1, sub_lo))
        z = z_ref[...]
        sz = _silu(z)
        y = jnp.concatenate([p["y"] for p in pairs], axis=1)
        rstds, yns = _gnorm(y * sz)
        dout = dy_ref[...]
        dng_ref[...] += _colsum(dout * jnp.concatenate(yns, axis=1))
        dyn = dout * ng_ref[...]
        half = SSD_INNER // SSD_GROUPS
        dygs = []
        for g in range(SSD_GROUPS):
            dyn_g = dyn[:, half * g:half * (g + 1)]
            dygs.append(rstds[g] * (dyn_g - yns[g] * jnp.mean(dyn_g * yns[g], axis=-1, keepdims=True)))
        dyg = jnp.concatenate(dygs, axis=1)
        dyv = dyg * sz
        dz_ref[...] = (dyg * y * _dsilu(z)).astype(_ACT)

        da_acc = jnp.zeros((q, LANE), F32)
        ddt = jnp.zeros((q, LANE), F32)
        dds = jnp.zeros((1, LANE), F32)
        dxs, dbs, dcs = [], [], []
        for g in range(SSD_GROUPS):
            b_mat, c_mat, g_mat = mats[g]
            dg_mat = jnp.zeros((q, q), F32)
            db = jnp.zeros((q, SSD_STATE), F32)
            dc = jnp.zeros((q, SSD_STATE), F32)
            for jj in range(ppg):
                j = g * ppg + jj
                ha = 2 * j
                p = pairs[j]
                hj = hin_ref[0, j]
                dyp = dyv[:, LANE * j:LANE * (j + 1)]
                dsum = _colsum(dyp * p["x"])
                dds = dds + put(ha, _rowsum(jnp.where(lo1, dsum, 0.0)), _rowsum(jnp.where(lo1, 0.0, dsum)))
                dx = dyp * p["dp"]
                dw = dyp * p["ea"]
                dc = dc + _dot(dw, hj)
                dh_yo = _dot_tn(dw, c_mat)
                ra, rb = halves(dyp * p["yo"])
                da_acc = da_acc + put(ha, ra, rb)
                dxd = jnp.zeros((q, LANE), F32)
                for idx in range(2):
                    dyh = jnp.where(lo, dyp, 0.0) if idx == 0 else jnp.where(lo, 0.0, dyp)
                    dm = _dot_nt(dyh, p["xd"])
                    dxd = dxd + _dot_tn(p["ms"][idx], dyh)
                    dg_mat = dg_mat + dm * p["ls"][idx]
                    t = dm * p["ms"][idx]
                    da_h = _rowsum(t) - _rowsum(t.T)
                    da_acc = da_acc + jnp.where(lane == ha + idx, da_h, 0.0)
                dhn = dh_ref[j]
                s = _rowsum(dhn * hj)
                sa = jnp.sum(jnp.where(sub_lo[:, :1], s, 0.0), keepdims=True)
                sb = jnp.sum(jnp.where(sub_lo[:, :1], 0.0, s), keepdims=True)
                cda, cdb = jnp.exp(alast[:, ha:ha + 1]), jnp.exp(alast[:, ha + 1:ha + 2])
                db = db + _dot(p["xd"] * p["e"], dhn)
                r = _dot_nt(b_mat, dhn)
                dxd = dxd + r * p["e"]
                qa, qb = halves(r * p["xd"] * p["e"])
                da_acc = da_acc - put(ha, qa, qb)
                tot_a = sa * cda + jnp.sum(qa, keepdims=True)
                tot_b = sb * cdb + jnp.sum(qb, keepdims=True)
                da_acc = da_acc + jnp.where(lastrow, put(ha, tot_a, tot_b), 0.0)
                dh_ref[j] = p["cd"] * dhn + dh_yo
                dx = dx + dxd * p["dtp"]
                ua, ub = halves(dxd * p["x"])
                ddt = ddt + put(ha, ua, ub)
                dxs.append(dx)
            dc = dc + _dot(dg_mat, b_mat)
            db = db + _dot_tn(dg_mat, c_mat)
            dbs.append(db)
            dcs.append(dc)
        r2 = lax.broadcasted_iota(jnp.int32, (q, q), 0)
        c2 = lax.broadcasted_iota(jnp.int32, (q, q), 1)
        dda = jnp.dot((c2 >= r2).astype(F32), da_acc, precision=_HI, preferred_element_type=F32)
        ddt = ddt + dda * a
        dalog_ref[...] += _colsum(dda * dt) * a
        ddsk_ref[...] += dds
        draw = jnp.where(lane < SSD_HEADS, ddt * _sigmoid(raw), 0.0)
        ddtb_ref[...] += _colsum(draw)
        dmisc_ref[...] = draw
        dconv = jnp.concatenate(dxs + dbs + dcs, axis=1) * _dsilu(conv)
        dcb_ref[...] += _colsum(dconv)
        nxt = carry_ref[...]
        dpre = jnp.zeros_like(dconv)
        for k in range(SSD_CONV):
            dcw_ref[k:k + 1, :] += _colsum(dconv * _shift_down(pre, halo, SSD_CONV - 1 - k))
            dpre = dpre + _shift_up(dconv, nxt, SSD_CONV - 1 - k) * cw_ref[k:k + 1, :]
        dpre_ref[...] = dpre.astype(_ACT)
        carry_ref[...] = dconv[:SUB]

    rev = lambda i: (nc - 1 - i, 0)
    vshape = lambda w, r=1: jax.ShapeDtypeStruct((r, w), F32)
    return pl.pallas_call(
        body, grid=(nc,),
        in_specs=_ssd_specs(nc, True) + [pl.BlockSpec((q, SSD_INNER), rev),
                                         pl.BlockSpec((1, npair, LANE, LANE), lambda i: (nc - 1 - i, 0, 0, 0))]
        + _ssd_param_specs(),
        out_specs=(pl.BlockSpec((q, SSD_XBC), rev), pl.BlockSpec((q, SSD_INNER), rev), pl.BlockSpec((q, LANE), rev),
                   _vec(SSD_XBC, SSD_CONV), _vec(SSD_XBC), _vec(LANE), _vec(LANE), _vec(LANE), _vec(SSD_INNER)),
        out_shape=(jax.ShapeDtypeStruct((S, SSD_XBC), _ACT), jax.ShapeDtypeStruct((S, SSD_INNER), _ACT),
                   jax.ShapeDtypeStruct((S, LANE), F32),
                   vshape(SSD_XBC, SSD_CONV), vshape(SSD_XBC), vshape(LANE), vshape(LANE), vshape(LANE),
                   vshape(SSD_INNER)),
        scratch_shapes=[pltpu.VMEM((npair, LANE, LANE), F32), pltpu.VMEM((SUB, SSD_XBC), F32)],
        compiler_params=_cp(1), name=name,
    )(proj, proj, proj, proj, dycat, hin, cw, cb, dtb, alog, dsk, ng)


def _rope(x, cosf, sina, sinb):
    return x * cosf + pltpu.roll(x, LANE - MLA_ROPE // 2, 1) * sina + pltpu.roll(x, MLA_ROPE // 2, 1) * sinb


def _rope_t(dy, cosf, sina, sinb):
    return dy * cosf + pltpu.roll(dy * sina, MLA_ROPE // 2, 1) + pltpu.roll(dy * sinb, LANE - MLA_ROPE // 2, 1)


def _rope_lanes(shape):
    lane = lax.broadcasted_iota(jnp.int32, shape, 1)
    return (lane >= ROPE_LANE) & (lane < ROPE_LANE + MLA_ROPE)


def _mla_prep_fwd(proj, cosf, sina, sinb, gq, gkv, wuq, wukv, *, name):
    S = proj.shape[0]
    ts = _tile(S, TS_ROW, SUB)
    hw = MLA_HEADS * LANE

    def body(cq_ref, ckv_ref, misc_ref, cos_ref, sa_ref, sb_ref, gq_ref, gkv_ref, wuq_ref, wukv_ref,
             q_ref, k_ref, v_ref, vt_ref):
        cosv, sav, sbv = cos_ref[...], sa_ref[...], sb_ref[...]
        cq = cq_ref[...]
        qn = cq * lax.rsqrt(jnp.mean(cq * cq, axis=-1, keepdims=True) + EPS) * gq_ref[...]
        qh = _dot(qn, wuq_ref[...])
        ckv = ckv_ref[...]
        kvn = ckv * lax.rsqrt(jnp.mean(ckv * ckv, axis=-1, keepdims=True) + EPS) * gkv_ref[...]
        kv = _dot(kvn, wukv_ref[...])
        kr = _rope(jnp.where(_rope_lanes((ts, LANE)), misc_ref[...], 0.0), cosv, sav, sbv)
        for h in range(MLA_HEADS):
            sl = slice(LANE * h, LANE * (h + 1))
            q_ref[:, sl] = (_rope(qh[:, sl], cosv, sav, sbv) * _Q_SCALE).astype(_ACT)
            k_ref[:, sl] = (kv[:, sl] + kr).astype(_ACT)
        v_ref[...] = kv[:, hw:].astype(_ACT)
        vt_ref[...] = kv[:, hw:].T.astype(_ACT)

    oshape = jax.ShapeDtypeStruct((S, hw), _ACT)
    return pl.pallas_call(
        body, grid=(S // ts,),
        in_specs=[_row(ts, MLA_QR, C_CQ // MLA_QR), _row(ts, MLA_KVR, C_CKV // MLA_KVR), _row(ts, LANE, C_MISC // LANE),
                  _row(ts, LANE), _row(ts, LANE), _row(ts, LANE), _vec(MLA_QR), _vec(MLA_KVR),
                  _vec(hw, MLA_QR), _vec(2 * hw, MLA_KVR)],
        out_specs=(_row(ts, hw),) * 3 + (pl.BlockSpec((hw, ts), lambda i: (0, i)),),
        out_shape=(oshape,) * 3 + (jax.ShapeDtypeStruct((hw, S), _ACT),), compiler_params=_cp(1), name=name,
    )(proj, proj, proj, cosf, sina, sinb, gq, gkv, wuq, wukv)


def _mla_prep_bwd(proj, dq, dk, dv, dmisc_ssd, cosf, sina, sinb, gq, gkv, wuq, wukv, *, name):
    S = proj.shape[0]
    ts = _tile(S, TS_ROW, SUB)
    hw = MLA_HEADS * LANE

    def body(cq_ref, ckv_ref, dq_ref, dk_ref, dv_ref, dms_ref, cos_ref, sa_ref, sb_ref, gq_ref, gkv_ref,
             wuq_ref, wukv_ref, dcq_ref, dckv_ref, dmisc_ref, dqh_ref, dkv_ref, qn_ref, kvn_ref, dgq_ref, dgkv_ref):
        i = pl.program_id(0)
        cosv, sav, sbv = cos_ref[...], sa_ref[...], sb_ref[...]

        @pl.when(i == 0)
        def _():
            dgq_ref[...] = jnp.zeros_like(dgq_ref)
            dgkv_ref[...] = jnp.zeros_like(dgkv_ref)

        dqh = jnp.concatenate([_rope_t(dq_ref[:, LANE * h:LANE * (h + 1)], cosv, sav, sbv)
                               for h in range(MLA_HEADS)], axis=1)
        dqh_ref[...] = dqh.astype(_ACT)
        dkv = jnp.concatenate([dk_ref[...], dv_ref[...]], axis=1)
        dkv_ref[...] = dkv.astype(_ACT)

        def norm_bwd(x, g, dn, dg_ref, n_ref):
            rstd = lax.rsqrt(jnp.mean(x * x, axis=-1, keepdims=True) + EPS)
            xhat = x * rstd
            n_ref[...] = (xhat * g).astype(_ACT)
            dg_ref[...] += _colsum(dn * xhat)
            dxh = dn * g
            return rstd * (dxh - xhat * jnp.mean(dxh * xhat, axis=-1, keepdims=True))

        dcq_ref[...] = norm_bwd(cq_ref[...], gq_ref[...], _dot_nt(dqh, wuq_ref[...]), dgq_ref, qn_ref).astype(_ACT)
        dckv_ref[...] = norm_bwd(ckv_ref[...], gkv_ref[...], _dot_nt(dkv, wukv_ref[...]), dgkv_ref, kvn_ref).astype(_ACT)
        dks = dk_ref[:, 0:LANE]
        for h in range(1, MLA_HEADS):
            dks = dks + dk_ref[:, LANE * h:LANE * (h + 1)]
        rl = _rope_lanes((ts, LANE))
        dkr = _rope_t(jnp.where(rl, dks, 0.0), cosv, sav, sbv)
        dmisc_ref[...] = (dms_ref[...] + jnp.where(rl, dkr, 0.0)).astype(_ACT)

    act = lambda w: jax.ShapeDtypeStruct((S, w), _ACT)
    return pl.pallas_call(
        body, grid=(S // ts,),
        in_specs=[_row(ts, MLA_QR, C_CQ // MLA_QR), _row(ts, MLA_KVR, C_CKV // MLA_KVR),
                  _row(ts, hw), _row(ts, hw), _row(ts, hw), _row(ts, LANE),
                  _row(ts, LANE), _row(ts, LANE), _row(ts, LANE), _vec(MLA_QR), _vec(MLA_KVR),
                  _vec(hw, MLA_QR), _vec(2 * hw, MLA_KVR)],
        out_specs=(_row(ts, MLA_QR), _row(ts, MLA_KVR), _row(ts, LANE), _row(ts, hw), _row(ts, 2 * hw),
                   _row(ts, MLA_QR), _row(ts, MLA_KVR), _vec(MLA_QR), _vec(MLA_KVR)),
        out_shape=(act(MLA_QR), act(MLA_KVR), act(LANE), act(hw), act(2 * hw), act(MLA_QR), act(MLA_KVR),
                   jax.ShapeDtypeStruct((1, MLA_QR), F32), jax.ShapeDtypeStruct((1, MLA_KVR), F32)),
        compiler_params=_cp(1), name=name,
    )(proj, proj, dq, dk, dv, dmisc_ssd, cosf, sina, sinb, gq, gkv, wuq, wukv)


_MLA_SCALE = 1.0 / math.sqrt(MLA_NOPE + MLA_ROPE)
_LOG2E = 1.4426950408889634
_Q_SCALE = _MLA_SCALE * _LOG2E
ATT_CHUNK = 512


def _tri_grid(nq, by_key):
    if by_key:
        pairs = [(i, j) for j in range(nq) for i in range(j, nq)]
    else:
        pairs = [(i, j) for i in range(nq) for j in range(i + 1)]
    return jnp.asarray([p[0] for p in pairs], jnp.int32), jnp.asarray([p[1] for p in pairs], jnp.int32)


def _attn_fwd(q, k, vt, *, name):
    S = q.shape[0]
    tq = _tile(S, TQ_ATT)
    nq = S // tq
    itab, jtab = _tri_grid(nq, False)

    def body(it_ref, jt_ref, q_ref, k_ref, vt_ref, o_ref, lse_ref, lset_ref, m_ref, l_ref, acc_ref):
        t = pl.program_id(1)
        i, j = it_ref[t], jt_ref[t]

        @pl.when(j == 0)
        def _():
            m_ref[...] = jnp.full_like(m_ref, -jnp.inf)
            l_ref[...] = jnp.zeros_like(l_ref)
            acc_ref[...] = jnp.zeros_like(acc_ref)

        def step(diagonal):
            s = _dot_nt(k_ref[...], q_ref[...])
            if diagonal:
                kk = lax.broadcasted_iota(jnp.int32, (tq, tq), 0)
                s = jnp.where(kk <= lax.broadcasted_iota(jnp.int32, (tq, tq), 1), s, -jnp.inf)
            m_prev = m_ref[...]
            m_new = jnp.maximum(m_prev, jnp.max(s, axis=0, keepdims=True))
            p = jnp.exp2(s - m_new)
            alpha = jnp.exp2(m_prev - m_new)
            l_ref[...] = alpha * l_ref[...] + _colsum(p)
            acc_ref[...] = alpha * acc_ref[...] + _dot(vt_ref[...], p)
            m_ref[...] = m_new

        pl.when(j < i)(functools.partial(step, False))
        pl.when(j == i)(functools.partial(step, True))

        @pl.when(j == i)
        def _():
            o_ref[...] = (acc_ref[...] / l_ref[...]).T
            lse = m_ref[...] + jnp.log2(l_ref[...])
            lset_ref[...] = jnp.broadcast_to(lse, (SUB, tq))
            lse_ref[...] = jnp.broadcast_to(lse, (LANE, tq)).T

    qspec = pl.BlockSpec((tq, LANE), lambda h, t, it, jt: (it[t], h))
    kspec = pl.BlockSpec((tq, LANE), lambda h, t, it, jt: (jt[t], h))
    vtspec = pl.BlockSpec((LANE, tq), lambda h, t, it, jt: (h, jt[t]))
    oshape = jax.ShapeDtypeStruct((S, MLA_HEADS * LANE), F32)
    return pl.pallas_call(
        body,
        grid_spec=pltpu.PrefetchScalarGridSpec(
            num_scalar_prefetch=2, grid=(MLA_HEADS, itab.shape[0]), in_specs=[qspec, kspec, vtspec],
            out_specs=(qspec, qspec, pl.BlockSpec((SUB, tq), lambda h, t, it, jt: (h, it[t]))),
            scratch_shapes=[pltpu.VMEM((1, tq), F32), pltpu.VMEM((1, tq), F32), pltpu.VMEM((LANE, tq), F32)]),
        out_shape=(oshape, oshape, jax.ShapeDtypeStruct((MLA_HEADS * SUB, S), F32)),
        compiler_params=_cp(2), name=name)(itab, jtab, q, k, vt)


def _attn_bwd_dq(q, k, v, o, lse, dycat, *, name):
    S = q.shape[0]
    tq = _tile(S, TQ_ATT)
    nq = S // tq
    rc = min(ATT_CHUNK, tq)
    itab, jtab = _tri_grid(nq, False)

    def body(it_ref, jt_ref, q_ref, k_ref, v_ref, o_ref, lse_ref, do_ref, dq_ref, acc_ref):
        t = pl.program_id(1)
        i, j = it_ref[t], jt_ref[t]

        @pl.when(j == 0)
        def _():
            acc_ref[...] = jnp.zeros_like(acc_ref)

        def step(diagonal):
            kv, vv = k_ref[...], v_ref[...]
            for r in range(tq // rc):
                rows = slice(r * rc, (r + 1) * rc)
                s = _dot_nt(q_ref[rows, :], kv)
                if diagonal:
                    rr = r * rc + lax.broadcasted_iota(jnp.int32, (rc, tq), 0)
                    s = jnp.where(lax.broadcasted_iota(jnp.int32, (rc, tq), 1) <= rr, s, -jnp.inf)
                p = jnp.exp2(s - lse_ref[rows, 0:1])
                dov = do_ref[rows, :]
                delta = _rowsum(dov * o_ref[rows, :])
                ds = p * (_dot_nt(dov, vv) - delta)
                acc_ref[rows, :] += _dot(ds, kv)

        pl.when(j < i)(functools.partial(step, False))
        pl.when(j == i)(functools.partial(step, True))

        @pl.when(j == i)
        def _():
            dq_ref[...] = acc_ref[...] * _MLA_SCALE

    qspec = pl.BlockSpec((tq, LANE), lambda h, t, it, jt: (it[t], h))
    kspec = pl.BlockSpec((tq, LANE), lambda h, t, it, jt: (jt[t], h))
    dospec = pl.BlockSpec((tq, LANE), lambda h, t, it, jt: (it[t], SSD_INNER // LANE + h))
    return pl.pallas_call(
        body,
        grid_spec=pltpu.PrefetchScalarGridSpec(
            num_scalar_prefetch=2, grid=(MLA_HEADS, itab.shape[0]),
            in_specs=[qspec, kspec, kspec, qspec, qspec, dospec], out_specs=qspec,
            scratch_shapes=[pltpu.VMEM((tq, LANE), F32)]),
        out_shape=jax.ShapeDtypeStruct((S, MLA_HEADS * LANE), F32),
        compiler_params=_cp(2), name=name)(itab, jtab, q, k, v, o, lse, dycat)


def _attn_bwd_dkv(q, k, v, o, lset, dycat, *, name):
    S = q.shape[0]
    tq = _tile(S, TQ_ATT)
    nq = S // tq
    kc = min(ATT_CHUNK, tq)
    itab, jtab = _tri_grid(nq, True)

    def body(it_ref, jt_ref, q_ref, k_ref, v_ref, o_ref, lset_ref, do_ref, dk_ref, dv_ref, dk_acc, dv_acc):
        t = pl.program_id(1)
        i, j = it_ref[t], jt_ref[t]

        @pl.when(i == j)
        def _():
            dk_acc[...] = jnp.zeros_like(dk_acc)
            dv_acc[...] = jnp.zeros_like(dv_acc)

        def step(diagonal):
            qv, dov = q_ref[...], do_ref[...]
            delta = lax.dot_general(jnp.ones((SUB, LANE), F32), dov * o_ref[...], (((1,), (1,)), ((), ())),
                                    precision=_HI, preferred_element_type=F32)[0:1]
            lse = lset_ref[0:1, :]
            for c in range(tq // kc):
                rows = slice(c * kc, (c + 1) * kc)
                s = _dot_nt(k_ref[rows, :], qv)
                if diagonal:
                    kk = c * kc + lax.broadcasted_iota(jnp.int32, (kc, tq), 0)
                    s = jnp.where(kk <= lax.broadcasted_iota(jnp.int32, (kc, tq), 1), s, -jnp.inf)
                p = jnp.exp2(s - lse)
                dv_acc[rows, :] += _dot(p, dov)
                ds = p * (_dot_nt(v_ref[rows, :], dov) - delta)
                dk_acc[rows, :] += _dot(ds, qv)

        pl.when(i > j)(functools.partial(step, False))
        pl.when(i == j)(functools.partial(step, True))

        @pl.when(i == nq - 1)
        def _():
            dk_ref[...] = dk_acc[...] * (1.0 / _LOG2E)
            dv_ref[...] = dv_acc[...]

    qspec = pl.BlockSpec((tq, LANE), lambda h, t, it, jt: (it[t], h))
    kspec = pl.BlockSpec((tq, LANE), lambda h, t, it, jt: (jt[t], h))
    dospec = pl.BlockSpec((tq, LANE), lambda h, t, it, jt: (it[t], SSD_INNER // LANE + h))
    lspec = pl.BlockSpec((SUB, tq), lambda h, t, it, jt: (h, it[t]))
    oshape = jax.ShapeDtypeStruct((S, MLA_HEADS * LANE), F32)
    return pl.pallas_call(
        body,
        grid_spec=pltpu.PrefetchScalarGridSpec(
            num_scalar_prefetch=2, grid=(MLA_HEADS, itab.shape[0]),
            in_specs=[qspec, kspec, kspec, qspec, lspec, dospec], out_specs=(kspec, kspec),
            scratch_shapes=[pltpu.VMEM((tq, LANE), F32), pltpu.VMEM((tq, LANE), F32)]),
        out_shape=(oshape, oshape), compiler_params=_cp(2), name=name)(itab, jtab, q, k, v, o, lset, dycat)


_SWA_SCALE = 1.0 / math.sqrt(SWA_HD)
_SWA_KW = SWA_KV * LANE


def _swa_specs(S, ts, rev):
    n = S // ts
    t = (lambda i: n - 1 - i) if rev else (lambda i: i)
    hb = lambda i: jnp.maximum(t(i) * (ts // WINDOW) - 1, 0)
    return [
        pl.BlockSpec((ts, SWA_HEADS * LANE), lambda i: (t(i), C_SQ // (SWA_HEADS * LANE))),
        pl.BlockSpec((ts, _SWA_KW), lambda i: (t(i), C_SK // _SWA_KW)),
        pl.BlockSpec((WINDOW, _SWA_KW), lambda i: (hb(i), C_SK // _SWA_KW)),
        pl.BlockSpec((ts, _SWA_KW), lambda i: (t(i), C_SV // _SWA_KW)),
        pl.BlockSpec((WINDOW, _SWA_KW), lambda i: (hb(i), C_SV // _SWA_KW)),
    ]


def _swa_scores(qh, kk, t, b, ts):
    s = _dot_nt(qh, kk) * _SWA_SCALE
    row = lax.broadcasted_iota(jnp.int32, (WINDOW, 2 * WINDOW), 0)
    col = lax.broadcasted_iota(jnp.int32, (WINDOW, 2 * WINDOW), 1)
    rel = WINDOW + row - col
    kpos = t * ts + (b - 1) * WINDOW + col
    return jnp.where((rel >= 0) & (rel < WINDOW) & (kpos >= 0), s, -jnp.inf)


def _swa_fwd(proj, sinks, *, name):
    S = proj.shape[0]
    ts = _tile(S, TS_SWA)
    nb = ts // WINDOW

    def body(q_ref, k_ref, kh_ref, v_ref, vh_ref, sink_ref, o_ref, lse_ref):
        t = pl.program_id(0)
        kext = jnp.concatenate([kh_ref[...], k_ref[...]], axis=0)
        vext = jnp.concatenate([vh_ref[...], v_ref[...]], axis=0)
        for b in range(nb):
            rows = slice(WINDOW * b, WINDOW * (b + 1))
            for h in range(SWA_HEADS):
                kvl = slice(LANE * (h // (SWA_HEADS // SWA_KV)), LANE * (h // (SWA_HEADS // SWA_KV) + 1))
                hl = slice(LANE * h, LANE * (h + 1))
                kk = kext[WINDOW * b:WINDOW * (b + 2), kvl]
                vv = vext[WINDOW * b:WINDOW * (b + 2), kvl]
                s = _swa_scores(q_ref[rows, hl], kk, t, b, ts)
                sk = sink_ref[:, h:h + 1]
                m = jnp.maximum(jnp.max(s, axis=1, keepdims=True), sk)
                p = jnp.exp(s - m)
                den = _rowsum(p) + jnp.exp(sk - m)
                o_ref[rows, hl] = _dot(p, vv) / den
                lse_ref[rows, hl] = jnp.broadcast_to(m + jnp.log(den), (WINDOW, LANE))

    oshape = jax.ShapeDtypeStruct((S, SWA_HEADS * LANE), F32)
    ospec = pl.BlockSpec((ts, SWA_HEADS * LANE), lambda i: (i, 0))
    return pl.pallas_call(
        body, grid=(S // ts,), in_specs=_swa_specs(S, ts, False) + [_vec(LANE)], out_specs=(ospec, ospec),
        out_shape=(oshape, oshape), compiler_params=_cp(1), name=name)(proj, proj, proj, proj, proj, sinks)


def _swa_bwd(proj, o, lse, dycat, sinks, *, name):
    S = proj.shape[0]
    ts = _tile(S, TS_SWA)
    nb = ts // WINDOW
    n = S // ts
    grp = SWA_HEADS // SWA_KV

    def body(q_ref, k_ref, kh_ref, v_ref, vh_ref, o_ref, lse_ref, do_ref, sink_ref,
             dq_ref, dk_ref, dv_ref, dsink_ref, dk_carry, dv_carry):
        i = pl.program_id(0)
        t = n - 1 - i

        @pl.when(i == 0)
        def _():
            dk_carry[...] = jnp.zeros_like(dk_carry)
            dv_carry[...] = jnp.zeros_like(dv_carry)
            dsink_ref[...] = jnp.zeros_like(dsink_ref)

        kext = jnp.concatenate([kh_ref[...], k_ref[...]], axis=0)
        vext = jnp.concatenate([vh_ref[...], v_ref[...]], axis=0)
        lane1 = lax.broadcasted_iota(jnp.int32, (1, LANE), 1)
        dkb = [[jnp.zeros((WINDOW, LANE), F32) for _ in range(SWA_KV)] for _ in range(nb + 1)]
        dvb = [[jnp.zeros((WINDOW, LANE), F32) for _ in range(SWA_KV)] for _ in range(nb + 1)]
        dsink = jnp.zeros((1, LANE), F32)
        for b in range(nb):
            rows = slice(WINDOW * b, WINDOW * (b + 1))
            for h in range(SWA_HEADS):
                kvh = h // grp
                kvl = slice(LANE * kvh, LANE * (kvh + 1))
                hl = slice(LANE * h, LANE * (h + 1))
                kk = kext[WINDOW * b:WINDOW * (b + 2), kvl]
                vv = vext[WINDOW * b:WINDOW * (b + 2), kvl]
                qh = q_ref[rows, hl]
                lse_h = lse_ref[rows, LANE * h:LANE * h + 1]
                p = jnp.exp(_swa_scores(qh, kk, t, b, ts) - lse_h)
                doh = do_ref[rows, hl]
                delta = _rowsum(doh * o_ref[rows, hl])
                ds = p * (_dot_nt(doh, vv) - delta)
                sk = sink_ref[:, h:h + 1]
                dsink = dsink + jnp.where(lane1 == h, -jnp.sum(jnp.exp(sk - lse_h) * delta, keepdims=True), 0.0)
                dq_ref[rows, hl] = (_dot(ds, kk) * _SWA_SCALE).astype(_ACT)
                dkk = _dot_tn(ds, qh) * _SWA_SCALE
                dvv = _dot_tn(p, doh)
                dkb[b][kvh] = dkb[b][kvh] + dkk[:WINDOW]
                dkb[b + 1][kvh] = dkb[b + 1][kvh] + dkk[WINDOW:]
                dvb[b][kvh] = dvb[b][kvh] + dvv[:WINDOW]
                dvb[b + 1][kvh] = dvb[b + 1][kvh] + dvv[WINDOW:]
        dsink_ref[...] += dsink
        for dref, blocks, carry in ((dk_ref, dkb, dk_carry), (dv_ref, dvb, dv_carry)):
            old = carry[...]
            for b in range(1, nb + 1):
                blk = jnp.concatenate(blocks[b], axis=1)
                if b == nb:
                    blk = blk + old
                dref[WINDOW * (b - 1):WINDOW * b, :] = blk.astype(_ACT)
            carry[...] = jnp.concatenate(blocks[0], axis=1)

    hw = SWA_HEADS * LANE
    rev = lambda i: (n - 1 - i, 0)
    mix = lambda i: (n - 1 - i, (SSD_INNER + MLA_HEADS * LANE) // hw)
    return pl.pallas_call(
        body, grid=(n,),
        in_specs=_swa_specs(S, ts, True) + [pl.BlockSpec((ts, hw), rev), pl.BlockSpec((ts, hw), rev),
                                            pl.BlockSpec((ts, hw), mix), _vec(LANE)],
        out_specs=(pl.BlockSpec((ts, hw), rev), pl.BlockSpec((ts, _SWA_KW), rev), pl.BlockSpec((ts, _SWA_KW), rev),
                   _vec(LANE)),
        out_shape=(jax.ShapeDtypeStruct((S, hw), _ACT), jax.ShapeDtypeStruct((S, _SWA_KW), _ACT),
                   jax.ShapeDtypeStruct((S, _SWA_KW), _ACT), jax.ShapeDtypeStruct((1, LANE), F32)),
        scratch_shapes=[pltpu.VMEM((WINDOW, _SWA_KW), F32), pltpu.VMEM((WINDOW, _SWA_KW), F32)],
        compiler_params=_cp(1), name=name)(proj, proj, proj, proj, proj, o, lse, dycat, sinks)


def _exchange(arrays, *, scatter, name):
    n = len(arrays)

    def body(*refs):
        ins, outs = refs[:n], refs[n:2 * n]
        send_sems, recv_sems, loc_sems = refs[2 * n:]
        x, y, c = lax.axis_index("x"), lax.axis_index("y"), lax.axis_index("c")
        me = 4 * x + 2 * y + c

        def src(i, dest):
            return ins[i].at[dest] if scatter else ins[i]

        local = [pltpu.make_async_copy(src(i, me), outs[i].at[me], loc_sems.at[i]) for i in range(n)]
        for cp in local:
            cp.start()
        sends, recvs = [], []
        for k in range(1, NDEV):
            px = 1 - x if k & 4 else x
            py = 1 - y if k & 2 else y
            pc = 1 - c if k & 1 else c
            peer = 4 * px + 2 * py + pc
            for i in range(n):
                common = dict(send_sem=send_sems.at[i, k - 1], recv_sem=recv_sems.at[i, k - 1],
                              device_id=(px, py, pc), device_id_type=pl.DeviceIdType.MESH)
                sends.append(pltpu.make_async_remote_copy(src_ref=src(i, peer), dst_ref=outs[i].at[me], **common))
                recvs.append(pltpu.make_async_remote_copy(src_ref=src(i, peer), dst_ref=outs[i].at[peer], **common))
        for cp in sends:
            cp.start()
        for cp in recvs:
            cp.wait_recv()
        for cp in sends:
            cp.wait_send()
        for cp in local:
            cp.wait()

    hbm = pl.BlockSpec(memory_space=pl.ANY)
    out_shape = tuple(jax.ShapeDtypeStruct(a.shape if scatter else (NDEV,) + a.shape, a.dtype) for a in arrays)
    return pl.pallas_call(
        body, in_specs=[hbm] * n, out_specs=tuple([hbm] * n), out_shape=out_shape,
        scratch_shapes=[pltpu.SemaphoreType.DMA((n, NDEV - 1)), pltpu.SemaphoreType.DMA((n, NDEV - 1)),
                        pltpu.SemaphoreType.DMA((n,))],
        name=name)(*arrays)


def _adamw(w, m, v, parts, *, name):
    R, C = w.shape
    npart = parts.shape[0]
    cap = max(SUB, ((1 << 18) // C) // SUB * SUB)
    tr = _tile(R, cap, SUB)

    def body(w_ref, m_ref, v_ref, p_ref, g_ref, d_ref, mo_ref, vo_ref):
        g = p_ref[0]
        for k in range(1, npart):
            g = g + p_ref[k]
        mn = ADAM_B1 * m_ref[...] + (1.0 - ADAM_B1) * g
        vn = ADAM_B2 * v_ref[...] + (1.0 - ADAM_B2) * (g * g)
        m_hat = mn / (1.0 - ADAM_B1 ** ADAM_STEP)
        v_hat = vn / (1.0 - ADAM_B2 ** ADAM_STEP)
        g_ref[...] = g
        d_ref[...] = -ADAM_LR * (m_hat / (jnp.sqrt(v_hat) + ADAM_EPS) + ADAM_WD * w_ref[...])
        mo_ref[...] = mn
        vo_ref[...] = vn

    spec = pl.BlockSpec((tr, C), lambda i: (i, 0))
    oshape = jax.ShapeDtypeStruct((R, C), F32)
    return pl.pallas_call(
        body, grid=(R // tr,), in_specs=[spec] * 3 + [pl.BlockSpec((npart, tr, C), lambda i: (0, i, 0))],
        out_specs=(spec,) * 4, out_shape=(oshape,) * 4, compiler_params=_cp(1), name=name)(w, m, v, parts)


def _adamw_layer(l, w, m, v, parts, prev, *, name):
    L, R, C = w.shape
    npart = parts.shape[0]
    cap = max(SUB, ((1 << 18) // C) // SUB * SUB)
    tr = _tile(R, cap, SUB)
    nprev = 0 if prev is None else 4

    def body(*refs):
        w_ref, m_ref, v_ref, p_ref = refs[:4]
        g_ref, d_ref, mo_ref, vo_ref = refs[4 + nprev:]
        g = p_ref[0]
        for k in range(1, npart):
            g = g + p_ref[k]
        mn = ADAM_B1 * m_ref[...] + (1.0 - ADAM_B1) * g
        vn = ADAM_B2 * v_ref[...] + (1.0 - ADAM_B2) * (g * g)
        m_hat = mn / (1.0 - ADAM_B1 ** ADAM_STEP)
        v_hat = vn / (1.0 - ADAM_B2 ** ADAM_STEP)
        g_ref[...] = g
        d_ref[...] = -ADAM_LR * (m_hat / (jnp.sqrt(v_hat) + ADAM_EPS) + ADAM_WD * w_ref[...])
        mo_ref[...] = mn
        vo_ref[...] = vn

    spec = pl.BlockSpec((None, tr, C), lambda i: (l, i, 0))
    oshape = jax.ShapeDtypeStruct((L, R, C), F32)
    return pl.pallas_call(
        body, grid=(R // tr,),
        in_specs=[spec] * 3 + [pl.BlockSpec((npart, tr, C), lambda i: (0, i, 0))]
        + [pl.BlockSpec(memory_space=pl.ANY)] * nprev,
        out_specs=(spec,) * 4, out_shape=(oshape,) * 4,
        input_output_aliases={4 + k: k for k in range(nprev)},
        compiler_params=_cp(1), name=name)(w, m, v, parts, *(prev or ()))


_HBM = pl.BlockSpec(memory_space=pltpu.HBM)
_SEM = pl.BlockSpec(memory_space=pltpu.SEMAPHORE)
_EFFECT = pltpu.SideEffectType.DATAFLOW_SIDE_EFFECTING


def _peers():
    x, y, c = lax.axis_index("x"), lax.axis_index("y"), lax.axis_index("c")
    out = []
    for k in range(1, NDEV):
        px = 1 - x if k & 4 else x
        py = 1 - y if k & 2 else y
        pc = 1 - c if k & 1 else c
        out.append((k - 1, (px, py, pc), 4 * px + 2 * py + pc))
    return 4 * x + 2 * y + c, out


def _xchg_start(arrays, *, scatter, name):
    n = len(arrays)
    lands = [lax.empty(a.shape if scatter else (NDEV,) + a.shape, a.dtype) for a in arrays]

    def body(*refs):
        ins, lnd = refs[:n], refs[n:2 * n]
        send_sems, recv_sems = refs[2 * n], refs[2 * n + 1]
        token = refs[-1]
        me, peers = _peers()
        for k, dev, peer in peers:
            for i in range(n):
                pltpu.make_async_remote_copy(
                    src_ref=ins[i].at[peer] if scatter else ins[i], dst_ref=lnd[i].at[me],
                    send_sem=send_sems.at[i * (NDEV - 1) + k], recv_sem=recv_sems.at[i * (NDEV - 1) + k],
                    device_id=dev, device_id_type=pl.DeviceIdType.MESH).start()
        token[...] = jnp.zeros_like(token)

    sems = pltpu.SemaphoreType.DMA((n * (NDEV - 1),))
    res = pl.pallas_call(
        body, name=name,
        out_shape=(sems, sems) + tuple(pltpu.HBM(t.shape, t.dtype) for t in list(arrays) + lands)
        + (jax.ShapeDtypeStruct((SUB, LANE), F32),),
        in_specs=[_HBM] * (2 * n), out_specs=(_SEM, _SEM) + (_HBM,) * (2 * n) + (pl.BlockSpec(memory_space=pltpu.VMEM),),
        input_output_aliases={i: 2 + i for i in range(2 * n)},
        compiler_params=pltpu.CompilerParams(has_side_effects=_EFFECT),
    )(*[pltpu.with_memory_space_constraint(t, pltpu.HBM) for t in list(arrays) + lands])
    return dict(send=res[0], recv=res[1], thru=list(res[2:2 + 2 * n]), token=res[-1], scatter=scatter, n=n)


def _xchg_wait(handle, after, *, name):
    n, scatter = handle["n"], handle["scatter"]
    thru = handle["thru"]

    def body(*refs):
        ins, lnd = refs[:n], refs[n:2 * n]
        send_sems, recv_sems = refs[2 * n], refs[2 * n + 1]
        me, peers = _peers()
        for k, dev, peer in peers:
            for i in range(n):
                cp = pltpu.make_async_remote_copy(
                    src_ref=ins[i].at[peer] if scatter else ins[i], dst_ref=lnd[i].at[peer],
                    send_sem=send_sems.at[i * (NDEV - 1) + k], recv_sem=recv_sems.at[i * (NDEV - 1) + k],
                    device_id=dev, device_id_type=pl.DeviceIdType.MESH)
                cp.wait_send()
                cp.wait_recv()

    res = pl.pallas_call(
        body, name=name, out_shape=tuple(pltpu.HBM(t.shape, t.dtype) for t in thru),
        in_specs=[_HBM] * (2 * n) + [_SEM, _SEM, pl.BlockSpec(memory_space=pl.ANY)], out_specs=(_HBM,) * (2 * n),
        input_output_aliases={i: i for i in range(2 * n)},
        compiler_params=pltpu.CompilerParams(has_side_effects=_EFFECT),
    )(*thru, handle["send"], handle["recv"], after)
    return list(res[:n]), list(res[n:])


def _pad_heads(w, nh, hd, axis=-1):
    axis = axis % w.ndim
    shp = w.shape
    w = w.reshape(shp[:axis] + (nh, hd) + shp[axis + 1:])
    pads = [(0, 0)] * w.ndim
    pads[axis + 1] = (0, LANE - hd)
    return jnp.pad(w, pads).reshape(shp[:axis] + (nh * LANE,) + shp[axis + 1:])


def _unpad_heads(w, nh, hd, axis=-1):
    axis = axis % w.ndim
    shp = w.shape
    w = w.reshape(shp[:axis] + (nh, LANE) + shp[axis + 1:])
    w = lax.slice_in_dim(w, 0, hd, axis=axis + 1)
    return w.reshape(shp[:axis] + (nh * hd,) + shp[axis + 1:])


_O_DT = SSD_INNER + SSD_XBC
_O_CQ = _O_DT + SSD_HEADS
_O_CKV = _O_CQ + MLA_QR
_O_KR = _O_CKV + MLA_KVR
_O_SQ = _O_KR + MLA_ROPE
_O_SK = _O_SQ + SWA_HEADS * SWA_HD
_O_SV = _O_SK + SWA_KV * SWA_HD


def _w_in_to_padded(w, axis=-1):
    axis = axis % w.ndim
    cut = lambda a, b: lax.slice_in_dim(w, a, b, axis=axis)
    z, xbc, dt = cut(0, SSD_INNER), cut(SSD_INNER, _O_DT), cut(_O_DT, _O_CQ)
    cq, ckv, kr = cut(_O_CQ, _O_CKV), cut(_O_CKV, _O_KR), cut(_O_KR, _O_SQ)
    sq, sk, sv = cut(_O_SQ, _O_SK), cut(_O_SK, _O_SV), cut(_O_SV, D_IN)
    zeros = lambda n: jnp.zeros(w.shape[:axis] + (n,) + w.shape[axis + 1:], w.dtype)
    return jnp.concatenate([xbc, z, cq, ckv, dt, zeros(ROPE_LANE - SSD_HEADS), kr, zeros(LANE - ROPE_LANE - MLA_ROPE),
                            _pad_heads(sq, SWA_HEADS, SWA_HD, axis), _pad_heads(sk, SWA_KV, SWA_HD, axis),
                            _pad_heads(sv, SWA_KV, SWA_HD, axis)], axis=axis)


def _w_in_from_padded(g, axis=-1):
    axis = axis % g.ndim
    cut = lambda a, b: lax.slice_in_dim(g, a, b, axis=axis)
    xbc, z, cq, ckv = cut(C_XBC, C_Z), cut(C_Z, C_CQ), cut(C_CQ, C_CKV), cut(C_CKV, C_MISC)
    dt, kr = cut(C_MISC, C_MISC + SSD_HEADS), cut(C_MISC + ROPE_LANE, C_MISC + ROPE_LANE + MLA_ROPE)
    sq = _unpad_heads(cut(C_SQ, C_SK), SWA_HEADS, SWA_HD, axis)
    sk = _unpad_heads(cut(C_SK, C_SV), SWA_KV, SWA_HD, axis)
    sv = _unpad_heads(cut(C_SV, D_INP), SWA_KV, SWA_HD, axis)
    return jnp.concatenate([z, xbc, dt, cq, ckv, kr, sq, sk, sv], axis=axis)


def _w_out_to_padded(w):
    a = SSD_INNER
    b = a + MLA_HEADS * MLA_V
    return jnp.concatenate([w[..., :a, :], _pad_heads(w[..., a:b, :], MLA_HEADS, MLA_V, axis=-2),
                            _pad_heads(w[..., b:, :], SWA_HEADS, SWA_HD, axis=-2)], axis=-2)


def _w_out_from_padded(g):
    a = SSD_INNER
    b = a + MLA_HEADS * LANE
    return jnp.concatenate([g[..., :a, :], _unpad_heads(g[..., a:b, :], MLA_HEADS, MLA_V, axis=-2),
                            _unpad_heads(g[..., b:, :], SWA_HEADS, SWA_HD, axis=-2)], axis=-2)


def _w_ukv_to_padded(w):
    w4 = w.reshape(w.shape[:-1] + (MLA_HEADS, MLA_NOPE + MLA_V))
    flat = lambda t: t.reshape(w.shape[:-1] + (MLA_HEADS * t.shape[-1],))
    return jnp.concatenate([_pad_heads(flat(w4[..., :MLA_NOPE]), MLA_HEADS, MLA_NOPE),
                            _pad_heads(flat(w4[..., MLA_NOPE:]), MLA_HEADS, MLA_V)], axis=-1)


def _w_ukv_from_padded(g):
    hw = MLA_HEADS * LANE
    gk = _unpad_heads(g[..., :hw], MLA_HEADS, MLA_NOPE).reshape(g.shape[:-1] + (MLA_HEADS, MLA_NOPE))
    gv = _unpad_heads(g[..., hw:], MLA_HEADS, MLA_V).reshape(g.shape[:-1] + (MLA_HEADS, MLA_V))
    return jnp.concatenate([gk, gv], axis=-1).reshape(g.shape[:-1] + (MLA_HEADS * (MLA_NOPE + MLA_V),))


def _pad_lane(v):
    return jnp.pad(v, [(0, 0)] * (v.ndim - 1) + [(0, LANE - v.shape[-1])])


def _rope_tables(positions):
    inv_freq = ROPE_THETA ** (-jnp.arange(0, MLA_ROPE, 2, dtype=F32) / MLA_ROPE)
    ang = positions.astype(F32).reshape(-1, 1) * inv_freq
    cos, sin = jnp.cos(ang), jnp.sin(ang)
    S = ang.shape[0]
    one, zero = jnp.ones((S, ROPE_LANE), F32), jnp.zeros((S, ROPE_LANE), F32)
    tail1, tail0 = jnp.ones((S, LANE - ROPE_LANE - MLA_ROPE), F32), jnp.zeros((S, LANE - ROPE_LANE - MLA_ROPE), F32)
    z16 = jnp.zeros_like(sin)
    return (jnp.concatenate([one, cos, cos, tail1], axis=1), jnp.concatenate([zero, -sin, z16, tail0], axis=1),
            jnp.concatenate([zero, z16, sin, tail0], axis=1))


def _layer_fwd(l, x_in, f_prev, gate_prev, mod, P, tabs):
    sh1, sc1, g1, sh2, sc2, g2 = [mod[k:k + 1] for k in range(6)]
    tag = f"l{l}_"
    if f_prev is None:
        x0 = x_in
        h1 = _norm_fwd(x0, P["n1g"], sc1, sh1, name=tag + "norm1")
    else:
        x0, h1 = _norm_fwd(x_in, P["n1g"], sc1, sh1, f=f_prev, gate=gate_prev, name=tag + "norm1")
    proj = _mm(h1, P["w_in"], tb=True, name=tag + "proj")
    P.update(P.pop("late")(proj))
    y_ssd, hin = _ssd_fwd(proj, P["ssd_cw"], P["ssd_cb"], P["dtb"], P["alog"], P["dsk"],
                          P["ssd_ng"], name=tag + "ssd")
    q, k, v, vt = _mla_prep_fwd(proj, *tabs, P["gq"], P["gkv"], P["w_uq"], P["w_ukv"], name=tag + "mla_prep")
    o_mla, lse_mla, lset_mla = _attn_fwd(q, k, vt, name=tag + "mla_attn")
    o_swa, lse_swa = _swa_fwd(proj, P["sinks"], name=tag + "swa")
    ycat = jnp.concatenate([y_ssd.astype(_ACT), o_mla.astype(_ACT), o_swa.astype(_ACT)], axis=1)
    y = _mm(ycat, P["w_out"], name=tag + "out")
    x1, h2 = _norm_fwd(x0, P["n2g"], sc2, sh2, f=y, gate=g1, name=tag + "norm2")
    up = _mm(h2, P["w_up"], tb=True, name=tag + "up")
    act = _ffn_act_fwd(up, P["fcw"], P["fcb"], name=tag + "ffn_act")
    f = _mm(act, P["w_down"], name=tag + "down")
    saved = dict(x0=x0, h1=h1, proj=proj, hin=hin, q=q, k=k, v=v, o_mla=o_mla, lse_mla=lse_mla, lset_mla=lset_mla, o_swa=o_swa,
                 lse_swa=lse_swa, ycat=ycat, y=y, x1=x1, h2=h2, up=up, act=act, f=f, mod=mod)
    return x1, f, g2, saved


def _layer_bwd(l, dxo, sv, P, tabs, on_ffn_grads):
    mod = sv["mod"]
    sh1, sc1, g1, sh2, sc2, g2 = [mod[k:k + 1] for k in range(6)]
    tag = f"l{l}_b_"
    G = {}
    df, dg2 = _gate_bwd(dxo, sv["f"], g2, name=tag + "gate2")
    dact = _mm(df, P["w_down"], tb=True, name=tag + "dact")
    G["w_down"] = _mm(sv["act"], df, ta=True, name=tag + "dw_down")
    du = _ffn_act_bwd(sv["up"], dact, P["fcw"], P["fcb"], name=tag + "ffn_act")
    dup, G["fcw"], G["fcb"] = _ffn_conv_bwd(du, sv["up"], P["fcw"], name=tag + "ffn_conv")
    dh2 = _mm(dup, P["w_up"], name=tag + "dh2")
    G["w_up"] = _mm(dup, sv["h2"], ta=True, name=tag + "dw_up")
    token = on_ffn_grads(l, G)
    if token is not None:
        sc2 = sc2 + token
    dx1, G["n2g"], dsc2, dsh2 = _norm_bwd(dh2, sv["x1"], dxo, P["n2g"], sc2, name=tag + "norm2")
    dy, dg1 = _gate_bwd(dx1, sv["y"], g1, name=tag + "gate1")
    dycat = _mm(dy, P["w_out"], tb=True, name=tag + "dycat")
    G["w_out"] = _mm(sv["ycat"], dy, ta=True, name=tag + "dw_out")
    proj = sv["proj"]
    (dpre, dz, dmisc_ssd, G["ssd_cw"], G["ssd_cb"], G["dtb"], G["alog"], G["dsk"], G["ssd_ng"]) = _ssd_bwd(
        proj, dycat, sv["hin"], P["ssd_cw"], P["ssd_cb"], P["dtb"], P["alog"], P["dsk"],
        P["ssd_ng"], name=tag + "ssd")
    att = (sv["q"], sv["k"], sv["v"], sv["o_mla"])
    dq = _attn_bwd_dq(*att, sv["lse_mla"], dycat, name=tag + "mla_dq")
    dk, dv = _attn_bwd_dkv(*att, sv["lset_mla"], dycat, name=tag + "mla_dkv")
    dcq, dckv, dmisc, dqh, dkv, qn, kvn, G["gq"], G["gkv"] = _mla_prep_bwd(
        proj, dq, dk, dv, dmisc_ssd, *tabs, P["gq"], P["gkv"], P["w_uq"], P["w_ukv"], name=tag + "mla_prep")
    G["w_uq"] = _mm(qn, dqh, ta=True, name=tag + "dw_uq")
    G["w_ukv"] = _mm(kvn, dkv, ta=True, name=tag + "dw_ukv")
    dsq, dsk_, dsv_, G["sinks"] = _swa_bwd(proj, sv["o_swa"], sv["lse_swa"], dycat, P["sinks"], name=tag + "swa")
    dproj = jnp.concatenate([dpre, dz, dcq, dckv, dmisc, dsq, dsk_, dsv_], axis=1)
    dh1 = _mm(dproj, P["w_in"], name=tag + "dh1")
    G["w_in"] = _mm(dproj, sv["h1"], ta=True, name=tag + "dw_in")
    dx0, G["n1g"], dsc1, dsh1 = _norm_bwd(dh1, sv["x0"], dx1, P["n1g"], sc1, name=tag + "norm1")
    G["mod"] = jnp.concatenate([dsh1, dsc1, dg1, dsh2, dsc2, dg2], axis=0)
    return dx0, G


def _local_step(x, tgt, mods, get_params, tabs, final_g, on_grads, on_ffn_grads):
    saved, params = [], []
    xin, f, gate = x, None, None
    for l in range(DEPTH):
        params.append(get_params(l, x if f is None else f))
        xin, f, gate, sv = _layer_fwd(l, xin, f, gate, mods[l], params[l], tabs)
        saved.append(sv)
    loss, dx, dfinal = _final_loss(xin, f, gate, final_g, tgt, name="final_loss")
    for l in reversed(range(DEPTH)):
        dx, G = _layer_bwd(l, dx, saved[l], params[l], tabs, on_ffn_grads)
        token = on_grads(l, G)
        if token is not None and l > 0:
            saved[l - 1]["mod"] = saved[l - 1]["mod"] + token
    return loss[0, 0], dx, dfinal


_WEIGHTS = ['ada_w', 'ada_b', 'norm1_g', 'norm2_g', 'w_in', 'ssd_conv_w', 'ssd_conv_b', 'ssd_dt_bias', 'ssd_a_log',
            'ssd_d', 'ssd_norm_g', 'mla_q_norm_g', 'mla_w_uq', 'mla_kv_norm_g', 'mla_w_ukv', 'swa_sinks', 'w_out',
            'ffn_w_up', 'ffn_conv_w', 'ffn_conv_b', 'ffn_w_down', 'final_norm_g']
_INPUTS = ['x', 'c', 'positions'] + _WEIGHTS + ['loss_target'] + ['m_' + n for n in _WEIGHTS] + ['v_' + n for n in _WEIGHTS]
_SMALL = [('ada_b', 'mod'), ('norm1_g', 'n1g'), ('norm2_g', 'n2g'), ('ssd_conv_b', 'ssd_cb'), ('ssd_dt_bias', 'dtb'),
          ('ssd_a_log', 'alog'), ('ssd_d', 'dsk'), ('ssd_norm_g', 'ssd_ng'), ('mla_q_norm_g', 'gq'),
          ('mla_kv_norm_g', 'gkv'), ('swa_sinks', 'sinks'), ('ffn_conv_b', 'fcb')]
_SHARDED = [('w_in', 'w_in', 2), ('ssd_conv_w', 'ssd_cw', 2), ('mla_w_uq', 'w_uq', 2), ('mla_w_ukv', 'w_ukv', 2),
            ('w_out', 'w_out', 1), ('ffn_w_up', 'w_up', 2), ('ffn_conv_w', 'fcw', 2), ('ffn_w_down', 'w_down', 1)]
_SHARDED_NAMES = [n for n, _, _ in _SHARDED]
_TRANSPOSED = ('w_in', 'ffn_w_up')


def _pack_small(per_layer, final):
    parts = []
    for name, _ in _SMALL:
        v = per_layer[name]
        v = v.reshape(DEPTH, -1)
        pad = (-v.shape[1]) % LANE
        parts.append(jnp.pad(v, ((0, 0), (0, pad))).reshape(-1))
    parts.append(final.reshape(-1))
    return jnp.concatenate(parts).reshape(-1, LANE)


def _unpack_small(packed, shapes):
    flat = packed.reshape(-1)
    out, off = {}, 0
    for name, _ in _SMALL:
        n = math.prod(shapes[name][1:])
        npad = n + (-n) % LANE
        out[name] = flat[off:off + DEPTH * npad].reshape(DEPTH, npad)[:, :n].reshape(shapes[name])
        off += DEPTH * npad
    out['final_norm_g'] = flat[off:off + D]
    return out


def _shard_major(g, axis):
    shp = g.shape
    g = g.reshape(shp[:axis] + (NDEV, shp[axis] // NDEV) + shp[axis + 1:])
    return jnp.moveaxis(g, axis, 0)


def _unshard(g, axis):
    g = jnp.moveaxis(g, 0, axis)
    shp = g.shape
    return g.reshape(shp[:axis] + (shp[axis] * shp[axis + 1],) + shp[axis + 2:])


def kernel(x, c, positions, ada_w, ada_b, norm1_g, norm2_g, w_in, ssd_conv_w, ssd_conv_b, ssd_dt_bias, ssd_a_log, ssd_d, ssd_norm_g, mla_q_norm_g, mla_w_uq, mla_kv_norm_g, mla_w_ukv, swa_sinks, w_out, ffn_w_up, ffn_conv_w, ffn_conv_b, ffn_w_down, final_norm_g, loss_target, m_ada_w, m_ada_b, m_norm1_g, m_norm2_g, m_w_in, m_ssd_conv_w, m_ssd_conv_b, m_ssd_dt_bias, m_ssd_a_log, m_ssd_d, m_ssd_norm_g, m_mla_q_norm_g, m_mla_w_uq, m_mla_kv_norm_g, m_mla_w_ukv, m_swa_sinks, m_w_out, m_ffn_w_up, m_ffn_conv_w, m_ffn_conv_b, m_ffn_w_down, m_final_norm_g, v_ada_w, v_ada_b, v_norm1_g, v_norm2_g, v_w_in, v_ssd_conv_w, v_ssd_conv_b, v_ssd_dt_bias, v_ssd_a_log, v_ssd_d, v_ssd_norm_g, v_mla_q_norm_g, v_mla_w_uq, v_mla_kv_norm_g, v_mla_w_ukv, v_swa_sinks, v_w_out, v_ffn_w_up, v_ffn_conv_w, v_ffn_conv_b, v_ffn_w_down, v_final_norm_g):
    a = dict(zip(_INPUTS, (x, c, positions, ada_w, ada_b, norm1_g, norm2_g, w_in, ssd_conv_w, ssd_conv_b, ssd_dt_bias, ssd_a_log, ssd_d, ssd_norm_g, mla_q_norm_g, mla_w_uq, mla_kv_norm_g, mla_w_ukv, swa_sinks, w_out, ffn_w_up, ffn_conv_w, ffn_conv_b, ffn_w_down, final_norm_g, loss_target, m_ada_w, m_ada_b, m_norm1_g, m_norm2_g, m_w_in, m_ssd_conv_w, m_ssd_conv_b, m_ssd_dt_bias, m_ssd_a_log, m_ssd_d, m_ssd_norm_g, m_mla_q_norm_g, m_mla_w_uq, m_mla_kv_norm_g, m_mla_w_ukv, m_swa_sinks, m_w_out, m_ffn_w_up, m_ffn_conv_w, m_ffn_conv_b, m_ffn_w_down, m_final_norm_g, v_ada_w, v_ada_b, v_norm1_g, v_norm2_g, v_w_in, v_ssd_conv_w, v_ssd_conv_b, v_ssd_dt_bias, v_ssd_a_log, v_ssd_d, v_ssd_norm_g, v_mla_q_norm_g, v_mla_w_uq, v_mla_kv_norm_g, v_mla_w_ukv, v_swa_sinks, v_w_out, v_ffn_w_up, v_ffn_conv_w, v_ffn_conv_b, v_ffn_w_down, v_final_norm_g)))
    axes = ("x", "y", "c")
    me = 4 * lax.axis_index("x") + 2 * lax.axis_index("y") + lax.axis_index("c")
    ncol = ada_w.shape[-1]

    c_all = _exchange([c], scatter=False, name="gather_c")[0]
    c_act = _silu_call(c_all.reshape(NDEV, D), name="c_act")
    mod_part = jnp.stack([_mm(c_act, ada_w[l], name=f"mod{l}") for l in range(DEPTH)])
    mod_all = _exchange([mod_part], scatter=False, name="gather_mod")[0]
    mod_mine = lax.dynamic_index_in_dim(mod_all, me, axis=2, keepdims=False)
    mods = (jnp.moveaxis(mod_mine, 0, 1).reshape(DEPTH, 6 * D) + ada_b).reshape(DEPTH, 6, D)
    tabs = _rope_tables(positions)

    mxu_names = ('w_in', 'mla_w_uq', 'mla_w_ukv', 'w_out', 'ffn_w_up', 'ffn_w_down')
    kform = lambda n, t: jnp.swapaxes(t, -1, -2) if n in _TRANSPOSED else t
    shard_of = {n: (key, 1 if n in _TRANSPOSED else ax) for n, key, ax in _SHARDED}
    early_w, late_w = _SHARDED_NAMES[:4], _SHARDED_NAMES[4:]
    mods, raw = lax.optimization_barrier((mods, {n: a[n] for n in _SHARDED_NAMES}))
    own_of = lambda names, l: [kform(n, raw[n][l]).astype(_MXU) if n in mxu_names else raw[n][l] for n in names]
    gathers, prev = [], None
    for l in range(DEPTH):
        gathers.append({})
        for grp, names in (("early", early_w), ("late", late_w)):
            srcs = own_of(names, l)
            if prev is not None:
                srcs, _ = lax.optimization_barrier((srcs, prev))
            gathers[l][grp] = _xchg_start(srcs, scatter=False, name=f"gather_start_{grp}{l}")
            prev = gathers[l][grp]["token"]

    def place_own(landed, mine):
        return [lax.dynamic_update_index_in_dim(t, o, me, 0) for t, o in zip(landed, mine)]

    def gathered(l, grp, names, after):
        mine, landed = _xchg_wait(gathers[l][grp], after, name=f"gather_wait_{grp}{l}")
        return {n: _unshard(g, shard_of[n][1] - 1) for n, g in zip(names, place_own(landed, mine))}

    def get_params(l, after):
        full = gathered(l, "early", early_w, after)
        vec = lambda t: t[l].reshape(1, -1)

        def late(after2):
            rest = gathered(l, "late", late_w, after2)
            return dict(w_out=_w_out_to_padded(rest['w_out']), w_up=rest['ffn_w_up'], w_down=rest['ffn_w_down'],
                        fcw=rest['ffn_conv_w'])

        return dict(
            w_in=_w_in_to_padded(full['w_in'], axis=0), w_uq=_pad_heads(full['mla_w_uq'], MLA_HEADS, MLA_NOPE + MLA_ROPE),
            w_ukv=_w_ukv_to_padded(full['mla_w_ukv']), ssd_cw=full['ssd_conv_w'], late=late,
            ssd_cb=vec(ssd_conv_b), dtb=vec(_pad_lane(ssd_dt_bias)), alog=vec(_pad_lane(ssd_a_log)),
            dsk=vec(_pad_lane(ssd_d)), ssd_ng=vec(ssd_norm_g), gq=vec(mla_q_norm_g), gkv=vec(mla_kv_norm_g),
            sinks=vec(_pad_lane(swa_sinks)), fcb=vec(ffn_conv_b), n1g=vec(norm1_g), n2g=vec(norm2_g))

    unpad = dict(w_in=functools.partial(_w_in_from_padded, axis=0), w_out=_w_out_from_padded, w_ukv=_w_ukv_from_padded,
                 w_uq=lambda g: _unpad_heads(g, MLA_HEADS, MLA_NOPE + MLA_ROPE))
    ffn_w, mixer_w = _SHARDED_NAMES[5:], _SHARDED_NAMES[:5]
    grads = [None] * DEPTH
    scatters = [dict() for _ in range(DEPTH)]

    def send(l, grp, names, G):
        parts = [_shard_major(unpad.get(shard_of[n][0], lambda g: g)(G[shard_of[n][0]]), shard_of[n][1] - 1)
                 for n in names]
        scatters[l][grp] = _xchg_start(parts, scatter=True, name=f"scatter_start_{grp}{l}")
        return scatters[l][grp]["token"][0, 0]

    def on_ffn_grads(l, G):
        return send(l, "ffn", ffn_w, G)

    def on_grads(l, G):
        grads[l] = G
        return send(l, "mixer", mixer_w, G)

    mods = mods + sum(g[grp]["token"][0, 0] for g in gathers for grp in ("early", "late"))
    loss, dx, dfinal = _local_step(x[0], loss_target[0], mods, get_params, tabs, final_norm_g.reshape(1, D),
                                   on_grads, on_ffn_grads)
    loss = lax.psum(loss, axes)

    stack = lambda key: jnp.stack([grads[l][key] for l in range(DEPTH)])
    small_g = {name: stack(key).reshape(DEPTH, -1) for name, key in _SMALL}
    small_parts = _exchange([_pack_small(small_g, dfinal)], scatter=False, name="gather_small")[0]

    out_g, out_d, out_m, out_v = {}, {}, {}, {}
    chain = {name: None for name in _SHARDED_NAMES}
    for l in reversed(range(DEPTH)):
        for grp, names in (("ffn", ffn_w), ("mixer", mixer_w)):
            mine, landed = _xchg_wait(scatters[l][grp], dx, name=f"scatter_wait_{grp}{l}")
            parts = place_own(landed, [lax.dynamic_index_in_dim(t, me, 0, keepdims=False) for t in mine])
            for name, pv in zip(names, parts):
                chain[name] = _adamw_layer(l, kform(name, a[name]), kform(name, a['m_' + name]),
                                           kform(name, a['v_' + name]), pv, chain[name], name=f"adamw_{name}{l}")
    for name in _SHARDED_NAMES:
        out_g[name], out_d[name], out_m[name], out_v[name] = [kform(name, t) for t in chain[name]]

    def update(name, wv, mv, vv, pv):
        shp = wv.shape
        r = lambda t: t.reshape((-1, shp[-1]))
        res = _adamw(r(wv), r(mv), r(vv), pv.reshape((pv.shape[0], -1, shp[-1])), name="adamw_" + name)
        out_g[name], out_d[name], out_m[name], out_v[name] = [t.reshape(shp) for t in res]

    n_ada = DEPTH * 6 * D // LANE
    dmod_all = small_parts[:, :n_ada].reshape(NDEV, DEPTH, 6 * D)
    dmod_mine = lax.dynamic_slice_in_dim(dmod_all, me * ncol, ncol, axis=2)
    g_ada = jnp.stack([_mm(c_act, dmod_mine[:, l], ta=True, name=f"dw_ada{l}") for l in range(DEPTH)])
    update('ada_w', ada_w, m_ada_w, v_ada_w, g_ada[None])
    shapes = {n: a[n].shape for n, _ in _SMALL}
    pk = lambda pre: _pack_small({n: a[pre + n] for n, _ in _SMALL}, a[pre + 'final_norm_g'])
    res = _adamw(pk(''), pk('m_'), pk('v_'), small_parts, name="adamw_small")
    for dst, t in zip((out_g, out_d, out_m, out_v), res):
        dst.update(_unpack_small(t, shapes))

    outs = [loss, dx[None]]
    for dct in (out_g, out_d, out_m, out_v):
        outs += [dct[n] for n in _WEIGHTS]
    return tuple(outs)
```

```python
import functools
import math

import jax
import jax.numpy as jnp
from jax import lax
from jax.experimental import pallas as pl
from jax.experimental.pallas import tpu as pltpu

F32 = jnp.float32
_MXU = jnp.bfloat16
_ACT = jnp.bfloat16
_HI = lax.Precision.HIGHEST
EPS = 1e-6
NDEV = 8
DEPTH = 4
D = 1024
LANE = 128
SUB = 8
VMEM_LIMIT = 56 * 1024 * 1024

SSD_INNER, SSD_STATE, SSD_HEADS, SSD_GROUPS, SSD_CHUNK, SSD_CONV = 512, 128, 8, 2, 128, 4
SSD_XBC = SSD_INNER + 2 * SSD_GROUPS * SSD_STATE
MLA_HEADS, MLA_NOPE, MLA_ROPE, MLA_V, MLA_QR, MLA_KVR = 4, 64, 32, 64, 256, 128
SWA_HEADS, SWA_KV, SWA_HD, WINDOW = 4, 2, 64, 128
D_FF, FFN_CONV = 2816, 3
D_IN = 2472
ROPE_THETA = 10000.0
C_XBC, C_Z, C_CQ, C_CKV, C_MISC, C_SQ, C_SK, C_SV, D_INP = 0, 1024, 1536, 1792, 1920, 2048, 2560, 2816, 3072
ROPE_LANE = 64
D_MIXP = 1536

ADAM_LR, ADAM_B1, ADAM_B2, ADAM_EPS, ADAM_WD, ADAM_STEP = 0.001, 0.9, 0.999, 1e-08, 0.01, 10

TS_ROW = 512
TS_FFN = 256
TQ_ATT = 1024
TS_SWA = 512


def _tile(n, cap, q=LANE):
    best = None
    for t in range(q, min(n, cap) + 1, q):
        if n % t == 0:
            best = t
    return n if best is None else best


def _cp(ngrid):
    return pltpu.CompilerParams(dimension_semantics=("arbitrary",) * ngrid, vmem_limit_bytes=VMEM_LIMIT)


def _dot(a, b):
    return jnp.dot(a.astype(_MXU), b.astype(_MXU), preferred_element_type=F32)


def _dot_nt(a, b):
    return lax.dot_general(a.astype(_MXU), b.astype(_MXU), (((1,), (1,)), ((), ())), preferred_element_type=F32)


def _dot_tn(a, b):
    return jnp.dot(a.T.astype(_MXU), b.astype(_MXU), preferred_element_type=F32)


def _sigmoid(x):
    return 1.0 / (1.0 + jnp.exp(-x))


def _silu(x):
    return x * _sigmoid(x)


def _dsilu(x):
    s = _sigmoid(x)
    return s * (1.0 + x * (1.0 - s))


def _softplus(x):
    u = jnp.exp(-jnp.abs(x))
    w = 1.0 + u
    log1p = jnp.where(w == 1.0, u, jnp.log(w) * u / jnp.where(w == 1.0, 1.0, w - 1.0))
    return jnp.maximum(x, 0.0) + log1p


def _colsum(x):
    return jnp.sum(x, axis=0, keepdims=True)


def _rowsum(x):
    return jnp.sum(x, axis=1, keepdims=True)


def _shift_down(t, halo, j):
    if j == 0:
        return t
    n = t.shape[0]
    rolled = pltpu.roll(t, j, 0)
    row = lax.broadcasted_iota(jnp.int32, (SUB, t.shape[1]), 0)
    first = jnp.where(row < j, pltpu.roll(halo, j, 0), rolled[:SUB])
    return jnp.concatenate([first, rolled[SUB:]], axis=0) if n > SUB else first


def _shift_up(t, halo, j):
    if j == 0:
        return t
    n = t.shape[0]
    rolled = pltpu.roll(t, n - j, 0)
    row = lax.broadcasted_iota(jnp.int32, (SUB, t.shape[1]), 0)
    last = jnp.where(row >= SUB - j, pltpu.roll(halo, SUB - j, 0), rolled[n - SUB:])
    return jnp.concatenate([rolled[:n - SUB], last], axis=0) if n > SUB else last


def _mm(a, b, *, ta=False, tb=False, out_dtype=F32, name):
    if ta:
        K, M = a.shape
    else:
        M, K = a.shape
    if tb:
        N, K2 = b.shape
    else:
        K2, N = b.shape
    assert K == K2, (a.shape, b.shape, ta, tb)
    tm, tn, tk = _tile(M, 1536), _tile(N, 1408), _tile(K, 1536)
    nk = K // tk
    dn = (((0 if ta else 1,), (1 if tb else 0,)), ((), ()))

    def body(a_ref, b_ref, o_ref, acc_ref):
        k = pl.program_id(2)
        part = lax.dot_general(a_ref[...].astype(_MXU), b_ref[...].astype(_MXU), dn, preferred_element_type=F32)

        @pl.when(k == 0)
        def _():
            acc_ref[...] = part

        @pl.when(k > 0)
        def _():
            acc_ref[...] += part

        @pl.when(k == nk - 1)
        def _():
            o_ref[...] = acc_ref[...].astype(out_dtype)

    a_spec = pl.BlockSpec((tk, tm), lambda i, j, k: (k, i)) if ta else pl.BlockSpec((tm, tk), lambda i, j, k: (i, k))
    b_spec = pl.BlockSpec((tn, tk), lambda i, j, k: (j, k)) if tb else pl.BlockSpec((tk, tn), lambda i, j, k: (k, j))
    return pl.pallas_call(
        body, grid=(M // tm, N // tn, nk), in_specs=[a_spec, b_spec],
        out_specs=pl.BlockSpec((tm, tn), lambda i, j, k: (i, j)),
        out_shape=jax.ShapeDtypeStruct((M, N), out_dtype),
        scratch_shapes=[pltpu.VMEM((tm, tn), F32)], compiler_params=_cp(3), name=name)(a, b)


def _row(ts, w, col=0):
    return pl.BlockSpec((ts, w), lambda i: (i, col))


def _vec(w, r=1):
    return pl.BlockSpec((r, w), lambda i: (0, 0))


def _silu_call(x, name):
    def body(x_ref, o_ref):
        o_ref[...] = _silu(x_ref[...])
    return pl.pallas_call(body, out_shape=jax.ShapeDtypeStruct(x.shape, F32), name=name)(x)


def _norm_fwd(x, g, sc, sh, *, f=None, gate=None, name):
    S, dm = x.shape
    ts = _tile(S, TS_ROW, SUB)
    res = f is not None

    def body(*refs):
        if res:
            x_ref, f_ref, gate_ref, g_ref, sc_ref, sh_ref, xo_ref, h_ref = refs
            xv = x_ref[...] + gate_ref[...] * f_ref[...]
            xo_ref[...] = xv
        else:
            x_ref, g_ref, sc_ref, sh_ref, h_ref = refs
            xv = x_ref[...]
        rstd = lax.rsqrt(jnp.mean(xv * xv, axis=-1, keepdims=True) + EPS)
        h_ref[...] = ((xv * rstd) * g_ref[...] * (1.0 + sc_ref[...]) + sh_ref[...]).astype(_ACT)

    ins = [x] + ([f, gate] if res else []) + [g, sc, sh]
    in_specs = [_row(ts, dm)] + ([_row(ts, dm), _vec(dm)] if res else []) + [_vec(dm)] * 3
    h_shape = jax.ShapeDtypeStruct((S, dm), _ACT)
    if res:
        out_shape, out_specs = (jax.ShapeDtypeStruct((S, dm), F32), h_shape), (_row(ts, dm), _row(ts, dm))
    else:
        out_shape, out_specs = h_shape, _row(ts, dm)
    return pl.pallas_call(body, grid=(S // ts,), in_specs=in_specs, out_specs=out_specs, out_shape=out_shape,
                          compiler_params=_cp(1), name=name)(*ins)


def _norm_bwd(dh, x, dres, g, sc, *, name):
    S, dm = x.shape
    ts = _tile(S, TS_ROW, SUB)

    def body(dh_ref, x_ref, dres_ref, g_ref, sc_ref, dx_ref, dg_ref, dsc_ref, dsh_ref):
        i = pl.program_id(0)
        xv = x_ref[...]
        dhv = dh_ref[...]
        rstd = lax.rsqrt(jnp.mean(xv * xv, axis=-1, keepdims=True) + EPS)
        xhat = xv * rstd
        hn = xhat * g_ref[...]
        dhn = dhv * (1.0 + sc_ref[...])
        dxh = dhn * g_ref[...]
        dx_ref[...] = dres_ref[...] + rstd * (dxh - xhat * jnp.mean(dxh * xhat, axis=-1, keepdims=True))

        @pl.when(i == 0)
        def _():
            dg_ref[...] = jnp.zeros_like(dg_ref)
            dsc_ref[...] = jnp.zeros_like(dsc_ref)
            dsh_ref[...] = jnp.zeros_like(dsh_ref)

        dg_ref[...] += _colsum(dhn * xhat)
        dsc_ref[...] += _colsum(dhv * hn)
        dsh_ref[...] += _colsum(dhv)

    vshape = jax.ShapeDtypeStruct((1, dm), F32)
    return pl.pallas_call(
        body, grid=(S // ts,), in_specs=[_row(ts, dm)] * 3 + [_vec(dm)] * 2,
        out_specs=(_row(ts, dm), _vec(dm), _vec(dm), _vec(dm)),
        out_shape=(jax.ShapeDtypeStruct((S, dm), F32), vshape, vshape, vshape),
        compiler_params=_cp(1), name=name)(dh, x, dres, g, sc)


def _gate_bwd(dxo, f, gate, *, name):
    S, dm = f.shape
    ts = _tile(S, TS_ROW, SUB)

    def body(dxo_ref, f_ref, gate_ref, df_ref, dgate_ref):
        i = pl.program_id(0)
        dv = dxo_ref[...]
        df_ref[...] = (gate_ref[...] * dv).astype(_ACT)

        @pl.when(i == 0)
        def _():
            dgate_ref[...] = jnp.zeros_like(dgate_ref)

        dgate_ref[...] += _colsum(dv * f_ref[...])

    return pl.pallas_call(
        body, grid=(S // ts,), in_specs=[_row(ts, dm), _row(ts, dm), _vec(dm)],
        out_specs=(_row(ts, dm), _vec(dm)),
        out_shape=(jax.ShapeDtypeStruct((S, dm), _ACT), jax.ShapeDtypeStruct((1, dm), F32)),
        compiler_params=_cp(1), name=name)(dxo, f, gate)


def _final_loss(x, f, gate, g, tgt, *, name):
    S, dm = x.shape
    ts = _tile(S, TS_ROW, SUB)

    def body(x_ref, f_ref, gate_ref, g_ref, t_ref, loss_ref, dx_ref, dg_ref):
        i = pl.program_id(0)
        xv = x_ref[...] + gate_ref[...] * f_ref[...]
        rstd = lax.rsqrt(jnp.mean(xv * xv, axis=-1, keepdims=True) + EPS)
        xhat = xv * rstd
        err = xhat * g_ref[...] - t_ref[...]
        dy = err * (1.0 / dm)
        dxh = dy * g_ref[...]
        dx_ref[...] = rstd * (dxh - xhat * jnp.mean(dxh * xhat, axis=-1, keepdims=True))

        @pl.when(i == 0)
        def _():
            loss_ref[...] = jnp.zeros_like(loss_ref)
            dg_ref[...] = jnp.zeros_like(dg_ref)

        loss_ref[...] += jnp.full((1, LANE), 0.5 * jnp.sum(jnp.mean(err * err, axis=-1, keepdims=True)), F32)
        dg_ref[...] += _colsum(dy * xhat)

    return pl.pallas_call(
        body, grid=(S // ts,), in_specs=[_row(ts, dm), _row(ts, dm), _vec(dm), _vec(dm), _row(ts, dm)],
        out_specs=(_vec(LANE), _row(ts, dm), _vec(dm)),
        out_shape=(jax.ShapeDtypeStruct((1, LANE), F32), jax.ShapeDtypeStruct((S, dm), F32),
                   jax.ShapeDtypeStruct((1, dm), F32)),
        compiler_params=_cp(1), name=name)(x, f, gate, g, tgt)


def _ffn_conv(t, halo, cw_ref, cb_ref):
    return ((cb_ref[...] + _shift_down(t, halo, 2) * cw_ref[0:1, :]) + _shift_down(t, halo, 1) * cw_ref[1:2, :]) \
        + t * cw_ref[2:3, :]


def _prev_halo_spec(ts, w, col=0):
    return pl.BlockSpec((SUB, w), lambda i: (jnp.maximum(i * (ts // SUB) - 1, 0), col))


def _ffn_act_fwd(up, cw, cb, *, name):
    S, w2 = up.shape
    ff = w2 // 2
    ts = _tile(S, TS_FFN, SUB)

    def body(up_ref, halo_ref, cw_ref, cb_ref, act_ref):
        i = pl.program_id(0)
        t = up_ref[...]
        halo = jnp.where(i > 0, halo_ref[...], 0.0)
        u = _ffn_conv(t, halo, cw_ref, cb_ref)
        act_ref[...] = (_silu(u[:, :ff]) * u[:, ff:]).astype(_ACT)

    return pl.pallas_call(
        body, grid=(S // ts,), in_specs=[_row(ts, w2), _prev_halo_spec(ts, w2), _vec(w2, FFN_CONV), _vec(w2)],
        out_specs=_row(ts, ff), out_shape=jax.ShapeDtypeStruct((S, ff), _ACT),
        compiler_params=_cp(1), name=name)(up, up, cw, cb)


def _ffn_bwd(up, dact, cw, cb, *, name):
    S, w2 = up.shape
    ff = w2 // 2
    ts = _tile(S, TS_FFN, SUB)
    n = S // ts

    def body(up_ref, halo_ref, dact_ref, cw_ref, cb_ref, dup_ref, dcw_ref, dcb_ref, carry_ref):
        i = pl.program_id(0)
        t_idx = n - 1 - i

        @pl.when(i == 0)
        def _():
            carry_ref[...] = jnp.zeros_like(carry_ref)
            dcw_ref[...] = jnp.zeros_like(dcw_ref)
            dcb_ref[...] = jnp.zeros_like(dcb_ref)

        t = up_ref[...]
        halo = jnp.where(t_idx > 0, halo_ref[...], 0.0)
        u = _ffn_conv(t, halo, cw_ref, cb_ref)
        a, b = u[:, :ff], u[:, ff:]
        da = dact_ref[...]
        dv = jnp.concatenate([da * b * _dsilu(a), da * _silu(a)], axis=1)
        nxt = carry_ref[...]
        dup = (dv * cw_ref[2:3, :] + _shift_up(dv, nxt, 1) * cw_ref[1:2, :]) + _shift_up(dv, nxt, 2) * cw_ref[0:1, :]
        dup_ref[...] = dup.astype(_ACT)
        dcb_ref[...] += _colsum(dv)
        dcw_ref[2:3, :] += _colsum(dv * t)
        dcw_ref[1:2, :] += _colsum(dv * _shift_down(t, halo, 1))
        dcw_ref[0:1, :] += _colsum(dv * _shift_down(t, halo, 2))
        carry_ref[...] = dv[:SUB]

    rev = lambda w: pl.BlockSpec((ts, w), lambda i: (n - 1 - i, 0))
    halo_spec = pl.BlockSpec((SUB, w2), lambda i: (jnp.maximum((n - 1 - i) * (ts // SUB) - 1, 0), 0))
    return pl.pallas_call(
        body, grid=(n,), in_specs=[rev(w2), halo_spec, rev(ff), _vec(w2, FFN_CONV), _vec(w2)],
        out_specs=(rev(w2), _vec(w2, FFN_CONV), _vec(w2)),
        out_shape=(jax.ShapeDtypeStruct((S, w2), _ACT), jax.ShapeDtypeStruct((FFN_CONV, w2), F32),
                   jax.ShapeDtypeStruct((1, w2), F32)),
        scratch_shapes=[pltpu.VMEM((SUB, w2), F32)], compiler_params=_cp(1), name=name)(up, up, dact, cw, cb)


def _ssd_core(pre, halo, misc, cw_ref, cb_ref, dtb, alog):
    q = pre.shape[0]
    conv = cb_ref[...]
    for k in range(SSD_CONV):
        conv = conv + _shift_down(pre, halo, SSD_CONV - 1 - k) * cw_ref[k:k + 1, :]
    xbc = _silu(conv)
    raw = misc + dtb
    dt = _softplus(raw)
    a = -jnp.exp(alog)
    r = lax.broadcasted_iota(jnp.int32, (q, q), 0)
    c = lax.broadcasted_iota(jnp.int32, (q, q), 1)
    tri = r >= c
    acum = jnp.dot(tri.astype(F32), dt * a, precision=_HI, preferred_element_type=F32)
    return conv, xbc, raw, dt, a, acum, acum.T, tri


def _sel(v, j, lo):
    return jnp.where(lo, v[:, 2 * j:2 * j + 1], v[:, 2 * j + 1:2 * j + 2])


def _ssd_pair_fwd(xbc, dt, acum, acum_t, tri, dsk, g_mat, b_mat, c_mat, h_pair, j, lo, lo1, sub_lo):
    q = xbc.shape[0]
    x = xbc[:, LANE * j:LANE * (j + 1)]
    dtp = _sel(dt, j, lo)
    ap = _sel(acum, j, lo)
    xd = x * dtp
    ls, ms = [], []
    for h in (2 * j, 2 * j + 1):
        seg = acum[:, h:h + 1] - acum_t[h:h + 1, :]
        l_mat = jnp.exp(jnp.where(tri, seg, -jnp.inf))
        ls.append(l_mat)
        ms.append(g_mat * l_mat)
    yd = jnp.where(lo, _dot(ms[0], xd), _dot(ms[1], xd))
    ea = jnp.exp(ap)
    yo = _dot_nt(c_mat, h_pair) * ea
    dp = _sel(dsk, j, lo1)
    alast = acum[q - 1:q, :]
    e = jnp.exp(_sel(alast, j, lo1) - ap)
    cd = jnp.where(sub_lo, jnp.exp(alast[:, 2 * j:2 * j + 1]), jnp.exp(alast[:, 2 * j + 1:2 * j + 2]))
    return dict(x=x, dtp=dtp, ap=ap, xd=xd, ls=ls, ms=ms, ea=ea, yo=yo, dp=dp, e=e, cd=cd, y=yd + yo + x * dp)


def _gnorm(yg):
    half = SSD_INNER // SSD_GROUPS
    rstds, yns = [], []
    for g in range(SSD_GROUPS):
        part = yg[:, half * g:half * (g + 1)]
        rstd = lax.rsqrt(jnp.mean(part * part, axis=-1, keepdims=True) + EPS)
        rstds.append(rstd)
        yns.append(part * rstd)
    return rstds, yns


def _ssd_specs(nc, rev):
    q = SSD_CHUNK
    cidx = (lambda i: nc - 1 - i) if rev else (lambda i: i)
    return [
        pl.BlockSpec((q, SSD_XBC), lambda i: (cidx(i), C_XBC // SSD_XBC)),
        pl.BlockSpec((SUB, SSD_XBC), lambda i: (jnp.maximum(cidx(i) * (q // SUB) - 1, 0), C_XBC // SSD_XBC)),
        pl.BlockSpec((q, SSD_INNER), lambda i: (cidx(i), C_Z // SSD_INNER)),
        pl.BlockSpec((q, LANE), lambda i: (cidx(i), C_MISC // LANE)),
    ]


def _ssd_param_specs():
    return [_vec(SSD_XBC, SSD_CONV), _vec(SSD_XBC), _vec(LANE), _vec(LANE), _vec(LANE), _vec(SSD_INNER)]


def _ssd_fwd(proj, cw, cb, dtb, alog, dsk, ng, *, name):
    S = proj.shape[0]
    q = SSD_CHUNK
    nc = S // q
    npair = SSD_HEADS // 2

    def body(xbc_ref, halo_ref, z_ref, misc_ref, cw_ref, cb_ref, dtb_ref, alog_ref, dsk_ref, ng_ref,
             y_ref, hin_ref, h_ref):
        c = pl.program_id(0)

        @pl.when(c == 0)
        def _():
            h_ref[...] = jnp.zeros_like(h_ref)

        pre = xbc_ref[...]
        halo = jnp.where(c > 0, halo_ref[...], 0.0)
        conv, xbc, raw, dt, a, acum, acum_t, tri = _ssd_core(pre, halo, misc_ref[...], cw_ref, cb_ref,
                                                             dtb_ref[...], alog_ref[...])
        lo = lax.broadcasted_iota(jnp.int32, (q, LANE), 1) < LANE // 2
        lo1 = lo[:1]
        sub_lo = lax.broadcasted_iota(jnp.int32, (LANE, LANE), 0) < LANE // 2
        ys = []
        for g in range(SSD_GROUPS):
            b_mat = xbc[:, SSD_INNER + SSD_STATE * g:SSD_INNER + SSD_STATE * (g + 1)]
            c_mat = xbc[:, SSD_INNER + SSD_STATE * (SSD_GROUPS + g):SSD_INNER + SSD_STATE * (SSD_GROUPS + g + 1)]
            g_mat = _dot_nt(c_mat, b_mat)
            for jj in range(npair // SSD_GROUPS):
                j = g * (npair // SSD_GROUPS) + jj
                hj = h_ref[j]
                p = _ssd_pair_fwd(xbc, dt, acum, acum_t, tri, dsk_ref[...], g_mat, b_mat, c_mat, hj, j, lo, lo1, sub_lo)
                ys.append(p["y"])
                hin_ref[0, j] = hj
                h_ref[j] = p["cd"] * hj + _dot_tn(p["xd"] * p["e"], b_mat)
        yg = jnp.concatenate(ys, axis=1) * _silu(z_ref[...])
        _, yns = _gnorm(yg)
        y_ref[...] = jnp.concatenate(yns, axis=1) * ng_ref[...]

    return pl.pallas_call(
        body, grid=(nc,), in_specs=_ssd_specs(nc, False) + _ssd_param_specs(),
        out_specs=(pl.BlockSpec((q, SSD_INNER), lambda i: (i, 0)),
                   pl.BlockSpec((1, npair, LANE, LANE), lambda i: (i, 0, 0, 0))),
        out_shape=(jax.ShapeDtypeStruct((S, SSD_INNER), F32), jax.ShapeDtypeStruct((nc, npair, LANE, LANE), F32)),
        scratch_shapes=[pltpu.VMEM((npair, LANE, LANE), F32)], compiler_params=_cp(1), name=name,
    )(proj, proj, proj, proj, cw, cb, dtb, alog, dsk, ng)


def _ssd_bwd(proj, dycat, hin, cw, cb, dtb, alog, dsk, ng, *, name):
    S = proj.shape[0]
    q = SSD_CHUNK
    nc = S // q
    npair = SSD_HEADS // 2
    ppg = npair // SSD_GROUPS

    def body(xbc_ref, halo_ref, z_ref, misc_ref, dy_ref, hin_ref, cw_ref, cb_ref, dtb_ref, alog_ref, dsk_ref, ng_ref,
             dpre_ref, dz_ref, dmisc_ref, dcw_ref, dcb_ref, ddtb_ref, dalog_ref, ddsk_ref, dng_ref,
             dh_ref, carry_ref):
        i = pl.program_id(0)
        c = nc - 1 - i

        @pl.when(i == 0)
        def _():
            dh_ref[...] = jnp.zeros_like(dh_ref)
            carry_ref[...] = jnp.zeros_like(carry_ref)
            for r in (dcw_ref, dcb_ref, ddtb_ref, dalog_ref, ddsk_ref, dng_ref):
                r[...] = jnp.zeros_like(r)

        pre = xbc_ref[...]
        halo = jnp.where(c > 0, halo_ref[...], 0.0)
        conv, xbc, raw, dt, a, acum, acum_t, tri = _ssd_core(pre, halo, misc_ref[...], cw_ref, cb_ref,
                                                             dtb_ref[...], alog_ref[...])
        lane = lax.broadcasted_iota(jnp.int32, (q, LANE), 1)
        lane1 = lane[:1]
        rowi = lax.broadcasted_iota(jnp.int32, (q, LANE), 0)
        lastrow = rowi == q - 1
        lo = lane < LANE // 2
        lo1 = lo[:1]
        sub_lo = lax.broadcasted_iota(jnp.int32, (LANE, LANE), 0) < LANE // 2
        dsk = dsk_ref[...]
        alast = acum[q - 1:q, :]

        def halves(t):
            return _rowsum(jnp.where(lo, t, 0.0)), _rowsum(jnp.where(lo, 0.0, t))

        def put(ha, va, vb):
            ln = lane if va.shape[0] == q else lane1
            return jnp.where(ln == ha, va, 0.0) + jnp.where(ln == ha + 1, vb, 0.0)

        mats, pairs = [], []
        for g in range(SSD_GROUPS):
            b_mat = xbc[:, SSD_INNER + SSD_STATE * g:SSD_INNER + SSD_STATE * (g + 1)]
            c_mat = xbc[:, SSD_INNER + SSD_STATE * (SSD_GROUPS + g):SSD_INNER + SSD_STATE * (SSD_GROUPS + g + 1)]
            g_mat = _dot_nt(c_mat, b_mat)
            mats.append((b_mat, c_mat, g_mat))
            for jj in range(ppg):
                j = g * ppg + jj
                pairs.append(_ssd_pair_fwd(xbc, dt, acum, acum_t, tri, dsk, g_mat, b_mat, c_mat, hin_ref[0, j],
                                           j, lo, lo1, sub_lo))
        z = z_ref[...]
        sz = _silu(z)
        y = jnp.concatenate([p["y"] for p in pairs], axis=1)
        rstds, yns = _gnorm(y * sz)
        dout = dy_ref[...]
        dng_ref[...] += _colsum(dout * jnp.concatenate(yns, axis=1))
        dyn = dout * ng_ref[...]
        half = SSD_INNER // SSD_GROUPS
        dygs = []
        for g in range(SSD_GROUPS):
            dyn_g = dyn[:, half * g:half * (g + 1)]
            dygs.append(rstds[g] * (dyn_g - yns[g] * jnp.mean(dyn_g * yns[g], axis=-1, keepdims=True)))
        dyg = jnp.concatenate(dygs, axis=1)
        dyv = dyg * sz
        dz_ref[...] = (dyg * y * _dsilu(z)).astype(_ACT)

        da_acc = jnp.zeros((q, LANE), F32)
        ddt = jnp.zeros((q, LANE), F32)
        dds = jnp.zeros((1, LANE), F32)
        dxs, dbs, dcs = [], [], []
        for g in range(SSD_GROUPS):
            b_mat, c_mat, g_mat = mats[g]
            dg_mat = jnp.zeros((q, q), F32)
            db = jnp.zeros((q, SSD_STATE), F32)
            dc = jnp.zeros((q, SSD_STATE), F32)
            for jj in range(ppg):
                j = g * ppg + jj
                ha = 2 * j
                p = pairs[j]
                hj = hin_ref[0, j]
                dyp = dyv[:, LANE * j:LANE * (j + 1)]
                dsum = _colsum(dyp * p["x"])
                dds = dds + put(ha, _rowsum(jnp.where(lo1, dsum, 0.0)), _rowsum(jnp.where(lo1, 0.0, dsum)))
                dx = dyp * p["dp"]
                dw = dyp * p["ea"]
                dc = dc + _dot(dw, hj)
                dh_yo = _dot_tn(dw, c_mat)
                ra, rb = halves(dyp * p["yo"])
                da_acc = da_acc + put(ha, ra, rb)
                dxd = jnp.zeros((q, LANE), F32)
                for idx in range(2):
                    dyh = jnp.where(lo, dyp, 0.0) if idx == 0 else jnp.where(lo, 0.0, dyp)
                    dm = _dot_nt(dyh, p["xd"])
                    dxd = dxd + _dot_tn(p["ms"][idx], dyh)
                    dg_mat = dg_mat + dm * p["ls"][idx]
                    t = dm * p["ms"][idx]
                    da_h = _rowsum(t) - _rowsum(t.T)
                    da_acc = da_acc + jnp.where(lane == ha + idx, da_h, 0.0)
                dhn = dh_ref[j]
                s = _rowsum(dhn * hj)
                sa = jnp.sum(jnp.where(sub_lo[:, :1], s, 0.0), keepdims=True)
                sb = jnp.sum(jnp.where(sub_lo[:, :1], 0.0, s), keepdims=True)
                cda, cdb = jnp.exp(alast[:, ha:ha + 1]), jnp.exp(alast[:, ha + 1:ha + 2])
                db = db + _dot(p["xd"] * p["e"], dhn)
                r = _dot_nt(b_mat, dhn)
                dxd = dxd + r * p["e"]
                qa, qb = halves(r * p["xd"] * p["e"])
                da_acc = da_acc - put(ha, qa, qb)
                tot_a = sa * cda + jnp.sum(qa, keepdims=True)
                tot_b = sb * cdb + jnp.sum(qb, keepdims=True)
                da_acc = da_acc + jnp.where(lastrow, put(ha, tot_a, tot_b), 0.0)
                dh_ref[j] = p["cd"] * dhn + dh_yo
                dx = dx + dxd * p["dtp"]
                ua, ub = halves(dxd * p["x"])
                ddt = ddt + put(ha, ua, ub)
                dxs.append(dx)
            dc = dc + _dot(dg_mat, b_mat)
            db = db + _dot_tn(dg_mat, c_mat)
            dbs.append(db)
            dcs.append(dc)
        r2 = lax.broadcasted_iota(jnp.int32, (q, q), 0)
        c2 = lax.broadcasted_iota(jnp.int32, (q, q), 1)
        dda = jnp.dot((c2 >= r2).astype(F32), da_acc, precision=_HI, preferred_element_type=F32)
        ddt = ddt + dda * a
        dalog_ref[...] += _colsum(dda * dt) * a
        ddsk_ref[...] += dds
        draw = jnp.where(lane < SSD_HEADS, ddt * _sigmoid(raw), 0.0)
        ddtb_ref[...] += _colsum(draw)
        dmisc_ref[...] = draw
        dconv = jnp.concatenate(dxs + dbs + dcs, axis=1) * _dsilu(conv)
        dcb_ref[...] += _colsum(dconv)
        nxt = carry_ref[...]
        dpre = jnp.zeros_like(dconv)
        for k in range(SSD_CONV):
            dcw_ref[k:k + 1, :] += _colsum(dconv * _shift_down(pre, halo, SSD_CONV - 1 - k))
            dpre = dpre + _shift_up(dconv, nxt, SSD_CONV - 1 - k) * cw_ref[k:k + 1, :]
        dpre_ref[...] = dpre.astype(_ACT)
        carry_ref[...] = dconv[:SUB]

    rev = lambda i: (nc - 1 - i, 0)
    vshape = lambda w, r=1: jax.ShapeDtypeStruct((r, w), F32)
    return pl.pallas_call(
        body, grid=(nc,),
        in_specs=_ssd_specs(nc, True) + [pl.BlockSpec((q, SSD_INNER), rev),
                                         pl.BlockSpec((1, npair, LANE, LANE), lambda i: (nc - 1 - i, 0, 0, 0))]
        + _ssd_param_specs(),
        out_specs=(pl.BlockSpec((q, SSD_XBC), rev), pl.BlockSpec((q, SSD_INNER), rev), pl.BlockSpec((q, LANE), rev),
                   _vec(SSD_XBC, SSD_CONV), _vec(SSD_XBC), _vec(LANE), _vec(LANE), _vec(LANE), _vec(SSD_INNER)),
        out_shape=(jax.ShapeDtypeStruct((S, SSD_XBC), _ACT), jax.ShapeDtypeStruct((S, SSD_INNER), _ACT),
                   jax.ShapeDtypeStruct((S, LANE), F32),
                   vshape(SSD_XBC, SSD_CONV), vshape(SSD_XBC), vshape(LANE), vshape(LANE), vshape(LANE),
                   vshape(SSD_INNER)),
        scratch_shapes=[pltpu.VMEM((npair, LANE, LANE), F32), pltpu.VMEM((SUB, SSD_XBC), F32)],
        compiler_params=_cp(1), name=name,
    )(proj, proj, proj, proj, dycat, hin, cw, cb, dtb, alog, dsk, ng)


def _rope(x, cosf, sina, sinb):
    return x * cosf + pltpu.roll(x, LANE - MLA_ROPE // 2, 1) * sina + pltpu.roll(x, MLA_ROPE // 2, 1) * sinb


def _rope_t(dy, cosf, sina, sinb):
    return dy * cosf + pltpu.roll(dy * sina, MLA_ROPE // 2, 1) + pltpu.roll(dy * sinb, LANE - MLA_ROPE // 2, 1)


def _rope_lanes(shape):
    lane = lax.broadcasted_iota(jnp.int32, shape, 1)
    return (lane >= ROPE_LANE) & (lane < ROPE_LANE + MLA_ROPE)


def _mla_prep_fwd(proj, cosf, sina, sinb, gq, gkv, wuq, wukv, *, name):
    S = proj.shape[0]
    ts = _tile(S, TS_ROW, SUB)
    hw = MLA_HEADS * LANE

    def body(cq_ref, ckv_ref, misc_ref, cos_ref, sa_ref, sb_ref, gq_ref, gkv_ref, wuq_ref, wukv_ref,
             q_ref, k_ref, v_ref, vt_ref):
        cosv, sav, sbv = cos_ref[...], sa_ref[...], sb_ref[...]
        cq = cq_ref[...]
        qn = cq * lax.rsqrt(jnp.mean(cq * cq, axis=-1, keepdims=True) + EPS) * gq_ref[...]
        qh = _dot(qn, wuq_ref[...])
        ckv = ckv_ref[...]
        kvn = ckv * lax.rsqrt(jnp.mean(ckv * ckv, axis=-1, keepdims=True) + EPS) * gkv_ref[...]
        kv = _dot(kvn, wukv_ref[...])
        kr = _rope(jnp.where(_rope_lanes((ts, LANE)), misc_ref[...], 0.0), cosv, sav, sbv)
        for h in range(MLA_HEADS):
            sl = slice(LANE * h, LANE * (h + 1))
            q_ref[:, sl] = (_rope(qh[:, sl], cosv, sav, sbv) * _Q_SCALE).astype(_ACT)
            k_ref[:, sl] = (kv[:, sl] + kr).astype(_ACT)
        v_ref[...] = kv[:, hw:].astype(_ACT)
        vt_ref[...] = kv[:, hw:].T.astype(_ACT)

    oshape = jax.ShapeDtypeStruct((S, hw), _ACT)
    return pl.pallas_call(
        body, grid=(S // ts,),
        in_specs=[_row(ts, MLA_QR, C_CQ // MLA_QR), _row(ts, MLA_KVR, C_CKV // MLA_KVR), _row(ts, LANE, C_MISC // LANE),
                  _row(ts, LANE), _row(ts, LANE), _row(ts, LANE), _vec(MLA_QR), _vec(MLA_KVR),
                  _vec(hw, MLA_QR), _vec(2 * hw, MLA_KVR)],
        out_specs=(_row(ts, hw),) * 3 + (pl.BlockSpec((hw, ts), lambda i: (0, i)),),
        out_shape=(oshape,) * 3 + (jax.ShapeDtypeStruct((hw, S), _ACT),), compiler_params=_cp(1), name=name,
    )(proj, proj, proj, cosf, sina, sinb, gq, gkv, wuq, wukv)


def _mla_prep_bwd(proj, dq, dk, dv, dmisc_ssd, cosf, sina, sinb, gq, gkv, wuq, wukv, *, name):
    S = proj.shape[0]
    ts = _tile(S, TS_ROW, SUB)
    hw = MLA_HEADS * LANE

    def body(cq_ref, ckv_ref, dq_ref, dk_ref, dv_ref, dms_ref, cos_ref, sa_ref, sb_ref, gq_ref, gkv_ref,
             wuq_ref, wukv_ref, dcq_ref, dckv_ref, dmisc_ref, dqh_ref, dkv_ref, qn_ref, kvn_ref, dgq_ref, dgkv_ref):
        i = pl.program_id(0)
        cosv, sav, sbv = cos_ref[...], sa_ref[...], sb_ref[...]

        @pl.when(i == 0)
        def _():
            dgq_ref[...] = jnp.zeros_like(dgq_ref)
            dgkv_ref[...] = jnp.zeros_like(dgkv_ref)

        dqh = jnp.concatenate([_rope_t(dq_ref[:, LANE * h:LANE * (h + 1)], cosv, sav, sbv)
                               for h in range(MLA_HEADS)], axis=1)
        dqh_ref[...] = dqh.astype(_ACT)
        dkv = jnp.concatenate([dk_ref[...], dv_ref[...]], axis=1)
        dkv_ref[...] = dkv.astype(_ACT)

        def norm_bwd(x, g, dn, dg_ref, n_ref):
            rstd = lax.rsqrt(jnp.mean(x * x, axis=-1, keepdims=True) + EPS)
            xhat = x * rstd
            n_ref[...] = (xhat * g).astype(_ACT)
            dg_ref[...] += _colsum(dn * xhat)
            dxh = dn * g
            return rstd * (dxh - xhat * jnp.mean(dxh * xhat, axis=-1, keepdims=True))

        dcq_ref[...] = norm_bwd(cq_ref[...], gq_ref[...], _dot_nt(dqh, wuq_ref[...]), dgq_ref, qn_ref).astype(_ACT)
        dckv_ref[...] = norm_bwd(ckv_ref[...], gkv_ref[...], _dot_nt(dkv, wukv_ref[...]), dgkv_ref, kvn_ref).astype(_ACT)
        dks = dk_ref[:, 0:LANE]
        for h in range(1, MLA_HEADS):
            dks = dks + dk_ref[:, LANE * h:LANE * (h + 1)]
        rl = _rope_lanes((ts, LANE))
        dkr = _rope_t(jnp.where(rl, dks, 0.0), cosv, sav, sbv)
        dmisc_ref[...] = (dms_ref[...] + jnp.where(rl, dkr, 0.0)).astype(_ACT)

    act = lambda w: jax.ShapeDtypeStruct((S, w), _ACT)
    return pl.pallas_call(
        body, grid=(S // ts,),
        in_specs=[_row(ts, MLA_QR, C_CQ // MLA_QR), _row(ts, MLA_KVR, C_CKV // MLA_KVR),
                  _row(ts, hw), _row(ts, hw), _row(ts, hw), _row(ts, LANE),
                  _row(ts, LANE), _row(ts, LANE), _row(ts, LANE), _vec(MLA_QR), _vec(MLA_KVR),
                  _vec(hw, MLA_QR), _vec(2 * hw, MLA_KVR)],
        out_specs=(_row(ts, MLA_QR), _row(ts, MLA_KVR), _row(ts, LANE), _row(ts, hw), _row(ts, 2 * hw),
                   _row(ts, MLA_QR), _row(ts, MLA_KVR), _vec(MLA_QR), _vec(MLA_KVR)),
        out_shape=(act(MLA_QR), act(MLA_KVR), act(LANE), act(hw), act(2 * hw), act(MLA_QR), act(MLA_KVR),
                   jax.ShapeDtypeStruct((1, MLA_QR), F32), jax.ShapeDtypeStruct((1, MLA_KVR), F32)),
        compiler_params=_cp(1), name=name,
    )(proj, proj, dq, dk, dv, dmisc_ssd, cosf, sina, sinb, gq, gkv, wuq, wukv)


_MLA_SCALE = 1.0 / math.sqrt(MLA_NOPE + MLA_ROPE)
_LOG2E = 1.4426950408889634
_Q_SCALE = _MLA_SCALE * _LOG2E
ATT_CHUNK = 1024


def _tri_grid(nq, by_key):
    if by_key:
        pairs = [(i, j) for j in range(nq) for i in range(j, nq)]
    else:
        pairs = [(i, j) for i in range(nq) for j in range(i + 1)]
    return jnp.asarray([p[0] for p in pairs], jnp.int32), jnp.asarray([p[1] for p in pairs], jnp.int32)


def _attn_fwd(q, k, vt, *, name):
    S = q.shape[0]
    tq = _tile(S, TQ_ATT)
    nq = S // tq
    itab, jtab = _tri_grid(nq, False)

    def body(it_ref, jt_ref, q_ref, k_ref, vt_ref, o_ref, lse_ref, lset_ref, m_ref, l_ref, acc_ref):
        t = pl.program_id(1)
        i, j = it_ref[t], jt_ref[t]

        @pl.when(j == 0)
        def _():
            m_ref[...] = jnp.full_like(m_ref, -jnp.inf)
            l_ref[...] = jnp.zeros_like(l_ref)
            acc_ref[...] = jnp.zeros_like(acc_ref)

        def step(diagonal):
            s = _dot_nt(k_ref[...], q_ref[...])
            if diagonal:
                kk = lax.broadcasted_iota(jnp.int32, (tq, tq), 0)
                s = jnp.where(kk <= lax.broadcasted_iota(jnp.int32, (tq, tq), 1), s, -jnp.inf)
            m_prev = m_ref[...]
            m_new = jnp.maximum(m_prev, jnp.max(s, axis=0, keepdims=True))
            p = jnp.exp2(s - m_new)
            alpha = jnp.exp2(m_prev - m_new)
            l_ref[...] = alpha * l_ref[...] + _colsum(p)
            acc_ref[...] = alpha * acc_ref[...] + _dot(vt_ref[...], p)
            m_ref[...] = m_new

        pl.when(j < i)(functools.partial(step, False))
        pl.when(j == i)(functools.partial(step, True))

        @pl.when(j == i)
        def _():
            o_ref[...] = (acc_ref[...] / l_ref[...]).T
            lse = m_ref[...] + jnp.log2(l_ref[...])
            lset_ref[...] = jnp.broadcast_to(lse, (SUB, tq))
            lse_ref[...] = jnp.broadcast_to(lse, (LANE, tq)).T

    qspec = pl.BlockSpec((tq, LANE), lambda h, t, it, jt: (it[t], h))
    kspec = pl.BlockSpec((tq, LANE), lambda h, t, it, jt: (jt[t], h))
    vtspec = pl.BlockSpec((LANE, tq), lambda h, t, it, jt: (h, jt[t]))
    oshape = jax.ShapeDtypeStruct((S, MLA_HEADS * LANE), F32)
    return pl.pallas_call(
        body,
        grid_spec=pltpu.PrefetchScalarGridSpec(
            num_scalar_prefetch=2, grid=(MLA_HEADS, itab.shape[0]), in_specs=[qspec, kspec, vtspec],
            out_specs=(qspec, qspec, pl.BlockSpec((SUB, tq), lambda h, t, it, jt: (h, it[t]))),
            scratch_shapes=[pltpu.VMEM((1, tq), F32), pltpu.VMEM((1, tq), F32), pltpu.VMEM((LANE, tq), F32)]),
        out_shape=(oshape, oshape, jax.ShapeDtypeStruct((MLA_HEADS * SUB, S), F32)),
        compiler_params=_cp(2), name=name)(itab, jtab, q, k, vt)


def _attn_bwd_dq(q, k, v, o, lse, dycat, *, name):
    S = q.shape[0]
    tq = _tile(S, TQ_ATT)
    nq = S // tq
    rc = min(ATT_CHUNK, tq)
    itab, jtab = _tri_grid(nq, False)

    def body(it_ref, jt_ref, q_ref, k_ref, v_ref, o_ref, lse_ref, do_ref, dq_ref, acc_ref):
        t = pl.program_id(1)
        i, j = it_ref[t], jt_ref[t]

        @pl.when(j == 0)
        def _():
            acc_ref[...] = jnp.zeros_like(acc_ref)

        def step(diagonal):
            kv, vv = k_ref[...], v_ref[...]
            for r in range(tq // rc):
                rows = slice(r * rc, (r + 1) * rc)
                s = _dot_nt(q_ref[rows, :], kv)
                if diagonal:
                    rr = r * rc + lax.broadcasted_iota(jnp.int32, (rc, tq), 0)
                    s = jnp.where(lax.broadcasted_iota(jnp.int32, (rc, tq), 1) <= rr, s, -jnp.inf)
                p = jnp.exp2(s - lse_ref[rows, 0:1])
                dov = do_ref[rows, :]
                delta = _rowsum(dov * o_ref[rows, :])
                ds = p * (_dot_nt(dov, vv) - delta)
                acc_ref[rows, :] += _dot(ds, kv)

        pl.when(j < i)(functools.partial(step, False))
        pl.when(j == i)(functools.partial(step, True))

        @pl.when(j == i)
        def _():
            dq_ref[...] = acc_ref[...] * _MLA_SCALE

    qspec = pl.BlockSpec((tq, LANE), lambda h, t, it, jt: (it[t], h))
    kspec = pl.BlockSpec((tq, LANE), lambda h, t, it, jt: (jt[t], h))
    dospec = pl.BlockSpec((tq, LANE), lambda h, t, it, jt: (it[t], SSD_INNER // LANE + h))
    return pl.pallas_call(
        body,
        grid_spec=pltpu.PrefetchScalarGridSpec(
            num_scalar_prefetch=2, grid=(MLA_HEADS, itab.shape[0]),
            in_specs=[qspec, kspec, kspec, qspec, qspec, dospec], out_specs=qspec,
            scratch_shapes=[pltpu.VMEM((tq, LANE), F32)]),
        out_shape=jax.ShapeDtypeStruct((S, MLA_HEADS * LANE), F32),
        compiler_params=_cp(2), name=name)(itab, jtab, q, k, v, o, lse, dycat)


def _attn_bwd_dkv(q, k, v, o, lset, dycat, *, name):
    S = q.shape[0]
    tq = _tile(S, TQ_ATT)
    nq = S // tq
    kc = min(ATT_CHUNK, tq)
    itab, jtab = _tri_grid(nq, True)

    def body(it_ref, jt_ref, q_ref, k_ref, v_ref, o_ref, lset_ref, do_ref, dk_ref, dv_ref, dk_acc, dv_acc):
        t = pl.program_id(1)
        i, j = it_ref[t], jt_ref[t]

        @pl.when(i == j)
        def _():
            dk_acc[...] = jnp.zeros_like(dk_acc)
            dv_acc[...] = jnp.zeros_like(dv_acc)

        def step(diagonal):
            qv, dov = q_ref[...], do_ref[...]
            delta = lax.dot_general(jnp.ones((SUB, LANE), F32), dov * o_ref[...], (((1,), (1,)), ((), ())),
                                    precision=_HI, preferred_element_type=F32)[0:1]
            lse = lset_ref[0:1, :]
            for c in range(tq // kc):
                rows = slice(c * kc, (c + 1) * kc)
                s = _dot_nt(k_ref[rows, :], qv)
                if diagonal:
                    kk = c * kc + lax.broadcasted_iota(jnp.int32, (kc, tq), 0)
                    s = jnp.where(kk <= lax.broadcasted_iota(jnp.int32, (kc, tq), 1), s, -jnp.inf)
                p = jnp.exp2(s - lse)
                dv_acc[rows, :] += _dot(p, dov)
                ds = p * (_dot_nt(v_ref[rows, :], dov) - delta)
                dk_acc[rows, :] += _dot(ds, qv)

        pl.when(i > j)(functools.partial(step, False))
        pl.when(i == j)(functools.partial(step, True))

        @pl.when(i == nq - 1)
        def _():
            dk_ref[...] = dk_acc[...] * (1.0 / _LOG2E)
            dv_ref[...] = dv_acc[...]

    qspec = pl.BlockSpec((tq, LANE), lambda h, t, it, jt: (it[t], h))
    kspec = pl.BlockSpec((tq, LANE), lambda h, t, it, jt: (jt[t], h))
    dospec = pl.BlockSpec((tq, LANE), lambda h, t, it, jt: (it[t], SSD_INNER // LANE + h))
    lspec = pl.BlockSpec((SUB, tq), lambda h, t, it, jt: (h, it[t]))
    oshape = jax.ShapeDtypeStruct((S, MLA_HEADS * LANE), F32)
    return pl.pallas_call(
        body,
        grid_spec=pltpu.PrefetchScalarGridSpec(
            num_scalar_prefetch=2, grid=(MLA_HEADS, itab.shape[0]),
            in_specs=[qspec, kspec, kspec, qspec, lspec, dospec], out_specs=(kspec, kspec),
            scratch_shapes=[pltpu.VMEM((tq, LANE), F32), pltpu.VMEM((tq, LANE), F32)]),
        out_shape=(oshape, oshape), compiler_params=_cp(2), name=name)(itab, jtab, q, k, v, o, lset, dycat)


_SWA_SCALE = 1.0 / math.sqrt(SWA_HD)
_SWA_KW = SWA_KV * LANE


def _swa_specs(S, ts, rev):
    n = S // ts
    t = (lambda i: n - 1 - i) if rev else (lambda i: i)
    hb = lambda i: jnp.maximum(t(i) * (ts // WINDOW) - 1, 0)
    return [
        pl.BlockSpec((ts, SWA_HEADS * LANE), lambda i: (t(i), C_SQ // (SWA_HEADS * LANE))),
        pl.BlockSpec((ts, _SWA_KW), lambda i: (t(i), C_SK // _SWA_KW)),
        pl.BlockSpec((WINDOW, _SWA_KW), lambda i: (hb(i), C_SK // _SWA_KW)),
        pl.BlockSpec((ts, _SWA_KW), lambda i: (t(i), C_SV // _SWA_KW)),
        pl.BlockSpec((WINDOW, _SWA_KW), lambda i: (hb(i), C_SV // _SWA_KW)),
    ]


def _swa_scores(qh, kk, t, b, ts):
    s = _dot_nt(qh, kk) * _SWA_SCALE
    row = lax.broadcasted_iota(jnp.int32, (WINDOW, 2 * WINDOW), 0)
    col = lax.broadcasted_iota(jnp.int32, (WINDOW, 2 * WINDOW), 1)
    rel = WINDOW + row - col
    kpos = t * ts + (b - 1) * WINDOW + col
    return jnp.where((rel >= 0) & (rel < WINDOW) & (kpos >= 0), s, -jnp.inf)


def _swa_fwd(proj, sinks, *, name):
    S = proj.shape[0]
    ts = _tile(S, TS_SWA)
    nb = ts // WINDOW

    def body(q_ref, k_ref, kh_ref, v_ref, vh_ref, sink_ref, o_ref, lse_ref):
        t = pl.program_id(0)
        kext = jnp.concatenate([kh_ref[...], k_ref[...]], axis=0)
        vext = jnp.concatenate([vh_ref[...], v_ref[...]], axis=0)
        for b in range(nb):
            rows = slice(WINDOW * b, WINDOW * (b + 1))
            for h in range(SWA_HEADS):
                kvl = slice(LANE * (h // (SWA_HEADS // SWA_KV)), LANE * (h // (SWA_HEADS // SWA_KV) + 1))
                hl = slice(LANE * h, LANE * (h + 1))
                kk = kext[WINDOW * b:WINDOW * (b + 2), kvl]
                vv = vext[WINDOW * b:WINDOW * (b + 2), kvl]
                s = _swa_scores(q_ref[rows, hl], kk, t, b, ts)
                sk = sink_ref[:, h:h + 1]
                m = jnp.maximum(jnp.max(s, axis=1, keepdims=True), sk)
                p = jnp.exp(s - m)
                den = _rowsum(p) + jnp.exp(sk - m)
                o_ref[rows, hl] = _dot(p, vv) / den
                lse_ref[rows, hl] = jnp.broadcast_to(m + jnp.log(den), (WINDOW, LANE))

    oshape = jax.ShapeDtypeStruct((S, SWA_HEADS * LANE), F32)
    ospec = pl.BlockSpec((ts, SWA_HEADS * LANE), lambda i: (i, 0))
    return pl.pallas_call(
        body, grid=(S // ts,), in_specs=_swa_specs(S, ts, False) + [_vec(LANE)], out_specs=(ospec, ospec),
        out_shape=(oshape, oshape), compiler_params=_cp(1), name=name)(proj, proj, proj, proj, proj, sinks)


def _swa_bwd(proj, o, lse, dycat, sinks, *, name):
    S = proj.shape[0]
    ts = _tile(S, TS_SWA)
    nb = ts // WINDOW
    n = S // ts
    grp = SWA_HEADS // SWA_KV

    def body(q_ref, k_ref, kh_ref, v_ref, vh_ref, o_ref, lse_ref, do_ref, sink_ref,
             dq_ref, dk_ref, dv_ref, dsink_ref, dk_carry, dv_carry):
        i = pl.program_id(0)
        t = n - 1 - i

        @pl.when(i == 0)
        def _():
            dk_carry[...] = jnp.zeros_like(dk_carry)
            dv_carry[...] = jnp.zeros_like(dv_carry)
            dsink_ref[...] = jnp.zeros_like(dsink_ref)

        kext = jnp.concatenate([kh_ref[...], k_ref[...]], axis=0)
        vext = jnp.concatenate([vh_ref[...], v_ref[...]], axis=0)
        lane1 = lax.broadcasted_iota(jnp.int32, (1, LANE), 1)
        dkb = [[jnp.zeros((WINDOW, LANE), F32) for _ in range(SWA_KV)] for _ in range(nb + 1)]
        dvb = [[jnp.zeros((WINDOW, LANE), F32) for _ in range(SWA_KV)] for _ in range(nb + 1)]
        dsink = jnp.zeros((1, LANE), F32)
        for b in range(nb):
            rows = slice(WINDOW * b, WINDOW * (b + 1))
            for h in range(SWA_HEADS):
                kvh = h // grp
                kvl = slice(LANE * kvh, LANE * (kvh + 1))
                hl = slice(LANE * h, LANE * (h + 1))
                kk = kext[WINDOW * b:WINDOW * (b + 2), kvl]
                vv = vext[WINDOW * b:WINDOW * (b + 2), kvl]
                qh = q_ref[rows, hl]
                lse_h = lse_ref[rows, LANE * h:LANE * h + 1]
                p = jnp.exp(_swa_scores(qh, kk, t, b, ts) - lse_h)
                doh = do_ref[rows, hl]
                delta = _rowsum(doh * o_ref[rows, hl])
                ds = p * (_dot_nt(doh, vv) - delta)
                sk = sink_ref[:, h:h + 1]
                dsink = dsink + jnp.where(lane1 == h, -jnp.sum(jnp.exp(sk - lse_h) * delta, keepdims=True), 0.0)
                dq_ref[rows, hl] = (_dot(ds, kk) * _SWA_SCALE).astype(_ACT)
                dkk = _dot_tn(ds, qh) * _SWA_SCALE
                dvv = _dot_tn(p, doh)
                dkb[b][kvh] = dkb[b][kvh] + dkk[:WINDOW]
                dkb[b + 1][kvh] = dkb[b + 1][kvh] + dkk[WINDOW:]
                dvb[b][kvh] = dvb[b][kvh] + dvv[:WINDOW]
                dvb[b + 1][kvh] = dvb[b + 1][kvh] + dvv[WINDOW:]
        dsink_ref[...] += dsink
        for dref, blocks, carry in ((dk_ref, dkb, dk_carry), (dv_ref, dvb, dv_carry)):
            old = carry[...]
            for b in range(1, nb + 1):
                blk = jnp.concatenate(blocks[b], axis=1)
                if b == nb:
                    blk = blk + old
                dref[WINDOW * (b - 1):WINDOW * b, :] = blk.astype(_ACT)
            carry[...] = jnp.concatenate(blocks[0], axis=1)

    hw = SWA_HEADS * LANE
    rev = lambda i: (n - 1 - i, 0)
    mix = lambda i: (n - 1 - i, (SSD_INNER + MLA_HEADS * LANE) // hw)
    return pl.pallas_call(
        body, grid=(n,),
        in_specs=_swa_specs(S, ts, True) + [pl.BlockSpec((ts, hw), rev), pl.BlockSpec((ts, hw), rev),
                                            pl.BlockSpec((ts, hw), mix), _vec(LANE)],
        out_specs=(pl.BlockSpec((ts, hw), rev), pl.BlockSpec((ts, _SWA_KW), rev), pl.BlockSpec((ts, _SWA_KW), rev),
                   _vec(LANE)),
        out_shape=(jax.ShapeDtypeStruct((S, hw), _ACT), jax.ShapeDtypeStruct((S, _SWA_KW), _ACT),
                   jax.ShapeDtypeStruct((S, _SWA_KW), _ACT), jax.ShapeDtypeStruct((1, LANE), F32)),
        scratch_shapes=[pltpu.VMEM((WINDOW, _SWA_KW), F32), pltpu.VMEM((WINDOW, _SWA_KW), F32)],
        compiler_params=_cp(1), name=name)(proj, proj, proj, proj, proj, o, lse, dycat, sinks)


def _exchange(arrays, *, scatter, name):
    n = len(arrays)

    def body(*refs):
        ins, outs = refs[:n], refs[n:2 * n]
        send_sems, recv_sems, loc_sems = refs[2 * n:]
        x, y, c = lax.axis_index("x"), lax.axis_index("y"), lax.axis_index("c")
        me = 4 * x + 2 * y + c

        def src(i, dest):
            return ins[i].at[dest] if scatter else ins[i]

        local = [pltpu.make_async_copy(src(i, me), outs[i].at[me], loc_sems.at[i]) for i in range(n)]
        for cp in local:
            cp.start()
        sends, recvs = [], []
        for k in range(1, NDEV):
            px = 1 - x if k & 4 else x
            py = 1 - y if k & 2 else y
            pc = 1 - c if k & 1 else c
            peer = 4 * px + 2 * py + pc
            for i in range(n):
                common = dict(send_sem=send_sems.at[i, k - 1], recv_sem=recv_sems.at[i, k - 1],
                              device_id=(px, py, pc), device_id_type=pl.DeviceIdType.MESH)
                sends.append(pltpu.make_async_remote_copy(src_ref=src(i, peer), dst_ref=outs[i].at[me], **common))
                recvs.append(pltpu.make_async_remote_copy(src_ref=src(i, peer), dst_ref=outs[i].at[peer], **common))
        for cp in sends:
            cp.start()
        for cp in recvs:
            cp.wait_recv()
        for cp in sends:
            cp.wait_send()
        for cp in local:
            cp.wait()

    hbm = pl.BlockSpec(memory_space=pl.ANY)
    out_shape = tuple(jax.ShapeDtypeStruct(a.shape if scatter else (NDEV,) + a.shape, a.dtype) for a in arrays)
    return pl.pallas_call(
        body, in_specs=[hbm] * n, out_specs=tuple([hbm] * n), out_shape=out_shape,
        scratch_shapes=[pltpu.SemaphoreType.DMA((n, NDEV - 1)), pltpu.SemaphoreType.DMA((n, NDEV - 1)),
                        pltpu.SemaphoreType.DMA((n,))],
        name=name)(*arrays)


def _adamw(w, m, v, parts, *, name):
    R, C = w.shape
    npart = parts.shape[0]
    cap = max(SUB, ((1 << 18) // C) // SUB * SUB)
    tr = _tile(R, cap, SUB)

    def body(w_ref, m_ref, v_ref, p_ref, g_ref, d_ref, mo_ref, vo_ref):
        g = p_ref[0]
        for k in range(1, npart):
            g = g + p_ref[k]
        mn = ADAM_B1 * m_ref[...] + (1.0 - ADAM_B1) * g
        vn = ADAM_B2 * v_ref[...] + (1.0 - ADAM_B2) * (g * g)
        m_hat = mn / (1.0 - ADAM_B1 ** ADAM_STEP)
        v_hat = vn / (1.0 - ADAM_B2 ** ADAM_STEP)
        g_ref[...] = g
        d_ref[...] = -ADAM_LR * (m_hat / (jnp.sqrt(v_hat) + ADAM_EPS) + ADAM_WD * w_ref[...])
        mo_ref[...] = mn
        vo_ref[...] = vn

    spec = pl.BlockSpec((tr, C), lambda i: (i, 0))
    oshape = jax.ShapeDtypeStruct((R, C), F32)
    return pl.pallas_call(
        body, grid=(R // tr,), in_specs=[spec] * 3 + [pl.BlockSpec((npart, tr, C), lambda i: (0, i, 0))],
        out_specs=(spec,) * 4, out_shape=(oshape,) * 4, compiler_params=_cp(1), name=name)(w, m, v, parts)


def _adamw_layer(l, w, m, v, parts, prev, *, name):
    L, R, C = w.shape
    npart = parts.shape[0]
    cap = max(SUB, ((1 << 18) // C) // SUB * SUB)
    tr = _tile(R, cap, SUB)
    nprev = 0 if prev is None else 4

    def body(*refs):
        w_ref, m_ref, v_ref, p_ref = refs[:4]
        g_ref, d_ref, mo_ref, vo_ref = refs[4 + nprev:]
        g = p_ref[0]
        for k in range(1, npart):
            g = g + p_ref[k]
        mn = ADAM_B1 * m_ref[...] + (1.0 - ADAM_B1) * g
        vn = ADAM_B2 * v_ref[...] + (1.0 - ADAM_B2) * (g * g)
        m_hat = mn / (1.0 - ADAM_B1 ** ADAM_STEP)
        v_hat = vn / (1.0 - ADAM_B2 ** ADAM_STEP)
        g_ref[...] = g
        d_ref[...] = -ADAM_LR * (m_hat / (jnp.sqrt(v_hat) + ADAM_EPS) + ADAM_WD * w_ref[...])
        mo_ref[...] = mn
        vo_ref[...] = vn

    spec = pl.BlockSpec((None, tr, C), lambda i: (l, i, 0))
    oshape = jax.ShapeDtypeStruct((L, R, C), F32)
    return pl.pallas_call(
        body, grid=(R // tr,),
        in_specs=[spec] * 3 + [pl.BlockSpec((npart, tr, C), lambda i: (0, i, 0))]
        + [pl.BlockSpec(memory_space=pl.ANY)] * nprev,
        out_specs=(spec,) * 4, out_shape=(oshape,) * 4,
        input_output_aliases={4 + k: k for k in range(nprev)},
        compiler_params=_cp(1), name=name)(w, m, v, parts, *(prev or ()))


_HBM = pl.BlockSpec(memory_space=pltpu.HBM)
_SEM = pl.BlockSpec(memory_space=pltpu.SEMAPHORE)
_EFFECT = pltpu.SideEffectType.DATAFLOW_SIDE_EFFECTING


def _peers():
    x, y, c = lax.axis_index("x"), lax.axis_index("y"), lax.axis_index("c")
    out = []
    for k in range(1, NDEV):
        px = 1 - x if k & 4 else x
        py = 1 - y if k & 2 else y
        pc = 1 - c if k & 1 else c
        out.append((k - 1, (px, py, pc), 4 * px + 2 * py + pc))
    return 4 * x + 2 * y + c, out


def _xchg_start(arrays, *, scatter, name):
    n = len(arrays)
    lands = [lax.empty(a.shape if scatter else (NDEV,) + a.shape, a.dtype) for a in arrays]

    def body(*refs):
        ins, lnd = refs[:n], refs[n:2 * n]
        send_sems, recv_sems = refs[2 * n], refs[2 * n + 1]
        token = refs[-1]
        me, peers = _peers()
        for k, dev, peer in peers:
            for i in range(n):
                pltpu.make_async_remote_copy(
                    src_ref=ins[i].at[peer] if scatter else ins[i], dst_ref=lnd[i].at[me],
                    send_sem=send_sems.at[i * (NDEV - 1) + k], recv_sem=recv_sems.at[i * (NDEV - 1) + k],
                    device_id=dev, device_id_type=pl.DeviceIdType.MESH).start()
        token[...] = jnp.zeros_like(token)

    sems = pltpu.SemaphoreType.DMA((n * (NDEV - 1),))
    res = pl.pallas_call(
        body, name=name,
        out_shape=(sems, sems) + tuple(pltpu.HBM(t.shape, t.dtype) for t in list(arrays) + lands)
        + (jax.ShapeDtypeStruct((SUB, LANE), F32),),
        in_specs=[_HBM] * (2 * n), out_specs=(_SEM, _SEM) + (_HBM,) * (2 * n) + (pl.BlockSpec(memory_space=pltpu.VMEM),),
        input_output_aliases={i: 2 + i for i in range(2 * n)},
        compiler_params=pltpu.CompilerParams(has_side_effects=_EFFECT),
    )(*[pltpu.with_memory_space_constraint(t, pltpu.HBM) for t in list(arrays) + lands])
    return dict(send=res[0], recv=res[1], thru=list(res[2:2 + 2 * n]), token=res[-1], scatter=scatter, n=n)


def _xchg_wait(handle, after, *, name):
    n, scatter = handle["n"], handle["scatter"]
    thru = handle["thru"]

    def body(*refs):
        ins, lnd = refs[:n], refs[n:2 * n]
        send_sems, recv_sems = refs[2 * n], refs[2 * n + 1]
        me, peers = _peers()
        for k, dev, peer in peers:
            for i in range(n):
                cp = pltpu.make_async_remote_copy(
                    src_ref=ins[i].at[peer] if scatter else ins[i], dst_ref=lnd[i].at[peer],
                    send_sem=send_sems.at[i * (NDEV - 1) + k], recv_sem=recv_sems.at[i * (NDEV - 1) + k],
                    device_id=dev, device_id_type=pl.DeviceIdType.MESH)
                cp.wait_send()
                cp.wait_recv()

    res = pl.pallas_call(
        body, name=name, out_shape=tuple(pltpu.HBM(t.shape, t.dtype) for t in thru),
        in_specs=[_HBM] * (2 * n) + [_SEM, _SEM, pl.BlockSpec(memory_space=pl.ANY)], out_specs=(_HBM,) * (2 * n),
        input_output_aliases={i: i for i in range(2 * n)},
        compiler_params=pltpu.CompilerParams(has_side_effects=_EFFECT),
    )(*thru, handle["send"], handle["recv"], after)
    return list(res[:n]), list(res[n:])


def _pad_heads(w, nh, hd, axis=-1):
    axis = axis % w.ndim
    shp = w.shape
    w = w.reshape(shp[:axis] + (nh, hd) + shp[axis + 1:])
    pads = [(0, 0)] * w.ndim
    pads[axis + 1] = (0, LANE - hd)
    return jnp.pad(w, pads).reshape(shp[:axis] + (nh * LANE,) + shp[axis + 1:])


def _unpad_heads(w, nh, hd, axis=-1):
    axis = axis % w.ndim
    shp = w.shape
    w = w.reshape(shp[:axis] + (nh, LANE) + shp[axis + 1:])
    w = lax.slice_in_dim(w, 0, hd, axis=axis + 1)
    return w.reshape(shp[:axis] + (nh * hd,) + shp[axis + 1:])


_O_DT = SSD_INNER + SSD_XBC
_O_CQ = _O_DT + SSD_HEADS
_O_CKV = _O_CQ + MLA_QR
_O_KR = _O_CKV + MLA_KVR
_O_SQ = _O_KR + MLA_ROPE
_O_SK = _O_SQ + SWA_HEADS * SWA_HD
_O_SV = _O_SK + SWA_KV * SWA_HD


def _w_in_to_padded(w, axis=-1):
    axis = axis % w.ndim
    cut = lambda a, b: lax.slice_in_dim(w, a, b, axis=axis)
    z, xbc, dt = cut(0, SSD_INNER), cut(SSD_INNER, _O_DT), cut(_O_DT, _O_CQ)
    cq, ckv, kr = cut(_O_CQ, _O_CKV), cut(_O_CKV, _O_KR), cut(_O_KR, _O_SQ)
    sq, sk, sv = cut(_O_SQ, _O_SK), cut(_O_SK, _O_SV), cut(_O_SV, D_IN)
    zeros = lambda n: jnp.zeros(w.shape[:axis] + (n,) + w.shape[axis + 1:], w.dtype)
    return jnp.concatenate([xbc, z, cq, ckv, dt, zeros(ROPE_LANE - SSD_HEADS), kr, zeros(LANE - ROPE_LANE - MLA_ROPE),
                            _pad_heads(sq, SWA_HEADS, SWA_HD, axis), _pad_heads(sk, SWA_KV, SWA_HD, axis),
                            _pad_heads(sv, SWA_KV, SWA_HD, axis)], axis=axis)


def _w_in_from_padded(g, axis=-1):
    axis = axis % g.ndim
    cut = lambda a, b: lax.slice_in_dim(g, a, b, axis=axis)
    xbc, z, cq, ckv = cut(C_XBC, C_Z), cut(C_Z, C_CQ), cut(C_CQ, C_CKV), cut(C_CKV, C_MISC)
    dt, kr = cut(C_MISC, C_MISC + SSD_HEADS), cut(C_MISC + ROPE_LANE, C_MISC + ROPE_LANE + MLA_ROPE)
    sq = _unpad_heads(cut(C_SQ, C_SK), SWA_HEADS, SWA_HD, axis)
    sk = _unpad_heads(cut(C_SK, C_SV), SWA_KV, SWA_HD, axis)
    sv = _unpad_heads(cut(C_SV, D_INP), SWA_KV, SWA_HD, axis)
    return jnp.concatenate([z, xbc, dt, cq, ckv, kr, sq, sk, sv], axis=axis)


def _w_out_to_padded(w):
    a = SSD_INNER
    b = a + MLA_HEADS * MLA_V
    return jnp.concatenate([w[..., :a, :], _pad_heads(w[..., a:b, :], MLA_HEADS, MLA_V, axis=-2),
                            _pad_heads(w[..., b:, :], SWA_HEADS, SWA_HD, axis=-2)], axis=-2)


def _w_out_from_padded(g):
    a = SSD_INNER
    b = a + MLA_HEADS * LANE
    return jnp.concatenate([g[..., :a, :], _unpad_heads(g[..., a:b, :], MLA_HEADS, MLA_V, axis=-2),
                            _unpad_heads(g[..., b:, :], SWA_HEADS, SWA_HD, axis=-2)], axis=-2)


def _w_ukv_to_padded(w):
    w4 = w.reshape(w.shape[:-1] + (MLA_HEADS, MLA_NOPE + MLA_V))
    flat = lambda t: t.reshape(w.shape[:-1] + (MLA_HEADS * t.shape[-1],))
    return jnp.concatenate([_pad_heads(flat(w4[..., :MLA_NOPE]), MLA_HEADS, MLA_NOPE),
                            _pad_heads(flat(w4[..., MLA_NOPE:]), MLA_HEADS, MLA_V)], axis=-1)


def _w_ukv_from_padded(g):
    hw = MLA_HEADS * LANE
    gk = _unpad_heads(g[..., :hw], MLA_HEADS, MLA_NOPE).reshape(g.shape[:-1] + (MLA_HEADS, MLA_NOPE))
    gv = _unpad_heads(g[..., hw:], MLA_HEADS, MLA_V).reshape(g.shape[:-1] + (MLA_HEADS, MLA_V))
    return jnp.concatenate([gk, gv], axis=-1).reshape(g.shape[:-1] + (MLA_HEADS * (MLA_NOPE + MLA_V),))


def _pad_lane(v):
    return jnp.pad(v, [(0, 0)] * (v.ndim - 1) + [(0, LANE - v.shape[-1])])


def _rope_tables(positions):
    inv_freq = ROPE_THETA ** (-jnp.arange(0, MLA_ROPE, 2, dtype=F32) / MLA_ROPE)
    ang = positions.astype(F32).reshape(-1, 1) * inv_freq
    cos, sin = jnp.cos(ang), jnp.sin(ang)
    S = ang.shape[0]
    one, zero = jnp.ones((S, ROPE_LANE), F32), jnp.zeros((S, ROPE_LANE), F32)
    tail1, tail0 = jnp.ones((S, LANE - ROPE_LANE - MLA_ROPE), F32), jnp.zeros((S, LANE - ROPE_LANE - MLA_ROPE), F32)
    z16 = jnp.zeros_like(sin)
    return (jnp.concatenate([one, cos, cos, tail1], axis=1), jnp.concatenate([zero, -sin, z16, tail0], axis=1),
            jnp.concatenate([zero, z16, sin, tail0], axis=1))


def _layer_fwd(l, x_in, f_prev, gate_prev, mod, P, tabs):
    sh1, sc1, g1, sh2, sc2, g2 = [mod[k:k + 1] for k in range(6)]
    tag = f"l{l}_"
    if f_prev is None:
        x0 = x_in
        h1 = _norm_fwd(x0, P["n1g"], sc1, sh1, name=tag + "norm1")
    else:
        x0, h1 = _norm_fwd(x_in, P["n1g"], sc1, sh1, f=f_prev, gate=gate_prev, name=tag + "norm1")
    proj = _mm(h1, P["w_in"], tb=True, name=tag + "proj")
    P.update(P.pop("mid")(proj))
    y_ssd, hin = _ssd_fwd(proj, P["ssd_cw"], P["ssd_cb"], P["dtb"], P["alog"], P["dsk"],
                          P["ssd_ng"], name=tag + "ssd")
    q, k, v, vt = _mla_prep_fwd(proj, *tabs, P["gq"], P["gkv"], P["w_uq"], P["w_ukv"], name=tag + "mla_prep")
    o_mla, lse_mla, lset_mla = _attn_fwd(q, k, vt, name=tag + "mla_attn")
    o_swa, lse_swa = _swa_fwd(proj, P["sinks"], name=tag + "swa")
    ycat = jnp.concatenate([y_ssd.astype(_ACT), o_mla.astype(_ACT), o_swa.astype(_ACT)], axis=1)
    y = _mm(ycat, P["w_out"], name=tag + "out")
    P.update(P.pop("late")(y))
    x1, h2 = _norm_fwd(x0, P["n2g"], sc2, sh2, f=y, gate=g1, name=tag + "norm2")
    up = _mm(h2, P["w_up"], tb=True, name=tag + "up")
    act = _ffn_act_fwd(up, P["fcw"], P["fcb"], name=tag + "ffn_act")
    f = _mm(act, P["w_down"], name=tag + "down")
    saved = dict(x0=x0, h1=h1, proj=proj, hin=hin, q=q, k=k, v=v, o_mla=o_mla, lse_mla=lse_mla, lset_mla=lset_mla, o_swa=o_swa,
                 lse_swa=lse_swa, ycat=ycat, y=y, x1=x1, h2=h2, up=up, act=act, f=f, mod=mod)
    return x1, f, g2, saved


def _layer_bwd(l, dxo, sv, P, tabs, on_part):
    mod = sv["mod"]
    sh1, sc1, g1, sh2, sc2, g2 = [mod[k:k + 1] for k in range(6)]
    tag = f"l{l}_b_"
    G = {}
    df, dg2 = _gate_bwd(dxo, sv["f"], g2, name=tag + "gate2")
    dact = _mm(df, P["w_down"], tb=True, name=tag + "dact")
    G["w_down"] = _mm(sv["act"], df, ta=True, name=tag + "dw_down")
    dup, G["fcw"], G["fcb"] = _ffn_bwd(sv["up"], dact, P["fcw"], P["fcb"], name=tag + "ffn")
    dh2 = _mm(dup, P["w_up"], name=tag + "dh2")
    G["w_up"] = _mm(dup, sv["h2"], ta=True, name=tag + "dw_up")
    token = on_part(l, "ffn", G)
    if token is not None:
        sc2 = sc2 + token
    dx1, G["n2g"], dsc2, dsh2 = _norm_bwd(dh2, sv["x1"], dxo, P["n2g"], sc2, name=tag + "norm2")
    dy, dg1 = _gate_bwd(dx1, sv["y"], g1, name=tag + "gate1")
    dycat = _mm(dy, P["w_out"], tb=True, name=tag + "dycat")
    G["w_out"] = _mm(sv["ycat"], dy, ta=True, name=tag + "dw_out")
    token = on_part(l, "out", G)
    ssd_cb = P["ssd_cb"] if token is None else P["ssd_cb"] + token
    proj = sv["proj"]
    (dpre, dz, dmisc_ssd, G["ssd_cw"], G["ssd_cb"], G["dtb"], G["alog"], G["dsk"], G["ssd_ng"]) = _ssd_bwd(
        proj, dycat, sv["hin"], P["ssd_cw"], ssd_cb, P["dtb"], P["alog"], P["dsk"],
        P["ssd_ng"], name=tag + "ssd")
    att = (sv["q"], sv["k"], sv["v"], sv["o_mla"])
    dq = _attn_bwd_dq(*att, sv["lse_mla"], dycat, name=tag + "mla_dq")
    dk, dv = _attn_bwd_dkv(*att, sv["lset_mla"], dycat, name=tag + "mla_dkv")
    dcq, dckv, dmisc, dqh, dkv, qn, kvn, G["gq"], G["gkv"] = _mla_prep_bwd(
        proj, dq, dk, dv, dmisc_ssd, *tabs, P["gq"], P["gkv"], P["w_uq"], P["w_ukv"], name=tag + "mla_prep")
    G["w_uq"] = _mm(qn, dqh, ta=True, name=tag + "dw_uq")
    G["w_ukv"] = _mm(kvn, dkv, ta=True, name=tag + "dw_ukv")
    dsq, dsk_, dsv_, G["sinks"] = _swa_bwd(proj, sv["o_swa"], sv["lse_swa"], dycat, P["sinks"], name=tag + "swa")
    dproj = jnp.concatenate([dpre, dz, dcq, dckv, dmisc, dsq, dsk_, dsv_], axis=1)
    dh1 = _mm(dproj, P["w_in"], name=tag + "dh1")
    G["w_in"] = _mm(dproj, sv["h1"], ta=True, name=tag + "dw_in")
    dx0, G["n1g"], dsc1, dsh1 = _norm_bwd(dh1, sv["x0"], dx1, P["n1g"], sc1, name=tag + "norm1")
    G["mod"] = jnp.concatenate([dsh1, dsc1, dg1, dsh2, dsc2, dg2], axis=0)
    return dx0, G


def _local_step(x, tgt, mods, get_params, tabs, final_g, on_grads, on_part):
    saved, params = [], []
    xin, f, gate = x, None, None
    for l in range(DEPTH):
        params.append(get_params(l, x if f is None else f))
        xin, f, gate, sv = _layer_fwd(l, xin, f, gate, mods[l], params[l], tabs)
        saved.append(sv)
    loss, dx, dfinal = _final_loss(xin, f, gate, final_g, tgt, name="final_loss")
    for l in reversed(range(DEPTH)):
        dx, G = _layer_bwd(l, dx, saved[l], params[l], tabs, on_part)
        token = on_grads(l, G)
        if token is not None and l > 0:
            saved[l - 1]["mod"] = saved[l - 1]["mod"] + token
    return loss[0, 0], dx, dfinal


_WEIGHTS = ['ada_w', 'ada_b', 'norm1_g', 'norm2_g', 'w_in', 'ssd_conv_w', 'ssd_conv_b', 'ssd_dt_bias', 'ssd_a_log',
            'ssd_d', 'ssd_norm_g', 'mla_q_norm_g', 'mla_w_uq', 'mla_kv_norm_g', 'mla_w_ukv', 'swa_sinks', 'w_out',
            'ffn_w_up', 'ffn_conv_w', 'ffn_conv_b', 'ffn_w_down', 'final_norm_g']
_INPUTS = ['x', 'c', 'positions'] + _WEIGHTS + ['loss_target'] + ['m_' + n for n in _WEIGHTS] + ['v_' + n for n in _WEIGHTS]
_SMALL = [('ada_b', 'mod'), ('norm1_g', 'n1g'), ('norm2_g', 'n2g'), ('ssd_conv_b', 'ssd_cb'), ('ssd_dt_bias', 'dtb'),
          ('ssd_a_log', 'alog'), ('ssd_d', 'dsk'), ('ssd_norm_g', 'ssd_ng'), ('mla_q_norm_g', 'gq'),
          ('mla_kv_norm_g', 'gkv'), ('swa_sinks', 'sinks'), ('ffn_conv_b', 'fcb')]
_SHARDED = [('w_in', 'w_in', 2), ('ssd_conv_w', 'ssd_cw', 2), ('mla_w_uq', 'w_uq', 2), ('mla_w_ukv', 'w_ukv', 2),
            ('w_out', 'w_out', 1), ('ffn_w_up', 'w_up', 2), ('ffn_conv_w', 'fcw', 2), ('ffn_w_down', 'w_down', 1)]
_SHARDED_NAMES = [n for n, _, _ in _SHARDED]
_TRANSPOSED = ('w_in', 'ffn_w_up')


def _pack_small(per_layer, final):
    parts = []
    for name, _ in _SMALL:
        v = per_layer[name]
        v = v.reshape(DEPTH, -1)
        pad = (-v.shape[1]) % LANE
        parts.append(jnp.pad(v, ((0, 0), (0, pad))).reshape(-1))
    parts.append(final.reshape(-1))
    return jnp.concatenate(parts).reshape(-1, LANE)


def _unpack_small(packed, shapes):
    flat = packed.reshape(-1)
    out, off = {}, 0
    for name, _ in _SMALL:
        n = math.prod(shapes[name][1:])
        npad = n + (-n) % LANE
        out[name] = flat[off:off + DEPTH * npad].reshape(DEPTH, npad)[:, :n].reshape(shapes[name])
        off += DEPTH * npad
    out['final_norm_g'] = flat[off:off + D]
    return out


def _shard_major(g, axis):
    shp = g.shape
    g = g.reshape(shp[:axis] + (NDEV, shp[axis] // NDEV) + shp[axis + 1:])
    return jnp.moveaxis(g, axis, 0)


def _unshard(g, axis):
    g = jnp.moveaxis(g, 0, axis)
    shp = g.shape
    return g.reshape(shp[:axis] + (shp[axis] * shp[axis + 1],) + shp[axis + 2:])


def kernel(x, c, positions, ada_w, ada_b, norm1_g, norm2_g, w_in, ssd_conv_w, ssd_conv_b, ssd_dt_bias, ssd_a_log, ssd_d, ssd_norm_g, mla_q_norm_g, mla_w_uq, mla_kv_norm_g, mla_w_ukv, swa_sinks, w_out, ffn_w_up, ffn_conv_w, ffn_conv_b, ffn_w_down, final_norm_g, loss_target, m_ada_w, m_ada_b, m_norm1_g, m_norm2_g, m_w_in, m_ssd_conv_w, m_ssd_conv_b, m_ssd_dt_bias, m_ssd_a_log, m_ssd_d, m_ssd_norm_g, m_mla_q_norm_g, m_mla_w_uq, m_mla_kv_norm_g, m_mla_w_ukv, m_swa_sinks, m_w_out, m_ffn_w_up, m_ffn_conv_w, m_ffn_conv_b, m_ffn_w_down, m_final_norm_g, v_ada_w, v_ada_b, v_norm1_g, v_norm2_g, v_w_in, v_ssd_conv_w, v_ssd_conv_b, v_ssd_dt_bias, v_ssd_a_log, v_ssd_d, v_ssd_norm_g, v_mla_q_norm_g, v_mla_w_uq, v_mla_kv_norm_g, v_mla_w_ukv, v_swa_sinks, v_w_out, v_ffn_w_up, v_ffn_conv_w, v_ffn_conv_b, v_ffn_w_down, v_final_norm_g):
    a = dict(zip(_INPUTS, (x, c, positions, ada_w, ada_b, norm1_g, norm2_g, w_in, ssd_conv_w, ssd_conv_b, ssd_dt_bias, ssd_a_log, ssd_d, ssd_norm_g, mla_q_norm_g, mla_w_uq, mla_kv_norm_g, mla_w_ukv, swa_sinks, w_out, ffn_w_up, ffn_conv_w, ffn_conv_b, ffn_w_down, final_norm_g, loss_target, m_ada_w, m_ada_b, m_norm1_g, m_norm2_g, m_w_in, m_ssd_conv_w, m_ssd_conv_b, m_ssd_dt_bias, m_ssd_a_log, m_ssd_d, m_ssd_norm_g, m_mla_q_norm_g, m_mla_w_uq, m_mla_kv_norm_g, m_mla_w_ukv, m_swa_sinks, m_w_out, m_ffn_w_up, m_ffn_conv_w, m_ffn_conv_b, m_ffn_w_down, m_final_norm_g, v_ada_w, v_ada_b, v_norm1_g, v_norm2_g, v_w_in, v_ssd_conv_w, v_ssd_conv_b, v_ssd_dt_bias, v_ssd_a_log, v_ssd_d, v_ssd_norm_g, v_mla_q_norm_g, v_mla_w_uq, v_mla_kv_norm_g, v_mla_w_ukv, v_swa_sinks, v_w_out, v_ffn_w_up, v_ffn_conv_w, v_ffn_conv_b, v_ffn_w_down, v_final_norm_g)))
    axes = ("x", "y", "c")
    me = 4 * lax.axis_index("x") + 2 * lax.axis_index("y") + lax.axis_index("c")
    ncol = ada_w.shape[-1]

    c_all = _exchange([c], scatter=False, name="gather_c")[0]
    c_act = _silu_call(c_all.reshape(NDEV, D), name="c_act")
    mod_part = jnp.stack([_mm(c_act, ada_w[l], name=f"mod{l}") for l in range(DEPTH)])
    mod_all = _exchange([mod_part], scatter=False, name="gather_mod")[0]
    mod_mine = lax.dynamic_index_in_dim(mod_all, me, axis=2, keepdims=False)
    mods = (jnp.moveaxis(mod_mine, 0, 1).reshape(DEPTH, 6 * D) + ada_b).reshape(DEPTH, 6, D)
    tabs = _rope_tables(positions)

    mxu_names = ('w_in', 'mla_w_uq', 'mla_w_ukv', 'w_out', 'ffn_w_up', 'ffn_w_down')
    kform = lambda n, t: jnp.swapaxes(t, -1, -2) if n in _TRANSPOSED else t
    shard_of = {n: (key, 1 if n in _TRANSPOSED else ax) for n, key, ax in _SHARDED}
    gather_groups = (("early", _SHARDED_NAMES[:4]), ("mid", _SHARDED_NAMES[4:5]), ("late", _SHARDED_NAMES[5:]))
    mods, raw = lax.optimization_barrier((mods, {n: a[n] for n in _SHARDED_NAMES}))
    own_of = lambda names, l: [kform(n, raw[n][l]).astype(_MXU) if n in mxu_names else raw[n][l] for n in names]
    gathers, prev = [], None
    for l in range(DEPTH):
        gathers.append({})
        for grp, names in gather_groups:
            srcs = own_of(names, l)
            if prev is not None:
                srcs, _ = lax.optimization_barrier((srcs, prev))
            gathers[l][grp] = _xchg_start(srcs, scatter=False, name=f"gather_start_{grp}{l}")
            prev = gathers[l][grp]["token"]

    def place_own(landed, mine):
        return [lax.dynamic_update_index_in_dim(t, o, me, 0) for t, o in zip(landed, mine)]

    def gathered(l, grp, after):
        names = dict(gather_groups)[grp]
        mine, landed = _xchg_wait(gathers[l][grp], after, name=f"gather_wait_{grp}{l}")
        return {n: _unshard(g, shard_of[n][1] - 1) for n, g in zip(names, place_own(landed, mine))}

    def get_params(l, after):
        full = gathered(l, "early", after)
        vec = lambda t: t[l].reshape(1, -1)

        def mid(after2):
            return dict(w_out=_w_out_to_padded(gathered(l, "mid", after2)['w_out']))

        def late(after2):
            rest = gathered(l, "late", after2)
            return dict(w_up=rest['ffn_w_up'], w_down=rest['ffn_w_down'], fcw=rest['ffn_conv_w'])

        return dict(
            w_in=_w_in_to_padded(full['w_in'], axis=0), w_uq=_pad_heads(full['mla_w_uq'], MLA_HEADS, MLA_NOPE + MLA_ROPE),
            w_ukv=_w_ukv_to_padded(full['mla_w_ukv']), ssd_cw=full['ssd_conv_w'], mid=mid, late=late,
            ssd_cb=vec(ssd_conv_b), dtb=vec(_pad_lane(ssd_dt_bias)), alog=vec(_pad_lane(ssd_a_log)),
            dsk=vec(_pad_lane(ssd_d)), ssd_ng=vec(ssd_norm_g), gq=vec(mla_q_norm_g), gkv=vec(mla_kv_norm_g),
            sinks=vec(_pad_lane(swa_sinks)), fcb=vec(ffn_conv_b), n1g=vec(norm1_g), n2g=vec(norm2_g))

    unpad = dict(w_in=functools.partial(_w_in_from_padded, axis=0), w_out=_w_out_from_padded, w_ukv=_w_ukv_from_padded,
                 w_uq=lambda g: _unpad_heads(g, MLA_HEADS, MLA_NOPE + MLA_ROPE))
    scatter_groups = (("ffn", _SHARDED_NAMES[5:]), ("out", _SHARDED_NAMES[4:5]), ("mixer", _SHARDED_NAMES[:4]))
    grads = [None] * DEPTH
    scatters = [dict() for _ in range(DEPTH)]

    def on_part(l, grp, G):
        parts = [_shard_major(unpad.get(shard_of[n][0], lambda g: g)(G[shard_of[n][0]]), shard_of[n][1] - 1)
                 for n in dict(scatter_groups)[grp]]
        scatters[l][grp] = _xchg_start(parts, scatter=True, name=f"scatter_start_{grp}{l}")
        return scatters[l][grp]["token"][0, 0]

    def on_grads(l, G):
        grads[l] = G
        return on_part(l, "mixer", G)

    mods = mods + sum(g[grp]["token"][0, 0] for g in gathers for grp, _ in gather_groups)
    loss, dx, dfinal = _local_step(x[0], loss_target[0], mods, get_params, tabs, final_norm_g.reshape(1, D),
                                   on_grads, on_part)
    loss = lax.psum(loss, axes)

    stack = lambda key: jnp.stack([grads[l][key] for l in range(DEPTH)])
    small_g = {name: stack(key).reshape(DEPTH, -1) for name, key in _SMALL}
    small_parts = _exchange([_pack_small(small_g, dfinal)], scatter=False, name="gather_small")[0]

    out_g, out_d, out_m, out_v = {}, {}, {}, {}
    chain = {name: None for name in _SHARDED_NAMES}
    for l in reversed(range(DEPTH)):
        for grp, names in scatter_groups:
            mine, landed = _xchg_wait(scatters[l][grp], dx, name=f"scatter_wait_{grp}{l}")
            parts = place_own(landed, [lax.dynamic_index_in_dim(t, me, 0, keepdims=False) for t in mine])
            for name, pv in zip(names, parts):
                chain[name] = _adamw_layer(l, kform(name, a[name]), kform(name, a['m_' + name]),
                                           kform(name, a['v_' + name]), pv, chain[name], name=f"adamw_{name}{l}")
    for name in _SHARDED_NAMES:
        out_g[name], out_d[name], out_m[name], out_v[name] = [kform(name, t) for t in chain[name]]

    def update(name, wv, mv, vv, pv):
        shp = wv.shape
        r = lambda t: t.reshape((-1, shp[-1]))
        res = _adamw(r(wv), r(mv), r(vv), pv.reshape((pv.shape[0], -1, shp[-1])), name="adamw_" + name)
        out_g[name], out_d[name], out_m[name], out_v[name] = [t.reshape(shp) for t in res]

    n_ada = DEPTH * 6 * D // LANE
    dmod_all = small_parts[:, :n_ada].reshape(NDEV, DEPTH, 6 * D)
    dmod_mine = lax.dynamic_slice_in_dim(dmod_all, me * ncol, ncol, axis=2)
    g_ada = jnp.stack([_mm(c_act, dmod_mine[:, l], ta=True, name=f"dw_ada{l}") for l in range(DEPTH)])
    update('ada_w', ada_w, m_ada_w, v_ada_w, g_ada[None])
    shapes = {n: a[n].shape for n, _ in _SMALL}
    pk = lambda pre: _pack_small({n: a[pre + n] for n, _ in _SMALL}, a[pre + 'final_norm_g'])
    res = _adamw(pk(''), pk('m_'), pk('v_'), small_parts, name="adamw_small")
    for dst, t in zip((out_g, out_d, out_m, out_v), res):
        dst.update(_unpack_small(t, shapes))

    outs = [loss, dx[None]]
    for dct in (out_g, out_d, out_m, out_v):
        outs += [dct[n] for n in _WEIGHTS]
    return tuple(outs)
```

```python
import functools
import math

import jax
import jax.numpy as jnp
from jax import lax
from jax.experimental import pallas as pl
from jax.experimental.pallas import tpu as pltpu

F32 = jnp.float32
_MXU = jnp.bfloat16
_ACT = jnp.bfloat16
_HI = lax.Precision.HIGHEST
EPS = 1e-6
NDEV = 8
DEPTH = 4
D = 1024
LANE = 128
SUB = 8
VMEM_LIMIT = 56 * 1024 * 1024

SSD_INNER, SSD_STATE, SSD_HEADS, SSD_GROUPS, SSD_CHUNK, SSD_CONV = 512, 128, 8, 2, 128, 4
SSD_XBC = SSD_INNER + 2 * SSD_GROUPS * SSD_STATE
MLA_HEADS, MLA_NOPE, MLA_ROPE, MLA_V, MLA_QR, MLA_KVR = 4, 64, 32, 64, 256, 128
SWA_HEADS, SWA_KV, SWA_HD, WINDOW = 4, 2, 64, 128
D_FF, FFN_CONV = 2816, 3
D_IN = 2472
ROPE_THETA = 10000.0
C_XBC, C_Z, C_CQ, C_CKV, C_MISC, C_SQ, C_SK, C_SV, D_INP = 0, 1024, 1536, 1792, 1920, 2048, 2560, 2816, 3072
ROPE_LANE = 64
D_MIXP = 1536

ADAM_LR, ADAM_B1, ADAM_B2, ADAM_EPS, ADAM_WD, ADAM_STEP = 0.001, 0.9, 0.999, 1e-08, 0.01, 10

TS_ROW = 512
TS_FFN = 256
TQ_ATT = 1024
TS_SWA = 512


def _tile(n, cap, q=LANE):
    best = None
    for t in range(q, min(n, cap) + 1, q):
        if n % t == 0:
            best = t
    return n if best is None else best


def _cp(ngrid):
    return pltpu.CompilerParams(dimension_semantics=("arbitrary",) * ngrid, vmem_limit_bytes=VMEM_LIMIT)


def _dot(a, b):
    return jnp.dot(a.astype(_MXU), b.astype(_MXU), preferred_element_type=F32)


def _dot_nt(a, b):
    return lax.dot_general(a.astype(_MXU), b.astype(_MXU), (((1,), (1,)), ((), ())), preferred_element_type=F32)


def _dot_tn(a, b):
    return jnp.dot(a.T.astype(_MXU), b.astype(_MXU), preferred_element_type=F32)


def _sigmoid(x):
    return 1.0 / (1.0 + jnp.exp(-x))


def _sigmoid_t(x):
    return 0.5 * jnp.tanh(0.5 * x) + 0.5


def _silu(x):
    return x * _sigmoid_t(x)


def _silu_grad(x):
    s = _sigmoid_t(x)
    return x * s, s * (1.0 + x * (1.0 - s))


def _dsilu(x):
    return _silu_grad(x)[1]


def _softplus(x):
    u = jnp.exp(-jnp.abs(x))
    w = 1.0 + u
    log1p = jnp.where(w == 1.0, u, jnp.log(w) * u / jnp.where(w == 1.0, 1.0, w - 1.0))
    return jnp.maximum(x, 0.0) + log1p


def _colsum(x):
    return jnp.sum(x, axis=0, keepdims=True)


def _rowsum(x):
    return jnp.sum(x, axis=1, keepdims=True)


def _shift_down(t, halo, j):
    if j == 0:
        return t
    n = t.shape[0]
    rolled = pltpu.roll(t, j, 0)
    row = lax.broadcasted_iota(jnp.int32, (SUB, t.shape[1]), 0)
    first = jnp.where(row < j, pltpu.roll(halo, j, 0), rolled[:SUB])
    return jnp.concatenate([first, rolled[SUB:]], axis=0) if n > SUB else first


def _shift_up(t, halo, j):
    if j == 0:
        return t
    n = t.shape[0]
    rolled = pltpu.roll(t, n - j, 0)
    row = lax.broadcasted_iota(jnp.int32, (SUB, t.shape[1]), 0)
    last = jnp.where(row >= SUB - j, pltpu.roll(halo, SUB - j, 0), rolled[n - SUB:])
    return jnp.concatenate([rolled[:n - SUB], last], axis=0) if n > SUB else last


def _mm(a, b, *, ta=False, tb=False, out_dtype=F32, name):
    if ta:
        K, M = a.shape
    else:
        M, K = a.shape
    if tb:
        N, K2 = b.shape
    else:
        K2, N = b.shape
    assert K == K2, (a.shape, b.shape, ta, tb)
    tm, tn, tk = _tile(M, 1536), _tile(N, 1408), _tile(K, 1536)
    nk = K // tk
    dn = (((0 if ta else 1,), (1 if tb else 0,)), ((), ()))

    def body(a_ref, b_ref, o_ref, *acc):
        part = lax.dot_general(a_ref[...].astype(_MXU), b_ref[...].astype(_MXU), dn, preferred_element_type=F32)
        if nk == 1:
            o_ref[...] = part.astype(out_dtype)
            return
        acc_ref, = acc
        k = pl.program_id(2)

        @pl.when(k == 0)
        def _():
            acc_ref[...] = part

        @pl.when(k > 0)
        def _():
            acc_ref[...] += part

        @pl.when(k == nk - 1)
        def _():
            o_ref[...] = acc_ref[...].astype(out_dtype)

    a_spec = pl.BlockSpec((tk, tm), lambda i, j, k: (k, i)) if ta else pl.BlockSpec((tm, tk), lambda i, j, k: (i, k))
    b_spec = pl.BlockSpec((tn, tk), lambda i, j, k: (j, k)) if tb else pl.BlockSpec((tk, tn), lambda i, j, k: (k, j))
    return pl.pallas_call(
        body, grid=(M // tm, N // tn, nk), in_specs=[a_spec, b_spec],
        out_specs=pl.BlockSpec((tm, tn), lambda i, j, k: (i, j)),
        out_shape=jax.ShapeDtypeStruct((M, N), out_dtype),
        scratch_shapes=[pltpu.VMEM((tm, tn), F32)] * (nk > 1), compiler_params=_cp(3), name=name)(a, b)


def _row(ts, w, col=0):
    return pl.BlockSpec((ts, w), lambda i: (i, col))


def _vec(w, r=1):
    return pl.BlockSpec((r, w), lambda i: (0, 0))


def _silu_call(x, name):
    def body(x_ref, o_ref):
        o_ref[...] = _silu(x_ref[...])
    return pl.pallas_call(body, out_shape=jax.ShapeDtypeStruct(x.shape, F32), name=name)(x)


def _norm_fwd(x, g, sc, sh, *, f=None, gate=None, name):
    S, dm = x.shape
    ts = _tile(S, TS_ROW, SUB)
    res = f is not None

    def body(*refs):
        if res:
            x_ref, f_ref, gate_ref, g_ref, sc_ref, sh_ref, xo_ref, h_ref = refs
            xv = x_ref[...] + gate_ref[...] * f_ref[...]
            xo_ref[...] = xv
        else:
            x_ref, g_ref, sc_ref, sh_ref, h_ref = refs
            xv = x_ref[...]
        rstd = lax.rsqrt(jnp.mean(xv * xv, axis=-1, keepdims=True) + EPS)
        h_ref[...] = ((xv * rstd) * g_ref[...] * (1.0 + sc_ref[...]) + sh_ref[...]).astype(_ACT)

    ins = [x] + ([f, gate] if res else []) + [g, sc, sh]
    in_specs = [_row(ts, dm)] + ([_row(ts, dm), _vec(dm)] if res else []) + [_vec(dm)] * 3
    h_shape = jax.ShapeDtypeStruct((S, dm), _ACT)
    if res:
        out_shape, out_specs = (jax.ShapeDtypeStruct((S, dm), F32), h_shape), (_row(ts, dm), _row(ts, dm))
    else:
        out_shape, out_specs = h_shape, _row(ts, dm)
    return pl.pallas_call(body, grid=(S // ts,), in_specs=in_specs, out_specs=out_specs, out_shape=out_shape,
                          compiler_params=_cp(1), name=name)(*ins)


def _norm_bwd(dh, x, dres, g, sc, *, name):
    S, dm = x.shape
    ts = _tile(S, TS_ROW, SUB)

    def body(dh_ref, x_ref, dres_ref, g_ref, sc_ref, dx_ref, dg_ref, dsc_ref, dsh_ref):
        i = pl.program_id(0)
        xv = x_ref[...]
        dhv = dh_ref[...]
        rstd = lax.rsqrt(jnp.mean(xv * xv, axis=-1, keepdims=True) + EPS)
        xhat = xv * rstd
        hn = xhat * g_ref[...]
        dhn = dhv * (1.0 + sc_ref[...])
        dxh = dhn * g_ref[...]
        dx_ref[...] = dres_ref[...] + rstd * (dxh - xhat * jnp.mean(dxh * xhat, axis=-1, keepdims=True))

        @pl.when(i == 0)
        def _():
            dg_ref[...] = jnp.zeros_like(dg_ref)
            dsc_ref[...] = jnp.zeros_like(dsc_ref)
            dsh_ref[...] = jnp.zeros_like(dsh_ref)

        dg_ref[...] += _colsum(dhn * xhat)
        dsc_ref[...] += _colsum(dhv * hn)
        dsh_ref[...] += _colsum(dhv)

    vshape = jax.ShapeDtypeStruct((1, dm), F32)
    return pl.pallas_call(
        body, grid=(S // ts,), in_specs=[_row(ts, dm)] * 3 + [_vec(dm)] * 2,
        out_specs=(_row(ts, dm), _vec(dm), _vec(dm), _vec(dm)),
        out_shape=(jax.ShapeDtypeStruct((S, dm), F32), vshape, vshape, vshape),
        compiler_params=_cp(1), name=name)(dh, x, dres, g, sc)


def _gate_bwd(dxo, f, gate, *, name):
    S, dm = f.shape
    ts = _tile(S, TS_ROW, SUB)

    def body(dxo_ref, f_ref, gate_ref, df_ref, dgate_ref):
        i = pl.program_id(0)
        dv = dxo_ref[...]
        df_ref[...] = (gate_ref[...] * dv).astype(_ACT)

        @pl.when(i == 0)
        def _():
            dgate_ref[...] = jnp.zeros_like(dgate_ref)

        dgate_ref[...] += _colsum(dv * f_ref[...])

    return pl.pallas_call(
        body, grid=(S // ts,), in_specs=[_row(ts, dm), _row(ts, dm), _vec(dm)],
        out_specs=(_row(ts, dm), _vec(dm)),
        out_shape=(jax.ShapeDtypeStruct((S, dm), _ACT), jax.ShapeDtypeStruct((1, dm), F32)),
        compiler_params=_cp(1), name=name)(dxo, f, gate)


def _final_loss(x, f, gate, g, tgt, *, name):
    S, dm = x.shape
    ts = _tile(S, TS_ROW, SUB)

    def body(x_ref, f_ref, gate_ref, g_ref, t_ref, loss_ref, dx_ref, dg_ref):
        i = pl.program_id(0)
        xv = x_ref[...] + gate_ref[...] * f_ref[...]
        rstd = lax.rsqrt(jnp.mean(xv * xv, axis=-1, keepdims=True) + EPS)
        xhat = xv * rstd
        err = xhat * g_ref[...] - t_ref[...]
        dy = err * (1.0 / dm)
        dxh = dy * g_ref[...]
        dx_ref[...] = rstd * (dxh - xhat * jnp.mean(dxh * xhat, axis=-1, keepdims=True))

        @pl.when(i == 0)
        def _():
            loss_ref[...] = jnp.zeros_like(loss_ref)
            dg_ref[...] = jnp.zeros_like(dg_ref)

        loss_ref[...] += jnp.full((1, LANE), 0.5 * jnp.sum(jnp.mean(err * err, axis=-1, keepdims=True)), F32)
        dg_ref[...] += _colsum(dy * xhat)

    return pl.pallas_call(
        body, grid=(S // ts,), in_specs=[_row(ts, dm), _row(ts, dm), _vec(dm), _vec(dm), _row(ts, dm)],
        out_specs=(_vec(LANE), _row(ts, dm), _vec(dm)),
        out_shape=(jax.ShapeDtypeStruct((1, LANE), F32), jax.ShapeDtypeStruct((S, dm), F32),
                   jax.ShapeDtypeStruct((1, dm), F32)),
        compiler_params=_cp(1), name=name)(x, f, gate, g, tgt)


def _ffn_conv(t, halo, cw_ref, cb_ref):
    t1, t2 = _shift_down(t, halo, 1), _shift_down(t, halo, 2)
    return ((cb_ref[...] + t2 * cw_ref[0:1, :]) + t1 * cw_ref[1:2, :]) + t * cw_ref[2:3, :], t1, t2


def _prev_halo_spec(ts, w, col=0):
    return pl.BlockSpec((SUB, w), lambda i: (jnp.maximum(i * (ts // SUB) - 1, 0), col))


def _ffn_act_fwd(up, cw, cb, *, name):
    S, w2 = up.shape
    ff = w2 // 2
    ts = _tile(S, TS_FFN, SUB)

    def body(up_ref, halo_ref, cw_ref, cb_ref, act_ref):
        i = pl.program_id(0)
        t = up_ref[...]
        halo = jnp.where(i > 0, halo_ref[...], 0.0)
        u, _, _ = _ffn_conv(t, halo, cw_ref, cb_ref)
        act_ref[...] = (_silu(u[:, :ff]) * u[:, ff:]).astype(_ACT)

    return pl.pallas_call(
        body, grid=(S // ts,), in_specs=[_row(ts, w2), _prev_halo_spec(ts, w2), _vec(w2, FFN_CONV), _vec(w2)],
        out_specs=_row(ts, ff), out_shape=jax.ShapeDtypeStruct((S, ff), _ACT),
        compiler_params=_cp(1), name=name)(up, up, cw, cb)


def _ffn_bwd(up, dact, cw, cb, *, name):
    S, w2 = up.shape
    ff = w2 // 2
    ts = _tile(S, TS_FFN, SUB)
    n = S // ts

    def body(up_ref, halo_ref, dact_ref, cw_ref, cb_ref, dup_ref, dcw_ref, dcb_ref, carry_ref):
        i = pl.program_id(0)
        t_idx = n - 1 - i

        @pl.when(i == 0)
        def _():
            carry_ref[...] = jnp.zeros_like(carry_ref)
            dcw_ref[...] = jnp.zeros_like(dcw_ref)
            dcb_ref[...] = jnp.zeros_like(dcb_ref)

        t = up_ref[...]
        halo = jnp.where(t_idx > 0, halo_ref[...], 0.0)
        u, t1, t2 = _ffn_conv(t, halo, cw_ref, cb_ref)
        a, b = u[:, :ff], u[:, ff:]
        da = dact_ref[...]
        sa, dsa = _silu_grad(a)
        dv = jnp.concatenate([da * b * dsa, da * sa], axis=1)
        nxt = carry_ref[...]
        dup = (dv * cw_ref[2:3, :] + _shift_up(dv, nxt, 1) * cw_ref[1:2, :]) + _shift_up(dv, nxt, 2) * cw_ref[0:1, :]
        dup_ref[...] = dup.astype(_ACT)
        dcb_ref[...] += _colsum(dv)
        dcw_ref[2:3, :] += _colsum(dv * t)
        dcw_ref[1:2, :] += _colsum(dv * t1)
        dcw_ref[0:1, :] += _colsum(dv * t2)
        carry_ref[...] = dv[:SUB]

    rev = lambda w: pl.BlockSpec((ts, w), lambda i: (n - 1 - i, 0))
    halo_spec = pl.BlockSpec((SUB, w2), lambda i: (jnp.maximum((n - 1 - i) * (ts // SUB) - 1, 0), 0))
    return pl.pallas_call(
        body, grid=(n,), in_specs=[rev(w2), halo_spec, rev(ff), _vec(w2, FFN_CONV), _vec(w2)],
        out_specs=(rev(w2), _vec(w2, FFN_CONV), _vec(w2)),
        out_shape=(jax.ShapeDtypeStruct((S, w2), _ACT), jax.ShapeDtypeStruct((FFN_CONV, w2), F32),
                   jax.ShapeDtypeStruct((1, w2), F32)),
        scratch_shapes=[pltpu.VMEM((SUB, w2), F32)], compiler_params=_cp(1), name=name)(up, up, dact, cw, cb)


def _ssd_core(pre, halo, misc, cw_ref, cb_ref, dtb, alog):
    q = pre.shape[0]
    conv = cb_ref[...]
    for k in range(SSD_CONV):
        conv = conv + _shift_down(pre, halo, SSD_CONV - 1 - k) * cw_ref[k:k + 1, :]
    xbc = _silu(conv)
    raw = misc + dtb
    dt = _softplus(raw)
    a = -jnp.exp(alog)
    r = lax.broadcasted_iota(jnp.int32, (q, q), 0)
    c = lax.broadcasted_iota(jnp.int32, (q, q), 1)
    tri = r >= c
    acum = jnp.dot(tri.astype(F32), dt * a, precision=_HI, preferred_element_type=F32)
    return conv, xbc, raw, dt, a, acum, acum.T, tri


def _sel(v, j, lo):
    return jnp.where(lo, v[:, 2 * j:2 * j + 1], v[:, 2 * j + 1:2 * j + 2])


def _ssd_pair_fwd(xbc, dt, acum, acum_t, tri, dsk, g_mat, b_mat, c_mat, h_pair, j, lo, lo1, sub_lo):
    q = xbc.shape[0]
    x = xbc[:, LANE * j:LANE * (j + 1)]
    dtp = _sel(dt, j, lo)
    ap = _sel(acum, j, lo)
    xd = x * dtp
    ls, ms = [], []
    for h in (2 * j, 2 * j + 1):
        seg = acum[:, h:h + 1] - acum_t[h:h + 1, :]
        l_mat = jnp.exp(jnp.where(tri, seg, -jnp.inf))
        ls.append(l_mat)
        ms.append(g_mat * l_mat)
    yd = jnp.where(lo, _dot(ms[0], xd), _dot(ms[1], xd))
    ea = jnp.exp(ap)
    yo = _dot_nt(c_mat, h_pair) * ea
    dp = _sel(dsk, j, lo1)
    alast = acum[q - 1:q, :]
    e = jnp.exp(_sel(alast, j, lo1) - ap)
    cd = jnp.where(sub_lo, jnp.exp(alast[:, 2 * j:2 * j + 1]), jnp.exp(alast[:, 2 * j + 1:2 * j + 2]))
    return dict(x=x, dtp=dtp, ap=ap, xd=xd, ls=ls, ms=ms, ea=ea, yo=yo, dp=dp, e=e, cd=cd, y=yd + yo + x * dp)


def _gnorm(yg):
    half = SSD_INNER // SSD_GROUPS
    rstds, yns = [], []
    for g in range(SSD_GROUPS):
        part = yg[:, half * g:half * (g + 1)]
        rstd = lax.rsqrt(jnp.mean(part * part, axis=-1, keepdims=True) + EPS)
        rstds.append(rstd)
        yns.append(part * rstd)
    return rstds, yns


def _ssd_specs(nc, rev):
    q = SSD_CHUNK
    cidx = (lambda i: nc - 1 - i) if rev else (lambda i: i)
    return [
        pl.BlockSpec((q, SSD_XBC), lambda i: (cidx(i), C_XBC // SSD_XBC)),
        pl.BlockSpec((SUB, SSD_XBC), lambda i: (jnp.maximum(cidx(i) * (q // SUB) - 1, 0), C_XBC // SSD_XBC)),
        pl.BlockSpec((q, SSD_INNER), lambda i: (cidx(i), C_Z // SSD_INNER)),
        pl.BlockSpec((q, LANE), lambda i: (cidx(i), C_MISC // LANE)),
    ]


def _ssd_param_specs():
    return [_vec(SSD_XBC, SSD_CONV), _vec(SSD_XBC), _vec(LANE), _vec(LANE), _vec(LANE), _vec(SSD_INNER)]


def _ssd_fwd(proj, cw, cb, dtb, alog, dsk, ng, *, name):
    S = proj.shape[0]
    q = SSD_CHUNK
    nc = S // q
    npair = SSD_HEADS // 2

    def body(xbc_ref, halo_ref, z_ref, misc_ref, cw_ref, cb_ref, dtb_ref, alog_ref, dsk_ref, ng_ref,
             y_ref, hin_ref, h_ref):
        c = pl.program_id(0)

        @pl.when(c == 0)
        def _():
            h_ref[...] = jnp.zeros_like(h_ref)

        pre = xbc_ref[...]
        halo = jnp.where(c > 0, halo_ref[...], 0.0)
        conv, xbc, raw, dt, a, acum, acum_t, tri = _ssd_core(pre, halo, misc_ref[...], cw_ref, cb_ref,
                                                             dtb_ref[...], alog_ref[...])
        lo = lax.broadcasted_iota(jnp.int32, (q, LANE), 1) < LANE // 2
        lo1 = lo[:1]
        sub_lo = lax.broadcasted_iota(jnp.int32, (LANE, LANE), 0) < LANE // 2
        ys = []
        for g in range(SSD_GROUPS):
            b_mat = xbc[:, SSD_INNER + SSD_STATE * g:SSD_INNER + SSD_STATE * (g + 1)]
            c_mat = xbc[:, SSD_INNER + SSD_STATE * (SSD_GROUPS + g):SSD_INNER + SSD_STATE * (SSD_GROUPS + g + 1)]
            g_mat = _dot_nt(c_mat, b_mat)
            for jj in range(npair // SSD_GROUPS):
                j = g * (npair // SSD_GROUPS) + jj
                hj = h_ref[j]
                p = _ssd_pair_fwd(xbc, dt, acum, acum_t, tri, dsk_ref[...], g_mat, b_mat, c_mat, hj, j, lo, lo1, sub_lo)
                ys.append(p["y"])
                hin_ref[0, j] = hj
                h_ref[j] = p["cd"] * hj + _dot_tn(p["xd"] * p["e"], b_mat)
        yg = jnp.concatenate(ys, axis=1) * _silu(z_ref[...])
        _, yns = _gnorm(yg)
        y_ref[...] = jnp.concatenate(yns, axis=1) * ng_ref[...]

    return pl.pallas_call(
        body, grid=(nc,), in_specs=_ssd_specs(nc, False) + _ssd_param_specs(),
        out_specs=(pl.BlockSpec((q, SSD_INNER), lambda i: (i, 0)),
                   pl.BlockSpec((1, npair, LANE, LANE), lambda i: (i, 0, 0, 0))),
        out_shape=(jax.ShapeDtypeStruct((S, SSD_INNER), F32), jax.ShapeDtypeStruct((nc, npair, LANE, LANE), F32)),
        scratch_shapes=[pltpu.VMEM((npair, LANE, LANE), F32)], compiler_params=_cp(1), name=name,
    )(proj, proj, proj, proj, cw, cb, dtb, alog, dsk, ng)


def _ssd_bwd(proj, dycat, hin, cw, cb, dtb, alog, dsk, ng, *, name):
    S = proj.shape[0]
    q = SSD_CHUNK
    nc = S // q
    npair = SSD_HEADS // 2
    ppg = npair // SSD_GROUPS

    def body(xbc_ref, halo_ref, z_ref, misc_ref, dy_ref, hin_ref, cw_ref, cb_ref, dtb_ref, alog_ref, dsk_ref, ng_ref,
             dpre_ref, dz_ref, dmisc_ref, dcw_ref, dcb_ref, ddtb_ref, dalog_ref, ddsk_ref, dng_ref,
             dh_ref, carry_ref):
        i = pl.program_id(0)
        c = nc - 1 - i

        @pl.when(i == 0)
        def _():
            dh_ref[...] = jnp.zeros_like(dh_ref)
            carry_ref[...] = jnp.zeros_like(carry_ref)
            for r in (dcw_ref, dcb_ref, ddtb_ref, dalog_ref, ddsk_ref, dng_ref):
                r[...] = jnp.zeros_like(r)

        pre = xbc_ref[...]
        halo = jnp.where(c > 0, halo_ref[...], 0.0)
        conv, xbc, raw, dt, a, acum, acum_t, tri = _ssd_core(pre, halo, misc_ref[...], cw_ref, cb_ref,
                                                             dtb_ref[...], alog_ref[...])
        lane = lax.broadcasted_iota(jnp.int32, (q, LANE), 1)
        lane1 = lane[:1]
        rowi = lax.broadcasted_iota(jnp.int32, (q, LANE), 0)
        lastrow = rowi == q - 1
        lo = lane < LANE // 2
        lo1 = lo[:1]
        sub_lo = lax.broadcasted_iota(jnp.int32, (LANE, LANE), 0) < LANE // 2
        dsk = dsk_ref[...]
        alast = acum[q - 1:q, :]

        def halves(t):
            return _rowsum(jnp.where(lo, t, 0.0)), _rowsum(jnp.where(lo, 0.0, t))

        def put(ha, va, vb):
            ln = lane if va.shape[0] == q else lane1
            return jnp.where(ln == ha, va, 0.0) + jnp.where(ln == ha + 1, vb, 0.0)

        mats, pairs = [], []
        for g in range(SSD_GROUPS):
            b_mat = xbc[:, SSD_INNER + SSD_STATE * g:SSD_INNER + SSD_STATE * (g + 1)]
            c_mat = xbc[:, SSD_INNER + SSD_STATE * (SSD_GROUPS + g):SSD_INNER + SSD_STATE * (SSD_GROUPS + g + 1)]
            g_mat = _dot_nt(c_mat, b_mat)
            mats.append((b_mat, c_mat, g_mat))
            for jj in range(ppg):
                j = g * ppg + jj
                pairs.append(_ssd_pair_fwd(xbc, dt, acum, acum_t, tri, dsk, g_mat, b_mat, c_mat, hin_ref[0, j],
                                           j, lo, lo1, sub_lo))
        z = z_ref[...]
        sz, dsz = _silu_grad(z)
        y = jnp.concatenate([p["y"] for p in pairs], axis=1)
        rstds, yns = _gnorm(y * sz)
        dout = dy_ref[...]
        dng_ref[...] += _colsum(dout * jnp.concatenate(yns, axis=1))
        dyn = dout * ng_ref[...]
        half = SSD_INNER // SSD_GROUPS
        dygs = []
        for g in range(SSD_GROUPS):
            dyn_g = dyn[:, half * g:half * (g + 1)]
            dygs.append(rstds[g] * (dyn_g - yns[g] * jnp.mean(dyn_g * yns[g], axis=-1, keepdims=True)))
        dyg = jnp.concatenate(dygs, axis=1)
        dyv = dyg * sz
        dz_ref[...] = (dyg * y * dsz).astype(_ACT)

        da_acc = jnp.zeros((q, LANE), F32)
        ddt = jnp.zeros((q, LANE), F32)
        dds = jnp.zeros((1, LANE), F32)
        dxs, dbs, dcs = [], [], []
        for g in range(SSD_GROUPS):
            b_mat, c_mat, g_mat = mats[g]
            dg_mat = jnp.zeros((q, q), F32)
            db = jnp.zeros((q, SSD_STATE), F32)
            dc = jnp.zeros((q, SSD_STATE), F32)
            for jj in range(ppg):
                j = g * ppg + jj
                ha = 2 * j
                p = pairs[j]
                hj = hin_ref[0, j]
                dyp = dyv[:, LANE * j:LANE * (j + 1)]
                dsum = _colsum(dyp * p["x"])
                dds = dds + put(ha, _rowsum(jnp.where(lo1, dsum, 0.0)), _rowsum(jnp.where(lo1, 0.0, dsum)))
                dx = dyp * p["dp"]
                dw = dyp * p["ea"]
                dc = dc + _dot(dw, hj)
                dh_yo = _dot_tn(dw, c_mat)
                ra, rb = halves(dyp * p["yo"])
                da_acc = da_acc + put(ha, ra, rb)
                dxd = jnp.zeros((q, LANE), F32)
                for idx in range(2):
                    dyh = jnp.where(lo, dyp, 0.0) if idx == 0 else jnp.where(lo, 0.0, dyp)
                    dm = _dot_nt(dyh, p["xd"])
                    dxd = dxd + _dot_tn(p["ms"][idx], dyh)
                    dg_mat = dg_mat + dm * p["ls"][idx]
                    t = dm * p["ms"][idx]
                    da_h = _rowsum(t) - _rowsum(t.T)
                    da_acc = da_acc + jnp.where(lane == ha + idx, da_h, 0.0)
                dhn = dh_ref[j]
                s = _rowsum(dhn * hj)
                sa = jnp.sum(jnp.where(sub_lo[:, :1], s, 0.0), keepdims=True)
                sb = jnp.sum(jnp.where(sub_lo[:, :1], 0.0, s), keepdims=True)
                cda, cdb = jnp.exp(alast[:, ha:ha + 1]), jnp.exp(alast[:, ha + 1:ha + 2])
                db = db + _dot(p["xd"] * p["e"], dhn)
                r = _dot_nt(b_mat, dhn)
                dxd = dxd + r * p["e"]
                qa, qb = halves(r * p["xd"] * p["e"])
                da_acc = da_acc - put(ha, qa, qb)
                tot_a = sa * cda + jnp.sum(qa, keepdims=True)
                tot_b = sb * cdb + jnp.sum(qb, keepdims=True)
                da_acc = da_acc + jnp.where(lastrow, put(ha, tot_a, tot_b), 0.0)
                dh_ref[j] = p["cd"] * dhn + dh_yo
                dx = dx + dxd * p["dtp"]
                ua, ub = halves(dxd * p["x"])
                ddt = ddt + put(ha, ua, ub)
                dxs.append(dx)
            dc = dc + _dot(dg_mat, b_mat)
            db = db + _dot_tn(dg_mat, c_mat)
            dbs.append(db)
            dcs.append(dc)
        r2 = lax.broadcasted_iota(jnp.int32, (q, q), 0)
        c2 = lax.broadcasted_iota(jnp.int32, (q, q), 1)
        dda = jnp.dot((c2 >= r2).astype(F32), da_acc, precision=_HI, preferred_element_type=F32)
        ddt = ddt + dda * a
        dalog_ref[...] += _colsum(dda * dt) * a
        ddsk_ref[...] += dds
        draw = jnp.where(lane < SSD_HEADS, ddt * _sigmoid(raw), 0.0)
        ddtb_ref[...] += _colsum(draw)
        dmisc_ref[...] = draw
        dconv = jnp.concatenate(dxs + dbs + dcs, axis=1) * _dsilu(conv)
        dcb_ref[...] += _colsum(dconv)
        nxt = carry_ref[...]
        dpre = jnp.zeros_like(dconv)
        for k in range(SSD_CONV):
            dcw_ref[k:k + 1, :] += _colsum(dconv * _shift_down(pre, halo, SSD_CONV - 1 - k))
            dpre = dpre + _shift_up(dconv, nxt, SSD_CONV - 1 - k) * cw_ref[k:k + 1, :]
        dpre_ref[...] = dpre.astype(_ACT)
        carry_ref[...] = dconv[:SUB]

    rev = lambda i: (nc - 1 - i, 0)
    vshape = lambda w, r=1: jax.ShapeDtypeStruct((r, w), F32)
    return pl.pallas_call(
        body, grid=(nc,),
        in_specs=_ssd_specs(nc, True) + [pl.BlockSpec((q, SSD_INNER), rev),
                                         pl.BlockSpec((1, npair, LANE, LANE), lambda i: (nc - 1 - i, 0, 0, 0))]
        + _ssd_param_specs(),
        out_specs=(pl.BlockSpec((q, SSD_XBC), rev), pl.BlockSpec((q, SSD_INNER), rev), pl.BlockSpec((q, LANE), rev),
                   _vec(SSD_XBC, SSD_CONV), _vec(SSD_XBC), _vec(LANE), _vec(LANE), _vec(LANE), _vec(SSD_INNER)),
        out_shape=(jax.ShapeDtypeStruct((S, SSD_XBC), _ACT), jax.ShapeDtypeStruct((S, SSD_INNER), _ACT),
                   jax.ShapeDtypeStruct((S, LANE), F32),
                   vshape(SSD_XBC, SSD_CONV), vshape(SSD_XBC), vshape(LANE), vshape(LANE), vshape(LANE),
                   vshape(SSD_INNER)),
        scratch_shapes=[pltpu.VMEM((npair, LANE, LANE), F32), pltpu.VMEM((SUB, SSD_XBC), F32)],
        compiler_params=_cp(1), name=name,
    )(proj, proj, proj, proj, dycat, hin, cw, cb, dtb, alog, dsk, ng)


def _rope(x, cosf, sina, sinb):
    return x * cosf + pltpu.roll(x, LANE - MLA_ROPE // 2, 1) * sina + pltpu.roll(x, MLA_ROPE // 2, 1) * sinb


def _rope_t(dy, cosf, sina, sinb):
    return dy * cosf + pltpu.roll(dy * sina, MLA_ROPE // 2, 1) + pltpu.roll(dy * sinb, LANE - MLA_ROPE // 2, 1)


def _rope_lanes(shape):
    lane = lax.broadcasted_iota(jnp.int32, shape, 1)
    return (lane >= ROPE_LANE) & (lane < ROPE_LANE + MLA_ROPE)


def _mla_prep_fwd(proj, cosf, sina, sinb, gq, gkv, wuq, wukv, *, name):
    S = proj.shape[0]
    ts = _tile(S, TS_ROW, SUB)
    hw = MLA_HEADS * LANE

    def body(cq_ref, ckv_ref, misc_ref, cos_ref, sa_ref, sb_ref, gq_ref, gkv_ref, wuq_ref, wukv_ref,
             q_ref, k_ref, v_ref, vt_ref):
        cosv, sav, sbv = cos_ref[...], sa_ref[...], sb_ref[...]
        cq = cq_ref[...]
        qn = cq * lax.rsqrt(jnp.mean(cq * cq, axis=-1, keepdims=True) + EPS) * gq_ref[...]
        qh = _dot(qn, wuq_ref[...])
        ckv = ckv_ref[...]
        kvn = ckv * lax.rsqrt(jnp.mean(ckv * ckv, axis=-1, keepdims=True) + EPS) * gkv_ref[...]
        kv = _dot(kvn, wukv_ref[...])
        kr = _rope(jnp.where(_rope_lanes((ts, LANE)), misc_ref[...], 0.0), cosv, sav, sbv)
        for h in range(MLA_HEADS):
            sl = slice(LANE * h, LANE * (h + 1))
            q_ref[:, sl] = (_rope(qh[:, sl], cosv, sav, sbv) * _Q_SCALE).astype(_ACT)
            k_ref[:, sl] = (kv[:, sl] + kr).astype(_ACT)
        v_ref[...] = kv[:, hw:].astype(_ACT)
        vt_ref[...] = kv[:, hw:].T.astype(_ACT)

    oshape = jax.ShapeDtypeStruct((S, hw), _ACT)
    return pl.pallas_call(
        body, grid=(S // ts,),
        in_specs=[_row(ts, MLA_QR, C_CQ // MLA_QR), _row(ts, MLA_KVR, C_CKV // MLA_KVR), _row(ts, LANE, C_MISC // LANE),
                  _row(ts, LANE), _row(ts, LANE), _row(ts, LANE), _vec(MLA_QR), _vec(MLA_KVR),
                  _vec(hw, MLA_QR), _vec(2 * hw, MLA_KVR)],
        out_specs=(_row(ts, hw),) * 3 + (pl.BlockSpec((hw, ts), lambda i: (0, i)),),
        out_shape=(oshape,) * 3 + (jax.ShapeDtypeStruct((hw, S), _ACT),), compiler_params=_cp(1), name=name,
    )(proj, proj, proj, cosf, sina, sinb, gq, gkv, wuq, wukv)


def _mla_prep_bwd(proj, dq, dk, dv, dmisc_ssd, cosf, sina, sinb, gq, gkv, wuq, wukv, *, name):
    S = proj.shape[0]
    ts = _tile(S, TS_ROW, SUB)
    hw = MLA_HEADS * LANE

    def body(cq_ref, ckv_ref, dq_ref, dk_ref, dv_ref, dms_ref, cos_ref, sa_ref, sb_ref, gq_ref, gkv_ref,
             wuq_ref, wukv_ref, dcq_ref, dckv_ref, dmisc_ref, dqh_ref, dkv_ref, qn_ref, kvn_ref, dgq_ref, dgkv_ref):
        i = pl.program_id(0)
        cosv, sav, sbv = cos_ref[...], sa_ref[...], sb_ref[...]

        @pl.when(i == 0)
        def _():
            dgq_ref[...] = jnp.zeros_like(dgq_ref)
            dgkv_ref[...] = jnp.zeros_like(dgkv_ref)

        dqh = jnp.concatenate([_rope_t(dq_ref[:, LANE * h:LANE * (h + 1)], cosv, sav, sbv)
                               for h in range(MLA_HEADS)], axis=1)
        dqh_ref[...] = dqh.astype(_ACT)
        dkv = jnp.concatenate([dk_ref[...], dv_ref[...]], axis=1)
        dkv_ref[...] = dkv.astype(_ACT)

        def norm_bwd(x, g, dn, dg_ref, n_ref):
            rstd = lax.rsqrt(jnp.mean(x * x, axis=-1, keepdims=True) + EPS)
            xhat = x * rstd
            n_ref[...] = (xhat * g).astype(_ACT)
            dg_ref[...] += _colsum(dn * xhat)
            dxh = dn * g
            return rstd * (dxh - xhat * jnp.mean(dxh * xhat, axis=-1, keepdims=True))

        dcq_ref[...] = norm_bwd(cq_ref[...], gq_ref[...], _dot_nt(dqh, wuq_ref[...]), dgq_ref, qn_ref).astype(_ACT)
        dckv_ref[...] = norm_bwd(ckv_ref[...], gkv_ref[...], _dot_nt(dkv, wukv_ref[...]), dgkv_ref, kvn_ref).astype(_ACT)
        dks = dk_ref[:, 0:LANE]
        for h in range(1, MLA_HEADS):
            dks = dks + dk_ref[:, LANE * h:LANE * (h + 1)]
        rl = _rope_lanes((ts, LANE))
        dkr = _rope_t(jnp.where(rl, dks, 0.0), cosv, sav, sbv)
        dmisc_ref[...] = (dms_ref[...] + jnp.where(rl, dkr, 0.0)).astype(_ACT)

    act = lambda w: jax.ShapeDtypeStruct((S, w), _ACT)
    return pl.pallas_call(
        body, grid=(S // ts,),
        in_specs=[_row(ts, MLA_QR, C_CQ // MLA_QR), _row(ts, MLA_KVR, C_CKV // MLA_KVR),
                  _row(ts, hw), _row(ts, hw), _row(ts, hw), _row(ts, LANE),
                  _row(ts, LANE), _row(ts, LANE), _row(ts, LANE), _vec(MLA_QR), _vec(MLA_KVR),
                  _vec(hw, MLA_QR), _vec(2 * hw, MLA_KVR)],
        out_specs=(_row(ts, MLA_QR), _row(ts, MLA_KVR), _row(ts, LANE), _row(ts, hw), _row(ts, 2 * hw),
                   _row(ts, MLA_QR), _row(ts, MLA_KVR), _vec(MLA_QR), _vec(MLA_KVR)),
        out_shape=(act(MLA_QR), act(MLA_KVR), act(LANE), act(hw), act(2 * hw), act(MLA_QR), act(MLA_KVR),
                   jax.ShapeDtypeStruct((1, MLA_QR), F32), jax.ShapeDtypeStruct((1, MLA_KVR), F32)),
        compiler_params=_cp(1), name=name,
    )(proj, proj, dq, dk, dv, dmisc_ssd, cosf, sina, sinb, gq, gkv, wuq, wukv)


_MLA_SCALE = 1.0 / math.sqrt(MLA_NOPE + MLA_ROPE)
_LOG2E = 1.4426950408889634
_Q_SCALE = _MLA_SCALE * _LOG2E
ATT_CHUNK = 1024


def _tri_grid(nq, by_key):
    if by_key:
        pairs = [(i, j) for j in range(nq) for i in range(j, nq)]
    else:
        pairs = [(i, j) for i in range(nq) for j in range(i + 1)]
    return jnp.asarray([p[0] for p in pairs], jnp.int32), jnp.asarray([p[1] for p in pairs], jnp.int32)


def _attn_fwd(q, k, vt, *, name):
    S = q.shape[0]
    tq = _tile(S, TQ_ATT)
    nq = S // tq
    itab, jtab = _tri_grid(nq, False)

    def body(it_ref, jt_ref, q_ref, k_ref, vt_ref, o_ref, lse_ref, lset_ref, m_ref, l_ref, acc_ref):
        t = pl.program_id(1)
        i, j = it_ref[t], jt_ref[t]

        @pl.when(j == 0)
        def _():
            m_ref[...] = jnp.full_like(m_ref, -jnp.inf)
            l_ref[...] = jnp.zeros_like(l_ref)
            acc_ref[...] = jnp.zeros_like(acc_ref)

        def step(diagonal):
            s = _dot_nt(k_ref[...], q_ref[...])
            if diagonal:
                kk = lax.broadcasted_iota(jnp.int32, (tq, tq), 0)
                s = jnp.where(kk <= lax.broadcasted_iota(jnp.int32, (tq, tq), 1), s, -jnp.inf)
            m_prev = m_ref[...]
            m_new = jnp.maximum(m_prev, jnp.max(s, axis=0, keepdims=True))
            p = jnp.exp2(s - m_new)
            alpha = jnp.exp2(m_prev - m_new)
            l_ref[...] = alpha * l_ref[...] + _colsum(p)
            acc_ref[...] = alpha * acc_ref[...] + _dot(vt_ref[...], p)
            m_ref[...] = m_new

        pl.when(j < i)(functools.partial(step, False))
        pl.when(j == i)(functools.partial(step, True))

        @pl.when(j == i)
        def _():
            o_ref[...] = (acc_ref[...] / l_ref[...]).T
            lse = m_ref[...] + jnp.log2(l_ref[...])
            lset_ref[...] = jnp.broadcast_to(lse, (SUB, tq))
            lse_ref[...] = jnp.broadcast_to(lse, (LANE, tq)).T

    qspec = pl.BlockSpec((tq, LANE), lambda h, t, it, jt: (it[t], h))
    kspec = pl.BlockSpec((tq, LANE), lambda h, t, it, jt: (jt[t], h))
    vtspec = pl.BlockSpec((LANE, tq), lambda h, t, it, jt: (h, jt[t]))
    oshape = jax.ShapeDtypeStruct((S, MLA_HEADS * LANE), F32)
    return pl.pallas_call(
        body,
        grid_spec=pltpu.PrefetchScalarGridSpec(
            num_scalar_prefetch=2, grid=(MLA_HEADS, itab.shape[0]), in_specs=[qspec, kspec, vtspec],
            out_specs=(qspec, qspec, pl.BlockSpec((SUB, tq), lambda h, t, it, jt: (h, it[t]))),
            scratch_shapes=[pltpu.VMEM((1, tq), F32), pltpu.VMEM((1, tq), F32), pltpu.VMEM((LANE, tq), F32)]),
        out_shape=(oshape, oshape, jax.ShapeDtypeStruct((MLA_HEADS * SUB, S), F32)),
        compiler_params=_cp(2), name=name)(itab, jtab, q, k, vt)


def _attn_bwd_dq(q, k, v, o, lse, dycat, *, name):
    S = q.shape[0]
    tq = _tile(S, TQ_ATT)
    nq = S // tq
    rc = min(ATT_CHUNK, tq)
    itab, jtab = _tri_grid(nq, False)

    def body(it_ref, jt_ref, q_ref, k_ref, v_ref, o_ref, lse_ref, do_ref, dq_ref, acc_ref):
        t = pl.program_id(1)
        i, j = it_ref[t], jt_ref[t]

        @pl.when(j == 0)
        def _():
            acc_ref[...] = jnp.zeros_like(acc_ref)

        def step(diagonal):
            kv, vv = k_ref[...], v_ref[...]
            for r in range(tq // rc):
                rows = slice(r * rc, (r + 1) * rc)
                s = _dot_nt(q_ref[rows, :], kv)
                if diagonal:
                    rr = r * rc + lax.broadcasted_iota(jnp.int32, (rc, tq), 0)
                    s = jnp.where(lax.broadcasted_iota(jnp.int32, (rc, tq), 1) <= rr, s, -jnp.inf)
                p = jnp.exp2(s - lse_ref[rows, 0:1])
                dov = do_ref[rows, :]
                delta = _rowsum(dov * o_ref[rows, :])
                ds = p * (_dot_nt(dov, vv) - delta)
                acc_ref[rows, :] += _dot(ds, kv)

        pl.when(j < i)(functools.partial(step, False))
        pl.when(j == i)(functools.partial(step, True))

        @pl.when(j == i)
        def _():
            dq_ref[...] = acc_ref[...] * _MLA_SCALE

    qspec = pl.BlockSpec((tq, LANE), lambda h, t, it, jt: (it[t], h))
    kspec = pl.BlockSpec((tq, LANE), lambda h, t, it, jt: (jt[t], h))
    dospec = pl.BlockSpec((tq, LANE), lambda h, t, it, jt: (it[t], SSD_INNER // LANE + h))
    return pl.pallas_call(
        body,
        grid_spec=pltpu.PrefetchScalarGridSpec(
            num_scalar_prefetch=2, grid=(MLA_HEADS, itab.shape[0]),
            in_specs=[qspec, kspec, kspec, qspec, qspec, dospec], out_specs=qspec,
            scratch_shapes=[pltpu.VMEM((tq, LANE), F32)]),
        out_shape=jax.ShapeDtypeStruct((S, MLA_HEADS * LANE), F32),
        compiler_params=_cp(2), name=name)(itab, jtab, q, k, v, o, lse, dycat)


def _attn_bwd_dkv(q, k, v, o, lset, dycat, *, name):
    S = q.shape[0]
    tq = _tile(S, TQ_ATT)
    nq = S // tq
    kc = min(ATT_CHUNK, tq)
    itab, jtab = _tri_grid(nq, True)

    def body(it_ref, jt_ref, q_ref, k_ref, v_ref, o_ref, lset_ref, do_ref, dk_ref, dv_ref, dk_acc, dv_acc):
        t = pl.program_id(1)
        i, j = it_ref[t], jt_ref[t]

        @pl.when(i == j)
        def _():
            dk_acc[...] = jnp.zeros_like(dk_acc)
            dv_acc[...] = jnp.zeros_like(dv_acc)

        def step(diagonal):
            qv, dov = q_ref[...], do_ref[...]
            delta = lax.dot_general(jnp.ones((SUB, LANE), F32), dov * o_ref[...], (((1,), (1,)), ((), ())),
                                    precision=_HI, preferred_element_type=F32)[0:1]
            lse = lset_ref[0:1, :]
            for c in range(tq // kc):
                rows = slice(c * kc, (c + 1) * kc)
                s = _dot_nt(k_ref[rows, :], qv)
                if diagonal:
                    kk = c * kc + lax.broadcasted_iota(jnp.int32, (kc, tq), 0)
                    s = jnp.where(kk <= lax.broadcasted_iota(jnp.int32, (kc, tq), 1), s, -jnp.inf)
                p = jnp.exp2(s - lse)
                dv_acc[rows, :] += _dot(p, dov)
                ds = p * (_dot_nt(v_ref[rows, :], dov) - delta)
                dk_acc[rows, :] += _dot(ds, qv)

        pl.when(i > j)(functools.partial(step, False))
        pl.when(i == j)(functools.partial(step, True))

        @pl.when(i == nq - 1)
        def _():
            dk_ref[...] = dk_acc[...] * (1.0 / _LOG2E)
            dv_ref[...] = dv_acc[...]

    qspec = pl.BlockSpec((tq, LANE), lambda h, t, it, jt: (it[t], h))
    kspec = pl.BlockSpec((tq, LANE), lambda h, t, it, jt: (jt[t], h))
    dospec = pl.BlockSpec((tq, LANE), lambda h, t, it, jt: (it[t], SSD_INNER // LANE + h))
    lspec = pl.BlockSpec((SUB, tq), lambda h, t, it, jt: (h, it[t]))
    oshape = jax.ShapeDtypeStruct((S, MLA_HEADS * LANE), F32)
    return pl.pallas_call(
        body,
        grid_spec=pltpu.PrefetchScalarGridSpec(
            num_scalar_prefetch=2, grid=(MLA_HEADS, itab.shape[0]),
            in_specs=[qspec, kspec, kspec, qspec, lspec, dospec], out_specs=(kspec, kspec),
            scratch_shapes=[pltpu.VMEM((tq, LANE), F32), pltpu.VMEM((tq, LANE), F32)]),
        out_shape=(oshape, oshape), compiler_params=_cp(2), name=name)(itab, jtab, q, k, v, o, lset, dycat)


_SWA_SCALE = 1.0 / math.sqrt(SWA_HD)
_SWA_KW = SWA_KV * LANE


def _swa_specs(S, ts, rev):
    n = S // ts
    t = (lambda i: n - 1 - i) if rev else (lambda i: i)
    hb = lambda i: jnp.maximum(t(i) * (ts // WINDOW) - 1, 0)
    return [
        pl.BlockSpec((ts, SWA_HEADS * LANE), lambda i: (t(i), C_SQ // (SWA_HEADS * LANE))),
        pl.BlockSpec((ts, _SWA_KW), lambda i: (t(i), C_SK // _SWA_KW)),
        pl.BlockSpec((WINDOW, _SWA_KW), lambda i: (hb(i), C_SK // _SWA_KW)),
        pl.BlockSpec((ts, _SWA_KW), lambda i: (t(i), C_SV // _SWA_KW)),
        pl.BlockSpec((WINDOW, _SWA_KW), lambda i: (hb(i), C_SV // _SWA_KW)),
    ]


def _swa_scores(qh, kk, t, b, ts):
    s = _dot_nt(qh, kk) * _SWA_SCALE
    row = lax.broadcasted_iota(jnp.int32, (WINDOW, 2 * WINDOW), 0)
    col = lax.broadcasted_iota(jnp.int32, (WINDOW, 2 * WINDOW), 1)
    rel = WINDOW + row - col
    kpos = t * ts + (b - 1) * WINDOW + col
    return jnp.where((rel >= 0) & (rel < WINDOW) & (kpos >= 0), s, -jnp.inf)


def _swa_fwd(proj, sinks, *, name):
    S = proj.shape[0]
    ts = _tile(S, TS_SWA)
    nb = ts // WINDOW

    def body(q_ref, k_ref, kh_ref, v_ref, vh_ref, sink_ref, o_ref, lse_ref):
        t = pl.program_id(0)
        kext = jnp.concatenate([kh_ref[...], k_ref[...]], axis=0)
        vext = jnp.concatenate([vh_ref[...], v_ref[...]], axis=0)
        for b in range(nb):
            rows = slice(WINDOW * b, WINDOW * (b + 1))
            for h in range(SWA_HEADS):
                kvl = slice(LANE * (h // (SWA_HEADS // SWA_KV)), LANE * (h // (SWA_HEADS // SWA_KV) + 1))
                hl = slice(LANE * h, LANE * (h + 1))
                kk = kext[WINDOW * b:WINDOW * (b + 2), kvl]
                vv = vext[WINDOW * b:WINDOW * (b + 2), kvl]
                s = _swa_scores(q_ref[rows, hl], kk, t, b, ts)
                sk = sink_ref[:, h:h + 1]
                m = jnp.maximum(jnp.max(s, axis=1, keepdims=True), sk)
                p = jnp.exp(s - m)
                den = _rowsum(p) + jnp.exp(sk - m)
                o_ref[rows, hl] = _dot(p, vv) / den
                lse_ref[rows, hl] = jnp.broadcast_to(m + jnp.log(den), (WINDOW, LANE))

    oshape = jax.ShapeDtypeStruct((S, SWA_HEADS * LANE), F32)
    ospec = pl.BlockSpec((ts, SWA_HEADS * LANE), lambda i: (i, 0))
    return pl.pallas_call(
        body, grid=(S // ts,), in_specs=_swa_specs(S, ts, False) + [_vec(LANE)], out_specs=(ospec, ospec),
        out_shape=(oshape, oshape), compiler_params=_cp(1), name=name)(proj, proj, proj, proj, proj, sinks)


def _swa_bwd(proj, o, lse, dycat, sinks, *, name):
    S = proj.shape[0]
    ts = _tile(S, TS_SWA)
    nb = ts // WINDOW
    n = S // ts
    grp = SWA_HEADS // SWA_KV

    def body(q_ref, k_ref, kh_ref, v_ref, vh_ref, o_ref, lse_ref, do_ref, sink_ref,
             dq_ref, dk_ref, dv_ref, dsink_ref, dk_carry, dv_carry):
        i = pl.program_id(0)
        t = n - 1 - i

        @pl.when(i == 0)
        def _():
            dk_carry[...] = jnp.zeros_like(dk_carry)
            dv_carry[...] = jnp.zeros_like(dv_carry)
            dsink_ref[...] = jnp.zeros_like(dsink_ref)

        kext = jnp.concatenate([kh_ref[...], k_ref[...]], axis=0)
        vext = jnp.concatenate([vh_ref[...], v_ref[...]], axis=0)
        lane1 = lax.broadcasted_iota(jnp.int32, (1, LANE), 1)
        dkb = [[jnp.zeros((WINDOW, LANE), F32) for _ in range(SWA_KV)] for _ in range(nb + 1)]
        dvb = [[jnp.zeros((WINDOW, LANE), F32) for _ in range(SWA_KV)] for _ in range(nb + 1)]
        dsink = jnp.zeros((1, LANE), F32)
        for b in range(nb):
            rows = slice(WINDOW * b, WINDOW * (b + 1))
            for h in range(SWA_HEADS):
                kvh = h // grp
                kvl = slice(LANE * kvh, LANE * (kvh + 1))
                hl = slice(LANE * h, LANE * (h + 1))
                kk = kext[WINDOW * b:WINDOW * (b + 2), kvl]
                vv = vext[WINDOW * b:WINDOW * (b + 2), kvl]
                qh = q_ref[rows, hl]
                lse_h = lse_ref[rows, LANE * h:LANE * h + 1]
                p = jnp.exp(_swa_scores(qh, kk, t, b, ts) - lse_h)
                doh = do_ref[rows, hl]
                delta = _rowsum(doh * o_ref[rows, hl])
                ds = p * (_dot_nt(doh, vv) - delta)
                sk = sink_ref[:, h:h + 1]
                dsink = dsink + jnp.where(lane1 == h, -jnp.sum(jnp.exp(sk - lse_h) * delta, keepdims=True), 0.0)
                dq_ref[rows, hl] = (_dot(ds, kk) * _SWA_SCALE).astype(_ACT)
                dkk = _dot_tn(ds, qh) * _SWA_SCALE
                dvv = _dot_tn(p, doh)
                dkb[b][kvh] = dkb[b][kvh] + dkk[:WINDOW]
                dkb[b + 1][kvh] = dkb[b + 1][kvh] + dkk[WINDOW:]
                dvb[b][kvh] = dvb[b][kvh] + dvv[:WINDOW]
                dvb[b + 1][kvh] = dvb[b + 1][kvh] + dvv[WINDOW:]
        dsink_ref[...] += dsink
        for dref, blocks, carry in ((dk_ref, dkb, dk_carry), (dv_ref, dvb, dv_carry)):
            old = carry[...]
            for b in range(1, nb + 1):
                blk = jnp.concatenate(blocks[b], axis=1)
                if b == nb:
                    blk = blk + old
                dref[WINDOW * (b - 1):WINDOW * b, :] = blk.astype(_ACT)
            carry[...] = jnp.concatenate(blocks[0], axis=1)

    hw = SWA_HEADS * LANE
    rev = lambda i: (n - 1 - i, 0)
    mix = lambda i: (n - 1 - i, (SSD_INNER + MLA_HEADS * LANE) // hw)
    return pl.pallas_call(
        body, grid=(n,),
        in_specs=_swa_specs(S, ts, True) + [pl.BlockSpec((ts, hw), rev), pl.BlockSpec((ts, hw), rev),
                                            pl.BlockSpec((ts, hw), mix), _vec(LANE)],
        out_specs=(pl.BlockSpec((ts, hw), rev), pl.BlockSpec((ts, _SWA_KW), rev), pl.BlockSpec((ts, _SWA_KW), rev),
                   _vec(LANE)),
        out_shape=(jax.ShapeDtypeStruct((S, hw), _ACT), jax.ShapeDtypeStruct((S, _SWA_KW), _ACT),
                   jax.ShapeDtypeStruct((S, _SWA_KW), _ACT), jax.ShapeDtypeStruct((1, LANE), F32)),
        scratch_shapes=[pltpu.VMEM((WINDOW, _SWA_KW), F32), pltpu.VMEM((WINDOW, _SWA_KW), F32)],
        compiler_params=_cp(1), name=name)(proj, proj, proj, proj, proj, o, lse, dycat, sinks)


def _exchange(arrays, *, scatter, name):
    n = len(arrays)

    def body(*refs):
        ins, outs = refs[:n], refs[n:2 * n]
        send_sems, recv_sems, loc_sems = refs[2 * n:]
        x, y, c = lax.axis_index("x"), lax.axis_index("y"), lax.axis_index("c")
        me = 4 * x + 2 * y + c

        def src(i, dest):
            return ins[i].at[dest] if scatter else ins[i]

        local = [pltpu.make_async_copy(src(i, me), outs[i].at[me], loc_sems.at[i]) for i in range(n)]
        for cp in local:
            cp.start()
        sends, recvs = [], []
        for k in range(1, NDEV):
            px = 1 - x if k & 4 else x
            py = 1 - y if k & 2 else y
            pc = 1 - c if k & 1 else c
            peer = 4 * px + 2 * py + pc
            for i in range(n):
                common = dict(send_sem=send_sems.at[i, k - 1], recv_sem=recv_sems.at[i, k - 1],
                              device_id=(px, py, pc), device_id_type=pl.DeviceIdType.MESH)
                sends.append(pltpu.make_async_remote_copy(src_ref=src(i, peer), dst_ref=outs[i].at[me], **common))
                recvs.append(pltpu.make_async_remote_copy(src_ref=src(i, peer), dst_ref=outs[i].at[peer], **common))
        for cp in sends:
            cp.start()
        for cp in recvs:
            cp.wait_recv()
        for cp in sends:
            cp.wait_send()
        for cp in local:
            cp.wait()

    hbm = pl.BlockSpec(memory_space=pl.ANY)
    out_shape = tuple(jax.ShapeDtypeStruct(a.shape if scatter else (NDEV,) + a.shape, a.dtype) for a in arrays)
    return pl.pallas_call(
        body, in_specs=[hbm] * n, out_specs=tuple([hbm] * n), out_shape=out_shape,
        scratch_shapes=[pltpu.SemaphoreType.DMA((n, NDEV - 1)), pltpu.SemaphoreType.DMA((n, NDEV - 1)),
                        pltpu.SemaphoreType.DMA((n,))],
        name=name)(*arrays)


def _adamw(w, m, v, parts, *, name):
    R, C = w.shape
    npart = parts.shape[0]
    cap = max(SUB, ((1 << 18) // C) // SUB * SUB)
    tr = _tile(R, cap, SUB)

    def body(w_ref, m_ref, v_ref, p_ref, g_ref, d_ref, mo_ref, vo_ref):
        g = p_ref[0]
        for k in range(1, npart):
            g = g + p_ref[k]
        mn = ADAM_B1 * m_ref[...] + (1.0 - ADAM_B1) * g
        vn = ADAM_B2 * v_ref[...] + (1.0 - ADAM_B2) * (g * g)
        m_hat = mn / (1.0 - ADAM_B1 ** ADAM_STEP)
        v_hat = vn / (1.0 - ADAM_B2 ** ADAM_STEP)
        g_ref[...] = g
        d_ref[...] = -ADAM_LR * (m_hat / (jnp.sqrt(v_hat) + ADAM_EPS) + ADAM_WD * w_ref[...])
        mo_ref[...] = mn
        vo_ref[...] = vn

    spec = pl.BlockSpec((tr, C), lambda i: (i, 0))
    oshape = jax.ShapeDtypeStruct((R, C), F32)
    return pl.pallas_call(
        body, grid=(R // tr,), in_specs=[spec] * 3 + [pl.BlockSpec((npart, tr, C), lambda i: (0, i, 0))],
        out_specs=(spec,) * 4, out_shape=(oshape,) * 4, compiler_params=_cp(1), name=name)(w, m, v, parts)


def _adamw_layer(l, w, m, v, parts, prev, *, name):
    L, R, C = w.shape
    npart = parts.shape[0]
    cap = max(SUB, ((1 << 18) // C) // SUB * SUB)
    tr = _tile(R, cap, SUB)
    nprev = 0 if prev is None else 4

    def body(*refs):
        w_ref, m_ref, v_ref, p_ref = refs[:4]
        g_ref, d_ref, mo_ref, vo_ref = refs[4 + nprev:]
        g = p_ref[0]
        for k in range(1, npart):
            g = g + p_ref[k]
        mn = ADAM_B1 * m_ref[...] + (1.0 - ADAM_B1) * g
        vn = ADAM_B2 * v_ref[...] + (1.0 - ADAM_B2) * (g * g)
        m_hat = mn / (1.0 - ADAM_B1 ** ADAM_STEP)
        v_hat = vn / (1.0 - ADAM_B2 ** ADAM_STEP)
        g_ref[...] = g
        d_ref[...] = -ADAM_LR * (m_hat / (jnp.sqrt(v_hat) + ADAM_EPS) + ADAM_WD * w_ref[...])
        mo_ref[...] = mn
        vo_ref[...] = vn

    spec = pl.BlockSpec((None, tr, C), lambda i: (l, i, 0))
    oshape = jax.ShapeDtypeStruct((L, R, C), F32)
    return pl.pallas_call(
        body, grid=(R // tr,),
        in_specs=[spec] * 3 + [pl.BlockSpec((npart, tr, C), lambda i: (0, i, 0))]
        + [pl.BlockSpec(memory_space=pl.ANY)] * nprev,
        out_specs=(spec,) * 4, out_shape=(oshape,) * 4,
        input_output_aliases={4 + k: k for k in range(nprev)},
        compiler_params=_cp(1), name=name)(w, m, v, parts, *(prev or ()))


_HBM = pl.BlockSpec(memory_space=pltpu.HBM)
_SEM = pl.BlockSpec(memory_space=pltpu.SEMAPHORE)
_EFFECT = pltpu.SideEffectType.DATAFLOW_SIDE_EFFECTING


def _peers():
    x, y, c = lax.axis_index("x"), lax.axis_index("y"), lax.axis_index("c")
    out = []
    for k in range(1, NDEV):
        px = 1 - x if k & 4 else x
        py = 1 - y if k & 2 else y
        pc = 1 - c if k & 1 else c
        out.append((k - 1, (px, py, pc), 4 * px + 2 * py + pc))
    return 4 * x + 2 * y + c, out


def _xchg_start(arrays, *, scatter, name):
    n = len(arrays)
    lands = [lax.empty(a.shape if scatter else (NDEV,) + a.shape, a.dtype) for a in arrays]

    def body(*refs):
        ins, lnd = refs[:n], refs[n:2 * n]
        send_sems, recv_sems = refs[2 * n], refs[2 * n + 1]
        token = refs[-1]
        me, peers = _peers()
        for k, dev, peer in peers:
            for i in range(n):
                pltpu.make_async_remote_copy(
                    src_ref=ins[i].at[peer] if scatter else ins[i], dst_ref=lnd[i].at[me],
                    send_sem=send_sems.at[i * (NDEV - 1) + k], recv_sem=recv_sems.at[i * (NDEV - 1) + k],
                    device_id=dev, device_id_type=pl.DeviceIdType.MESH).start()
        token[...] = jnp.zeros_like(token)

    sems = pltpu.SemaphoreType.DMA((n * (NDEV - 1),))
    res = pl.pallas_call(
        body, name=name,
        out_shape=(sems, sems) + tuple(pltpu.HBM(t.shape, t.dtype) for t in list(arrays) + lands)
        + (jax.ShapeDtypeStruct((SUB, LANE), F32),),
        in_specs=[_HBM] * (2 * n), out_specs=(_SEM, _SEM) + (_HBM,) * (2 * n) + (pl.BlockSpec(memory_space=pltpu.VMEM),),
        input_output_aliases={i: 2 + i for i in range(2 * n)},
        compiler_params=pltpu.CompilerParams(has_side_effects=_EFFECT),
    )(*[pltpu.with_memory_space_constraint(t, pltpu.HBM) for t in list(arrays) + lands])
    return dict(send=res[0], recv=res[1], thru=list(res[2:2 + 2 * n]), token=res[-1], scatter=scatter, n=n)


def _xchg_wait(handle, after, *, name):
    n, scatter = handle["n"], handle["scatter"]
    thru = handle["thru"]

    def body(*refs):
        ins, lnd = refs[:n], refs[n:2 * n]
        send_sems, recv_sems = refs[2 * n], refs[2 * n + 1]
        me, peers = _peers()
        for k, dev, peer in peers:
            for i in range(n):
                cp = pltpu.make_async_remote_copy(
                    src_ref=ins[i].at[peer] if scatter else ins[i], dst_ref=lnd[i].at[peer],
                    send_sem=send_sems.at[i * (NDEV - 1) + k], recv_sem=recv_sems.at[i * (NDEV - 1) + k],
                    device_id=dev, device_id_type=pl.DeviceIdType.MESH)
                cp.wait_send()
                cp.wait_recv()

    res = pl.pallas_call(
        body, name=name, out_shape=tuple(pltpu.HBM(t.shape, t.dtype) for t in thru),
        in_specs=[_HBM] * (2 * n) + [_SEM, _SEM, pl.BlockSpec(memory_space=pl.ANY)], out_specs=(_HBM,) * (2 * n),
        input_output_aliases={i: i for i in range(2 * n)},
        compiler_params=pltpu.CompilerParams(has_side_effects=_EFFECT),
    )(*thru, handle["send"], handle["recv"], after)
    return list(res[:n]), list(res[n:])


def _pad_heads(w, nh, hd, axis=-1):
    axis = axis % w.ndim
    shp = w.shape
    w = w.reshape(shp[:axis] + (nh, hd) + shp[axis + 1:])
    pads = [(0, 0)] * w.ndim
    pads[axis + 1] = (0, LANE - hd)
    return jnp.pad(w, pads).reshape(shp[:axis] + (nh * LANE,) + shp[axis + 1:])


def _unpad_heads(w, nh, hd, axis=-1):
    axis = axis % w.ndim
    shp = w.shape
    w = w.reshape(shp[:axis] + (nh, LANE) + shp[axis + 1:])
    w = lax.slice_in_dim(w, 0, hd, axis=axis + 1)
    return w.reshape(shp[:axis] + (nh * hd,) + shp[axis + 1:])


_O_DT = SSD_INNER + SSD_XBC
_O_CQ = _O_DT + SSD_HEADS
_O_CKV = _O_CQ + MLA_QR
_O_KR = _O_CKV + MLA_KVR
_O_SQ = _O_KR + MLA_ROPE
_O_SK = _O_SQ + SWA_HEADS * SWA_HD
_O_SV = _O_SK + SWA_KV * SWA_HD


def _w_in_to_padded(w, axis=-1):
    axis = axis % w.ndim
    cut = lambda a, b: lax.slice_in_dim(w, a, b, axis=axis)
    z, xbc, dt = cut(0, SSD_INNER), cut(SSD_INNER, _O_DT), cut(_O_DT, _O_CQ)
    cq, ckv, kr = cut(_O_CQ, _O_CKV), cut(_O_CKV, _O_KR), cut(_O_KR, _O_SQ)
    sq, sk, sv = cut(_O_SQ, _O_SK), cut(_O_SK, _O_SV), cut(_O_SV, D_IN)
    zeros = lambda n: jnp.zeros(w.shape[:axis] + (n,) + w.shape[axis + 1:], w.dtype)
    return jnp.concatenate([xbc, z, cq, ckv, dt, zeros(ROPE_LANE - SSD_HEADS), kr, zeros(LANE - ROPE_LANE - MLA_ROPE),
                            _pad_heads(sq, SWA_HEADS, SWA_HD, axis), _pad_heads(sk, SWA_KV, SWA_HD, axis),
                            _pad_heads(sv, SWA_KV, SWA_HD, axis)], axis=axis)


def _w_in_from_padded(g, axis=-1):
    axis = axis % g.ndim
    cut = lambda a, b: lax.slice_in_dim(g, a, b, axis=axis)
    xbc, z, cq, ckv = cut(C_XBC, C_Z), cut(C_Z, C_CQ), cut(C_CQ, C_CKV), cut(C_CKV, C_MISC)
    dt, kr = cut(C_MISC, C_MISC + SSD_HEADS), cut(C_MISC + ROPE_LANE, C_MISC + ROPE_LANE + MLA_ROPE)
    sq = _unpad_heads(cut(C_SQ, C_SK), SWA_HEADS, SWA_HD, axis)
    sk = _unpad_heads(cut(C_SK, C_SV), SWA_KV, SWA_HD, axis)
    sv = _unpad_heads(cut(C_SV, D_INP), SWA_KV, SWA_HD, axis)
    return jnp.concatenate([z, xbc, dt, cq, ckv, kr, sq, sk, sv], axis=axis)


def _w_out_to_padded(w):
    a = SSD_INNER
    b = a + MLA_HEADS * MLA_V
    return jnp.concatenate([w[..., :a, :], _pad_heads(w[..., a:b, :], MLA_HEADS, MLA_V, axis=-2),
                            _pad_heads(w[..., b:, :], SWA_HEADS, SWA_HD, axis=-2)], axis=-2)


def _w_out_from_padded(g):
    a = SSD_INNER
    b = a + MLA_HEADS * LANE
    return jnp.concatenate([g[..., :a, :], _unpad_heads(g[..., a:b, :], MLA_HEADS, MLA_V, axis=-2),
                            _unpad_heads(g[..., b:, :], SWA_HEADS, SWA_HD, axis=-2)], axis=-2)


def _w_ukv_to_padded(w):
    w4 = w.reshape(w.shape[:-1] + (MLA_HEADS, MLA_NOPE + MLA_V))
    flat = lambda t: t.reshape(w.shape[:-1] + (MLA_HEADS * t.shape[-1],))
    return jnp.concatenate([_pad_heads(flat(w4[..., :MLA_NOPE]), MLA_HEADS, MLA_NOPE),
                            _pad_heads(flat(w4[..., MLA_NOPE:]), MLA_HEADS, MLA_V)], axis=-1)


def _w_ukv_from_padded(g):
    hw = MLA_HEADS * LANE
    gk = _unpad_heads(g[..., :hw], MLA_HEADS, MLA_NOPE).reshape(g.shape[:-1] + (MLA_HEADS, MLA_NOPE))
    gv = _unpad_heads(g[..., hw:], MLA_HEADS, MLA_V).reshape(g.shape[:-1] + (MLA_HEADS, MLA_V))
    return jnp.concatenate([gk, gv], axis=-1).reshape(g.shape[:-1] + (MLA_HEADS * (MLA_NOPE + MLA_V),))


def _pad_lane(v):
    return jnp.pad(v, [(0, 0)] * (v.ndim - 1) + [(0, LANE - v.shape[-1])])


def _rope_tables(positions):
    inv_freq = ROPE_THETA ** (-jnp.arange(0, MLA_ROPE, 2, dtype=F32) / MLA_ROPE)
    ang = positions.astype(F32).reshape(-1, 1) * inv_freq
    cos, sin = jnp.cos(ang), jnp.sin(ang)
    S = ang.shape[0]
    one, zero = jnp.ones((S, ROPE_LANE), F32), jnp.zeros((S, ROPE_LANE), F32)
    tail1, tail0 = jnp.ones((S, LANE - ROPE_LANE - MLA_ROPE), F32), jnp.zeros((S, LANE - ROPE_LANE - MLA_ROPE), F32)
    z16 = jnp.zeros_like(sin)
    return (jnp.concatenate([one, cos, cos, tail1], axis=1), jnp.concatenate([zero, -sin, z16, tail0], axis=1),
            jnp.concatenate([zero, z16, sin, tail0], axis=1))


def _layer_fwd(l, x_in, f_prev, gate_prev, mod, P, tabs):
    sh1, sc1, g1, sh2, sc2, g2 = [mod[k:k + 1] for k in range(6)]
    tag = f"l{l}_"
    if f_prev is None:
        x0 = x_in
        h1 = _norm_fwd(x0, P["n1g"], sc1, sh1, name=tag + "norm1")
    else:
        x0, h1 = _norm_fwd(x_in, P["n1g"], sc1, sh1, f=f_prev, gate=gate_prev, name=tag + "norm1")
    proj = _mm(h1, P["w_in"], tb=True, name=tag + "proj")
    P.update(P.pop("mid")(proj))
    y_ssd, hin = _ssd_fwd(proj, P["ssd_cw"], P["ssd_cb"], P["dtb"], P["alog"], P["dsk"],
                          P["ssd_ng"], name=tag + "ssd")
    q, k, v, vt = _mla_prep_fwd(proj, *tabs, P["gq"], P["gkv"], P["w_uq"], P["w_ukv"], name=tag + "mla_prep")
    o_mla, lse_mla, lset_mla = _attn_fwd(q, k, vt, name=tag + "mla_attn")
    o_swa, lse_swa = _swa_fwd(proj, P["sinks"], name=tag + "swa")
    ycat = jnp.concatenate([y_ssd.astype(_ACT), o_mla.astype(_ACT), o_swa.astype(_ACT)], axis=1)
    y = _mm(ycat, P["w_out"], name=tag + "out")
    P.update(P.pop("late")(y))
    x1, h2 = _norm_fwd(x0, P["n2g"], sc2, sh2, f=y, gate=g1, name=tag + "norm2")
    up = _mm(h2, P["w_up"], tb=True, name=tag + "up")
    act = _ffn_act_fwd(up, P["fcw"], P["fcb"], name=tag + "ffn_act")
    f = _mm(act, P["w_down"], name=tag + "down")
    saved = dict(x0=x0, h1=h1, proj=proj, hin=hin, q=q, k=k, v=v, o_mla=o_mla, lse_mla=lse_mla, lset_mla=lset_mla, o_swa=o_swa,
                 lse_swa=lse_swa, ycat=ycat, y=y, x1=x1, h2=h2, up=up, act=act, f=f, mod=mod)
    return x1, f, g2, saved


def _layer_bwd(l, dxo, sv, P, tabs, on_part):
    mod = sv["mod"]
    sh1, sc1, g1, sh2, sc2, g2 = [mod[k:k + 1] for k in range(6)]
    tag = f"l{l}_b_"
    G = {}
    df, dg2 = _gate_bwd(dxo, sv["f"], g2, name=tag + "gate2")
    dact = _mm(df, P["w_down"], tb=True, name=tag + "dact")
    G["w_down"] = _mm(sv["act"], df, ta=True, name=tag + "dw_down")
    dup, G["fcw"], G["fcb"] = _ffn_bwd(sv["up"], dact, P["fcw"], P["fcb"], name=tag + "ffn")
    dh2 = _mm(dup, P["w_up"], name=tag + "dh2")
    G["w_up"] = _mm(dup, sv["h2"], ta=True, name=tag + "dw_up")
    token = on_part(l, "ffn", G)
    if token is not None:
        sc2 = sc2 + token
    dx1, G["n2g"], dsc2, dsh2 = _norm_bwd(dh2, sv["x1"], dxo, P["n2g"], sc2, name=tag + "norm2")
    dy, dg1 = _gate_bwd(dx1, sv["y"], g1, name=tag + "gate1")
    dycat = _mm(dy, P["w_out"], tb=True, name=tag + "dycat")
    G["w_out"] = _mm(sv["ycat"], dy, ta=True, name=tag + "dw_out")
    token = on_part(l, "out", G)
    ssd_cb = P["ssd_cb"] if token is None else P["ssd_cb"] + token
    proj = sv["proj"]
    (dpre, dz, dmisc_ssd, G["ssd_cw"], G["ssd_cb"], G["dtb"], G["alog"], G["dsk"], G["ssd_ng"]) = _ssd_bwd(
        proj, dycat, sv["hin"], P["ssd_cw"], ssd_cb, P["dtb"], P["alog"], P["dsk"],
        P["ssd_ng"], name=tag + "ssd")
    att = (sv["q"], sv["k"], sv["v"], sv["o_mla"])
    dq = _attn_bwd_dq(*att, sv["lse_mla"], dycat, name=tag + "mla_dq")
    dk, dv = _attn_bwd_dkv(*att, sv["lset_mla"], dycat, name=tag + "mla_dkv")
    dcq, dckv, dmisc, dqh, dkv, qn, kvn, G["gq"], G["gkv"] = _mla_prep_bwd(
        proj, dq, dk, dv, dmisc_ssd, *tabs, P["gq"], P["gkv"], P["w_uq"], P["w_ukv"], name=tag + "mla_prep")
    G["w_uq"] = _mm(qn, dqh, ta=True, name=tag + "dw_uq")
    G["w_ukv"] = _mm(kvn, dkv, ta=True, name=tag + "dw_ukv")
    dsq, dsk_, dsv_, G["sinks"] = _swa_bwd(proj, sv["o_swa"], sv["lse_swa"], dycat, P["sinks"], name=tag + "swa")
    dproj = jnp.concatenate([dpre, dz, dcq, dckv, dmisc, dsq, dsk_, dsv_], axis=1)
    G["w_in"] = _mm(dproj, sv["h1"], ta=True, name=tag + "dw_in")
    token = on_part(l, "mixer", G)
    if token is not None:
        sc1 = sc1 + token
    dh1 = _mm(dproj, P["w_in"], name=tag + "dh1")
    dx0, G["n1g"], dsc1, dsh1 = _norm_bwd(dh1, sv["x0"], dx1, P["n1g"], sc1, name=tag + "norm1")
    G["mod"] = jnp.concatenate([dsh1, dsc1, dg1, dsh2, dsc2, dg2], axis=0)
    return dx0, G


def _local_step(x, tgt, mods, get_params, tabs, final_g, on_grads, on_part):
    saved, params = [], []
    xin, f, gate = x, None, None
    for l in range(DEPTH):
        params.append(get_params(l, x if f is None else f))
        xin, f, gate, sv = _layer_fwd(l, xin, f, gate, mods[l], params[l], tabs)
        saved.append(sv)
    loss, dx, dfinal = _final_loss(xin, f, gate, final_g, tgt, name="final_loss")
    for l in reversed(range(DEPTH)):
        dx, G = _layer_bwd(l, dx, saved[l], params[l], tabs, on_part)
        on_grads(l, G)
    return loss[0, 0], dx, dfinal


_WEIGHTS = ['ada_w', 'ada_b', 'norm1_g', 'norm2_g', 'w_in', 'ssd_conv_w', 'ssd_conv_b', 'ssd_dt_bias', 'ssd_a_log',
            'ssd_d', 'ssd_norm_g', 'mla_q_norm_g', 'mla_w_uq', 'mla_kv_norm_g', 'mla_w_ukv', 'swa_sinks', 'w_out',
            'ffn_w_up', 'ffn_conv_w', 'ffn_conv_b', 'ffn_w_down', 'final_norm_g']
_INPUTS = ['x', 'c', 'positions'] + _WEIGHTS + ['loss_target'] + ['m_' + n for n in _WEIGHTS] + ['v_' + n for n in _WEIGHTS]
_SMALL = [('ada_b', 'mod'), ('norm1_g', 'n1g'), ('norm2_g', 'n2g'), ('ssd_conv_b', 'ssd_cb'), ('ssd_dt_bias', 'dtb'),
          ('ssd_a_log', 'alog'), ('ssd_d', 'dsk'), ('ssd_norm_g', 'ssd_ng'), ('mla_q_norm_g', 'gq'),
          ('mla_kv_norm_g', 'gkv'), ('swa_sinks', 'sinks'), ('ffn_conv_b', 'fcb')]
_SHARDED = [('w_in', 'w_in', 2), ('ssd_conv_w', 'ssd_cw', 2), ('mla_w_uq', 'w_uq', 2), ('mla_w_ukv', 'w_ukv', 2),
            ('w_out', 'w_out', 1), ('ffn_w_up', 'w_up', 2), ('ffn_conv_w', 'fcw', 2), ('ffn_w_down', 'w_down', 1)]
_SHARDED_NAMES = [n for n, _, _ in _SHARDED]
_TRANSPOSED = ('w_in', 'ffn_w_up')


def _pack_small(per_layer, final):
    parts = []
    for name, _ in _SMALL:
        v = per_layer[name]
        v = v.reshape(DEPTH, -1)
        pad = (-v.shape[1]) % LANE
        parts.append(jnp.pad(v, ((0, 0), (0, pad))).reshape(-1))
    parts.append(final.reshape(-1))
    return jnp.concatenate(parts).reshape(-1, LANE)


def _unpack_small(packed, shapes):
    flat = packed.reshape(-1)
    out, off = {}, 0
    for name, _ in _SMALL:
        n = math.prod(shapes[name][1:])
        npad = n + (-n) % LANE
        out[name] = flat[off:off + DEPTH * npad].reshape(DEPTH, npad)[:, :n].reshape(shapes[name])
        off += DEPTH * npad
    out['final_norm_g'] = flat[off:off + D]
    return out


def _shard_major(g, axis):
    shp = g.shape
    g = g.reshape(shp[:axis] + (NDEV, shp[axis] // NDEV) + shp[axis + 1:])
    return jnp.moveaxis(g, axis, 0)


def _unshard(g, axis):
    g = jnp.moveaxis(g, 0, axis)
    shp = g.shape
    return g.reshape(shp[:axis] + (shp[axis] * shp[axis + 1],) + shp[axis + 2:])


def kernel(x, c, positions, ada_w, ada_b, norm1_g, norm2_g, w_in, ssd_conv_w, ssd_conv_b, ssd_dt_bias, ssd_a_log, ssd_d, ssd_norm_g, mla_q_norm_g, mla_w_uq, mla_kv_norm_g, mla_w_ukv, swa_sinks, w_out, ffn_w_up, ffn_conv_w, ffn_conv_b, ffn_w_down, final_norm_g, loss_target, m_ada_w, m_ada_b, m_norm1_g, m_norm2_g, m_w_in, m_ssd_conv_w, m_ssd_conv_b, m_ssd_dt_bias, m_ssd_a_log, m_ssd_d, m_ssd_norm_g, m_mla_q_norm_g, m_mla_w_uq, m_mla_kv_norm_g, m_mla_w_ukv, m_swa_sinks, m_w_out, m_ffn_w_up, m_ffn_conv_w, m_ffn_conv_b, m_ffn_w_down, m_final_norm_g, v_ada_w, v_ada_b, v_norm1_g, v_norm2_g, v_w_in, v_ssd_conv_w, v_ssd_conv_b, v_ssd_dt_bias, v_ssd_a_log, v_ssd_d, v_ssd_norm_g, v_mla_q_norm_g, v_mla_w_uq, v_mla_kv_norm_g, v_mla_w_ukv, v_swa_sinks, v_w_out, v_ffn_w_up, v_ffn_conv_w, v_ffn_conv_b, v_ffn_w_down, v_final_norm_g):
    a = dict(zip(_INPUTS, (x, c, positions, ada_w, ada_b, norm1_g, norm2_g, w_in, ssd_conv_w, ssd_conv_b, ssd_dt_bias, ssd_a_log, ssd_d, ssd_norm_g, mla_q_norm_g, mla_w_uq, mla_kv_norm_g, mla_w_ukv, swa_sinks, w_out, ffn_w_up, ffn_conv_w, ffn_conv_b, ffn_w_down, final_norm_g, loss_target, m_ada_w, m_ada_b, m_norm1_g, m_norm2_g, m_w_in, m_ssd_conv_w, m_ssd_conv_b, m_ssd_dt_bias, m_ssd_a_log, m_ssd_d, m_ssd_norm_g, m_mla_q_norm_g, m_mla_w_uq, m_mla_kv_norm_g, m_mla_w_ukv, m_swa_sinks, m_w_out, m_ffn_w_up, m_ffn_conv_w, m_ffn_conv_b, m_ffn_w_down, m_final_norm_g, v_ada_w, v_ada_b, v_norm1_g, v_norm2_g, v_w_in, v_ssd_conv_w, v_ssd_conv_b, v_ssd_dt_bias, v_ssd_a_log, v_ssd_d, v_ssd_norm_g, v_mla_q_norm_g, v_mla_w_uq, v_mla_kv_norm_g, v_mla_w_ukv, v_swa_sinks, v_w_out, v_ffn_w_up, v_ffn_conv_w, v_ffn_conv_b, v_ffn_w_down, v_final_norm_g)))
    axes = ("x", "y", "c")
    me = 4 * lax.axis_index("x") + 2 * lax.axis_index("y") + lax.axis_index("c")
    ncol = ada_w.shape[-1]

    c_all = _exchange([c], scatter=False, name="gather_c")[0]
    c_act = _silu_call(c_all.reshape(NDEV, D), name="c_act")
    mod_part = jnp.stack([_mm(c_act, ada_w[l], name=f"mod{l}") for l in range(DEPTH)])
    mod_all = _exchange([mod_part], scatter=False, name="gather_mod")[0]
    mod_mine = lax.dynamic_index_in_dim(mod_all, me, axis=2, keepdims=False)
    mods = (jnp.moveaxis(mod_mine, 0, 1).reshape(DEPTH, 6 * D) + ada_b).reshape(DEPTH, 6, D)
    tabs = _rope_tables(positions)

    mxu_names = ('w_in', 'mla_w_uq', 'mla_w_ukv', 'w_out', 'ffn_w_up', 'ffn_w_down')
    kform = lambda n, t: jnp.swapaxes(t, -1, -2) if n in _TRANSPOSED else t
    shard_of = {n: (key, 1 if n in _TRANSPOSED else ax) for n, key, ax in _SHARDED}
    gather_groups = (("early", _SHARDED_NAMES[:4]), ("mid", _SHARDED_NAMES[4:5]), ("late", _SHARDED_NAMES[5:]))
    mods, raw = lax.optimization_barrier((mods, {n: a[n] for n in _SHARDED_NAMES}))
    own_of = lambda names, l: [kform(n, raw[n][l]).astype(_MXU) if n in mxu_names else raw[n][l] for n in names]
    gathers, prev = [], None
    for l in range(DEPTH):
        gathers.append({})
        for grp, names in gather_groups:
            srcs = own_of(names, l)
            if prev is not None:
                srcs, _ = lax.optimization_barrier((srcs, prev))
            gathers[l][grp] = _xchg_start(srcs, scatter=False, name=f"gather_start_{grp}{l}")
            prev = gathers[l][grp]["token"]

    def place_own(landed, mine):
        return [lax.dynamic_update_index_in_dim(t, o, me, 0) for t, o in zip(landed, mine)]

    def gathered(l, grp, after):
        names = dict(gather_groups)[grp]
        mine, landed = _xchg_wait(gathers[l][grp], after, name=f"gather_wait_{grp}{l}")
        return {n: _unshard(g, shard_of[n][1] - 1) for n, g in zip(names, place_own(landed, mine))}

    def get_params(l, after):
        full = gathered(l, "early", after)
        vec = lambda t: t[l].reshape(1, -1)

        def mid(after2):
            return dict(w_out=_w_out_to_padded(gathered(l, "mid", after2)['w_out']))

        def late(after2):
            rest = gathered(l, "late", after2)
            return dict(w_up=rest['ffn_w_up'], w_down=rest['ffn_w_down'], fcw=rest['ffn_conv_w'])

        return dict(
            w_in=_w_in_to_padded(full['w_in'], axis=0), w_uq=_pad_heads(full['mla_w_uq'], MLA_HEADS, MLA_NOPE + MLA_ROPE),
            w_ukv=_w_ukv_to_padded(full['mla_w_ukv']), ssd_cw=full['ssd_conv_w'], mid=mid, late=late,
            ssd_cb=vec(ssd_conv_b), dtb=vec(_pad_lane(ssd_dt_bias)), alog=vec(_pad_lane(ssd_a_log)),
            dsk=vec(_pad_lane(ssd_d)), ssd_ng=vec(ssd_norm_g), gq=vec(mla_q_norm_g), gkv=vec(mla_kv_norm_g),
            sinks=vec(_pad_lane(swa_sinks)), fcb=vec(ffn_conv_b), n1g=vec(norm1_g), n2g=vec(norm2_g))

    unpad = dict(w_in=functools.partial(_w_in_from_padded, axis=0), w_out=_w_out_from_padded, w_ukv=_w_ukv_from_padded,
                 w_uq=lambda g: _unpad_heads(g, MLA_HEADS, MLA_NOPE + MLA_ROPE))
    scatter_groups = (("ffn", _SHARDED_NAMES[5:]), ("out", _SHARDED_NAMES[4:5]), ("mixer", _SHARDED_NAMES[:4]))
    grads = [None] * DEPTH
    scatters = [dict() for _ in range(DEPTH)]

    def on_part(l, grp, G):
        parts = [_shard_major(unpad.get(shard_of[n][0], lambda g: g)(G[shard_of[n][0]]), shard_of[n][1] - 1)
                 for n in dict(scatter_groups)[grp]]
        scatters[l][grp] = _xchg_start(parts, scatter=True, name=f"scatter_start_{grp}{l}")
        return scatters[l][grp]["token"][0, 0]

    def on_grads(l, G):
        grads[l] = G

    mods = mods + sum(g[grp]["token"][0, 0] for g in gathers for grp, _ in gather_groups)
    loss, dx, dfinal = _local_step(x[0], loss_target[0], mods, get_params, tabs, final_norm_g.reshape(1, D),
                                   on_grads, on_part)
    loss = lax.psum(loss, axes)

    stack = lambda key: jnp.stack([grads[l][key] for l in range(DEPTH)])
    small_g = {name: stack(key).reshape(DEPTH, -1) for name, key in _SMALL}
    small_parts = _exchange([_pack_small(small_g, dfinal)], scatter=False, name="gather_small")[0]

    out_g, out_d, out_m, out_v = {}, {}, {}, {}
    chain = {name: None for name in _SHARDED_NAMES}
    for l in reversed(range(DEPTH)):
        for grp, names in scatter_groups:
            mine, landed = _xchg_wait(scatters[l][grp], dx, name=f"scatter_wait_{grp}{l}")
            parts = place_own(landed, [lax.dynamic_index_in_dim(t, me, 0, keepdims=False) for t in mine])
            for name, pv in zip(names, parts):
                chain[name] = _adamw_layer(l, kform(name, a[name]), kform(name, a['m_' + name]),
                                           kform(name, a['v_' + name]), pv, chain[name], name=f"adamw_{name}{l}")
    for name in _SHARDED_NAMES:
        out_g[name], out_d[name], out_m[name], out_v[name] = [kform(name, t) for t in chain[name]]

    def update(name, wv, mv, vv, pv):
        shp = wv.shape
        r = lambda t: t.reshape((-1, shp[-1]))
        res = _adamw(r(wv), r(mv), r(vv), pv.reshape((pv.shape[0], -1, shp[-1])), name="adamw_" + name)
        out_g[name], out_d[name], out_m[name], out_v[name] = [t.reshape(shp) for t in res]

    n_ada = DEPTH * 6 * D // LANE
    dmod_all = small_parts[:, :n_ada].reshape(NDEV, DEPTH, 6 * D)
    dmod_mine = lax.dynamic_slice_in_dim(dmod_all, me * ncol, ncol, axis=2)
    g_ada = jnp.stack([_mm(c_act, dmod_mine[:, l], ta=True, name=f"dw_ada{l}") for l in range(DEPTH)])
    update('ada_w', ada_w, m_ada_w, v_ada_w, g_ada[None])
    shapes = {n: a[n].shape for n, _ in _SMALL}
    pk = lambda pre: _pack_small({n: a[pre + n] for n, _ in _SMALL}, a[pre + 'final_norm_g'])
    res = _adamw(pk(''), pk('m_'), pk('v_'), small_parts, name="adamw_small")
    for dst, t in zip((out_g, out_d, out_m, out_v), res):
        dst.update(_unpack_small(t, shapes))

    outs = [loss, dx[None]]
    for dct in (out_g, out_d, out_m, out_v):
        outs += [dct[n] for n in _WEIGHTS]
    return tuple(outs)
```

```python
import functools
import math

import jax
import jax.numpy as jnp
from jax import lax
from jax.experimental import pallas as pl
from jax.experimental.pallas import tpu as pltpu

F32 = jnp.float32
_MXU = jnp.bfloat16
_ACT = jnp.bfloat16
_HI = lax.Precision.HIGHEST
EPS = 1e-6
NDEV = 8
DEPTH = 4
D = 1024
LANE = 128
SUB = 8
VMEM_LIMIT = 56 * 1024 * 1024

SSD_INNER, SSD_STATE, SSD_HEADS, SSD_GROUPS, SSD_CHUNK, SSD_CONV = 512, 128, 8, 2, 128, 4
SSD_XBC = SSD_INNER + 2 * SSD_GROUPS * SSD_STATE
MLA_HEADS, MLA_NOPE, MLA_ROPE, MLA_V, MLA_QR, MLA_KVR = 4, 64, 32, 64, 256, 128
SWA_HEADS, SWA_KV, SWA_HD, WINDOW = 4, 2, 64, 128
D_FF, FFN_CONV = 2816, 3
D_IN = 2472
ROPE_THETA = 10000.0
C_XBC, C_Z, C_CQ, C_CKV, C_MISC, C_SQ, C_SK, C_SV, D_INP = 0, 1024, 1536, 1792, 1920, 2048, 2560, 2816, 3072
ROPE_LANE = 64
D_MIXP = 1536

ADAM_LR, ADAM_B1, ADAM_B2, ADAM_EPS, ADAM_WD, ADAM_STEP = 0.001, 0.9, 0.999, 1e-08, 0.01, 10

TS_ROW = 1024
TS_FFN = 256
TQ_ATT = 1024
TS_SWA = 512


def _tile(n, cap, q=LANE):
    best = None
    for t in range(q, min(n, cap) + 1, q):
        if n % t == 0:
            best = t
    return n if best is None else best


def _cp(ngrid):
    return pltpu.CompilerParams(dimension_semantics=("arbitrary",) * ngrid, vmem_limit_bytes=VMEM_LIMIT)


def _dot(a, b):
    return jnp.dot(a.astype(_MXU), b.astype(_MXU), preferred_element_type=F32)


def _dot_nt(a, b):
    return lax.dot_general(a.astype(_MXU), b.astype(_MXU), (((1,), (1,)), ((), ())), preferred_element_type=F32)


def _dot_tn(a, b):
    return jnp.dot(a.T.astype(_MXU), b.astype(_MXU), preferred_element_type=F32)


def _sigmoid(x):
    return 1.0 / (1.0 + jnp.exp(-x))


def _sigmoid_t(x):
    return 0.5 * jnp.tanh(0.5 * x) + 0.5


def _silu(x):
    return x * _sigmoid_t(x)


def _silu_grad(x):
    s = _sigmoid_t(x)
    return x * s, s * (1.0 + x * (1.0 - s))


def _dsilu(x):
    return _silu_grad(x)[1]


def _softplus(x):
    u = jnp.exp(-jnp.abs(x))
    w = 1.0 + u
    log1p = jnp.where(w == 1.0, u, jnp.log(w) * u / jnp.where(w == 1.0, 1.0, w - 1.0))
    return jnp.maximum(x, 0.0) + log1p


def _colsum(x):
    return jnp.sum(x, axis=0, keepdims=True)


def _rowsum(x):
    return jnp.sum(x, axis=1, keepdims=True)


def _shift_down(t, halo, j):
    if j == 0:
        return t
    n = t.shape[0]
    rolled = pltpu.roll(t, j, 0)
    row = lax.broadcasted_iota(jnp.int32, (SUB, t.shape[1]), 0)
    first = jnp.where(row < j, pltpu.roll(halo, j, 0), rolled[:SUB])
    return jnp.concatenate([first, rolled[SUB:]], axis=0) if n > SUB else first


def _shift_up(t, halo, j):
    if j == 0:
        return t
    n = t.shape[0]
    rolled = pltpu.roll(t, n - j, 0)
    row = lax.broadcasted_iota(jnp.int32, (SUB, t.shape[1]), 0)
    last = jnp.where(row >= SUB - j, pltpu.roll(halo, SUB - j, 0), rolled[n - SUB:])
    return jnp.concatenate([rolled[:n - SUB], last], axis=0) if n > SUB else last


def _mm(a, b, *, ta=False, tb=False, out_dtype=F32, name):
    if ta:
        K, M = a.shape
    else:
        M, K = a.shape
    if tb:
        N, K2 = b.shape
    else:
        K2, N = b.shape
    assert K == K2, (a.shape, b.shape, ta, tb)
    tk = _tile(K, 1536)
    nk = K // tk
    tm, tn = _tile(M, 2048 if nk == 1 else 1536), _tile(N, 1536 if nk == 1 else 1408)
    dn = (((0 if ta else 1,), (1 if tb else 0,)), ((), ()))

    def body(a_ref, b_ref, o_ref, *acc):
        part = lax.dot_general(a_ref[...].astype(_MXU), b_ref[...].astype(_MXU), dn, preferred_element_type=F32)
        if nk == 1:
            o_ref[...] = part.astype(out_dtype)
            return
        acc_ref, = acc
        k = pl.program_id(2)

        @pl.when(k == 0)
        def _():
            acc_ref[...] = part

        @pl.when(k > 0)
        def _():
            acc_ref[...] += part

        @pl.when(k == nk - 1)
        def _():
            o_ref[...] = acc_ref[...].astype(out_dtype)

    a_spec = pl.BlockSpec((tk, tm), lambda i, j, k: (k, i)) if ta else pl.BlockSpec((tm, tk), lambda i, j, k: (i, k))
    b_spec = pl.BlockSpec((tn, tk), lambda i, j, k: (j, k)) if tb else pl.BlockSpec((tk, tn), lambda i, j, k: (k, j))
    return pl.pallas_call(
        body, grid=(M // tm, N // tn, nk), in_specs=[a_spec, b_spec],
        out_specs=pl.BlockSpec((tm, tn), lambda i, j, k: (i, j)),
        out_shape=jax.ShapeDtypeStruct((M, N), out_dtype),
        scratch_shapes=[pltpu.VMEM((tm, tn), F32)] * (nk > 1), compiler_params=_cp(3), name=name)(a, b)


def _row(ts, w, col=0):
    return pl.BlockSpec((ts, w), lambda i: (i, col))


def _vec(w, r=1):
    return pl.BlockSpec((r, w), lambda i: (0, 0))


def _silu_call(x, name):
    def body(x_ref, o_ref):
        o_ref[...] = _silu(x_ref[...])
    return pl.pallas_call(body, out_shape=jax.ShapeDtypeStruct(x.shape, F32), name=name)(x)


def _norm_fwd(x, g, sc, sh, *, f=None, gate=None, name):
    S, dm = x.shape
    ts = _tile(S, TS_ROW, SUB)
    res = f is not None

    def body(*refs):
        if res:
            x_ref, f_ref, gate_ref, g_ref, sc_ref, sh_ref, xo_ref, h_ref = refs
            xv = x_ref[...] + gate_ref[...] * f_ref[...]
            xo_ref[...] = xv
        else:
            x_ref, g_ref, sc_ref, sh_ref, h_ref = refs
            xv = x_ref[...]
        rstd = lax.rsqrt(jnp.mean(xv * xv, axis=-1, keepdims=True) + EPS)
        h_ref[...] = ((xv * rstd) * g_ref[...] * (1.0 + sc_ref[...]) + sh_ref[...]).astype(_ACT)

    ins = [x] + ([f, gate] if res else []) + [g, sc, sh]
    in_specs = [_row(ts, dm)] + ([_row(ts, dm), _vec(dm)] if res else []) + [_vec(dm)] * 3
    h_shape = jax.ShapeDtypeStruct((S, dm), _ACT)
    if res:
        out_shape, out_specs = (jax.ShapeDtypeStruct((S, dm), F32), h_shape), (_row(ts, dm), _row(ts, dm))
    else:
        out_shape, out_specs = h_shape, _row(ts, dm)
    return pl.pallas_call(body, grid=(S // ts,), in_specs=in_specs, out_specs=out_specs, out_shape=out_shape,
                          compiler_params=_cp(1), name=name)(*ins)


def _norm_bwd(dh, x, dres, g, sc, *, name):
    S, dm = x.shape
    ts = _tile(S, TS_ROW, SUB)

    def body(dh_ref, x_ref, dres_ref, g_ref, sc_ref, dx_ref, dg_ref, dsc_ref, dsh_ref):
        i = pl.program_id(0)
        xv = x_ref[...]
        dhv = dh_ref[...]
        rstd = lax.rsqrt(jnp.mean(xv * xv, axis=-1, keepdims=True) + EPS)
        xhat = xv * rstd
        hn = xhat * g_ref[...]
        dhn = dhv * (1.0 + sc_ref[...])
        dxh = dhn * g_ref[...]
        dx_ref[...] = dres_ref[...] + rstd * (dxh - xhat * jnp.mean(dxh * xhat, axis=-1, keepdims=True))

        @pl.when(i == 0)
        def _():
            dg_ref[...] = jnp.zeros_like(dg_ref)
            dsc_ref[...] = jnp.zeros_like(dsc_ref)
            dsh_ref[...] = jnp.zeros_like(dsh_ref)

        dg_ref[...] += _colsum(dhn * xhat)
        dsc_ref[...] += _colsum(dhv * hn)
        dsh_ref[...] += _colsum(dhv)

    vshape = jax.ShapeDtypeStruct((1, dm), F32)
    return pl.pallas_call(
        body, grid=(S // ts,), in_specs=[_row(ts, dm)] * 3 + [_vec(dm)] * 2,
        out_specs=(_row(ts, dm), _vec(dm), _vec(dm), _vec(dm)),
        out_shape=(jax.ShapeDtypeStruct((S, dm), F32), vshape, vshape, vshape),
        compiler_params=_cp(1), name=name)(dh, x, dres, g, sc)


def _gate_bwd(dxo, f, gate, *, name):
    S, dm = f.shape
    ts = _tile(S, TS_ROW, SUB)

    def body(dxo_ref, f_ref, gate_ref, df_ref, dgate_ref):
        i = pl.program_id(0)
        dv = dxo_ref[...]
        df_ref[...] = (gate_ref[...] * dv).astype(_ACT)

        @pl.when(i == 0)
        def _():
            dgate_ref[...] = jnp.zeros_like(dgate_ref)

        dgate_ref[...] += _colsum(dv * f_ref[...])

    return pl.pallas_call(
        body, grid=(S // ts,), in_specs=[_row(ts, dm), _row(ts, dm), _vec(dm)],
        out_specs=(_row(ts, dm), _vec(dm)),
        out_shape=(jax.ShapeDtypeStruct((S, dm), _ACT), jax.ShapeDtypeStruct((1, dm), F32)),
        compiler_params=_cp(1), name=name)(dxo, f, gate)


def _final_loss(x, f, gate, g, tgt, *, name):
    S, dm = x.shape
    ts = _tile(S, TS_ROW, SUB)

    def body(x_ref, f_ref, gate_ref, g_ref, t_ref, loss_ref, dx_ref, dg_ref):
        i = pl.program_id(0)
        xv = x_ref[...] + gate_ref[...] * f_ref[...]
        rstd = lax.rsqrt(jnp.mean(xv * xv, axis=-1, keepdims=True) + EPS)
        xhat = xv * rstd
        err = xhat * g_ref[...] - t_ref[...]
        dy = err * (1.0 / dm)
        dxh = dy * g_ref[...]
        dx_ref[...] = rstd * (dxh - xhat * jnp.mean(dxh * xhat, axis=-1, keepdims=True))

        @pl.when(i == 0)
        def _():
            loss_ref[...] = jnp.zeros_like(loss_ref)
            dg_ref[...] = jnp.zeros_like(dg_ref)

        loss_ref[...] += jnp.full((1, LANE), 0.5 * jnp.sum(jnp.mean(err * err, axis=-1, keepdims=True)), F32)
        dg_ref[...] += _colsum(dy * xhat)

    return pl.pallas_call(
        body, grid=(S // ts,), in_specs=[_row(ts, dm), _row(ts, dm), _vec(dm), _vec(dm), _row(ts, dm)],
        out_specs=(_vec(LANE), _row(ts, dm), _vec(dm)),
        out_shape=(jax.ShapeDtypeStruct((1, LANE), F32), jax.ShapeDtypeStruct((S, dm), F32),
                   jax.ShapeDtypeStruct((1, dm), F32)),
        compiler_params=_cp(1), name=name)(x, f, gate, g, tgt)


def _ffn_conv(t, halo, cw_ref, cb_ref):
    t1, t2 = _shift_down(t, halo, 1), _shift_down(t, halo, 2)
    return ((cb_ref[...] + t2 * cw_ref[0:1, :]) + t1 * cw_ref[1:2, :]) + t * cw_ref[2:3, :], t1, t2


def _prev_halo_spec(ts, w, col=0):
    return pl.BlockSpec((SUB, w), lambda i: (jnp.maximum(i * (ts // SUB) - 1, 0), col))


def _ffn_act_fwd(up, cw, cb, *, name):
    S, w2 = up.shape
    ff = w2 // 2
    ts = _tile(S, TS_FFN, SUB)

    def body(up_ref, halo_ref, cw_ref, cb_ref, act_ref):
        i = pl.program_id(0)
        t = up_ref[...]
        halo = jnp.where(i > 0, halo_ref[...], 0.0)
        u, _, _ = _ffn_conv(t, halo, cw_ref, cb_ref)
        act_ref[...] = (_silu(u[:, :ff]) * u[:, ff:]).astype(_ACT)

    return pl.pallas_call(
        body, grid=(S // ts,), in_specs=[_row(ts, w2), _prev_halo_spec(ts, w2), _vec(w2, FFN_CONV), _vec(w2)],
        out_specs=_row(ts, ff), out_shape=jax.ShapeDtypeStruct((S, ff), _ACT),
        compiler_params=_cp(1), name=name)(up, up, cw, cb)


def _ffn_bwd(up, dact, cw, cb, *, name):
    S, w2 = up.shape
    ff = w2 // 2
    ts = _tile(S, TS_FFN, SUB)
    n = S // ts

    def body(up_ref, halo_ref, dact_ref, cw_ref, cb_ref, dup_ref, dcw_ref, dcb_ref, carry_ref):
        i = pl.program_id(0)
        t_idx = n - 1 - i

        @pl.when(i == 0)
        def _():
            carry_ref[...] = jnp.zeros_like(carry_ref)
            dcw_ref[...] = jnp.zeros_like(dcw_ref)
            dcb_ref[...] = jnp.zeros_like(dcb_ref)

        t = up_ref[...]
        halo = jnp.where(t_idx > 0, halo_ref[...], 0.0)
        u, t1, t2 = _ffn_conv(t, halo, cw_ref, cb_ref)
        a, b = u[:, :ff], u[:, ff:]
        da = dact_ref[...]
        sa, dsa = _silu_grad(a)
        dv = jnp.concatenate([da * b * dsa, da * sa], axis=1)
        nxt = carry_ref[...]
        dup = (dv * cw_ref[2:3, :] + _shift_up(dv, nxt, 1) * cw_ref[1:2, :]) + _shift_up(dv, nxt, 2) * cw_ref[0:1, :]
        dup_ref[...] = dup.astype(_ACT)
        dcb_ref[...] += _colsum(dv)
        dcw_ref[2:3, :] += _colsum(dv * t)
        dcw_ref[1:2, :] += _colsum(dv * t1)
        dcw_ref[0:1, :] += _colsum(dv * t2)
        carry_ref[...] = dv[:SUB]

    rev = lambda w: pl.BlockSpec((ts, w), lambda i: (n - 1 - i, 0))
    halo_spec = pl.BlockSpec((SUB, w2), lambda i: (jnp.maximum((n - 1 - i) * (ts // SUB) - 1, 0), 0))
    return pl.pallas_call(
        body, grid=(n,), in_specs=[rev(w2), halo_spec, rev(ff), _vec(w2, FFN_CONV), _vec(w2)],
        out_specs=(rev(w2), _vec(w2, FFN_CONV), _vec(w2)),
        out_shape=(jax.ShapeDtypeStruct((S, w2), _ACT), jax.ShapeDtypeStruct((FFN_CONV, w2), F32),
                   jax.ShapeDtypeStruct((1, w2), F32)),
        scratch_shapes=[pltpu.VMEM((SUB, w2), F32)], compiler_params=_cp(1), name=name)(up, up, dact, cw, cb)


def _ssd_core(pre, halo, misc, cw_ref, cb_ref, dtb, alog):
    q = pre.shape[0]
    conv = cb_ref[...]
    for k in range(SSD_CONV):
        conv = conv + _shift_down(pre, halo, SSD_CONV - 1 - k) * cw_ref[k:k + 1, :]
    xbc = _silu(conv)
    raw = misc + dtb
    dt = _softplus(raw)
    a = -jnp.exp(alog)
    r = lax.broadcasted_iota(jnp.int32, (q, q), 0)
    c = lax.broadcasted_iota(jnp.int32, (q, q), 1)
    tri = r >= c
    acum = jnp.dot(tri.astype(F32), dt * a, precision=_HI, preferred_element_type=F32)
    return conv, xbc, raw, dt, a, acum, acum.T, tri


def _sel(v, j, lo):
    return jnp.where(lo, v[:, 2 * j:2 * j + 1], v[:, 2 * j + 1:2 * j + 2])


def _ssd_pair_fwd(xbc, dt, acum, acum_t, tri, dsk, g_mat, b_mat, c_mat, h_pair, j, lo, lo1, sub_lo):
    q = xbc.shape[0]
    x = xbc[:, LANE * j:LANE * (j + 1)]
    dtp = _sel(dt, j, lo)
    ap = _sel(acum, j, lo)
    xd = x * dtp
    ls, ms = [], []
    for h in (2 * j, 2 * j + 1):
        seg = acum[:, h:h + 1] - acum_t[h:h + 1, :]
        l_mat = jnp.exp(jnp.where(tri, seg, -jnp.inf))
        ls.append(l_mat)
        ms.append(g_mat * l_mat)
    yd = jnp.where(lo, _dot(ms[0], xd), _dot(ms[1], xd))
    ea = jnp.exp(ap)
    yo = _dot_nt(c_mat, h_pair) * ea
    dp = _sel(dsk, j, lo1)
    alast = acum[q - 1:q, :]
    e = jnp.exp(_sel(alast, j, lo1) - ap)
    cd = jnp.where(sub_lo, jnp.exp(alast[:, 2 * j:2 * j + 1]), jnp.exp(alast[:, 2 * j + 1:2 * j + 2]))
    return dict(x=x, dtp=dtp, ap=ap, xd=xd, ls=ls, ms=ms, ea=ea, yo=yo, dp=dp, e=e, cd=cd, y=yd + yo + x * dp)


def _gnorm(yg):
    half = SSD_INNER // SSD_GROUPS
    rstds, yns = [], []
    for g in range(SSD_GROUPS):
        part = yg[:, half * g:half * (g + 1)]
        rstd = lax.rsqrt(jnp.mean(part * part, axis=-1, keepdims=True) + EPS)
        rstds.append(rstd)
        yns.append(part * rstd)
    return rstds, yns


def _ssd_specs(nc, rev):
    q = SSD_CHUNK
    cidx = (lambda i: nc - 1 - i) if rev else (lambda i: i)
    return [
        pl.BlockSpec((q, SSD_XBC), lambda i: (cidx(i), C_XBC // SSD_XBC)),
        pl.BlockSpec((SUB, SSD_XBC), lambda i: (jnp.maximum(cidx(i) * (q // SUB) - 1, 0), C_XBC // SSD_XBC)),
        pl.BlockSpec((q, SSD_INNER), lambda i: (cidx(i), C_Z // SSD_INNER)),
        pl.BlockSpec((q, LANE), lambda i: (cidx(i), C_MISC // LANE)),
    ]


def _ssd_param_specs():
    return [_vec(SSD_XBC, SSD_CONV), _vec(SSD_XBC), _vec(LANE), _vec(LANE), _vec(LANE), _vec(SSD_INNER)]


def _ssd_fwd(proj, cw, cb, dtb, alog, dsk, ng, *, name):
    S = proj.shape[0]
    q = SSD_CHUNK
    nc = S // q
    npair = SSD_HEADS // 2

    def body(xbc_ref, halo_ref, z_ref, misc_ref, cw_ref, cb_ref, dtb_ref, alog_ref, dsk_ref, ng_ref,
             y_ref, hin_ref, h_ref):
        c = pl.program_id(0)

        @pl.when(c == 0)
        def _():
            h_ref[...] = jnp.zeros_like(h_ref)

        pre = xbc_ref[...]
        halo = jnp.where(c > 0, halo_ref[...], 0.0)
        conv, xbc, raw, dt, a, acum, acum_t, tri = _ssd_core(pre, halo, misc_ref[...], cw_ref, cb_ref,
                                                             dtb_ref[...], alog_ref[...])
        lo = lax.broadcasted_iota(jnp.int32, (q, LANE), 1) < LANE // 2
        lo1 = lo[:1]
        sub_lo = lax.broadcasted_iota(jnp.int32, (LANE, LANE), 0) < LANE // 2
        ys = []
        for g in range(SSD_GROUPS):
            b_mat = xbc[:, SSD_INNER + SSD_STATE * g:SSD_INNER + SSD_STATE * (g + 1)]
            c_mat = xbc[:, SSD_INNER + SSD_STATE * (SSD_GROUPS + g):SSD_INNER + SSD_STATE * (SSD_GROUPS + g + 1)]
            g_mat = _dot_nt(c_mat, b_mat)
            for jj in range(npair // SSD_GROUPS):
                j = g * (npair // SSD_GROUPS) + jj
                hj = h_ref[j]
                p = _ssd_pair_fwd(xbc, dt, acum, acum_t, tri, dsk_ref[...], g_mat, b_mat, c_mat, hj, j, lo, lo1, sub_lo)
                ys.append(p["y"])
                hin_ref[0, j] = hj
                h_ref[j] = p["cd"] * hj + _dot_tn(p["xd"] * p["e"], b_mat)
        yg = jnp.concatenate(ys, axis=1) * _silu(z_ref[...])
        _, yns = _gnorm(yg)
        y_ref[...] = jnp.concatenate(yns, axis=1) * ng_ref[...]

    return pl.pallas_call(
        body, grid=(nc,), in_specs=_ssd_specs(nc, False) + _ssd_param_specs(),
        out_specs=(pl.BlockSpec((q, SSD_INNER), lambda i: (i, 0)),
                   pl.BlockSpec((1, npair, LANE, LANE), lambda i: (i, 0, 0, 0))),
        out_shape=(jax.ShapeDtypeStruct((S, SSD_INNER), F32), jax.ShapeDtypeStruct((nc, npair, LANE, LANE), F32)),
        scratch_shapes=[pltpu.VMEM((npair, LANE, LANE), F32)], compiler_params=_cp(1), name=name,
    )(proj, proj, proj, proj, cw, cb, dtb, alog, dsk, ng)


def _ssd_bwd(proj, dycat, hin, cw, cb, dtb, alog, dsk, ng, *, name):
    S = proj.shape[0]
    q = SSD_CHUNK
    nc = S // q
    npair = SSD_HEADS // 2
    ppg = npair // SSD_GROUPS

    def body(xbc_ref, halo_ref, z_ref, misc_ref, dy_ref, hin_ref, cw_ref, cb_ref, dtb_ref, alog_ref, dsk_ref, ng_ref,
             dpre_ref, dz_ref, dmisc_ref, dcw_ref, dcb_ref, ddtb_ref, dalog_ref, ddsk_ref, dng_ref,
             dh_ref, carry_ref):
        i = pl.program_id(0)
        c = nc - 1 - i

        @pl.when(i == 0)
        def _():
            dh_ref[...] = jnp.zeros_like(dh_ref)
            carry_ref[...] = jnp.zeros_like(carry_ref)
            for r in (dcw_ref, dcb_ref, ddtb_ref, dalog_ref, ddsk_ref, dng_ref):
                r[...] = jnp.zeros_like(r)

        pre = xbc_ref[...]
        halo = jnp.where(c > 0, halo_ref[...], 0.0)
        conv, xbc, raw, dt, a, acum, acum_t, tri = _ssd_core(pre, halo, misc_ref[...], cw_ref, cb_ref,
                                                             dtb_ref[...], alog_ref[...])
        lane = lax.broadcasted_iota(jnp.int32, (q, LANE), 1)
        lane1 = lane[:1]
        rowi = lax.broadcasted_iota(jnp.int32, (q, LANE), 0)
        lastrow = rowi == q - 1
        lo = lane < LANE // 2
        lo1 = lo[:1]
        sub_lo = lax.broadcasted_iota(jnp.int32, (LANE, LANE), 0) < LANE // 2
        dsk = dsk_ref[...]
        alast = acum[q - 1:q, :]

        def halves(t):
            return _rowsum(jnp.where(lo, t, 0.0)), _rowsum(jnp.where(lo, 0.0, t))

        def put(ha, va, vb):
            ln = lane if va.shape[0] == q else lane1
            return jnp.where(ln == ha, va, 0.0) + jnp.where(ln == ha + 1, vb, 0.0)

        mats, pairs = [], []
        for g in range(SSD_GROUPS):
            b_mat = xbc[:, SSD_INNER + SSD_STATE * g:SSD_INNER + SSD_STATE * (g + 1)]
            c_mat = xbc[:, SSD_INNER + SSD_STATE * (SSD_GROUPS + g):SSD_INNER + SSD_STATE * (SSD_GROUPS + g + 1)]
            g_mat = _dot_nt(c_mat, b_mat)
            mats.append((b_mat, c_mat, g_mat))
            for jj in range(ppg):
                j = g * ppg + jj
                pairs.append(_ssd_pair_fwd(xbc, dt, acum, acum_t, tri, dsk, g_mat, b_mat, c_mat, hin_ref[0, j],
                                           j, lo, lo1, sub_lo))
        z = z_ref[...]
        sz, dsz = _silu_grad(z)
        y = jnp.concatenate([p["y"] for p in pairs], axis=1)
        rstds, yns = _gnorm(y * sz)
        dout = dy_ref[...]
        dng_ref[...] += _colsum(dout * jnp.concatenate(yns, axis=1))
        dyn = dout * ng_ref[...]
        half = SSD_INNER // SSD_GROUPS
        dygs = []
        for g in range(SSD_GROUPS):
            dyn_g = dyn[:, half * g:half * (g + 1)]
            dygs.append(rstds[g] * (dyn_g - yns[g] * jnp.mean(dyn_g * yns[g], axis=-1, keepdims=True)))
        dyg = jnp.concatenate(dygs, axis=1)
        dyv = dyg * sz
        dz_ref[...] = (dyg * y * dsz).astype(_ACT)

        da_acc = jnp.zeros((q, LANE), F32)
        ddt = jnp.zeros((q, LANE), F32)
        dds = jnp.zeros((1, LANE), F32)
        dxs, dbs, dcs = [], [], []
        for g in range(SSD_GROUPS):
            b_mat, c_mat, g_mat = mats[g]
            dg_mat = jnp.zeros((q, q), F32)
            db = jnp.zeros((q, SSD_STATE), F32)
            dc = jnp.zeros((q, SSD_STATE), F32)
            for jj in range(ppg):
                j = g * ppg + jj
                ha = 2 * j
                p = pairs[j]
                hj = hin_ref[0, j]
                dyp = dyv[:, LANE * j:LANE * (j + 1)]
                dsum = _colsum(dyp * p["x"])
                dds = dds + put(ha, _rowsum(jnp.where(lo1, dsum, 0.0)), _rowsum(jnp.where(lo1, 0.0, dsum)))
                dx = dyp * p["dp"]
                dw = dyp * p["ea"]
                dc = dc + _dot(dw, hj)
                dh_yo = _dot_tn(dw, c_mat)
                ra, rb = halves(dyp * p["yo"])
                da_acc = da_acc + put(ha, ra, rb)
                dxd = jnp.zeros((q, LANE), F32)
                for idx in range(2):
                    dyh = jnp.where(lo, dyp, 0.0) if idx == 0 else jnp.where(lo, 0.0, dyp)
                    dm = _dot_nt(dyh, p["xd"])
                    dxd = dxd + _dot_tn(p["ms"][idx], dyh)
                    dg_mat = dg_mat + dm * p["ls"][idx]
                    t = dm * p["ms"][idx]
                    da_h = _rowsum(t) - _rowsum(t.T)
                    da_acc = da_acc + jnp.where(lane == ha + idx, da_h, 0.0)
                dhn = dh_ref[j]
                s = _rowsum(dhn * hj)
                sa = jnp.sum(jnp.where(sub_lo[:, :1], s, 0.0), keepdims=True)
                sb = jnp.sum(jnp.where(sub_lo[:, :1], 0.0, s), keepdims=True)
                cda, cdb = jnp.exp(alast[:, ha:ha + 1]), jnp.exp(alast[:, ha + 1:ha + 2])
                db = db + _dot(p["xd"] * p["e"], dhn)
                r = _dot_nt(b_mat, dhn)
                dxd = dxd + r * p["e"]
                qa, qb = halves(r * p["xd"] * p["e"])
                da_acc = da_acc - put(ha, qa, qb)
                tot_a = sa * cda + jnp.sum(qa, keepdims=True)
                tot_b = sb * cdb + jnp.sum(qb, keepdims=True)
                da_acc = da_acc + jnp.where(lastrow, put(ha, tot_a, tot_b), 0.0)
                dh_ref[j] = p["cd"] * dhn + dh_yo
                dx = dx + dxd * p["dtp"]
                ua, ub = halves(dxd * p["x"])
                ddt = ddt + put(ha, ua, ub)
                dxs.append(dx)
            dc = dc + _dot(dg_mat, b_mat)
            db = db + _dot_tn(dg_mat, c_mat)
            dbs.append(db)
            dcs.append(dc)
        r2 = lax.broadcasted_iota(jnp.int32, (q, q), 0)
        c2 = lax.broadcasted_iota(jnp.int32, (q, q), 1)
        dda = jnp.dot((c2 >= r2).astype(F32), da_acc, precision=_HI, preferred_element_type=F32)
        ddt = ddt + dda * a
        dalog_ref[...] += _colsum(dda * dt) * a
        ddsk_ref[...] += dds
        draw = jnp.where(lane < SSD_HEADS, ddt * _sigmoid(raw), 0.0)
        ddtb_ref[...] += _colsum(draw)
        dmisc_ref[...] = draw
        dconv = jnp.concatenate(dxs + dbs + dcs, axis=1) * _dsilu(conv)
        dcb_ref[...] += _colsum(dconv)
        nxt = carry_ref[...]
        dpre = jnp.zeros_like(dconv)
        for k in range(SSD_CONV):
            dcw_ref[k:k + 1, :] += _colsum(dconv * _shift_down(pre, halo, SSD_CONV - 1 - k))
            dpre = dpre + _shift_up(dconv, nxt, SSD_CONV - 1 - k) * cw_ref[k:k + 1, :]
        dpre_ref[...] = dpre.astype(_ACT)
        carry_ref[...] = dconv[:SUB]

    rev = lambda i: (nc - 1 - i, 0)
    vshape = lambda w, r=1: jax.ShapeDtypeStruct((r, w), F32)
    return pl.pallas_call(
        body, grid=(nc,),
        in_specs=_ssd_specs(nc, True) + [pl.BlockSpec((q, SSD_INNER), rev),
                                         pl.BlockSpec((1, npair, LANE, LANE), lambda i: (nc - 1 - i, 0, 0, 0))]
        + _ssd_param_specs(),
        out_specs=(pl.BlockSpec((q, SSD_XBC), rev), pl.BlockSpec((q, SSD_INNER), rev), pl.BlockSpec((q, LANE), rev),
                   _vec(SSD_XBC, SSD_CONV), _vec(SSD_XBC), _vec(LANE), _vec(LANE), _vec(LANE), _vec(SSD_INNER)),
        out_shape=(jax.ShapeDtypeStruct((S, SSD_XBC), _ACT), jax.ShapeDtypeStruct((S, SSD_INNER), _ACT),
                   jax.ShapeDtypeStruct((S, LANE), F32),
                   vshape(SSD_XBC, SSD_CONV), vshape(SSD_XBC), vshape(LANE), vshape(LANE), vshape(LANE),
                   vshape(SSD_INNER)),
        scratch_shapes=[pltpu.VMEM((npair, LANE, LANE), F32), pltpu.VMEM((SUB, SSD_XBC), F32)],
        compiler_params=_cp(1), name=name,
    )(proj, proj, proj, proj, dycat, hin, cw, cb, dtb, alog, dsk, ng)


def _rope(x, cosf, sina, sinb):
    return x * cosf + pltpu.roll(x, LANE - MLA_ROPE // 2, 1) * sina + pltpu.roll(x, MLA_ROPE // 2, 1) * sinb


def _rope_t(dy, cosf, sina, sinb):
    return dy * cosf + pltpu.roll(dy * sina, MLA_ROPE // 2, 1) + pltpu.roll(dy * sinb, LANE - MLA_ROPE // 2, 1)


def _rope_lanes(shape):
    lane = lax.broadcasted_iota(jnp.int32, shape, 1)
    return (lane >= ROPE_LANE) & (lane < ROPE_LANE + MLA_ROPE)


def _mla_prep_fwd(proj, cosf, sina, sinb, gq, gkv, wuq, wukv, *, name):
    S = proj.shape[0]
    ts = _tile(S, TS_ROW, SUB)
    hw = MLA_HEADS * LANE

    def body(cq_ref, ckv_ref, misc_ref, cos_ref, sa_ref, sb_ref, gq_ref, gkv_ref, wuq_ref, wukv_ref,
             q_ref, k_ref, v_ref, vt_ref):
        cosv, sav, sbv = cos_ref[...], sa_ref[...], sb_ref[...]
        cq = cq_ref[...]
        qn = cq * lax.rsqrt(jnp.mean(cq * cq, axis=-1, keepdims=True) + EPS) * gq_ref[...]
        qh = _dot(qn, wuq_ref[...])
        ckv = ckv_ref[...]
        kvn = ckv * lax.rsqrt(jnp.mean(ckv * ckv, axis=-1, keepdims=True) + EPS) * gkv_ref[...]
        kv = _dot(kvn, wukv_ref[...])
        kr = _rope(jnp.where(_rope_lanes((ts, LANE)), misc_ref[...], 0.0), cosv, sav, sbv)
        for h in range(MLA_HEADS):
            sl = slice(LANE * h, LANE * (h + 1))
            q_ref[:, sl] = (_rope(qh[:, sl], cosv, sav, sbv) * _Q_SCALE).astype(_ACT)
            k_ref[:, sl] = (kv[:, sl] + kr).astype(_ACT)
        v_ref[...] = kv[:, hw:].astype(_ACT)
        vt_ref[...] = kv[:, hw:].T.astype(_ACT)

    oshape = jax.ShapeDtypeStruct((S, hw), _ACT)
    return pl.pallas_call(
        body, grid=(S // ts,),
        in_specs=[_row(ts, MLA_QR, C_CQ // MLA_QR), _row(ts, MLA_KVR, C_CKV // MLA_KVR), _row(ts, LANE, C_MISC // LANE),
                  _row(ts, LANE), _row(ts, LANE), _row(ts, LANE), _vec(MLA_QR), _vec(MLA_KVR),
                  _vec(hw, MLA_QR), _vec(2 * hw, MLA_KVR)],
        out_specs=(_row(ts, hw),) * 3 + (pl.BlockSpec((hw, ts), lambda i: (0, i)),),
        out_shape=(oshape,) * 3 + (jax.ShapeDtypeStruct((hw, S), _ACT),), compiler_params=_cp(1), name=name,
    )(proj, proj, proj, cosf, sina, sinb, gq, gkv, wuq, wukv)


def _mla_prep_bwd(proj, dq, dk, dv, dmisc_ssd, cosf, sina, sinb, gq, gkv, wuq, wukv, *, name):
    S = proj.shape[0]
    ts = _tile(S, TS_ROW, SUB)
    hw = MLA_HEADS * LANE

    def body(cq_ref, ckv_ref, dq_ref, dk_ref, dv_ref, dms_ref, cos_ref, sa_ref, sb_ref, gq_ref, gkv_ref,
             wuq_ref, wukv_ref, dcq_ref, dckv_ref, dmisc_ref, dqh_ref, dkv_ref, qn_ref, kvn_ref, dgq_ref, dgkv_ref):
        i = pl.program_id(0)
        cosv, sav, sbv = cos_ref[...], sa_ref[...], sb_ref[...]

        @pl.when(i == 0)
        def _():
            dgq_ref[...] = jnp.zeros_like(dgq_ref)
            dgkv_ref[...] = jnp.zeros_like(dgkv_ref)

        dqh = jnp.concatenate([_rope_t(dq_ref[:, LANE * h:LANE * (h + 1)], cosv, sav, sbv)
                               for h in range(MLA_HEADS)], axis=1)
        dqh_ref[...] = dqh.astype(_ACT)
        dkv = jnp.concatenate([dk_ref[...], dv_ref[...]], axis=1)
        dkv_ref[...] = dkv.astype(_ACT)

        def norm_bwd(x, g, dn, dg_ref, n_ref):
            rstd = lax.rsqrt(jnp.mean(x * x, axis=-1, keepdims=True) + EPS)
            xhat = x * rstd
            n_ref[...] = (xhat * g).astype(_ACT)
            dg_ref[...] += _colsum(dn * xhat)
            dxh = dn * g
            return rstd * (dxh - xhat * jnp.mean(dxh * xhat, axis=-1, keepdims=True))

        dcq_ref[...] = norm_bwd(cq_ref[...], gq_ref[...], _dot_nt(dqh, wuq_ref[...]), dgq_ref, qn_ref).astype(_ACT)
        dckv_ref[...] = norm_bwd(ckv_ref[...], gkv_ref[...], _dot_nt(dkv, wukv_ref[...]), dgkv_ref, kvn_ref).astype(_ACT)
        dks = dk_ref[:, 0:LANE]
        for h in range(1, MLA_HEADS):
            dks = dks + dk_ref[:, LANE * h:LANE * (h + 1)]
        rl = _rope_lanes((ts, LANE))
        dkr = _rope_t(jnp.where(rl, dks, 0.0), cosv, sav, sbv)
        dmisc_ref[...] = (dms_ref[...] + jnp.where(rl, dkr, 0.0)).astype(_ACT)

    act = lambda w: jax.ShapeDtypeStruct((S, w), _ACT)
    return pl.pallas_call(
        body, grid=(S // ts,),
        in_specs=[_row(ts, MLA_QR, C_CQ // MLA_QR), _row(ts, MLA_KVR, C_CKV // MLA_KVR),
                  _row(ts, hw), _row(ts, hw), _row(ts, hw), _row(ts, LANE),
                  _row(ts, LANE), _row(ts, LANE), _row(ts, LANE), _vec(MLA_QR), _vec(MLA_KVR),
                  _vec(hw, MLA_QR), _vec(2 * hw, MLA_KVR)],
        out_specs=(_row(ts, MLA_QR), _row(ts, MLA_KVR), _row(ts, LANE), _row(ts, hw), _row(ts, 2 * hw),
                   _row(ts, MLA_QR), _row(ts, MLA_KVR), _vec(MLA_QR), _vec(MLA_KVR)),
        out_shape=(act(MLA_QR), act(MLA_KVR), act(LANE), act(hw), act(2 * hw), act(MLA_QR), act(MLA_KVR),
                   jax.ShapeDtypeStruct((1, MLA_QR), F32), jax.ShapeDtypeStruct((1, MLA_KVR), F32)),
        compiler_params=_cp(1), name=name,
    )(proj, proj, dq, dk, dv, dmisc_ssd, cosf, sina, sinb, gq, gkv, wuq, wukv)


_MLA_SCALE = 1.0 / math.sqrt(MLA_NOPE + MLA_ROPE)
_LOG2E = 1.4426950408889634
_Q_SCALE = _MLA_SCALE * _LOG2E
ATT_CHUNK = 1024


def _tri_grid(nq, by_key):
    if by_key:
        pairs = [(i, j) for j in range(nq) for i in range(j, nq)]
    else:
        pairs = [(i, j) for i in range(nq) for j in range(i + 1)]
    return jnp.asarray([p[0] for p in pairs], jnp.int32), jnp.asarray([p[1] for p in pairs], jnp.int32)


def _attn_fwd(q, k, vt, *, name):
    S = q.shape[0]
    tq = _tile(S, TQ_ATT)
    nq = S // tq
    itab, jtab = _tri_grid(nq, False)

    def body(it_ref, jt_ref, q_ref, k_ref, vt_ref, o_ref, lse_ref, lset_ref, m_ref, l_ref, acc_ref):
        t = pl.program_id(1)
        i, j = it_ref[t], jt_ref[t]

        @pl.when(j == 0)
        def _():
            m_ref[...] = jnp.full_like(m_ref, -jnp.inf)
            l_ref[...] = jnp.zeros_like(l_ref)
            acc_ref[...] = jnp.zeros_like(acc_ref)

        def step(diagonal):
            s = _dot_nt(k_ref[...], q_ref[...])
            if diagonal:
                kk = lax.broadcasted_iota(jnp.int32, (tq, tq), 0)
                s = jnp.where(kk <= lax.broadcasted_iota(jnp.int32, (tq, tq), 1), s, -jnp.inf)
            m_prev = m_ref[...]
            m_new = jnp.maximum(m_prev, jnp.max(s, axis=0, keepdims=True))
            p = jnp.exp2(s - m_new)
            alpha = jnp.exp2(m_prev - m_new)
            l_ref[...] = alpha * l_ref[...] + _colsum(p)
            acc_ref[...] = alpha * acc_ref[...] + _dot(vt_ref[...], p)
            m_ref[...] = m_new

        pl.when(j < i)(functools.partial(step, False))
        pl.when(j == i)(functools.partial(step, True))

        @pl.when(j == i)
        def _():
            o_ref[...] = (acc_ref[...] / l_ref[...]).T
            lse = m_ref[...] + jnp.log2(l_ref[...])
            lset_ref[...] = jnp.broadcast_to(lse, (SUB, tq))
            lse_ref[...] = jnp.broadcast_to(lse, (LANE, tq)).T

    qspec = pl.BlockSpec((tq, LANE), lambda h, t, it, jt: (it[t], h))
    kspec = pl.BlockSpec((tq, LANE), lambda h, t, it, jt: (jt[t], h))
    vtspec = pl.BlockSpec((LANE, tq), lambda h, t, it, jt: (h, jt[t]))
    oshape = jax.ShapeDtypeStruct((S, MLA_HEADS * LANE), F32)
    return pl.pallas_call(
        body,
        grid_spec=pltpu.PrefetchScalarGridSpec(
            num_scalar_prefetch=2, grid=(MLA_HEADS, itab.shape[0]), in_specs=[qspec, kspec, vtspec],
            out_specs=(qspec, qspec, pl.BlockSpec((SUB, tq), lambda h, t, it, jt: (h, it[t]))),
            scratch_shapes=[pltpu.VMEM((1, tq), F32), pltpu.VMEM((1, tq), F32), pltpu.VMEM((LANE, tq), F32)]),
        out_shape=(oshape, oshape, jax.ShapeDtypeStruct((MLA_HEADS * SUB, S), F32)),
        compiler_params=_cp(2), name=name)(itab, jtab, q, k, vt)


def _attn_bwd_dq(q, k, v, o, lse, dycat, *, name):
    S = q.shape[0]
    tq = _tile(S, TQ_ATT)
    nq = S // tq
    rc = min(ATT_CHUNK, tq)
    itab, jtab = _tri_grid(nq, False)

    def body(it_ref, jt_ref, q_ref, k_ref, v_ref, o_ref, lse_ref, do_ref, dq_ref, acc_ref):
        t = pl.program_id(1)
        i, j = it_ref[t], jt_ref[t]

        @pl.when(j == 0)
        def _():
            acc_ref[...] = jnp.zeros_like(acc_ref)

        def step(diagonal):
            kv, vv = k_ref[...], v_ref[...]
            for r in range(tq // rc):
                rows = slice(r * rc, (r + 1) * rc)
                s = _dot_nt(q_ref[rows, :], kv)
                if diagonal:
                    rr = r * rc + lax.broadcasted_iota(jnp.int32, (rc, tq), 0)
                    s = jnp.where(lax.broadcasted_iota(jnp.int32, (rc, tq), 1) <= rr, s, -jnp.inf)
                p = jnp.exp2(s - lse_ref[rows, 0:1])
                dov = do_ref[rows, :]
                delta = _rowsum(dov * o_ref[rows, :])
                ds = p * (_dot_nt(dov, vv) - delta)
                acc_ref[rows, :] += _dot(ds, kv)

        pl.when(j < i)(functools.partial(step, False))
        pl.when(j == i)(functools.partial(step, True))

        @pl.when(j == i)
        def _():
            dq_ref[...] = acc_ref[...] * _MLA_SCALE

    qspec = pl.BlockSpec((tq, LANE), lambda h, t, it, jt: (it[t], h))
    kspec = pl.BlockSpec((tq, LANE), lambda h, t, it, jt: (jt[t], h))
    dospec = pl.BlockSpec((tq, LANE), lambda h, t, it, jt: (it[t], SSD_INNER // LANE + h))
    return pl.pallas_call(
        body,
        grid_spec=pltpu.PrefetchScalarGridSpec(
            num_scalar_prefetch=2, grid=(MLA_HEADS, itab.shape[0]),
            in_specs=[qspec, kspec, kspec, qspec, qspec, dospec], out_specs=qspec,
            scratch_shapes=[pltpu.VMEM((tq, LANE), F32)]),
        out_shape=jax.ShapeDtypeStruct((S, MLA_HEADS * LANE), F32),
        compiler_params=_cp(2), name=name)(itab, jtab, q, k, v, o, lse, dycat)


def _attn_bwd_dkv(q, k, v, o, lset, dycat, *, name):
    S = q.shape[0]
    tq = _tile(S, TQ_ATT)
    nq = S // tq
    kc = min(ATT_CHUNK, tq)
    itab, jtab = _tri_grid(nq, True)

    def body(it_ref, jt_ref, q_ref, k_ref, v_ref, o_ref, lset_ref, do_ref, dk_ref, dv_ref, dk_acc, dv_acc):
        t = pl.program_id(1)
        i, j = it_ref[t], jt_ref[t]

        @pl.when(i == j)
        def _():
            dk_acc[...] = jnp.zeros_like(dk_acc)
            dv_acc[...] = jnp.zeros_like(dv_acc)

        def step(diagonal):
            qv, dov = q_ref[...], do_ref[...]
            delta = lax.dot_general(jnp.ones((SUB, LANE), F32), dov * o_ref[...], (((1,), (1,)), ((), ())),
                                    precision=_HI, preferred_element_type=F32)[0:1]
            lse = lset_ref[0:1, :]
            for c in range(tq // kc):
                rows = slice(c * kc, (c + 1) * kc)
                s = _dot_nt(k_ref[rows, :], qv)
                if diagonal:
                    kk = c * kc + lax.broadcasted_iota(jnp.int32, (kc, tq), 0)
                    s = jnp.where(kk <= lax.broadcasted_iota(jnp.int32, (kc, tq), 1), s, -jnp.inf)
                p = jnp.exp2(s - lse)
                dv_acc[rows, :] += _dot(p, dov)
                ds = p * (_dot_nt(v_ref[rows, :], dov) - delta)
                dk_acc[rows, :] += _dot(ds, qv)

        pl.when(i > j)(functools.partial(step, False))
        pl.when(i == j)(functools.partial(step, True))

        @pl.when(i == nq - 1)
        def _():
            dk_ref[...] = dk_acc[...] * (1.0 / _LOG2E)
            dv_ref[...] = dv_acc[...]

    qspec = pl.BlockSpec((tq, LANE), lambda h, t, it, jt: (it[t], h))
    kspec = pl.BlockSpec((tq, LANE), lambda h, t, it, jt: (jt[t], h))
    dospec = pl.BlockSpec((tq, LANE), lambda h, t, it, jt: (it[t], SSD_INNER // LANE + h))
    lspec = pl.BlockSpec((SUB, tq), lambda h, t, it, jt: (h, it[t]))
    oshape = jax.ShapeDtypeStruct((S, MLA_HEADS * LANE), F32)
    return pl.pallas_call(
        body,
        grid_spec=pltpu.PrefetchScalarGridSpec(
            num_scalar_prefetch=2, grid=(MLA_HEADS, itab.shape[0]),
            in_specs=[qspec, kspec, kspec, qspec, lspec, dospec], out_specs=(kspec, kspec),
            scratch_shapes=[pltpu.VMEM((tq, LANE), F32), pltpu.VMEM((tq, LANE), F32)]),
        out_shape=(oshape, oshape), compiler_params=_cp(2), name=name)(itab, jtab, q, k, v, o, lset, dycat)


_SWA_SCALE = 1.0 / math.sqrt(SWA_HD)
_SWA_KW = SWA_KV * LANE


def _swa_specs(S, ts, rev):
    n = S // ts
    t = (lambda i: n - 1 - i) if rev else (lambda i: i)
    hb = lambda i: jnp.maximum(t(i) * (ts // WINDOW) - 1, 0)
    return [
        pl.BlockSpec((ts, SWA_HEADS * LANE), lambda i: (t(i), C_SQ // (SWA_HEADS * LANE))),
        pl.BlockSpec((ts, _SWA_KW), lambda i: (t(i), C_SK // _SWA_KW)),
        pl.BlockSpec((WINDOW, _SWA_KW), lambda i: (hb(i), C_SK // _SWA_KW)),
        pl.BlockSpec((ts, _SWA_KW), lambda i: (t(i), C_SV // _SWA_KW)),
        pl.BlockSpec((WINDOW, _SWA_KW), lambda i: (hb(i), C_SV // _SWA_KW)),
    ]


def _swa_scores(qh, kk, t, b, ts):
    s = _dot_nt(qh, kk) * _SWA_SCALE
    row = lax.broadcasted_iota(jnp.int32, (WINDOW, 2 * WINDOW), 0)
    col = lax.broadcasted_iota(jnp.int32, (WINDOW, 2 * WINDOW), 1)
    rel = WINDOW + row - col
    kpos = t * ts + (b - 1) * WINDOW + col
    return jnp.where((rel >= 0) & (rel < WINDOW) & (kpos >= 0), s, -jnp.inf)


def _swa_fwd(proj, sinks, *, name):
    S = proj.shape[0]
    ts = _tile(S, TS_SWA)
    nb = ts // WINDOW

    def body(q_ref, k_ref, kh_ref, v_ref, vh_ref, sink_ref, o_ref, lse_ref):
        t = pl.program_id(0)
        kext = jnp.concatenate([kh_ref[...], k_ref[...]], axis=0)
        vext = jnp.concatenate([vh_ref[...], v_ref[...]], axis=0)
        for b in range(nb):
            rows = slice(WINDOW * b, WINDOW * (b + 1))
            for h in range(SWA_HEADS):
                kvl = slice(LANE * (h // (SWA_HEADS // SWA_KV)), LANE * (h // (SWA_HEADS // SWA_KV) + 1))
                hl = slice(LANE * h, LANE * (h + 1))
                kk = kext[WINDOW * b:WINDOW * (b + 2), kvl]
                vv = vext[WINDOW * b:WINDOW * (b + 2), kvl]
                s = _swa_scores(q_ref[rows, hl], kk, t, b, ts)
                sk = sink_ref[:, h:h + 1]
                m = jnp.maximum(jnp.max(s, axis=1, keepdims=True), sk)
                p = jnp.exp(s - m)
                den = _rowsum(p) + jnp.exp(sk - m)
                o_ref[rows, hl] = _dot(p, vv) / den
                lse_ref[rows, hl] = jnp.broadcast_to(m + jnp.log(den), (WINDOW, LANE))

    oshape = jax.ShapeDtypeStruct((S, SWA_HEADS * LANE), F32)
    ospec = pl.BlockSpec((ts, SWA_HEADS * LANE), lambda i: (i, 0))
    return pl.pallas_call(
        body, grid=(S // ts,), in_specs=_swa_specs(S, ts, False) + [_vec(LANE)], out_specs=(ospec, ospec),
        out_shape=(oshape, oshape), compiler_params=_cp(1), name=name)(proj, proj, proj, proj, proj, sinks)


def _swa_bwd(proj, o, lse, dycat, sinks, *, name):
    S = proj.shape[0]
    ts = _tile(S, TS_SWA)
    nb = ts // WINDOW
    n = S // ts
    grp = SWA_HEADS // SWA_KV

    def body(q_ref, k_ref, kh_ref, v_ref, vh_ref, o_ref, lse_ref, do_ref, sink_ref,
             dq_ref, dk_ref, dv_ref, dsink_ref, dk_carry, dv_carry):
        i = pl.program_id(0)
        t = n - 1 - i

        @pl.when(i == 0)
        def _():
            dk_carry[...] = jnp.zeros_like(dk_carry)
            dv_carry[...] = jnp.zeros_like(dv_carry)
            dsink_ref[...] = jnp.zeros_like(dsink_ref)

        kext = jnp.concatenate([kh_ref[...], k_ref[...]], axis=0)
        vext = jnp.concatenate([vh_ref[...], v_ref[...]], axis=0)
        lane1 = lax.broadcasted_iota(jnp.int32, (1, LANE), 1)
        dkb = [[jnp.zeros((WINDOW, LANE), F32) for _ in range(SWA_KV)] for _ in range(nb + 1)]
        dvb = [[jnp.zeros((WINDOW, LANE), F32) for _ in range(SWA_KV)] for _ in range(nb + 1)]
        dsink = jnp.zeros((1, LANE), F32)
        for b in range(nb):
            rows = slice(WINDOW * b, WINDOW * (b + 1))
            for h in range(SWA_HEADS):
                kvh = h // grp
                kvl = slice(LANE * kvh, LANE * (kvh + 1))
                hl = slice(LANE * h, LANE * (h + 1))
                kk = kext[WINDOW * b:WINDOW * (b + 2), kvl]
                vv = vext[WINDOW * b:WINDOW * (b + 2), kvl]
                qh = q_ref[rows, hl]
                lse_h = lse_ref[rows, LANE * h:LANE * h + 1]
                p = jnp.exp(_swa_scores(qh, kk, t, b, ts) - lse_h)
                doh = do_ref[rows, hl]
                delta = _rowsum(doh * o_ref[rows, hl])
                ds = p * (_dot_nt(doh, vv) - delta)
                sk = sink_ref[:, h:h + 1]
                dsink = dsink + jnp.where(lane1 == h, -jnp.sum(jnp.exp(sk - lse_h) * delta, keepdims=True), 0.0)
                dq_ref[rows, hl] = (_dot(ds, kk) * _SWA_SCALE).astype(_ACT)
                dkk = _dot_tn(ds, qh) * _SWA_SCALE
                dvv = _dot_tn(p, doh)
                dkb[b][kvh] = dkb[b][kvh] + dkk[:WINDOW]
                dkb[b + 1][kvh] = dkb[b + 1][kvh] + dkk[WINDOW:]
                dvb[b][kvh] = dvb[b][kvh] + dvv[:WINDOW]
                dvb[b + 1][kvh] = dvb[b + 1][kvh] + dvv[WINDOW:]
        dsink_ref[...] += dsink
        for dref, blocks, carry in ((dk_ref, dkb, dk_carry), (dv_ref, dvb, dv_carry)):
            old = carry[...]
            for b in range(1, nb + 1):
                blk = jnp.concatenate(blocks[b], axis=1)
                if b == nb:
                    blk = blk + old
                dref[WINDOW * (b - 1):WINDOW * b, :] = blk.astype(_ACT)
            carry[...] = jnp.concatenate(blocks[0], axis=1)

    hw = SWA_HEADS * LANE
    rev = lambda i: (n - 1 - i, 0)
    mix = lambda i: (n - 1 - i, (SSD_INNER + MLA_HEADS * LANE) // hw)
    return pl.pallas_call(
        body, grid=(n,),
        in_specs=_swa_specs(S, ts, True) + [pl.BlockSpec((ts, hw), rev), pl.BlockSpec((ts, hw), rev),
                                            pl.BlockSpec((ts, hw), mix), _vec(LANE)],
        out_specs=(pl.BlockSpec((ts, hw), rev), pl.BlockSpec((ts, _SWA_KW), rev), pl.BlockSpec((ts, _SWA_KW), rev),
                   _vec(LANE)),
        out_shape=(jax.ShapeDtypeStruct((S, hw), _ACT), jax.ShapeDtypeStruct((S, _SWA_KW), _ACT),
                   jax.ShapeDtypeStruct((S, _SWA_KW), _ACT), jax.ShapeDtypeStruct((1, LANE), F32)),
        scratch_shapes=[pltpu.VMEM((WINDOW, _SWA_KW), F32), pltpu.VMEM((WINDOW, _SWA_KW), F32)],
        compiler_params=_cp(1), name=name)(proj, proj, proj, proj, proj, o, lse, dycat, sinks)


def _exchange(arrays, *, scatter, name):
    n = len(arrays)

    def body(*refs):
        ins, outs = refs[:n], refs[n:2 * n]
        send_sems, recv_sems, loc_sems = refs[2 * n:]
        x, y, c = lax.axis_index("x"), lax.axis_index("y"), lax.axis_index("c")
        me = 4 * x + 2 * y + c

        def src(i, dest):
            return ins[i].at[dest] if scatter else ins[i]

        local = [pltpu.make_async_copy(src(i, me), outs[i].at[me], loc_sems.at[i]) for i in range(n)]
        for cp in local:
            cp.start()
        sends, recvs = [], []
        for k in range(1, NDEV):
            px = 1 - x if k & 4 else x
            py = 1 - y if k & 2 else y
            pc = 1 - c if k & 1 else c
            peer = 4 * px + 2 * py + pc
            for i in range(n):
                common = dict(send_sem=send_sems.at[i, k - 1], recv_sem=recv_sems.at[i, k - 1],
                              device_id=(px, py, pc), device_id_type=pl.DeviceIdType.MESH)
                sends.append(pltpu.make_async_remote_copy(src_ref=src(i, peer), dst_ref=outs[i].at[me], **common))
                recvs.append(pltpu.make_async_remote_copy(src_ref=src(i, peer), dst_ref=outs[i].at[peer], **common))
        for cp in sends:
            cp.start()
        for cp in recvs:
            cp.wait_recv()
        for cp in sends:
            cp.wait_send()
        for cp in local:
            cp.wait()

    hbm = pl.BlockSpec(memory_space=pl.ANY)
    out_shape = tuple(jax.ShapeDtypeStruct(a.shape if scatter else (NDEV,) + a.shape, a.dtype) for a in arrays)
    return pl.pallas_call(
        body, in_specs=[hbm] * n, out_specs=tuple([hbm] * n), out_shape=out_shape,
        scratch_shapes=[pltpu.SemaphoreType.DMA((n, NDEV - 1)), pltpu.SemaphoreType.DMA((n, NDEV - 1)),
                        pltpu.SemaphoreType.DMA((n,))],
        name=name)(*arrays)


def _adamw(w, m, v, parts, *, name):
    R, C = w.shape
    npart = parts.shape[0]
    cap = max(SUB, ((1 << 18) // C) // SUB * SUB)
    tr = _tile(R, cap, SUB)

    def body(w_ref, m_ref, v_ref, p_ref, g_ref, d_ref, mo_ref, vo_ref):
        g = p_ref[0]
        for k in range(1, npart):
            g = g + p_ref[k]
        mn = ADAM_B1 * m_ref[...] + (1.0 - ADAM_B1) * g
        vn = ADAM_B2 * v_ref[...] + (1.0 - ADAM_B2) * (g * g)
        m_hat = mn / (1.0 - ADAM_B1 ** ADAM_STEP)
        v_hat = vn / (1.0 - ADAM_B2 ** ADAM_STEP)
        g_ref[...] = g
        d_ref[...] = -ADAM_LR * (m_hat / (jnp.sqrt(v_hat) + ADAM_EPS) + ADAM_WD * w_ref[...])
        mo_ref[...] = mn
        vo_ref[...] = vn

    spec = pl.BlockSpec((tr, C), lambda i: (i, 0))
    oshape = jax.ShapeDtypeStruct((R, C), F32)
    return pl.pallas_call(
        body, grid=(R // tr,), in_specs=[spec] * 3 + [pl.BlockSpec((npart, tr, C), lambda i: (0, i, 0))],
        out_specs=(spec,) * 4, out_shape=(oshape,) * 4, compiler_params=_cp(1), name=name)(w, m, v, parts)


def _adamw_layer(l, w, m, v, parts, prev, *, name):
    L, R, C = w.shape
    npart = parts.shape[0]
    cap = max(SUB, ((1 << 18) // C) // SUB * SUB)
    tr = _tile(R, cap, SUB)
    nprev = 0 if prev is None else 4

    def body(*refs):
        w_ref, m_ref, v_ref, p_ref = refs[:4]
        g_ref, d_ref, mo_ref, vo_ref = refs[4 + nprev:]
        g = p_ref[0]
        for k in range(1, npart):
            g = g + p_ref[k]
        mn = ADAM_B1 * m_ref[...] + (1.0 - ADAM_B1) * g
        vn = ADAM_B2 * v_ref[...] + (1.0 - ADAM_B2) * (g * g)
        m_hat = mn / (1.0 - ADAM_B1 ** ADAM_STEP)
        v_hat = vn / (1.0 - ADAM_B2 ** ADAM_STEP)
        g_ref[...] = g
        d_ref[...] = -ADAM_LR * (m_hat / (jnp.sqrt(v_hat) + ADAM_EPS) + ADAM_WD * w_ref[...])
        mo_ref[...] = mn
        vo_ref[...] = vn

    spec = pl.BlockSpec((None, tr, C), lambda i: (l, i, 0))
    oshape = jax.ShapeDtypeStruct((L, R, C), F32)
    return pl.pallas_call(
        body, grid=(R // tr,),
        in_specs=[spec] * 3 + [pl.BlockSpec((npart, tr, C), lambda i: (0, i, 0))]
        + [pl.BlockSpec(memory_space=pl.ANY)] * nprev,
        out_specs=(spec,) * 4, out_shape=(oshape,) * 4,
        input_output_aliases={4 + k: k for k in range(nprev)},
        compiler_params=_cp(1), name=name)(w, m, v, parts, *(prev or ()))


_HBM = pl.BlockSpec(memory_space=pltpu.HBM)
_SEM = pl.BlockSpec(memory_space=pltpu.SEMAPHORE)
_EFFECT = pltpu.SideEffectType.DATAFLOW_SIDE_EFFECTING


def _peers():
    x, y, c = lax.axis_index("x"), lax.axis_index("y"), lax.axis_index("c")
    out = []
    for k in range(1, NDEV):
        px = 1 - x if k & 4 else x
        py = 1 - y if k & 2 else y
        pc = 1 - c if k & 1 else c
        out.append((k - 1, (px, py, pc), 4 * px + 2 * py + pc))
    return 4 * x + 2 * y + c, out


def _xchg_start(arrays, *, scatter, name):
    n = len(arrays)
    lands = [lax.empty(a.shape if scatter else (NDEV,) + a.shape, a.dtype) for a in arrays]

    def body(*refs):
        ins, lnd = refs[:n], refs[n:2 * n]
        send_sems, recv_sems = refs[2 * n], refs[2 * n + 1]
        token = refs[-1]
        me, peers = _peers()
        for k, dev, peer in peers:
            for i in range(n):
                pltpu.make_async_remote_copy(
                    src_ref=ins[i].at[peer] if scatter else ins[i], dst_ref=lnd[i].at[me],
                    send_sem=send_sems.at[i * (NDEV - 1) + k], recv_sem=recv_sems.at[i * (NDEV - 1) + k],
                    device_id=dev, device_id_type=pl.DeviceIdType.MESH).start()
        token[...] = jnp.zeros_like(token)

    sems = pltpu.SemaphoreType.DMA((n * (NDEV - 1),))
    res = pl.pallas_call(
        body, name=name,
        out_shape=(sems, sems) + tuple(pltpu.HBM(t.shape, t.dtype) for t in list(arrays) + lands)
        + (jax.ShapeDtypeStruct((SUB, LANE), F32),),
        in_specs=[_HBM] * (2 * n), out_specs=(_SEM, _SEM) + (_HBM,) * (2 * n) + (pl.BlockSpec(memory_space=pltpu.VMEM),),
        input_output_aliases={i: 2 + i for i in range(2 * n)},
        compiler_params=pltpu.CompilerParams(has_side_effects=_EFFECT),
    )(*[pltpu.with_memory_space_constraint(t, pltpu.HBM) for t in list(arrays) + lands])
    return dict(send=res[0], recv=res[1], thru=list(res[2:2 + 2 * n]), token=res[-1], scatter=scatter, n=n)


def _xchg_wait(handle, after, *, name):
    n, scatter = handle["n"], handle["scatter"]
    thru = handle["thru"]

    def body(*refs):
        ins, lnd = refs[:n], refs[n:2 * n]
        send_sems, recv_sems = refs[2 * n], refs[2 * n + 1]
        me, peers = _peers()
        for k, dev, peer in peers:
            for i in range(n):
                cp = pltpu.make_async_remote_copy(
                    src_ref=ins[i].at[peer] if scatter else ins[i], dst_ref=lnd[i].at[peer],
                    send_sem=send_sems.at[i * (NDEV - 1) + k], recv_sem=recv_sems.at[i * (NDEV - 1) + k],
                    device_id=dev, device_id_type=pl.DeviceIdType.MESH)
                cp.wait_send()
                cp.wait_recv()

    res = pl.pallas_call(
        body, name=name, out_shape=tuple(pltpu.HBM(t.shape, t.dtype) for t in thru),
        in_specs=[_HBM] * (2 * n) + [_SEM, _SEM, pl.BlockSpec(memory_space=pl.ANY)], out_specs=(_HBM,) * (2 * n),
        input_output_aliases={i: i for i in range(2 * n)},
        compiler_params=pltpu.CompilerParams(has_side_effects=_EFFECT),
    )(*thru, handle["send"], handle["recv"], after)
    return list(res[:n]), list(res[n:])


def _pad_heads(w, nh, hd, axis=-1):
    axis = axis % w.ndim
    shp = w.shape
    w = w.reshape(shp[:axis] + (nh, hd) + shp[axis + 1:])
    pads = [(0, 0)] * w.ndim
    pads[axis + 1] = (0, LANE - hd)
    return jnp.pad(w, pads).reshape(shp[:axis] + (nh * LANE,) + shp[axis + 1:])


def _unpad_heads(w, nh, hd, axis=-1):
    axis = axis % w.ndim
    shp = w.shape
    w = w.reshape(shp[:axis] + (nh, LANE) + shp[axis + 1:])
    w = lax.slice_in_dim(w, 0, hd, axis=axis + 1)
    return w.reshape(shp[:axis] + (nh * hd,) + shp[axis + 1:])


_O_DT = SSD_INNER + SSD_XBC
_O_CQ = _O_DT + SSD_HEADS
_O_CKV = _O_CQ + MLA_QR
_O_KR = _O_CKV + MLA_KVR
_O_SQ = _O_KR + MLA_ROPE
_O_SK = _O_SQ + SWA_HEADS * SWA_HD
_O_SV = _O_SK + SWA_KV * SWA_HD


def _w_in_to_padded(w, axis=-1):
    axis = axis % w.ndim
    cut = lambda a, b: lax.slice_in_dim(w, a, b, axis=axis)
    z, xbc, dt = cut(0, SSD_INNER), cut(SSD_INNER, _O_DT), cut(_O_DT, _O_CQ)
    cq, ckv, kr = cut(_O_CQ, _O_CKV), cut(_O_CKV, _O_KR), cut(_O_KR, _O_SQ)
    sq, sk, sv = cut(_O_SQ, _O_SK), cut(_O_SK, _O_SV), cut(_O_SV, D_IN)
    zeros = lambda n: jnp.zeros(w.shape[:axis] + (n,) + w.shape[axis + 1:], w.dtype)
    return jnp.concatenate([xbc, z, cq, ckv, dt, zeros(ROPE_LANE - SSD_HEADS), kr, zeros(LANE - ROPE_LANE - MLA_ROPE),
                            _pad_heads(sq, SWA_HEADS, SWA_HD, axis), _pad_heads(sk, SWA_KV, SWA_HD, axis),
                            _pad_heads(sv, SWA_KV, SWA_HD, axis)], axis=axis)


def _w_in_from_padded(g, axis=-1):
    axis = axis % g.ndim
    cut = lambda a, b: lax.slice_in_dim(g, a, b, axis=axis)
    xbc, z, cq, ckv = cut(C_XBC, C_Z), cut(C_Z, C_CQ), cut(C_CQ, C_CKV), cut(C_CKV, C_MISC)
    dt, kr = cut(C_MISC, C_MISC + SSD_HEADS), cut(C_MISC + ROPE_LANE, C_MISC + ROPE_LANE + MLA_ROPE)
    sq = _unpad_heads(cut(C_SQ, C_SK), SWA_HEADS, SWA_HD, axis)
    sk = _unpad_heads(cut(C_SK, C_SV), SWA_KV, SWA_HD, axis)
    sv = _unpad_heads(cut(C_SV, D_INP), SWA_KV, SWA_HD, axis)
    return jnp.concatenate([z, xbc, dt, cq, ckv, kr, sq, sk, sv], axis=axis)


def _w_out_to_padded(w):
    a = SSD_INNER
    b = a + MLA_HEADS * MLA_V
    return jnp.concatenate([w[..., :a, :], _pad_heads(w[..., a:b, :], MLA_HEADS, MLA_V, axis=-2),
                            _pad_heads(w[..., b:, :], SWA_HEADS, SWA_HD, axis=-2)], axis=-2)


def _w_out_from_padded(g):
    a = SSD_INNER
    b = a + MLA_HEADS * LANE
    return jnp.concatenate([g[..., :a, :], _unpad_heads(g[..., a:b, :], MLA_HEADS, MLA_V, axis=-2),
                            _unpad_heads(g[..., b:, :], SWA_HEADS, SWA_HD, axis=-2)], axis=-2)


def _w_ukv_to_padded(w):
    w4 = w.reshape(w.shape[:-1] + (MLA_HEADS, MLA_NOPE + MLA_V))
    flat = lambda t: t.reshape(w.shape[:-1] + (MLA_HEADS * t.shape[-1],))
    return jnp.concatenate([_pad_heads(flat(w4[..., :MLA_NOPE]), MLA_HEADS, MLA_NOPE),
                            _pad_heads(flat(w4[..., MLA_NOPE:]), MLA_HEADS, MLA_V)], axis=-1)


def _w_ukv_from_padded(g):
    hw = MLA_HEADS * LANE
    gk = _unpad_heads(g[..., :hw], MLA_HEADS, MLA_NOPE).reshape(g.shape[:-1] + (MLA_HEADS, MLA_NOPE))
    gv = _unpad_heads(g[..., hw:], MLA_HEADS, MLA_V).reshape(g.shape[:-1] + (MLA_HEADS, MLA_V))
    return jnp.concatenate([gk, gv], axis=-1).reshape(g.shape[:-1] + (MLA_HEADS * (MLA_NOPE + MLA_V),))


def _pad_lane(v):
    return jnp.pad(v, [(0, 0)] * (v.ndim - 1) + [(0, LANE - v.shape[-1])])


def _rope_tables(positions):
    inv_freq = ROPE_THETA ** (-jnp.arange(0, MLA_ROPE, 2, dtype=F32) / MLA_ROPE)
    ang = positions.astype(F32).reshape(-1, 1) * inv_freq
    cos, sin = jnp.cos(ang), jnp.sin(ang)
    S = ang.shape[0]
    one, zero = jnp.ones((S, ROPE_LANE), F32), jnp.zeros((S, ROPE_LANE), F32)
    tail1, tail0 = jnp.ones((S, LANE - ROPE_LANE - MLA_ROPE), F32), jnp.zeros((S, LANE - ROPE_LANE - MLA_ROPE), F32)
    z16 = jnp.zeros_like(sin)
    return (jnp.concatenate([one, cos, cos, tail1], axis=1), jnp.concatenate([zero, -sin, z16, tail0], axis=1),
            jnp.concatenate([zero, z16, sin, tail0], axis=1))


def _layer_fwd(l, x_in, f_prev, gate_prev, mod, P, tabs):
    sh1, sc1, g1, sh2, sc2, g2 = [mod[k:k + 1] for k in range(6)]
    tag = f"l{l}_"
    if f_prev is None:
        x0 = x_in
        h1 = _norm_fwd(x0, P["n1g"], sc1, sh1, name=tag + "norm1")
    else:
        x0, h1 = _norm_fwd(x_in, P["n1g"], sc1, sh1, f=f_prev, gate=gate_prev, name=tag + "norm1")
    proj = _mm(h1, P["w_in"], tb=True, name=tag + "proj")
    P.update(P.pop("mid")(proj))
    y_ssd, hin = _ssd_fwd(proj, P["ssd_cw"], P["ssd_cb"], P["dtb"], P["alog"], P["dsk"],
                          P["ssd_ng"], name=tag + "ssd")
    q, k, v, vt = _mla_prep_fwd(proj, *tabs, P["gq"], P["gkv"], P["w_uq"], P["w_ukv"], name=tag + "mla_prep")
    o_mla, lse_mla, lset_mla = _attn_fwd(q, k, vt, name=tag + "mla_attn")
    o_swa, lse_swa = _swa_fwd(proj, P["sinks"], name=tag + "swa")
    ycat = jnp.concatenate([y_ssd.astype(_ACT), o_mla.astype(_ACT), o_swa.astype(_ACT)], axis=1)
    y = _mm(ycat, P["w_out"], name=tag + "out")
    P.update(P.pop("late")(y))
    x1, h2 = _norm_fwd(x0, P["n2g"], sc2, sh2, f=y, gate=g1, name=tag + "norm2")
    up = _mm(h2, P["w_up"], tb=True, name=tag + "up")
    act = _ffn_act_fwd(up, P["fcw"], P["fcb"], name=tag + "ffn_act")
    f = _mm(act, P["w_down"], name=tag + "down")
    saved = dict(x0=x0, h1=h1, proj=proj, hin=hin, q=q, k=k, v=v, o_mla=o_mla, lse_mla=lse_mla, lset_mla=lset_mla, o_swa=o_swa,
                 lse_swa=lse_swa, ycat=ycat, y=y, x1=x1, h2=h2, up=up, act=act, f=f, mod=mod)
    return x1, f, g2, saved


def _layer_bwd(l, dxo, sv, P, tabs, on_part):
    mod = sv["mod"]
    sh1, sc1, g1, sh2, sc2, g2 = [mod[k:k + 1] for k in range(6)]
    tag = f"l{l}_b_"
    G = {}
    df, dg2 = _gate_bwd(dxo, sv["f"], g2, name=tag + "gate2")
    dact = _mm(df, P["w_down"], tb=True, name=tag + "dact")
    G["w_down"] = _mm(sv["act"], df, ta=True, name=tag + "dw_down")
    dup, G["fcw"], G["fcb"] = _ffn_bwd(sv["up"], dact, P["fcw"], P["fcb"], name=tag + "ffn")
    dh2 = _mm(dup, P["w_up"], name=tag + "dh2")
    G["w_up"] = _mm(dup, sv["h2"], ta=True, name=tag + "dw_up")
    token = on_part(l, "ffn", G)
    if token is not None:
        sc2 = sc2 + token
    dx1, G["n2g"], dsc2, dsh2 = _norm_bwd(dh2, sv["x1"], dxo, P["n2g"], sc2, name=tag + "norm2")
    dy, dg1 = _gate_bwd(dx1, sv["y"], g1, name=tag + "gate1")
    dycat = _mm(dy, P["w_out"], tb=True, name=tag + "dycat")
    G["w_out"] = _mm(sv["ycat"], dy, ta=True, name=tag + "dw_out")
    token = on_part(l, "out", G)
    ssd_cb = P["ssd_cb"] if token is None else P["ssd_cb"] + token
    proj = sv["proj"]
    (dpre, dz, dmisc_ssd, G["ssd_cw"], G["ssd_cb"], G["dtb"], G["alog"], G["dsk"], G["ssd_ng"]) = _ssd_bwd(
        proj, dycat, sv["hin"], P["ssd_cw"], ssd_cb, P["dtb"], P["alog"], P["dsk"],
        P["ssd_ng"], name=tag + "ssd")
    att = (sv["q"], sv["k"], sv["v"], sv["o_mla"])
    dq = _attn_bwd_dq(*att, sv["lse_mla"], dycat, name=tag + "mla_dq")
    dk, dv = _attn_bwd_dkv(*att, sv["lset_mla"], dycat, name=tag + "mla_dkv")
    dcq, dckv, dmisc, dqh, dkv, qn, kvn, G["gq"], G["gkv"] = _mla_prep_bwd(
        proj, dq, dk, dv, dmisc_ssd, *tabs, P["gq"], P["gkv"], P["w_uq"], P["w_ukv"], name=tag + "mla_prep")
    G["w_uq"] = _mm(qn, dqh, ta=True, name=tag + "dw_uq")
    G["w_ukv"] = _mm(kvn, dkv, ta=True, name=tag + "dw_ukv")
    dsq, dsk_, dsv_, G["sinks"] = _swa_bwd(proj, sv["o_swa"], sv["lse_swa"], dycat, P["sinks"], name=tag + "swa")
    dproj = jnp.concatenate([dpre, dz, dcq, dckv, dmisc, dsq, dsk_, dsv_], axis=1)
    G["w_in"] = _mm(dproj, sv["h1"], ta=True, name=tag + "dw_in")
    token = on_part(l, "mixer", G)
    if token is not None:
        sc1 = sc1 + token
    dh1 = _mm(dproj, P["w_in"], name=tag + "dh1")
    dx0, G["n1g"], dsc1, dsh1 = _norm_bwd(dh1, sv["x0"], dx1, P["n1g"], sc1, name=tag + "norm1")
    G["mod"] = jnp.concatenate([dsh1, dsc1, dg1, dsh2, dsc2, dg2], axis=0)
    return dx0, G


def _local_step(x, tgt, mods, get_params, tabs, final_g, on_grads, on_part):
    saved, params = [], []
    xin, f, gate = x, None, None
    for l in range(DEPTH):
        params.append(get_params(l, x if f is None else f))
        xin, f, gate, sv = _layer_fwd(l, xin, f, gate, mods[l], params[l], tabs)
        saved.append(sv)
    loss, dx, dfinal = _final_loss(xin, f, gate, final_g, tgt, name="final_loss")
    for l in reversed(range(DEPTH)):
        dx, G = _layer_bwd(l, dx, saved[l], params[l], tabs, on_part)
        on_grads(l, G)
    return loss[0, 0], dx, dfinal


_WEIGHTS = ['ada_w', 'ada_b', 'norm1_g', 'norm2_g', 'w_in', 'ssd_conv_w', 'ssd_conv_b', 'ssd_dt_bias', 'ssd_a_log',
            'ssd_d', 'ssd_norm_g', 'mla_q_norm_g', 'mla_w_uq', 'mla_kv_norm_g', 'mla_w_ukv', 'swa_sinks', 'w_out',
            'ffn_w_up', 'ffn_conv_w', 'ffn_conv_b', 'ffn_w_down', 'final_norm_g']
_INPUTS = ['x', 'c', 'positions'] + _WEIGHTS + ['loss_target'] + ['m_' + n for n in _WEIGHTS] + ['v_' + n for n in _WEIGHTS]
_SMALL = [('ada_b', 'mod'), ('norm1_g', 'n1g'), ('norm2_g', 'n2g'), ('ssd_conv_b', 'ssd_cb'), ('ssd_dt_bias', 'dtb'),
          ('ssd_a_log', 'alog'), ('ssd_d', 'dsk'), ('ssd_norm_g', 'ssd_ng'), ('mla_q_norm_g', 'gq'),
          ('mla_kv_norm_g', 'gkv'), ('swa_sinks', 'sinks'), ('ffn_conv_b', 'fcb')]
_SHARDED = [('w_in', 'w_in', 2), ('ssd_conv_w', 'ssd_cw', 2), ('mla_w_uq', 'w_uq', 2), ('mla_w_ukv', 'w_ukv', 2),
            ('w_out', 'w_out', 1), ('ffn_w_up', 'w_up', 2), ('ffn_conv_w', 'fcw', 2), ('ffn_w_down', 'w_down', 1)]
_SHARDED_NAMES = [n for n, _, _ in _SHARDED]
_TRANSPOSED = ('w_in', 'ffn_w_up')


def _pack_small(per_layer, final):
    parts = []
    for name, _ in _SMALL:
        v = per_layer[name]
        v = v.reshape(DEPTH, -1)
        pad = (-v.shape[1]) % LANE
        parts.append(jnp.pad(v, ((0, 0), (0, pad))).reshape(-1))
    parts.append(final.reshape(-1))
    return jnp.concatenate(parts).reshape(-1, LANE)


def _unpack_small(packed, shapes):
    flat = packed.reshape(-1)
    out, off = {}, 0
    for name, _ in _SMALL:
        n = math.prod(shapes[name][1:])
        npad = n + (-n) % LANE
        out[name] = flat[off:off + DEPTH * npad].reshape(DEPTH, npad)[:, :n].reshape(shapes[name])
        off += DEPTH * npad
    out['final_norm_g'] = flat[off:off + D]
    return out


def _shard_major(g, axis):
    shp = g.shape
    g = g.reshape(shp[:axis] + (NDEV, shp[axis] // NDEV) + shp[axis + 1:])
    return jnp.moveaxis(g, axis, 0)


def _unshard(g, axis):
    g = jnp.moveaxis(g, 0, axis)
    shp = g.shape
    return g.reshape(shp[:axis] + (shp[axis] * shp[axis + 1],) + shp[axis + 2:])


def kernel(x, c, positions, ada_w, ada_b, norm1_g, norm2_g, w_in, ssd_conv_w, ssd_conv_b, ssd_dt_bias, ssd_a_log, ssd_d, ssd_norm_g, mla_q_norm_g, mla_w_uq, mla_kv_norm_g, mla_w_ukv, swa_sinks, w_out, ffn_w_up, ffn_conv_w, ffn_conv_b, ffn_w_down, final_norm_g, loss_target, m_ada_w, m_ada_b, m_norm1_g, m_norm2_g, m_w_in, m_ssd_conv_w, m_ssd_conv_b, m_ssd_dt_bias, m_ssd_a_log, m_ssd_d, m_ssd_norm_g, m_mla_q_norm_g, m_mla_w_uq, m_mla_kv_norm_g, m_mla_w_ukv, m_swa_sinks, m_w_out, m_ffn_w_up, m_ffn_conv_w, m_ffn_conv_b, m_ffn_w_down, m_final_norm_g, v_ada_w, v_ada_b, v_norm1_g, v_norm2_g, v_w_in, v_ssd_conv_w, v_ssd_conv_b, v_ssd_dt_bias, v_ssd_a_log, v_ssd_d, v_ssd_norm_g, v_mla_q_norm_g, v_mla_w_uq, v_mla_kv_norm_g, v_mla_w_ukv, v_swa_sinks, v_w_out, v_ffn_w_up, v_ffn_conv_w, v_ffn_conv_b, v_ffn_w_down, v_final_norm_g):
    a = dict(zip(_INPUTS, (x, c, positions, ada_w, ada_b, norm1_g, norm2_g, w_in, ssd_conv_w, ssd_conv_b, ssd_dt_bias, ssd_a_log, ssd_d, ssd_norm_g, mla_q_norm_g, mla_w_uq, mla_kv_norm_g, mla_w_ukv, swa_sinks, w_out, ffn_w_up, ffn_conv_w, ffn_conv_b, ffn_w_down, final_norm_g, loss_target, m_ada_w, m_ada_b, m_norm1_g, m_norm2_g, m_w_in, m_ssd_conv_w, m_ssd_conv_b, m_ssd_dt_bias, m_ssd_a_log, m_ssd_d, m_ssd_norm_g, m_mla_q_norm_g, m_mla_w_uq, m_mla_kv_norm_g, m_mla_w_ukv, m_swa_sinks, m_w_out, m_ffn_w_up, m_ffn_conv_w, m_ffn_conv_b, m_ffn_w_down, m_final_norm_g, v_ada_w, v_ada_b, v_norm1_g, v_norm2_g, v_w_in, v_ssd_conv_w, v_ssd_conv_b, v_ssd_dt_bias, v_ssd_a_log, v_ssd_d, v_ssd_norm_g, v_mla_q_norm_g, v_mla_w_uq, v_mla_kv_norm_g, v_mla_w_ukv, v_swa_sinks, v_w_out, v_ffn_w_up, v_ffn_conv_w, v_ffn_conv_b, v_ffn_w_down, v_final_norm_g)))
    axes = ("x", "y", "c")
    me = 4 * lax.axis_index("x") + 2 * lax.axis_index("y") + lax.axis_index("c")
    ncol = ada_w.shape[-1]

    c_all = _exchange([c], scatter=False, name="gather_c")[0]
    c_act = _silu_call(c_all.reshape(NDEV, D), name="c_act")
    mod_part = jnp.stack([_mm(c_act, ada_w[l], name=f"mod{l}") for l in range(DEPTH)])
    mod_all = _exchange([mod_part], scatter=False, name="gather_mod")[0]
    mod_mine = lax.dynamic_index_in_dim(mod_all, me, axis=2, keepdims=False)
    mods = (jnp.moveaxis(mod_mine, 0, 1).reshape(DEPTH, 6 * D) + ada_b).reshape(DEPTH, 6, D)
    tabs = _rope_tables(positions)

    mxu_names = ('w_in', 'mla_w_uq', 'mla_w_ukv', 'w_out', 'ffn_w_up', 'ffn_w_down')
    kform = lambda n, t: jnp.swapaxes(t, -1, -2) if n in _TRANSPOSED else t
    shard_of = {n: (key, 1 if n in _TRANSPOSED else ax) for n, key, ax in _SHARDED}
    gather_groups = (("early", _SHARDED_NAMES[:4]), ("mid", _SHARDED_NAMES[4:5]), ("late", _SHARDED_NAMES[5:]))
    mods, raw = lax.optimization_barrier((mods, {n: a[n] for n in _SHARDED_NAMES}))
    own_of = lambda names, l: [kform(n, raw[n][l]).astype(_MXU) if n in mxu_names else raw[n][l] for n in names]
    gathers, prev = [], None
    for l in range(DEPTH):
        gathers.append({})
        for grp, names in gather_groups:
            srcs = own_of(names, l)
            if prev is not None:
                srcs, _ = lax.optimization_barrier((srcs, prev))
            gathers[l][grp] = _xchg_start(srcs, scatter=False, name=f"gather_start_{grp}{l}")
            prev = gathers[l][grp]["token"]

    def place_own(landed, mine):
        return [lax.dynamic_update_index_in_dim(t, o, me, 0) for t, o in zip(landed, mine)]

    def gathered(l, grp, after):
        names = dict(gather_groups)[grp]
        mine, landed = _xchg_wait(gathers[l][grp], after, name=f"gather_wait_{grp}{l}")
        return {n: _unshard(g, shard_of[n][1] - 1) for n, g in zip(names, place_own(landed, mine))}

    def get_params(l, after):
        full = gathered(l, "early", after)
        vec = lambda t: t[l].reshape(1, -1)

        def mid(after2):
            return dict(w_out=_w_out_to_padded(gathered(l, "mid", after2)['w_out']))

        def late(after2):
            rest = gathered(l, "late", after2)
            return dict(w_up=rest['ffn_w_up'], w_down=rest['ffn_w_down'], fcw=rest['ffn_conv_w'])

        return dict(
            w_in=_w_in_to_padded(full['w_in'], axis=0), w_uq=_pad_heads(full['mla_w_uq'], MLA_HEADS, MLA_NOPE + MLA_ROPE),
            w_ukv=_w_ukv_to_padded(full['mla_w_ukv']), ssd_cw=full['ssd_conv_w'], mid=mid, late=late,
            ssd_cb=vec(ssd_conv_b), dtb=vec(_pad_lane(ssd_dt_bias)), alog=vec(_pad_lane(ssd_a_log)),
            dsk=vec(_pad_lane(ssd_d)), ssd_ng=vec(ssd_norm_g), gq=vec(mla_q_norm_g), gkv=vec(mla_kv_norm_g),
            sinks=vec(_pad_lane(swa_sinks)), fcb=vec(ffn_conv_b), n1g=vec(norm1_g), n2g=vec(norm2_g))

    unpad = dict(w_in=functools.partial(_w_in_from_padded, axis=0), w_out=_w_out_from_padded, w_ukv=_w_ukv_from_padded,
                 w_uq=lambda g: _unpad_heads(g, MLA_HEADS, MLA_NOPE + MLA_ROPE))
    scatter_groups = (("ffn", _SHARDED_NAMES[5:]), ("out", _SHARDED_NAMES[4:5]), ("mixer", _SHARDED_NAMES[:4]))
    grads = [None] * DEPTH
    scatters = [dict() for _ in range(DEPTH)]

    def on_part(l, grp, G):
        parts = [_shard_major(unpad.get(shard_of[n][0], lambda g: g)(G[shard_of[n][0]]), shard_of[n][1] - 1)
                 for n in dict(scatter_groups)[grp]]
        scatters[l][grp] = _xchg_start(parts, scatter=True, name=f"scatter_start_{grp}{l}")
        return scatters[l][grp]["token"][0, 0]

    def on_grads(l, G):
        grads[l] = G

    mods = mods + sum(g[grp]["token"][0, 0] for g in gathers for grp, _ in gather_groups)
    loss, dx, dfinal = _local_step(x[0], loss_target[0], mods, get_params, tabs, final_norm_g.reshape(1, D),
                                   on_grads, on_part)
    loss = lax.psum(loss, axes)

    stack = lambda key: jnp.stack([grads[l][key] for l in range(DEPTH)])
    small_g = {name: stack(key).reshape(DEPTH, -1) for name, key in _SMALL}
    small_parts = _exchange([_pack_small(small_g, dfinal)], scatter=False, name="gather_small")[0]

    out_g, out_d, out_m, out_v = {}, {}, {}, {}
    chain = {name: None for name in _SHARDED_NAMES}
    for l in reversed(range(DEPTH)):
        for grp, names in scatter_groups:
            mine, landed = _xchg_wait(scatters[l][grp], dx, name=f"scatter_wait_{grp}{l}")
            parts = place_own(landed, [lax.dynamic_index_in_dim(t, me, 0, keepdims=False) for t in mine])
            for name, pv in zip(names, parts):
                chain[name] = _adamw_layer(l, kform(name, a[name]), kform(name, a['m_' + name]),
                                           kform(name, a['v_' + name]), pv, chain[name], name=f"adamw_{name}{l}")
    for name in _SHARDED_NAMES:
        out_g[name], out_d[name], out_m[name], out_v[name] = [kform(name, t) for t in chain[name]]

    def update(name, wv, mv, vv, pv):
        shp = wv.shape
        r = lambda t: t.reshape((-1, shp[-1]))
        res = _adamw(r(wv), r(mv), r(vv), pv.reshape((pv.shape[0], -1, shp[-1])), name="adamw_" + name)
        out_g[name], out_d[name], out_m[name], out_v[name] = [t.reshape(shp) for t in res]

    n_ada = DEPTH * 6 * D // LANE
    dmod_all = small_parts[:, :n_ada].reshape(NDEV, DEPTH, 6 * D)
    dmod_mine = lax.dynamic_slice_in_dim(dmod_all, me * ncol, ncol, axis=2)
    g_ada = jnp.stack([_mm(c_act, dmod_mine[:, l], ta=True, name=f"dw_ada{l}") for l in range(DEPTH)])
    update('ada_w', ada_w, m_ada_w, v_ada_w, g_ada[None])
    shapes = {n: a[n].shape for n, _ in _SMALL}
    pk = lambda pre: _pack_small({n: a[pre + n] for n, _ in _SMALL}, a[pre + 'final_norm_g'])
    res = _adamw(pk(''), pk('m_'), pk('v_'), small_parts, name="adamw_small")
    for dst, t in zip((out_g, out_d, out_m, out_v), res):
        dst.update(_unpack_small(t, shapes))

    outs = [loss, dx[None]]
    for dct in (out_g, out_d, out_m, out_v):
        outs += [dct[n] for n in _WEIGHTS]
    return tuple(outs)
```

```python
import functools
import math

import jax
import jax.numpy as jnp
from jax import lax
from jax.experimental import pallas as pl
from jax.experimental.pallas import tpu as pltpu

F32 = jnp.float32
_MXU = jnp.bfloat16
_ACT = jnp.bfloat16
_HI = lax.Precision.HIGHEST
EPS = 1e-6
NDEV = 8
DEPTH = 4
D = 1024
LANE = 128
SUB = 8
VMEM_LIMIT = 56 * 1024 * 1024

SSD_INNER, SSD_STATE, SSD_HEADS, SSD_GROUPS, SSD_CHUNK, SSD_CONV = 512, 128, 8, 2, 128, 4
SSD_XBC = SSD_INNER + 2 * SSD_GROUPS * SSD_STATE
MLA_HEADS, MLA_NOPE, MLA_ROPE, MLA_V, MLA_QR, MLA_KVR = 4, 64, 32, 64, 256, 128
SWA_HEADS, SWA_KV, SWA_HD, WINDOW = 4, 2, 64, 128
D_FF, FFN_CONV = 2816, 3
D_IN = 2472
ROPE_THETA = 10000.0
C_XBC, C_Z, C_CQ, C_CKV, C_MISC, C_SQ, C_SK, C_SV, D_INP = 0, 1024, 1536, 1792, 1920, 2048, 2560, 2816, 3072
ROPE_LANE = 64
D_MIXP = 1536

ADAM_LR, ADAM_B1, ADAM_B2, ADAM_EPS, ADAM_WD, ADAM_STEP = 0.001, 0.9, 0.999, 1e-08, 0.01, 10

TS_ROW = 1024
TS_FFN = 256
TQ_ATT = 1024
TS_SWA = 512


def _tile(n, cap, q=LANE):
    best = None
    for t in range(q, min(n, cap) + 1, q):
        if n % t == 0:
            best = t
    return n if best is None else best


def _cp(ngrid):
    return pltpu.CompilerParams(dimension_semantics=("arbitrary",) * ngrid, vmem_limit_bytes=VMEM_LIMIT)


def _dot(a, b):
    return jnp.dot(a.astype(_MXU), b.astype(_MXU), preferred_element_type=F32)


def _dot_nt(a, b):
    return lax.dot_general(a.astype(_MXU), b.astype(_MXU), (((1,), (1,)), ((), ())), preferred_element_type=F32)


def _dot_tn(a, b):
    return jnp.dot(a.T.astype(_MXU), b.astype(_MXU), preferred_element_type=F32)


def _sigmoid(x):
    return 1.0 / (1.0 + jnp.exp(-x))


def _sigmoid_t(x):
    return 0.5 * jnp.tanh(0.5 * x) + 0.5


def _silu(x):
    return x * _sigmoid_t(x)


def _silu_grad(x):
    s = _sigmoid_t(x)
    return x * s, s * (1.0 + x * (1.0 - s))


def _dsilu(x):
    return _silu_grad(x)[1]


def _softplus(x):
    u = jnp.exp(-jnp.abs(x))
    w = 1.0 + u
    log1p = jnp.where(w == 1.0, u, jnp.log(w) * u / jnp.where(w == 1.0, 1.0, w - 1.0))
    return jnp.maximum(x, 0.0) + log1p


def _colsum(x):
    return jnp.sum(x, axis=0, keepdims=True)


def _rowsum(x):
    return jnp.sum(x, axis=1, keepdims=True)


def _shift_down(t, halo, j):
    if j == 0:
        return t
    n = t.shape[0]
    rolled = pltpu.roll(t, j, 0)
    row = lax.broadcasted_iota(jnp.int32, (SUB, t.shape[1]), 0)
    first = jnp.where(row < j, pltpu.roll(halo, j, 0), rolled[:SUB])
    return jnp.concatenate([first, rolled[SUB:]], axis=0) if n > SUB else first


def _shift_up(t, halo, j):
    if j == 0:
        return t
    n = t.shape[0]
    rolled = pltpu.roll(t, n - j, 0)
    row = lax.broadcasted_iota(jnp.int32, (SUB, t.shape[1]), 0)
    last = jnp.where(row >= SUB - j, pltpu.roll(halo, SUB - j, 0), rolled[n - SUB:])
    return jnp.concatenate([rolled[:n - SUB], last], axis=0) if n > SUB else last


def _mm(a, b, *, ta=False, tb=False, out_dtype=F32, name):
    if ta:
        K, M = a.shape
    else:
        M, K = a.shape
    if tb:
        N, K2 = b.shape
    else:
        K2, N = b.shape
    assert K == K2, (a.shape, b.shape, ta, tb)
    tk = _tile(K, 1536)
    nk = K // tk
    tm, tn = _tile(M, 2048 if nk == 1 else 1536), _tile(N, 1536 if nk == 1 else 1408)
    dn = (((0 if ta else 1,), (1 if tb else 0,)), ((), ()))

    def body(a_ref, b_ref, o_ref, *acc):
        part = lax.dot_general(a_ref[...].astype(_MXU), b_ref[...].astype(_MXU), dn, preferred_element_type=F32)
        if nk == 1:
            o_ref[...] = part.astype(out_dtype)
            return
        acc_ref, = acc
        k = pl.program_id(2)

        @pl.when(k == 0)
        def _():
            acc_ref[...] = part

        @pl.when(k > 0)
        def _():
            acc_ref[...] += part

        @pl.when(k == nk - 1)
        def _():
            o_ref[...] = acc_ref[...].astype(out_dtype)

    a_spec = pl.BlockSpec((tk, tm), lambda i, j, k: (k, i)) if ta else pl.BlockSpec((tm, tk), lambda i, j, k: (i, k))
    b_spec = pl.BlockSpec((tn, tk), lambda i, j, k: (j, k)) if tb else pl.BlockSpec((tk, tn), lambda i, j, k: (k, j))
    return pl.pallas_call(
        body, grid=(M // tm, N // tn, nk), in_specs=[a_spec, b_spec],
        out_specs=pl.BlockSpec((tm, tn), lambda i, j, k: (i, j)),
        out_shape=jax.ShapeDtypeStruct((M, N), out_dtype),
        scratch_shapes=[pltpu.VMEM((tm, tn), F32)] * (nk > 1), compiler_params=_cp(3), name=name)(a, b)


def _row(ts, w, col=0):
    return pl.BlockSpec((ts, w), lambda i: (i, col))


def _vec(w, r=1):
    return pl.BlockSpec((r, w), lambda i: (0, 0))


def _silu_call(x, name):
    def body(x_ref, o_ref):
        o_ref[...] = _silu(x_ref[...])
    return pl.pallas_call(body, out_shape=jax.ShapeDtypeStruct(x.shape, F32), name=name)(x)


def _norm_fwd(x, g, sc, sh, *, f=None, gate=None, name):
    S, dm = x.shape
    ts = _tile(S, TS_ROW, SUB)
    res = f is not None

    def body(*refs):
        if res:
            x_ref, f_ref, gate_ref, g_ref, sc_ref, sh_ref, xo_ref, h_ref = refs
            xv = x_ref[...] + gate_ref[...] * f_ref[...]
            xo_ref[...] = xv
        else:
            x_ref, g_ref, sc_ref, sh_ref, h_ref = refs
            xv = x_ref[...]
        rstd = lax.rsqrt(jnp.mean(xv * xv, axis=-1, keepdims=True) + EPS)
        h_ref[...] = ((xv * rstd) * g_ref[...] * (1.0 + sc_ref[...]) + sh_ref[...]).astype(_ACT)

    ins = [x] + ([f, gate] if res else []) + [g, sc, sh]
    in_specs = [_row(ts, dm)] + ([_row(ts, dm), _vec(dm)] if res else []) + [_vec(dm)] * 3
    h_shape = jax.ShapeDtypeStruct((S, dm), _ACT)
    if res:
        out_shape, out_specs = (jax.ShapeDtypeStruct((S, dm), F32), h_shape), (_row(ts, dm), _row(ts, dm))
    else:
        out_shape, out_specs = h_shape, _row(ts, dm)
    return pl.pallas_call(body, grid=(S // ts,), in_specs=in_specs, out_specs=out_specs, out_shape=out_shape,
                          compiler_params=_cp(1), name=name)(*ins)


def _norm_bwd(dh, x, dres, g, sc, *, name):
    S, dm = x.shape
    ts = _tile(S, TS_ROW, SUB)

    def body(dh_ref, x_ref, dres_ref, g_ref, sc_ref, dx_ref, dg_ref, dsc_ref, dsh_ref):
        i = pl.program_id(0)
        xv = x_ref[...]
        dhv = dh_ref[...]
        rstd = lax.rsqrt(jnp.mean(xv * xv, axis=-1, keepdims=True) + EPS)
        xhat = xv * rstd
        hn = xhat * g_ref[...]
        dhn = dhv * (1.0 + sc_ref[...])
        dxh = dhn * g_ref[...]
        dx_ref[...] = dres_ref[...] + rstd * (dxh - xhat * jnp.mean(dxh * xhat, axis=-1, keepdims=True))

        @pl.when(i == 0)
        def _():
            dg_ref[...] = jnp.zeros_like(dg_ref)
            dsc_ref[...] = jnp.zeros_like(dsc_ref)
            dsh_ref[...] = jnp.zeros_like(dsh_ref)

        dg_ref[...] += _colsum(dhn * xhat)
        dsc_ref[...] += _colsum(dhv * hn)
        dsh_ref[...] += _colsum(dhv)

    vshape = jax.ShapeDtypeStruct((1, dm), F32)
    return pl.pallas_call(
        body, grid=(S // ts,), in_specs=[_row(ts, dm)] * 3 + [_vec(dm)] * 2,
        out_specs=(_row(ts, dm), _vec(dm), _vec(dm), _vec(dm)),
        out_shape=(jax.ShapeDtypeStruct((S, dm), F32), vshape, vshape, vshape),
        compiler_params=_cp(1), name=name)(dh, x, dres, g, sc)


def _gate_bwd(dxo, f, gate, *, name):
    S, dm = f.shape
    ts = _tile(S, TS_ROW, SUB)

    def body(dxo_ref, f_ref, gate_ref, df_ref, dgate_ref):
        i = pl.program_id(0)
        dv = dxo_ref[...]
        df_ref[...] = (gate_ref[...] * dv).astype(_ACT)

        @pl.when(i == 0)
        def _():
            dgate_ref[...] = jnp.zeros_like(dgate_ref)

        dgate_ref[...] += _colsum(dv * f_ref[...])

    return pl.pallas_call(
        body, grid=(S // ts,), in_specs=[_row(ts, dm), _row(ts, dm), _vec(dm)],
        out_specs=(_row(ts, dm), _vec(dm)),
        out_shape=(jax.ShapeDtypeStruct((S, dm), _ACT), jax.ShapeDtypeStruct((1, dm), F32)),
        compiler_params=_cp(1), name=name)(dxo, f, gate)


def _final_loss(x, f, gate, g, tgt, *, name):
    S, dm = x.shape
    ts = _tile(S, TS_ROW, SUB)

    def body(x_ref, f_ref, gate_ref, g_ref, t_ref, loss_ref, dx_ref, dg_ref):
        i = pl.program_id(0)
        xv = x_ref[...] + gate_ref[...] * f_ref[...]
        rstd = lax.rsqrt(jnp.mean(xv * xv, axis=-1, keepdims=True) + EPS)
        xhat = xv * rstd
        err = xhat * g_ref[...] - t_ref[...]
        dy = err * (1.0 / dm)
        dxh = dy * g_ref[...]
        dx_ref[...] = rstd * (dxh - xhat * jnp.mean(dxh * xhat, axis=-1, keepdims=True))

        @pl.when(i == 0)
        def _():
            loss_ref[...] = jnp.zeros_like(loss_ref)
            dg_ref[...] = jnp.zeros_like(dg_ref)

        loss_ref[...] += jnp.full((1, LANE), 0.5 * jnp.sum(jnp.mean(err * err, axis=-1, keepdims=True)), F32)
        dg_ref[...] += _colsum(dy * xhat)

    return pl.pallas_call(
        body, grid=(S // ts,), in_specs=[_row(ts, dm), _row(ts, dm), _vec(dm), _vec(dm), _row(ts, dm)],
        out_specs=(_vec(LANE), _row(ts, dm), _vec(dm)),
        out_shape=(jax.ShapeDtypeStruct((1, LANE), F32), jax.ShapeDtypeStruct((S, dm), F32),
                   jax.ShapeDtypeStruct((1, dm), F32)),
        compiler_params=_cp(1), name=name)(x, f, gate, g, tgt)


def _ffn_conv(t, halo, cw_ref, cb_ref):
    t1, t2 = _shift_down(t, halo, 1), _shift_down(t, halo, 2)
    return ((cb_ref[...] + t2 * cw_ref[0:1, :]) + t1 * cw_ref[1:2, :]) + t * cw_ref[2:3, :], t1, t2


def _prev_halo_spec(ts, w, col=0):
    return pl.BlockSpec((SUB, w), lambda i: (jnp.maximum(i * (ts // SUB) - 1, 0), col))


def _ffn_act_fwd(up, cw, cb, *, name):
    S, w2 = up.shape
    ff = w2 // 2
    ts = _tile(S, TS_FFN, SUB)

    def body(up_ref, halo_ref, cw_ref, cb_ref, act_ref):
        i = pl.program_id(0)
        t = up_ref[...]
        halo = jnp.where(i > 0, halo_ref[...], 0.0)
        u, _, _ = _ffn_conv(t, halo, cw_ref, cb_ref)
        act_ref[...] = (_silu(u[:, :ff]) * u[:, ff:]).astype(_ACT)

    return pl.pallas_call(
        body, grid=(S // ts,), in_specs=[_row(ts, w2), _prev_halo_spec(ts, w2), _vec(w2, FFN_CONV), _vec(w2)],
        out_specs=_row(ts, ff), out_shape=jax.ShapeDtypeStruct((S, ff), _ACT),
        compiler_params=_cp(1), name=name)(up, up, cw, cb)


def _ffn_bwd(up, dact, cw, cb, *, name):
    S, w2 = up.shape
    ff = w2 // 2
    ts = _tile(S, TS_FFN, SUB)
    n = S // ts

    def body(up_ref, halo_ref, dact_ref, cw_ref, cb_ref, dup_ref, dcw_ref, dcb_ref, carry_ref):
        i = pl.program_id(0)
        t_idx = n - 1 - i

        @pl.when(i == 0)
        def _():
            carry_ref[...] = jnp.zeros_like(carry_ref)
            dcw_ref[...] = jnp.zeros_like(dcw_ref)
            dcb_ref[...] = jnp.zeros_like(dcb_ref)

        t = up_ref[...]
        halo = jnp.where(t_idx > 0, halo_ref[...], 0.0)
        u, t1, t2 = _ffn_conv(t, halo, cw_ref, cb_ref)
        a, b = u[:, :ff], u[:, ff:]
        da = dact_ref[...]
        sa, dsa = _silu_grad(a)
        dv = jnp.concatenate([da * b * dsa, da * sa], axis=1)
        nxt = carry_ref[...]
        dup = (dv * cw_ref[2:3, :] + _shift_up(dv, nxt, 1) * cw_ref[1:2, :]) + _shift_up(dv, nxt, 2) * cw_ref[0:1, :]
        dup_ref[...] = dup.astype(_ACT)
        dcb_ref[...] += _colsum(dv)
        dcw_ref[2:3, :] += _colsum(dv * t)
        dcw_ref[1:2, :] += _colsum(dv * t1)
        dcw_ref[0:1, :] += _colsum(dv * t2)
        carry_ref[...] = dv[:SUB]

    rev = lambda w: pl.BlockSpec((ts, w), lambda i: (n - 1 - i, 0))
    halo_spec = pl.BlockSpec((SUB, w2), lambda i: (jnp.maximum((n - 1 - i) * (ts // SUB) - 1, 0), 0))
    return pl.pallas_call(
        body, grid=(n,), in_specs=[rev(w2), halo_spec, rev(ff), _vec(w2, FFN_CONV), _vec(w2)],
        out_specs=(rev(w2), _vec(w2, FFN_CONV), _vec(w2)),
        out_shape=(jax.ShapeDtypeStruct((S, w2), _ACT), jax.ShapeDtypeStruct((FFN_CONV, w2), F32),
                   jax.ShapeDtypeStruct((1, w2), F32)),
        scratch_shapes=[pltpu.VMEM((SUB, w2), F32)], compiler_params=_cp(1), name=name)(up, up, dact, cw, cb)


def _ssd_core(pre, halo, misc, cw_ref, cb_ref, dtb, alog):
    q = pre.shape[0]
    conv = cb_ref[...]
    for k in range(SSD_CONV):
        conv = conv + _shift_down(pre, halo, SSD_CONV - 1 - k) * cw_ref[k:k + 1, :]
    xbc = _silu(conv)
    raw = misc + dtb
    dt = _softplus(raw)
    a = -jnp.exp(alog)
    r = lax.broadcasted_iota(jnp.int32, (q, q), 0)
    c = lax.broadcasted_iota(jnp.int32, (q, q), 1)
    tri = r >= c
    acum = jnp.dot(tri.astype(F32), dt * a, precision=_HI, preferred_element_type=F32)
    return conv, xbc, raw, dt, a, acum, acum.T, tri


def _sel(v, j, lo):
    return jnp.where(lo, v[:, 2 * j:2 * j + 1], v[:, 2 * j + 1:2 * j + 2])


def _ssd_pair_fwd(xbc, dt, acum, acum_t, tri, dsk, g_mat, b_mat, c_mat, h_pair, j, lo, lo1, sub_lo):
    q = xbc.shape[0]
    x = xbc[:, LANE * j:LANE * (j + 1)]
    dtp = _sel(dt, j, lo)
    ap = _sel(acum, j, lo)
    xd = x * dtp
    ls, ms = [], []
    for h in (2 * j, 2 * j + 1):
        seg = acum[:, h:h + 1] - acum_t[h:h + 1, :]
        l_mat = jnp.exp(jnp.where(tri, seg, -jnp.inf))
        ls.append(l_mat)
        ms.append(g_mat * l_mat)
    yd = jnp.where(lo, _dot(ms[0], xd), _dot(ms[1], xd))
    ea = jnp.exp(ap)
    yo = _dot_nt(c_mat, h_pair) * ea
    dp = _sel(dsk, j, lo1)
    alast = acum[q - 1:q, :]
    e = jnp.exp(_sel(alast, j, lo1) - ap)
    cd = jnp.where(sub_lo, jnp.exp(alast[:, 2 * j:2 * j + 1]), jnp.exp(alast[:, 2 * j + 1:2 * j + 2]))
    return dict(x=x, dtp=dtp, ap=ap, xd=xd, ls=ls, ms=ms, ea=ea, yo=yo, dp=dp, e=e, cd=cd, y=yd + yo + x * dp)


def _gnorm(yg):
    half = SSD_INNER // SSD_GROUPS
    rstds, yns = [], []
    for g in range(SSD_GROUPS):
        part = yg[:, half * g:half * (g + 1)]
        rstd = lax.rsqrt(jnp.mean(part * part, axis=-1, keepdims=True) + EPS)
        rstds.append(rstd)
        yns.append(part * rstd)
    return rstds, yns


def _ssd_specs(nc, rev):
    q = SSD_CHUNK
    cidx = (lambda i: nc - 1 - i) if rev else (lambda i: i)
    return [
        pl.BlockSpec((q, SSD_XBC), lambda i: (cidx(i), C_XBC // SSD_XBC)),
        pl.BlockSpec((SUB, SSD_XBC), lambda i: (jnp.maximum(cidx(i) * (q // SUB) - 1, 0), C_XBC // SSD_XBC)),
        pl.BlockSpec((q, SSD_INNER), lambda i: (cidx(i), C_Z // SSD_INNER)),
        pl.BlockSpec((q, LANE), lambda i: (cidx(i), C_MISC // LANE)),
    ]


def _ssd_param_specs():
    return [_vec(SSD_XBC, SSD_CONV), _vec(SSD_XBC), _vec(LANE), _vec(LANE), _vec(LANE), _vec(SSD_INNER)]


def _ssd_fwd(proj, cw, cb, dtb, alog, dsk, ng, *, name):
    S = proj.shape[0]
    q = SSD_CHUNK
    nc = S // q
    npair = SSD_HEADS // 2

    def body(xbc_ref, halo_ref, z_ref, misc_ref, cw_ref, cb_ref, dtb_ref, alog_ref, dsk_ref, ng_ref,
             y_ref, hin_ref, h_ref):
        c = pl.program_id(0)

        @pl.when(c == 0)
        def _():
            h_ref[...] = jnp.zeros_like(h_ref)

        pre = xbc_ref[...]
        halo = jnp.where(c > 0, halo_ref[...], 0.0)
        conv, xbc, raw, dt, a, acum, acum_t, tri = _ssd_core(pre, halo, misc_ref[...], cw_ref, cb_ref,
                                                             dtb_ref[...], alog_ref[...])
        lo = lax.broadcasted_iota(jnp.int32, (q, LANE), 1) < LANE // 2
        lo1 = lo[:1]
        sub_lo = lax.broadcasted_iota(jnp.int32, (LANE, LANE), 0) < LANE // 2
        ys = []
        for g in range(SSD_GROUPS):
            b_mat = xbc[:, SSD_INNER + SSD_STATE * g:SSD_INNER + SSD_STATE * (g + 1)]
            c_mat = xbc[:, SSD_INNER + SSD_STATE * (SSD_GROUPS + g):SSD_INNER + SSD_STATE * (SSD_GROUPS + g + 1)]
            g_mat = _dot_nt(c_mat, b_mat)
            for jj in range(npair // SSD_GROUPS):
                j = g * (npair // SSD_GROUPS) + jj
                hj = h_ref[j]
                p = _ssd_pair_fwd(xbc, dt, acum, acum_t, tri, dsk_ref[...], g_mat, b_mat, c_mat, hj, j, lo, lo1, sub_lo)
                ys.append(p["y"])
                hin_ref[0, j] = hj
                h_ref[j] = p["cd"] * hj + _dot_tn(p["xd"] * p["e"], b_mat)
        yg = jnp.concatenate(ys, axis=1) * _silu(z_ref[...])
        _, yns = _gnorm(yg)
        y_ref[...] = jnp.concatenate(yns, axis=1) * ng_ref[...]

    return pl.pallas_call(
        body, grid=(nc,), in_specs=_ssd_specs(nc, False) + _ssd_param_specs(),
        out_specs=(pl.BlockSpec((q, SSD_INNER), lambda i: (i, 0)),
                   pl.BlockSpec((1, npair, LANE, LANE), lambda i: (i, 0, 0, 0))),
        out_shape=(jax.ShapeDtypeStruct((S, SSD_INNER), F32), jax.ShapeDtypeStruct((nc, npair, LANE, LANE), F32)),
        scratch_shapes=[pltpu.VMEM((npair, LANE, LANE), F32)], compiler_params=_cp(1), name=name,
    )(proj, proj, proj, proj, cw, cb, dtb, alog, dsk, ng)


def _ssd_bwd(proj, dycat, hin, cw, cb, dtb, alog, dsk, ng, *, name):
    S = proj.shape[0]
    q = SSD_CHUNK
    nc = S // q
    npair = SSD_HEADS // 2
    ppg = npair // SSD_GROUPS

    def body(xbc_ref, halo_ref, z_ref, misc_ref, dy_ref, hin_ref, cw_ref, cb_ref, dtb_ref, alog_ref, dsk_ref, ng_ref,
             dpre_ref, dz_ref, dmisc_ref, dcw_ref, dcb_ref, ddtb_ref, dalog_ref, ddsk_ref, dng_ref,
             dh_ref, carry_ref):
        i = pl.program_id(0)
        c = nc - 1 - i

        @pl.when(i == 0)
        def _():
            dh_ref[...] = jnp.zeros_like(dh_ref)
            carry_ref[...] = jnp.zeros_like(carry_ref)
            for r in (dcw_ref, dcb_ref, ddtb_ref, dalog_ref, ddsk_ref, dng_ref):
                r[...] = jnp.zeros_like(r)

        pre = xbc_ref[...]
        halo = jnp.where(c > 0, halo_ref[...], 0.0)
        conv, xbc, raw, dt, a, acum, acum_t, tri = _ssd_core(pre, halo, misc_ref[...], cw_ref, cb_ref,
                                                             dtb_ref[...], alog_ref[...])
        lane = lax.broadcasted_iota(jnp.int32, (q, LANE), 1)
        lane1 = lane[:1]
        rowi = lax.broadcasted_iota(jnp.int32, (q, LANE), 0)
        lastrow = rowi == q - 1
        lo = lane < LANE // 2
        lo1 = lo[:1]
        sub_lo = lax.broadcasted_iota(jnp.int32, (LANE, LANE), 0) < LANE // 2
        dsk = dsk_ref[...]
        alast = acum[q - 1:q, :]

        def halves(t):
            return _rowsum(jnp.where(lo, t, 0.0)), _rowsum(jnp.where(lo, 0.0, t))

        def put(ha, va, vb):
            ln = lane if va.shape[0] == q else lane1
            return jnp.where(ln == ha, va, 0.0) + jnp.where(ln == ha + 1, vb, 0.0)

        mats, pairs = [], []
        for g in range(SSD_GROUPS):
            b_mat = xbc[:, SSD_INNER + SSD_STATE * g:SSD_INNER + SSD_STATE * (g + 1)]
            c_mat = xbc[:, SSD_INNER + SSD_STATE * (SSD_GROUPS + g):SSD_INNER + SSD_STATE * (SSD_GROUPS + g + 1)]
            g_mat = _dot_nt(c_mat, b_mat)
            mats.append((b_mat, c_mat, g_mat))
            for jj in range(ppg):
                j = g * ppg + jj
                pairs.append(_ssd_pair_fwd(xbc, dt, acum, acum_t, tri, dsk, g_mat, b_mat, c_mat, hin_ref[0, j],
                                           j, lo, lo1, sub_lo))
        z = z_ref[...]
        sz, dsz = _silu_grad(z)
        y = jnp.concatenate([p["y"] for p in pairs], axis=1)
        rstds, yns = _gnorm(y * sz)
        dout = dy_ref[...]
        dng_ref[...] += _colsum(dout * jnp.concatenate(yns, axis=1))
        dyn = dout * ng_ref[...]
        half = SSD_INNER // SSD_GROUPS
        dygs = []
        for g in range(SSD_GROUPS):
            dyn_g = dyn[:, half * g:half * (g + 1)]
            dygs.append(rstds[g] * (dyn_g - yns[g] * jnp.mean(dyn_g * yns[g], axis=-1, keepdims=True)))
        dyg = jnp.concatenate(dygs, axis=1)
        dyv = dyg * sz
        dz_ref[...] = (dyg * y * dsz).astype(_ACT)

        da_acc = jnp.zeros((q, LANE), F32)
        ddt = jnp.zeros((q, LANE), F32)
        dds = jnp.zeros((1, LANE), F32)
        dxs, dbs, dcs = [], [], []
        for g in range(SSD_GROUPS):
            b_mat, c_mat, g_mat = mats[g]
            dg_mat = jnp.zeros((q, q), F32)
            db = jnp.zeros((q, SSD_STATE), F32)
            dc = jnp.zeros((q, SSD_STATE), F32)
            for jj in range(ppg):
                j = g * ppg + jj
                ha = 2 * j
                p = pairs[j]
                hj = hin_ref[0, j]
                dyp = dyv[:, LANE * j:LANE * (j + 1)]
                dsum = _colsum(dyp * p["x"])
                dds = dds + put(ha, _rowsum(jnp.where(lo1, dsum, 0.0)), _rowsum(jnp.where(lo1, 0.0, dsum)))
                dx = dyp * p["dp"]
                dw = dyp * p["ea"]
                dc = dc + _dot(dw, hj)
                dh_yo = _dot_tn(dw, c_mat)
                ra, rb = halves(dyp * p["yo"])
                da_acc = da_acc + put(ha, ra, rb)
                dxd = jnp.zeros((q, LANE), F32)
                for idx in range(2):
                    dyh = jnp.where(lo, dyp, 0.0) if idx == 0 else jnp.where(lo, 0.0, dyp)
                    dm = _dot_nt(dyh, p["xd"])
                    dxd = dxd + _dot_tn(p["ms"][idx], dyh)
                    dg_mat = dg_mat + dm * p["ls"][idx]
                    t = dm * p["ms"][idx]
                    da_h = _rowsum(t) - _rowsum(t.T)
                    da_acc = da_acc + jnp.where(lane == ha + idx, da_h, 0.0)
                dhn = dh_ref[j]
                s = _rowsum(dhn * hj)
                sa = jnp.sum(jnp.where(sub_lo[:, :1], s, 0.0), keepdims=True)
                sb = jnp.sum(jnp.where(sub_lo[:, :1], 0.0, s), keepdims=True)
                cda, cdb = jnp.exp(alast[:, ha:ha + 1]), jnp.exp(alast[:, ha + 1:ha + 2])
                db = db + _dot(p["xd"] * p["e"], dhn)
                r = _dot_nt(b_mat, dhn)
                dxd = dxd + r * p["e"]
                qa, qb = halves(r * p["xd"] * p["e"])
                da_acc = da_acc - put(ha, qa, qb)
                tot_a = sa * cda + jnp.sum(qa, keepdims=True)
                tot_b = sb * cdb + jnp.sum(qb, keepdims=True)
                da_acc = da_acc + jnp.where(lastrow, put(ha, tot_a, tot_b), 0.0)
                dh_ref[j] = p["cd"] * dhn + dh_yo
                dx = dx + dxd * p["dtp"]
                ua, ub = halves(dxd * p["x"])
                ddt = ddt + put(ha, ua, ub)
                dxs.append(dx)
            dc = dc + _dot(dg_mat, b_mat)
            db = db + _dot_tn(dg_mat, c_mat)
            dbs.append(db)
            dcs.append(dc)
        r2 = lax.broadcasted_iota(jnp.int32, (q, q), 0)
        c2 = lax.broadcasted_iota(jnp.int32, (q, q), 1)
        dda = jnp.dot((c2 >= r2).astype(F32), da_acc, precision=_HI, preferred_element_type=F32)
        ddt = ddt + dda * a
        dalog_ref[...] += _colsum(dda * dt) * a
        ddsk_ref[...] += dds
        draw = jnp.where(lane < SSD_HEADS, ddt * _sigmoid(raw), 0.0)
        ddtb_ref[...] += _colsum(draw)
        dmisc_ref[...] = draw
        dconv = jnp.concatenate(dxs + dbs + dcs, axis=1) * _dsilu(conv)
        dcb_ref[...] += _colsum(dconv)
        nxt = carry_ref[...]
        dpre = jnp.zeros_like(dconv)
        for k in range(SSD_CONV):
            dcw_ref[k:k + 1, :] += _colsum(dconv * _shift_down(pre, halo, SSD_CONV - 1 - k))
            dpre = dpre + _shift_up(dconv, nxt, SSD_CONV - 1 - k) * cw_ref[k:k + 1, :]
        dpre_ref[...] = dpre.astype(_ACT)
        carry_ref[...] = dconv[:SUB]

    rev = lambda i: (nc - 1 - i, 0)
    vshape = lambda w, r=1: jax.ShapeDtypeStruct((r, w), F32)
    return pl.pallas_call(
        body, grid=(nc,),
        in_specs=_ssd_specs(nc, True) + [pl.BlockSpec((q, SSD_INNER), rev),
                                         pl.BlockSpec((1, npair, LANE, LANE), lambda i: (nc - 1 - i, 0, 0, 0))]
        + _ssd_param_specs(),
        out_specs=(pl.BlockSpec((q, SSD_XBC), rev), pl.BlockSpec((q, SSD_INNER), rev), pl.BlockSpec((q, LANE), rev),
                   _vec(SSD_XBC, SSD_CONV), _vec(SSD_XBC), _vec(LANE), _vec(LANE), _vec(LANE), _vec(SSD_INNER)),
        out_shape=(jax.ShapeDtypeStruct((S, SSD_XBC), _ACT), jax.ShapeDtypeStruct((S, SSD_INNER), _ACT),
                   jax.ShapeDtypeStruct((S, LANE), F32),
                   vshape(SSD_XBC, SSD_CONV), vshape(SSD_XBC), vshape(LANE), vshape(LANE), vshape(LANE),
                   vshape(SSD_INNER)),
        scratch_shapes=[pltpu.VMEM((npair, LANE, LANE), F32), pltpu.VMEM((SUB, SSD_XBC), F32)],
        compiler_params=_cp(1), name=name,
    )(proj, proj, proj, proj, dycat, hin, cw, cb, dtb, alog, dsk, ng)


def _rope(x, cosf, sina, sinb):
    return x * cosf + pltpu.roll(x, LANE - MLA_ROPE // 2, 1) * sina + pltpu.roll(x, MLA_ROPE // 2, 1) * sinb


def _rope_t(dy, cosf, sina, sinb):
    return dy * cosf + pltpu.roll(dy * sina, MLA_ROPE // 2, 1) + pltpu.roll(dy * sinb, LANE - MLA_ROPE // 2, 1)


def _rope_lanes(shape):
    lane = lax.broadcasted_iota(jnp.int32, shape, 1)
    return (lane >= ROPE_LANE) & (lane < ROPE_LANE + MLA_ROPE)


def _mla_prep_fwd(proj, cosf, sina, sinb, gq, gkv, wuq, wukv, *, name):
    S = proj.shape[0]
    ts = _tile(S, TS_ROW, SUB)
    hw = MLA_HEADS * LANE

    def body(cq_ref, ckv_ref, misc_ref, cos_ref, sa_ref, sb_ref, gq_ref, gkv_ref, wuq_ref, wukv_ref,
             q_ref, k_ref, v_ref, vt_ref):
        cosv, sav, sbv = cos_ref[...], sa_ref[...], sb_ref[...]
        cq = cq_ref[...]
        qn = cq * lax.rsqrt(jnp.mean(cq * cq, axis=-1, keepdims=True) + EPS) * gq_ref[...]
        qh = _dot(qn, wuq_ref[...])
        ckv = ckv_ref[...]
        kvn = ckv * lax.rsqrt(jnp.mean(ckv * ckv, axis=-1, keepdims=True) + EPS) * gkv_ref[...]
        kv = _dot(kvn, wukv_ref[...])
        kr = _rope(jnp.where(_rope_lanes((ts, LANE)), misc_ref[...], 0.0), cosv, sav, sbv)
        for h in range(MLA_HEADS):
            sl = slice(LANE * h, LANE * (h + 1))
            q_ref[:, sl] = (_rope(qh[:, sl], cosv, sav, sbv) * _Q_SCALE).astype(_ACT)
            k_ref[:, sl] = (kv[:, sl] + kr).astype(_ACT)
        v_ref[...] = kv[:, hw:].astype(_ACT)
        vt_ref[...] = kv[:, hw:].T.astype(_ACT)

    oshape = jax.ShapeDtypeStruct((S, hw), _ACT)
    return pl.pallas_call(
        body, grid=(S // ts,),
        in_specs=[_row(ts, MLA_QR, C_CQ // MLA_QR), _row(ts, MLA_KVR, C_CKV // MLA_KVR), _row(ts, LANE, C_MISC // LANE),
                  _row(ts, LANE), _row(ts, LANE), _row(ts, LANE), _vec(MLA_QR), _vec(MLA_KVR),
                  _vec(hw, MLA_QR), _vec(2 * hw, MLA_KVR)],
        out_specs=(_row(ts, hw),) * 3 + (pl.BlockSpec((hw, ts), lambda i: (0, i)),),
        out_shape=(oshape,) * 3 + (jax.ShapeDtypeStruct((hw, S), _ACT),), compiler_params=_cp(1), name=name,
    )(proj, proj, proj, cosf, sina, sinb, gq, gkv, wuq, wukv)


def _mla_prep_bwd(proj, dq, dk, dv, dmisc_ssd, cosf, sina, sinb, gq, gkv, wuq, wukv, *, name):
    S = proj.shape[0]
    ts = _tile(S, TS_ROW, SUB)
    hw = MLA_HEADS * LANE

    def body(cq_ref, ckv_ref, dq_ref, dk_ref, dv_ref, dms_ref, cos_ref, sa_ref, sb_ref, gq_ref, gkv_ref,
             wuq_ref, wukv_ref, dcq_ref, dckv_ref, dmisc_ref, dqh_ref, dkv_ref, qn_ref, kvn_ref, dgq_ref, dgkv_ref):
        i = pl.program_id(0)
        cosv, sav, sbv = cos_ref[...], sa_ref[...], sb_ref[...]

        @pl.when(i == 0)
        def _():
            dgq_ref[...] = jnp.zeros_like(dgq_ref)
            dgkv_ref[...] = jnp.zeros_like(dgkv_ref)

        dqh = jnp.concatenate([_rope_t(dq_ref[:, LANE * h:LANE * (h + 1)], cosv, sav, sbv)
                               for h in range(MLA_HEADS)], axis=1)
        dqh_ref[...] = dqh.astype(_ACT)
        dkv = jnp.concatenate([dk_ref[...], dv_ref[...]], axis=1)
        dkv_ref[...] = dkv.astype(_ACT)

        def norm_bwd(x, g, dn, dg_ref, n_ref):
            rstd = lax.rsqrt(jnp.mean(x * x, axis=-1, keepdims=True) + EPS)
            xhat = x * rstd
            n_ref[...] = (xhat * g).astype(_ACT)
            dg_ref[...] += _colsum(dn * xhat)
            dxh = dn * g
            return rstd * (dxh - xhat * jnp.mean(dxh * xhat, axis=-1, keepdims=True))

        dcq_ref[...] = norm_bwd(cq_ref[...], gq_ref[...], _dot_nt(dqh, wuq_ref[...]), dgq_ref, qn_ref).astype(_ACT)
        dckv_ref[...] = norm_bwd(ckv_ref[...], gkv_ref[...], _dot_nt(dkv, wukv_ref[...]), dgkv_ref, kvn_ref).astype(_ACT)
        dks = dk_ref[:, 0:LANE]
        for h in range(1, MLA_HEADS):
            dks = dks + dk_ref[:, LANE * h:LANE * (h + 1)]
        rl = _rope_lanes((ts, LANE))
        dkr = _rope_t(jnp.where(rl, dks, 0.0), cosv, sav, sbv)
        dmisc_ref[...] = (dms_ref[...] + jnp.where(rl, dkr, 0.0)).astype(_ACT)

    act = lambda w: jax.ShapeDtypeStruct((S, w), _ACT)
    return pl.pallas_call(
        body, grid=(S // ts,),
        in_specs=[_row(ts, MLA_QR, C_CQ // MLA_QR), _row(ts, MLA_KVR, C_CKV // MLA_KVR),
                  _row(ts, hw), _row(ts, hw), _row(ts, hw), _row(ts, LANE),
                  _row(ts, LANE), _row(ts, LANE), _row(ts, LANE), _vec(MLA_QR), _vec(MLA_KVR),
                  _vec(hw, MLA_QR), _vec(2 * hw, MLA_KVR)],
        out_specs=(_row(ts, MLA_QR), _row(ts, MLA_KVR), _row(ts, LANE), _row(ts, hw), _row(ts, 2 * hw),
                   _row(ts, MLA_QR), _row(ts, MLA_KVR), _vec(MLA_QR), _vec(MLA_KVR)),
        out_shape=(act(MLA_QR), act(MLA_KVR), act(LANE), act(hw), act(2 * hw), act(MLA_QR), act(MLA_KVR),
                   jax.ShapeDtypeStruct((1, MLA_QR), F32), jax.ShapeDtypeStruct((1, MLA_KVR), F32)),
        compiler_params=_cp(1), name=name,
    )(proj, proj, dq, dk, dv, dmisc_ssd, cosf, sina, sinb, gq, gkv, wuq, wukv)


_MLA_SCALE = 1.0 / math.sqrt(MLA_NOPE + MLA_ROPE)
_LOG2E = 1.4426950408889634
_Q_SCALE = _MLA_SCALE * _LOG2E
ATT_CHUNK = 1024


def _tri_grid(nq, by_key):
    if by_key:
        pairs = [(i, j) for j in range(nq) for i in range(j, nq)]
    else:
        pairs = [(i, j) for i in range(nq) for j in range(i + 1)]
    return jnp.asarray([p[0] for p in pairs], jnp.int32), jnp.asarray([p[1] for p in pairs], jnp.int32)


def _attn_fwd(q, k, vt, *, name):
    S = q.shape[0]
    tq = _tile(S, TQ_ATT)
    nq = S // tq
    itab, jtab = _tri_grid(nq, False)

    def body(it_ref, jt_ref, q_ref, k_ref, vt_ref, o_ref, lse_ref, lset_ref, m_ref, l_ref, acc_ref):
        t = pl.program_id(1)
        i, j = it_ref[t], jt_ref[t]

        @pl.when(j == 0)
        def _():
            m_ref[...] = jnp.full_like(m_ref, -jnp.inf)
            l_ref[...] = jnp.zeros_like(l_ref)
            acc_ref[...] = jnp.zeros_like(acc_ref)

        def step(diagonal):
            s = _dot_nt(k_ref[...], q_ref[...])
            if diagonal:
                kk = lax.broadcasted_iota(jnp.int32, (tq, tq), 0)
                s = jnp.where(kk <= lax.broadcasted_iota(jnp.int32, (tq, tq), 1), s, -jnp.inf)
            m_prev = m_ref[...]
            m_new = jnp.maximum(m_prev, jnp.max(s, axis=0, keepdims=True))
            p = jnp.exp2(s - m_new)
            alpha = jnp.exp2(m_prev - m_new)
            l_ref[...] = alpha * l_ref[...] + _colsum(p)
            acc_ref[...] = alpha * acc_ref[...] + _dot(vt_ref[...], p)
            m_ref[...] = m_new

        pl.when(j < i)(functools.partial(step, False))
        pl.when(j == i)(functools.partial(step, True))

        @pl.when(j == i)
        def _():
            o_ref[...] = (acc_ref[...] / l_ref[...]).T
            lse = m_ref[...] + jnp.log2(l_ref[...])
            lset_ref[...] = jnp.broadcast_to(lse, (SUB, tq))
            lse_ref[...] = jnp.broadcast_to(lse, (LANE, tq)).T

    qspec = pl.BlockSpec((tq, LANE), lambda h, t, it, jt: (it[t], h))
    kspec = pl.BlockSpec((tq, LANE), lambda h, t, it, jt: (jt[t], h))
    vtspec = pl.BlockSpec((LANE, tq), lambda h, t, it, jt: (h, jt[t]))
    oshape = jax.ShapeDtypeStruct((S, MLA_HEADS * LANE), F32)
    return pl.pallas_call(
        body,
        grid_spec=pltpu.PrefetchScalarGridSpec(
            num_scalar_prefetch=2, grid=(MLA_HEADS, itab.shape[0]), in_specs=[qspec, kspec, vtspec],
            out_specs=(qspec, qspec, pl.BlockSpec((SUB, tq), lambda h, t, it, jt: (h, it[t]))),
            scratch_shapes=[pltpu.VMEM((1, tq), F32), pltpu.VMEM((1, tq), F32), pltpu.VMEM((LANE, tq), F32)]),
        out_shape=(oshape, oshape, jax.ShapeDtypeStruct((MLA_HEADS * SUB, S), F32)),
        compiler_params=_cp(2), name=name)(itab, jtab, q, k, vt)


def _attn_bwd_dq(q, k, v, o, lse, dycat, *, name):
    S = q.shape[0]
    tq = _tile(S, TQ_ATT)
    nq = S // tq
    rc = min(ATT_CHUNK, tq)
    itab, jtab = _tri_grid(nq, False)

    def body(it_ref, jt_ref, q_ref, k_ref, v_ref, o_ref, lse_ref, do_ref, dq_ref, acc_ref):
        t = pl.program_id(1)
        i, j = it_ref[t], jt_ref[t]

        @pl.when(j == 0)
        def _():
            acc_ref[...] = jnp.zeros_like(acc_ref)

        def step(diagonal):
            kv, vv = k_ref[...], v_ref[...]
            for r in range(tq // rc):
                rows = slice(r * rc, (r + 1) * rc)
                s = _dot_nt(q_ref[rows, :], kv)
                if diagonal:
                    rr = r * rc + lax.broadcasted_iota(jnp.int32, (rc, tq), 0)
                    s = jnp.where(lax.broadcasted_iota(jnp.int32, (rc, tq), 1) <= rr, s, -jnp.inf)
                p = jnp.exp2(s - lse_ref[rows, 0:1])
                dov = do_ref[rows, :]
                delta = _rowsum(dov * o_ref[rows, :])
                ds = p * (_dot_nt(dov, vv) - delta)
                acc_ref[rows, :] += _dot(ds, kv)

        pl.when(j < i)(functools.partial(step, False))
        pl.when(j == i)(functools.partial(step, True))

        @pl.when(j == i)
        def _():
            dq_ref[...] = acc_ref[...] * _MLA_SCALE

    qspec = pl.BlockSpec((tq, LANE), lambda h, t, it, jt: (it[t], h))
    kspec = pl.BlockSpec((tq, LANE), lambda h, t, it, jt: (jt[t], h))
    dospec = pl.BlockSpec((tq, LANE), lambda h, t, it, jt: (it[t], SSD_INNER // LANE + h))
    return pl.pallas_call(
        body,
        grid_spec=pltpu.PrefetchScalarGridSpec(
            num_scalar_prefetch=2, grid=(MLA_HEADS, itab.shape[0]),
            in_specs=[qspec, kspec, kspec, qspec, qspec, dospec], out_specs=qspec,
            scratch_shapes=[pltpu.VMEM((tq, LANE), F32)]),
        out_shape=jax.ShapeDtypeStruct((S, MLA_HEADS * LANE), F32),
        compiler_params=_cp(2), name=name)(itab, jtab, q, k, v, o, lse, dycat)


def _attn_bwd_dkv(q, k, v, o, lset, dycat, *, name):
    S = q.shape[0]
    tq = _tile(S, TQ_ATT)
    nq = S // tq
    kc = min(ATT_CHUNK, tq)
    itab, jtab = _tri_grid(nq, True)

    def body(it_ref, jt_ref, q_ref, k_ref, v_ref, o_ref, lset_ref, do_ref, dk_ref, dv_ref, dk_acc, dv_acc):
        t = pl.program_id(1)
        i, j = it_ref[t], jt_ref[t]

        @pl.when(i == j)
        def _():
            dk_acc[...] = jnp.zeros_like(dk_acc)
            dv_acc[...] = jnp.zeros_like(dv_acc)

        def step(diagonal):
            qv, dov = q_ref[...], do_ref[...]
            delta = lax.dot_general(jnp.ones((SUB, LANE), F32), dov * o_ref[...], (((1,), (1,)), ((), ())),
                                    precision=_HI, preferred_element_type=F32)[0:1]
            lse = lset_ref[0:1, :]
            for c in range(tq // kc):
                rows = slice(c * kc, (c + 1) * kc)
                s = _dot_nt(k_ref[rows, :], qv)
                if diagonal:
                    kk = c * kc + lax.broadcasted_iota(jnp.int32, (kc, tq), 0)
                    s = jnp.where(kk <= lax.broadcasted_iota(jnp.int32, (kc, tq), 1), s, -jnp.inf)
                p = jnp.exp2(s - lse)
                dv_acc[rows, :] += _dot(p, dov)
                ds = p * (_dot_nt(v_ref[rows, :], dov) - delta)
                dk_acc[rows, :] += _dot(ds, qv)

        pl.when(i > j)(functools.partial(step, False))
        pl.when(i == j)(functools.partial(step, True))

        @pl.when(i == nq - 1)
        def _():
            dk_ref[...] = dk_acc[...] * (1.0 / _LOG2E)
            dv_ref[...] = dv_acc[...]

    qspec = pl.BlockSpec((tq, LANE), lambda h, t, it, jt: (it[t], h))
    kspec = pl.BlockSpec((tq, LANE), lambda h, t, it, jt: (jt[t], h))
    dospec = pl.BlockSpec((tq, LANE), lambda h, t, it, jt: (it[t], SSD_INNER // LANE + h))
    lspec = pl.BlockSpec((SUB, tq), lambda h, t, it, jt: (h, it[t]))
    oshape = jax.ShapeDtypeStruct((S, MLA_HEADS * LANE), F32)
    return pl.pallas_call(
        body,
        grid_spec=pltpu.PrefetchScalarGridSpec(
            num_scalar_prefetch=2, grid=(MLA_HEADS, itab.shape[0]),
            in_specs=[qspec, kspec, kspec, qspec, lspec, dospec], out_specs=(kspec, kspec),
            scratch_shapes=[pltpu.VMEM((tq, LANE), F32), pltpu.VMEM((tq, LANE), F32)]),
        out_shape=(oshape, oshape), compiler_params=_cp(2), name=name)(itab, jtab, q, k, v, o, lset, dycat)


_SWA_SCALE = 1.0 / math.sqrt(SWA_HD)
_SWA_KW = SWA_KV * LANE


def _swa_specs(S, ts, rev):
    n = S // ts
    t = (lambda i: n - 1 - i) if rev else (lambda i: i)
    hb = lambda i: jnp.maximum(t(i) * (ts // WINDOW) - 1, 0)
    return [
        pl.BlockSpec((ts, SWA_HEADS * LANE), lambda i: (t(i), C_SQ // (SWA_HEADS * LANE))),
        pl.BlockSpec((ts, _SWA_KW), lambda i: (t(i), C_SK // _SWA_KW)),
        pl.BlockSpec((WINDOW, _SWA_KW), lambda i: (hb(i), C_SK // _SWA_KW)),
        pl.BlockSpec((ts, _SWA_KW), lambda i: (t(i), C_SV // _SWA_KW)),
        pl.BlockSpec((WINDOW, _SWA_KW), lambda i: (hb(i), C_SV // _SWA_KW)),
    ]


def _swa_scores(qh, kk, t, b, ts):
    s = _dot_nt(qh, kk) * _SWA_SCALE
    row = lax.broadcasted_iota(jnp.int32, (WINDOW, 2 * WINDOW), 0)
    col = lax.broadcasted_iota(jnp.int32, (WINDOW, 2 * WINDOW), 1)
    rel = WINDOW + row - col
    kpos = t * ts + (b - 1) * WINDOW + col
    return jnp.where((rel >= 0) & (rel < WINDOW) & (kpos >= 0), s, -jnp.inf)


def _swa_fwd(proj, sinks, *, name):
    S = proj.shape[0]
    ts = _tile(S, TS_SWA)
    nb = ts // WINDOW

    def body(q_ref, k_ref, kh_ref, v_ref, vh_ref, sink_ref, o_ref, lse_ref):
        t = pl.program_id(0)
        kext = jnp.concatenate([kh_ref[...], k_ref[...]], axis=0)
        vext = jnp.concatenate([vh_ref[...], v_ref[...]], axis=0)
        for b in range(nb):
            rows = slice(WINDOW * b, WINDOW * (b + 1))
            for h in range(SWA_HEADS):
                kvl = slice(LANE * (h // (SWA_HEADS // SWA_KV)), LANE * (h // (SWA_HEADS // SWA_KV) + 1))
                hl = slice(LANE * h, LANE * (h + 1))
                kk = kext[WINDOW * b:WINDOW * (b + 2), kvl]
                vv = vext[WINDOW * b:WINDOW * (b + 2), kvl]
                s = _swa_scores(q_ref[rows, hl], kk, t, b, ts)
                sk = sink_ref[:, h:h + 1]
                m = jnp.maximum(jnp.max(s, axis=1, keepdims=True), sk)
                p = jnp.exp(s - m)
                den = _rowsum(p) + jnp.exp(sk - m)
                o_ref[rows, hl] = _dot(p, vv) / den
                lse_ref[rows, hl] = jnp.broadcast_to(m + jnp.log(den), (WINDOW, LANE))

    oshape = jax.ShapeDtypeStruct((S, SWA_HEADS * LANE), F32)
    ospec = pl.BlockSpec((ts, SWA_HEADS * LANE), lambda i: (i, 0))
    return pl.pallas_call(
        body, grid=(S // ts,), in_specs=_swa_specs(S, ts, False) + [_vec(LANE)], out_specs=(ospec, ospec),
        out_shape=(oshape, oshape), compiler_params=_cp(1), name=name)(proj, proj, proj, proj, proj, sinks)


def _swa_bwd(proj, o, lse, dycat, sinks, *, name):
    S = proj.shape[0]
    ts = _tile(S, TS_SWA)
    nb = ts // WINDOW
    n = S // ts
    grp = SWA_HEADS // SWA_KV

    def body(q_ref, k_ref, kh_ref, v_ref, vh_ref, o_ref, lse_ref, do_ref, sink_ref,
             dq_ref, dk_ref, dv_ref, dsink_ref, dk_carry, dv_carry):
        i = pl.program_id(0)
        t = n - 1 - i

        @pl.when(i == 0)
        def _():
            dk_carry[...] = jnp.zeros_like(dk_carry)
            dv_carry[...] = jnp.zeros_like(dv_carry)
            dsink_ref[...] = jnp.zeros_like(dsink_ref)

        kext = jnp.concatenate([kh_ref[...], k_ref[...]], axis=0)
        vext = jnp.concatenate([vh_ref[...], v_ref[...]], axis=0)
        lane1 = lax.broadcasted_iota(jnp.int32, (1, LANE), 1)
        dkb = [[jnp.zeros((WINDOW, LANE), F32) for _ in range(SWA_KV)] for _ in range(nb + 1)]
        dvb = [[jnp.zeros((WINDOW, LANE), F32) for _ in range(SWA_KV)] for _ in range(nb + 1)]
        dsink = jnp.zeros((1, LANE), F32)
        for b in range(nb):
            rows = slice(WINDOW * b, WINDOW * (b + 1))
            for h in range(SWA_HEADS):
                kvh = h // grp
                kvl = slice(LANE * kvh, LANE * (kvh + 1))
                hl = slice(LANE * h, LANE * (h + 1))
                kk = kext[WINDOW * b:WINDOW * (b + 2), kvl]
                vv = vext[WINDOW * b:WINDOW * (b + 2), kvl]
                qh = q_ref[rows, hl]
                lse_h = lse_ref[rows, LANE * h:LANE * h + 1]
                p = jnp.exp(_swa_scores(qh, kk, t, b, ts) - lse_h)
                doh = do_ref[rows, hl]
                delta = _rowsum(doh * o_ref[rows, hl])
                ds = p * (_dot_nt(doh, vv) - delta)
                sk = sink_ref[:, h:h + 1]
                dsink = dsink + jnp.where(lane1 == h, -jnp.sum(jnp.exp(sk - lse_h) * delta, keepdims=True), 0.0)
                dq_ref[rows, hl] = (_dot(ds, kk) * _SWA_SCALE).astype(_ACT)
                dkk = _dot_tn(ds, qh) * _SWA_SCALE
                dvv = _dot_tn(p, doh)
                dkb[b][kvh] = dkb[b][kvh] + dkk[:WINDOW]
                dkb[b + 1][kvh] = dkb[b + 1][kvh] + dkk[WINDOW:]
                dvb[b][kvh] = dvb[b][kvh] + dvv[:WINDOW]
                dvb[b + 1][kvh] = dvb[b + 1][kvh] + dvv[WINDOW:]
        dsink_ref[...] += dsink
        for dref, blocks, carry in ((dk_ref, dkb, dk_carry), (dv_ref, dvb, dv_carry)):
            old = carry[...]
            for b in range(1, nb + 1):
                blk = jnp.concatenate(blocks[b], axis=1)
                if b == nb:
                    blk = blk + old
                dref[WINDOW * (b - 1):WINDOW * b, :] = blk.astype(_ACT)
            carry[...] = jnp.concatenate(blocks[0], axis=1)

    hw = SWA_HEADS * LANE
    rev = lambda i: (n - 1 - i, 0)
    mix = lambda i: (n - 1 - i, (SSD_INNER + MLA_HEADS * LANE) // hw)
    return pl.pallas_call(
        body, grid=(n,),
        in_specs=_swa_specs(S, ts, True) + [pl.BlockSpec((ts, hw), rev), pl.BlockSpec((ts, hw), rev),
                                            pl.BlockSpec((ts, hw), mix), _vec(LANE)],
        out_specs=(pl.BlockSpec((ts, hw), rev), pl.BlockSpec((ts, _SWA_KW), rev), pl.BlockSpec((ts, _SWA_KW), rev),
                   _vec(LANE)),
        out_shape=(jax.ShapeDtypeStruct((S, hw), _ACT), jax.ShapeDtypeStruct((S, _SWA_KW), _ACT),
                   jax.ShapeDtypeStruct((S, _SWA_KW), _ACT), jax.ShapeDtypeStruct((1, LANE), F32)),
        scratch_shapes=[pltpu.VMEM((WINDOW, _SWA_KW), F32), pltpu.VMEM((WINDOW, _SWA_KW), F32)],
        compiler_params=_cp(1), name=name)(proj, proj, proj, proj, proj, o, lse, dycat, sinks)


def _exchange(arrays, *, scatter, name):
    n = len(arrays)

    def body(*refs):
        ins, outs = refs[:n], refs[n:2 * n]
        send_sems, recv_sems, loc_sems = refs[2 * n:]
        x, y, c = lax.axis_index("x"), lax.axis_index("y"), lax.axis_index("c")
        me = 4 * x + 2 * y + c

        def src(i, dest):
            return ins[i].at[dest] if scatter else ins[i]

        local = [pltpu.make_async_copy(src(i, me), outs[i].at[me], loc_sems.at[i]) for i in range(n)]
        for cp in local:
            cp.start()
        sends, recvs = [], []
        for k in range(1, NDEV):
            px = 1 - x if k & 4 else x
            py = 1 - y if k & 2 else y
            pc = 1 - c if k & 1 else c
            peer = 4 * px + 2 * py + pc
            for i in range(n):
                common = dict(send_sem=send_sems.at[i, k - 1], recv_sem=recv_sems.at[i, k - 1],
                              device_id=(px, py, pc), device_id_type=pl.DeviceIdType.MESH)
                sends.append(pltpu.make_async_remote_copy(src_ref=src(i, peer), dst_ref=outs[i].at[me], **common))
                recvs.append(pltpu.make_async_remote_copy(src_ref=src(i, peer), dst_ref=outs[i].at[peer], **common))
        for cp in sends:
            cp.start()
        for cp in recvs:
            cp.wait_recv()
        for cp in sends:
            cp.wait_send()
        for cp in local:
            cp.wait()

    hbm = pl.BlockSpec(memory_space=pl.ANY)
    out_shape = tuple(jax.ShapeDtypeStruct(a.shape if scatter else (NDEV,) + a.shape, a.dtype) for a in arrays)
    return pl.pallas_call(
        body, in_specs=[hbm] * n, out_specs=tuple([hbm] * n), out_shape=out_shape,
        scratch_shapes=[pltpu.SemaphoreType.DMA((n, NDEV - 1)), pltpu.SemaphoreType.DMA((n, NDEV - 1)),
                        pltpu.SemaphoreType.DMA((n,))],
        name=name)(*arrays)


def _adamw(w, m, v, parts, *, name):
    R, C = w.shape
    npart = parts.shape[0]
    cap = max(SUB, ((1 << 18) // C) // SUB * SUB)
    tr = _tile(R, cap, SUB)

    def body(w_ref, m_ref, v_ref, p_ref, g_ref, d_ref, mo_ref, vo_ref):
        g = p_ref[0]
        for k in range(1, npart):
            g = g + p_ref[k]
        mn = ADAM_B1 * m_ref[...] + (1.0 - ADAM_B1) * g
        vn = ADAM_B2 * v_ref[...] + (1.0 - ADAM_B2) * (g * g)
        m_hat = mn / (1.0 - ADAM_B1 ** ADAM_STEP)
        v_hat = vn / (1.0 - ADAM_B2 ** ADAM_STEP)
        g_ref[...] = g
        d_ref[...] = -ADAM_LR * (m_hat / (jnp.sqrt(v_hat) + ADAM_EPS) + ADAM_WD * w_ref[...])
        mo_ref[...] = mn
        vo_ref[...] = vn

    spec = pl.BlockSpec((tr, C), lambda i: (i, 0))
    oshape = jax.ShapeDtypeStruct((R, C), F32)
    return pl.pallas_call(
        body, grid=(R // tr,), in_specs=[spec] * 3 + [pl.BlockSpec((npart, tr, C), lambda i: (0, i, 0))],
        out_specs=(spec,) * 4, out_shape=(oshape,) * 4, compiler_params=_cp(1), name=name)(w, m, v, parts)


def _adamw_layer(l, w, m, v, landed, mine, me, prev, *, name):
    L, R, C = w.shape
    npart = landed.shape[0]
    cap = max(SUB, ((1 << 18) // C) // SUB * SUB)
    tr = _tile(R, cap, SUB)
    nprev = 0 if prev is None else 4

    def body(me_ref, *refs):
        w_ref, m_ref, v_ref, p_ref, own_ref = refs[:5]
        g_ref, d_ref, mo_ref, vo_ref = refs[5 + nprev:]
        own = own_ref[...]
        g = jnp.where(me_ref[0] == 0, own, p_ref[0])
        for k in range(1, npart):
            g = g + jnp.where(me_ref[0] == k, own, p_ref[k])
        mn = ADAM_B1 * m_ref[...] + (1.0 - ADAM_B1) * g
        vn = ADAM_B2 * v_ref[...] + (1.0 - ADAM_B2) * (g * g)
        m_hat = mn / (1.0 - ADAM_B1 ** ADAM_STEP)
        v_hat = vn / (1.0 - ADAM_B2 ** ADAM_STEP)
        g_ref[...] = g
        d_ref[...] = -ADAM_LR * (m_hat / (jnp.sqrt(v_hat) + ADAM_EPS) + ADAM_WD * w_ref[...])
        mo_ref[...] = mn
        vo_ref[...] = vn

    spec = pl.BlockSpec((None, tr, C), lambda i, me_ref: (l, i, 0))
    oshape = jax.ShapeDtypeStruct((L, R, C), F32)
    return pl.pallas_call(
        body,
        grid_spec=pltpu.PrefetchScalarGridSpec(
            num_scalar_prefetch=1, grid=(R // tr,),
            in_specs=[spec] * 3 + [pl.BlockSpec((npart, tr, C), lambda i, me_ref: (0, i, 0)),
                                   pl.BlockSpec((None, tr, C), lambda i, me_ref: (me_ref[0], i, 0))]
            + [pl.BlockSpec(memory_space=pl.ANY)] * nprev,
            out_specs=(spec,) * 4),
        out_shape=(oshape,) * 4, input_output_aliases={6 + k: k for k in range(nprev)},
        compiler_params=_cp(1), name=name)(me, w, m, v, landed, mine, *(prev or ()))


_HBM = pl.BlockSpec(memory_space=pltpu.HBM)
_SEM = pl.BlockSpec(memory_space=pltpu.SEMAPHORE)
_EFFECT = pltpu.SideEffectType.DATAFLOW_SIDE_EFFECTING


def _peers():
    x, y, c = lax.axis_index("x"), lax.axis_index("y"), lax.axis_index("c")
    out = []
    for k in range(1, NDEV):
        px = 1 - x if k & 4 else x
        py = 1 - y if k & 2 else y
        pc = 1 - c if k & 1 else c
        out.append((k - 1, (px, py, pc), 4 * px + 2 * py + pc))
    return 4 * x + 2 * y + c, out


def _xchg_start(arrays, *, scatter, name):
    n = len(arrays)
    lands = [lax.empty(a.shape if scatter else (NDEV,) + a.shape, a.dtype) for a in arrays]

    def body(*refs):
        ins, lnd = refs[:n], refs[n:2 * n]
        send_sems, recv_sems = refs[2 * n], refs[2 * n + 1]
        token = refs[-1]
        me, peers = _peers()
        for k, dev, peer in peers:
            for i in range(n):
                pltpu.make_async_remote_copy(
                    src_ref=ins[i].at[peer] if scatter else ins[i], dst_ref=lnd[i].at[me],
                    send_sem=send_sems.at[i * (NDEV - 1) + k], recv_sem=recv_sems.at[i * (NDEV - 1) + k],
                    device_id=dev, device_id_type=pl.DeviceIdType.MESH).start()
        token[...] = jnp.zeros_like(token)

    sems = pltpu.SemaphoreType.DMA((n * (NDEV - 1),))
    res = pl.pallas_call(
        body, name=name,
        out_shape=(sems, sems) + tuple(pltpu.HBM(t.shape, t.dtype) for t in list(arrays) + lands)
        + (jax.ShapeDtypeStruct((SUB, LANE), F32),),
        in_specs=[_HBM] * (2 * n), out_specs=(_SEM, _SEM) + (_HBM,) * (2 * n) + (pl.BlockSpec(memory_space=pltpu.VMEM),),
        input_output_aliases={i: 2 + i for i in range(2 * n)},
        compiler_params=pltpu.CompilerParams(has_side_effects=_EFFECT),
    )(*[pltpu.with_memory_space_constraint(t, pltpu.HBM) for t in list(arrays) + lands])
    return dict(send=res[0], recv=res[1], thru=list(res[2:2 + 2 * n]), token=res[-1], scatter=scatter, n=n)


def _xchg_wait(handle, after, *, name):
    n, scatter = handle["n"], handle["scatter"]
    thru = handle["thru"]

    def body(*refs):
        ins, lnd = refs[:n], refs[n:2 * n]
        send_sems, recv_sems = refs[2 * n], refs[2 * n + 1]
        me, peers = _peers()
        for k, dev, peer in peers:
            for i in range(n):
                cp = pltpu.make_async_remote_copy(
                    src_ref=ins[i].at[peer] if scatter else ins[i], dst_ref=lnd[i].at[peer],
                    send_sem=send_sems.at[i * (NDEV - 1) + k], recv_sem=recv_sems.at[i * (NDEV - 1) + k],
                    device_id=dev, device_id_type=pl.DeviceIdType.MESH)
                cp.wait_send()
                cp.wait_recv()

    res = pl.pallas_call(
        body, name=name, out_shape=tuple(pltpu.HBM(t.shape, t.dtype) for t in thru),
        in_specs=[_HBM] * (2 * n) + [_SEM, _SEM, pl.BlockSpec(memory_space=pl.ANY)], out_specs=(_HBM,) * (2 * n),
        input_output_aliases={i: i for i in range(2 * n)},
        compiler_params=pltpu.CompilerParams(has_side_effects=_EFFECT),
    )(*thru, handle["send"], handle["recv"], after)
    return list(res[:n]), list(res[n:])


def _pad_heads(w, nh, hd, axis=-1):
    axis = axis % w.ndim
    shp = w.shape
    w = w.reshape(shp[:axis] + (nh, hd) + shp[axis + 1:])
    pads = [(0, 0)] * w.ndim
    pads[axis + 1] = (0, LANE - hd)
    return jnp.pad(w, pads).reshape(shp[:axis] + (nh * LANE,) + shp[axis + 1:])


def _unpad_heads(w, nh, hd, axis=-1):
    axis = axis % w.ndim
    shp = w.shape
    w = w.reshape(shp[:axis] + (nh, LANE) + shp[axis + 1:])
    w = lax.slice_in_dim(w, 0, hd, axis=axis + 1)
    return w.reshape(shp[:axis] + (nh * hd,) + shp[axis + 1:])


_O_DT = SSD_INNER + SSD_XBC
_O_CQ = _O_DT + SSD_HEADS
_O_CKV = _O_CQ + MLA_QR
_O_KR = _O_CKV + MLA_KVR
_O_SQ = _O_KR + MLA_ROPE
_O_SK = _O_SQ + SWA_HEADS * SWA_HD
_O_SV = _O_SK + SWA_KV * SWA_HD


def _w_in_to_padded(w, axis=-1):
    axis = axis % w.ndim
    cut = lambda a, b: lax.slice_in_dim(w, a, b, axis=axis)
    z, xbc, dt = cut(0, SSD_INNER), cut(SSD_INNER, _O_DT), cut(_O_DT, _O_CQ)
    cq, ckv, kr = cut(_O_CQ, _O_CKV), cut(_O_CKV, _O_KR), cut(_O_KR, _O_SQ)
    sq, sk, sv = cut(_O_SQ, _O_SK), cut(_O_SK, _O_SV), cut(_O_SV, D_IN)
    zeros = lambda n: jnp.zeros(w.shape[:axis] + (n,) + w.shape[axis + 1:], w.dtype)
    return jnp.concatenate([xbc, z, cq, ckv, dt, zeros(ROPE_LANE - SSD_HEADS), kr, zeros(LANE - ROPE_LANE - MLA_ROPE),
                            _pad_heads(sq, SWA_HEADS, SWA_HD, axis), _pad_heads(sk, SWA_KV, SWA_HD, axis),
                            _pad_heads(sv, SWA_KV, SWA_HD, axis)], axis=axis)


def _w_in_from_padded(g, axis=-1):
    axis = axis % g.ndim
    cut = lambda a, b: lax.slice_in_dim(g, a, b, axis=axis)
    xbc, z, cq, ckv = cut(C_XBC, C_Z), cut(C_Z, C_CQ), cut(C_CQ, C_CKV), cut(C_CKV, C_MISC)
    dt, kr = cut(C_MISC, C_MISC + SSD_HEADS), cut(C_MISC + ROPE_LANE, C_MISC + ROPE_LANE + MLA_ROPE)
    sq = _unpad_heads(cut(C_SQ, C_SK), SWA_HEADS, SWA_HD, axis)
    sk = _unpad_heads(cut(C_SK, C_SV), SWA_KV, SWA_HD, axis)
    sv = _unpad_heads(cut(C_SV, D_INP), SWA_KV, SWA_HD, axis)
    return jnp.concatenate([z, xbc, dt, cq, ckv, kr, sq, sk, sv], axis=axis)


def _w_out_to_padded(w):
    a = SSD_INNER
    b = a + MLA_HEADS * MLA_V
    return jnp.concatenate([w[..., :a, :], _pad_heads(w[..., a:b, :], MLA_HEADS, MLA_V, axis=-2),
                            _pad_heads(w[..., b:, :], SWA_HEADS, SWA_HD, axis=-2)], axis=-2)


def _w_out_from_padded(g):
    a = SSD_INNER
    b = a + MLA_HEADS * LANE
    return jnp.concatenate([g[..., :a, :], _unpad_heads(g[..., a:b, :], MLA_HEADS, MLA_V, axis=-2),
                            _unpad_heads(g[..., b:, :], SWA_HEADS, SWA_HD, axis=-2)], axis=-2)


def _w_ukv_to_padded(w):
    w4 = w.reshape(w.shape[:-1] + (MLA_HEADS, MLA_NOPE + MLA_V))
    flat = lambda t: t.reshape(w.shape[:-1] + (MLA_HEADS * t.shape[-1],))
    return jnp.concatenate([_pad_heads(flat(w4[..., :MLA_NOPE]), MLA_HEADS, MLA_NOPE),
                            _pad_heads(flat(w4[..., MLA_NOPE:]), MLA_HEADS, MLA_V)], axis=-1)


def _w_ukv_from_padded(g):
    hw = MLA_HEADS * LANE
    gk = _unpad_heads(g[..., :hw], MLA_HEADS, MLA_NOPE).reshape(g.shape[:-1] + (MLA_HEADS, MLA_NOPE))
    gv = _unpad_heads(g[..., hw:], MLA_HEADS, MLA_V).reshape(g.shape[:-1] + (MLA_HEADS, MLA_V))
    return jnp.concatenate([gk, gv], axis=-1).reshape(g.shape[:-1] + (MLA_HEADS * (MLA_NOPE + MLA_V),))


def _pad_lane(v):
    return jnp.pad(v, [(0, 0)] * (v.ndim - 1) + [(0, LANE - v.shape[-1])])


def _rope_tables(positions):
    inv_freq = ROPE_THETA ** (-jnp.arange(0, MLA_ROPE, 2, dtype=F32) / MLA_ROPE)
    ang = positions.astype(F32).reshape(-1, 1) * inv_freq
    cos, sin = jnp.cos(ang), jnp.sin(ang)
    S = ang.shape[0]
    one, zero = jnp.ones((S, ROPE_LANE), F32), jnp.zeros((S, ROPE_LANE), F32)
    tail1, tail0 = jnp.ones((S, LANE - ROPE_LANE - MLA_ROPE), F32), jnp.zeros((S, LANE - ROPE_LANE - MLA_ROPE), F32)
    z16 = jnp.zeros_like(sin)
    return (jnp.concatenate([one, cos, cos, tail1], axis=1), jnp.concatenate([zero, -sin, z16, tail0], axis=1),
            jnp.concatenate([zero, z16, sin, tail0], axis=1))


def _layer_fwd(l, x_in, f_prev, gate_prev, mod, P, tabs):
    sh1, sc1, g1, sh2, sc2, g2 = [mod[k:k + 1] for k in range(6)]
    tag = f"l{l}_"
    if f_prev is None:
        x0 = x_in
        h1 = _norm_fwd(x0, P["n1g"], sc1, sh1, name=tag + "norm1")
    else:
        x0, h1 = _norm_fwd(x_in, P["n1g"], sc1, sh1, f=f_prev, gate=gate_prev, name=tag + "norm1")
    proj = _mm(h1, P["w_in"], tb=True, name=tag + "proj")
    P.update(P.pop("mid")(proj))
    y_ssd, hin = _ssd_fwd(proj, P["ssd_cw"], P["ssd_cb"], P["dtb"], P["alog"], P["dsk"],
                          P["ssd_ng"], name=tag + "ssd")
    q, k, v, vt = _mla_prep_fwd(proj, *tabs, P["gq"], P["gkv"], P["w_uq"], P["w_ukv"], name=tag + "mla_prep")
    o_mla, lse_mla, lset_mla = _attn_fwd(q, k, vt, name=tag + "mla_attn")
    o_swa, lse_swa = _swa_fwd(proj, P["sinks"], name=tag + "swa")
    ycat = jnp.concatenate([y_ssd.astype(_ACT), o_mla.astype(_ACT), o_swa.astype(_ACT)], axis=1)
    y = _mm(ycat, P["w_out"], name=tag + "out")
    P.update(P.pop("late")(y))
    x1, h2 = _norm_fwd(x0, P["n2g"], sc2, sh2, f=y, gate=g1, name=tag + "norm2")
    up = _mm(h2, P["w_up"], tb=True, name=tag + "up")
    act = _ffn_act_fwd(up, P["fcw"], P["fcb"], name=tag + "ffn_act")
    f = _mm(act, P["w_down"], name=tag + "down")
    saved = dict(x0=x0, h1=h1, proj=proj, hin=hin, q=q, k=k, v=v, o_mla=o_mla, lse_mla=lse_mla, lset_mla=lset_mla, o_swa=o_swa,
                 lse_swa=lse_swa, ycat=ycat, y=y, x1=x1, h2=h2, up=up, act=act, f=f, mod=mod)
    return x1, f, g2, saved


def _layer_bwd(l, dxo, sv, P, tabs, on_part):
    mod = sv["mod"]
    sh1, sc1, g1, sh2, sc2, g2 = [mod[k:k + 1] for k in range(6)]
    tag = f"l{l}_b_"
    G = {}
    df, dg2 = _gate_bwd(dxo, sv["f"], g2, name=tag + "gate2")
    dact = _mm(df, P["w_down"], tb=True, name=tag + "dact")
    G["w_down"] = _mm(sv["act"], df, ta=True, name=tag + "dw_down")
    dup, G["fcw"], G["fcb"] = _ffn_bwd(sv["up"], dact, P["fcw"], P["fcb"], name=tag + "ffn")
    dh2 = _mm(dup, P["w_up"], name=tag + "dh2")
    G["w_up"] = _mm(dup, sv["h2"], ta=True, name=tag + "dw_up")
    token = on_part(l, "ffn", G)
    if token is not None:
        sc2 = sc2 + token
    dx1, G["n2g"], dsc2, dsh2 = _norm_bwd(dh2, sv["x1"], dxo, P["n2g"], sc2, name=tag + "norm2")
    dy, dg1 = _gate_bwd(dx1, sv["y"], g1, name=tag + "gate1")
    dycat = _mm(dy, P["w_out"], tb=True, name=tag + "dycat")
    G["w_out"] = _mm(sv["ycat"], dy, ta=True, name=tag + "dw_out")
    token = on_part(l, "out", G)
    ssd_cb = P["ssd_cb"] if token is None else P["ssd_cb"] + token
    proj = sv["proj"]
    (dpre, dz, dmisc_ssd, G["ssd_cw"], G["ssd_cb"], G["dtb"], G["alog"], G["dsk"], G["ssd_ng"]) = _ssd_bwd(
        proj, dycat, sv["hin"], P["ssd_cw"], ssd_cb, P["dtb"], P["alog"], P["dsk"],
        P["ssd_ng"], name=tag + "ssd")
    att = (sv["q"], sv["k"], sv["v"], sv["o_mla"])
    dq = _attn_bwd_dq(*att, sv["lse_mla"], dycat, name=tag + "mla_dq")
    dk, dv = _attn_bwd_dkv(*att, sv["lset_mla"], dycat, name=tag + "mla_dkv")
    dcq, dckv, dmisc, dqh, dkv, qn, kvn, G["gq"], G["gkv"] = _mla_prep_bwd(
        proj, dq, dk, dv, dmisc_ssd, *tabs, P["gq"], P["gkv"], P["w_uq"], P["w_ukv"], name=tag + "mla_prep")
    G["w_uq"] = _mm(qn, dqh, ta=True, name=tag + "dw_uq")
    G["w_ukv"] = _mm(kvn, dkv, ta=True, name=tag + "dw_ukv")
    dsq, dsk_, dsv_, G["sinks"] = _swa_bwd(proj, sv["o_swa"], sv["lse_swa"], dycat, P["sinks"], name=tag + "swa")
    dproj = jnp.concatenate([dpre, dz, dcq, dckv, dmisc, dsq, dsk_, dsv_], axis=1)
    G["w_in"] = _mm(dproj, sv["h1"], ta=True, name=tag + "dw_in")
    token = on_part(l, "mixer", G)
    if token is not None:
        sc1 = sc1 + token
    dh1 = _mm(dproj, P["w_in"], name=tag + "dh1")
    dx0, G["n1g"], dsc1, dsh1 = _norm_bwd(dh1, sv["x0"], dx1, P["n1g"], sc1, name=tag + "norm1")
    G["mod"] = jnp.concatenate([dsh1, dsc1, dg1, dsh2, dsc2, dg2], axis=0)
    return dx0, G


def _local_step(x, tgt, mods, get_params, tabs, final_g, on_grads, on_part):
    saved, params = [], []
    xin, f, gate = x, None, None
    for l in range(DEPTH):
        params.append(get_params(l, x if f is None else f))
        xin, f, gate, sv = _layer_fwd(l, xin, f, gate, mods[l], params[l], tabs)
        saved.append(sv)
    loss, dx, dfinal = _final_loss(xin, f, gate, final_g, tgt, name="final_loss")
    for l in reversed(range(DEPTH)):
        dx, G = _layer_bwd(l, dx, saved[l], params[l], tabs, on_part)
        on_grads(l, G)
    return loss[0, 0], dx, dfinal


_WEIGHTS = ['ada_w', 'ada_b', 'norm1_g', 'norm2_g', 'w_in', 'ssd_conv_w', 'ssd_conv_b', 'ssd_dt_bias', 'ssd_a_log',
            'ssd_d', 'ssd_norm_g', 'mla_q_norm_g', 'mla_w_uq', 'mla_kv_norm_g', 'mla_w_ukv', 'swa_sinks', 'w_out',
            'ffn_w_up', 'ffn_conv_w', 'ffn_conv_b', 'ffn_w_down', 'final_norm_g']
_INPUTS = ['x', 'c', 'positions'] + _WEIGHTS + ['loss_target'] + ['m_' + n for n in _WEIGHTS] + ['v_' + n for n in _WEIGHTS]
_SMALL = [('ada_b', 'mod'), ('norm1_g', 'n1g'), ('norm2_g', 'n2g'), ('ssd_conv_b', 'ssd_cb'), ('ssd_dt_bias', 'dtb'),
          ('ssd_a_log', 'alog'), ('ssd_d', 'dsk'), ('ssd_norm_g', 'ssd_ng'), ('mla_q_norm_g', 'gq'),
          ('mla_kv_norm_g', 'gkv'), ('swa_sinks', 'sinks'), ('ffn_conv_b', 'fcb')]
_SHARDED = [('w_in', 'w_in', 2), ('ssd_conv_w', 'ssd_cw', 2), ('mla_w_uq', 'w_uq', 2), ('mla_w_ukv', 'w_ukv', 2),
            ('w_out', 'w_out', 1), ('ffn_w_up', 'w_up', 2), ('ffn_conv_w', 'fcw', 2), ('ffn_w_down', 'w_down', 1)]
_SHARDED_NAMES = [n for n, _, _ in _SHARDED]
_TRANSPOSED = ('w_in', 'ffn_w_up')


def _pack_small(per_layer, final):
    parts = []
    for name, _ in _SMALL:
        v = per_layer[name]
        v = v.reshape(DEPTH, -1)
        pad = (-v.shape[1]) % LANE
        parts.append(jnp.pad(v, ((0, 0), (0, pad))).reshape(-1))
    parts.append(final.reshape(-1))
    return jnp.concatenate(parts).reshape(-1, LANE)


def _unpack_small(packed, shapes):
    flat = packed.reshape(-1)
    out, off = {}, 0
    for name, _ in _SMALL:
        n = math.prod(shapes[name][1:])
        npad = n + (-n) % LANE
        out[name] = flat[off:off + DEPTH * npad].reshape(DEPTH, npad)[:, :n].reshape(shapes[name])
        off += DEPTH * npad
    out['final_norm_g'] = flat[off:off + D]
    return out


def _shard_major(g, axis):
    shp = g.shape
    g = g.reshape(shp[:axis] + (NDEV, shp[axis] // NDEV) + shp[axis + 1:])
    return jnp.moveaxis(g, axis, 0)


def _unshard(g, axis):
    g = jnp.moveaxis(g, 0, axis)
    shp = g.shape
    return g.reshape(shp[:axis] + (shp[axis] * shp[axis + 1],) + shp[axis + 2:])


def kernel(x, c, positions, ada_w, ada_b, norm1_g, norm2_g, w_in, ssd_conv_w, ssd_conv_b, ssd_dt_bias, ssd_a_log, ssd_d, ssd_norm_g, mla_q_norm_g, mla_w_uq, mla_kv_norm_g, mla_w_ukv, swa_sinks, w_out, ffn_w_up, ffn_conv_w, ffn_conv_b, ffn_w_down, final_norm_g, loss_target, m_ada_w, m_ada_b, m_norm1_g, m_norm2_g, m_w_in, m_ssd_conv_w, m_ssd_conv_b, m_ssd_dt_bias, m_ssd_a_log, m_ssd_d, m_ssd_norm_g, m_mla_q_norm_g, m_mla_w_uq, m_mla_kv_norm_g, m_mla_w_ukv, m_swa_sinks, m_w_out, m_ffn_w_up, m_ffn_conv_w, m_ffn_conv_b, m_ffn_w_down, m_final_norm_g, v_ada_w, v_ada_b, v_norm1_g, v_norm2_g, v_w_in, v_ssd_conv_w, v_ssd_conv_b, v_ssd_dt_bias, v_ssd_a_log, v_ssd_d, v_ssd_norm_g, v_mla_q_norm_g, v_mla_w_uq, v_mla_kv_norm_g, v_mla_w_ukv, v_swa_sinks, v_w_out, v_ffn_w_up, v_ffn_conv_w, v_ffn_conv_b, v_ffn_w_down, v_final_norm_g):
    a = dict(zip(_INPUTS, (x, c, positions, ada_w, ada_b, norm1_g, norm2_g, w_in, ssd_conv_w, ssd_conv_b, ssd_dt_bias, ssd_a_log, ssd_d, ssd_norm_g, mla_q_norm_g, mla_w_uq, mla_kv_norm_g, mla_w_ukv, swa_sinks, w_out, ffn_w_up, ffn_conv_w, ffn_conv_b, ffn_w_down, final_norm_g, loss_target, m_ada_w, m_ada_b, m_norm1_g, m_norm2_g, m_w_in, m_ssd_conv_w, m_ssd_conv_b, m_ssd_dt_bias, m_ssd_a_log, m_ssd_d, m_ssd_norm_g, m_mla_q_norm_g, m_mla_w_uq, m_mla_kv_norm_g, m_mla_w_ukv, m_swa_sinks, m_w_out, m_ffn_w_up, m_ffn_conv_w, m_ffn_conv_b, m_ffn_w_down, m_final_norm_g, v_ada_w, v_ada_b, v_norm1_g, v_norm2_g, v_w_in, v_ssd_conv_w, v_ssd_conv_b, v_ssd_dt_bias, v_ssd_a_log, v_ssd_d, v_ssd_norm_g, v_mla_q_norm_g, v_mla_w_uq, v_mla_kv_norm_g, v_mla_w_ukv, v_swa_sinks, v_w_out, v_ffn_w_up, v_ffn_conv_w, v_ffn_conv_b, v_ffn_w_down, v_final_norm_g)))
    axes = ("x", "y", "c")
    me = 4 * lax.axis_index("x") + 2 * lax.axis_index("y") + lax.axis_index("c")
    ncol = ada_w.shape[-1]

    kform = lambda n, t: jnp.swapaxes(t, -1, -2) if n in _TRANSPOSED else t
    mxu_names = ('w_in', 'mla_w_uq', 'mla_w_ukv', 'w_out', 'ffn_w_up', 'ffn_w_down')
    gather_groups = (("early", _SHARDED_NAMES[:4]), ("mid", _SHARDED_NAMES[4:5]), ("late", _SHARDED_NAMES[5:]))

    def own_of(src, names, l):
        return [kform(n, src[n][l]).astype(_MXU) if n in mxu_names else src[n][l] for n in names]

    first_gather = _xchg_start(own_of(a, gather_groups[0][1], 0), scatter=False, name="gather_start_early0")

    c_all = _exchange([c + first_gather["token"][0, 0]], scatter=False, name="gather_c")[0]
    c_act = _silu_call(c_all.reshape(NDEV, D), name="c_act")
    mod_part = jnp.stack([_mm(c_act, ada_w[l], name=f"mod{l}") for l in range(DEPTH)])
    mod_all = _exchange([mod_part], scatter=False, name="gather_mod")[0]
    mod_mine = lax.dynamic_index_in_dim(mod_all, me, axis=2, keepdims=False)
    mods = (jnp.moveaxis(mod_mine, 0, 1).reshape(DEPTH, 6 * D) + ada_b).reshape(DEPTH, 6, D)
    tabs = _rope_tables(positions)

    shard_of = {n: (key, 1 if n in _TRANSPOSED else ax) for n, key, ax in _SHARDED}
    mods, raw = lax.optimization_barrier((mods, {n: a[n] for n in _SHARDED_NAMES}))
    gathers, prev = [], first_gather["token"]
    for l in range(DEPTH):
        gathers.append({})
        for grp, names in gather_groups:
            if (l, grp) == (0, "early"):
                gathers[l][grp] = first_gather
                continue
            srcs, _ = lax.optimization_barrier((own_of(raw, names, l), prev))
            gathers[l][grp] = _xchg_start(srcs, scatter=False, name=f"gather_start_{grp}{l}")
            prev = gathers[l][grp]["token"]

    def place_own(landed, mine):
        return [lax.dynamic_update_index_in_dim(t, o, me, 0) for t, o in zip(landed, mine)]

    def gathered(l, grp, after):
        names = dict(gather_groups)[grp]
        mine, landed = _xchg_wait(gathers[l][grp], after, name=f"gather_wait_{grp}{l}")
        return {n: _unshard(g, shard_of[n][1] - 1) for n, g in zip(names, place_own(landed, mine))}

    def get_params(l, after):
        full = gathered(l, "early", after)
        vec = lambda t: t[l].reshape(1, -1)

        def mid(after2):
            return dict(w_out=_w_out_to_padded(gathered(l, "mid", after2)['w_out']))

        def late(after2):
            rest = gathered(l, "late", after2)
            return dict(w_up=rest['ffn_w_up'], w_down=rest['ffn_w_down'], fcw=rest['ffn_conv_w'])

        return dict(
            w_in=_w_in_to_padded(full['w_in'], axis=0), w_uq=_pad_heads(full['mla_w_uq'], MLA_HEADS, MLA_NOPE + MLA_ROPE),
            w_ukv=_w_ukv_to_padded(full['mla_w_ukv']), ssd_cw=full['ssd_conv_w'], mid=mid, late=late,
            ssd_cb=vec(ssd_conv_b), dtb=vec(_pad_lane(ssd_dt_bias)), alog=vec(_pad_lane(ssd_a_log)),
            dsk=vec(_pad_lane(ssd_d)), ssd_ng=vec(ssd_norm_g), gq=vec(mla_q_norm_g), gkv=vec(mla_kv_norm_g),
            sinks=vec(_pad_lane(swa_sinks)), fcb=vec(ffn_conv_b), n1g=vec(norm1_g), n2g=vec(norm2_g))

    unpad = dict(w_in=functools.partial(_w_in_from_padded, axis=0), w_out=_w_out_from_padded, w_ukv=_w_ukv_from_padded,
                 w_uq=lambda g: _unpad_heads(g, MLA_HEADS, MLA_NOPE + MLA_ROPE))
    scatter_groups = (("ffn", _SHARDED_NAMES[5:]), ("out", _SHARDED_NAMES[4:5]), ("mixer", _SHARDED_NAMES[:4]))
    grads = [None] * DEPTH
    scatters = [dict() for _ in range(DEPTH)]

    def on_part(l, grp, G):
        parts = [_shard_major(unpad.get(shard_of[n][0], lambda g: g)(G[shard_of[n][0]]), shard_of[n][1] - 1)
                 for n in dict(scatter_groups)[grp]]
        scatters[l][grp] = _xchg_start(parts, scatter=True, name=f"scatter_start_{grp}{l}")
        return scatters[l][grp]["token"][0, 0]

    def on_grads(l, G):
        grads[l] = G

    mods = mods + sum(g[grp]["token"][0, 0] for g in gathers for grp, _ in gather_groups)
    loss, dx, dfinal = _local_step(x[0], loss_target[0], mods, get_params, tabs, final_norm_g.reshape(1, D),
                                   on_grads, on_part)
    loss = lax.psum(loss, axes)

    stack = lambda key: jnp.stack([grads[l][key] for l in range(DEPTH)])
    small_g = {name: stack(key).reshape(DEPTH, -1) for name, key in _SMALL}
    small_gather = _xchg_start([_pack_small(small_g, dfinal)], scatter=False, name="gather_small_start")

    out_g, out_d, out_m, out_v = {}, {}, {}, {}
    chain = {name: None for name in _SHARDED_NAMES}
    me_arr = jnp.reshape(me, (1,)).astype(jnp.int32)
    after = small_gather["token"]
    for l in reversed(range(DEPTH)):
        for grp, names in scatter_groups:
            mine, landed = _xchg_wait(scatters[l][grp], after, name=f"scatter_wait_{grp}{l}")
            for name, own, got in zip(names, mine, landed):
                chain[name] = _adamw_layer(l, kform(name, a[name]), kform(name, a['m_' + name]),
                                           kform(name, a['v_' + name]), got, own, me_arr, chain[name],
                                           name=f"adamw_{name}{l}")
    for name in _SHARDED_NAMES:
        out_g[name], out_d[name], out_m[name], out_v[name] = [kform(name, t) for t in chain[name]]
    small_parts = place_own(*reversed(_xchg_wait(small_gather, chain[_SHARDED_NAMES[0]][0],
                                                 name="gather_small_wait")))[0]

    def update(name, wv, mv, vv, pv):
        shp = wv.shape
        r = lambda t: t.reshape((-1, shp[-1]))
        res = _adamw(r(wv), r(mv), r(vv), pv.reshape((pv.shape[0], -1, shp[-1])), name="adamw_" + name)
        out_g[name], out_d[name], out_m[name], out_v[name] = [t.reshape(shp) for t in res]

    n_ada = DEPTH * 6 * D // LANE
    dmod_all = small_parts[:, :n_ada].reshape(NDEV, DEPTH, 6 * D)
    dmod_mine = lax.dynamic_slice_in_dim(dmod_all, me * ncol, ncol, axis=2)
    g_ada = jnp.stack([_mm(c_act, dmod_mine[:, l], ta=True, name=f"dw_ada{l}") for l in range(DEPTH)])
    update('ada_w', ada_w, m_ada_w, v_ada_w, g_ada[None])
    shapes = {n: a[n].shape for n, _ in _SMALL}
    pk = lambda pre: _pack_small({n: a[pre + n] for n, _ in _SMALL}, a[pre + 'final_norm_g'])
    res = _adamw(pk(''), pk('m_'), pk('v_'), small_parts, name="adamw_small")
    for dst, t in zip((out_g, out_d, out_m, out_v), res):
        dst.update(_unpack_small(t, shapes))

    outs = [loss, dx[None]]
    for dct in (out_g, out_d, out_m, out_v):
        outs += [dct[n] for n in _WEIGHTS]
    return tuple(outs)
```

```python
import functools
import math

import jax
import jax.numpy as jnp
from jax import lax
from jax.experimental import pallas as pl
from jax.experimental.pallas import tpu as pltpu

F32 = jnp.float32
_MXU = jnp.bfloat16
_ACT = jnp.bfloat16
_HI = lax.Precision.HIGHEST
EPS = 1e-6
NDEV = 8
DEPTH = 4
D = 1024
LANE = 128
SUB = 8
VMEM_LIMIT = 56 * 1024 * 1024

SSD_INNER, SSD_STATE, SSD_HEADS, SSD_GROUPS, SSD_CHUNK, SSD_CONV = 512, 128, 8, 2, 128, 4
SSD_XBC = SSD_INNER + 2 * SSD_GROUPS * SSD_STATE
MLA_HEADS, MLA_NOPE, MLA_ROPE, MLA_V, MLA_QR, MLA_KVR = 4, 64, 32, 64, 256, 128
SWA_HEADS, SWA_KV, SWA_HD, WINDOW = 4, 2, 64, 128
D_FF, FFN_CONV = 2816, 3
D_IN = 2472
ROPE_THETA = 10000.0
C_XBC, C_Z, C_CQ, C_CKV, C_MISC, C_SQ, C_SK, C_SV, D_INP = 0, 1024, 1536, 1792, 1920, 2048, 2560, 2816, 3072
ROPE_LANE = 64
D_MIXP = 1536

ADAM_LR, ADAM_B1, ADAM_B2, ADAM_EPS, ADAM_WD, ADAM_STEP = 0.001, 0.9, 0.999, 1e-08, 0.01, 10

TS_ROW = 1024
TS_FFN = 256
TQ_ATT = 1024
TS_SWA = 512


def _tile(n, cap, q=LANE):
    best = None
    for t in range(q, min(n, cap) + 1, q):
        if n % t == 0:
            best = t
    return n if best is None else best


def _cp(ngrid):
    return pltpu.CompilerParams(dimension_semantics=("arbitrary",) * ngrid, vmem_limit_bytes=VMEM_LIMIT)


def _dot(a, b):
    return jnp.dot(a.astype(_MXU), b.astype(_MXU), preferred_element_type=F32)


def _dot_nt(a, b):
    return lax.dot_general(a.astype(_MXU), b.astype(_MXU), (((1,), (1,)), ((), ())), preferred_element_type=F32)


def _dot_tn(a, b):
    return jnp.dot(a.T.astype(_MXU), b.astype(_MXU), preferred_element_type=F32)


def _sigmoid(x):
    return 1.0 / (1.0 + jnp.exp(-x))


def _sigmoid_t(x):
    return 0.5 * jnp.tanh(0.5 * x) + 0.5


def _silu(x):
    return x * _sigmoid_t(x)


def _silu_grad(x):
    s = _sigmoid_t(x)
    return x * s, s * (1.0 + x * (1.0 - s))


def _dsilu(x):
    return _silu_grad(x)[1]


def _softplus(x):
    u = jnp.exp(-jnp.abs(x))
    w = 1.0 + u
    log1p = jnp.where(w == 1.0, u, jnp.log(w) * u / jnp.where(w == 1.0, 1.0, w - 1.0))
    return jnp.maximum(x, 0.0) + log1p


def _colsum(x):
    return jnp.sum(x, axis=0, keepdims=True)


def _rowsum(x):
    return jnp.sum(x, axis=1, keepdims=True)


def _shift_down(t, halo, j):
    if j == 0:
        return t
    n = t.shape[0]
    rolled = pltpu.roll(t, j, 0)
    row = lax.broadcasted_iota(jnp.int32, (SUB, t.shape[1]), 0)
    first = jnp.where(row < j, pltpu.roll(halo, j, 0), rolled[:SUB])
    return jnp.concatenate([first, rolled[SUB:]], axis=0) if n > SUB else first


def _shift_up(t, halo, j):
    if j == 0:
        return t
    n = t.shape[0]
    rolled = pltpu.roll(t, n - j, 0)
    row = lax.broadcasted_iota(jnp.int32, (SUB, t.shape[1]), 0)
    last = jnp.where(row >= SUB - j, pltpu.roll(halo, SUB - j, 0), rolled[n - SUB:])
    return jnp.concatenate([rolled[:n - SUB], last], axis=0) if n > SUB else last


def _mm(a, b, *, ta=False, tb=False, out_dtype=F32, name):
    if ta:
        K, M = a.shape
    else:
        M, K = a.shape
    if tb:
        N, K2 = b.shape
    else:
        K2, N = b.shape
    assert K == K2, (a.shape, b.shape, ta, tb)
    tk = _tile(K, 1536)
    nk = K // tk
    tm, tn = _tile(M, 2048 if nk == 1 else 1536), _tile(N, 1536 if nk == 1 else 1408)
    dn = (((0 if ta else 1,), (1 if tb else 0,)), ((), ()))

    def body(a_ref, b_ref, o_ref, *acc):
        part = lax.dot_general(a_ref[...].astype(_MXU), b_ref[...].astype(_MXU), dn, preferred_element_type=F32)
        if nk == 1:
            o_ref[...] = part.astype(out_dtype)
            return
        acc_ref, = acc
        k = pl.program_id(2)

        @pl.when(k == 0)
        def _():
            acc_ref[...] = part

        @pl.when(k > 0)
        def _():
            acc_ref[...] += part

        @pl.when(k == nk - 1)
        def _():
            o_ref[...] = acc_ref[...].astype(out_dtype)

    a_spec = pl.BlockSpec((tk, tm), lambda i, j, k: (k, i)) if ta else pl.BlockSpec((tm, tk), lambda i, j, k: (i, k))
    b_spec = pl.BlockSpec((tn, tk), lambda i, j, k: (j, k)) if tb else pl.BlockSpec((tk, tn), lambda i, j, k: (k, j))
    return pl.pallas_call(
        body, grid=(M // tm, N // tn, nk), in_specs=[a_spec, b_spec],
        out_specs=pl.BlockSpec((tm, tn), lambda i, j, k: (i, j)),
        out_shape=jax.ShapeDtypeStruct((M, N), out_dtype),
        scratch_shapes=[pltpu.VMEM((tm, tn), F32)] * (nk > 1), compiler_params=_cp(3), name=name)(a, b)


def _row(ts, w, col=0):
    return pl.BlockSpec((ts, w), lambda i: (i, col))


def _vec(w, r=1):
    return pl.BlockSpec((r, w), lambda i: (0, 0))


def _silu_call(x, name):
    def body(x_ref, o_ref):
        o_ref[...] = _silu(x_ref[...])
    return pl.pallas_call(body, out_shape=jax.ShapeDtypeStruct(x.shape, F32), name=name)(x)


def _norm_fwd(x, g, sc, sh, *, f=None, gate=None, name):
    S, dm = x.shape
    ts = _tile(S, TS_ROW, SUB)
    res = f is not None

    def body(*refs):
        if res:
            x_ref, f_ref, gate_ref, g_ref, sc_ref, sh_ref, xo_ref, h_ref = refs
            xv = x_ref[...] + gate_ref[...] * f_ref[...]
            xo_ref[...] = xv
        else:
            x_ref, g_ref, sc_ref, sh_ref, h_ref = refs
            xv = x_ref[...]
        rstd = lax.rsqrt(jnp.mean(xv * xv, axis=-1, keepdims=True) + EPS)
        h_ref[...] = ((xv * rstd) * g_ref[...] * (1.0 + sc_ref[...]) + sh_ref[...]).astype(_ACT)

    ins = [x] + ([f, gate] if res else []) + [g, sc, sh]
    in_specs = [_row(ts, dm)] + ([_row(ts, dm), _vec(dm)] if res else []) + [_vec(dm)] * 3
    h_shape = jax.ShapeDtypeStruct((S, dm), _ACT)
    if res:
        out_shape, out_specs = (jax.ShapeDtypeStruct((S, dm), F32), h_shape), (_row(ts, dm), _row(ts, dm))
    else:
        out_shape, out_specs = h_shape, _row(ts, dm)
    return pl.pallas_call(body, grid=(S // ts,), in_specs=in_specs, out_specs=out_specs, out_shape=out_shape,
                          compiler_params=_cp(1), name=name)(*ins)


def _norm_bwd(dh, x, dres, g, sc, *, name):
    S, dm = x.shape
    ts = _tile(S, TS_ROW, SUB)

    def body(dh_ref, x_ref, dres_ref, g_ref, sc_ref, dx_ref, dg_ref, dsc_ref, dsh_ref):
        i = pl.program_id(0)
        xv = x_ref[...]
        dhv = dh_ref[...]
        rstd = lax.rsqrt(jnp.mean(xv * xv, axis=-1, keepdims=True) + EPS)
        xhat = xv * rstd
        hn = xhat * g_ref[...]
        dhn = dhv * (1.0 + sc_ref[...])
        dxh = dhn * g_ref[...]
        dx_ref[...] = dres_ref[...] + rstd * (dxh - xhat * jnp.mean(dxh * xhat, axis=-1, keepdims=True))

        @pl.when(i == 0)
        def _():
            dg_ref[...] = jnp.zeros_like(dg_ref)
            dsc_ref[...] = jnp.zeros_like(dsc_ref)
            dsh_ref[...] = jnp.zeros_like(dsh_ref)

        dg_ref[...] += _colsum(dhn * xhat)
        dsc_ref[...] += _colsum(dhv * hn)
        dsh_ref[...] += _colsum(dhv)

    vshape = jax.ShapeDtypeStruct((1, dm), F32)
    return pl.pallas_call(
        body, grid=(S // ts,), in_specs=[_row(ts, dm)] * 3 + [_vec(dm)] * 2,
        out_specs=(_row(ts, dm), _vec(dm), _vec(dm), _vec(dm)),
        out_shape=(jax.ShapeDtypeStruct((S, dm), F32), vshape, vshape, vshape),
        compiler_params=_cp(1), name=name)(dh, x, dres, g, sc)


def _gate_bwd(dxo, f, gate, *, name):
    S, dm = f.shape
    ts = _tile(S, TS_ROW, SUB)

    def body(dxo_ref, f_ref, gate_ref, df_ref, dgate_ref):
        i = pl.program_id(0)
        dv = dxo_ref[...]
        df_ref[...] = (gate_ref[...] * dv).astype(_ACT)

        @pl.when(i == 0)
        def _():
            dgate_ref[...] = jnp.zeros_like(dgate_ref)

        dgate_ref[...] += _colsum(dv * f_ref[...])

    return pl.pallas_call(
        body, grid=(S // ts,), in_specs=[_row(ts, dm), _row(ts, dm), _vec(dm)],
        out_specs=(_row(ts, dm), _vec(dm)),
        out_shape=(jax.ShapeDtypeStruct((S, dm), _ACT), jax.ShapeDtypeStruct((1, dm), F32)),
        compiler_params=_cp(1), name=name)(dxo, f, gate)


def _final_loss(x, f, gate, g, tgt, *, name):
    S, dm = x.shape
    ts = _tile(S, TS_ROW, SUB)

    def body(x_ref, f_ref, gate_ref, g_ref, t_ref, loss_ref, dx_ref, dg_ref):
        i = pl.program_id(0)
        xv = x_ref[...] + gate_ref[...] * f_ref[...]
        rstd = lax.rsqrt(jnp.mean(xv * xv, axis=-1, keepdims=True) + EPS)
        xhat = xv * rstd
        err = xhat * g_ref[...] - t_ref[...]
        dy = err * (1.0 / dm)
        dxh = dy * g_ref[...]
        dx_ref[...] = rstd * (dxh - xhat * jnp.mean(dxh * xhat, axis=-1, keepdims=True))

        @pl.when(i == 0)
        def _():
            loss_ref[...] = jnp.zeros_like(loss_ref)
            dg_ref[...] = jnp.zeros_like(dg_ref)

        loss_ref[...] += jnp.full((1, LANE), 0.5 * jnp.sum(jnp.mean(err * err, axis=-1, keepdims=True)), F32)
        dg_ref[...] += _colsum(dy * xhat)

    return pl.pallas_call(
        body, grid=(S // ts,), in_specs=[_row(ts, dm), _row(ts, dm), _vec(dm), _vec(dm), _row(ts, dm)],
        out_specs=(_vec(LANE), _row(ts, dm), _vec(dm)),
        out_shape=(jax.ShapeDtypeStruct((1, LANE), F32), jax.ShapeDtypeStruct((S, dm), F32),
                   jax.ShapeDtypeStruct((1, dm), F32)),
        compiler_params=_cp(1), name=name)(x, f, gate, g, tgt)


def _ffn_conv(t, halo, cw_ref, cb_ref):
    t1, t2 = _shift_down(t, halo, 1), _shift_down(t, halo, 2)
    return ((cb_ref[...] + t2 * cw_ref[0:1, :]) + t1 * cw_ref[1:2, :]) + t * cw_ref[2:3, :], t1, t2


def _prev_halo_spec(ts, w, col=0):
    return pl.BlockSpec((SUB, w), lambda i: (jnp.maximum(i * (ts // SUB) - 1, 0), col))


def _ffn_act_fwd(up, cw, cb, *, name):
    S, w2 = up.shape
    ff = w2 // 2
    ts = _tile(S, TS_FFN, SUB)

    def body(up_ref, halo_ref, cw_ref, cb_ref, act_ref):
        i = pl.program_id(0)
        t = up_ref[...]
        halo = jnp.where(i > 0, halo_ref[...], 0.0)
        u, _, _ = _ffn_conv(t, halo, cw_ref, cb_ref)
        act_ref[...] = (_silu(u[:, :ff]) * u[:, ff:]).astype(_ACT)

    return pl.pallas_call(
        body, grid=(S // ts,), in_specs=[_row(ts, w2), _prev_halo_spec(ts, w2), _vec(w2, FFN_CONV), _vec(w2)],
        out_specs=_row(ts, ff), out_shape=jax.ShapeDtypeStruct((S, ff), _ACT),
        compiler_params=_cp(1), name=name)(up, up, cw, cb)


def _ffn_bwd(up, dact, cw, cb, *, name):
    S, w2 = up.shape
    ff = w2 // 2
    ts = _tile(S, TS_FFN, SUB)
    n = S // ts

    def body(up_ref, halo_ref, dact_ref, cw_ref, cb_ref, dup_ref, dcw_ref, dcb_ref, carry_ref):
        i = pl.program_id(0)
        t_idx = n - 1 - i

        @pl.when(i == 0)
        def _():
            carry_ref[...] = jnp.zeros_like(carry_ref)
            dcw_ref[...] = jnp.zeros_like(dcw_ref)
            dcb_ref[...] = jnp.zeros_like(dcb_ref)

        t = up_ref[...]
        halo = jnp.where(t_idx > 0, halo_ref[...], 0.0)
        u, t1, t2 = _ffn_conv(t, halo, cw_ref, cb_ref)
        a, b = u[:, :ff], u[:, ff:]
        da = dact_ref[...]
        sa, dsa = _silu_grad(a)
        dv = jnp.concatenate([da * b * dsa, da * sa], axis=1)
        nxt = carry_ref[...]
        dup = (dv * cw_ref[2:3, :] + _shift_up(dv, nxt, 1) * cw_ref[1:2, :]) + _shift_up(dv, nxt, 2) * cw_ref[0:1, :]
        dup_ref[...] = dup.astype(_ACT)
        dcb_ref[...] += _colsum(dv)
        dcw_ref[2:3, :] += _colsum(dv * t)
        dcw_ref[1:2, :] += _colsum(dv * t1)
        dcw_ref[0:1, :] += _colsum(dv * t2)
        carry_ref[...] = dv[:SUB]

    rev = lambda w: pl.BlockSpec((ts, w), lambda i: (n - 1 - i, 0))
    halo_spec = pl.BlockSpec((SUB, w2), lambda i: (jnp.maximum((n - 1 - i) * (ts // SUB) - 1, 0), 0))
    return pl.pallas_call(
        body, grid=(n,), in_specs=[rev(w2), halo_spec, rev(ff), _vec(w2, FFN_CONV), _vec(w2)],
        out_specs=(rev(w2), _vec(w2, FFN_CONV), _vec(w2)),
        out_shape=(jax.ShapeDtypeStruct((S, w2), _ACT), jax.ShapeDtypeStruct((FFN_CONV, w2), F32),
                   jax.ShapeDtypeStruct((1, w2), F32)),
        scratch_shapes=[pltpu.VMEM((SUB, w2), F32)], compiler_params=_cp(1), name=name)(up, up, dact, cw, cb)


def _ssd_core(pre, halo, misc, cw_ref, cb_ref, dtb, alog):
    q = pre.shape[0]
    conv = cb_ref[...]
    for k in range(SSD_CONV):
        conv = conv + _shift_down(pre, halo, SSD_CONV - 1 - k) * cw_ref[k:k + 1, :]
    xbc = _silu(conv)
    raw = misc + dtb
    dt = _softplus(raw)
    a = -jnp.exp(alog)
    r = lax.broadcasted_iota(jnp.int32, (q, q), 0)
    c = lax.broadcasted_iota(jnp.int32, (q, q), 1)
    tri = r >= c
    acum = jnp.dot(tri.astype(F32), dt * a, precision=_HI, preferred_element_type=F32)
    return conv, xbc, raw, dt, a, acum, acum.T, tri


def _sel(v, j, lo):
    return jnp.where(lo, v[:, 2 * j:2 * j + 1], v[:, 2 * j + 1:2 * j + 2])


def _ssd_pair_fwd(xbc, dt, acum, acum_t, tri, dsk, g_mat, b_mat, c_mat, h_pair, j, lo, lo1, sub_lo):
    q = xbc.shape[0]
    x = xbc[:, LANE * j:LANE * (j + 1)]
    dtp = _sel(dt, j, lo)
    ap = _sel(acum, j, lo)
    xd = x * dtp
    ls, ms = [], []
    for h in (2 * j, 2 * j + 1):
        seg = acum[:, h:h + 1] - acum_t[h:h + 1, :]
        l_mat = jnp.exp(jnp.where(tri, seg, -jnp.inf))
        ls.append(l_mat)
        ms.append(g_mat * l_mat)
    yd = jnp.where(lo, _dot(ms[0], xd), _dot(ms[1], xd))
    ea = jnp.exp(ap)
    yo = _dot_nt(c_mat, h_pair) * ea
    dp = _sel(dsk, j, lo1)
    alast = acum[q - 1:q, :]
    e = jnp.exp(_sel(alast, j, lo1) - ap)
    cd = jnp.where(sub_lo, jnp.exp(alast[:, 2 * j:2 * j + 1]), jnp.exp(alast[:, 2 * j + 1:2 * j + 2]))
    return dict(x=x, dtp=dtp, ap=ap, xd=xd, ls=ls, ms=ms, ea=ea, yo=yo, dp=dp, e=e, cd=cd, y=yd + yo + x * dp)


def _gnorm(yg):
    half = SSD_INNER // SSD_GROUPS
    rstds, yns = [], []
    for g in range(SSD_GROUPS):
        part = yg[:, half * g:half * (g + 1)]
        rstd = lax.rsqrt(jnp.mean(part * part, axis=-1, keepdims=True) + EPS)
        rstds.append(rstd)
        yns.append(part * rstd)
    return rstds, yns


def _ssd_specs(nc, rev):
    q = SSD_CHUNK
    cidx = (lambda i: nc - 1 - i) if rev else (lambda i: i)
    return [
        pl.BlockSpec((q, SSD_XBC), lambda i: (cidx(i), C_XBC // SSD_XBC)),
        pl.BlockSpec((SUB, SSD_XBC), lambda i: (jnp.maximum(cidx(i) * (q // SUB) - 1, 0), C_XBC // SSD_XBC)),
        pl.BlockSpec((q, SSD_INNER), lambda i: (cidx(i), C_Z // SSD_INNER)),
        pl.BlockSpec((q, LANE), lambda i: (cidx(i), C_MISC // LANE)),
    ]


def _ssd_param_specs():
    return [_vec(SSD_XBC, SSD_CONV), _vec(SSD_XBC), _vec(LANE), _vec(LANE), _vec(LANE), _vec(SSD_INNER)]


def _ssd_fwd(proj, cw, cb, dtb, alog, dsk, ng, *, name):
    S = proj.shape[0]
    q = SSD_CHUNK
    nc = S // q
    npair = SSD_HEADS // 2

    def body(xbc_ref, halo_ref, z_ref, misc_ref, cw_ref, cb_ref, dtb_ref, alog_ref, dsk_ref, ng_ref,
             y_ref, hin_ref, h_ref):
        c = pl.program_id(0)

        @pl.when(c == 0)
        def _():
            h_ref[...] = jnp.zeros_like(h_ref)

        pre = xbc_ref[...]
        halo = jnp.where(c > 0, halo_ref[...], 0.0)
        conv, xbc, raw, dt, a, acum, acum_t, tri = _ssd_core(pre, halo, misc_ref[...], cw_ref, cb_ref,
                                                             dtb_ref[...], alog_ref[...])
        lo = lax.broadcasted_iota(jnp.int32, (q, LANE), 1) < LANE // 2
        lo1 = lo[:1]
        sub_lo = lax.broadcasted_iota(jnp.int32, (LANE, LANE), 0) < LANE // 2
        ys = []
        for g in range(SSD_GROUPS):
            b_mat = xbc[:, SSD_INNER + SSD_STATE * g:SSD_INNER + SSD_STATE * (g + 1)]
            c_mat = xbc[:, SSD_INNER + SSD_STATE * (SSD_GROUPS + g):SSD_INNER + SSD_STATE * (SSD_GROUPS + g + 1)]
            g_mat = _dot_nt(c_mat, b_mat)
            for jj in range(npair // SSD_GROUPS):
                j = g * (npair // SSD_GROUPS) + jj
                hj = h_ref[j]
                p = _ssd_pair_fwd(xbc, dt, acum, acum_t, tri, dsk_ref[...], g_mat, b_mat, c_mat, hj, j, lo, lo1, sub_lo)
                ys.append(p["y"])
                hin_ref[0, j] = hj
                h_ref[j] = p["cd"] * hj + _dot_tn(p["xd"] * p["e"], b_mat)
        yg = jnp.concatenate(ys, axis=1) * _silu(z_ref[...])
        _, yns = _gnorm(yg)
        y_ref[...] = jnp.concatenate(yns, axis=1) * ng_ref[...]

    return pl.pallas_call(
        body, grid=(nc,), in_specs=_ssd_specs(nc, False) + _ssd_param_specs(),
        out_specs=(pl.BlockSpec((q, SSD_INNER), lambda i: (i, 0)),
                   pl.BlockSpec((1, npair, LANE, LANE), lambda i: (i, 0, 0, 0))),
        out_shape=(jax.ShapeDtypeStruct((S, SSD_INNER), F32), jax.ShapeDtypeStruct((nc, npair, LANE, LANE), F32)),
        scratch_shapes=[pltpu.VMEM((npair, LANE, LANE), F32)], compiler_params=_cp(1), name=name,
    )(proj, proj, proj, proj, cw, cb, dtb, alog, dsk, ng)


def _ssd_bwd(proj, dycat, hin, cw, cb, dtb, alog, dsk, ng, *, name):
    S = proj.shape[0]
    q = SSD_CHUNK
    nc = S // q
    npair = SSD_HEADS // 2
    ppg = npair // SSD_GROUPS

    def body(xbc_ref, halo_ref, z_ref, misc_ref, dy_ref, hin_ref, cw_ref, cb_ref, dtb_ref, alog_ref, dsk_ref, ng_ref,
             dpre_ref, dz_ref, dmisc_ref, dcw_ref, dcb_ref, ddtb_ref, dalog_ref, ddsk_ref, dng_ref,
             dh_ref, carry_ref):
        i = pl.program_id(0)
        c = nc - 1 - i

        @pl.when(i == 0)
        def _():
            dh_ref[...] = jnp.zeros_like(dh_ref)
            carry_ref[...] = jnp.zeros_like(carry_ref)
            for r in (dcw_ref, dcb_ref, ddtb_ref, dalog_ref, ddsk_ref, dng_ref):
                r[...] = jnp.zeros_like(r)

        pre = xbc_ref[...]
        halo = jnp.where(c > 0, halo_ref[...], 0.0)
        conv, xbc, raw, dt, a, acum, acum_t, tri = _ssd_core(pre, halo, misc_ref[...], cw_ref, cb_ref,
                                                             dtb_ref[...], alog_ref[...])
        lane = lax.broadcasted_iota(jnp.int32, (q, LANE), 1)
        lane1 = lane[:1]
        rowi = lax.broadcasted_iota(jnp.int32, (q, LANE), 0)
        lastrow = rowi == q - 1
        lo = lane < LANE // 2
        lo1 = lo[:1]
        sub_lo = lax.broadcasted_iota(jnp.int32, (LANE, LANE), 0) < LANE // 2
        dsk = dsk_ref[...]
        alast = acum[q - 1:q, :]

        def halves(t):
            return _rowsum(jnp.where(lo, t, 0.0)), _rowsum(jnp.where(lo, 0.0, t))

        def put(ha, va, vb):
            ln = lane if va.shape[0] == q else lane1
            return jnp.where(ln == ha, va, 0.0) + jnp.where(ln == ha + 1, vb, 0.0)

        mats, pairs = [], []
        for g in range(SSD_GROUPS):
            b_mat = xbc[:, SSD_INNER + SSD_STATE * g:SSD_INNER + SSD_STATE * (g + 1)]
            c_mat = xbc[:, SSD_INNER + SSD_STATE * (SSD_GROUPS + g):SSD_INNER + SSD_STATE * (SSD_GROUPS + g + 1)]
            g_mat = _dot_nt(c_mat, b_mat)
            mats.append((b_mat, c_mat, g_mat))
            for jj in range(ppg):
                j = g * ppg + jj
                pairs.append(_ssd_pair_fwd(xbc, dt, acum, acum_t, tri, dsk, g_mat, b_mat, c_mat, hin_ref[0, j],
                                           j, lo, lo1, sub_lo))
        z = z_ref[...]
        sz, dsz = _silu_grad(z)
        y = jnp.concatenate([p["y"] for p in pairs], axis=1)
        rstds, yns = _gnorm(y * sz)
        dout = dy_ref[...]
        dng_ref[...] += _colsum(dout * jnp.concatenate(yns, axis=1))
        dyn = dout * ng_ref[...]
        half = SSD_INNER // SSD_GROUPS
        dygs = []
        for g in range(SSD_GROUPS):
            dyn_g = dyn[:, half * g:half * (g + 1)]
            dygs.append(rstds[g] * (dyn_g - yns[g] * jnp.mean(dyn_g * yns[g], axis=-1, keepdims=True)))
        dyg = jnp.concatenate(dygs, axis=1)
        dyv = dyg * sz
        dz_ref[...] = (dyg * y * dsz).astype(_ACT)

        da_acc = jnp.zeros((q, LANE), F32)
        ddt = jnp.zeros((q, LANE), F32)
        dds = jnp.zeros((1, LANE), F32)
        dxs, dbs, dcs = [], [], []
        for g in range(SSD_GROUPS):
            b_mat, c_mat, g_mat = mats[g]
            dg_mat = jnp.zeros((q, q), F32)
            db = jnp.zeros((q, SSD_STATE), F32)
            dc = jnp.zeros((q, SSD_STATE), F32)
            for jj in range(ppg):
                j = g * ppg + jj
                ha = 2 * j
                p = pairs[j]
                hj = hin_ref[0, j]
                dyp = dyv[:, LANE * j:LANE * (j + 1)]
                dsum = _colsum(dyp * p["x"])
                dds = dds + put(ha, _rowsum(jnp.where(lo1, dsum, 0.0)), _rowsum(jnp.where(lo1, 0.0, dsum)))
                dx = dyp * p["dp"]
                dw = dyp * p["ea"]
                dc = dc + _dot(dw, hj)
                dh_yo = _dot_tn(dw, c_mat)
                ra, rb = halves(dyp * p["yo"])
                da_acc = da_acc + put(ha, ra, rb)
                dxd = jnp.zeros((q, LANE), F32)
                for idx in range(2):
                    dyh = jnp.where(lo, dyp, 0.0) if idx == 0 else jnp.where(lo, 0.0, dyp)
                    dm = _dot_nt(dyh, p["xd"])
                    dxd = dxd + _dot_tn(p["ms"][idx], dyh)
                    dg_mat = dg_mat + dm * p["ls"][idx]
                    t = dm * p["ms"][idx]
                    da_h = _rowsum(t) - _rowsum(t.T)
                    da_acc = da_acc + jnp.where(lane == ha + idx, da_h, 0.0)
                dhn = dh_ref[j]
                s = _rowsum(dhn * hj)
                sa = jnp.sum(jnp.where(sub_lo[:, :1], s, 0.0), keepdims=True)
                sb = jnp.sum(jnp.where(sub_lo[:, :1], 0.0, s), keepdims=True)
                cda, cdb = jnp.exp(alast[:, ha:ha + 1]), jnp.exp(alast[:, ha + 1:ha + 2])
                db = db + _dot(p["xd"] * p["e"], dhn)
                r = _dot_nt(b_mat, dhn)
                dxd = dxd + r * p["e"]
                qa, qb = halves(r * p["xd"] * p["e"])
                da_acc = da_acc - put(ha, qa, qb)
                tot_a = sa * cda + jnp.sum(qa, keepdims=True)
                tot_b = sb * cdb + jnp.sum(qb, keepdims=True)
                da_acc = da_acc + jnp.where(lastrow, put(ha, tot_a, tot_b), 0.0)
                dh_ref[j] = p["cd"] * dhn + dh_yo
                dx = dx + dxd * p["dtp"]
                ua, ub = halves(dxd * p["x"])
                ddt = ddt + put(ha, ua, ub)
                dxs.append(dx)
            dc = dc + _dot(dg_mat, b_mat)
            db = db + _dot_tn(dg_mat, c_mat)
            dbs.append(db)
            dcs.append(dc)
        r2 = lax.broadcasted_iota(jnp.int32, (q, q), 0)
        c2 = lax.broadcasted_iota(jnp.int32, (q, q), 1)
        dda = jnp.dot((c2 >= r2).astype(F32), da_acc, precision=_HI, preferred_element_type=F32)
        ddt = ddt + dda * a
        dalog_ref[...] += _colsum(dda * dt) * a
        ddsk_ref[...] += dds
        draw = jnp.where(lane < SSD_HEADS, ddt * _sigmoid(raw), 0.0)
        ddtb_ref[...] += _colsum(draw)
        dmisc_ref[...] = draw
        dconv = jnp.concatenate(dxs + dbs + dcs, axis=1) * _dsilu(conv)
        dcb_ref[...] += _colsum(dconv)
        nxt = carry_ref[...]
        dpre = jnp.zeros_like(dconv)
        for k in range(SSD_CONV):
            dcw_ref[k:k + 1, :] += _colsum(dconv * _shift_down(pre, halo, SSD_CONV - 1 - k))
            dpre = dpre + _shift_up(dconv, nxt, SSD_CONV - 1 - k) * cw_ref[k:k + 1, :]
        dpre_ref[...] = dpre.astype(_ACT)
        carry_ref[...] = dconv[:SUB]

    rev = lambda i: (nc - 1 - i, 0)
    vshape = lambda w, r=1: jax.ShapeDtypeStruct((r, w), F32)
    return pl.pallas_call(
        body, grid=(nc,),
        in_specs=_ssd_specs(nc, True) + [pl.BlockSpec((q, SSD_INNER), rev),
                                         pl.BlockSpec((1, npair, LANE, LANE), lambda i: (nc - 1 - i, 0, 0, 0))]
        + _ssd_param_specs(),
        out_specs=(pl.BlockSpec((q, SSD_XBC), rev), pl.BlockSpec((q, SSD_INNER), rev), pl.BlockSpec((q, LANE), rev),
                   _vec(SSD_XBC, SSD_CONV), _vec(SSD_XBC), _vec(LANE), _vec(LANE), _vec(LANE), _vec(SSD_INNER)),
        out_shape=(jax.ShapeDtypeStruct((S, SSD_XBC), _ACT), jax.ShapeDtypeStruct((S, SSD_INNER), _ACT),
                   jax.ShapeDtypeStruct((S, LANE), F32),
                   vshape(SSD_XBC, SSD_CONV), vshape(SSD_XBC), vshape(LANE), vshape(LANE), vshape(LANE),
                   vshape(SSD_INNER)),
        scratch_shapes=[pltpu.VMEM((npair, LANE, LANE), F32), pltpu.VMEM((SUB, SSD_XBC), F32)],
        compiler_params=_cp(1), name=name,
    )(proj, proj, proj, proj, dycat, hin, cw, cb, dtb, alog, dsk, ng)


def _rope(x, cosf, sina, sinb):
    return x * cosf + pltpu.roll(x, LANE - MLA_ROPE // 2, 1) * sina + pltpu.roll(x, MLA_ROPE // 2, 1) * sinb


def _rope_t(dy, cosf, sina, sinb):
    return dy * cosf + pltpu.roll(dy * sina, MLA_ROPE // 2, 1) + pltpu.roll(dy * sinb, LANE - MLA_ROPE // 2, 1)


def _rope_lanes(shape):
    lane = lax.broadcasted_iota(jnp.int32, shape, 1)
    return (lane >= ROPE_LANE) & (lane < ROPE_LANE + MLA_ROPE)


def _mla_prep_fwd(proj, cosf, sina, sinb, gq, gkv, wuq, wukv, *, name):
    S = proj.shape[0]
    ts = _tile(S, TS_ROW, SUB)
    hw = MLA_HEADS * LANE

    def body(cq_ref, ckv_ref, misc_ref, cos_ref, sa_ref, sb_ref, gq_ref, gkv_ref, wuq_ref, wukv_ref,
             q_ref, k_ref, v_ref, vt_ref):
        cosv, sav, sbv = cos_ref[...], sa_ref[...], sb_ref[...]
        cq = cq_ref[...]
        qn = cq * lax.rsqrt(jnp.mean(cq * cq, axis=-1, keepdims=True) + EPS) * gq_ref[...]
        qh = _dot(qn, wuq_ref[...])
        ckv = ckv_ref[...]
        kvn = ckv * lax.rsqrt(jnp.mean(ckv * ckv, axis=-1, keepdims=True) + EPS) * gkv_ref[...]
        kv = _dot(kvn, wukv_ref[...])
        kr = _rope(jnp.where(_rope_lanes((ts, LANE)), misc_ref[...], 0.0), cosv, sav, sbv)
        for h in range(MLA_HEADS):
            sl = slice(LANE * h, LANE * (h + 1))
            q_ref[:, sl] = (_rope(qh[:, sl], cosv, sav, sbv) * _Q_SCALE).astype(_ACT)
            k_ref[:, sl] = (kv[:, sl] + kr).astype(_ACT)
        v_ref[...] = kv[:, hw:].astype(_ACT)
        vt_ref[...] = kv[:, hw:].T.astype(_ACT)

    oshape = jax.ShapeDtypeStruct((S, hw), _ACT)
    return pl.pallas_call(
        body, grid=(S // ts,),
        in_specs=[_row(ts, MLA_QR, C_CQ // MLA_QR), _row(ts, MLA_KVR, C_CKV // MLA_KVR), _row(ts, LANE, C_MISC // LANE),
                  _row(ts, LANE), _row(ts, LANE), _row(ts, LANE), _vec(MLA_QR), _vec(MLA_KVR),
                  _vec(hw, MLA_QR), _vec(2 * hw, MLA_KVR)],
        out_specs=(_row(ts, hw),) * 3 + (pl.BlockSpec((hw, ts), lambda i: (0, i)),),
        out_shape=(oshape,) * 3 + (jax.ShapeDtypeStruct((hw, S), _ACT),), compiler_params=_cp(1), name=name,
    )(proj, proj, proj, cosf, sina, sinb, gq, gkv, wuq, wukv)


def _mla_prep_bwd(proj, dq, dk, dv, dmisc_ssd, cosf, sina, sinb, gq, gkv, wuq, wukv, *, name):
    S = proj.shape[0]
    ts = _tile(S, TS_ROW, SUB)
    hw = MLA_HEADS * LANE

    def body(cq_ref, ckv_ref, dq_ref, dk_ref, dv_ref, dms_ref, cos_ref, sa_ref, sb_ref, gq_ref, gkv_ref,
             wuq_ref, wukv_ref, dcq_ref, dckv_ref, dmisc_ref, dqh_ref, dkv_ref, qn_ref, kvn_ref, dgq_ref, dgkv_ref):
        i = pl.program_id(0)
        cosv, sav, sbv = cos_ref[...], sa_ref[...], sb_ref[...]

        @pl.when(i == 0)
        def _():
            dgq_ref[...] = jnp.zeros_like(dgq_ref)
            dgkv_ref[...] = jnp.zeros_like(dgkv_ref)

        dqh = jnp.concatenate([_rope_t(dq_ref[:, LANE * h:LANE * (h + 1)], cosv, sav, sbv)
                               for h in range(MLA_HEADS)], axis=1)
        dqh_ref[...] = dqh.astype(_ACT)
        dkv = jnp.concatenate([dk_ref[...], dv_ref[...]], axis=1)
        dkv_ref[...] = dkv.astype(_ACT)

        def norm_bwd(x, g, dn, dg_ref, n_ref):
            rstd = lax.rsqrt(jnp.mean(x * x, axis=-1, keepdims=True) + EPS)
            xhat = x * rstd
            n_ref[...] = (xhat * g).astype(_ACT)
            dg_ref[...] += _colsum(dn * xhat)
            dxh = dn * g
            return rstd * (dxh - xhat * jnp.mean(dxh * xhat, axis=-1, keepdims=True))

        dcq_ref[...] = norm_bwd(cq_ref[...], gq_ref[...], _dot_nt(dqh, wuq_ref[...]), dgq_ref, qn_ref).astype(_ACT)
        dckv_ref[...] = norm_bwd(ckv_ref[...], gkv_ref[...], _dot_nt(dkv, wukv_ref[...]), dgkv_ref, kvn_ref).astype(_ACT)
        dks = dk_ref[:, 0:LANE]
        for h in range(1, MLA_HEADS):
            dks = dks + dk_ref[:, LANE * h:LANE * (h + 1)]
        rl = _rope_lanes((ts, LANE))
        dkr = _rope_t(jnp.where(rl, dks, 0.0), cosv, sav, sbv)
        dmisc_ref[...] = (dms_ref[...] + jnp.where(rl, dkr, 0.0)).astype(_ACT)

    act = lambda w: jax.ShapeDtypeStruct((S, w), _ACT)
    return pl.pallas_call(
        body, grid=(S // ts,),
        in_specs=[_row(ts, MLA_QR, C_CQ // MLA_QR), _row(ts, MLA_KVR, C_CKV // MLA_KVR),
                  _row(ts, hw), _row(ts, hw), _row(ts, hw), _row(ts, LANE),
                  _row(ts, LANE), _row(ts, LANE), _row(ts, LANE), _vec(MLA_QR), _vec(MLA_KVR),
                  _vec(hw, MLA_QR), _vec(2 * hw, MLA_KVR)],
        out_specs=(_row(ts, MLA_QR), _row(ts, MLA_KVR), _row(ts, LANE), _row(ts, hw), _row(ts, 2 * hw),
                   _row(ts, MLA_QR), _row(ts, MLA_KVR), _vec(MLA_QR), _vec(MLA_KVR)),
        out_shape=(act(MLA_QR), act(MLA_KVR), act(LANE), act(hw), act(2 * hw), act(MLA_QR), act(MLA_KVR),
                   jax.ShapeDtypeStruct((1, MLA_QR), F32), jax.ShapeDtypeStruct((1, MLA_KVR), F32)),
        compiler_params=_cp(1), name=name,
    )(proj, proj, dq, dk, dv, dmisc_ssd, cosf, sina, sinb, gq, gkv, wuq, wukv)


_MLA_SCALE = 1.0 / math.sqrt(MLA_NOPE + MLA_ROPE)
_LOG2E = 1.4426950408889634
_Q_SCALE = _MLA_SCALE * _LOG2E
ATT_CHUNK = 1024


def _tri_grid(nq, by_key):
    if by_key:
        pairs = [(i, j) for j in range(nq) for i in range(j, nq)]
    else:
        pairs = [(i, j) for i in range(nq) for j in range(i + 1)]
    return jnp.asarray([p[0] for p in pairs], jnp.int32), jnp.asarray([p[1] for p in pairs], jnp.int32)


def _attn_fwd(q, k, vt, *, name):
    S = q.shape[0]
    tq = _tile(S, TQ_ATT)
    nq = S // tq
    itab, jtab = _tri_grid(nq, False)

    def body(it_ref, jt_ref, q_ref, k_ref, vt_ref, o_ref, lse_ref, lset_ref, m_ref, l_ref, acc_ref):
        t = pl.program_id(1)
        i, j = it_ref[t], jt_ref[t]

        @pl.when(j == 0)
        def _():
            m_ref[...] = jnp.full_like(m_ref, -jnp.inf)
            l_ref[...] = jnp.zeros_like(l_ref)
            acc_ref[...] = jnp.zeros_like(acc_ref)

        def step(diagonal):
            s = _dot_nt(k_ref[...], q_ref[...])
            if diagonal:
                kk = lax.broadcasted_iota(jnp.int32, (tq, tq), 0)
                s = jnp.where(kk <= lax.broadcasted_iota(jnp.int32, (tq, tq), 1), s, -jnp.inf)
            m_prev = m_ref[...]
            m_new = jnp.maximum(m_prev, jnp.max(s, axis=0, keepdims=True))
            p = jnp.exp2(s - m_new)
            alpha = jnp.exp2(m_prev - m_new)
            l_ref[...] = alpha * l_ref[...] + _colsum(p)
            acc_ref[...] = alpha * acc_ref[...] + _dot(vt_ref[...], p)
            m_ref[...] = m_new

        pl.when(j < i)(functools.partial(step, False))
        pl.when(j == i)(functools.partial(step, True))

        @pl.when(j == i)
        def _():
            o_ref[...] = (acc_ref[...] / l_ref[...]).T
            lse = m_ref[...] + jnp.log2(l_ref[...])
            lset_ref[...] = jnp.broadcast_to(lse, (SUB, tq))
            lse_ref[...] = jnp.broadcast_to(lse, (LANE, tq)).T

    qspec = pl.BlockSpec((tq, LANE), lambda h, t, it, jt: (it[t], h))
    kspec = pl.BlockSpec((tq, LANE), lambda h, t, it, jt: (jt[t], h))
    vtspec = pl.BlockSpec((LANE, tq), lambda h, t, it, jt: (h, jt[t]))
    oshape = jax.ShapeDtypeStruct((S, MLA_HEADS * LANE), F32)
    return pl.pallas_call(
        body,
        grid_spec=pltpu.PrefetchScalarGridSpec(
            num_scalar_prefetch=2, grid=(MLA_HEADS, itab.shape[0]), in_specs=[qspec, kspec, vtspec],
            out_specs=(qspec, qspec, pl.BlockSpec((SUB, tq), lambda h, t, it, jt: (h, it[t]))),
            scratch_shapes=[pltpu.VMEM((1, tq), F32), pltpu.VMEM((1, tq), F32), pltpu.VMEM((LANE, tq), F32)]),
        out_shape=(oshape, oshape, jax.ShapeDtypeStruct((MLA_HEADS * SUB, S), F32)),
        compiler_params=_cp(2), name=name)(itab, jtab, q, k, vt)


def _attn_bwd_dq(q, k, v, o, lse, dycat, *, name):
    S = q.shape[0]
    tq = _tile(S, TQ_ATT)
    nq = S // tq
    rc = min(ATT_CHUNK, tq)
    itab, jtab = _tri_grid(nq, False)

    def body(it_ref, jt_ref, q_ref, k_ref, v_ref, o_ref, lse_ref, do_ref, dq_ref, acc_ref):
        t = pl.program_id(1)
        i, j = it_ref[t], jt_ref[t]

        @pl.when(j == 0)
        def _():
            acc_ref[...] = jnp.zeros_like(acc_ref)

        def step(diagonal):
            kv, vv = k_ref[...], v_ref[...]
            for r in range(tq // rc):
                rows = slice(r * rc, (r + 1) * rc)
                s = _dot_nt(q_ref[rows, :], kv)
                if diagonal:
                    rr = r * rc + lax.broadcasted_iota(jnp.int32, (rc, tq), 0)
                    s = jnp.where(lax.broadcasted_iota(jnp.int32, (rc, tq), 1) <= rr, s, -jnp.inf)
                p = jnp.exp2(s - lse_ref[rows, 0:1])
                dov = do_ref[rows, :]
                delta = _rowsum(dov * o_ref[rows, :])
                ds = p * (_dot_nt(dov, vv) - delta)
                acc_ref[rows, :] += _dot(ds, kv)

        pl.when(j < i)(functools.partial(step, False))
        pl.when(j == i)(functools.partial(step, True))

        @pl.when(j == i)
        def _():
            dq_ref[...] = acc_ref[...] * _MLA_SCALE

    qspec = pl.BlockSpec((tq, LANE), lambda h, t, it, jt: (it[t], h))
    kspec = pl.BlockSpec((tq, LANE), lambda h, t, it, jt: (jt[t], h))
    dospec = pl.BlockSpec((tq, LANE), lambda h, t, it, jt: (it[t], SSD_INNER // LANE + h))
    return pl.pallas_call(
        body,
        grid_spec=pltpu.PrefetchScalarGridSpec(
            num_scalar_prefetch=2, grid=(MLA_HEADS, itab.shape[0]),
            in_specs=[qspec, kspec, kspec, qspec, qspec, dospec], out_specs=qspec,
            scratch_shapes=[pltpu.VMEM((tq, LANE), F32)]),
        out_shape=jax.ShapeDtypeStruct((S, MLA_HEADS * LANE), F32),
        compiler_params=_cp(2), name=name)(itab, jtab, q, k, v, o, lse, dycat)


def _attn_bwd_dkv(q, k, v, o, lset, dycat, *, name):
    S = q.shape[0]
    tq = _tile(S, TQ_ATT)
    nq = S // tq
    kc = min(ATT_CHUNK, tq)
    itab, jtab = _tri_grid(nq, True)

    def body(it_ref, jt_ref, q_ref, k_ref, v_ref, o_ref, lset_ref, do_ref, dk_ref, dv_ref, dk_acc, dv_acc):
        t = pl.program_id(1)
        i, j = it_ref[t], jt_ref[t]

        @pl.when(i == j)
        def _():
            dk_acc[...] = jnp.zeros_like(dk_acc)
            dv_acc[...] = jnp.zeros_like(dv_acc)

        def step(diagonal):
            qv, dov = q_ref[...], do_ref[...]
            delta = lax.dot_general(jnp.ones((SUB, LANE), F32), dov * o_ref[...], (((1,), (1,)), ((), ())),
                                    precision=_HI, preferred_element_type=F32)[0:1]
            lse = lset_ref[0:1, :]
            for c in range(tq // kc):
                rows = slice(c * kc, (c + 1) * kc)
                s = _dot_nt(k_ref[rows, :], qv)
                if diagonal:
                    kk = c * kc + lax.broadcasted_iota(jnp.int32, (kc, tq), 0)
                    s = jnp.where(kk <= lax.broadcasted_iota(jnp.int32, (kc, tq), 1), s, -jnp.inf)
                p = jnp.exp2(s - lse)
                dv_acc[rows, :] += _dot(p, dov)
                ds = p * (_dot_nt(v_ref[rows, :], dov) - delta)
                dk_acc[rows, :] += _dot(ds, qv)

        pl.when(i > j)(functools.partial(step, False))
        pl.when(i == j)(functools.partial(step, True))

        @pl.when(i == nq - 1)
        def _():
            dk_ref[...] = dk_acc[...] * (1.0 / _LOG2E)
            dv_ref[...] = dv_acc[...]

    qspec = pl.BlockSpec((tq, LANE), lambda h, t, it, jt: (it[t], h))
    kspec = pl.BlockSpec((tq, LANE), lambda h, t, it, jt: (jt[t], h))
    dospec = pl.BlockSpec((tq, LANE), lambda h, t, it, jt: (it[t], SSD_INNER // LANE + h))
    lspec = pl.BlockSpec((SUB, tq), lambda h, t, it, jt: (h, it[t]))
    oshape = jax.ShapeDtypeStruct((S, MLA_HEADS * LANE), F32)
    return pl.pallas_call(
        body,
        grid_spec=pltpu.PrefetchScalarGridSpec(
            num_scalar_prefetch=2, grid=(MLA_HEADS, itab.shape[0]),
            in_specs=[qspec, kspec, kspec, qspec, lspec, dospec], out_specs=(kspec, kspec),
            scratch_shapes=[pltpu.VMEM((tq, LANE), F32), pltpu.VMEM((tq, LANE), F32)]),
        out_shape=(oshape, oshape), compiler_params=_cp(2), name=name)(itab, jtab, q, k, v, o, lset, dycat)


_SWA_SCALE = 1.0 / math.sqrt(SWA_HD)
_SWA_KW = SWA_KV * LANE


def _swa_specs(S, ts, rev):
    n = S // ts
    t = (lambda i: n - 1 - i) if rev else (lambda i: i)
    hb = lambda i: jnp.maximum(t(i) * (ts // WINDOW) - 1, 0)
    return [
        pl.BlockSpec((ts, SWA_HEADS * LANE), lambda i: (t(i), C_SQ // (SWA_HEADS * LANE))),
        pl.BlockSpec((ts, _SWA_KW), lambda i: (t(i), C_SK // _SWA_KW)),
        pl.BlockSpec((WINDOW, _SWA_KW), lambda i: (hb(i), C_SK // _SWA_KW)),
        pl.BlockSpec((ts, _SWA_KW), lambda i: (t(i), C_SV // _SWA_KW)),
        pl.BlockSpec((WINDOW, _SWA_KW), lambda i: (hb(i), C_SV // _SWA_KW)),
    ]


def _swa_scores(qh, kk, t, b, ts):
    s = _dot_nt(qh, kk) * _SWA_SCALE
    row = lax.broadcasted_iota(jnp.int32, (WINDOW, 2 * WINDOW), 0)
    col = lax.broadcasted_iota(jnp.int32, (WINDOW, 2 * WINDOW), 1)
    rel = WINDOW + row - col
    kpos = t * ts + (b - 1) * WINDOW + col
    return jnp.where((rel >= 0) & (rel < WINDOW) & (kpos >= 0), s, -jnp.inf)


def _swa_fwd(proj, sinks, *, name):
    S = proj.shape[0]
    ts = _tile(S, TS_SWA)
    nb = ts // WINDOW

    def body(q_ref, k_ref, kh_ref, v_ref, vh_ref, sink_ref, o_ref, lse_ref):
        t = pl.program_id(0)
        kext = jnp.concatenate([kh_ref[...], k_ref[...]], axis=0)
        vext = jnp.concatenate([vh_ref[...], v_ref[...]], axis=0)
        for b in range(nb):
            rows = slice(WINDOW * b, WINDOW * (b + 1))
            for h in range(SWA_HEADS):
                kvl = slice(LANE * (h // (SWA_HEADS // SWA_KV)), LANE * (h // (SWA_HEADS // SWA_KV) + 1))
                hl = slice(LANE * h, LANE * (h + 1))
                kk = kext[WINDOW * b:WINDOW * (b + 2), kvl]
                vv = vext[WINDOW * b:WINDOW * (b + 2), kvl]
                s = _swa_scores(q_ref[rows, hl], kk, t, b, ts)
                sk = sink_ref[:, h:h + 1]
                m = jnp.maximum(jnp.max(s, axis=1, keepdims=True), sk)
                p = jnp.exp(s - m)
                den = _rowsum(p) + jnp.exp(sk - m)
                o_ref[rows, hl] = _dot(p, vv) / den
                lse_ref[rows, hl] = jnp.broadcast_to(m + jnp.log(den), (WINDOW, LANE))

    oshape = jax.ShapeDtypeStruct((S, SWA_HEADS * LANE), F32)
    ospec = pl.BlockSpec((ts, SWA_HEADS * LANE), lambda i: (i, 0))
    return pl.pallas_call(
        body, grid=(S // ts,), in_specs=_swa_specs(S, ts, False) + [_vec(LANE)], out_specs=(ospec, ospec),
        out_shape=(oshape, oshape), compiler_params=_cp(1), name=name)(proj, proj, proj, proj, proj, sinks)


def _swa_bwd(proj, o, lse, dycat, sinks, *, name):
    S = proj.shape[0]
    ts = _tile(S, TS_SWA)
    nb = ts // WINDOW
    n = S // ts
    grp = SWA_HEADS // SWA_KV

    def body(q_ref, k_ref, kh_ref, v_ref, vh_ref, o_ref, lse_ref, do_ref, sink_ref,
             dq_ref, dk_ref, dv_ref, dsink_ref, dk_carry, dv_carry):
        i = pl.program_id(0)
        t = n - 1 - i

        @pl.when(i == 0)
        def _():
            dk_carry[...] = jnp.zeros_like(dk_carry)
            dv_carry[...] = jnp.zeros_like(dv_carry)
            dsink_ref[...] = jnp.zeros_like(dsink_ref)

        kext = jnp.concatenate([kh_ref[...], k_ref[...]], axis=0)
        vext = jnp.concatenate([vh_ref[...], v_ref[...]], axis=0)
        lane1 = lax.broadcasted_iota(jnp.int32, (1, LANE), 1)
        dkb = [[jnp.zeros((WINDOW, LANE), F32) for _ in range(SWA_KV)] for _ in range(nb + 1)]
        dvb = [[jnp.zeros((WINDOW, LANE), F32) for _ in range(SWA_KV)] for _ in range(nb + 1)]
        dsink = jnp.zeros((1, LANE), F32)
        for b in range(nb):
            rows = slice(WINDOW * b, WINDOW * (b + 1))
            for h in range(SWA_HEADS):
                kvh = h // grp
                kvl = slice(LANE * kvh, LANE * (kvh + 1))
                hl = slice(LANE * h, LANE * (h + 1))
                kk = kext[WINDOW * b:WINDOW * (b + 2), kvl]
                vv = vext[WINDOW * b:WINDOW * (b + 2), kvl]
                qh = q_ref[rows, hl]
                lse_h = lse_ref[rows, LANE * h:LANE * h + 1]
                p = jnp.exp(_swa_scores(qh, kk, t, b, ts) - lse_h)
                doh = do_ref[rows, hl]
                delta = _rowsum(doh * o_ref[rows, hl])
                ds = p * (_dot_nt(doh, vv) - delta)
                sk = sink_ref[:, h:h + 1]
                dsink = dsink + jnp.where(lane1 == h, -jnp.sum(jnp.exp(sk - lse_h) * delta, keepdims=True), 0.0)
                dq_ref[rows, hl] = (_dot(ds, kk) * _SWA_SCALE).astype(_ACT)
                dkk = _dot_tn(ds, qh) * _SWA_SCALE
                dvv = _dot_tn(p, doh)
                dkb[b][kvh] = dkb[b][kvh] + dkk[:WINDOW]
                dkb[b + 1][kvh] = dkb[b + 1][kvh] + dkk[WINDOW:]
                dvb[b][kvh] = dvb[b][kvh] + dvv[:WINDOW]
                dvb[b + 1][kvh] = dvb[b + 1][kvh] + dvv[WINDOW:]
        dsink_ref[...] += dsink
        for dref, blocks, carry in ((dk_ref, dkb, dk_carry), (dv_ref, dvb, dv_carry)):
            old = carry[...]
            for b in range(1, nb + 1):
                blk = jnp.concatenate(blocks[b], axis=1)
                if b == nb:
                    blk = blk + old
                dref[WINDOW * (b - 1):WINDOW * b, :] = blk.astype(_ACT)
            carry[...] = jnp.concatenate(blocks[0], axis=1)

    hw = SWA_HEADS * LANE
    rev = lambda i: (n - 1 - i, 0)
    mix = lambda i: (n - 1 - i, (SSD_INNER + MLA_HEADS * LANE) // hw)
    return pl.pallas_call(
        body, grid=(n,),
        in_specs=_swa_specs(S, ts, True) + [pl.BlockSpec((ts, hw), rev), pl.BlockSpec((ts, hw), rev),
                                            pl.BlockSpec((ts, hw), mix), _vec(LANE)],
        out_specs=(pl.BlockSpec((ts, hw), rev), pl.BlockSpec((ts, _SWA_KW), rev), pl.BlockSpec((ts, _SWA_KW), rev),
                   _vec(LANE)),
        out_shape=(jax.ShapeDtypeStruct((S, hw), _ACT), jax.ShapeDtypeStruct((S, _SWA_KW), _ACT),
                   jax.ShapeDtypeStruct((S, _SWA_KW), _ACT), jax.ShapeDtypeStruct((1, LANE), F32)),
        scratch_shapes=[pltpu.VMEM((WINDOW, _SWA_KW), F32), pltpu.VMEM((WINDOW, _SWA_KW), F32)],
        compiler_params=_cp(1), name=name)(proj, proj, proj, proj, proj, o, lse, dycat, sinks)


def _exchange(arrays, *, scatter, name):
    n = len(arrays)

    def body(*refs):
        ins, outs = refs[:n], refs[n:2 * n]
        send_sems, recv_sems, loc_sems = refs[2 * n:]
        x, y, c = lax.axis_index("x"), lax.axis_index("y"), lax.axis_index("c")
        me = 4 * x + 2 * y + c

        def src(i, dest):
            return ins[i].at[dest] if scatter else ins[i]

        local = [pltpu.make_async_copy(src(i, me), outs[i].at[me], loc_sems.at[i]) for i in range(n)]
        for cp in local:
            cp.start()
        sends, recvs = [], []
        for k in range(1, NDEV):
            px = 1 - x if k & 4 else x
            py = 1 - y if k & 2 else y
            pc = 1 - c if k & 1 else c
            peer = 4 * px + 2 * py + pc
            for i in range(n):
                common = dict(send_sem=send_sems.at[i, k - 1], recv_sem=recv_sems.at[i, k - 1],
                              device_id=(px, py, pc), device_id_type=pl.DeviceIdType.MESH)
                sends.append(pltpu.make_async_remote_copy(src_ref=src(i, peer), dst_ref=outs[i].at[me], **common))
                recvs.append(pltpu.make_async_remote_copy(src_ref=src(i, peer), dst_ref=outs[i].at[peer], **common))
        for cp in sends:
            cp.start()
        for cp in recvs:
            cp.wait_recv()
        for cp in sends:
            cp.wait_send()
        for cp in local:
            cp.wait()

    hbm = pl.BlockSpec(memory_space=pl.ANY)
    out_shape = tuple(jax.ShapeDtypeStruct(a.shape if scatter else (NDEV,) + a.shape, a.dtype) for a in arrays)
    return pl.pallas_call(
        body, in_specs=[hbm] * n, out_specs=tuple([hbm] * n), out_shape=out_shape,
        scratch_shapes=[pltpu.SemaphoreType.DMA((n, NDEV - 1)), pltpu.SemaphoreType.DMA((n, NDEV - 1)),
                        pltpu.SemaphoreType.DMA((n,))],
        name=name)(*arrays)


def _adamw(w, m, v, parts, *, name):
    R, C = w.shape
    npart = parts.shape[0]
    cap = max(SUB, ((1 << 18) // C) // SUB * SUB)
    tr = _tile(R, cap, SUB)

    def body(w_ref, m_ref, v_ref, p_ref, g_ref, d_ref, mo_ref, vo_ref):
        g = p_ref[0]
        for k in range(1, npart):
            g = g + p_ref[k]
        mn = ADAM_B1 * m_ref[...] + (1.0 - ADAM_B1) * g
        vn = ADAM_B2 * v_ref[...] + (1.0 - ADAM_B2) * (g * g)
        m_hat = mn / (1.0 - ADAM_B1 ** ADAM_STEP)
        v_hat = vn / (1.0 - ADAM_B2 ** ADAM_STEP)
        g_ref[...] = g
        d_ref[...] = -ADAM_LR * (m_hat / (jnp.sqrt(v_hat) + ADAM_EPS) + ADAM_WD * w_ref[...])
        mo_ref[...] = mn
        vo_ref[...] = vn

    spec = pl.BlockSpec((tr, C), lambda i: (i, 0))
    oshape = jax.ShapeDtypeStruct((R, C), F32)
    return pl.pallas_call(
        body, grid=(R // tr,), in_specs=[spec] * 3 + [pl.BlockSpec((npart, tr, C), lambda i: (0, i, 0))],
        out_specs=(spec,) * 4, out_shape=(oshape,) * 4, compiler_params=_cp(1), name=name)(w, m, v, parts)


def _adamw_many(ws, ms, vs, landed, mine, me, *, name):
    n = len(ws)

    def body(me_ref, *refs):
        w_r, m_r, v_r, p_r, o_r = (refs[k * n:(k + 1) * n] for k in range(5))
        outs = refs[5 * n:]
        for i in range(n):
            own = o_r[i][...]
            g = jnp.where(me_ref[0] == 0, own, p_r[i][0])
            for k in range(1, NDEV):
                g = g + jnp.where(me_ref[0] == k, own, p_r[i][k])
            mn = ADAM_B1 * m_r[i][...] + (1.0 - ADAM_B1) * g
            vn = ADAM_B2 * v_r[i][...] + (1.0 - ADAM_B2) * (g * g)
            m_hat = mn / (1.0 - ADAM_B1 ** ADAM_STEP)
            v_hat = vn / (1.0 - ADAM_B2 ** ADAM_STEP)
            outs[4 * i][...] = g
            outs[4 * i + 1][...] = -ADAM_LR * (m_hat / (jnp.sqrt(v_hat) + ADAM_EPS) + ADAM_WD * w_r[i][...])
            outs[4 * i + 2][...] = mn
            outs[4 * i + 3][...] = vn

    vmem = pl.BlockSpec(memory_space=pltpu.VMEM)
    return pl.pallas_call(
        body, in_specs=[pl.BlockSpec(memory_space=pltpu.SMEM)] + [vmem] * (5 * n), out_specs=(vmem,) * (4 * n),
        out_shape=tuple(jax.ShapeDtypeStruct(w.shape, F32) for w in ws for _ in range(4)),
        name=name)(me, *ws, *ms, *vs, *landed, *mine)


def _adamw_layer(l, w, m, v, landed, mine, me, prev, *, name):
    L, R, C = w.shape
    npart = landed.shape[0]
    cap = max(SUB, ((1 << 18) // C) // SUB * SUB)
    tr = _tile(R, cap, SUB)
    nprev = 0 if prev is None else 4

    def body(me_ref, *refs):
        w_ref, m_ref, v_ref, p_ref, own_ref = refs[:5]
        g_ref, d_ref, mo_ref, vo_ref = refs[5 + nprev:]
        own = own_ref[...]
        g = jnp.where(me_ref[0] == 0, own, p_ref[0])
        for k in range(1, npart):
            g = g + jnp.where(me_ref[0] == k, own, p_ref[k])
        mn = ADAM_B1 * m_ref[...] + (1.0 - ADAM_B1) * g
        vn = ADAM_B2 * v_ref[...] + (1.0 - ADAM_B2) * (g * g)
        m_hat = mn / (1.0 - ADAM_B1 ** ADAM_STEP)
        v_hat = vn / (1.0 - ADAM_B2 ** ADAM_STEP)
        g_ref[...] = g
        d_ref[...] = -ADAM_LR * (m_hat / (jnp.sqrt(v_hat) + ADAM_EPS) + ADAM_WD * w_ref[...])
        mo_ref[...] = mn
        vo_ref[...] = vn

    spec = pl.BlockSpec((None, tr, C), lambda i, me_ref: (l, i, 0))
    oshape = jax.ShapeDtypeStruct((L, R, C), F32)
    return pl.pallas_call(
        body,
        grid_spec=pltpu.PrefetchScalarGridSpec(
            num_scalar_prefetch=1, grid=(R // tr,),
            in_specs=[spec] * 3 + [pl.BlockSpec((npart, tr, C), lambda i, me_ref: (0, i, 0)),
                                   pl.BlockSpec((None, tr, C), lambda i, me_ref: (me_ref[0], i, 0))]
            + [pl.BlockSpec(memory_space=pl.ANY)] * nprev,
            out_specs=(spec,) * 4),
        out_shape=(oshape,) * 4, input_output_aliases={6 + k: k for k in range(nprev)},
        compiler_params=_cp(1), name=name)(me, w, m, v, landed, mine, *(prev or ()))


_HBM = pl.BlockSpec(memory_space=pltpu.HBM)
_SEM = pl.BlockSpec(memory_space=pltpu.SEMAPHORE)
_EFFECT = pltpu.SideEffectType.DATAFLOW_SIDE_EFFECTING


def _peers():
    x, y, c = lax.axis_index("x"), lax.axis_index("y"), lax.axis_index("c")
    out = []
    for k in range(1, NDEV):
        px = 1 - x if k & 4 else x
        py = 1 - y if k & 2 else y
        pc = 1 - c if k & 1 else c
        out.append((k - 1, (px, py, pc), 4 * px + 2 * py + pc))
    return 4 * x + 2 * y + c, out


def _xchg_start(arrays, *, scatter, name):
    n = len(arrays)
    lands = [lax.empty(a.shape if scatter else (NDEV,) + a.shape, a.dtype) for a in arrays]

    def body(*refs):
        ins, lnd = refs[:n], refs[n:2 * n]
        send_sems, recv_sems = refs[2 * n], refs[2 * n + 1]
        token = refs[-1]
        me, peers = _peers()
        for k, dev, peer in peers:
            for i in range(n):
                pltpu.make_async_remote_copy(
                    src_ref=ins[i].at[peer] if scatter else ins[i], dst_ref=lnd[i].at[me],
                    send_sem=send_sems.at[i * (NDEV - 1) + k], recv_sem=recv_sems.at[i * (NDEV - 1) + k],
                    device_id=dev, device_id_type=pl.DeviceIdType.MESH).start()
        token[...] = jnp.zeros_like(token)

    sems = pltpu.SemaphoreType.DMA((n * (NDEV - 1),))
    res = pl.pallas_call(
        body, name=name,
        out_shape=(sems, sems) + tuple(pltpu.HBM(t.shape, t.dtype) for t in list(arrays) + lands)
        + (jax.ShapeDtypeStruct((SUB, LANE), F32),),
        in_specs=[_HBM] * (2 * n), out_specs=(_SEM, _SEM) + (_HBM,) * (2 * n) + (pl.BlockSpec(memory_space=pltpu.VMEM),),
        input_output_aliases={i: 2 + i for i in range(2 * n)},
        compiler_params=pltpu.CompilerParams(has_side_effects=_EFFECT),
    )(*[pltpu.with_memory_space_constraint(t, pltpu.HBM) for t in list(arrays) + lands])
    return dict(send=res[0], recv=res[1], thru=list(res[2:2 + 2 * n]), token=res[-1], scatter=scatter, n=n)


def _xchg_wait(handle, after, *, name):
    n, scatter = handle["n"], handle["scatter"]
    thru = handle["thru"]

    def body(*refs):
        ins, lnd = refs[:n], refs[n:2 * n]
        send_sems, recv_sems = refs[2 * n], refs[2 * n + 1]
        me, peers = _peers()
        for k, dev, peer in peers:
            for i in range(n):
                cp = pltpu.make_async_remote_copy(
                    src_ref=ins[i].at[peer] if scatter else ins[i], dst_ref=lnd[i].at[peer],
                    send_sem=send_sems.at[i * (NDEV - 1) + k], recv_sem=recv_sems.at[i * (NDEV - 1) + k],
                    device_id=dev, device_id_type=pl.DeviceIdType.MESH)
                cp.wait_send()
                cp.wait_recv()

    res = pl.pallas_call(
        body, name=name, out_shape=tuple(pltpu.HBM(t.shape, t.dtype) for t in thru),
        in_specs=[_HBM] * (2 * n) + [_SEM, _SEM, pl.BlockSpec(memory_space=pl.ANY)], out_specs=(_HBM,) * (2 * n),
        input_output_aliases={i: i for i in range(2 * n)},
        compiler_params=pltpu.CompilerParams(has_side_effects=_EFFECT),
    )(*thru, handle["send"], handle["recv"], after)
    return list(res[:n]), list(res[n:])


def _pad_heads(w, nh, hd, axis=-1):
    axis = axis % w.ndim
    shp = w.shape
    w = w.reshape(shp[:axis] + (nh, hd) + shp[axis + 1:])
    pads = [(0, 0)] * w.ndim
    pads[axis + 1] = (0, LANE - hd)
    return jnp.pad(w, pads).reshape(shp[:axis] + (nh * LANE,) + shp[axis + 1:])


def _unpad_heads(w, nh, hd, axis=-1):
    axis = axis % w.ndim
    shp = w.shape
    w = w.reshape(shp[:axis] + (nh, LANE) + shp[axis + 1:])
    w = lax.slice_in_dim(w, 0, hd, axis=axis + 1)
    return w.reshape(shp[:axis] + (nh * hd,) + shp[axis + 1:])


_O_DT = SSD_INNER + SSD_XBC
_O_CQ = _O_DT + SSD_HEADS
_O_CKV = _O_CQ + MLA_QR
_O_KR = _O_CKV + MLA_KVR
_O_SQ = _O_KR + MLA_ROPE
_O_SK = _O_SQ + SWA_HEADS * SWA_HD
_O_SV = _O_SK + SWA_KV * SWA_HD


def _w_in_to_padded(w, axis=-1):
    axis = axis % w.ndim
    cut = lambda a, b: lax.slice_in_dim(w, a, b, axis=axis)
    z, xbc, dt = cut(0, SSD_INNER), cut(SSD_INNER, _O_DT), cut(_O_DT, _O_CQ)
    cq, ckv, kr = cut(_O_CQ, _O_CKV), cut(_O_CKV, _O_KR), cut(_O_KR, _O_SQ)
    sq, sk, sv = cut(_O_SQ, _O_SK), cut(_O_SK, _O_SV), cut(_O_SV, D_IN)
    zeros = lambda n: jnp.zeros(w.shape[:axis] + (n,) + w.shape[axis + 1:], w.dtype)
    return jnp.concatenate([xbc, z, cq, ckv, dt, zeros(ROPE_LANE - SSD_HEADS), kr, zeros(LANE - ROPE_LANE - MLA_ROPE),
                            _pad_heads(sq, SWA_HEADS, SWA_HD, axis), _pad_heads(sk, SWA_KV, SWA_HD, axis),
                            _pad_heads(sv, SWA_KV, SWA_HD, axis)], axis=axis)


def _w_in_from_padded(g, axis=-1):
    axis = axis % g.ndim
    cut = lambda a, b: lax.slice_in_dim(g, a, b, axis=axis)
    xbc, z, cq, ckv = cut(C_XBC, C_Z), cut(C_Z, C_CQ), cut(C_CQ, C_CKV), cut(C_CKV, C_MISC)
    dt, kr = cut(C_MISC, C_MISC + SSD_HEADS), cut(C_MISC + ROPE_LANE, C_MISC + ROPE_LANE + MLA_ROPE)
    sq = _unpad_heads(cut(C_SQ, C_SK), SWA_HEADS, SWA_HD, axis)
    sk = _unpad_heads(cut(C_SK, C_SV), SWA_KV, SWA_HD, axis)
    sv = _unpad_heads(cut(C_SV, D_INP), SWA_KV, SWA_HD, axis)
    return jnp.concatenate([z, xbc, dt, cq, ckv, kr, sq, sk, sv], axis=axis)


def _w_out_to_padded(w):
    a = SSD_INNER
    b = a + MLA_HEADS * MLA_V
    return jnp.concatenate([w[..., :a, :], _pad_heads(w[..., a:b, :], MLA_HEADS, MLA_V, axis=-2),
                            _pad_heads(w[..., b:, :], SWA_HEADS, SWA_HD, axis=-2)], axis=-2)


def _w_out_from_padded(g):
    a = SSD_INNER
    b = a + MLA_HEADS * LANE
    return jnp.concatenate([g[..., :a, :], _unpad_heads(g[..., a:b, :], MLA_HEADS, MLA_V, axis=-2),
                            _unpad_heads(g[..., b:, :], SWA_HEADS, SWA_HD, axis=-2)], axis=-2)


def _w_ukv_to_padded(w):
    w4 = w.reshape(w.shape[:-1] + (MLA_HEADS, MLA_NOPE + MLA_V))
    flat = lambda t: t.reshape(w.shape[:-1] + (MLA_HEADS * t.shape[-1],))
    return jnp.concatenate([_pad_heads(flat(w4[..., :MLA_NOPE]), MLA_HEADS, MLA_NOPE),
                            _pad_heads(flat(w4[..., MLA_NOPE:]), MLA_HEADS, MLA_V)], axis=-1)


def _w_ukv_from_padded(g):
    hw = MLA_HEADS * LANE
    gk = _unpad_heads(g[..., :hw], MLA_HEADS, MLA_NOPE).reshape(g.shape[:-1] + (MLA_HEADS, MLA_NOPE))
    gv = _unpad_heads(g[..., hw:], MLA_HEADS, MLA_V).reshape(g.shape[:-1] + (MLA_HEADS, MLA_V))
    return jnp.concatenate([gk, gv], axis=-1).reshape(g.shape[:-1] + (MLA_HEADS * (MLA_NOPE + MLA_V),))


def _pad_lane(v):
    return jnp.pad(v, [(0, 0)] * (v.ndim - 1) + [(0, LANE - v.shape[-1])])


def _rope_tables(positions):
    inv_freq = ROPE_THETA ** (-jnp.arange(0, MLA_ROPE, 2, dtype=F32) / MLA_ROPE)
    ang = positions.astype(F32).reshape(-1, 1) * inv_freq
    cos, sin = jnp.cos(ang), jnp.sin(ang)
    S = ang.shape[0]
    one, zero = jnp.ones((S, ROPE_LANE), F32), jnp.zeros((S, ROPE_LANE), F32)
    tail1, tail0 = jnp.ones((S, LANE - ROPE_LANE - MLA_ROPE), F32), jnp.zeros((S, LANE - ROPE_LANE - MLA_ROPE), F32)
    z16 = jnp.zeros_like(sin)
    return (jnp.concatenate([one, cos, cos, tail1], axis=1), jnp.concatenate([zero, -sin, z16, tail0], axis=1),
            jnp.concatenate([zero, z16, sin, tail0], axis=1))


def _layer_fwd(l, x_in, f_prev, gate_prev, mod, P, tabs):
    sh1, sc1, g1, sh2, sc2, g2 = [mod[k:k + 1] for k in range(6)]
    tag = f"l{l}_"
    if f_prev is None:
        x0 = x_in
        h1 = _norm_fwd(x0, P["n1g"], sc1, sh1, name=tag + "norm1")
    else:
        x0, h1 = _norm_fwd(x_in, P["n1g"], sc1, sh1, f=f_prev, gate=gate_prev, name=tag + "norm1")
    proj = _mm(h1, P["w_in"], tb=True, name=tag + "proj")
    P.update(P.pop("mid")(proj))
    y_ssd, hin = _ssd_fwd(proj, P["ssd_cw"], P["ssd_cb"], P["dtb"], P["alog"], P["dsk"],
                          P["ssd_ng"], name=tag + "ssd")
    q, k, v, vt = _mla_prep_fwd(proj, *tabs, P["gq"], P["gkv"], P["w_uq"], P["w_ukv"], name=tag + "mla_prep")
    o_mla, lse_mla, lset_mla = _attn_fwd(q, k, vt, name=tag + "mla_attn")
    o_swa, lse_swa = _swa_fwd(proj, P["sinks"], name=tag + "swa")
    ycat = jnp.concatenate([y_ssd.astype(_ACT), o_mla.astype(_ACT), o_swa.astype(_ACT)], axis=1)
    y = _mm(ycat, P["w_out"], name=tag + "out")
    P.update(P.pop("late")(y))
    x1, h2 = _norm_fwd(x0, P["n2g"], sc2, sh2, f=y, gate=g1, name=tag + "norm2")
    up = _mm(h2, P["w_up"], tb=True, name=tag + "up")
    act = _ffn_act_fwd(up, P["fcw"], P["fcb"], name=tag + "ffn_act")
    f = _mm(act, P["w_down"], name=tag + "down")
    saved = dict(x0=x0, h1=h1, proj=proj, hin=hin, q=q, k=k, v=v, o_mla=o_mla, lse_mla=lse_mla, lset_mla=lset_mla, o_swa=o_swa,
                 lse_swa=lse_swa, ycat=ycat, y=y, x1=x1, h2=h2, up=up, act=act, f=f, mod=mod)
    return x1, f, g2, saved


def _layer_bwd(l, dxo, sv, P, tabs, on_part):
    mod = sv["mod"]
    sh1, sc1, g1, sh2, sc2, g2 = [mod[k:k + 1] for k in range(6)]
    tag = f"l{l}_b_"
    G = {}
    df, dg2 = _gate_bwd(dxo, sv["f"], g2, name=tag + "gate2")
    dact = _mm(df, P["w_down"], tb=True, name=tag + "dact")
    G["w_down"] = _mm(sv["act"], df, ta=True, name=tag + "dw_down")
    dup, G["fcw"], G["fcb"] = _ffn_bwd(sv["up"], dact, P["fcw"], P["fcb"], name=tag + "ffn")
    dh2 = _mm(dup, P["w_up"], name=tag + "dh2")
    G["w_up"] = _mm(dup, sv["h2"], ta=True, name=tag + "dw_up")
    token = on_part(l, "ffn", G)
    if token is not None:
        sc2 = sc2 + token
    dx1, G["n2g"], dsc2, dsh2 = _norm_bwd(dh2, sv["x1"], dxo, P["n2g"], sc2, name=tag + "norm2")
    dy, dg1 = _gate_bwd(dx1, sv["y"], g1, name=tag + "gate1")
    dycat = _mm(dy, P["w_out"], tb=True, name=tag + "dycat")
    G["w_out"] = _mm(sv["ycat"], dy, ta=True, name=tag + "dw_out")
    token = on_part(l, "out", G)
    ssd_cb = P["ssd_cb"] if token is None else P["ssd_cb"] + token
    proj = sv["proj"]
    (dpre, dz, dmisc_ssd, G["ssd_cw"], G["ssd_cb"], G["dtb"], G["alog"], G["dsk"], G["ssd_ng"]) = _ssd_bwd(
        proj, dycat, sv["hin"], P["ssd_cw"], ssd_cb, P["dtb"], P["alog"], P["dsk"],
        P["ssd_ng"], name=tag + "ssd")
    att = (sv["q"], sv["k"], sv["v"], sv["o_mla"])
    dq = _attn_bwd_dq(*att, sv["lse_mla"], dycat, name=tag + "mla_dq")
    dk, dv = _attn_bwd_dkv(*att, sv["lset_mla"], dycat, name=tag + "mla_dkv")
    dcq, dckv, dmisc, dqh, dkv, qn, kvn, G["gq"], G["gkv"] = _mla_prep_bwd(
        proj, dq, dk, dv, dmisc_ssd, *tabs, P["gq"], P["gkv"], P["w_uq"], P["w_ukv"], name=tag + "mla_prep")
    G["w_uq"] = _mm(qn, dqh, ta=True, name=tag + "dw_uq")
    G["w_ukv"] = _mm(kvn, dkv, ta=True, name=tag + "dw_ukv")
    dsq, dsk_, dsv_, G["sinks"] = _swa_bwd(proj, sv["o_swa"], sv["lse_swa"], dycat, P["sinks"], name=tag + "swa")
    dproj = jnp.concatenate([dpre, dz, dcq, dckv, dmisc, dsq, dsk_, dsv_], axis=1)
    G["w_in"] = _mm(dproj, sv["h1"], ta=True, name=tag + "dw_in")
    token = on_part(l, "mixer", G)
    if token is not None:
        sc1 = sc1 + token
    dh1 = _mm(dproj, P["w_in"], name=tag + "dh1")
    dx0, G["n1g"], dsc1, dsh1 = _norm_bwd(dh1, sv["x0"], dx1, P["n1g"], sc1, name=tag + "norm1")
    G["mod"] = jnp.concatenate([dsh1, dsc1, dg1, dsh2, dsc2, dg2], axis=0)
    return dx0, G


def _local_step(x, tgt, mods, get_params, tabs, final_g, on_grads, on_part):
    saved, params = [], []
    xin, f, gate = x, None, None
    for l in range(DEPTH):
        params.append(get_params(l, x if f is None else f))
        xin, f, gate, sv = _layer_fwd(l, xin, f, gate, mods[l], params[l], tabs)
        saved.append(sv)
    loss, dx, dfinal = _final_loss(xin, f, gate, final_g, tgt, name="final_loss")
    for l in reversed(range(DEPTH)):
        dx, G = _layer_bwd(l, dx, saved[l], params[l], tabs, on_part)
        on_grads(l, G)
    return loss[0, 0], dx, dfinal


_WEIGHTS = ['ada_w', 'ada_b', 'norm1_g', 'norm2_g', 'w_in', 'ssd_conv_w', 'ssd_conv_b', 'ssd_dt_bias', 'ssd_a_log',
            'ssd_d', 'ssd_norm_g', 'mla_q_norm_g', 'mla_w_uq', 'mla_kv_norm_g', 'mla_w_ukv', 'swa_sinks', 'w_out',
            'ffn_w_up', 'ffn_conv_w', 'ffn_conv_b', 'ffn_w_down', 'final_norm_g']
_INPUTS = ['x', 'c', 'positions'] + _WEIGHTS + ['loss_target'] + ['m_' + n for n in _WEIGHTS] + ['v_' + n for n in _WEIGHTS]
_SMALL = [('ada_b', 'mod'), ('norm1_g', 'n1g'), ('norm2_g', 'n2g'), ('ssd_conv_b', 'ssd_cb'), ('ssd_dt_bias', 'dtb'),
          ('ssd_a_log', 'alog'), ('ssd_d', 'dsk'), ('ssd_norm_g', 'ssd_ng'), ('mla_q_norm_g', 'gq'),
          ('mla_kv_norm_g', 'gkv'), ('swa_sinks', 'sinks'), ('ffn_conv_b', 'fcb')]
_SHARDED = [('w_in', 'w_in', 2), ('ssd_conv_w', 'ssd_cw', 2), ('mla_w_uq', 'w_uq', 2), ('mla_w_ukv', 'w_ukv', 2),
            ('w_out', 'w_out', 1), ('ffn_w_up', 'w_up', 2), ('ffn_conv_w', 'fcw', 2), ('ffn_w_down', 'w_down', 1)]
_SHARDED_NAMES = [n for n, _, _ in _SHARDED]
_TRANSPOSED = ('w_in', 'ffn_w_up')


def _shard_major(g, axis):
    shp = g.shape
    g = g.reshape(shp[:axis] + (NDEV, shp[axis] // NDEV) + shp[axis + 1:])
    return jnp.moveaxis(g, axis, 0)


def _unshard(g, axis):
    g = jnp.moveaxis(g, 0, axis)
    shp = g.shape
    return g.reshape(shp[:axis] + (shp[axis] * shp[axis + 1],) + shp[axis + 2:])


def kernel(x, c, positions, ada_w, ada_b, norm1_g, norm2_g, w_in, ssd_conv_w, ssd_conv_b, ssd_dt_bias, ssd_a_log, ssd_d, ssd_norm_g, mla_q_norm_g, mla_w_uq, mla_kv_norm_g, mla_w_ukv, swa_sinks, w_out, ffn_w_up, ffn_conv_w, ffn_conv_b, ffn_w_down, final_norm_g, loss_target, m_ada_w, m_ada_b, m_norm1_g, m_norm2_g, m_w_in, m_ssd_conv_w, m_ssd_conv_b, m_ssd_dt_bias, m_ssd_a_log, m_ssd_d, m_ssd_norm_g, m_mla_q_norm_g, m_mla_w_uq, m_mla_kv_norm_g, m_mla_w_ukv, m_swa_sinks, m_w_out, m_ffn_w_up, m_ffn_conv_w, m_ffn_conv_b, m_ffn_w_down, m_final_norm_g, v_ada_w, v_ada_b, v_norm1_g, v_norm2_g, v_w_in, v_ssd_conv_w, v_ssd_conv_b, v_ssd_dt_bias, v_ssd_a_log, v_ssd_d, v_ssd_norm_g, v_mla_q_norm_g, v_mla_w_uq, v_mla_kv_norm_g, v_mla_w_ukv, v_swa_sinks, v_w_out, v_ffn_w_up, v_ffn_conv_w, v_ffn_conv_b, v_ffn_w_down, v_final_norm_g):
    a = dict(zip(_INPUTS, (x, c, positions, ada_w, ada_b, norm1_g, norm2_g, w_in, ssd_conv_w, ssd_conv_b, ssd_dt_bias, ssd_a_log, ssd_d, ssd_norm_g, mla_q_norm_g, mla_w_uq, mla_kv_norm_g, mla_w_ukv, swa_sinks, w_out, ffn_w_up, ffn_conv_w, ffn_conv_b, ffn_w_down, final_norm_g, loss_target, m_ada_w, m_ada_b, m_norm1_g, m_norm2_g, m_w_in, m_ssd_conv_w, m_ssd_conv_b, m_ssd_dt_bias, m_ssd_a_log, m_ssd_d, m_ssd_norm_g, m_mla_q_norm_g, m_mla_w_uq, m_mla_kv_norm_g, m_mla_w_ukv, m_swa_sinks, m_w_out, m_ffn_w_up, m_ffn_conv_w, m_ffn_conv_b, m_ffn_w_down, m_final_norm_g, v_ada_w, v_ada_b, v_norm1_g, v_norm2_g, v_w_in, v_ssd_conv_w, v_ssd_conv_b, v_ssd_dt_bias, v_ssd_a_log, v_ssd_d, v_ssd_norm_g, v_mla_q_norm_g, v_mla_w_uq, v_mla_kv_norm_g, v_mla_w_ukv, v_swa_sinks, v_w_out, v_ffn_w_up, v_ffn_conv_w, v_ffn_conv_b, v_ffn_w_down, v_final_norm_g)))
    axes = ("x", "y", "c")
    me = 4 * lax.axis_index("x") + 2 * lax.axis_index("y") + lax.axis_index("c")
    ncol = ada_w.shape[-1]

    kform = lambda n, t: jnp.swapaxes(t, -1, -2) if n in _TRANSPOSED else t
    mxu_names = ('w_in', 'mla_w_uq', 'mla_w_ukv', 'w_out', 'ffn_w_up', 'ffn_w_down')
    gather_groups = (("early", _SHARDED_NAMES[:4]), ("mid", _SHARDED_NAMES[4:5]), ("late", _SHARDED_NAMES[5:]))

    def own_of(src, names, l):
        return [kform(n, src[n][l]).astype(_MXU) if n in mxu_names else src[n][l] for n in names]

    first_gather = _xchg_start(own_of(a, gather_groups[0][1], 0), scatter=False, name="gather_start_early0")

    c_all = _exchange([c + first_gather["token"][0, 0]], scatter=False, name="gather_c")[0]
    c_act = _silu_call(c_all.reshape(NDEV, D), name="c_act")
    mod_part = jnp.stack([_mm(c_act, ada_w[l], name=f"mod{l}") for l in range(DEPTH)])
    mod_all = _exchange([mod_part], scatter=False, name="gather_mod")[0]
    mod_mine = lax.dynamic_index_in_dim(mod_all, me, axis=2, keepdims=False)
    mods = (jnp.moveaxis(mod_mine, 0, 1).reshape(DEPTH, 6 * D) + ada_b).reshape(DEPTH, 6, D)
    tabs = _rope_tables(positions)

    shard_of = {n: (key, 1 if n in _TRANSPOSED else ax) for n, key, ax in _SHARDED}
    mods, raw = lax.optimization_barrier((mods, {n: a[n] for n in _SHARDED_NAMES}))
    gathers, prev = [], first_gather["token"]
    for l in range(DEPTH):
        gathers.append({})
        for grp, names in gather_groups:
            if (l, grp) == (0, "early"):
                gathers[l][grp] = first_gather
                continue
            srcs, _ = lax.optimization_barrier((own_of(raw, names, l), prev))
            gathers[l][grp] = _xchg_start(srcs, scatter=False, name=f"gather_start_{grp}{l}")
            prev = gathers[l][grp]["token"]

    def place_own(landed, mine):
        return [lax.dynamic_update_index_in_dim(t, o, me, 0) for t, o in zip(landed, mine)]

    def gathered(l, grp, after):
        names = dict(gather_groups)[grp]
        mine, landed = _xchg_wait(gathers[l][grp], after, name=f"gather_wait_{grp}{l}")
        return {n: _unshard(g, shard_of[n][1] - 1) for n, g in zip(names, place_own(landed, mine))}

    def get_params(l, after):
        full = gathered(l, "early", mods if l == 0 else after)
        vec = lambda t: t[l].reshape(1, -1)

        def mid(after2):
            return dict(w_out=_w_out_to_padded(gathered(l, "mid", after2)['w_out']))

        def late(after2):
            rest = gathered(l, "late", after2)
            return dict(w_up=rest['ffn_w_up'], w_down=rest['ffn_w_down'], fcw=rest['ffn_conv_w'])

        return dict(
            w_in=_w_in_to_padded(full['w_in'], axis=0), w_uq=_pad_heads(full['mla_w_uq'], MLA_HEADS, MLA_NOPE + MLA_ROPE),
            w_ukv=_w_ukv_to_padded(full['mla_w_ukv']), ssd_cw=full['ssd_conv_w'], mid=mid, late=late,
            ssd_cb=vec(ssd_conv_b), dtb=vec(_pad_lane(ssd_dt_bias)), alog=vec(_pad_lane(ssd_a_log)),
            dsk=vec(_pad_lane(ssd_d)), ssd_ng=vec(ssd_norm_g), gq=vec(mla_q_norm_g), gkv=vec(mla_kv_norm_g),
            sinks=vec(_pad_lane(swa_sinks)), fcb=vec(ffn_conv_b), n1g=vec(norm1_g), n2g=vec(norm2_g))

    unpad = dict(w_in=functools.partial(_w_in_from_padded, axis=0), w_out=_w_out_from_padded, w_ukv=_w_ukv_from_padded,
                 w_uq=lambda g: _unpad_heads(g, MLA_HEADS, MLA_NOPE + MLA_ROPE))
    scatter_groups = (("ffn", _SHARDED_NAMES[5:]), ("out", _SHARDED_NAMES[4:5]), ("mixer", _SHARDED_NAMES[:4]))
    grads = [None] * DEPTH
    scatters = [dict() for _ in range(DEPTH)]

    def on_part(l, grp, G):
        parts = [_shard_major(unpad.get(shard_of[n][0], lambda g: g)(G[shard_of[n][0]]), shard_of[n][1] - 1)
                 for n in dict(scatter_groups)[grp]]
        scatters[l][grp] = _xchg_start(parts, scatter=True, name=f"scatter_start_{grp}{l}")
        return scatters[l][grp]["token"][0, 0]

    def on_grads(l, G):
        grads[l] = G

    mods = mods + sum(g[grp]["token"][0, 0] for g in gathers for grp, _ in gather_groups)
    loss, dx, dfinal = _local_step(x[0], loss_target[0], mods, get_params, tabs, final_norm_g.reshape(1, D),
                                   on_grads, on_part)
    loss = lax.psum(loss, axes)

    stack = lambda key: jnp.stack([grads[l][key] for l in range(DEPTH)])
    small_names = [n for n, _ in _SMALL] + ['final_norm_g']
    small_g = [stack(key).reshape(DEPTH, -1)[:, :a[name].shape[1]] for name, key in _SMALL] + [dfinal]
    small_gather = _xchg_start(small_g, scatter=False, name="gather_small_start")

    out_g, out_d, out_m, out_v = {}, {}, {}, {}
    chain = {name: None for name in _SHARDED_NAMES}
    me_arr = jnp.reshape(me, (1,)).astype(jnp.int32)
    after = small_gather["token"]
    for l in reversed(range(DEPTH)):
        for grp, names in scatter_groups:
            mine, landed = _xchg_wait(scatters[l][grp], after, name=f"scatter_wait_{grp}{l}")
            for name, own, got in zip(names, mine, landed):
                chain[name] = _adamw_layer(l, kform(name, a[name]), kform(name, a['m_' + name]),
                                           kform(name, a['v_' + name]), got, own, me_arr, chain[name],
                                           name=f"adamw_{name}{l}")
    for name in _SHARDED_NAMES:
        out_g[name], out_d[name], out_m[name], out_v[name] = [kform(name, t) for t in chain[name]]
    small_mine, small_landed = _xchg_wait(small_gather, chain[_SHARDED_NAMES[0]][0], name="gather_small_wait")
    row = lambda t: t.reshape(1, -1) if t.ndim == 1 else t
    res = _adamw_many([row(a[n]) for n in small_names], [row(a['m_' + n]) for n in small_names],
                      [row(a['v_' + n]) for n in small_names], small_landed, small_mine, me_arr, name="adamw_small")
    for i, n in enumerate(small_names):
        out_g[n], out_d[n], out_m[n], out_v[n] = [t.reshape(a[n].shape) for t in res[4 * i:4 * i + 4]]

    dmod_all = place_own(small_landed[:1], small_mine[:1])[0]
    dmod_mine = lax.dynamic_slice_in_dim(dmod_all, me * ncol, ncol, axis=2)
    g_ada = jnp.stack([_mm(c_act, dmod_mine[:, l], ta=True, name=f"dw_ada{l}") for l in range(DEPTH)])
    shp = ada_w.shape
    res = _adamw(*[t.reshape(-1, shp[-1]) for t in (ada_w, m_ada_w, v_ada_w)], g_ada.reshape(1, -1, shp[-1]),
                 name="adamw_ada_w")
    out_g['ada_w'], out_d['ada_w'], out_m['ada_w'], out_v['ada_w'] = [t.reshape(shp) for t in res]

    outs = [loss, dx[None]]
    for dct in (out_g, out_d, out_m, out_v):
        outs += [dct[n] for n in _WEIGHTS]
    return tuple(outs)
```

```python
import functools
import math

import jax
import jax.numpy as jnp
from jax import lax
from jax.experimental import pallas as pl
from jax.experimental.pallas import tpu as pltpu

F32 = jnp.float32
_MXU = jnp.bfloat16
_ACT = jnp.bfloat16
_HI = lax.Precision.HIGHEST
EPS = 1e-6
NDEV = 8
DEPTH = 4
D = 1024
LANE = 128
SUB = 8
VMEM_LIMIT = 56 * 1024 * 1024

SSD_INNER, SSD_STATE, SSD_HEADS, SSD_GROUPS, SSD_CHUNK, SSD_CONV = 512, 128, 8, 2, 128, 4
SSD_XBC = SSD_INNER + 2 * SSD_GROUPS * SSD_STATE
MLA_HEADS, MLA_NOPE, MLA_ROPE, MLA_V, MLA_QR, MLA_KVR = 4, 64, 32, 64, 256, 128
SWA_HEADS, SWA_KV, SWA_HD, WINDOW = 4, 2, 64, 128
D_FF, FFN_CONV = 2816, 3
D_IN = 2472
ROPE_THETA = 10000.0
C_XBC, C_Z, C_CQ, C_CKV, C_MISC, C_SQ, C_SK, C_SV, D_INP = 0, 1024, 1536, 1792, 1920, 2048, 2560, 2816, 3072
ROPE_LANE = 64
D_MIXP = 1536

ADAM_LR, ADAM_B1, ADAM_B2, ADAM_EPS, ADAM_WD, ADAM_STEP = 0.001, 0.9, 0.999, 1e-08, 0.01, 10

TS_ROW = 1024
TS_FFN = 256
TQ_ATT = 1024
TS_SWA = 512


def _tile(n, cap, q=LANE):
    best = None
    for t in range(q, min(n, cap) + 1, q):
        if n % t == 0:
            best = t
    return n if best is None else best


def _cp(ngrid):
    return pltpu.CompilerParams(dimension_semantics=("arbitrary",) * ngrid, vmem_limit_bytes=VMEM_LIMIT)


def _dot(a, b):
    return jnp.dot(a.astype(_MXU), b.astype(_MXU), preferred_element_type=F32)


def _dot_nt(a, b):
    return lax.dot_general(a.astype(_MXU), b.astype(_MXU), (((1,), (1,)), ((), ())), preferred_element_type=F32)


def _dot_tn(a, b):
    return jnp.dot(a.T.astype(_MXU), b.astype(_MXU), preferred_element_type=F32)


def _sigmoid(x):
    return 1.0 / (1.0 + jnp.exp(-x))


def _sigmoid_t(x):
    return 0.5 * jnp.tanh(0.5 * x) + 0.5


def _silu(x):
    return x * _sigmoid_t(x)


def _silu_grad(x):
    s = _sigmoid_t(x)
    return x * s, s * (1.0 + x * (1.0 - s))


def _dsilu(x):
    return _silu_grad(x)[1]


def _softplus(x):
    u = jnp.exp(-jnp.abs(x))
    w = 1.0 + u
    log1p = jnp.where(w == 1.0, u, jnp.log(w) * u / jnp.where(w == 1.0, 1.0, w - 1.0))
    return jnp.maximum(x, 0.0) + log1p


def _colsum(x):
    return jnp.sum(x, axis=0, keepdims=True)


def _rowsum(x):
    return jnp.sum(x, axis=1, keepdims=True)


def _shift_down(t, halo, j):
    if j == 0:
        return t
    n = t.shape[0]
    rolled = pltpu.roll(t, j, 0)
    row = lax.broadcasted_iota(jnp.int32, (SUB, t.shape[1]), 0)
    first = jnp.where(row < j, pltpu.roll(halo, j, 0), rolled[:SUB])
    return jnp.concatenate([first, rolled[SUB:]], axis=0) if n > SUB else first


def _shift_up(t, halo, j):
    if j == 0:
        return t
    n = t.shape[0]
    rolled = pltpu.roll(t, n - j, 0)
    row = lax.broadcasted_iota(jnp.int32, (SUB, t.shape[1]), 0)
    last = jnp.where(row >= SUB - j, pltpu.roll(halo, SUB - j, 0), rolled[n - SUB:])
    return jnp.concatenate([rolled[:n - SUB], last], axis=0) if n > SUB else last


def _mm(a, b, *, ta=False, tb=False, out_dtype=F32, name):
    if ta:
        K, M = a.shape
    else:
        M, K = a.shape
    if tb:
        N, K2 = b.shape
    else:
        K2, N = b.shape
    assert K == K2, (a.shape, b.shape, ta, tb)
    tk = _tile(K, 1536)
    nk = K // tk
    tm, tn = _tile(M, 2048 if nk == 1 else 1536), _tile(N, 1536 if nk == 1 else 1408)
    dn = (((0 if ta else 1,), (1 if tb else 0,)), ((), ()))

    def body(a_ref, b_ref, o_ref, *acc):
        part = lax.dot_general(a_ref[...].astype(_MXU), b_ref[...].astype(_MXU), dn, preferred_element_type=F32)
        if nk == 1:
            o_ref[...] = part.astype(out_dtype)
            return
        acc_ref, = acc
        k = pl.program_id(2)

        @pl.when(k == 0)
        def _():
            acc_ref[...] = part

        @pl.when(k > 0)
        def _():
            acc_ref[...] += part

        @pl.when(k == nk - 1)
        def _():
            o_ref[...] = acc_ref[...].astype(out_dtype)

    a_spec = pl.BlockSpec((tk, tm), lambda i, j, k: (k, i)) if ta else pl.BlockSpec((tm, tk), lambda i, j, k: (i, k))
    b_spec = pl.BlockSpec((tn, tk), lambda i, j, k: (j, k)) if tb else pl.BlockSpec((tk, tn), lambda i, j, k: (k, j))
    return pl.pallas_call(
        body, grid=(M // tm, N // tn, nk), in_specs=[a_spec, b_spec],
        out_specs=pl.BlockSpec((tm, tn), lambda i, j, k: (i, j)),
        out_shape=jax.ShapeDtypeStruct((M, N), out_dtype),
        scratch_shapes=[pltpu.VMEM((tm, tn), F32)] * (nk > 1), compiler_params=_cp(3), name=name)(a, b)


def _row(ts, w, col=0):
    return pl.BlockSpec((ts, w), lambda i: (i, col))


def _vec(w, r=1):
    return pl.BlockSpec((r, w), lambda i: (0, 0))


def _silu_call(x, name):
    def body(x_ref, o_ref):
        o_ref[...] = _silu(x_ref[...])
    return pl.pallas_call(body, out_shape=jax.ShapeDtypeStruct(x.shape, F32), name=name)(x)


def _norm_fwd(x, g, sc, sh, *, f=None, gate=None, name):
    S, dm = x.shape
    ts = _tile(S, TS_ROW, SUB)
    res = f is not None

    def body(*refs):
        if res:
            x_ref, f_ref, gate_ref, g_ref, sc_ref, sh_ref, xo_ref, h_ref = refs
            xv = x_ref[...] + gate_ref[...] * f_ref[...]
            xo_ref[...] = xv
        else:
            x_ref, g_ref, sc_ref, sh_ref, h_ref = refs
            xv = x_ref[...]
        rstd = lax.rsqrt(jnp.mean(xv * xv, axis=-1, keepdims=True) + EPS)
        h_ref[...] = ((xv * rstd) * g_ref[...] * (1.0 + sc_ref[...]) + sh_ref[...]).astype(_ACT)

    ins = [x] + ([f, gate] if res else []) + [g, sc, sh]
    in_specs = [_row(ts, dm)] + ([_row(ts, dm), _vec(dm)] if res else []) + [_vec(dm)] * 3
    h_shape = jax.ShapeDtypeStruct((S, dm), _ACT)
    if res:
        out_shape, out_specs = (jax.ShapeDtypeStruct((S, dm), F32), h_shape), (_row(ts, dm), _row(ts, dm))
    else:
        out_shape, out_specs = h_shape, _row(ts, dm)
    return pl.pallas_call(body, grid=(S // ts,), in_specs=in_specs, out_specs=out_specs, out_shape=out_shape,
                          compiler_params=_cp(1), name=name)(*ins)


def _norm_bwd(dh, x, dres, g, sc, *, name):
    S, dm = x.shape
    ts = _tile(S, TS_ROW, SUB)

    def body(dh_ref, x_ref, dres_ref, g_ref, sc_ref, dx_ref, dg_ref, dsc_ref, dsh_ref):
        i = pl.program_id(0)
        xv = x_ref[...]
        dhv = dh_ref[...]
        rstd = lax.rsqrt(jnp.mean(xv * xv, axis=-1, keepdims=True) + EPS)
        xhat = xv * rstd
        hn = xhat * g_ref[...]
        dhn = dhv * (1.0 + sc_ref[...])
        dxh = dhn * g_ref[...]
        dx_ref[...] = dres_ref[...] + rstd * (dxh - xhat * jnp.mean(dxh * xhat, axis=-1, keepdims=True))

        @pl.when(i == 0)
        def _():
            dg_ref[...] = jnp.zeros_like(dg_ref)
            dsc_ref[...] = jnp.zeros_like(dsc_ref)
            dsh_ref[...] = jnp.zeros_like(dsh_ref)

        dg_ref[...] += _colsum(dhn * xhat)
        dsc_ref[...] += _colsum(dhv * hn)
        dsh_ref[...] += _colsum(dhv)

    vshape = jax.ShapeDtypeStruct((1, dm), F32)
    return pl.pallas_call(
        body, grid=(S // ts,), in_specs=[_row(ts, dm)] * 3 + [_vec(dm)] * 2,
        out_specs=(_row(ts, dm), _vec(dm), _vec(dm), _vec(dm)),
        out_shape=(jax.ShapeDtypeStruct((S, dm), F32), vshape, vshape, vshape),
        compiler_params=_cp(1), name=name)(dh, x, dres, g, sc)


def _gate_bwd(dxo, f, gate, *, name):
    S, dm = f.shape
    ts = _tile(S, TS_ROW, SUB)

    def body(dxo_ref, f_ref, gate_ref, df_ref, dgate_ref):
        i = pl.program_id(0)
        dv = dxo_ref[...]
        df_ref[...] = (gate_ref[...] * dv).astype(_ACT)

        @pl.when(i == 0)
        def _():
            dgate_ref[...] = jnp.zeros_like(dgate_ref)

        dgate_ref[...] += _colsum(dv * f_ref[...])

    return pl.pallas_call(
        body, grid=(S // ts,), in_specs=[_row(ts, dm), _row(ts, dm), _vec(dm)],
        out_specs=(_row(ts, dm), _vec(dm)),
        out_shape=(jax.ShapeDtypeStruct((S, dm), _ACT), jax.ShapeDtypeStruct((1, dm), F32)),
        compiler_params=_cp(1), name=name)(dxo, f, gate)


def _final_loss(x, f, gate, g, tgt, *, name):
    S, dm = x.shape
    ts = _tile(S, TS_ROW, SUB)

    def body(x_ref, f_ref, gate_ref, g_ref, t_ref, loss_ref, dx_ref, dg_ref):
        i = pl.program_id(0)
        xv = x_ref[...] + gate_ref[...] * f_ref[...]
        rstd = lax.rsqrt(jnp.mean(xv * xv, axis=-1, keepdims=True) + EPS)
        xhat = xv * rstd
        err = xhat * g_ref[...] - t_ref[...]
        dy = err * (1.0 / dm)
        dxh = dy * g_ref[...]
        dx_ref[...] = rstd * (dxh - xhat * jnp.mean(dxh * xhat, axis=-1, keepdims=True))

        @pl.when(i == 0)
        def _():
            loss_ref[...] = jnp.zeros_like(loss_ref)
            dg_ref[...] = jnp.zeros_like(dg_ref)

        loss_ref[...] += jnp.full((1, LANE), 0.5 * jnp.sum(jnp.mean(err * err, axis=-1, keepdims=True)), F32)
        dg_ref[...] += _colsum(dy * xhat)

    return pl.pallas_call(
        body, grid=(S // ts,), in_specs=[_row(ts, dm), _row(ts, dm), _vec(dm), _vec(dm), _row(ts, dm)],
        out_specs=(_vec(LANE), _row(ts, dm), _vec(dm)),
        out_shape=(jax.ShapeDtypeStruct((1, LANE), F32), jax.ShapeDtypeStruct((S, dm), F32),
                   jax.ShapeDtypeStruct((1, dm), F32)),
        compiler_params=_cp(1), name=name)(x, f, gate, g, tgt)


def _ffn_conv(t, halo, cw_ref, cb_ref):
    t1, t2 = _shift_down(t, halo, 1), _shift_down(t, halo, 2)
    return ((cb_ref[...] + t2 * cw_ref[0:1, :]) + t1 * cw_ref[1:2, :]) + t * cw_ref[2:3, :], t1, t2


def _prev_halo_spec(ts, w, col=0):
    return pl.BlockSpec((SUB, w), lambda i: (jnp.maximum(i * (ts // SUB) - 1, 0), col))


def _ffn_act_fwd(up, cw, cb, *, name):
    S, w2 = up.shape
    ff = w2 // 2
    ts = _tile(S, TS_FFN, SUB)

    def body(up_ref, halo_ref, cw_ref, cb_ref, act_ref):
        i = pl.program_id(0)
        t = up_ref[...]
        halo = jnp.where(i > 0, halo_ref[...], 0.0)
        u, _, _ = _ffn_conv(t, halo, cw_ref, cb_ref)
        act_ref[...] = (_silu(u[:, :ff]) * u[:, ff:]).astype(_ACT)

    return pl.pallas_call(
        body, grid=(S // ts,), in_specs=[_row(ts, w2), _prev_halo_spec(ts, w2), _vec(w2, FFN_CONV), _vec(w2)],
        out_specs=_row(ts, ff), out_shape=jax.ShapeDtypeStruct((S, ff), _ACT),
        compiler_params=_cp(1), name=name)(up, up, cw, cb)


def _ffn_bwd(up, dact, cw, cb, *, name):
    S, w2 = up.shape
    ff = w2 // 2
    ts = _tile(S, TS_FFN, SUB)
    n = S // ts

    def body(up_ref, halo_ref, dact_ref, cw_ref, cb_ref, dup_ref, dcw_ref, dcb_ref, carry_ref):
        i = pl.program_id(0)
        t_idx = n - 1 - i

        @pl.when(i == 0)
        def _():
            carry_ref[...] = jnp.zeros_like(carry_ref)
            dcw_ref[...] = jnp.zeros_like(dcw_ref)
            dcb_ref[...] = jnp.zeros_like(dcb_ref)

        t = up_ref[...]
        halo = jnp.where(t_idx > 0, halo_ref[...], 0.0)
        u, t1, t2 = _ffn_conv(t, halo, cw_ref, cb_ref)
        a, b = u[:, :ff], u[:, ff:]
        da = dact_ref[...]
        sa, dsa = _silu_grad(a)
        dv = jnp.concatenate([da * b * dsa, da * sa], axis=1)
        nxt = carry_ref[...]
        dup = (dv * cw_ref[2:3, :] + _shift_up(dv, nxt, 1) * cw_ref[1:2, :]) + _shift_up(dv, nxt, 2) * cw_ref[0:1, :]
        dup_ref[...] = dup.astype(_ACT)
        dcb_ref[...] += _colsum(dv)
        dcw_ref[2:3, :] += _colsum(dv * t)
        dcw_ref[1:2, :] += _colsum(dv * t1)
        dcw_ref[0:1, :] += _colsum(dv * t2)
        carry_ref[...] = dv[:SUB]

    rev = lambda w: pl.BlockSpec((ts, w), lambda i: (n - 1 - i, 0))
    halo_spec = pl.BlockSpec((SUB, w2), lambda i: (jnp.maximum((n - 1 - i) * (ts // SUB) - 1, 0), 0))
    return pl.pallas_call(
        body, grid=(n,), in_specs=[rev(w2), halo_spec, rev(ff), _vec(w2, FFN_CONV), _vec(w2)],
        out_specs=(rev(w2), _vec(w2, FFN_CONV), _vec(w2)),
        out_shape=(jax.ShapeDtypeStruct((S, w2), _ACT), jax.ShapeDtypeStruct((FFN_CONV, w2), F32),
                   jax.ShapeDtypeStruct((1, w2), F32)),
        scratch_shapes=[pltpu.VMEM((SUB, w2), F32)], compiler_params=_cp(1), name=name)(up, up, dact, cw, cb)


def _ssd_core(pre, halo, misc, cw_ref, cb_ref, dtb, alog):
    q = pre.shape[0]
    conv = cb_ref[...]
    for k in range(SSD_CONV):
        conv = conv + _shift_down(pre, halo, SSD_CONV - 1 - k) * cw_ref[k:k + 1, :]
    xbc = _silu(conv)
    raw = misc + dtb
    dt = _softplus(raw)
    a = -jnp.exp(alog)
    r = lax.broadcasted_iota(jnp.int32, (q, q), 0)
    c = lax.broadcasted_iota(jnp.int32, (q, q), 1)
    tri = r >= c
    acum = jnp.dot(tri.astype(F32), dt * a, precision=_HI, preferred_element_type=F32)
    return conv, xbc, raw, dt, a, acum, acum.T, tri


def _sel(v, j, lo):
    return jnp.where(lo, v[:, 2 * j:2 * j + 1], v[:, 2 * j + 1:2 * j + 2])


def _ssd_pair_fwd(xbc, dt, acum, acum_t, tri, dsk, g_mat, b_mat, c_mat, h_pair, j, lo, lo1, sub_lo):
    q = xbc.shape[0]
    x = xbc[:, LANE * j:LANE * (j + 1)]
    dtp = _sel(dt, j, lo)
    ap = _sel(acum, j, lo)
    xd = x * dtp
    ls, ms = [], []
    for h in (2 * j, 2 * j + 1):
        seg = acum[:, h:h + 1] - acum_t[h:h + 1, :]
        l_mat = jnp.exp(jnp.where(tri, seg, -jnp.inf))
        ls.append(l_mat)
        ms.append(g_mat * l_mat)
    yd = jnp.where(lo, _dot(ms[0], xd), _dot(ms[1], xd))
    ea = jnp.exp(ap)
    yo = _dot_nt(c_mat, h_pair) * ea
    dp = _sel(dsk, j, lo1)
    alast = acum[q - 1:q, :]
    e = jnp.exp(_sel(alast, j, lo1) - ap)
    cd = jnp.where(sub_lo, jnp.exp(alast[:, 2 * j:2 * j + 1]), jnp.exp(alast[:, 2 * j + 1:2 * j + 2]))
    return dict(x=x, dtp=dtp, ap=ap, xd=xd, ls=ls, ms=ms, ea=ea, yo=yo, dp=dp, e=e, cd=cd, y=yd + yo + x * dp)


def _gnorm(yg):
    half = SSD_INNER // SSD_GROUPS
    rstds, yns = [], []
    for g in range(SSD_GROUPS):
        part = yg[:, half * g:half * (g + 1)]
        rstd = lax.rsqrt(jnp.mean(part * part, axis=-1, keepdims=True) + EPS)
        rstds.append(rstd)
        yns.append(part * rstd)
    return rstds, yns


def _ssd_specs(nc, rev):
    q = SSD_CHUNK
    cidx = (lambda i: nc - 1 - i) if rev else (lambda i: i)
    return [
        pl.BlockSpec((q, SSD_XBC), lambda i: (cidx(i), C_XBC // SSD_XBC)),
        pl.BlockSpec((SUB, SSD_XBC), lambda i: (jnp.maximum(cidx(i) * (q // SUB) - 1, 0), C_XBC // SSD_XBC)),
        pl.BlockSpec((q, SSD_INNER), lambda i: (cidx(i), C_Z // SSD_INNER)),
        pl.BlockSpec((q, LANE), lambda i: (cidx(i), C_MISC // LANE)),
    ]


def _ssd_param_specs():
    return [_vec(SSD_XBC, SSD_CONV), _vec(SSD_XBC), _vec(LANE), _vec(LANE), _vec(LANE), _vec(SSD_INNER)]


def _ssd_fwd(proj, cw, cb, dtb, alog, dsk, ng, *, name):
    S = proj.shape[0]
    q = SSD_CHUNK
    nc = S // q
    npair = SSD_HEADS // 2

    def body(xbc_ref, halo_ref, z_ref, misc_ref, cw_ref, cb_ref, dtb_ref, alog_ref, dsk_ref, ng_ref,
             y_ref, hin_ref, h_ref):
        c = pl.program_id(0)

        @pl.when(c == 0)
        def _():
            h_ref[...] = jnp.zeros_like(h_ref)

        pre = xbc_ref[...]
        halo = jnp.where(c > 0, halo_ref[...], 0.0)
        conv, xbc, raw, dt, a, acum, acum_t, tri = _ssd_core(pre, halo, misc_ref[...], cw_ref, cb_ref,
                                                             dtb_ref[...], alog_ref[...])
        lo = lax.broadcasted_iota(jnp.int32, (q, LANE), 1) < LANE // 2
        lo1 = lo[:1]
        sub_lo = lax.broadcasted_iota(jnp.int32, (LANE, LANE), 0) < LANE // 2
        ys = []
        for g in range(SSD_GROUPS):
            b_mat = xbc[:, SSD_INNER + SSD_STATE * g:SSD_INNER + SSD_STATE * (g + 1)]
            c_mat = xbc[:, SSD_INNER + SSD_STATE * (SSD_GROUPS + g):SSD_INNER + SSD_STATE * (SSD_GROUPS + g + 1)]
            g_mat = _dot_nt(c_mat, b_mat)
            for jj in range(npair // SSD_GROUPS):
                j = g * (npair // SSD_GROUPS) + jj
                hj = h_ref[j]
                p = _ssd_pair_fwd(xbc, dt, acum, acum_t, tri, dsk_ref[...], g_mat, b_mat, c_mat, hj, j, lo, lo1, sub_lo)
                ys.append(p["y"])
                hin_ref[0, j] = hj
                h_ref[j] = p["cd"] * hj + _dot_tn(p["xd"] * p["e"], b_mat)
        yg = jnp.concatenate(ys, axis=1) * _silu(z_ref[...])
        _, yns = _gnorm(yg)
        y_ref[...] = jnp.concatenate(yns, axis=1) * ng_ref[...]

    return pl.pallas_call(
        body, grid=(nc,), in_specs=_ssd_specs(nc, False) + _ssd_param_specs(),
        out_specs=(pl.BlockSpec((q, SSD_INNER), lambda i: (i, 0)),
                   pl.BlockSpec((1, npair, LANE, LANE), lambda i: (i, 0, 0, 0))),
        out_shape=(jax.ShapeDtypeStruct((S, SSD_INNER), F32), jax.ShapeDtypeStruct((nc, npair, LANE, LANE), F32)),
        scratch_shapes=[pltpu.VMEM((npair, LANE, LANE), F32)], compiler_params=_cp(1), name=name,
    )(proj, proj, proj, proj, cw, cb, dtb, alog, dsk, ng)


def _ssd_bwd(proj, dycat, hin, cw, cb, dtb, alog, dsk, ng, *, name):
    S = proj.shape[0]
    q = SSD_CHUNK
    nc = S // q
    npair = SSD_HEADS // 2
    ppg = npair // SSD_GROUPS

    def body(xbc_ref, halo_ref, z_ref, misc_ref, dy_ref, hin_ref, cw_ref, cb_ref, dtb_ref, alog_ref, dsk_ref, ng_ref,
             dpre_ref, dz_ref, dmisc_ref, dcw_ref, dcb_ref, ddtb_ref, dalog_ref, ddsk_ref, dng_ref,
             dh_ref, carry_ref):
        i = pl.program_id(0)
        c = nc - 1 - i

        @pl.when(i == 0)
        def _():
            dh_ref[...] = jnp.zeros_like(dh_ref)
            carry_ref[...] = jnp.zeros_like(carry_ref)
            for r in (dcw_ref, dcb_ref, ddtb_ref, dalog_ref, ddsk_ref, dng_ref):
                r[...] = jnp.zeros_like(r)

        pre = xbc_ref[...]
        halo = jnp.where(c > 0, halo_ref[...], 0.0)
        conv, xbc, raw, dt, a, acum, acum_t, tri = _ssd_core(pre, halo, misc_ref[...], cw_ref, cb_ref,
                                                             dtb_ref[...], alog_ref[...])
        lane = lax.broadcasted_iota(jnp.int32, (q, LANE), 1)
        lane1 = lane[:1]
        rowi = lax.broadcasted_iota(jnp.int32, (q, LANE), 0)
        lastrow = rowi == q - 1
        lo = lane < LANE // 2
        lo1 = lo[:1]
        sub_lo = lax.broadcasted_iota(jnp.int32, (LANE, LANE), 0) < LANE // 2
        dsk = dsk_ref[...]
        alast = acum[q - 1:q, :]

        def halves(t):
            return _rowsum(jnp.where(lo, t, 0.0)), _rowsum(jnp.where(lo, 0.0, t))

        def put(ha, va, vb):
            ln = lane if va.shape[0] == q else lane1
            return jnp.where(ln == ha, va, 0.0) + jnp.where(ln == ha + 1, vb, 0.0)

        mats, pairs = [], []
        for g in range(SSD_GROUPS):
            b_mat = xbc[:, SSD_INNER + SSD_STATE * g:SSD_INNER + SSD_STATE * (g + 1)]
            c_mat = xbc[:, SSD_INNER + SSD_STATE * (SSD_GROUPS + g):SSD_INNER + SSD_STATE * (SSD_GROUPS + g + 1)]
            g_mat = _dot_nt(c_mat, b_mat)
            mats.append((b_mat, c_mat, g_mat))
            for jj in range(ppg):
                j = g * ppg + jj
                pairs.append(_ssd_pair_fwd(xbc, dt, acum, acum_t, tri, dsk, g_mat, b_mat, c_mat, hin_ref[0, j],
                                           j, lo, lo1, sub_lo))
        z = z_ref[...]
        sz, dsz = _silu_grad(z)
        y = jnp.concatenate([p["y"] for p in pairs], axis=1)
        rstds, yns = _gnorm(y * sz)
        dout = dy_ref[...]
        dng_ref[...] += _colsum(dout * jnp.concatenate(yns, axis=1))
        dyn = dout * ng_ref[...]
        half = SSD_INNER // SSD_GROUPS
        dygs = []
        for g in range(SSD_GROUPS):
            dyn_g = dyn[:, half * g:half * (g + 1)]
            dygs.append(rstds[g] * (dyn_g - yns[g] * jnp.mean(dyn_g * yns[g], axis=-1, keepdims=True)))
        dyg = jnp.concatenate(dygs, axis=1)
        dyv = dyg * sz
        dz_ref[...] = (dyg * y * dsz).astype(_ACT)

        da_acc = jnp.zeros((q, LANE), F32)
        ddt = jnp.zeros((q, LANE), F32)
        dds = jnp.zeros((1, LANE), F32)
        dxs, dbs, dcs = [], [], []
        for g in range(SSD_GROUPS):
            b_mat, c_mat, g_mat = mats[g]
            dg_mat = jnp.zeros((q, q), F32)
            db = jnp.zeros((q, SSD_STATE), F32)
            dc = jnp.zeros((q, SSD_STATE), F32)
            for jj in range(ppg):
                j = g * ppg + jj
                ha = 2 * j
                p = pairs[j]
                hj = hin_ref[0, j]
                dyp = dyv[:, LANE * j:LANE * (j + 1)]
                dsum = _colsum(dyp * p["x"])
                dds = dds + put(ha, _rowsum(jnp.where(lo1, dsum, 0.0)), _rowsum(jnp.where(lo1, 0.0, dsum)))
                dx = dyp * p["dp"]
                dw = dyp * p["ea"]
                dc = dc + _dot(dw, hj)
                dh_yo = _dot_tn(dw, c_mat)
                ra, rb = halves(dyp * p["yo"])
                da_acc = da_acc + put(ha, ra, rb)
                dxd = jnp.zeros((q, LANE), F32)
                for idx in range(2):
                    dyh = jnp.where(lo, dyp, 0.0) if idx == 0 else jnp.where(lo, 0.0, dyp)
                    dm = _dot_nt(dyh, p["xd"])
                    dxd = dxd + _dot_tn(p["ms"][idx], dyh)
                    dg_mat = dg_mat + dm * p["ls"][idx]
                    t = dm * p["ms"][idx]
                    da_h = _rowsum(t) - _rowsum(t.T)
                    da_acc = da_acc + jnp.where(lane == ha + idx, da_h, 0.0)
                dhn = dh_ref[j]
                s = _rowsum(dhn * hj)
                sa = jnp.sum(jnp.where(sub_lo[:, :1], s, 0.0), keepdims=True)
                sb = jnp.sum(jnp.where(sub_lo[:, :1], 0.0, s), keepdims=True)
                cda, cdb = jnp.exp(alast[:, ha:ha + 1]), jnp.exp(alast[:, ha + 1:ha + 2])
                db = db + _dot(p["xd"] * p["e"], dhn)
                r = _dot_nt(b_mat, dhn)
                dxd = dxd + r * p["e"]
                qa, qb = halves(r * p["xd"] * p["e"])
                da_acc = da_acc - put(ha, qa, qb)
                tot_a = sa * cda + jnp.sum(qa, keepdims=True)
                tot_b = sb * cdb + jnp.sum(qb, keepdims=True)
                da_acc = da_acc + jnp.where(lastrow, put(ha, tot_a, tot_b), 0.0)
                dh_ref[j] = p["cd"] * dhn + dh_yo
                dx = dx + dxd * p["dtp"]
                ua, ub = halves(dxd * p["x"])
                ddt = ddt + put(ha, ua, ub)
                dxs.append(dx)
            dc = dc + _dot(dg_mat, b_mat)
            db = db + _dot_tn(dg_mat, c_mat)
            dbs.append(db)
            dcs.append(dc)
        r2 = lax.broadcasted_iota(jnp.int32, (q, q), 0)
        c2 = lax.broadcasted_iota(jnp.int32, (q, q), 1)
        dda = jnp.dot((c2 >= r2).astype(F32), da_acc, precision=_HI, preferred_element_type=F32)
        ddt = ddt + dda * a
        dalog_ref[...] += _colsum(dda * dt) * a
        ddsk_ref[...] += dds
        draw = jnp.where(lane < SSD_HEADS, ddt * _sigmoid(raw), 0.0)
        ddtb_ref[...] += _colsum(draw)
        dmisc_ref[...] = draw
        dconv = jnp.concatenate(dxs + dbs + dcs, axis=1) * _dsilu(conv)
        dcb_ref[...] += _colsum(dconv)
        nxt = carry_ref[...]
        dpre = jnp.zeros_like(dconv)
        for k in range(SSD_CONV):
            dcw_ref[k:k + 1, :] += _colsum(dconv * _shift_down(pre, halo, SSD_CONV - 1 - k))
            dpre = dpre + _shift_up(dconv, nxt, SSD_CONV - 1 - k) * cw_ref[k:k + 1, :]
        dpre_ref[...] = dpre.astype(_ACT)
        carry_ref[...] = dconv[:SUB]

    rev = lambda i: (nc - 1 - i, 0)
    vshape = lambda w, r=1: jax.ShapeDtypeStruct((r, w), F32)
    return pl.pallas_call(
        body, grid=(nc,),
        in_specs=_ssd_specs(nc, True) + [pl.BlockSpec((q, SSD_INNER), rev),
                                         pl.BlockSpec((1, npair, LANE, LANE), lambda i: (nc - 1 - i, 0, 0, 0))]
        + _ssd_param_specs(),
        out_specs=(pl.BlockSpec((q, SSD_XBC), rev), pl.BlockSpec((q, SSD_INNER), rev), pl.BlockSpec((q, LANE), rev),
                   _vec(SSD_XBC, SSD_CONV), _vec(SSD_XBC), _vec(LANE), _vec(LANE), _vec(LANE), _vec(SSD_INNER)),
        out_shape=(jax.ShapeDtypeStruct((S, SSD_XBC), _ACT), jax.ShapeDtypeStruct((S, SSD_INNER), _ACT),
                   jax.ShapeDtypeStruct((S, LANE), F32),
                   vshape(SSD_XBC, SSD_CONV), vshape(SSD_XBC), vshape(LANE), vshape(LANE), vshape(LANE),
                   vshape(SSD_INNER)),
        scratch_shapes=[pltpu.VMEM((npair, LANE, LANE), F32), pltpu.VMEM((SUB, SSD_XBC), F32)],
        compiler_params=_cp(1), name=name,
    )(proj, proj, proj, proj, dycat, hin, cw, cb, dtb, alog, dsk, ng)


def _rope(x, cosf, sina, sinb):
    return x * cosf + pltpu.roll(x, LANE - MLA_ROPE // 2, 1) * sina + pltpu.roll(x, MLA_ROPE // 2, 1) * sinb


def _rope_t(dy, cosf, sina, sinb):
    return dy * cosf + pltpu.roll(dy * sina, MLA_ROPE // 2, 1) + pltpu.roll(dy * sinb, LANE - MLA_ROPE // 2, 1)


def _rope_lanes(shape):
    lane = lax.broadcasted_iota(jnp.int32, shape, 1)
    return (lane >= ROPE_LANE) & (lane < ROPE_LANE + MLA_ROPE)


def _mla_prep_fwd(proj, cosf, sina, sinb, gq, gkv, wuq, wukv, *, name):
    S = proj.shape[0]
    ts = _tile(S, TS_ROW, SUB)
    hw = MLA_HEADS * LANE

    def body(cq_ref, ckv_ref, misc_ref, cos_ref, sa_ref, sb_ref, gq_ref, gkv_ref, wuq_ref, wukv_ref,
             q_ref, k_ref, v_ref, vt_ref):
        cosv, sav, sbv = cos_ref[...], sa_ref[...], sb_ref[...]
        cq = cq_ref[...]
        qn = cq * lax.rsqrt(jnp.mean(cq * cq, axis=-1, keepdims=True) + EPS) * gq_ref[...]
        qh = _dot(qn, wuq_ref[...])
        ckv = ckv_ref[...]
        kvn = ckv * lax.rsqrt(jnp.mean(ckv * ckv, axis=-1, keepdims=True) + EPS) * gkv_ref[...]
        kv = _dot(kvn, wukv_ref[...])
        kr = _rope(jnp.where(_rope_lanes((ts, LANE)), misc_ref[...], 0.0), cosv, sav, sbv)
        for h in range(MLA_HEADS):
            sl = slice(LANE * h, LANE * (h + 1))
            q_ref[:, sl] = (_rope(qh[:, sl], cosv, sav, sbv) * _Q_SCALE).astype(_ACT)
            k_ref[:, sl] = (kv[:, sl] + kr).astype(_ACT)
        v_ref[...] = kv[:, hw:].astype(_ACT)
        vt_ref[...] = kv[:, hw:].T.astype(_ACT)

    oshape = jax.ShapeDtypeStruct((S, hw), _ACT)
    return pl.pallas_call(
        body, grid=(S // ts,),
        in_specs=[_row(ts, MLA_QR, C_CQ // MLA_QR), _row(ts, MLA_KVR, C_CKV // MLA_KVR), _row(ts, LANE, C_MISC // LANE),
                  _row(ts, LANE), _row(ts, LANE), _row(ts, LANE), _vec(MLA_QR), _vec(MLA_KVR),
                  _vec(hw, MLA_QR), _vec(2 * hw, MLA_KVR)],
        out_specs=(_row(ts, hw),) * 3 + (pl.BlockSpec((hw, ts), lambda i: (0, i)),),
        out_shape=(oshape,) * 3 + (jax.ShapeDtypeStruct((hw, S), _ACT),), compiler_params=_cp(1), name=name,
    )(proj, proj, proj, cosf, sina, sinb, gq, gkv, wuq, wukv)


def _mla_prep_bwd(proj, dq, dk, dv, dmisc_ssd, cosf, sina, sinb, gq, gkv, wuq, wukv, *, name):
    S = proj.shape[0]
    ts = _tile(S, TS_ROW, SUB)
    hw = MLA_HEADS * LANE

    def body(cq_ref, ckv_ref, dq_ref, dk_ref, dv_ref, dms_ref, cos_ref, sa_ref, sb_ref, gq_ref, gkv_ref,
             wuq_ref, wukv_ref, dcq_ref, dckv_ref, dmisc_ref, dqh_ref, dkv_ref, qn_ref, kvn_ref, dgq_ref, dgkv_ref):
        i = pl.program_id(0)
        cosv, sav, sbv = cos_ref[...], sa_ref[...], sb_ref[...]

        @pl.when(i == 0)
        def _():
            dgq_ref[...] = jnp.zeros_like(dgq_ref)
            dgkv_ref[...] = jnp.zeros_like(dgkv_ref)

        dqh = jnp.concatenate([_rope_t(dq_ref[:, LANE * h:LANE * (h + 1)], cosv, sav, sbv)
                               for h in range(MLA_HEADS)], axis=1)
        dqh_ref[...] = dqh.astype(_ACT)
        dkv = jnp.concatenate([dk_ref[...], dv_ref[...]], axis=1)
        dkv_ref[...] = dkv.astype(_ACT)

        def norm_bwd(x, g, dn, dg_ref, n_ref):
            rstd = lax.rsqrt(jnp.mean(x * x, axis=-1, keepdims=True) + EPS)
            xhat = x * rstd
            n_ref[...] = (xhat * g).astype(_ACT)
            dg_ref[...] += _colsum(dn * xhat)
            dxh = dn * g
            return rstd * (dxh - xhat * jnp.mean(dxh * xhat, axis=-1, keepdims=True))

        dcq_ref[...] = norm_bwd(cq_ref[...], gq_ref[...], _dot_nt(dqh, wuq_ref[...]), dgq_ref, qn_ref).astype(_ACT)
        dckv_ref[...] = norm_bwd(ckv_ref[...], gkv_ref[...], _dot_nt(dkv, wukv_ref[...]), dgkv_ref, kvn_ref).astype(_ACT)
        dks = dk_ref[:, 0:LANE]
        for h in range(1, MLA_HEADS):
            dks = dks + dk_ref[:, LANE * h:LANE * (h + 1)]
        rl = _rope_lanes((ts, LANE))
        dkr = _rope_t(jnp.where(rl, dks, 0.0), cosv, sav, sbv)
        dmisc_ref[...] = (dms_ref[...] + jnp.where(rl, dkr, 0.0)).astype(_ACT)

    act = lambda w: jax.ShapeDtypeStruct((S, w), _ACT)
    return pl.pallas_call(
        body, grid=(S // ts,),
        in_specs=[_row(ts, MLA_QR, C_CQ // MLA_QR), _row(ts, MLA_KVR, C_CKV // MLA_KVR),
                  _row(ts, hw), _row(ts, hw), _row(ts, hw), _row(ts, LANE),
                  _row(ts, LANE), _row(ts, LANE), _row(ts, LANE), _vec(MLA_QR), _vec(MLA_KVR),
                  _vec(hw, MLA_QR), _vec(2 * hw, MLA_KVR)],
        out_specs=(_row(ts, MLA_QR), _row(ts, MLA_KVR), _row(ts, LANE), _row(ts, hw), _row(ts, 2 * hw),
                   _row(ts, MLA_QR), _row(ts, MLA_KVR), _vec(MLA_QR), _vec(MLA_KVR)),
        out_shape=(act(MLA_QR), act(MLA_KVR), act(LANE), act(hw), act(2 * hw), act(MLA_QR), act(MLA_KVR),
                   jax.ShapeDtypeStruct((1, MLA_QR), F32), jax.ShapeDtypeStruct((1, MLA_KVR), F32)),
        compiler_params=_cp(1), name=name,
    )(proj, proj, dq, dk, dv, dmisc_ssd, cosf, sina, sinb, gq, gkv, wuq, wukv)


_MLA_SCALE = 1.0 / math.sqrt(MLA_NOPE + MLA_ROPE)
_LOG2E = 1.4426950408889634
_Q_SCALE = _MLA_SCALE * _LOG2E
ATT_CHUNK = 1024


def _tri_grid(nq, by_key):
    if by_key:
        pairs = [(i, j) for j in range(nq) for i in range(j, nq)]
    else:
        pairs = [(i, j) for i in range(nq) for j in range(i + 1)]
    return jnp.asarray([p[0] for p in pairs], jnp.int32), jnp.asarray([p[1] for p in pairs], jnp.int32)


def _attn_fwd(q, k, vt, *, name):
    S = q.shape[0]
    tq = _tile(S, TQ_ATT)
    nq = S // tq
    itab, jtab = _tri_grid(nq, False)

    def body(it_ref, jt_ref, q_ref, k_ref, vt_ref, o_ref, lse_ref, lset_ref, m_ref, l_ref, acc_ref):
        t = pl.program_id(1)
        i, j = it_ref[t], jt_ref[t]

        @pl.when(j == 0)
        def _():
            m_ref[...] = jnp.full_like(m_ref, -jnp.inf)
            l_ref[...] = jnp.zeros_like(l_ref)
            acc_ref[...] = jnp.zeros_like(acc_ref)

        def step(diagonal):
            s = _dot_nt(k_ref[...], q_ref[...])
            if diagonal:
                kk = lax.broadcasted_iota(jnp.int32, (tq, tq), 0)
                s = jnp.where(kk <= lax.broadcasted_iota(jnp.int32, (tq, tq), 1), s, -jnp.inf)
            m_prev = m_ref[...]
            m_new = jnp.maximum(m_prev, jnp.max(s, axis=0, keepdims=True))
            p = jnp.exp2(s - m_new)
            alpha = jnp.exp2(m_prev - m_new)
            l_ref[...] = alpha * l_ref[...] + _colsum(p)
            acc_ref[...] = alpha * acc_ref[...] + _dot(vt_ref[...], p)
            m_ref[...] = m_new

        pl.when(j < i)(functools.partial(step, False))
        pl.when(j == i)(functools.partial(step, True))

        @pl.when(j == i)
        def _():
            o_ref[...] = (acc_ref[...] / l_ref[...]).T
            lse = m_ref[...] + jnp.log2(l_ref[...])
            lset_ref[...] = jnp.broadcast_to(lse, (SUB, tq))
            lse_ref[...] = jnp.broadcast_to(lse, (LANE, tq)).T

    qspec = pl.BlockSpec((tq, LANE), lambda h, t, it, jt: (it[t], h))
    kspec = pl.BlockSpec((tq, LANE), lambda h, t, it, jt: (jt[t], h))
    vtspec = pl.BlockSpec((LANE, tq), lambda h, t, it, jt: (h, jt[t]))
    oshape = jax.ShapeDtypeStruct((S, MLA_HEADS * LANE), F32)
    return pl.pallas_call(
        body,
        grid_spec=pltpu.PrefetchScalarGridSpec(
            num_scalar_prefetch=2, grid=(MLA_HEADS, itab.shape[0]), in_specs=[qspec, kspec, vtspec],
            out_specs=(qspec, qspec, pl.BlockSpec((SUB, tq), lambda h, t, it, jt: (h, it[t]))),
            scratch_shapes=[pltpu.VMEM((1, tq), F32), pltpu.VMEM((1, tq), F32), pltpu.VMEM((LANE, tq), F32)]),
        out_shape=(oshape, oshape, jax.ShapeDtypeStruct((MLA_HEADS * SUB, S), F32)),
        compiler_params=_cp(2), name=name)(itab, jtab, q, k, vt)


def _attn_bwd_dq(q, k, v, o, lse, dycat, *, name):
    S = q.shape[0]
    tq = _tile(S, TQ_ATT)
    nq = S // tq
    rc = min(ATT_CHUNK, tq)
    itab, jtab = _tri_grid(nq, False)

    def body(it_ref, jt_ref, q_ref, k_ref, v_ref, o_ref, lse_ref, do_ref, dq_ref, acc_ref):
        t = pl.program_id(1)
        i, j = it_ref[t], jt_ref[t]

        @pl.when(j == 0)
        def _():
            acc_ref[...] = jnp.zeros_like(acc_ref)

        def step(diagonal):
            kv, vv = k_ref[...], v_ref[...]
            for r in range(tq // rc):
                rows = slice(r * rc, (r + 1) * rc)
                s = _dot_nt(q_ref[rows, :], kv)
                if diagonal:
                    rr = r * rc + lax.broadcasted_iota(jnp.int32, (rc, tq), 0)
                    s = jnp.where(lax.broadcasted_iota(jnp.int32, (rc, tq), 1) <= rr, s, -jnp.inf)
                p = jnp.exp2(s - lse_ref[rows, 0:1])
                dov = do_ref[rows, :]
                delta = _rowsum(dov * o_ref[rows, :])
                ds = p * (_dot_nt(dov, vv) - delta)
                acc_ref[rows, :] += _dot(ds, kv)

        pl.when(j < i)(functools.partial(step, False))
        pl.when(j == i)(functools.partial(step, True))

        @pl.when(j == i)
        def _():
            dq_ref[...] = acc_ref[...] * _MLA_SCALE

    qspec = pl.BlockSpec((tq, LANE), lambda h, t, it, jt: (it[t], h))
    kspec = pl.BlockSpec((tq, LANE), lambda h, t, it, jt: (jt[t], h))
    dospec = pl.BlockSpec((tq, LANE), lambda h, t, it, jt: (it[t], SSD_INNER // LANE + h))
    return pl.pallas_call(
        body,
        grid_spec=pltpu.PrefetchScalarGridSpec(
            num_scalar_prefetch=2, grid=(MLA_HEADS, itab.shape[0]),
            in_specs=[qspec, kspec, kspec, qspec, qspec, dospec], out_specs=qspec,
            scratch_shapes=[pltpu.VMEM((tq, LANE), F32)]),
        out_shape=jax.ShapeDtypeStruct((S, MLA_HEADS * LANE), F32),
        compiler_params=_cp(2), name=name)(itab, jtab, q, k, v, o, lse, dycat)


def _attn_bwd_dkv(q, k, v, o, lset, dycat, *, name):
    S = q.shape[0]
    tq = _tile(S, TQ_ATT)
    nq = S // tq
    kc = min(ATT_CHUNK, tq)
    itab, jtab = _tri_grid(nq, True)

    def body(it_ref, jt_ref, q_ref, k_ref, v_ref, o_ref, lset_ref, do_ref, dk_ref, dv_ref, dk_acc, dv_acc):
        t = pl.program_id(1)
        i, j = it_ref[t], jt_ref[t]

        @pl.when(i == j)
        def _():
            dk_acc[...] = jnp.zeros_like(dk_acc)
            dv_acc[...] = jnp.zeros_like(dv_acc)

        def step(diagonal):
            qv, dov = q_ref[...], do_ref[...]
            delta = lax.dot_general(jnp.ones((SUB, LANE), F32), dov * o_ref[...], (((1,), (1,)), ((), ())),
                                    precision=_HI, preferred_element_type=F32)[0:1]
            lse = lset_ref[0:1, :]
            for c in range(tq // kc):
                rows = slice(c * kc, (c + 1) * kc)
                s = _dot_nt(k_ref[rows, :], qv)
                if diagonal:
                    kk = c * kc + lax.broadcasted_iota(jnp.int32, (kc, tq), 0)
                    s = jnp.where(kk <= lax.broadcasted_iota(jnp.int32, (kc, tq), 1), s, -jnp.inf)
                p = jnp.exp2(s - lse)
                dv_acc[rows, :] += _dot(p, dov)
                ds = p * (_dot_nt(v_ref[rows, :], dov) - delta)
                dk_acc[rows, :] += _dot(ds, qv)

        pl.when(i > j)(functools.partial(step, False))
        pl.when(i == j)(functools.partial(step, True))

        @pl.when(i == nq - 1)
        def _():
            dk_ref[...] = dk_acc[...] * (1.0 / _LOG2E)
            dv_ref[...] = dv_acc[...]

    qspec = pl.BlockSpec((tq, LANE), lambda h, t, it, jt: (it[t], h))
    kspec = pl.BlockSpec((tq, LANE), lambda h, t, it, jt: (jt[t], h))
    dospec = pl.BlockSpec((tq, LANE), lambda h, t, it, jt: (it[t], SSD_INNER // LANE + h))
    lspec = pl.BlockSpec((SUB, tq), lambda h, t, it, jt: (h, it[t]))
    oshape = jax.ShapeDtypeStruct((S, MLA_HEADS * LANE), F32)
    return pl.pallas_call(
        body,
        grid_spec=pltpu.PrefetchScalarGridSpec(
            num_scalar_prefetch=2, grid=(MLA_HEADS, itab.shape[0]),
            in_specs=[qspec, kspec, kspec, qspec, lspec, dospec], out_specs=(kspec, kspec),
            scratch_shapes=[pltpu.VMEM((tq, LANE), F32), pltpu.VMEM((tq, LANE), F32)]),
        out_shape=(oshape, oshape), compiler_params=_cp(2), name=name)(itab, jtab, q, k, v, o, lset, dycat)


_SWA_SCALE = 1.0 / math.sqrt(SWA_HD)
_SWA_KW = SWA_KV * LANE


def _swa_specs(S, ts, rev):
    n = S // ts
    t = (lambda i: n - 1 - i) if rev else (lambda i: i)
    hb = lambda i: jnp.maximum(t(i) * (ts // WINDOW) - 1, 0)
    return [
        pl.BlockSpec((ts, SWA_HEADS * LANE), lambda i: (t(i), C_SQ // (SWA_HEADS * LANE))),
        pl.BlockSpec((ts, _SWA_KW), lambda i: (t(i), C_SK // _SWA_KW)),
        pl.BlockSpec((WINDOW, _SWA_KW), lambda i: (hb(i), C_SK // _SWA_KW)),
        pl.BlockSpec((ts, _SWA_KW), lambda i: (t(i), C_SV // _SWA_KW)),
        pl.BlockSpec((WINDOW, _SWA_KW), lambda i: (hb(i), C_SV // _SWA_KW)),
    ]


def _swa_scores(qh, kk, t, b, ts):
    s = _dot_nt(qh, kk) * _SWA_SCALE
    row = lax.broadcasted_iota(jnp.int32, (WINDOW, 2 * WINDOW), 0)
    col = lax.broadcasted_iota(jnp.int32, (WINDOW, 2 * WINDOW), 1)
    rel = WINDOW + row - col
    kpos = t * ts + (b - 1) * WINDOW + col
    return jnp.where((rel >= 0) & (rel < WINDOW) & (kpos >= 0), s, -jnp.inf)


def _swa_fwd(proj, sinks, *, name):
    S = proj.shape[0]
    ts = _tile(S, TS_SWA)
    nb = ts // WINDOW

    def body(q_ref, k_ref, kh_ref, v_ref, vh_ref, sink_ref, o_ref, lse_ref):
        t = pl.program_id(0)
        kext = jnp.concatenate([kh_ref[...], k_ref[...]], axis=0)
        vext = jnp.concatenate([vh_ref[...], v_ref[...]], axis=0)
        for b in range(nb):
            rows = slice(WINDOW * b, WINDOW * (b + 1))
            for h in range(SWA_HEADS):
                kvl = slice(LANE * (h // (SWA_HEADS // SWA_KV)), LANE * (h // (SWA_HEADS // SWA_KV) + 1))
                hl = slice(LANE * h, LANE * (h + 1))
                kk = kext[WINDOW * b:WINDOW * (b + 2), kvl]
                vv = vext[WINDOW * b:WINDOW * (b + 2), kvl]
                s = _swa_scores(q_ref[rows, hl], kk, t, b, ts)
                sk = sink_ref[:, h:h + 1]
                m = jnp.maximum(jnp.max(s, axis=1, keepdims=True), sk)
                p = jnp.exp(s - m)
                den = _rowsum(p) + jnp.exp(sk - m)
                o_ref[rows, hl] = _dot(p, vv) / den
                lse_ref[rows, hl] = jnp.broadcast_to(m + jnp.log(den), (WINDOW, LANE))

    oshape = jax.ShapeDtypeStruct((S, SWA_HEADS * LANE), F32)
    ospec = pl.BlockSpec((ts, SWA_HEADS * LANE), lambda i: (i, 0))
    return pl.pallas_call(
        body, grid=(S // ts,), in_specs=_swa_specs(S, ts, False) + [_vec(LANE)], out_specs=(ospec, ospec),
        out_shape=(oshape, oshape), compiler_params=_cp(1), name=name)(proj, proj, proj, proj, proj, sinks)


def _swa_bwd(proj, o, lse, dycat, sinks, *, name):
    S = proj.shape[0]
    ts = _tile(S, TS_SWA)
    nb = ts // WINDOW
    n = S // ts
    grp = SWA_HEADS // SWA_KV

    def body(q_ref, k_ref, kh_ref, v_ref, vh_ref, o_ref, lse_ref, do_ref, sink_ref,
             dq_ref, dk_ref, dv_ref, dsink_ref, dk_carry, dv_carry):
        i = pl.program_id(0)
        t = n - 1 - i

        @pl.when(i == 0)
        def _():
            dk_carry[...] = jnp.zeros_like(dk_carry)
            dv_carry[...] = jnp.zeros_like(dv_carry)
            dsink_ref[...] = jnp.zeros_like(dsink_ref)

        kext = jnp.concatenate([kh_ref[...], k_ref[...]], axis=0)
        vext = jnp.concatenate([vh_ref[...], v_ref[...]], axis=0)
        lane1 = lax.broadcasted_iota(jnp.int32, (1, LANE), 1)
        dkb = [[jnp.zeros((WINDOW, LANE), F32) for _ in range(SWA_KV)] for _ in range(nb + 1)]
        dvb = [[jnp.zeros((WINDOW, LANE), F32) for _ in range(SWA_KV)] for _ in range(nb + 1)]
        dsink = jnp.zeros((1, LANE), F32)
        for b in range(nb):
            rows = slice(WINDOW * b, WINDOW * (b + 1))
            for h in range(SWA_HEADS):
                kvh = h // grp
                kvl = slice(LANE * kvh, LANE * (kvh + 1))
                hl = slice(LANE * h, LANE * (h + 1))
                kk = kext[WINDOW * b:WINDOW * (b + 2), kvl]
                vv = vext[WINDOW * b:WINDOW * (b + 2), kvl]
                qh = q_ref[rows, hl]
                lse_h = lse_ref[rows, LANE * h:LANE * h + 1]
                p = jnp.exp(_swa_scores(qh, kk, t, b, ts) - lse_h)
                doh = do_ref[rows, hl]
                delta = _rowsum(doh * o_ref[rows, hl])
                ds = p * (_dot_nt(doh, vv) - delta)
                sk = sink_ref[:, h:h + 1]
                dsink = dsink + jnp.where(lane1 == h, -jnp.sum(jnp.exp(sk - lse_h) * delta, keepdims=True), 0.0)
                dq_ref[rows, hl] = (_dot(ds, kk) * _SWA_SCALE).astype(_ACT)
                dkk = _dot_tn(ds, qh) * _SWA_SCALE
                dvv = _dot_tn(p, doh)
                dkb[b][kvh] = dkb[b][kvh] + dkk[:WINDOW]
                dkb[b + 1][kvh] = dkb[b + 1][kvh] + dkk[WINDOW:]
                dvb[b][kvh] = dvb[b][kvh] + dvv[:WINDOW]
                dvb[b + 1][kvh] = dvb[b + 1][kvh] + dvv[WINDOW:]
        dsink_ref[...] += dsink
        for dref, blocks, carry in ((dk_ref, dkb, dk_carry), (dv_ref, dvb, dv_carry)):
            old = carry[...]
            for b in range(1, nb + 1):
                blk = jnp.concatenate(blocks[b], axis=1)
                if b == nb:
                    blk = blk + old
                dref[WINDOW * (b - 1):WINDOW * b, :] = blk.astype(_ACT)
            carry[...] = jnp.concatenate(blocks[0], axis=1)

    hw = SWA_HEADS * LANE
    rev = lambda i: (n - 1 - i, 0)
    mix = lambda i: (n - 1 - i, (SSD_INNER + MLA_HEADS * LANE) // hw)
    return pl.pallas_call(
        body, grid=(n,),
        in_specs=_swa_specs(S, ts, True) + [pl.BlockSpec((ts, hw), rev), pl.BlockSpec((ts, hw), rev),
                                            pl.BlockSpec((ts, hw), mix), _vec(LANE)],
        out_specs=(pl.BlockSpec((ts, hw), rev), pl.BlockSpec((ts, _SWA_KW), rev), pl.BlockSpec((ts, _SWA_KW), rev),
                   _vec(LANE)),
        out_shape=(jax.ShapeDtypeStruct((S, hw), _ACT), jax.ShapeDtypeStruct((S, _SWA_KW), _ACT),
                   jax.ShapeDtypeStruct((S, _SWA_KW), _ACT), jax.ShapeDtypeStruct((1, LANE), F32)),
        scratch_shapes=[pltpu.VMEM((WINDOW, _SWA_KW), F32), pltpu.VMEM((WINDOW, _SWA_KW), F32)],
        compiler_params=_cp(1), name=name)(proj, proj, proj, proj, proj, o, lse, dycat, sinks)


def _exchange(arrays, *, scatter, name):
    n = len(arrays)

    def body(*refs):
        ins, outs = refs[:n], refs[n:2 * n]
        send_sems, recv_sems, loc_sems = refs[2 * n:]
        x, y, c = lax.axis_index("x"), lax.axis_index("y"), lax.axis_index("c")
        me = 4 * x + 2 * y + c

        def src(i, dest):
            return ins[i].at[dest] if scatter else ins[i]

        local = [pltpu.make_async_copy(src(i, me), outs[i].at[me], loc_sems.at[i]) for i in range(n)]
        for cp in local:
            cp.start()
        sends, recvs = [], []
        for k in range(1, NDEV):
            px = 1 - x if k & 4 else x
            py = 1 - y if k & 2 else y
            pc = 1 - c if k & 1 else c
            peer = 4 * px + 2 * py + pc
            for i in range(n):
                common = dict(send_sem=send_sems.at[i, k - 1], recv_sem=recv_sems.at[i, k - 1],
                              device_id=(px, py, pc), device_id_type=pl.DeviceIdType.MESH)
                sends.append(pltpu.make_async_remote_copy(src_ref=src(i, peer), dst_ref=outs[i].at[me], **common))
                recvs.append(pltpu.make_async_remote_copy(src_ref=src(i, peer), dst_ref=outs[i].at[peer], **common))
        for cp in sends:
            cp.start()
        for cp in recvs:
            cp.wait_recv()
        for cp in sends:
            cp.wait_send()
        for cp in local:
            cp.wait()

    hbm = pl.BlockSpec(memory_space=pl.ANY)
    out_shape = tuple(jax.ShapeDtypeStruct(a.shape if scatter else (NDEV,) + a.shape, a.dtype) for a in arrays)
    return pl.pallas_call(
        body, in_specs=[hbm] * n, out_specs=tuple([hbm] * n), out_shape=out_shape,
        scratch_shapes=[pltpu.SemaphoreType.DMA((n, NDEV - 1)), pltpu.SemaphoreType.DMA((n, NDEV - 1)),
                        pltpu.SemaphoreType.DMA((n,))],
        name=name)(*arrays)


def _adamw(w, m, v, parts, *, name):
    R, C = w.shape
    npart = parts.shape[0]
    cap = max(SUB, ((1 << 18) // C) // SUB * SUB)
    tr = _tile(R, cap, SUB)

    def body(w_ref, m_ref, v_ref, p_ref, g_ref, d_ref, mo_ref, vo_ref):
        g = p_ref[0]
        for k in range(1, npart):
            g = g + p_ref[k]
        mn = ADAM_B1 * m_ref[...] + (1.0 - ADAM_B1) * g
        vn = ADAM_B2 * v_ref[...] + (1.0 - ADAM_B2) * (g * g)
        m_hat = mn / (1.0 - ADAM_B1 ** ADAM_STEP)
        v_hat = vn / (1.0 - ADAM_B2 ** ADAM_STEP)
        g_ref[...] = g
        d_ref[...] = -ADAM_LR * (m_hat / (jnp.sqrt(v_hat) + ADAM_EPS) + ADAM_WD * w_ref[...])
        mo_ref[...] = mn
        vo_ref[...] = vn

    spec = pl.BlockSpec((tr, C), lambda i: (i, 0))
    oshape = jax.ShapeDtypeStruct((R, C), F32)
    return pl.pallas_call(
        body, grid=(R // tr,), in_specs=[spec] * 3 + [pl.BlockSpec((npart, tr, C), lambda i: (0, i, 0))],
        out_specs=(spec,) * 4, out_shape=(oshape,) * 4, compiler_params=_cp(1), name=name)(w, m, v, parts)


def _adamw_many(ws, ms, vs, landed, mine, me, *, name):
    n = len(ws)

    def body(me_ref, *refs):
        w_r, m_r, v_r, p_r, o_r = (refs[k * n:(k + 1) * n] for k in range(5))
        outs = refs[5 * n:]
        for i in range(n):
            own = o_r[i][...]
            g = jnp.where(me_ref[0] == 0, own, p_r[i][0])
            for k in range(1, NDEV):
                g = g + jnp.where(me_ref[0] == k, own, p_r[i][k])
            mn = ADAM_B1 * m_r[i][...] + (1.0 - ADAM_B1) * g
            vn = ADAM_B2 * v_r[i][...] + (1.0 - ADAM_B2) * (g * g)
            m_hat = mn / (1.0 - ADAM_B1 ** ADAM_STEP)
            v_hat = vn / (1.0 - ADAM_B2 ** ADAM_STEP)
            outs[4 * i][...] = g
            outs[4 * i + 1][...] = -ADAM_LR * (m_hat / (jnp.sqrt(v_hat) + ADAM_EPS) + ADAM_WD * w_r[i][...])
            outs[4 * i + 2][...] = mn
            outs[4 * i + 3][...] = vn

    vmem = pl.BlockSpec(memory_space=pltpu.VMEM)
    return pl.pallas_call(
        body, in_specs=[pl.BlockSpec(memory_space=pltpu.SMEM)] + [vmem] * (5 * n), out_specs=(vmem,) * (4 * n),
        out_shape=tuple(jax.ShapeDtypeStruct(w.shape, F32) for w in ws for _ in range(4)),
        name=name)(me, *ws, *ms, *vs, *landed, *mine)


def _adamw_layer(l, w, m, v, landed, mine, me, prev, *, name):
    L, R, C = w.shape
    npart = landed.shape[0]
    cap = max(2 * SUB, ((1 << 18) // C) // (2 * SUB) * (2 * SUB))
    tr = _tile(R, cap, 2 * SUB)
    nprev = 0 if prev is None else 4

    def body(me_ref, *refs):
        w_ref, m_ref, v_ref, p_ref, own_ref = refs[:5]
        g_ref, d_ref, mo_ref, vo_ref = refs[5 + nprev:]
        own = own_ref[...].astype(F32)
        g = jnp.where(me_ref[0] == 0, own, p_ref[0].astype(F32))
        for k in range(1, npart):
            g = g + jnp.where(me_ref[0] == k, own, p_ref[k].astype(F32))
        mn = ADAM_B1 * m_ref[...] + (1.0 - ADAM_B1) * g
        vn = ADAM_B2 * v_ref[...] + (1.0 - ADAM_B2) * (g * g)
        m_hat = mn / (1.0 - ADAM_B1 ** ADAM_STEP)
        v_hat = vn / (1.0 - ADAM_B2 ** ADAM_STEP)
        g_ref[...] = g
        d_ref[...] = -ADAM_LR * (m_hat / (jnp.sqrt(v_hat) + ADAM_EPS) + ADAM_WD * w_ref[...])
        mo_ref[...] = mn
        vo_ref[...] = vn

    spec = pl.BlockSpec((None, tr, C), lambda i, me_ref: (l, i, 0))
    oshape = jax.ShapeDtypeStruct((L, R, C), F32)
    return pl.pallas_call(
        body,
        grid_spec=pltpu.PrefetchScalarGridSpec(
            num_scalar_prefetch=1, grid=(R // tr,),
            in_specs=[spec] * 3 + [pl.BlockSpec((npart, tr, C), lambda i, me_ref: (0, i, 0)),
                                   pl.BlockSpec((None, tr, C), lambda i, me_ref: (me_ref[0], i, 0))]
            + [pl.BlockSpec(memory_space=pl.ANY)] * nprev,
            out_specs=(spec,) * 4),
        out_shape=(oshape,) * 4, input_output_aliases={6 + k: k for k in range(nprev)},
        compiler_params=_cp(1), name=name)(me, w, m, v, landed, mine, *(prev or ()))


_HBM = pl.BlockSpec(memory_space=pltpu.HBM)
_SEM = pl.BlockSpec(memory_space=pltpu.SEMAPHORE)
_EFFECT = pltpu.SideEffectType.DATAFLOW_SIDE_EFFECTING


def _peers():
    x, y, c = lax.axis_index("x"), lax.axis_index("y"), lax.axis_index("c")
    out = []
    for k in range(1, NDEV):
        px = 1 - x if k & 4 else x
        py = 1 - y if k & 2 else y
        pc = 1 - c if k & 1 else c
        out.append((k - 1, (px, py, pc), 4 * px + 2 * py + pc))
    return 4 * x + 2 * y + c, out


def _xchg_start(arrays, *, scatter, name):
    n = len(arrays)
    lands = [lax.empty(a.shape if scatter else (NDEV,) + a.shape, a.dtype) for a in arrays]

    def body(*refs):
        ins, lnd = refs[:n], refs[n:2 * n]
        send_sems, recv_sems = refs[2 * n], refs[2 * n + 1]
        token = refs[-1]
        me, peers = _peers()
        for k, dev, peer in peers:
            for i in range(n):
                pltpu.make_async_remote_copy(
                    src_ref=ins[i].at[peer] if scatter else ins[i], dst_ref=lnd[i].at[me],
                    send_sem=send_sems.at[i * (NDEV - 1) + k], recv_sem=recv_sems.at[i * (NDEV - 1) + k],
                    device_id=dev, device_id_type=pl.DeviceIdType.MESH).start()
        token[...] = jnp.zeros_like(token)

    sems = pltpu.SemaphoreType.DMA((n * (NDEV - 1),))
    res = pl.pallas_call(
        body, name=name,
        out_shape=(sems, sems) + tuple(pltpu.HBM(t.shape, t.dtype) for t in list(arrays) + lands)
        + (jax.ShapeDtypeStruct((SUB, LANE), F32),),
        in_specs=[_HBM] * (2 * n), out_specs=(_SEM, _SEM) + (_HBM,) * (2 * n) + (pl.BlockSpec(memory_space=pltpu.VMEM),),
        input_output_aliases={i: 2 + i for i in range(2 * n)},
        compiler_params=pltpu.CompilerParams(has_side_effects=_EFFECT),
    )(*[pltpu.with_memory_space_constraint(t, pltpu.HBM) for t in list(arrays) + lands])
    return dict(send=res[0], recv=res[1], thru=list(res[2:2 + 2 * n]), token=res[-1], scatter=scatter, n=n)


def _xchg_wait(handle, after, *, name):
    n, scatter = handle["n"], handle["scatter"]
    thru = handle["thru"]

    def body(*refs):
        ins, lnd = refs[:n], refs[n:2 * n]
        send_sems, recv_sems = refs[2 * n], refs[2 * n + 1]
        me, peers = _peers()
        for k, dev, peer in peers:
            for i in range(n):
                cp = pltpu.make_async_remote_copy(
                    src_ref=ins[i].at[peer] if scatter else ins[i], dst_ref=lnd[i].at[peer],
                    send_sem=send_sems.at[i * (NDEV - 1) + k], recv_sem=recv_sems.at[i * (NDEV - 1) + k],
                    device_id=dev, device_id_type=pl.DeviceIdType.MESH)
                cp.wait_send()
                cp.wait_recv()

    res = pl.pallas_call(
        body, name=name, out_shape=tuple(pltpu.HBM(t.shape, t.dtype) for t in thru),
        in_specs=[_HBM] * (2 * n) + [_SEM, _SEM, pl.BlockSpec(memory_space=pl.ANY)], out_specs=(_HBM,) * (2 * n),
        input_output_aliases={i: i for i in range(2 * n)},
        compiler_params=pltpu.CompilerParams(has_side_effects=_EFFECT),
    )(*thru, handle["send"], handle["recv"], after)
    return list(res[:n]), list(res[n:])


def _pad_heads(w, nh, hd, axis=-1):
    axis = axis % w.ndim
    shp = w.shape
    w = w.reshape(shp[:axis] + (nh, hd) + shp[axis + 1:])
    pads = [(0, 0)] * w.ndim
    pads[axis + 1] = (0, LANE - hd)
    return jnp.pad(w, pads).reshape(shp[:axis] + (nh * LANE,) + shp[axis + 1:])


def _unpad_heads(w, nh, hd, axis=-1):
    axis = axis % w.ndim
    shp = w.shape
    w = w.reshape(shp[:axis] + (nh, LANE) + shp[axis + 1:])
    w = lax.slice_in_dim(w, 0, hd, axis=axis + 1)
    return w.reshape(shp[:axis] + (nh * hd,) + shp[axis + 1:])


_O_DT = SSD_INNER + SSD_XBC
_O_CQ = _O_DT + SSD_HEADS
_O_CKV = _O_CQ + MLA_QR
_O_KR = _O_CKV + MLA_KVR
_O_SQ = _O_KR + MLA_ROPE
_O_SK = _O_SQ + SWA_HEADS * SWA_HD
_O_SV = _O_SK + SWA_KV * SWA_HD


def _w_in_to_padded(w, axis=-1):
    axis = axis % w.ndim
    cut = lambda a, b: lax.slice_in_dim(w, a, b, axis=axis)
    z, xbc, dt = cut(0, SSD_INNER), cut(SSD_INNER, _O_DT), cut(_O_DT, _O_CQ)
    cq, ckv, kr = cut(_O_CQ, _O_CKV), cut(_O_CKV, _O_KR), cut(_O_KR, _O_SQ)
    sq, sk, sv = cut(_O_SQ, _O_SK), cut(_O_SK, _O_SV), cut(_O_SV, D_IN)
    zeros = lambda n: jnp.zeros(w.shape[:axis] + (n,) + w.shape[axis + 1:], w.dtype)
    return jnp.concatenate([xbc, z, cq, ckv, dt, zeros(ROPE_LANE - SSD_HEADS), kr, zeros(LANE - ROPE_LANE - MLA_ROPE),
                            _pad_heads(sq, SWA_HEADS, SWA_HD, axis), _pad_heads(sk, SWA_KV, SWA_HD, axis),
                            _pad_heads(sv, SWA_KV, SWA_HD, axis)], axis=axis)


def _w_in_from_padded(g, axis=-1):
    axis = axis % g.ndim
    cut = lambda a, b: lax.slice_in_dim(g, a, b, axis=axis)
    xbc, z, cq, ckv = cut(C_XBC, C_Z), cut(C_Z, C_CQ), cut(C_CQ, C_CKV), cut(C_CKV, C_MISC)
    dt, kr = cut(C_MISC, C_MISC + SSD_HEADS), cut(C_MISC + ROPE_LANE, C_MISC + ROPE_LANE + MLA_ROPE)
    sq = _unpad_heads(cut(C_SQ, C_SK), SWA_HEADS, SWA_HD, axis)
    sk = _unpad_heads(cut(C_SK, C_SV), SWA_KV, SWA_HD, axis)
    sv = _unpad_heads(cut(C_SV, D_INP), SWA_KV, SWA_HD, axis)
    return jnp.concatenate([z, xbc, dt, cq, ckv, kr, sq, sk, sv], axis=axis)


def _w_out_to_padded(w):
    a = SSD_INNER
    b = a + MLA_HEADS * MLA_V
    return jnp.concatenate([w[..., :a, :], _pad_heads(w[..., a:b, :], MLA_HEADS, MLA_V, axis=-2),
                            _pad_heads(w[..., b:, :], SWA_HEADS, SWA_HD, axis=-2)], axis=-2)


def _w_out_from_padded(g):
    a = SSD_INNER
    b = a + MLA_HEADS * LANE
    return jnp.concatenate([g[..., :a, :], _unpad_heads(g[..., a:b, :], MLA_HEADS, MLA_V, axis=-2),
                            _unpad_heads(g[..., b:, :], SWA_HEADS, SWA_HD, axis=-2)], axis=-2)


def _w_ukv_to_padded(w):
    w4 = w.reshape(w.shape[:-1] + (MLA_HEADS, MLA_NOPE + MLA_V))
    flat = lambda t: t.reshape(w.shape[:-1] + (MLA_HEADS * t.shape[-1],))
    return jnp.concatenate([_pad_heads(flat(w4[..., :MLA_NOPE]), MLA_HEADS, MLA_NOPE),
                            _pad_heads(flat(w4[..., MLA_NOPE:]), MLA_HEADS, MLA_V)], axis=-1)


def _w_ukv_from_padded(g):
    hw = MLA_HEADS * LANE
    gk = _unpad_heads(g[..., :hw], MLA_HEADS, MLA_NOPE).reshape(g.shape[:-1] + (MLA_HEADS, MLA_NOPE))
    gv = _unpad_heads(g[..., hw:], MLA_HEADS, MLA_V).reshape(g.shape[:-1] + (MLA_HEADS, MLA_V))
    return jnp.concatenate([gk, gv], axis=-1).reshape(g.shape[:-1] + (MLA_HEADS * (MLA_NOPE + MLA_V),))


def _pad_lane(v):
    return jnp.pad(v, [(0, 0)] * (v.ndim - 1) + [(0, LANE - v.shape[-1])])


def _rope_tables(positions):
    inv_freq = ROPE_THETA ** (-jnp.arange(0, MLA_ROPE, 2, dtype=F32) / MLA_ROPE)
    ang = positions.astype(F32).reshape(-1, 1) * inv_freq
    cos, sin = jnp.cos(ang), jnp.sin(ang)
    S = ang.shape[0]
    one, zero = jnp.ones((S, ROPE_LANE), F32), jnp.zeros((S, ROPE_LANE), F32)
    tail1, tail0 = jnp.ones((S, LANE - ROPE_LANE - MLA_ROPE), F32), jnp.zeros((S, LANE - ROPE_LANE - MLA_ROPE), F32)
    z16 = jnp.zeros_like(sin)
    return (jnp.concatenate([one, cos, cos, tail1], axis=1), jnp.concatenate([zero, -sin, z16, tail0], axis=1),
            jnp.concatenate([zero, z16, sin, tail0], axis=1))


def _layer_fwd(l, x_in, f_prev, gate_prev, mod, P, tabs):
    sh1, sc1, g1, sh2, sc2, g2 = [mod[k:k + 1] for k in range(6)]
    tag = f"l{l}_"
    if f_prev is None:
        x0 = x_in
        h1 = _norm_fwd(x0, P["n1g"], sc1, sh1, name=tag + "norm1")
    else:
        x0, h1 = _norm_fwd(x_in, P["n1g"], sc1, sh1, f=f_prev, gate=gate_prev, name=tag + "norm1")
    proj = _mm(h1, P["w_in"], tb=True, name=tag + "proj")
    P.update(P.pop("mid")(proj))
    y_ssd, hin = _ssd_fwd(proj, P["ssd_cw"], P["ssd_cb"], P["dtb"], P["alog"], P["dsk"],
                          P["ssd_ng"], name=tag + "ssd")
    q, k, v, vt = _mla_prep_fwd(proj, *tabs, P["gq"], P["gkv"], P["w_uq"], P["w_ukv"], name=tag + "mla_prep")
    o_mla, lse_mla, lset_mla = _attn_fwd(q, k, vt, name=tag + "mla_attn")
    o_swa, lse_swa = _swa_fwd(proj, P["sinks"], name=tag + "swa")
    ycat = jnp.concatenate([y_ssd.astype(_ACT), o_mla.astype(_ACT), o_swa.astype(_ACT)], axis=1)
    y = _mm(ycat, P["w_out"], name=tag + "out")
    P.update(P.pop("late")(y))
    x1, h2 = _norm_fwd(x0, P["n2g"], sc2, sh2, f=y, gate=g1, name=tag + "norm2")
    up = _mm(h2, P["w_up"], tb=True, name=tag + "up")
    act = _ffn_act_fwd(up, P["fcw"], P["fcb"], name=tag + "ffn_act")
    f = _mm(act, P["w_down"], name=tag + "down")
    saved = dict(x0=x0, h1=h1, proj=proj, hin=hin, q=q, k=k, v=v, o_mla=o_mla, lse_mla=lse_mla, lset_mla=lset_mla, o_swa=o_swa,
                 lse_swa=lse_swa, ycat=ycat, y=y, x1=x1, h2=h2, up=up, act=act, f=f, mod=mod)
    return x1, f, g2, saved


def _layer_bwd(l, dxo, sv, P, tabs, on_part):
    mod = sv["mod"]
    sh1, sc1, g1, sh2, sc2, g2 = [mod[k:k + 1] for k in range(6)]
    tag = f"l{l}_b_"
    G = {}
    df, dg2 = _gate_bwd(dxo, sv["f"], g2, name=tag + "gate2")
    dact = _mm(df, P["w_down"], tb=True, name=tag + "dact")
    G["w_down"] = _mm(sv["act"], df, ta=True, name=tag + "dw_down")
    dup, G["fcw"], G["fcb"] = _ffn_bwd(sv["up"], dact, P["fcw"], P["fcb"], name=tag + "ffn")
    dh2 = _mm(dup, P["w_up"], name=tag + "dh2")
    G["w_up"] = _mm(dup, sv["h2"], ta=True, name=tag + "dw_up")
    token = on_part(l, "ffn", G)
    if token is not None:
        sc2 = sc2 + token
    dx1, G["n2g"], dsc2, dsh2 = _norm_bwd(dh2, sv["x1"], dxo, P["n2g"], sc2, name=tag + "norm2")
    dy, dg1 = _gate_bwd(dx1, sv["y"], g1, name=tag + "gate1")
    dycat = _mm(dy, P["w_out"], tb=True, name=tag + "dycat")
    G["w_out"] = _mm(sv["ycat"], dy, ta=True, name=tag + "dw_out")
    token = on_part(l, "out", G)
    ssd_cb = P["ssd_cb"] if token is None else P["ssd_cb"] + token
    proj = sv["proj"]
    (dpre, dz, dmisc_ssd, G["ssd_cw"], G["ssd_cb"], G["dtb"], G["alog"], G["dsk"], G["ssd_ng"]) = _ssd_bwd(
        proj, dycat, sv["hin"], P["ssd_cw"], ssd_cb, P["dtb"], P["alog"], P["dsk"],
        P["ssd_ng"], name=tag + "ssd")
    att = (sv["q"], sv["k"], sv["v"], sv["o_mla"])
    dq = _attn_bwd_dq(*att, sv["lse_mla"], dycat, name=tag + "mla_dq")
    dk, dv = _attn_bwd_dkv(*att, sv["lset_mla"], dycat, name=tag + "mla_dkv")
    dcq, dckv, dmisc, dqh, dkv, qn, kvn, G["gq"], G["gkv"] = _mla_prep_bwd(
        proj, dq, dk, dv, dmisc_ssd, *tabs, P["gq"], P["gkv"], P["w_uq"], P["w_ukv"], name=tag + "mla_prep")
    G["w_uq"] = _mm(qn, dqh, ta=True, name=tag + "dw_uq")
    G["w_ukv"] = _mm(kvn, dkv, ta=True, name=tag + "dw_ukv")
    dsq, dsk_, dsv_, G["sinks"] = _swa_bwd(proj, sv["o_swa"], sv["lse_swa"], dycat, P["sinks"], name=tag + "swa")
    dproj = jnp.concatenate([dpre, dz, dcq, dckv, dmisc, dsq, dsk_, dsv_], axis=1)
    G["w_in"] = _mm(dproj, sv["h1"], ta=True, name=tag + "dw_in")
    token = on_part(l, "mixer", G)
    if token is not None:
        sc1 = sc1 + token
    dh1 = _mm(dproj, P["w_in"], name=tag + "dh1")
    dx0, G["n1g"], dsc1, dsh1 = _norm_bwd(dh1, sv["x0"], dx1, P["n1g"], sc1, name=tag + "norm1")
    G["mod"] = jnp.concatenate([dsh1, dsc1, dg1, dsh2, dsc2, dg2], axis=0)
    return dx0, G


def _local_step(x, tgt, mods, get_params, tabs, final_g, on_grads, on_part):
    saved, params = [], []
    xin, f, gate = x, None, None
    for l in range(DEPTH):
        params.append(get_params(l, x if f is None else f))
        xin, f, gate, sv = _layer_fwd(l, xin, f, gate, mods[l], params[l], tabs)
        saved.append(sv)
    loss, dx, dfinal = _final_loss(xin, f, gate, final_g, tgt, name="final_loss")
    for l in reversed(range(DEPTH)):
        dx, G = _layer_bwd(l, dx, saved[l], params[l], tabs, on_part)
        on_grads(l, G)
    return loss[0, 0], dx, dfinal


_WEIGHTS = ['ada_w', 'ada_b', 'norm1_g', 'norm2_g', 'w_in', 'ssd_conv_w', 'ssd_conv_b', 'ssd_dt_bias', 'ssd_a_log',
            'ssd_d', 'ssd_norm_g', 'mla_q_norm_g', 'mla_w_uq', 'mla_kv_norm_g', 'mla_w_ukv', 'swa_sinks', 'w_out',
            'ffn_w_up', 'ffn_conv_w', 'ffn_conv_b', 'ffn_w_down', 'final_norm_g']
_INPUTS = ['x', 'c', 'positions'] + _WEIGHTS + ['loss_target'] + ['m_' + n for n in _WEIGHTS] + ['v_' + n for n in _WEIGHTS]
_SMALL = [('ada_b', 'mod'), ('norm1_g', 'n1g'), ('norm2_g', 'n2g'), ('ssd_conv_b', 'ssd_cb'), ('ssd_dt_bias', 'dtb'),
          ('ssd_a_log', 'alog'), ('ssd_d', 'dsk'), ('ssd_norm_g', 'ssd_ng'), ('mla_q_norm_g', 'gq'),
          ('mla_kv_norm_g', 'gkv'), ('swa_sinks', 'sinks'), ('ffn_conv_b', 'fcb')]
_SHARDED = [('w_in', 'w_in', 2), ('ssd_conv_w', 'ssd_cw', 2), ('mla_w_uq', 'w_uq', 2), ('mla_w_ukv', 'w_ukv', 2),
            ('w_out', 'w_out', 1), ('ffn_w_up', 'w_up', 2), ('ffn_conv_w', 'fcw', 2), ('ffn_w_down', 'w_down', 1)]
_SHARDED_NAMES = [n for n, _, _ in _SHARDED]
_TRANSPOSED = ('w_in', 'ffn_w_up')


def _shard_major(g, axis):
    shp = g.shape
    g = g.reshape(shp[:axis] + (NDEV, shp[axis] // NDEV) + shp[axis + 1:])
    return jnp.moveaxis(g, axis, 0)


def _unshard(g, axis):
    g = jnp.moveaxis(g, 0, axis)
    shp = g.shape
    return g.reshape(shp[:axis] + (shp[axis] * shp[axis + 1],) + shp[axis + 2:])


def kernel(x, c, positions, ada_w, ada_b, norm1_g, norm2_g, w_in, ssd_conv_w, ssd_conv_b, ssd_dt_bias, ssd_a_log, ssd_d, ssd_norm_g, mla_q_norm_g, mla_w_uq, mla_kv_norm_g, mla_w_ukv, swa_sinks, w_out, ffn_w_up, ffn_conv_w, ffn_conv_b, ffn_w_down, final_norm_g, loss_target, m_ada_w, m_ada_b, m_norm1_g, m_norm2_g, m_w_in, m_ssd_conv_w, m_ssd_conv_b, m_ssd_dt_bias, m_ssd_a_log, m_ssd_d, m_ssd_norm_g, m_mla_q_norm_g, m_mla_w_uq, m_mla_kv_norm_g, m_mla_w_ukv, m_swa_sinks, m_w_out, m_ffn_w_up, m_ffn_conv_w, m_ffn_conv_b, m_ffn_w_down, m_final_norm_g, v_ada_w, v_ada_b, v_norm1_g, v_norm2_g, v_w_in, v_ssd_conv_w, v_ssd_conv_b, v_ssd_dt_bias, v_ssd_a_log, v_ssd_d, v_ssd_norm_g, v_mla_q_norm_g, v_mla_w_uq, v_mla_kv_norm_g, v_mla_w_ukv, v_swa_sinks, v_w_out, v_ffn_w_up, v_ffn_conv_w, v_ffn_conv_b, v_ffn_w_down, v_final_norm_g):
    a = dict(zip(_INPUTS, (x, c, positions, ada_w, ada_b, norm1_g, norm2_g, w_in, ssd_conv_w, ssd_conv_b, ssd_dt_bias, ssd_a_log, ssd_d, ssd_norm_g, mla_q_norm_g, mla_w_uq, mla_kv_norm_g, mla_w_ukv, swa_sinks, w_out, ffn_w_up, ffn_conv_w, ffn_conv_b, ffn_w_down, final_norm_g, loss_target, m_ada_w, m_ada_b, m_norm1_g, m_norm2_g, m_w_in, m_ssd_conv_w, m_ssd_conv_b, m_ssd_dt_bias, m_ssd_a_log, m_ssd_d, m_ssd_norm_g, m_mla_q_norm_g, m_mla_w_uq, m_mla_kv_norm_g, m_mla_w_ukv, m_swa_sinks, m_w_out, m_ffn_w_up, m_ffn_conv_w, m_ffn_conv_b, m_ffn_w_down, m_final_norm_g, v_ada_w, v_ada_b, v_norm1_g, v_norm2_g, v_w_in, v_ssd_conv_w, v_ssd_conv_b, v_ssd_dt_bias, v_ssd_a_log, v_ssd_d, v_ssd_norm_g, v_mla_q_norm_g, v_mla_w_uq, v_mla_kv_norm_g, v_mla_w_ukv, v_swa_sinks, v_w_out, v_ffn_w_up, v_ffn_conv_w, v_ffn_conv_b, v_ffn_w_down, v_final_norm_g)))
    axes = ("x", "y", "c")
    me = 4 * lax.axis_index("x") + 2 * lax.axis_index("y") + lax.axis_index("c")
    ncol = ada_w.shape[-1]

    kform = lambda n, t: jnp.swapaxes(t, -1, -2) if n in _TRANSPOSED else t
    mxu_names = ('w_in', 'mla_w_uq', 'mla_w_ukv', 'w_out', 'ffn_w_up', 'ffn_w_down')
    gather_groups = (("early", _SHARDED_NAMES[:4]), ("mid", _SHARDED_NAMES[4:5]), ("late", _SHARDED_NAMES[5:]))

    def own_of(src, names, l):
        return [kform(n, src[n][l]).astype(_MXU) if n in mxu_names else src[n][l] for n in names]

    first_gather = _xchg_start(own_of(a, gather_groups[0][1], 0), scatter=False, name="gather_start_early0")

    c_all = _exchange([c + first_gather["token"][0, 0]], scatter=False, name="gather_c")[0]
    c_act = _silu_call(c_all.reshape(NDEV, D), name="c_act")
    mod_part = jnp.stack([_mm(c_act, ada_w[l], name=f"mod{l}") for l in range(DEPTH)])
    mod_all = _exchange([mod_part], scatter=False, name="gather_mod")[0]
    mod_mine = lax.dynamic_index_in_dim(mod_all, me, axis=2, keepdims=False)
    mods = (jnp.moveaxis(mod_mine, 0, 1).reshape(DEPTH, 6 * D) + ada_b).reshape(DEPTH, 6, D)
    tabs = _rope_tables(positions)

    shard_of = {n: (key, 1 if n in _TRANSPOSED else ax) for n, key, ax in _SHARDED}
    mods, raw = lax.optimization_barrier((mods, {n: a[n] for n in _SHARDED_NAMES}))
    gathers, prev = [], first_gather["token"]
    for l in range(DEPTH):
        gathers.append({})
        for grp, names in gather_groups:
            if (l, grp) == (0, "early"):
                gathers[l][grp] = first_gather
                continue
            srcs, _ = lax.optimization_barrier((own_of(raw, names, l), prev))
            gathers[l][grp] = _xchg_start(srcs, scatter=False, name=f"gather_start_{grp}{l}")
            prev = gathers[l][grp]["token"]

    def place_own(landed, mine):
        slot = lambda t: lax.broadcasted_iota(jnp.int32, (NDEV,) + (1,) * (t.ndim - 1), 0)
        return [jnp.where(slot(t) == me, o[None], t) for t, o in zip(landed, mine)]

    def gathered(l, grp, after):
        names = dict(gather_groups)[grp]
        mine, landed = _xchg_wait(gathers[l][grp], after, name=f"gather_wait_{grp}{l}")
        return {n: _unshard(g, shard_of[n][1] - 1) for n, g in zip(names, place_own(landed, mine))}

    def get_params(l, after):
        full = gathered(l, "early", mods if l == 0 else after)
        vec = lambda t: t[l].reshape(1, -1)

        def mid(after2):
            return dict(w_out=_w_out_to_padded(gathered(l, "mid", after2)['w_out']))

        def late(after2):
            rest = gathered(l, "late", after2)
            return dict(w_up=rest['ffn_w_up'], w_down=rest['ffn_w_down'], fcw=rest['ffn_conv_w'])

        return dict(
            w_in=_w_in_to_padded(full['w_in'], axis=0), w_uq=_pad_heads(full['mla_w_uq'], MLA_HEADS, MLA_NOPE + MLA_ROPE),
            w_ukv=_w_ukv_to_padded(full['mla_w_ukv']), ssd_cw=full['ssd_conv_w'], mid=mid, late=late,
            ssd_cb=vec(ssd_conv_b), dtb=vec(_pad_lane(ssd_dt_bias)), alog=vec(_pad_lane(ssd_a_log)),
            dsk=vec(_pad_lane(ssd_d)), ssd_ng=vec(ssd_norm_g), gq=vec(mla_q_norm_g), gkv=vec(mla_kv_norm_g),
            sinks=vec(_pad_lane(swa_sinks)), fcb=vec(ffn_conv_b), n1g=vec(norm1_g), n2g=vec(norm2_g))

    unpad = dict(w_in=functools.partial(_w_in_from_padded, axis=0), w_out=_w_out_from_padded, w_ukv=_w_ukv_from_padded,
                 w_uq=lambda g: _unpad_heads(g, MLA_HEADS, MLA_NOPE + MLA_ROPE))
    scatter_groups = (("ffn", _SHARDED_NAMES[5:]), ("out", _SHARDED_NAMES[4:5]), ("mixer", _SHARDED_NAMES[:4]))
    grads = [None] * DEPTH
    scatters = [dict() for _ in range(DEPTH)]

    def on_part(l, grp, G):
        parts = [_shard_major(unpad.get(shard_of[n][0], lambda g: g)(G[shard_of[n][0]]), shard_of[n][1] - 1).astype(_ACT)
                 for n in dict(scatter_groups)[grp]]
        scatters[l][grp] = _xchg_start(parts, scatter=True, name=f"scatter_start_{grp}{l}")
        return scatters[l][grp]["token"][0, 0]

    def on_grads(l, G):
        grads[l] = G

    mods = mods + sum(g[grp]["token"][0, 0] for g in gathers for grp, _ in gather_groups)
    loss, dx, dfinal = _local_step(x[0], loss_target[0], mods, get_params, tabs, final_norm_g.reshape(1, D),
                                   on_grads, on_part)
    loss = lax.psum(loss, axes)

    stack = lambda key: jnp.stack([grads[l][key] for l in range(DEPTH)])
    small_names = [n for n, _ in _SMALL] + ['final_norm_g']
    small_g = [stack(key).reshape(DEPTH, -1)[:, :a[name].shape[1]] for name, key in _SMALL] + [dfinal]
    small_gather = _xchg_start(small_g, scatter=False, name="gather_small_start")

    out_g, out_d, out_m, out_v = {}, {}, {}, {}
    chain = {name: None for name in _SHARDED_NAMES}
    me_arr = jnp.reshape(me, (1,)).astype(jnp.int32)
    after = small_gather["token"]
    for l in reversed(range(DEPTH)):
        for grp, names in scatter_groups:
            mine, landed = _xchg_wait(scatters[l][grp], after, name=f"scatter_wait_{grp}{l}")
            for name, own, got in zip(names, mine, landed):
                chain[name] = _adamw_layer(l, kform(name, a[name]), kform(name, a['m_' + name]),
                                           kform(name, a['v_' + name]), got, own, me_arr, chain[name],
                                           name=f"adamw_{name}{l}")
    for name in _SHARDED_NAMES:
        out_g[name], out_d[name], out_m[name], out_v[name] = [kform(name, t) for t in chain[name]]
    small_mine, small_landed = _xchg_wait(small_gather, chain[_SHARDED_NAMES[0]][0], name="gather_small_wait")
    row = lambda t: t.reshape(1, -1) if t.ndim == 1 else t
    res = _adamw_many([row(a[n]) for n in small_names], [row(a['m_' + n]) for n in small_names],
                      [row(a['v_' + n]) for n in small_names], small_landed, small_mine, me_arr, name="adamw_small")
    for i, n in enumerate(small_names):
        out_g[n], out_d[n], out_m[n], out_v[n] = [t.reshape(a[n].shape) for t in res[4 * i:4 * i + 4]]

    dmod_all = place_own(small_landed[:1], small_mine[:1])[0]
    dmod_mine = lax.dynamic_slice_in_dim(dmod_all, me * ncol, ncol, axis=2)
    g_ada = jnp.stack([_mm(c_act, dmod_mine[:, l], ta=True, name=f"dw_ada{l}") for l in range(DEPTH)])
    shp = ada_w.shape
    res = _adamw(*[t.reshape(-1, shp[-1]) for t in (ada_w, m_ada_w, v_ada_w)], g_ada.reshape(1, -1, shp[-1]),
                 name="adamw_ada_w")
    out_g['ada_w'], out_d['ada_w'], out_m['ada_w'], out_v['ada_w'] = [t.reshape(shp) for t in res]

    outs = [loss, dx[None]]
    for dct in (out_g, out_d, out_m, out_v):
        outs += [dct[n] for n in _WEIGHTS]
    return tuple(outs)
```

```python
import functools
import math

import jax
import jax.numpy as jnp
from jax import lax
from jax.experimental import pallas as pl
from jax.experimental.pallas import tpu as pltpu

F32 = jnp.float32
_MXU = jnp.bfloat16
_ACT = jnp.bfloat16
_HI = lax.Precision.HIGHEST
EPS = 1e-6
NDEV = 8
DEPTH = 4
D = 1024
LANE = 128
SUB = 8
VMEM_LIMIT = 56 * 1024 * 1024

SSD_INNER, SSD_STATE, SSD_HEADS, SSD_GROUPS, SSD_CHUNK, SSD_CONV = 512, 128, 8, 2, 128, 4
SSD_XBC = SSD_INNER + 2 * SSD_GROUPS * SSD_STATE
MLA_HEADS, MLA_NOPE, MLA_ROPE, MLA_V, MLA_QR, MLA_KVR = 4, 64, 32, 64, 256, 128
SWA_HEADS, SWA_KV, SWA_HD, WINDOW = 4, 2, 64, 128
D_FF, FFN_CONV = 2816, 3
D_IN = 2472
ROPE_THETA = 10000.0
C_XBC, C_Z, C_CQ, C_CKV, C_MISC, C_SQ, C_SK, C_SV, D_INP = 0, 1024, 1536, 1792, 1920, 2048, 2560, 2816, 3072
ROPE_LANE = 64
D_MIXP = 1536

ADAM_LR, ADAM_B1, ADAM_B2, ADAM_EPS, ADAM_WD, ADAM_STEP = 0.001, 0.9, 0.999, 1e-08, 0.01, 10

TS_ROW = 1024
TS_FFN = 256
TQ_ATT = 1024
TS_SWA = 512


def _tile(n, cap, q=LANE):
    best = None
    for t in range(q, min(n, cap) + 1, q):
        if n % t == 0:
            best = t
    return n if best is None else best


def _cp(ngrid):
    return pltpu.CompilerParams(dimension_semantics=("arbitrary",) * ngrid, vmem_limit_bytes=VMEM_LIMIT)


def _dot(a, b):
    return jnp.dot(a.astype(_MXU), b.astype(_MXU), preferred_element_type=F32)


def _dot_nt(a, b):
    return lax.dot_general(a.astype(_MXU), b.astype(_MXU), (((1,), (1,)), ((), ())), preferred_element_type=F32)


def _dot_tn(a, b):
    return jnp.dot(a.T.astype(_MXU), b.astype(_MXU), preferred_element_type=F32)


def _sigmoid(x):
    return 1.0 / (1.0 + jnp.exp(-x))


def _sigmoid_t(x):
    return 0.5 * jnp.tanh(0.5 * x) + 0.5


def _silu(x):
    return x * _sigmoid_t(x)


def _silu_grad(x):
    s = _sigmoid_t(x)
    return x * s, s * (1.0 + x * (1.0 - s))


def _dsilu(x):
    return _silu_grad(x)[1]


def _softplus(x):
    u = jnp.exp(-jnp.abs(x))
    w = 1.0 + u
    log1p = jnp.where(w == 1.0, u, jnp.log(w) * u / jnp.where(w == 1.0, 1.0, w - 1.0))
    return jnp.maximum(x, 0.0) + log1p


def _colsum(x):
    return jnp.sum(x, axis=0, keepdims=True)


def _rowsum(x):
    return jnp.sum(x, axis=1, keepdims=True)


def _shift_down(t, halo, j):
    if j == 0:
        return t
    n = t.shape[0]
    rolled = pltpu.roll(t, j, 0)
    row = lax.broadcasted_iota(jnp.int32, (SUB, t.shape[1]), 0)
    first = jnp.where(row < j, pltpu.roll(halo, j, 0), rolled[:SUB])
    return jnp.concatenate([first, rolled[SUB:]], axis=0) if n > SUB else first


def _shift_up(t, halo, j):
    if j == 0:
        return t
    n = t.shape[0]
    rolled = pltpu.roll(t, n - j, 0)
    row = lax.broadcasted_iota(jnp.int32, (SUB, t.shape[1]), 0)
    last = jnp.where(row >= SUB - j, pltpu.roll(halo, SUB - j, 0), rolled[n - SUB:])
    return jnp.concatenate([rolled[:n - SUB], last], axis=0) if n > SUB else last


def _mm(a, b, *, ta=False, tb=False, out_dtype=F32, name):
    if ta:
        K, M = a.shape
    else:
        M, K = a.shape
    if tb:
        N, K2 = b.shape
    else:
        K2, N = b.shape
    assert K == K2, (a.shape, b.shape, ta, tb)
    tk = _tile(K, 1536)
    nk = K // tk
    tm, tn = _tile(M, 2048 if nk == 1 else 1536), _tile(N, 1536 if nk == 1 else 1408)
    dn = (((0 if ta else 1,), (1 if tb else 0,)), ((), ()))

    def body(a_ref, b_ref, o_ref, *acc):
        part = lax.dot_general(a_ref[...].astype(_MXU), b_ref[...].astype(_MXU), dn, preferred_element_type=F32)
        if nk == 1:
            o_ref[...] = part.astype(out_dtype)
            return
        acc_ref, = acc
        k = pl.program_id(2)

        @pl.when(k == 0)
        def _():
            acc_ref[...] = part

        @pl.when(k > 0)
        def _():
            acc_ref[...] += part

        @pl.when(k == nk - 1)
        def _():
            o_ref[...] = acc_ref[...].astype(out_dtype)

    a_spec = pl.BlockSpec((tk, tm), lambda i, j, k: (k, i)) if ta else pl.BlockSpec((tm, tk), lambda i, j, k: (i, k))
    b_spec = pl.BlockSpec((tn, tk), lambda i, j, k: (j, k)) if tb else pl.BlockSpec((tk, tn), lambda i, j, k: (k, j))
    return pl.pallas_call(
        body, grid=(M // tm, N // tn, nk), in_specs=[a_spec, b_spec],
        out_specs=pl.BlockSpec((tm, tn), lambda i, j, k: (i, j)),
        out_shape=jax.ShapeDtypeStruct((M, N), out_dtype),
        scratch_shapes=[pltpu.VMEM((tm, tn), F32)] * (nk > 1), compiler_params=_cp(3), name=name)(a, b)


def _row(ts, w, col=0):
    return pl.BlockSpec((ts, w), lambda i: (i, col))


def _vec(w, r=1):
    return pl.BlockSpec((r, w), lambda i: (0, 0))


def _silu_call(x, name):
    def body(x_ref, o_ref):
        o_ref[...] = _silu(x_ref[...])
    return pl.pallas_call(body, out_shape=jax.ShapeDtypeStruct(x.shape, F32), name=name)(x)


def _norm_fwd(x, g, sc, sh, *, f=None, gate=None, name):
    S, dm = x.shape
    ts = _tile(S, TS_ROW, SUB)
    res = f is not None

    def body(*refs):
        if res:
            x_ref, f_ref, gate_ref, g_ref, sc_ref, sh_ref, xo_ref, h_ref = refs
            xv = x_ref[...] + gate_ref[...] * f_ref[...]
            xo_ref[...] = xv
        else:
            x_ref, g_ref, sc_ref, sh_ref, h_ref = refs
            xv = x_ref[...]
        rstd = lax.rsqrt(jnp.mean(xv * xv, axis=-1, keepdims=True) + EPS)
        h_ref[...] = ((xv * rstd) * g_ref[...] * (1.0 + sc_ref[...]) + sh_ref[...]).astype(_ACT)

    ins = [x] + ([f, gate] if res else []) + [g, sc, sh]
    in_specs = [_row(ts, dm)] + ([_row(ts, dm), _vec(dm)] if res else []) + [_vec(dm)] * 3
    h_shape = jax.ShapeDtypeStruct((S, dm), _ACT)
    if res:
        out_shape, out_specs = (jax.ShapeDtypeStruct((S, dm), F32), h_shape), (_row(ts, dm), _row(ts, dm))
    else:
        out_shape, out_specs = h_shape, _row(ts, dm)
    return pl.pallas_call(body, grid=(S // ts,), in_specs=in_specs, out_specs=out_specs, out_shape=out_shape,
                          compiler_params=_cp(1), name=name)(*ins)


def _norm_bwd(dh, x, dres, g, sc, *, name):
    S, dm = x.shape
    ts = _tile(S, TS_ROW, SUB)

    def body(dh_ref, x_ref, dres_ref, g_ref, sc_ref, dx_ref, dg_ref, dsc_ref, dsh_ref):
        i = pl.program_id(0)
        xv = x_ref[...]
        dhv = dh_ref[...]
        rstd = lax.rsqrt(jnp.mean(xv * xv, axis=-1, keepdims=True) + EPS)
        xhat = xv * rstd
        hn = xhat * g_ref[...]
        dhn = dhv * (1.0 + sc_ref[...])
        dxh = dhn * g_ref[...]
        dx_ref[...] = dres_ref[...] + rstd * (dxh - xhat * jnp.mean(dxh * xhat, axis=-1, keepdims=True))

        @pl.when(i == 0)
        def _():
            dg_ref[...] = jnp.zeros_like(dg_ref)
            dsc_ref[...] = jnp.zeros_like(dsc_ref)
            dsh_ref[...] = jnp.zeros_like(dsh_ref)

        dg_ref[...] += _colsum(dhn * xhat)
        dsc_ref[...] += _colsum(dhv * hn)
        dsh_ref[...] += _colsum(dhv)

    vshape = jax.ShapeDtypeStruct((1, dm), F32)
    return pl.pallas_call(
        body, grid=(S // ts,), in_specs=[_row(ts, dm)] * 3 + [_vec(dm)] * 2,
        out_specs=(_row(ts, dm), _vec(dm), _vec(dm), _vec(dm)),
        out_shape=(jax.ShapeDtypeStruct((S, dm), F32), vshape, vshape, vshape),
        compiler_params=_cp(1), name=name)(dh, x, dres, g, sc)


def _gate_bwd(dxo, f, gate, *, name):
    S, dm = f.shape
    ts = _tile(S, TS_ROW, SUB)

    def body(dxo_ref, f_ref, gate_ref, df_ref, dgate_ref):
        i = pl.program_id(0)
        dv = dxo_ref[...]
        df_ref[...] = (gate_ref[...] * dv).astype(_ACT)

        @pl.when(i == 0)
        def _():
            dgate_ref[...] = jnp.zeros_like(dgate_ref)

        dgate_ref[...] += _colsum(dv * f_ref[...])

    return pl.pallas_call(
        body, grid=(S // ts,), in_specs=[_row(ts, dm), _row(ts, dm), _vec(dm)],
        out_specs=(_row(ts, dm), _vec(dm)),
        out_shape=(jax.ShapeDtypeStruct((S, dm), _ACT), jax.ShapeDtypeStruct((1, dm), F32)),
        compiler_params=_cp(1), name=name)(dxo, f, gate)


def _final_loss(x, f, gate, g, tgt, *, name):
    S, dm = x.shape
    ts = _tile(S, TS_ROW, SUB)

    def body(x_ref, f_ref, gate_ref, g_ref, t_ref, loss_ref, dx_ref, dg_ref):
        i = pl.program_id(0)
        xv = x_ref[...] + gate_ref[...] * f_ref[...]
        rstd = lax.rsqrt(jnp.mean(xv * xv, axis=-1, keepdims=True) + EPS)
        xhat = xv * rstd
        err = xhat * g_ref[...] - t_ref[...]
        dy = err * (1.0 / dm)
        dxh = dy * g_ref[...]
        dx_ref[...] = rstd * (dxh - xhat * jnp.mean(dxh * xhat, axis=-1, keepdims=True))

        @pl.when(i == 0)
        def _():
            loss_ref[...] = jnp.zeros_like(loss_ref)
            dg_ref[...] = jnp.zeros_like(dg_ref)

        loss_ref[...] += jnp.full((1, LANE), 0.5 * jnp.sum(jnp.mean(err * err, axis=-1, keepdims=True)), F32)
        dg_ref[...] += _colsum(dy * xhat)

    return pl.pallas_call(
        body, grid=(S // ts,), in_specs=[_row(ts, dm), _row(ts, dm), _vec(dm), _vec(dm), _row(ts, dm)],
        out_specs=(_vec(LANE), _row(ts, dm), _vec(dm)),
        out_shape=(jax.ShapeDtypeStruct((1, LANE), F32), jax.ShapeDtypeStruct((S, dm), F32),
                   jax.ShapeDtypeStruct((1, dm), F32)),
        compiler_params=_cp(1), name=name)(x, f, gate, g, tgt)


def _ffn_conv(t, halo, cw_ref, cb_ref):
    t1, t2 = _shift_down(t, halo, 1), _shift_down(t, halo, 2)
    return ((cb_ref[...] + t2 * cw_ref[0:1, :]) + t1 * cw_ref[1:2, :]) + t * cw_ref[2:3, :], t1, t2


def _prev_halo_spec(ts, w, col=0):
    return pl.BlockSpec((SUB, w), lambda i: (jnp.maximum(i * (ts // SUB) - 1, 0), col))


def _ffn_act_fwd(up, cw, cb, *, name):
    S, w2 = up.shape
    ff = w2 // 2
    ts = _tile(S, TS_FFN, SUB)

    def body(up_ref, halo_ref, cw_ref, cb_ref, act_ref):
        i = pl.program_id(0)
        t = up_ref[...]
        halo = jnp.where(i > 0, halo_ref[...], 0.0)
        u, _, _ = _ffn_conv(t, halo, cw_ref, cb_ref)
        act_ref[...] = (_silu(u[:, :ff]) * u[:, ff:]).astype(_ACT)

    return pl.pallas_call(
        body, grid=(S // ts,), in_specs=[_row(ts, w2), _prev_halo_spec(ts, w2), _vec(w2, FFN_CONV), _vec(w2)],
        out_specs=_row(ts, ff), out_shape=jax.ShapeDtypeStruct((S, ff), _ACT),
        compiler_params=_cp(1), name=name)(up, up, cw, cb)


def _ffn_bwd(up, dact, cw, cb, *, name):
    S, w2 = up.shape
    ff = w2 // 2
    ts = _tile(S, TS_FFN, SUB)
    n = S // ts

    def body(up_ref, halo_ref, dact_ref, cw_ref, cb_ref, dup_ref, dcw_ref, dcb_ref, carry_ref):
        i = pl.program_id(0)
        t_idx = n - 1 - i

        @pl.when(i == 0)
        def _():
            carry_ref[...] = jnp.zeros_like(carry_ref)
            dcw_ref[...] = jnp.zeros_like(dcw_ref)
            dcb_ref[...] = jnp.zeros_like(dcb_ref)

        t = up_ref[...]
        halo = jnp.where(t_idx > 0, halo_ref[...], 0.0)
        u, t1, t2 = _ffn_conv(t, halo, cw_ref, cb_ref)
        a, b = u[:, :ff], u[:, ff:]
        da = dact_ref[...]
        sa, dsa = _silu_grad(a)
        dv = jnp.concatenate([da * b * dsa, da * sa], axis=1)
        nxt = carry_ref[...]
        dup = (dv * cw_ref[2:3, :] + _shift_up(dv, nxt, 1) * cw_ref[1:2, :]) + _shift_up(dv, nxt, 2) * cw_ref[0:1, :]
        dup_ref[...] = dup.astype(_ACT)
        dcb_ref[...] += _colsum(dv)
        dcw_ref[2:3, :] += _colsum(dv * t)
        dcw_ref[1:2, :] += _colsum(dv * t1)
        dcw_ref[0:1, :] += _colsum(dv * t2)
        carry_ref[...] = dv[:SUB]

    rev = lambda w: pl.BlockSpec((ts, w), lambda i: (n - 1 - i, 0))
    halo_spec = pl.BlockSpec((SUB, w2), lambda i: (jnp.maximum((n - 1 - i) * (ts // SUB) - 1, 0), 0))
    return pl.pallas_call(
        body, grid=(n,), in_specs=[rev(w2), halo_spec, rev(ff), _vec(w2, FFN_CONV), _vec(w2)],
        out_specs=(rev(w2), _vec(w2, FFN_CONV), _vec(w2)),
        out_shape=(jax.ShapeDtypeStruct((S, w2), _ACT), jax.ShapeDtypeStruct((FFN_CONV, w2), F32),
                   jax.ShapeDtypeStruct((1, w2), F32)),
        scratch_shapes=[pltpu.VMEM((SUB, w2), F32)], compiler_params=_cp(1), name=name)(up, up, dact, cw, cb)


def _ssd_core(pre, halo, misc, cw_ref, cb_ref, dtb, alog):
    q = pre.shape[0]
    conv = cb_ref[...]
    for k in range(SSD_CONV):
        conv = conv + _shift_down(pre, halo, SSD_CONV - 1 - k) * cw_ref[k:k + 1, :]
    xbc = _silu(conv)
    raw = misc + dtb
    dt = _softplus(raw)
    a = -jnp.exp(alog)
    r = lax.broadcasted_iota(jnp.int32, (q, q), 0)
    c = lax.broadcasted_iota(jnp.int32, (q, q), 1)
    tri = r >= c
    acum = jnp.dot(tri.astype(F32), dt * a, precision=_HI, preferred_element_type=F32)
    return conv, xbc, raw, dt, a, acum, acum.T, tri


def _sel(v, j, lo):
    return jnp.where(lo, v[:, 2 * j:2 * j + 1], v[:, 2 * j + 1:2 * j + 2])


def _ssd_pair_fwd(xbc, dt, acum, acum_t, tri, dsk, g_mat, b_mat, c_mat, h_pair, j, lo, lo1, sub_lo):
    q = xbc.shape[0]
    x = xbc[:, LANE * j:LANE * (j + 1)]
    dtp = _sel(dt, j, lo)
    ap = _sel(acum, j, lo)
    xd = x * dtp
    ls, ms = [], []
    for h in (2 * j, 2 * j + 1):
        seg = acum[:, h:h + 1] - acum_t[h:h + 1, :]
        l_mat = jnp.exp(jnp.where(tri, seg, -jnp.inf))
        ls.append(l_mat)
        ms.append(g_mat * l_mat)
    yd = jnp.where(lo, _dot(ms[0], xd), _dot(ms[1], xd))
    ea = jnp.exp(ap)
    yo = _dot_nt(c_mat, h_pair) * ea
    dp = _sel(dsk, j, lo1)
    alast = acum[q - 1:q, :]
    e = jnp.exp(_sel(alast, j, lo1) - ap)
    cd = jnp.where(sub_lo, jnp.exp(alast[:, 2 * j:2 * j + 1]), jnp.exp(alast[:, 2 * j + 1:2 * j + 2]))
    return dict(x=x, dtp=dtp, ap=ap, xd=xd, ls=ls, ms=ms, ea=ea, yo=yo, dp=dp, e=e, cd=cd, y=yd + yo + x * dp)


def _gnorm(yg):
    half = SSD_INNER // SSD_GROUPS
    rstds, yns = [], []
    for g in range(SSD_GROUPS):
        part = yg[:, half * g:half * (g + 1)]
        rstd = lax.rsqrt(jnp.mean(part * part, axis=-1, keepdims=True) + EPS)
        rstds.append(rstd)
        yns.append(part * rstd)
    return rstds, yns


def _ssd_specs(nc, rev):
    q = SSD_CHUNK
    cidx = (lambda i: nc - 1 - i) if rev else (lambda i: i)
    return [
        pl.BlockSpec((q, SSD_XBC), lambda i: (cidx(i), C_XBC // SSD_XBC)),
        pl.BlockSpec((SUB, SSD_XBC), lambda i: (jnp.maximum(cidx(i) * (q // SUB) - 1, 0), C_XBC // SSD_XBC)),
        pl.BlockSpec((q, SSD_INNER), lambda i: (cidx(i), C_Z // SSD_INNER)),
        pl.BlockSpec((q, LANE), lambda i: (cidx(i), C_MISC // LANE)),
    ]


def _ssd_param_specs():
    return [_vec(SSD_XBC, SSD_CONV), _vec(SSD_XBC), _vec(LANE), _vec(LANE), _vec(LANE), _vec(SSD_INNER)]


def _ssd_fwd(proj, cw, cb, dtb, alog, dsk, ng, *, name):
    S = proj.shape[0]
    q = SSD_CHUNK
    nc = S // q
    npair = SSD_HEADS // 2

    def body(xbc_ref, halo_ref, z_ref, misc_ref, cw_ref, cb_ref, dtb_ref, alog_ref, dsk_ref, ng_ref,
             y_ref, hin_ref, h_ref):
        c = pl.program_id(0)

        @pl.when(c == 0)
        def _():
            h_ref[...] = jnp.zeros_like(h_ref)

        pre = xbc_ref[...]
        halo = jnp.where(c > 0, halo_ref[...], 0.0)
        conv, xbc, raw, dt, a, acum, acum_t, tri = _ssd_core(pre, halo, misc_ref[...], cw_ref, cb_ref,
                                                             dtb_ref[...], alog_ref[...])
        lo = lax.broadcasted_iota(jnp.int32, (q, LANE), 1) < LANE // 2
        lo1 = lo[:1]
        sub_lo = lax.broadcasted_iota(jnp.int32, (LANE, LANE), 0) < LANE // 2
        ys = []
        for g in range(SSD_GROUPS):
            b_mat = xbc[:, SSD_INNER + SSD_STATE * g:SSD_INNER + SSD_STATE * (g + 1)]
            c_mat = xbc[:, SSD_INNER + SSD_STATE * (SSD_GROUPS + g):SSD_INNER + SSD_STATE * (SSD_GROUPS + g + 1)]
            g_mat = _dot_nt(c_mat, b_mat)
            for jj in range(npair // SSD_GROUPS):
                j = g * (npair // SSD_GROUPS) + jj
                hj = h_ref[j]
                p = _ssd_pair_fwd(xbc, dt, acum, acum_t, tri, dsk_ref[...], g_mat, b_mat, c_mat, hj, j, lo, lo1, sub_lo)
                ys.append(p["y"])
                hin_ref[0, j] = hj
                h_ref[j] = p["cd"] * hj + _dot_tn(p["xd"] * p["e"], b_mat)
        yg = jnp.concatenate(ys, axis=1) * _silu(z_ref[...])
        _, yns = _gnorm(yg)
        y_ref[...] = jnp.concatenate(yns, axis=1) * ng_ref[...]

    return pl.pallas_call(
        body, grid=(nc,), in_specs=_ssd_specs(nc, False) + _ssd_param_specs(),
        out_specs=(pl.BlockSpec((q, SSD_INNER), lambda i: (i, 0)),
                   pl.BlockSpec((1, npair, LANE, LANE), lambda i: (i, 0, 0, 0))),
        out_shape=(jax.ShapeDtypeStruct((S, SSD_INNER), F32), jax.ShapeDtypeStruct((nc, npair, LANE, LANE), F32)),
        scratch_shapes=[pltpu.VMEM((npair, LANE, LANE), F32)], compiler_params=_cp(1), name=name,
    )(proj, proj, proj, proj, cw, cb, dtb, alog, dsk, ng)


def _ssd_bwd(proj, dycat, hin, cw, cb, dtb, alog, dsk, ng, *, name):
    S = proj.shape[0]
    q = SSD_CHUNK
    nc = S // q
    npair = SSD_HEADS // 2
    ppg = npair // SSD_GROUPS

    def body(xbc_ref, halo_ref, z_ref, misc_ref, dy_ref, hin_ref, cw_ref, cb_ref, dtb_ref, alog_ref, dsk_ref, ng_ref,
             dpre_ref, dz_ref, dmisc_ref, dcw_ref, dcb_ref, ddtb_ref, dalog_ref, ddsk_ref, dng_ref,
             dh_ref, carry_ref):
        i = pl.program_id(0)
        c = nc - 1 - i

        @pl.when(i == 0)
        def _():
            dh_ref[...] = jnp.zeros_like(dh_ref)
            carry_ref[...] = jnp.zeros_like(carry_ref)
            for r in (dcw_ref, dcb_ref, ddtb_ref, dalog_ref, ddsk_ref, dng_ref):
                r[...] = jnp.zeros_like(r)

        pre = xbc_ref[...]
        halo = jnp.where(c > 0, halo_ref[...], 0.0)
        conv, xbc, raw, dt, a, acum, acum_t, tri = _ssd_core(pre, halo, misc_ref[...], cw_ref, cb_ref,
                                                             dtb_ref[...], alog_ref[...])
        lane = lax.broadcasted_iota(jnp.int32, (q, LANE), 1)
        lane1 = lane[:1]
        rowi = lax.broadcasted_iota(jnp.int32, (q, LANE), 0)
        lastrow = rowi == q - 1
        lo = lane < LANE // 2
        lo1 = lo[:1]
        sub_lo = lax.broadcasted_iota(jnp.int32, (LANE, LANE), 0) < LANE // 2
        dsk = dsk_ref[...]
        alast = acum[q - 1:q, :]

        def halves(t):
            return _rowsum(jnp.where(lo, t, 0.0)), _rowsum(jnp.where(lo, 0.0, t))

        def put(ha, va, vb):
            ln = lane if va.shape[0] == q else lane1
            return jnp.where(ln == ha, va, 0.0) + jnp.where(ln == ha + 1, vb, 0.0)

        mats, pairs = [], []
        for g in range(SSD_GROUPS):
            b_mat = xbc[:, SSD_INNER + SSD_STATE * g:SSD_INNER + SSD_STATE * (g + 1)]
            c_mat = xbc[:, SSD_INNER + SSD_STATE * (SSD_GROUPS + g):SSD_INNER + SSD_STATE * (SSD_GROUPS + g + 1)]
            g_mat = _dot_nt(c_mat, b_mat)
            mats.append((b_mat, c_mat, g_mat))
            for jj in range(ppg):
                j = g * ppg + jj
                pairs.append(_ssd_pair_fwd(xbc, dt, acum, acum_t, tri, dsk, g_mat, b_mat, c_mat, hin_ref[0, j],
                                           j, lo, lo1, sub_lo))
        z = z_ref[...]
        sz, dsz = _silu_grad(z)
        y = jnp.concatenate([p["y"] for p in pairs], axis=1)
        rstds, yns = _gnorm(y * sz)
        dout = dy_ref[...]
        dng_ref[...] += _colsum(dout * jnp.concatenate(yns, axis=1))
        dyn = dout * ng_ref[...]
        half = SSD_INNER // SSD_GROUPS
        dygs = []
        for g in range(SSD_GROUPS):
            dyn_g = dyn[:, half * g:half * (g + 1)]
            dygs.append(rstds[g] * (dyn_g - yns[g] * jnp.mean(dyn_g * yns[g], axis=-1, keepdims=True)))
        dyg = jnp.concatenate(dygs, axis=1)
        dyv = dyg * sz
        dz_ref[...] = (dyg * y * dsz).astype(_ACT)

        da_acc = jnp.zeros((q, LANE), F32)
        ddt = jnp.zeros((q, LANE), F32)
        dds = jnp.zeros((1, LANE), F32)
        dxs, dbs, dcs = [], [], []
        for g in range(SSD_GROUPS):
            b_mat, c_mat, g_mat = mats[g]
            dg_mat = jnp.zeros((q, q), F32)
            db = jnp.zeros((q, SSD_STATE), F32)
            dc = jnp.zeros((q, SSD_STATE), F32)
            for jj in range(ppg):
                j = g * ppg + jj
                ha = 2 * j
                p = pairs[j]
                hj = hin_ref[0, j]
                dyp = dyv[:, LANE * j:LANE * (j + 1)]
                dsum = _colsum(dyp * p["x"])
                dds = dds + put(ha, _rowsum(jnp.where(lo1, dsum, 0.0)), _rowsum(jnp.where(lo1, 0.0, dsum)))
                dx = dyp * p["dp"]
                dw = dyp * p["ea"]
                dc = dc + _dot(dw, hj)
                dh_yo = _dot_tn(dw, c_mat)
                ra, rb = halves(dyp * p["yo"])
                da_acc = da_acc + put(ha, ra, rb)
                dxd = jnp.zeros((q, LANE), F32)
                for idx in range(2):
                    dyh = jnp.where(lo, dyp, 0.0) if idx == 0 else jnp.where(lo, 0.0, dyp)
                    dm = _dot_nt(dyh, p["xd"])
                    dxd = dxd + _dot_tn(p["ms"][idx], dyh)
                    dg_mat = dg_mat + dm * p["ls"][idx]
                    t = dm * p["ms"][idx]
                    da_h = _rowsum(t) - _rowsum(t.T)
                    da_acc = da_acc + jnp.where(lane == ha + idx, da_h, 0.0)
                dhn = dh_ref[j]
                s = _rowsum(dhn * hj)
                sa = jnp.sum(jnp.where(sub_lo[:, :1], s, 0.0), keepdims=True)
                sb = jnp.sum(jnp.where(sub_lo[:, :1], 0.0, s), keepdims=True)
                cda, cdb = jnp.exp(alast[:, ha:ha + 1]), jnp.exp(alast[:, ha + 1:ha + 2])
                db = db + _dot(p["xd"] * p["e"], dhn)
                r = _dot_nt(b_mat, dhn)
                dxd = dxd + r * p["e"]
                qa, qb = halves(r * p["xd"] * p["e"])
                da_acc = da_acc - put(ha, qa, qb)
                tot_a = sa * cda + jnp.sum(qa, keepdims=True)
                tot_b = sb * cdb + jnp.sum(qb, keepdims=True)
                da_acc = da_acc + jnp.where(lastrow, put(ha, tot_a, tot_b), 0.0)
                dh_ref[j] = p["cd"] * dhn + dh_yo
                dx = dx + dxd * p["dtp"]
                ua, ub = halves(dxd * p["x"])
                ddt = ddt + put(ha, ua, ub)
                dxs.append(dx)
            dc = dc + _dot(dg_mat, b_mat)
            db = db + _dot_tn(dg_mat, c_mat)
            dbs.append(db)
            dcs.append(dc)
        r2 = lax.broadcasted_iota(jnp.int32, (q, q), 0)
        c2 = lax.broadcasted_iota(jnp.int32, (q, q), 1)
        dda = jnp.dot((c2 >= r2).astype(F32), da_acc, precision=_HI, preferred_element_type=F32)
        ddt = ddt + dda * a
        dalog_ref[...] += _colsum(dda * dt) * a
        ddsk_ref[...] += dds
        draw = jnp.where(lane < SSD_HEADS, ddt * _sigmoid(raw), 0.0)
        ddtb_ref[...] += _colsum(draw)
        dmisc_ref[...] = draw
        dconv = jnp.concatenate(dxs + dbs + dcs, axis=1) * _dsilu(conv)
        dcb_ref[...] += _colsum(dconv)
        nxt = carry_ref[...]
        dpre = jnp.zeros_like(dconv)
        for k in range(SSD_CONV):
            dcw_ref[k:k + 1, :] += _colsum(dconv * _shift_down(pre, halo, SSD_CONV - 1 - k))
            dpre = dpre + _shift_up(dconv, nxt, SSD_CONV - 1 - k) * cw_ref[k:k + 1, :]
        dpre_ref[...] = dpre.astype(_ACT)
        carry_ref[...] = dconv[:SUB]

    rev = lambda i: (nc - 1 - i, 0)
    vshape = lambda w, r=1: jax.ShapeDtypeStruct((r, w), F32)
    return pl.pallas_call(
        body, grid=(nc,),
        in_specs=_ssd_specs(nc, True) + [pl.BlockSpec((q, SSD_INNER), rev),
                                         pl.BlockSpec((1, npair, LANE, LANE), lambda i: (nc - 1 - i, 0, 0, 0))]
        + _ssd_param_specs(),
        out_specs=(pl.BlockSpec((q, SSD_XBC), rev), pl.BlockSpec((q, SSD_INNER), rev), pl.BlockSpec((q, LANE), rev),
                   _vec(SSD_XBC, SSD_CONV), _vec(SSD_XBC), _vec(LANE), _vec(LANE), _vec(LANE), _vec(SSD_INNER)),
        out_shape=(jax.ShapeDtypeStruct((S, SSD_XBC), _ACT), jax.ShapeDtypeStruct((S, SSD_INNER), _ACT),
                   jax.ShapeDtypeStruct((S, LANE), F32),
                   vshape(SSD_XBC, SSD_CONV), vshape(SSD_XBC), vshape(LANE), vshape(LANE), vshape(LANE),
                   vshape(SSD_INNER)),
        scratch_shapes=[pltpu.VMEM((npair, LANE, LANE), F32), pltpu.VMEM((SUB, SSD_XBC), F32)],
        compiler_params=_cp(1), name=name,
    )(proj, proj, proj, proj, dycat, hin, cw, cb, dtb, alog, dsk, ng)


def _rope(x, cosf, sina, sinb):
    return x * cosf + pltpu.roll(x, LANE - MLA_ROPE // 2, 1) * sina + pltpu.roll(x, MLA_ROPE // 2, 1) * sinb


def _rope_t(dy, cosf, sina, sinb):
    return dy * cosf + pltpu.roll(dy * sina, MLA_ROPE // 2, 1) + pltpu.roll(dy * sinb, LANE - MLA_ROPE // 2, 1)


def _rope_lanes(shape):
    lane = lax.broadcasted_iota(jnp.int32, shape, 1)
    return (lane >= ROPE_LANE) & (lane < ROPE_LANE + MLA_ROPE)


def _mla_prep_fwd(proj, cosf, sina, sinb, gq, gkv, wuq, wukv, *, name):
    S = proj.shape[0]
    ts = _tile(S, TS_ROW, SUB)
    hw = MLA_HEADS * LANE

    def body(cq_ref, ckv_ref, misc_ref, cos_ref, sa_ref, sb_ref, gq_ref, gkv_ref, wuq_ref, wukv_ref,
             q_ref, k_ref, v_ref, vt_ref):
        cosv, sav, sbv = cos_ref[...], sa_ref[...], sb_ref[...]
        cq = cq_ref[...]
        qn = cq * lax.rsqrt(jnp.mean(cq * cq, axis=-1, keepdims=True) + EPS) * gq_ref[...]
        qh = _dot(qn, wuq_ref[...])
        ckv = ckv_ref[...]
        kvn = ckv * lax.rsqrt(jnp.mean(ckv * ckv, axis=-1, keepdims=True) + EPS) * gkv_ref[...]
        kv = _dot(kvn, wukv_ref[...])
        kr = _rope(jnp.where(_rope_lanes((ts, LANE)), misc_ref[...], 0.0), cosv, sav, sbv)
        for h in range(MLA_HEADS):
            sl = slice(LANE * h, LANE * (h + 1))
            q_ref[:, sl] = (_rope(qh[:, sl], cosv, sav, sbv) * _Q_SCALE).astype(_ACT)
            k_ref[:, sl] = (kv[:, sl] + kr).astype(_ACT)
        v_ref[...] = kv[:, hw:].astype(_ACT)
        vt_ref[...] = kv[:, hw:].T.astype(_ACT)

    oshape = jax.ShapeDtypeStruct((S, hw), _ACT)
    return pl.pallas_call(
        body, grid=(S // ts,),
        in_specs=[_row(ts, MLA_QR, C_CQ // MLA_QR), _row(ts, MLA_KVR, C_CKV // MLA_KVR), _row(ts, LANE, C_MISC // LANE),
                  _row(ts, LANE), _row(ts, LANE), _row(ts, LANE), _vec(MLA_QR), _vec(MLA_KVR),
                  _vec(hw, MLA_QR), _vec(2 * hw, MLA_KVR)],
        out_specs=(_row(ts, hw),) * 3 + (pl.BlockSpec((hw, ts), lambda i: (0, i)),),
        out_shape=(oshape,) * 3 + (jax.ShapeDtypeStruct((hw, S), _ACT),), compiler_params=_cp(1), name=name,
    )(proj, proj, proj, cosf, sina, sinb, gq, gkv, wuq, wukv)


def _mla_prep_bwd(proj, dq, dk, dv, dmisc_ssd, cosf, sina, sinb, gq, gkv, wuq, wukv, *, name):
    S = proj.shape[0]
    ts = _tile(S, TS_ROW, SUB)
    hw = MLA_HEADS * LANE

    def body(cq_ref, ckv_ref, dq_ref, dk_ref, dv_ref, dms_ref, cos_ref, sa_ref, sb_ref, gq_ref, gkv_ref,
             wuq_ref, wukv_ref, dcq_ref, dckv_ref, dmisc_ref, dqh_ref, dkv_ref, qn_ref, kvn_ref, dgq_ref, dgkv_ref):
        i = pl.program_id(0)
        cosv, sav, sbv = cos_ref[...], sa_ref[...], sb_ref[...]

        @pl.when(i == 0)
        def _():
            dgq_ref[...] = jnp.zeros_like(dgq_ref)
            dgkv_ref[...] = jnp.zeros_like(dgkv_ref)

        dqh = jnp.concatenate([_rope_t(dq_ref[:, LANE * h:LANE * (h + 1)], cosv, sav, sbv)
                               for h in range(MLA_HEADS)], axis=1)
        dqh_ref[...] = dqh.astype(_ACT)
        dkv = jnp.concatenate([dk_ref[...], dv_ref[...]], axis=1)
        dkv_ref[...] = dkv.astype(_ACT)

        def norm_bwd(x, g, dn, dg_ref, n_ref):
            rstd = lax.rsqrt(jnp.mean(x * x, axis=-1, keepdims=True) + EPS)
            xhat = x * rstd
            n_ref[...] = (xhat * g).astype(_ACT)
            dg_ref[...] += _colsum(dn * xhat)
            dxh = dn * g
            return rstd * (dxh - xhat * jnp.mean(dxh * xhat, axis=-1, keepdims=True))

        dcq_ref[...] = norm_bwd(cq_ref[...], gq_ref[...], _dot_nt(dqh, wuq_ref[...]), dgq_ref, qn_ref).astype(_ACT)
        dckv_ref[...] = norm_bwd(ckv_ref[...], gkv_ref[...], _dot_nt(dkv, wukv_ref[...]), dgkv_ref, kvn_ref).astype(_ACT)
        dks = dk_ref[:, 0:LANE]
        for h in range(1, MLA_HEADS):
            dks = dks + dk_ref[:, LANE * h:LANE * (h + 1)]
        rl = _rope_lanes((ts, LANE))
        dkr = _rope_t(jnp.where(rl, dks, 0.0), cosv, sav, sbv)
        dmisc_ref[...] = (dms_ref[...] + jnp.where(rl, dkr, 0.0)).astype(_ACT)

    act = lambda w: jax.ShapeDtypeStruct((S, w), _ACT)
    return pl.pallas_call(
        body, grid=(S // ts,),
        in_specs=[_row(ts, MLA_QR, C_CQ // MLA_QR), _row(ts, MLA_KVR, C_CKV // MLA_KVR),
                  _row(ts, hw), _row(ts, hw), _row(ts, hw), _row(ts, LANE),
                  _row(ts, LANE), _row(ts, LANE), _row(ts, LANE), _vec(MLA_QR), _vec(MLA_KVR),
                  _vec(hw, MLA_QR), _vec(2 * hw, MLA_KVR)],
        out_specs=(_row(ts, MLA_QR), _row(ts, MLA_KVR), _row(ts, LANE), _row(ts, hw), _row(ts, 2 * hw),
                   _row(ts, MLA_QR), _row(ts, MLA_KVR), _vec(MLA_QR), _vec(MLA_KVR)),
        out_shape=(act(MLA_QR), act(MLA_KVR), act(LANE), act(hw), act(2 * hw), act(MLA_QR), act(MLA_KVR),
                   jax.ShapeDtypeStruct((1, MLA_QR), F32), jax.ShapeDtypeStruct((1, MLA_KVR), F32)),
        compiler_params=_cp(1), name=name,
    )(proj, proj, dq, dk, dv, dmisc_ssd, cosf, sina, sinb, gq, gkv, wuq, wukv)


_MLA_SCALE = 1.0 / math.sqrt(MLA_NOPE + MLA_ROPE)
_LOG2E = 1.4426950408889634
_Q_SCALE = _MLA_SCALE * _LOG2E
ATT_CHUNK = 1024


def _tri_grid(nq, by_key):
    if by_key:
        pairs = [(i, j) for j in range(nq) for i in range(j, nq)]
    else:
        pairs = [(i, j) for i in range(nq) for j in range(i + 1)]
    return jnp.asarray([p[0] for p in pairs], jnp.int32), jnp.asarray([p[1] for p in pairs], jnp.int32)


def _attn_fwd(q, k, vt, *, name):
    S = q.shape[0]
    tq = _tile(S, TQ_ATT)
    nq = S // tq
    itab, jtab = _tri_grid(nq, False)

    def body(it_ref, jt_ref, q_ref, k_ref, vt_ref, o_ref, lse_ref, lset_ref, m_ref, l_ref, acc_ref):
        t = pl.program_id(1)
        i, j = it_ref[t], jt_ref[t]

        @pl.when(j == 0)
        def _():
            m_ref[...] = jnp.full_like(m_ref, -jnp.inf)
            l_ref[...] = jnp.zeros_like(l_ref)
            acc_ref[...] = jnp.zeros_like(acc_ref)

        def step(diagonal):
            s = _dot_nt(k_ref[...], q_ref[...])
            if diagonal:
                kk = lax.broadcasted_iota(jnp.int32, (tq, tq), 0)
                s = jnp.where(kk <= lax.broadcasted_iota(jnp.int32, (tq, tq), 1), s, -jnp.inf)
            m_prev = m_ref[...]
            m_new = jnp.maximum(m_prev, jnp.max(s, axis=0, keepdims=True))
            p = jnp.exp2(s - m_new)
            alpha = jnp.exp2(m_prev - m_new)
            l_ref[...] = alpha * l_ref[...] + _colsum(p)
            acc_ref[...] = alpha * acc_ref[...] + _dot(vt_ref[...], p)
            m_ref[...] = m_new

        pl.when(j < i)(functools.partial(step, False))
        pl.when(j == i)(functools.partial(step, True))

        @pl.when(j == i)
        def _():
            o_ref[...] = (acc_ref[...] / l_ref[...]).T
            lse = m_ref[...] + jnp.log2(l_ref[...])
            lset_ref[...] = jnp.broadcast_to(lse, (SUB, tq))
            lse_ref[...] = jnp.broadcast_to(lse, (LANE, tq)).T

    qspec = pl.BlockSpec((tq, LANE), lambda h, t, it, jt: (it[t], h))
    kspec = pl.BlockSpec((tq, LANE), lambda h, t, it, jt: (jt[t], h))
    vtspec = pl.BlockSpec((LANE, tq), lambda h, t, it, jt: (h, jt[t]))
    oshape = jax.ShapeDtypeStruct((S, MLA_HEADS * LANE), F32)
    return pl.pallas_call(
        body,
        grid_spec=pltpu.PrefetchScalarGridSpec(
            num_scalar_prefetch=2, grid=(MLA_HEADS, itab.shape[0]), in_specs=[qspec, kspec, vtspec],
            out_specs=(qspec, qspec, pl.BlockSpec((SUB, tq), lambda h, t, it, jt: (h, it[t]))),
            scratch_shapes=[pltpu.VMEM((1, tq), F32), pltpu.VMEM((1, tq), F32), pltpu.VMEM((LANE, tq), F32)]),
        out_shape=(oshape, oshape, jax.ShapeDtypeStruct((MLA_HEADS * SUB, S), F32)),
        compiler_params=_cp(2), name=name)(itab, jtab, q, k, vt)


def _attn_bwd_dq(q, k, v, o, lse, dycat, *, name):
    S = q.shape[0]
    tq = _tile(S, TQ_ATT)
    nq = S // tq
    rc = min(ATT_CHUNK, tq)
    itab, jtab = _tri_grid(nq, False)

    def body(it_ref, jt_ref, q_ref, k_ref, v_ref, o_ref, lse_ref, do_ref, dq_ref, acc_ref):
        t = pl.program_id(1)
        i, j = it_ref[t], jt_ref[t]

        @pl.when(j == 0)
        def _():
            acc_ref[...] = jnp.zeros_like(acc_ref)

        def step(diagonal):
            kv, vv = k_ref[...], v_ref[...]
            for r in range(tq // rc):
                rows = slice(r * rc, (r + 1) * rc)
                s = _dot_nt(q_ref[rows, :], kv)
                if diagonal:
                    rr = r * rc + lax.broadcasted_iota(jnp.int32, (rc, tq), 0)
                    s = jnp.where(lax.broadcasted_iota(jnp.int32, (rc, tq), 1) <= rr, s, -jnp.inf)
                p = jnp.exp2(s - lse_ref[rows, 0:1])
                dov = do_ref[rows, :]
                delta = _rowsum(dov * o_ref[rows, :])
                ds = p * (_dot_nt(dov, vv) - delta)
                acc_ref[rows, :] += _dot(ds, kv)

        pl.when(j < i)(functools.partial(step, False))
        pl.when(j == i)(functools.partial(step, True))

        @pl.when(j == i)
        def _():
            dq_ref[...] = acc_ref[...] * _MLA_SCALE

    qspec = pl.BlockSpec((tq, LANE), lambda h, t, it, jt: (it[t], h))
    kspec = pl.BlockSpec((tq, LANE), lambda h, t, it, jt: (jt[t], h))
    dospec = pl.BlockSpec((tq, LANE), lambda h, t, it, jt: (it[t], SSD_INNER // LANE + h))
    return pl.pallas_call(
        body,
        grid_spec=pltpu.PrefetchScalarGridSpec(
            num_scalar_prefetch=2, grid=(MLA_HEADS, itab.shape[0]),
            in_specs=[qspec, kspec, kspec, qspec, qspec, dospec], out_specs=qspec,
            scratch_shapes=[pltpu.VMEM((tq, LANE), F32)]),
        out_shape=jax.ShapeDtypeStruct((S, MLA_HEADS * LANE), F32),
        compiler_params=_cp(2), name=name)(itab, jtab, q, k, v, o, lse, dycat)


def _attn_bwd_dkv(q, k, v, o, lset, dycat, *, name):
    S = q.shape[0]
    tq = _tile(S, TQ_ATT)
    nq = S // tq
    kc = min(ATT_CHUNK, tq)
    itab, jtab = _tri_grid(nq, True)

    def body(it_ref, jt_ref, q_ref, k_ref, v_ref, o_ref, lset_ref, do_ref, dk_ref, dv_ref, dk_acc, dv_acc):
        t = pl.program_id(1)
        i, j = it_ref[t], jt_ref[t]

        @pl.when(i == j)
        def _():
            dk_acc[...] = jnp.zeros_like(dk_acc)
            dv_acc[...] = jnp.zeros_like(dv_acc)

        def step(diagonal):
            qv, dov = q_ref[...], do_ref[...]
            delta = lax.dot_general(jnp.ones((SUB, LANE), F32), dov * o_ref[...], (((1,), (1,)), ((), ())),
                                    precision=_HI, preferred_element_type=F32)[0:1]
            lse = lset_ref[0:1, :]
            for c in range(tq // kc):
                rows = slice(c * kc, (c + 1) * kc)
                s = _dot_nt(k_ref[rows, :], qv)
                if diagonal:
                    kk = c * kc + lax.broadcasted_iota(jnp.int32, (kc, tq), 0)
                    s = jnp.where(kk <= lax.broadcasted_iota(jnp.int32, (kc, tq), 1), s, -jnp.inf)
                p = jnp.exp2(s - lse)
                dv_acc[rows, :] += _dot(p, dov)
                ds = p * (_dot_nt(v_ref[rows, :], dov) - delta)
                dk_acc[rows, :] += _dot(ds, qv)

        pl.when(i > j)(functools.partial(step, False))
        pl.when(i == j)(functools.partial(step, True))

        @pl.when(i == nq - 1)
        def _():
            dk_ref[...] = dk_acc[...] * (1.0 / _LOG2E)
            dv_ref[...] = dv_acc[...]

    qspec = pl.BlockSpec((tq, LANE), lambda h, t, it, jt: (it[t], h))
    kspec = pl.BlockSpec((tq, LANE), lambda h, t, it, jt: (jt[t], h))
    dospec = pl.BlockSpec((tq, LANE), lambda h, t, it, jt: (it[t], SSD_INNER // LANE + h))
    lspec = pl.BlockSpec((SUB, tq), lambda h, t, it, jt: (h, it[t]))
    oshape = jax.ShapeDtypeStruct((S, MLA_HEADS * LANE), F32)
    return pl.pallas_call(
        body,
        grid_spec=pltpu.PrefetchScalarGridSpec(
            num_scalar_prefetch=2, grid=(MLA_HEADS, itab.shape[0]),
            in_specs=[qspec, kspec, kspec, qspec, lspec, dospec], out_specs=(kspec, kspec),
            scratch_shapes=[pltpu.VMEM((tq, LANE), F32), pltpu.VMEM((tq, LANE), F32)]),
        out_shape=(oshape, oshape), compiler_params=_cp(2), name=name)(itab, jtab, q, k, v, o, lset, dycat)


_SWA_SCALE = 1.0 / math.sqrt(SWA_HD)
_SWA_KW = SWA_KV * LANE


def _swa_specs(S, ts, rev):
    n = S // ts
    t = (lambda i: n - 1 - i) if rev else (lambda i: i)
    hb = lambda i: jnp.maximum(t(i) * (ts // WINDOW) - 1, 0)
    return [
        pl.BlockSpec((ts, SWA_HEADS * LANE), lambda i: (t(i), C_SQ // (SWA_HEADS * LANE))),
        pl.BlockSpec((ts, _SWA_KW), lambda i: (t(i), C_SK // _SWA_KW)),
        pl.BlockSpec((WINDOW, _SWA_KW), lambda i: (hb(i), C_SK // _SWA_KW)),
        pl.BlockSpec((ts, _SWA_KW), lambda i: (t(i), C_SV // _SWA_KW)),
        pl.BlockSpec((WINDOW, _SWA_KW), lambda i: (hb(i), C_SV // _SWA_KW)),
    ]


def _swa_scores(qh, kk, t, b, ts):
    s = _dot_nt(qh, kk) * _SWA_SCALE
    row = lax.broadcasted_iota(jnp.int32, (WINDOW, 2 * WINDOW), 0)
    col = lax.broadcasted_iota(jnp.int32, (WINDOW, 2 * WINDOW), 1)
    rel = WINDOW + row - col
    kpos = t * ts + (b - 1) * WINDOW + col
    return jnp.where((rel >= 0) & (rel < WINDOW) & (kpos >= 0), s, -jnp.inf)


def _swa_fwd(proj, sinks, *, name):
    S = proj.shape[0]
    ts = _tile(S, TS_SWA)
    nb = ts // WINDOW

    def body(q_ref, k_ref, kh_ref, v_ref, vh_ref, sink_ref, o_ref, lse_ref):
        t = pl.program_id(0)
        kext = jnp.concatenate([kh_ref[...], k_ref[...]], axis=0)
        vext = jnp.concatenate([vh_ref[...], v_ref[...]], axis=0)
        for b in range(nb):
            rows = slice(WINDOW * b, WINDOW * (b + 1))
            for h in range(SWA_HEADS):
                kvl = slice(LANE * (h // (SWA_HEADS // SWA_KV)), LANE * (h // (SWA_HEADS // SWA_KV) + 1))
                hl = slice(LANE * h, LANE * (h + 1))
                kk = kext[WINDOW * b:WINDOW * (b + 2), kvl]
                vv = vext[WINDOW * b:WINDOW * (b + 2), kvl]
                s = _swa_scores(q_ref[rows, hl], kk, t, b, ts)
                sk = sink_ref[:, h:h + 1]
                m = jnp.maximum(jnp.max(s, axis=1, keepdims=True), sk)
                p = jnp.exp(s - m)
                den = _rowsum(p) + jnp.exp(sk - m)
                o_ref[rows, hl] = _dot(p, vv) / den
                lse_ref[rows, hl] = jnp.broadcast_to(m + jnp.log(den), (WINDOW, LANE))

    oshape = jax.ShapeDtypeStruct((S, SWA_HEADS * LANE), F32)
    ospec = pl.BlockSpec((ts, SWA_HEADS * LANE), lambda i: (i, 0))
    return pl.pallas_call(
        body, grid=(S // ts,), in_specs=_swa_specs(S, ts, False) + [_vec(LANE)], out_specs=(ospec, ospec),
        out_shape=(oshape, oshape), compiler_params=_cp(1), name=name)(proj, proj, proj, proj, proj, sinks)


def _swa_bwd(proj, o, lse, dycat, sinks, *, name):
    S = proj.shape[0]
    ts = _tile(S, TS_SWA)
    nb = ts // WINDOW
    n = S // ts
    grp = SWA_HEADS // SWA_KV

    def body(q_ref, k_ref, kh_ref, v_ref, vh_ref, o_ref, lse_ref, do_ref, sink_ref,
             dq_ref, dk_ref, dv_ref, dsink_ref, dk_carry, dv_carry):
        i = pl.program_id(0)
        t = n - 1 - i

        @pl.when(i == 0)
        def _():
            dk_carry[...] = jnp.zeros_like(dk_carry)
            dv_carry[...] = jnp.zeros_like(dv_carry)
            dsink_ref[...] = jnp.zeros_like(dsink_ref)

        kext = jnp.concatenate([kh_ref[...], k_ref[...]], axis=0)
        vext = jnp.concatenate([vh_ref[...], v_ref[...]], axis=0)
        lane1 = lax.broadcasted_iota(jnp.int32, (1, LANE), 1)
        dkb = [[jnp.zeros((WINDOW, LANE), F32) for _ in range(SWA_KV)] for _ in range(nb + 1)]
        dvb = [[jnp.zeros((WINDOW, LANE), F32) for _ in range(SWA_KV)] for _ in range(nb + 1)]
        dsink = jnp.zeros((1, LANE), F32)
        for b in range(nb):
            rows = slice(WINDOW * b, WINDOW * (b + 1))
            for h in range(SWA_HEADS):
                kvh = h // grp
                kvl = slice(LANE * kvh, LANE * (kvh + 1))
                hl = slice(LANE * h, LANE * (h + 1))
                kk = kext[WINDOW * b:WINDOW * (b + 2), kvl]
                vv = vext[WINDOW * b:WINDOW * (b + 2), kvl]
                qh = q_ref[rows, hl]
                lse_h = lse_ref[rows, LANE * h:LANE * h + 1]
                p = jnp.exp(_swa_scores(qh, kk, t, b, ts) - lse_h)
                doh = do_ref[rows, hl]
                delta = _rowsum(doh * o_ref[rows, hl])
                ds = p * (_dot_nt(doh, vv) - delta)
                sk = sink_ref[:, h:h + 1]
                dsink = dsink + jnp.where(lane1 == h, -jnp.sum(jnp.exp(sk - lse_h) * delta, keepdims=True), 0.0)
                dq_ref[rows, hl] = (_dot(ds, kk) * _SWA_SCALE).astype(_ACT)
                dkk = _dot_tn(ds, qh) * _SWA_SCALE
                dvv = _dot_tn(p, doh)
                dkb[b][kvh] = dkb[b][kvh] + dkk[:WINDOW]
                dkb[b + 1][kvh] = dkb[b + 1][kvh] + dkk[WINDOW:]
                dvb[b][kvh] = dvb[b][kvh] + dvv[:WINDOW]
                dvb[b + 1][kvh] = dvb[b + 1][kvh] + dvv[WINDOW:]
        dsink_ref[...] += dsink
        for dref, blocks, carry in ((dk_ref, dkb, dk_carry), (dv_ref, dvb, dv_carry)):
            old = carry[...]
            for b in range(1, nb + 1):
                blk = jnp.concatenate(blocks[b], axis=1)
                if b == nb:
                    blk = blk + old
                dref[WINDOW * (b - 1):WINDOW * b, :] = blk.astype(_ACT)
            carry[...] = jnp.concatenate(blocks[0], axis=1)

    hw = SWA_HEADS * LANE
    rev = lambda i: (n - 1 - i, 0)
    mix = lambda i: (n - 1 - i, (SSD_INNER + MLA_HEADS * LANE) // hw)
    return pl.pallas_call(
        body, grid=(n,),
        in_specs=_swa_specs(S, ts, True) + [pl.BlockSpec((ts, hw), rev), pl.BlockSpec((ts, hw), rev),
                                            pl.BlockSpec((ts, hw), mix), _vec(LANE)],
        out_specs=(pl.BlockSpec((ts, hw), rev), pl.BlockSpec((ts, _SWA_KW), rev), pl.BlockSpec((ts, _SWA_KW), rev),
                   _vec(LANE)),
        out_shape=(jax.ShapeDtypeStruct((S, hw), _ACT), jax.ShapeDtypeStruct((S, _SWA_KW), _ACT),
                   jax.ShapeDtypeStruct((S, _SWA_KW), _ACT), jax.ShapeDtypeStruct((1, LANE), F32)),
        scratch_shapes=[pltpu.VMEM((WINDOW, _SWA_KW), F32), pltpu.VMEM((WINDOW, _SWA_KW), F32)],
        compiler_params=_cp(1), name=name)(proj, proj, proj, proj, proj, o, lse, dycat, sinks)


def _exchange(arrays, *, scatter, name):
    n = len(arrays)

    def body(*refs):
        ins, outs = refs[:n], refs[n:2 * n]
        send_sems, recv_sems, loc_sems = refs[2 * n:]
        x, y, c = lax.axis_index("x"), lax.axis_index("y"), lax.axis_index("c")
        me = 4 * x + 2 * y + c

        def src(i, dest):
            return ins[i].at[dest] if scatter else ins[i]

        local = [pltpu.make_async_copy(src(i, me), outs[i].at[me], loc_sems.at[i]) for i in range(n)]
        for cp in local:
            cp.start()
        sends, recvs = [], []
        for k in range(1, NDEV):
            px = 1 - x if k & 4 else x
            py = 1 - y if k & 2 else y
            pc = 1 - c if k & 1 else c
            peer = 4 * px + 2 * py + pc
            for i in range(n):
                common = dict(send_sem=send_sems.at[i, k - 1], recv_sem=recv_sems.at[i, k - 1],
                              device_id=(px, py, pc), device_id_type=pl.DeviceIdType.MESH)
                sends.append(pltpu.make_async_remote_copy(src_ref=src(i, peer), dst_ref=outs[i].at[me], **common))
                recvs.append(pltpu.make_async_remote_copy(src_ref=src(i, peer), dst_ref=outs[i].at[peer], **common))
        for cp in sends:
            cp.start()
        for cp in recvs:
            cp.wait_recv()
        for cp in sends:
            cp.wait_send()
        for cp in local:
            cp.wait()

    hbm = pl.BlockSpec(memory_space=pl.ANY)
    out_shape = tuple(jax.ShapeDtypeStruct(a.shape if scatter else (NDEV,) + a.shape, a.dtype) for a in arrays)
    return pl.pallas_call(
        body, in_specs=[hbm] * n, out_specs=tuple([hbm] * n), out_shape=out_shape,
        scratch_shapes=[pltpu.SemaphoreType.DMA((n, NDEV - 1)), pltpu.SemaphoreType.DMA((n, NDEV - 1)),
                        pltpu.SemaphoreType.DMA((n,))],
        name=name)(*arrays)


def _adamw(w, m, v, parts, *, name):
    R, C = w.shape
    npart = parts.shape[0]
    cap = max(SUB, ((1 << 18) // C) // SUB * SUB)
    tr = _tile(R, cap, SUB)

    def body(w_ref, m_ref, v_ref, p_ref, g_ref, d_ref, mo_ref, vo_ref):
        g = p_ref[0]
        for k in range(1, npart):
            g = g + p_ref[k]
        mn = ADAM_B1 * m_ref[...] + (1.0 - ADAM_B1) * g
        vn = ADAM_B2 * v_ref[...] + (1.0 - ADAM_B2) * (g * g)
        m_hat = mn / (1.0 - ADAM_B1 ** ADAM_STEP)
        v_hat = vn / (1.0 - ADAM_B2 ** ADAM_STEP)
        g_ref[...] = g
        d_ref[...] = -ADAM_LR * (m_hat / (jnp.sqrt(v_hat) + ADAM_EPS) + ADAM_WD * w_ref[...])
        mo_ref[...] = mn
        vo_ref[...] = vn

    spec = pl.BlockSpec((tr, C), lambda i: (i, 0))
    oshape = jax.ShapeDtypeStruct((R, C), F32)
    return pl.pallas_call(
        body, grid=(R // tr,), in_specs=[spec] * 3 + [pl.BlockSpec((npart, tr, C), lambda i: (0, i, 0))],
        out_specs=(spec,) * 4, out_shape=(oshape,) * 4, compiler_params=_cp(1), name=name)(w, m, v, parts)


def _adamw_many(ws, ms, vs, landed, mine, me, *, name):
    n = len(ws)

    def body(me_ref, *refs):
        w_r, m_r, v_r, p_r, o_r = (refs[k * n:(k + 1) * n] for k in range(5))
        outs = refs[5 * n:]
        for i in range(n):
            own = o_r[i][...]
            g = jnp.where(me_ref[0] == 0, own, p_r[i][0])
            for k in range(1, NDEV):
                g = g + jnp.where(me_ref[0] == k, own, p_r[i][k])
            mn = ADAM_B1 * m_r[i][...] + (1.0 - ADAM_B1) * g
            vn = ADAM_B2 * v_r[i][...] + (1.0 - ADAM_B2) * (g * g)
            m_hat = mn / (1.0 - ADAM_B1 ** ADAM_STEP)
            v_hat = vn / (1.0 - ADAM_B2 ** ADAM_STEP)
            outs[4 * i][...] = g
            outs[4 * i + 1][...] = -ADAM_LR * (m_hat / (jnp.sqrt(v_hat) + ADAM_EPS) + ADAM_WD * w_r[i][...])
            outs[4 * i + 2][...] = mn
            outs[4 * i + 3][...] = vn

    vmem = pl.BlockSpec(memory_space=pltpu.VMEM)
    return pl.pallas_call(
        body, in_specs=[pl.BlockSpec(memory_space=pltpu.SMEM)] + [vmem] * (5 * n), out_specs=(vmem,) * (4 * n),
        out_shape=tuple(jax.ShapeDtypeStruct(w.shape, F32) for w in ws for _ in range(4)),
        name=name)(me, *ws, *ms, *vs, *landed, *mine)


def _adamw_layer(l, w, m, v, landed, mine, me, prev, *, name):
    L, R, C = w.shape
    npart = landed.shape[0]
    cap = max(2 * SUB, ((1 << 18) // C) // (2 * SUB) * (2 * SUB))
    tr = _tile(R, cap, 2 * SUB)
    nprev = 0 if prev is None else 4

    def body(me_ref, *refs):
        w_ref, m_ref, v_ref, p_ref, own_ref = refs[:5]
        g_ref, d_ref, mo_ref, vo_ref = refs[5 + nprev:]
        own = own_ref[...].astype(F32)
        g = jnp.where(me_ref[0] == 0, own, p_ref[0].astype(F32))
        for k in range(1, npart):
            g = g + jnp.where(me_ref[0] == k, own, p_ref[k].astype(F32))
        mn = ADAM_B1 * m_ref[...] + (1.0 - ADAM_B1) * g
        vn = ADAM_B2 * v_ref[...] + (1.0 - ADAM_B2) * (g * g)
        m_hat = mn / (1.0 - ADAM_B1 ** ADAM_STEP)
        v_hat = vn / (1.0 - ADAM_B2 ** ADAM_STEP)
        g_ref[...] = g
        d_ref[...] = -ADAM_LR * (m_hat / (jnp.sqrt(v_hat) + ADAM_EPS) + ADAM_WD * w_ref[...])
        mo_ref[...] = mn
        vo_ref[...] = vn

    spec = pl.BlockSpec((None, tr, C), lambda i, me_ref: (l, i, 0))
    oshape = jax.ShapeDtypeStruct((L, R, C), F32)
    return pl.pallas_call(
        body,
        grid_spec=pltpu.PrefetchScalarGridSpec(
            num_scalar_prefetch=1, grid=(R // tr,),
            in_specs=[spec] * 3 + [pl.BlockSpec((npart, tr, C), lambda i, me_ref: (0, i, 0)),
                                   pl.BlockSpec((None, tr, C), lambda i, me_ref: (me_ref[0], i, 0))]
            + [pl.BlockSpec(memory_space=pl.ANY)] * nprev,
            out_specs=(spec,) * 4),
        out_shape=(oshape,) * 4, input_output_aliases={6 + k: k for k in range(nprev)},
        compiler_params=_cp(1), name=name)(me, w, m, v, landed, mine, *(prev or ()))


_HBM = pl.BlockSpec(memory_space=pltpu.HBM)
_SEM = pl.BlockSpec(memory_space=pltpu.SEMAPHORE)
_EFFECT = pltpu.SideEffectType.DATAFLOW_SIDE_EFFECTING


def _peers():
    x, y, c = lax.axis_index("x"), lax.axis_index("y"), lax.axis_index("c")
    out = []
    for k in range(1, NDEV):
        px = 1 - x if k & 4 else x
        py = 1 - y if k & 2 else y
        pc = 1 - c if k & 1 else c
        out.append((k - 1, (px, py, pc), 4 * px + 2 * py + pc))
    return 4 * x + 2 * y + c, out


def _xchg_start(arrays, *, scatter, name):
    n = len(arrays)
    lands = [lax.empty(a.shape if scatter else (NDEV,) + a.shape, a.dtype) for a in arrays]

    def body(*refs):
        ins, lnd = refs[:n], refs[n:2 * n]
        send_sems, recv_sems = refs[2 * n], refs[2 * n + 1]
        token = refs[-1]
        me, peers = _peers()
        for k, dev, peer in peers:
            for i in range(n):
                pltpu.make_async_remote_copy(
                    src_ref=ins[i].at[peer] if scatter else ins[i], dst_ref=lnd[i].at[me],
                    send_sem=send_sems.at[i * (NDEV - 1) + k], recv_sem=recv_sems.at[i * (NDEV - 1) + k],
                    device_id=dev, device_id_type=pl.DeviceIdType.MESH).start()
        token[...] = jnp.zeros_like(token)

    sems = pltpu.SemaphoreType.DMA((n * (NDEV - 1),))
    res = pl.pallas_call(
        body, name=name,
        out_shape=(sems, sems) + tuple(pltpu.HBM(t.shape, t.dtype) for t in list(arrays) + lands)
        + (jax.ShapeDtypeStruct((SUB, LANE), F32),),
        in_specs=[_HBM] * (2 * n), out_specs=(_SEM, _SEM) + (_HBM,) * (2 * n) + (pl.BlockSpec(memory_space=pltpu.VMEM),),
        input_output_aliases={i: 2 + i for i in range(2 * n)},
        compiler_params=pltpu.CompilerParams(has_side_effects=_EFFECT),
    )(*[pltpu.with_memory_space_constraint(t, pltpu.HBM) for t in list(arrays) + lands])
    return dict(send=res[0], recv=res[1], thru=list(res[2:2 + 2 * n]), token=res[-1], scatter=scatter, n=n)


def _xchg_wait(handle, after, *, name):
    n, scatter = handle["n"], handle["scatter"]
    thru = handle["thru"]

    def body(*refs):
        ins, lnd = refs[:n], refs[n:2 * n]
        send_sems, recv_sems = refs[2 * n], refs[2 * n + 1]
        me, peers = _peers()
        for k, dev, peer in peers:
            for i in range(n):
                cp = pltpu.make_async_remote_copy(
                    src_ref=ins[i].at[peer] if scatter else ins[i], dst_ref=lnd[i].at[peer],
                    send_sem=send_sems.at[i * (NDEV - 1) + k], recv_sem=recv_sems.at[i * (NDEV - 1) + k],
                    device_id=dev, device_id_type=pl.DeviceIdType.MESH)
                cp.wait_send()
                cp.wait_recv()

    res = pl.pallas_call(
        body, name=name, out_shape=tuple(pltpu.HBM(t.shape, t.dtype) for t in thru),
        in_specs=[_HBM] * (2 * n) + [_SEM, _SEM, pl.BlockSpec(memory_space=pl.ANY)], out_specs=(_HBM,) * (2 * n),
        input_output_aliases={i: i for i in range(2 * n)},
        compiler_params=pltpu.CompilerParams(has_side_effects=_EFFECT),
    )(*thru, handle["send"], handle["recv"], after)
    return list(res[:n]), list(res[n:])


def _pad_heads(w, nh, hd, axis=-1):
    axis = axis % w.ndim
    shp = w.shape
    w = w.reshape(shp[:axis] + (nh, hd) + shp[axis + 1:])
    pads = [(0, 0)] * w.ndim
    pads[axis + 1] = (0, LANE - hd)
    return jnp.pad(w, pads).reshape(shp[:axis] + (nh * LANE,) + shp[axis + 1:])


def _unpad_heads(w, nh, hd, axis=-1):
    axis = axis % w.ndim
    shp = w.shape
    w = w.reshape(shp[:axis] + (nh, LANE) + shp[axis + 1:])
    w = lax.slice_in_dim(w, 0, hd, axis=axis + 1)
    return w.reshape(shp[:axis] + (nh * hd,) + shp[axis + 1:])


_O_DT = SSD_INNER + SSD_XBC
_O_CQ = _O_DT + SSD_HEADS
_O_CKV = _O_CQ + MLA_QR
_O_KR = _O_CKV + MLA_KVR
_O_SQ = _O_KR + MLA_ROPE
_O_SK = _O_SQ + SWA_HEADS * SWA_HD
_O_SV = _O_SK + SWA_KV * SWA_HD


def _w_in_to_padded(w, axis=-1):
    axis = axis % w.ndim
    cut = lambda a, b: lax.slice_in_dim(w, a, b, axis=axis)
    z, xbc, dt = cut(0, SSD_INNER), cut(SSD_INNER, _O_DT), cut(_O_DT, _O_CQ)
    cq, ckv, kr = cut(_O_CQ, _O_CKV), cut(_O_CKV, _O_KR), cut(_O_KR, _O_SQ)
    sq, sk, sv = cut(_O_SQ, _O_SK), cut(_O_SK, _O_SV), cut(_O_SV, D_IN)
    zeros = lambda n: jnp.zeros(w.shape[:axis] + (n,) + w.shape[axis + 1:], w.dtype)
    return jnp.concatenate([xbc, z, cq, ckv, dt, zeros(ROPE_LANE - SSD_HEADS), kr, zeros(LANE - ROPE_LANE - MLA_ROPE),
                            _pad_heads(sq, SWA_HEADS, SWA_HD, axis), _pad_heads(sk, SWA_KV, SWA_HD, axis),
                            _pad_heads(sv, SWA_KV, SWA_HD, axis)], axis=axis)


def _w_in_from_padded(g, axis=-1):
    axis = axis % g.ndim
    cut = lambda a, b: lax.slice_in_dim(g, a, b, axis=axis)
    xbc, z, cq, ckv = cut(C_XBC, C_Z), cut(C_Z, C_CQ), cut(C_CQ, C_CKV), cut(C_CKV, C_MISC)
    dt, kr = cut(C_MISC, C_MISC + SSD_HEADS), cut(C_MISC + ROPE_LANE, C_MISC + ROPE_LANE + MLA_ROPE)
    sq = _unpad_heads(cut(C_SQ, C_SK), SWA_HEADS, SWA_HD, axis)
    sk = _unpad_heads(cut(C_SK, C_SV), SWA_KV, SWA_HD, axis)
    sv = _unpad_heads(cut(C_SV, D_INP), SWA_KV, SWA_HD, axis)
    return jnp.concatenate([z, xbc, dt, cq, ckv, kr, sq, sk, sv], axis=axis)


def _w_out_to_padded(w):
    a = SSD_INNER
    b = a + MLA_HEADS * MLA_V
    return jnp.concatenate([w[..., :a, :], _pad_heads(w[..., a:b, :], MLA_HEADS, MLA_V, axis=-2),
                            _pad_heads(w[..., b:, :], SWA_HEADS, SWA_HD, axis=-2)], axis=-2)


def _w_out_from_padded(g):
    a = SSD_INNER
    b = a + MLA_HEADS * LANE
    return jnp.concatenate([g[..., :a, :], _unpad_heads(g[..., a:b, :], MLA_HEADS, MLA_V, axis=-2),
                            _unpad_heads(g[..., b:, :], SWA_HEADS, SWA_HD, axis=-2)], axis=-2)


def _w_ukv_to_padded(w):
    w4 = w.reshape(w.shape[:-1] + (MLA_HEADS, MLA_NOPE + MLA_V))
    flat = lambda t: t.reshape(w.shape[:-1] + (MLA_HEADS * t.shape[-1],))
    return jnp.concatenate([_pad_heads(flat(w4[..., :MLA_NOPE]), MLA_HEADS, MLA_NOPE),
                            _pad_heads(flat(w4[..., MLA_NOPE:]), MLA_HEADS, MLA_V)], axis=-1)


def _w_ukv_from_padded(g):
    hw = MLA_HEADS * LANE
    gk = _unpad_heads(g[..., :hw], MLA_HEADS, MLA_NOPE).reshape(g.shape[:-1] + (MLA_HEADS, MLA_NOPE))
    gv = _unpad_heads(g[..., hw:], MLA_HEADS, MLA_V).reshape(g.shape[:-1] + (MLA_HEADS, MLA_V))
    return jnp.concatenate([gk, gv], axis=-1).reshape(g.shape[:-1] + (MLA_HEADS * (MLA_NOPE + MLA_V),))


def _pad_lane(v):
    return jnp.pad(v, [(0, 0)] * (v.ndim - 1) + [(0, LANE - v.shape[-1])])


def _rope_tables(positions):
    inv_freq = ROPE_THETA ** (-jnp.arange(0, MLA_ROPE, 2, dtype=F32) / MLA_ROPE)
    ang = positions.astype(F32).reshape(-1, 1) * inv_freq
    cos, sin = jnp.cos(ang), jnp.sin(ang)
    S = ang.shape[0]
    one, zero = jnp.ones((S, ROPE_LANE), F32), jnp.zeros((S, ROPE_LANE), F32)
    tail1, tail0 = jnp.ones((S, LANE - ROPE_LANE - MLA_ROPE), F32), jnp.zeros((S, LANE - ROPE_LANE - MLA_ROPE), F32)
    z16 = jnp.zeros_like(sin)
    return (jnp.concatenate([one, cos, cos, tail1], axis=1), jnp.concatenate([zero, -sin, z16, tail0], axis=1),
            jnp.concatenate([zero, z16, sin, tail0], axis=1))


def _layer_fwd(l, x_in, f_prev, gate_prev, mod, P, tabs):
    sh1, sc1, g1, sh2, sc2, g2 = [mod[k:k + 1] for k in range(6)]
    tag = f"l{l}_"
    if f_prev is None:
        x0 = x_in
        h1 = _norm_fwd(x0, P["n1g"], sc1, sh1, name=tag + "norm1")
    else:
        x0, h1 = _norm_fwd(x_in, P["n1g"], sc1, sh1, f=f_prev, gate=gate_prev, name=tag + "norm1")
    proj = _mm(h1, P["w_in"], tb=True, name=tag + "proj")
    P.update(P.pop("mid")(proj))
    y_ssd, hin = _ssd_fwd(proj, P["ssd_cw"], P["ssd_cb"], P["dtb"], P["alog"], P["dsk"],
                          P["ssd_ng"], name=tag + "ssd")
    q, k, v, vt = _mla_prep_fwd(proj, *tabs, P["gq"], P["gkv"], P["w_uq"], P["w_ukv"], name=tag + "mla_prep")
    o_mla, lse_mla, lset_mla = _attn_fwd(q, k, vt, name=tag + "mla_attn")
    o_swa, lse_swa = _swa_fwd(proj, P["sinks"], name=tag + "swa")
    ycat = jnp.concatenate([y_ssd.astype(_ACT), o_mla.astype(_ACT), o_swa.astype(_ACT)], axis=1)
    y = _mm(ycat, P["w_out"], name=tag + "out")
    P.update(P.pop("late")(y))
    x1, h2 = _norm_fwd(x0, P["n2g"], sc2, sh2, f=y, gate=g1, name=tag + "norm2")
    up = _mm(h2, P["w_up"], tb=True, name=tag + "up")
    act = _ffn_act_fwd(up, P["fcw"], P["fcb"], name=tag + "ffn_act")
    f = _mm(act, P["w_down"], name=tag + "down")
    saved = dict(x0=x0, h1=h1, proj=proj, hin=hin, q=q, k=k, v=v, o_mla=o_mla, lse_mla=lse_mla, lset_mla=lset_mla, o_swa=o_swa,
                 lse_swa=lse_swa, ycat=ycat, y=y, x1=x1, h2=h2, up=up, act=act, f=f, mod=mod)
    return x1, f, g2, saved


def _layer_bwd(l, dxo, sv, P, tabs, on_part):
    mod = sv["mod"]
    sh1, sc1, g1, sh2, sc2, g2 = [mod[k:k + 1] for k in range(6)]
    tag = f"l{l}_b_"
    G = {}
    df, dg2 = _gate_bwd(dxo, sv["f"], g2, name=tag + "gate2")
    dact = _mm(df, P["w_down"], tb=True, name=tag + "dact")
    G["w_down"] = _mm(sv["act"], df, ta=True, out_dtype=_ACT, name=tag + "dw_down")
    dup, G["fcw"], G["fcb"] = _ffn_bwd(sv["up"], dact, P["fcw"], P["fcb"], name=tag + "ffn")
    dh2 = _mm(dup, P["w_up"], name=tag + "dh2")
    G["w_up"] = _mm(dup, sv["h2"], ta=True, out_dtype=_ACT, name=tag + "dw_up")
    token = on_part(l, "ffn", G)
    if token is not None:
        sc2 = sc2 + token
    dx1, G["n2g"], dsc2, dsh2 = _norm_bwd(dh2, sv["x1"], dxo, P["n2g"], sc2, name=tag + "norm2")
    dy, dg1 = _gate_bwd(dx1, sv["y"], g1, name=tag + "gate1")
    dycat = _mm(dy, P["w_out"], tb=True, name=tag + "dycat")
    G["w_out"] = _mm(sv["ycat"], dy, ta=True, out_dtype=_ACT, name=tag + "dw_out")
    token = on_part(l, "out", G)
    ssd_cb = P["ssd_cb"] if token is None else P["ssd_cb"] + token
    proj = sv["proj"]
    (dpre, dz, dmisc_ssd, G["ssd_cw"], G["ssd_cb"], G["dtb"], G["alog"], G["dsk"], G["ssd_ng"]) = _ssd_bwd(
        proj, dycat, sv["hin"], P["ssd_cw"], ssd_cb, P["dtb"], P["alog"], P["dsk"],
        P["ssd_ng"], name=tag + "ssd")
    att = (sv["q"], sv["k"], sv["v"], sv["o_mla"])
    dq = _attn_bwd_dq(*att, sv["lse_mla"], dycat, name=tag + "mla_dq")
    dk, dv = _attn_bwd_dkv(*att, sv["lset_mla"], dycat, name=tag + "mla_dkv")
    dcq, dckv, dmisc, dqh, dkv, qn, kvn, G["gq"], G["gkv"] = _mla_prep_bwd(
        proj, dq, dk, dv, dmisc_ssd, *tabs, P["gq"], P["gkv"], P["w_uq"], P["w_ukv"], name=tag + "mla_prep")
    G["w_uq"] = _mm(qn, dqh, ta=True, out_dtype=_ACT, name=tag + "dw_uq")
    G["w_ukv"] = _mm(kvn, dkv, ta=True, out_dtype=_ACT, name=tag + "dw_ukv")
    dsq, dsk_, dsv_, G["sinks"] = _swa_bwd(proj, sv["o_swa"], sv["lse_swa"], dycat, P["sinks"], name=tag + "swa")
    dproj = jnp.concatenate([dpre, dz, dcq, dckv, dmisc, dsq, dsk_, dsv_], axis=1)
    G["w_in"] = _mm(dproj, sv["h1"], ta=True, out_dtype=_ACT, name=tag + "dw_in")
    token = on_part(l, "mixer", G)
    if token is not None:
        sc1 = sc1 + token
    dh1 = _mm(dproj, P["w_in"], name=tag + "dh1")
    dx0, G["n1g"], dsc1, dsh1 = _norm_bwd(dh1, sv["x0"], dx1, P["n1g"], sc1, name=tag + "norm1")
    G["mod"] = jnp.concatenate([dsh1, dsc1, dg1, dsh2, dsc2, dg2], axis=0)
    return dx0, G


def _local_step(x, tgt, mods, get_params, tabs, final_g, on_grads, on_part):
    saved, params = [], []
    xin, f, gate = x, None, None
    for l in range(DEPTH):
        params.append(get_params(l, x if f is None else f))
        xin, f, gate, sv = _layer_fwd(l, xin, f, gate, mods[l], params[l], tabs)
        saved.append(sv)
    loss, dx, dfinal = _final_loss(xin, f, gate, final_g, tgt, name="final_loss")
    for l in reversed(range(DEPTH)):
        dx, G = _layer_bwd(l, dx, saved[l], params[l], tabs, on_part)
        on_grads(l, G)
    return loss[0, 0], dx, dfinal


_WEIGHTS = ['ada_w', 'ada_b', 'norm1_g', 'norm2_g', 'w_in', 'ssd_conv_w', 'ssd_conv_b', 'ssd_dt_bias', 'ssd_a_log',
            'ssd_d', 'ssd_norm_g', 'mla_q_norm_g', 'mla_w_uq', 'mla_kv_norm_g', 'mla_w_ukv', 'swa_sinks', 'w_out',
            'ffn_w_up', 'ffn_conv_w', 'ffn_conv_b', 'ffn_w_down', 'final_norm_g']
_INPUTS = ['x', 'c', 'positions'] + _WEIGHTS + ['loss_target'] + ['m_' + n for n in _WEIGHTS] + ['v_' + n for n in _WEIGHTS]
_SMALL = [('ada_b', 'mod'), ('norm1_g', 'n1g'), ('norm2_g', 'n2g'), ('ssd_conv_b', 'ssd_cb'), ('ssd_dt_bias', 'dtb'),
          ('ssd_a_log', 'alog'), ('ssd_d', 'dsk'), ('ssd_norm_g', 'ssd_ng'), ('mla_q_norm_g', 'gq'),
          ('mla_kv_norm_g', 'gkv'), ('swa_sinks', 'sinks'), ('ffn_conv_b', 'fcb')]
_SHARDED = [('w_in', 'w_in', 2), ('ssd_conv_w', 'ssd_cw', 2), ('mla_w_uq', 'w_uq', 2), ('mla_w_ukv', 'w_ukv', 2),
            ('w_out', 'w_out', 1), ('ffn_w_up', 'w_up', 2), ('ffn_conv_w', 'fcw', 2), ('ffn_w_down', 'w_down', 1)]
_SHARDED_NAMES = [n for n, _, _ in _SHARDED]
_TRANSPOSED = ('w_in', 'ffn_w_up')


def _shard_major(g, axis):
    shp = g.shape
    g = g.reshape(shp[:axis] + (NDEV, shp[axis] // NDEV) + shp[axis + 1:])
    return jnp.moveaxis(g, axis, 0)


def _unshard(g, axis):
    g = jnp.moveaxis(g, 0, axis)
    shp = g.shape
    return g.reshape(shp[:axis] + (shp[axis] * shp[axis + 1],) + shp[axis + 2:])


def kernel(x, c, positions, ada_w, ada_b, norm1_g, norm2_g, w_in, ssd_conv_w, ssd_conv_b, ssd_dt_bias, ssd_a_log, ssd_d, ssd_norm_g, mla_q_norm_g, mla_w_uq, mla_kv_norm_g, mla_w_ukv, swa_sinks, w_out, ffn_w_up, ffn_conv_w, ffn_conv_b, ffn_w_down, final_norm_g, loss_target, m_ada_w, m_ada_b, m_norm1_g, m_norm2_g, m_w_in, m_ssd_conv_w, m_ssd_conv_b, m_ssd_dt_bias, m_ssd_a_log, m_ssd_d, m_ssd_norm_g, m_mla_q_norm_g, m_mla_w_uq, m_mla_kv_norm_g, m_mla_w_ukv, m_swa_sinks, m_w_out, m_ffn_w_up, m_ffn_conv_w, m_ffn_conv_b, m_ffn_w_down, m_final_norm_g, v_ada_w, v_ada_b, v_norm1_g, v_norm2_g, v_w_in, v_ssd_conv_w, v_ssd_conv_b, v_ssd_dt_bias, v_ssd_a_log, v_ssd_d, v_ssd_norm_g, v_mla_q_norm_g, v_mla_w_uq, v_mla_kv_norm_g, v_mla_w_ukv, v_swa_sinks, v_w_out, v_ffn_w_up, v_ffn_conv_w, v_ffn_conv_b, v_ffn_w_down, v_final_norm_g):
    a = dict(zip(_INPUTS, (x, c, positions, ada_w, ada_b, norm1_g, norm2_g, w_in, ssd_conv_w, ssd_conv_b, ssd_dt_bias, ssd_a_log, ssd_d, ssd_norm_g, mla_q_norm_g, mla_w_uq, mla_kv_norm_g, mla_w_ukv, swa_sinks, w_out, ffn_w_up, ffn_conv_w, ffn_conv_b, ffn_w_down, final_norm_g, loss_target, m_ada_w, m_ada_b, m_norm1_g, m_norm2_g, m_w_in, m_ssd_conv_w, m_ssd_conv_b, m_ssd_dt_bias, m_ssd_a_log, m_ssd_d, m_ssd_norm_g, m_mla_q_norm_g, m_mla_w_uq, m_mla_kv_norm_g, m_mla_w_ukv, m_swa_sinks, m_w_out, m_ffn_w_up, m_ffn_conv_w, m_ffn_conv_b, m_ffn_w_down, m_final_norm_g, v_ada_w, v_ada_b, v_norm1_g, v_norm2_g, v_w_in, v_ssd_conv_w, v_ssd_conv_b, v_ssd_dt_bias, v_ssd_a_log, v_ssd_d, v_ssd_norm_g, v_mla_q_norm_g, v_mla_w_uq, v_mla_kv_norm_g, v_mla_w_ukv, v_swa_sinks, v_w_out, v_ffn_w_up, v_ffn_conv_w, v_ffn_conv_b, v_ffn_w_down, v_final_norm_g)))
    axes = ("x", "y", "c")
    me = 4 * lax.axis_index("x") + 2 * lax.axis_index("y") + lax.axis_index("c")
    ncol = ada_w.shape[-1]

    kform = lambda n, t: jnp.swapaxes(t, -1, -2) if n in _TRANSPOSED else t
    mxu_names = ('w_in', 'mla_w_uq', 'mla_w_ukv', 'w_out', 'ffn_w_up', 'ffn_w_down')
    gather_groups = (("early", _SHARDED_NAMES[:4]), ("mid", _SHARDED_NAMES[4:5]), ("late", _SHARDED_NAMES[5:]))

    def own_of(src, names, l):
        return [kform(n, src[n][l]).astype(_MXU) if n in mxu_names else src[n][l] for n in names]

    first_gather = _xchg_start(own_of(a, gather_groups[0][1], 0), scatter=False, name="gather_start_early0")

    c_all = _exchange([c + first_gather["token"][0, 0]], scatter=False, name="gather_c")[0]
    c_act = _silu_call(c_all.reshape(NDEV, D), name="c_act")
    mod_part = jnp.stack([_mm(c_act, ada_w[l], name=f"mod{l}") for l in range(DEPTH)])
    mod_all = _exchange([mod_part], scatter=False, name="gather_mod")[0]
    mod_mine = lax.dynamic_index_in_dim(mod_all, me, axis=2, keepdims=False)
    mods = (jnp.moveaxis(mod_mine, 0, 1).reshape(DEPTH, 6 * D) + ada_b).reshape(DEPTH, 6, D)
    tabs = _rope_tables(positions)

    shard_of = {n: (key, 1 if n in _TRANSPOSED else ax) for n, key, ax in _SHARDED}
    mods, raw = lax.optimization_barrier((mods, {n: a[n] for n in _SHARDED_NAMES}))
    gathers, prev = [], first_gather["token"]
    for l in range(DEPTH):
        gathers.append({})
        for grp, names in gather_groups:
            if (l, grp) == (0, "early"):
                gathers[l][grp] = first_gather
                continue
            srcs, _ = lax.optimization_barrier((own_of(raw, names, l), prev))
            gathers[l][grp] = _xchg_start(srcs, scatter=False, name=f"gather_start_{grp}{l}")
            prev = gathers[l][grp]["token"]

    def place_own(landed, mine):
        slot = lambda t: lax.broadcasted_iota(jnp.int32, (NDEV,) + (1,) * (t.ndim - 1), 0)
        return [jnp.where(slot(t) == me, o[None], t) for t, o in zip(landed, mine)]

    def gathered(l, grp, after):
        names = dict(gather_groups)[grp]
        mine, landed = _xchg_wait(gathers[l][grp], after, name=f"gather_wait_{grp}{l}")
        return {n: _unshard(g, shard_of[n][1] - 1) for n, g in zip(names, place_own(landed, mine))}

    def get_params(l, after):
        full = gathered(l, "early", mods if l == 0 else after)
        vec = lambda t: t[l].reshape(1, -1)

        def mid(after2):
            return dict(w_out=_w_out_to_padded(gathered(l, "mid", after2)['w_out']))

        def late(after2):
            rest = gathered(l, "late", after2)
            return dict(w_up=rest['ffn_w_up'], w_down=rest['ffn_w_down'], fcw=rest['ffn_conv_w'])

        return dict(
            w_in=_w_in_to_padded(full['w_in'], axis=0), w_uq=_pad_heads(full['mla_w_uq'], MLA_HEADS, MLA_NOPE + MLA_ROPE),
            w_ukv=_w_ukv_to_padded(full['mla_w_ukv']), ssd_cw=full['ssd_conv_w'], mid=mid, late=late,
            ssd_cb=vec(ssd_conv_b), dtb=vec(_pad_lane(ssd_dt_bias)), alog=vec(_pad_lane(ssd_a_log)),
            dsk=vec(_pad_lane(ssd_d)), ssd_ng=vec(ssd_norm_g), gq=vec(mla_q_norm_g), gkv=vec(mla_kv_norm_g),
            sinks=vec(_pad_lane(swa_sinks)), fcb=vec(ffn_conv_b), n1g=vec(norm1_g), n2g=vec(norm2_g))

    unpad = dict(w_in=functools.partial(_w_in_from_padded, axis=0), w_out=_w_out_from_padded, w_ukv=_w_ukv_from_padded,
                 w_uq=lambda g: _unpad_heads(g, MLA_HEADS, MLA_NOPE + MLA_ROPE))
    scatter_groups = (("ffn", _SHARDED_NAMES[5:]), ("out", _SHARDED_NAMES[4:5]), ("mixer", _SHARDED_NAMES[:4]))
    grads = [None] * DEPTH
    scatters = [dict() for _ in range(DEPTH)]

    def on_part(l, grp, G):
        parts = [_shard_major(unpad.get(shard_of[n][0], lambda g: g)(G[shard_of[n][0]]), shard_of[n][1] - 1).astype(_ACT)
                 for n in dict(scatter_groups)[grp]]
        scatters[l][grp] = _xchg_start(parts, scatter=True, name=f"scatter_start_{grp}{l}")
        return scatters[l][grp]["token"][0, 0]

    def on_grads(l, G):
        grads[l] = G

    mods = mods + sum(g[grp]["token"][0, 0] for g in gathers for grp, _ in gather_groups)
    loss, dx, dfinal = _local_step(x[0], loss_target[0], mods, get_params, tabs, final_norm_g.reshape(1, D),
                                   on_grads, on_part)
    loss = lax.psum(loss, axes)

    stack = lambda key: jnp.stack([grads[l][key] for l in range(DEPTH)])
    small_names = [n for n, _ in _SMALL] + ['final_norm_g']
    small_g = [stack(key).reshape(DEPTH, -1)[:, :a[name].shape[1]] for name, key in _SMALL] + [dfinal]
    small_gather = _xchg_start(small_g, scatter=False, name="gather_small_start")

    out_g, out_d, out_m, out_v = {}, {}, {}, {}
    chain = {name: None for name in _SHARDED_NAMES}
    me_arr = jnp.reshape(me, (1,)).astype(jnp.int32)
    after = small_gather["token"]
    for l in reversed(range(DEPTH)):
        for grp, names in scatter_groups:
            mine, landed = _xchg_wait(scatters[l][grp], after, name=f"scatter_wait_{grp}{l}")
            for name, own, got in zip(names, mine, landed):
                chain[name] = _adamw_layer(l, kform(name, a[name]), kform(name, a['m_' + name]),
                                           kform(name, a['v_' + name]), got, own, me_arr, chain[name],
                                           name=f"adamw_{name}{l}")
    for name in _SHARDED_NAMES:
        out_g[name], out_d[name], out_m[name], out_v[name] = [kform(name, t) for t in chain[name]]
    small_mine, small_landed = _xchg_wait(small_gather, chain[_SHARDED_NAMES[0]][0], name="gather_small_wait")
    row = lambda t: t.reshape(1, -1) if t.ndim == 1 else t
    res = _adamw_many([row(a[n]) for n in small_names], [row(a['m_' + n]) for n in small_names],
                      [row(a['v_' + n]) for n in small_names], small_landed, small_mine, me_arr, name="adamw_small")
    for i, n in enumerate(small_names):
        out_g[n], out_d[n], out_m[n], out_v[n] = [t.reshape(a[n].shape) for t in res[4 * i:4 * i + 4]]

    dmod_all = place_own(small_landed[:1], small_mine[:1])[0]
    dmod_mine = lax.dynamic_slice_in_dim(dmod_all, me * ncol, ncol, axis=2)
    g_ada = jnp.stack([_mm(c_act, dmod_mine[:, l], ta=True, name=f"dw_ada{l}") for l in range(DEPTH)])
    shp = ada_w.shape
    res = _adamw(*[t.reshape(-1, shp[-1]) for t in (ada_w, m_ada_w, v_ada_w)], g_ada.reshape(1, -1, shp[-1]),
                 name="adamw_ada_w")
    out_g['ada_w'], out_d['ada_w'], out_m['ada_w'], out_v['ada_w'] = [t.reshape(shp) for t in res]

    outs = [loss, dx[None]]
    for dct in (out_g, out_d, out_m, out_v):
        outs += [dct[n] for n in _WEIGHTS]
    return tuple(outs)
```

```python
import functools
import math

import jax
import jax.numpy as jnp
from jax import lax
from jax.experimental import pallas as pl
from jax.experimental.pallas import tpu as pltpu

F32 = jnp.float32
_MXU = jnp.bfloat16
_ACT = jnp.bfloat16
_HI = lax.Precision.HIGHEST
EPS = 1e-6
NDEV = 8
DEPTH = 4
D = 1024
LANE = 128
SUB = 8
VMEM_LIMIT = 56 * 1024 * 1024

SSD_INNER, SSD_STATE, SSD_HEADS, SSD_GROUPS, SSD_CHUNK, SSD_CONV = 512, 128, 8, 2, 128, 4
SSD_XBC = SSD_INNER + 2 * SSD_GROUPS * SSD_STATE
MLA_HEADS, MLA_NOPE, MLA_ROPE, MLA_V, MLA_QR, MLA_KVR = 4, 64, 32, 64, 256, 128
SWA_HEADS, SWA_KV, SWA_HD, WINDOW = 4, 2, 64, 128
D_FF, FFN_CONV = 2816, 3
D_IN = 2472
ROPE_THETA = 10000.0
C_XBC, C_Z, C_CQ, C_CKV, C_MISC, C_SQ, C_SK, C_SV, D_INP = 0, 1024, 1536, 1792, 1920, 2048, 2560, 2816, 3072
ROPE_LANE = 64
D_MIXP = 1536

ADAM_LR, ADAM_B1, ADAM_B2, ADAM_EPS, ADAM_WD, ADAM_STEP = 0.001, 0.9, 0.999, 1e-08, 0.01, 10

TS_ROW = 1024
TS_FFN = 256
TQ_ATT = 1024
TS_SWA = 512


def _tile(n, cap, q=LANE):
    best = None
    for t in range(q, min(n, cap) + 1, q):
        if n % t == 0:
            best = t
    return n if best is None else best


def _cp(ngrid):
    return pltpu.CompilerParams(dimension_semantics=("arbitrary",) * ngrid, vmem_limit_bytes=VMEM_LIMIT)


def _dot(a, b):
    return jnp.dot(a.astype(_MXU), b.astype(_MXU), preferred_element_type=F32)


def _dot_nt(a, b):
    return lax.dot_general(a.astype(_MXU), b.astype(_MXU), (((1,), (1,)), ((), ())), preferred_element_type=F32)


def _dot_tn(a, b):
    return jnp.dot(a.T.astype(_MXU), b.astype(_MXU), preferred_element_type=F32)


def _sigmoid(x):
    return 1.0 / (1.0 + jnp.exp(-x))


def _sigmoid_t(x):
    return 0.5 * jnp.tanh(0.5 * x) + 0.5


def _silu(x):
    return x * _sigmoid_t(x)


def _silu_grad(x):
    s = _sigmoid_t(x)
    return x * s, s * (1.0 + x * (1.0 - s))


def _dsilu(x):
    return _silu_grad(x)[1]


def _softplus(x):
    u = jnp.exp(-jnp.abs(x))
    w = 1.0 + u
    log1p = jnp.where(w == 1.0, u, jnp.log(w) * u / jnp.where(w == 1.0, 1.0, w - 1.0))
    return jnp.maximum(x, 0.0) + log1p


def _colsum(x):
    return jnp.sum(x, axis=0, keepdims=True)


def _rowsum(x):
    return jnp.sum(x, axis=1, keepdims=True)


def _shift_down(t, halo, j):
    if j == 0:
        return t
    n = t.shape[0]
    rolled = pltpu.roll(t, j, 0)
    row = lax.broadcasted_iota(jnp.int32, (SUB, t.shape[1]), 0)
    first = jnp.where(row < j, pltpu.roll(halo, j, 0), rolled[:SUB])
    return jnp.concatenate([first, rolled[SUB:]], axis=0) if n > SUB else first


def _shift_up(t, halo, j):
    if j == 0:
        return t
    n = t.shape[0]
    rolled = pltpu.roll(t, n - j, 0)
    row = lax.broadcasted_iota(jnp.int32, (SUB, t.shape[1]), 0)
    last = jnp.where(row >= SUB - j, pltpu.roll(halo, SUB - j, 0), rolled[n - SUB:])
    return jnp.concatenate([rolled[:n - SUB], last], axis=0) if n > SUB else last


def _mm(a, b, *, ta=False, tb=False, out_dtype=F32, name):
    if ta:
        K, M = a.shape
    else:
        M, K = a.shape
    if tb:
        N, K2 = b.shape
    else:
        K2, N = b.shape
    assert K == K2, (a.shape, b.shape, ta, tb)
    tk = _tile(K, 1536)
    nk = K // tk
    tm, tn = _tile(M, 2048 if nk == 1 else 1536), _tile(N, 1536 if nk == 1 else 1408)
    dn = (((0 if ta else 1,), (1 if tb else 0,)), ((), ()))

    def body(a_ref, b_ref, o_ref, *acc):
        part = lax.dot_general(a_ref[...].astype(_MXU), b_ref[...].astype(_MXU), dn, preferred_element_type=F32)
        if nk == 1:
            o_ref[...] = part.astype(out_dtype)
            return
        acc_ref, = acc
        k = pl.program_id(2)

        @pl.when(k == 0)
        def _():
            acc_ref[...] = part

        @pl.when(k > 0)
        def _():
            acc_ref[...] += part

        @pl.when(k == nk - 1)
        def _():
            o_ref[...] = acc_ref[...].astype(out_dtype)

    a_spec = pl.BlockSpec((tk, tm), lambda i, j, k: (k, i)) if ta else pl.BlockSpec((tm, tk), lambda i, j, k: (i, k))
    b_spec = pl.BlockSpec((tn, tk), lambda i, j, k: (j, k)) if tb else pl.BlockSpec((tk, tn), lambda i, j, k: (k, j))
    return pl.pallas_call(
        body, grid=(M // tm, N // tn, nk), in_specs=[a_spec, b_spec],
        out_specs=pl.BlockSpec((tm, tn), lambda i, j, k: (i, j)),
        out_shape=jax.ShapeDtypeStruct((M, N), out_dtype),
        scratch_shapes=[pltpu.VMEM((tm, tn), F32)] * (nk > 1), compiler_params=_cp(3), name=name)(a, b)


def _row(ts, w, col=0):
    return pl.BlockSpec((ts, w), lambda i: (i, col))


def _vec(w, r=1):
    return pl.BlockSpec((r, w), lambda i: (0, 0))


def _silu_call(x, name):
    def body(x_ref, o_ref):
        o_ref[...] = _silu(x_ref[...])
    return pl.pallas_call(body, out_shape=jax.ShapeDtypeStruct(x.shape, F32), name=name)(x)


def _norm_fwd(x, g, sc, sh, *, f=None, gate=None, name):
    S, dm = x.shape
    ts = _tile(S, TS_ROW, SUB)
    res = f is not None

    def body(*refs):
        if res:
            x_ref, f_ref, gate_ref, g_ref, sc_ref, sh_ref, xo_ref, h_ref = refs
            xv = x_ref[...] + gate_ref[...] * f_ref[...]
            xo_ref[...] = xv
        else:
            x_ref, g_ref, sc_ref, sh_ref, h_ref = refs
            xv = x_ref[...]
        rstd = lax.rsqrt(jnp.mean(xv * xv, axis=-1, keepdims=True) + EPS)
        h_ref[...] = ((xv * rstd) * g_ref[...] * (1.0 + sc_ref[...]) + sh_ref[...]).astype(_ACT)

    ins = [x] + ([f, gate] if res else []) + [g, sc, sh]
    in_specs = [_row(ts, dm)] + ([_row(ts, dm), _vec(dm)] if res else []) + [_vec(dm)] * 3
    h_shape = jax.ShapeDtypeStruct((S, dm), _ACT)
    if res:
        out_shape, out_specs = (jax.ShapeDtypeStruct((S, dm), F32), h_shape), (_row(ts, dm), _row(ts, dm))
    else:
        out_shape, out_specs = h_shape, _row(ts, dm)
    return pl.pallas_call(body, grid=(S // ts,), in_specs=in_specs, out_specs=out_specs, out_shape=out_shape,
                          compiler_params=_cp(1), name=name)(*ins)


def _norm_bwd(dh, x, dres, g, sc, *, branch=None, name):
    S, dm = x.shape
    ts = _tile(S, TS_ROW // 2, SUB)
    nb = 0 if branch is None else 2

    def body(*refs):
        dh_ref, x_ref, dres_ref, g_ref, sc_ref = refs[:5]
        dx_ref, dg_ref, dsc_ref, dsh_ref = refs[5 + nb:9 + nb]
        i = pl.program_id(0)
        xv = x_ref[...]
        dhv = dh_ref[...]
        rstd = lax.rsqrt(jnp.mean(xv * xv, axis=-1, keepdims=True) + EPS)
        xhat = xv * rstd
        hn = xhat * g_ref[...]
        dhn = dhv * (1.0 + sc_ref[...])
        dxh = dhn * g_ref[...]
        dx = dres_ref[...] + rstd * (dxh - xhat * jnp.mean(dxh * xhat, axis=-1, keepdims=True))
        dx_ref[...] = dx

        @pl.when(i == 0)
        def _():
            for r in refs[6 + nb:]:
                if r.shape[0] == 1:
                    r[...] = jnp.zeros_like(r)

        dg_ref[...] += _colsum(dhn * xhat)
        dsc_ref[...] += _colsum(dhv * hn)
        dsh_ref[...] += _colsum(dhv)
        if branch is not None:
            f_ref, gate_ref = refs[5:7]
            df_ref, dgate_ref = refs[9 + nb:]
            df_ref[...] = (gate_ref[...] * dx).astype(_ACT)
            dgate_ref[...] += _colsum(dx * f_ref[...])

    vshape = jax.ShapeDtypeStruct((1, dm), F32)
    extra_in = [] if branch is None else list(branch)
    return pl.pallas_call(
        body, grid=(S // ts,),
        in_specs=[_row(ts, dm)] * 3 + [_vec(dm)] * 2 + ([_row(ts, dm), _vec(dm)] if nb else []),
        out_specs=(_row(ts, dm), _vec(dm), _vec(dm), _vec(dm)) + ((_row(ts, dm), _vec(dm)) if nb else ()),
        out_shape=(jax.ShapeDtypeStruct((S, dm), F32), vshape, vshape, vshape)
        + ((jax.ShapeDtypeStruct((S, dm), _ACT), vshape) if nb else ()),
        compiler_params=_cp(1), name=name)(dh, x, dres, g, sc, *extra_in)


def _final_loss(x, f, gate, g, tgt, *, name):
    S, dm = x.shape
    ts = _tile(S, TS_ROW, SUB)

    def body(x_ref, f_ref, gate_ref, g_ref, t_ref, loss_ref, dx_ref, dg_ref, df_ref, dgate_ref):
        i = pl.program_id(0)
        fv = f_ref[...]
        xv = x_ref[...] + gate_ref[...] * fv
        rstd = lax.rsqrt(jnp.mean(xv * xv, axis=-1, keepdims=True) + EPS)
        xhat = xv * rstd
        err = xhat * g_ref[...] - t_ref[...]
        dy = err * (1.0 / dm)
        dxh = dy * g_ref[...]
        dx = rstd * (dxh - xhat * jnp.mean(dxh * xhat, axis=-1, keepdims=True))
        dx_ref[...] = dx
        df_ref[...] = (gate_ref[...] * dx).astype(_ACT)

        @pl.when(i == 0)
        def _():
            loss_ref[...] = jnp.zeros_like(loss_ref)
            dg_ref[...] = jnp.zeros_like(dg_ref)
            dgate_ref[...] = jnp.zeros_like(dgate_ref)

        loss_ref[...] += jnp.full((1, LANE), 0.5 * jnp.sum(jnp.mean(err * err, axis=-1, keepdims=True)), F32)
        dg_ref[...] += _colsum(dy * xhat)
        dgate_ref[...] += _colsum(dx * fv)

    return pl.pallas_call(
        body, grid=(S // ts,), in_specs=[_row(ts, dm), _row(ts, dm), _vec(dm), _vec(dm), _row(ts, dm)],
        out_specs=(_vec(LANE), _row(ts, dm), _vec(dm), _row(ts, dm), _vec(dm)),
        out_shape=(jax.ShapeDtypeStruct((1, LANE), F32), jax.ShapeDtypeStruct((S, dm), F32),
                   jax.ShapeDtypeStruct((1, dm), F32), jax.ShapeDtypeStruct((S, dm), _ACT),
                   jax.ShapeDtypeStruct((1, dm), F32)),
        compiler_params=_cp(1), name=name)(x, f, gate, g, tgt)


def _ffn_conv(t, halo, cw_ref, cb_ref):
    t1, t2 = _shift_down(t, halo, 1), _shift_down(t, halo, 2)
    return ((cb_ref[...] + t2 * cw_ref[0:1, :]) + t1 * cw_ref[1:2, :]) + t * cw_ref[2:3, :], t1, t2


def _prev_halo_spec(ts, w, col=0):
    return pl.BlockSpec((SUB, w), lambda i: (jnp.maximum(i * (ts // SUB) - 1, 0), col))


def _ffn_act_fwd(up, cw, cb, *, name):
    S, w2 = up.shape
    ff = w2 // 2
    ts = _tile(S, TS_FFN, SUB)

    def body(up_ref, halo_ref, cw_ref, cb_ref, act_ref):
        i = pl.program_id(0)
        t = up_ref[...]
        halo = jnp.where(i > 0, halo_ref[...], 0.0)
        u, _, _ = _ffn_conv(t, halo, cw_ref, cb_ref)
        act_ref[...] = (_silu(u[:, :ff]) * u[:, ff:]).astype(_ACT)

    return pl.pallas_call(
        body, grid=(S // ts,), in_specs=[_row(ts, w2), _prev_halo_spec(ts, w2), _vec(w2, FFN_CONV), _vec(w2)],
        out_specs=_row(ts, ff), out_shape=jax.ShapeDtypeStruct((S, ff), _ACT),
        compiler_params=_cp(1), name=name)(up, up, cw, cb)


def _ffn_bwd(up, dact, cw, cb, *, name):
    S, w2 = up.shape
    ff = w2 // 2
    ts = _tile(S, TS_FFN, SUB)
    n = S // ts

    def body(up_ref, halo_ref, dact_ref, cw_ref, cb_ref, dup_ref, dcw_ref, dcb_ref, carry_ref):
        i = pl.program_id(0)
        t_idx = n - 1 - i

        @pl.when(i == 0)
        def _():
            carry_ref[...] = jnp.zeros_like(carry_ref)
            dcw_ref[...] = jnp.zeros_like(dcw_ref)
            dcb_ref[...] = jnp.zeros_like(dcb_ref)

        t = up_ref[...]
        halo = jnp.where(t_idx > 0, halo_ref[...], 0.0)
        u, t1, t2 = _ffn_conv(t, halo, cw_ref, cb_ref)
        a, b = u[:, :ff], u[:, ff:]
        da = dact_ref[...]
        sa, dsa = _silu_grad(a)
        dv = jnp.concatenate([da * b * dsa, da * sa], axis=1)
        nxt = carry_ref[...]
        dup = (dv * cw_ref[2:3, :] + _shift_up(dv, nxt, 1) * cw_ref[1:2, :]) + _shift_up(dv, nxt, 2) * cw_ref[0:1, :]
        dup_ref[...] = dup.astype(_ACT)
        dcb_ref[...] += _colsum(dv)
        dcw_ref[2:3, :] += _colsum(dv * t)
        dcw_ref[1:2, :] += _colsum(dv * t1)
        dcw_ref[0:1, :] += _colsum(dv * t2)
        carry_ref[...] = dv[:SUB]

    rev = lambda w: pl.BlockSpec((ts, w), lambda i: (n - 1 - i, 0))
    halo_spec = pl.BlockSpec((SUB, w2), lambda i: (jnp.maximum((n - 1 - i) * (ts // SUB) - 1, 0), 0))
    return pl.pallas_call(
        body, grid=(n,), in_specs=[rev(w2), halo_spec, rev(ff), _vec(w2, FFN_CONV), _vec(w2)],
        out_specs=(rev(w2), _vec(w2, FFN_CONV), _vec(w2)),
        out_shape=(jax.ShapeDtypeStruct((S, w2), _ACT), jax.ShapeDtypeStruct((FFN_CONV, w2), F32),
                   jax.ShapeDtypeStruct((1, w2), F32)),
        scratch_shapes=[pltpu.VMEM((SUB, w2), F32)], compiler_params=_cp(1), name=name)(up, up, dact, cw, cb)


def _ssd_core(pre, halo, misc, cw_ref, cb_ref, dtb, alog):
    q = pre.shape[0]
    conv = cb_ref[...]
    for k in range(SSD_CONV):
        conv = conv + _shift_down(pre, halo, SSD_CONV - 1 - k) * cw_ref[k:k + 1, :]
    xbc = _silu(conv)
    raw = misc + dtb
    dt = _softplus(raw)
    a = -jnp.exp(alog)
    r = lax.broadcasted_iota(jnp.int32, (q, q), 0)
    c = lax.broadcasted_iota(jnp.int32, (q, q), 1)
    tri = r >= c
    acum = jnp.dot(tri.astype(F32), dt * a, precision=_HI, preferred_element_type=F32)
    return conv, xbc, raw, dt, a, acum, acum.T, tri


def _sel(v, j, lo):
    return jnp.where(lo, v[:, 2 * j:2 * j + 1], v[:, 2 * j + 1:2 * j + 2])


def _ssd_pair_fwd(xbc, dt, acum, acum_t, tri, dsk, g_mat, b_mat, c_mat, h_pair, j, lo, lo1, sub_lo):
    q = xbc.shape[0]
    x = xbc[:, LANE * j:LANE * (j + 1)]
    dtp = _sel(dt, j, lo)
    ap = _sel(acum, j, lo)
    xd = x * dtp
    ls, ms = [], []
    for h in (2 * j, 2 * j + 1):
        seg = acum[:, h:h + 1] - acum_t[h:h + 1, :]
        l_mat = jnp.exp(jnp.where(tri, seg, -jnp.inf))
        ls.append(l_mat)
        ms.append(g_mat * l_mat)
    yd = jnp.where(lo, _dot(ms[0], xd), _dot(ms[1], xd))
    ea = jnp.exp(ap)
    yo = _dot_nt(c_mat, h_pair) * ea
    dp = _sel(dsk, j, lo1)
    alast = acum[q - 1:q, :]
    e = jnp.exp(_sel(alast, j, lo1) - ap)
    cd = jnp.where(sub_lo, jnp.exp(alast[:, 2 * j:2 * j + 1]), jnp.exp(alast[:, 2 * j + 1:2 * j + 2]))
    return dict(x=x, dtp=dtp, ap=ap, xd=xd, ls=ls, ms=ms, ea=ea, yo=yo, dp=dp, e=e, cd=cd, y=yd + yo + x * dp)


def _gnorm(yg):
    half = SSD_INNER // SSD_GROUPS
    rstds, yns = [], []
    for g in range(SSD_GROUPS):
        part = yg[:, half * g:half * (g + 1)]
        rstd = lax.rsqrt(jnp.mean(part * part, axis=-1, keepdims=True) + EPS)
        rstds.append(rstd)
        yns.append(part * rstd)
    return rstds, yns


def _ssd_specs(nc, rev):
    q = SSD_CHUNK
    cidx = (lambda i: nc - 1 - i) if rev else (lambda i: i)
    return [
        pl.BlockSpec((q, SSD_XBC), lambda i: (cidx(i), C_XBC // SSD_XBC)),
        pl.BlockSpec((SUB, SSD_XBC), lambda i: (jnp.maximum(cidx(i) * (q // SUB) - 1, 0), C_XBC // SSD_XBC)),
        pl.BlockSpec((q, SSD_INNER), lambda i: (cidx(i), C_Z // SSD_INNER)),
        pl.BlockSpec((q, LANE), lambda i: (cidx(i), C_MISC // LANE)),
    ]


def _ssd_param_specs():
    return [_vec(SSD_XBC, SSD_CONV), _vec(SSD_XBC), _vec(LANE), _vec(LANE), _vec(LANE), _vec(SSD_INNER)]


def _ssd_fwd(proj, cw, cb, dtb, alog, dsk, ng, *, name):
    S = proj.shape[0]
    q = SSD_CHUNK
    nc = S // q
    npair = SSD_HEADS // 2

    def body(xbc_ref, halo_ref, z_ref, misc_ref, cw_ref, cb_ref, dtb_ref, alog_ref, dsk_ref, ng_ref,
             y_ref, hin_ref, h_ref):
        c = pl.program_id(0)

        @pl.when(c == 0)
        def _():
            h_ref[...] = jnp.zeros_like(h_ref)

        pre = xbc_ref[...]
        halo = jnp.where(c > 0, halo_ref[...], 0.0)
        conv, xbc, raw, dt, a, acum, acum_t, tri = _ssd_core(pre, halo, misc_ref[...], cw_ref, cb_ref,
                                                             dtb_ref[...], alog_ref[...])
        lo = lax.broadcasted_iota(jnp.int32, (q, LANE), 1) < LANE // 2
        lo1 = lo[:1]
        sub_lo = lax.broadcasted_iota(jnp.int32, (LANE, LANE), 0) < LANE // 2
        ys = []
        for g in range(SSD_GROUPS):
            b_mat = xbc[:, SSD_INNER + SSD_STATE * g:SSD_INNER + SSD_STATE * (g + 1)]
            c_mat = xbc[:, SSD_INNER + SSD_STATE * (SSD_GROUPS + g):SSD_INNER + SSD_STATE * (SSD_GROUPS + g + 1)]
            g_mat = _dot_nt(c_mat, b_mat)
            for jj in range(npair // SSD_GROUPS):
                j = g * (npair // SSD_GROUPS) + jj
                hj = h_ref[j]
                p = _ssd_pair_fwd(xbc, dt, acum, acum_t, tri, dsk_ref[...], g_mat, b_mat, c_mat, hj, j, lo, lo1, sub_lo)
                ys.append(p["y"])
                hin_ref[0, j] = hj
                h_ref[j] = p["cd"] * hj + _dot_tn(p["xd"] * p["e"], b_mat)
        yg = jnp.concatenate(ys, axis=1) * _silu(z_ref[...])
        _, yns = _gnorm(yg)
        y_ref[...] = jnp.concatenate(yns, axis=1) * ng_ref[...]

    return pl.pallas_call(
        body, grid=(nc,), in_specs=_ssd_specs(nc, False) + _ssd_param_specs(),
        out_specs=(pl.BlockSpec((q, SSD_INNER), lambda i: (i, 0)),
                   pl.BlockSpec((1, npair, LANE, LANE), lambda i: (i, 0, 0, 0))),
        out_shape=(jax.ShapeDtypeStruct((S, SSD_INNER), F32), jax.ShapeDtypeStruct((nc, npair, LANE, LANE), F32)),
        scratch_shapes=[pltpu.VMEM((npair, LANE, LANE), F32)], compiler_params=_cp(1), name=name,
    )(proj, proj, proj, proj, cw, cb, dtb, alog, dsk, ng)


def _ssd_bwd(proj, dycat, hin, cw, cb, dtb, alog, dsk, ng, *, name):
    S = proj.shape[0]
    q = SSD_CHUNK
    nc = S // q
    npair = SSD_HEADS // 2
    ppg = npair // SSD_GROUPS

    def body(xbc_ref, halo_ref, z_ref, misc_ref, dy_ref, hin_ref, cw_ref, cb_ref, dtb_ref, alog_ref, dsk_ref, ng_ref,
             dpre_ref, dz_ref, dmisc_ref, dcw_ref, dcb_ref, ddtb_ref, dalog_ref, ddsk_ref, dng_ref,
             dh_ref, carry_ref):
        i = pl.program_id(0)
        c = nc - 1 - i

        @pl.when(i == 0)
        def _():
            dh_ref[...] = jnp.zeros_like(dh_ref)
            carry_ref[...] = jnp.zeros_like(carry_ref)
            for r in (dcw_ref, dcb_ref, ddtb_ref, dalog_ref, ddsk_ref, dng_ref):
                r[...] = jnp.zeros_like(r)

        pre = xbc_ref[...]
        halo = jnp.where(c > 0, halo_ref[...], 0.0)
        conv, xbc, raw, dt, a, acum, acum_t, tri = _ssd_core(pre, halo, misc_ref[...], cw_ref, cb_ref,
                                                             dtb_ref[...], alog_ref[...])
        lane = lax.broadcasted_iota(jnp.int32, (q, LANE), 1)
        lane1 = lane[:1]
        rowi = lax.broadcasted_iota(jnp.int32, (q, LANE), 0)
        lastrow = rowi == q - 1
        lo = lane < LANE // 2
        lo1 = lo[:1]
        sub_lo = lax.broadcasted_iota(jnp.int32, (LANE, LANE), 0) < LANE // 2
        dsk = dsk_ref[...]
        alast = acum[q - 1:q, :]

        def halves(t):
            return _rowsum(jnp.where(lo, t, 0.0)), _rowsum(jnp.where(lo, 0.0, t))

        def put(ha, va, vb):
            ln = lane if va.shape[0] == q else lane1
            return jnp.where(ln == ha, va, 0.0) + jnp.where(ln == ha + 1, vb, 0.0)

        mats, pairs = [], []
        for g in range(SSD_GROUPS):
            b_mat = xbc[:, SSD_INNER + SSD_STATE * g:SSD_INNER + SSD_STATE * (g + 1)]
            c_mat = xbc[:, SSD_INNER + SSD_STATE * (SSD_GROUPS + g):SSD_INNER + SSD_STATE * (SSD_GROUPS + g + 1)]
            g_mat = _dot_nt(c_mat, b_mat)
            mats.append((b_mat, c_mat, g_mat))
            for jj in range(ppg):
                j = g * ppg + jj
                pairs.append(_ssd_pair_fwd(xbc, dt, acum, acum_t, tri, dsk, g_mat, b_mat, c_mat, hin_ref[0, j],
                                           j, lo, lo1, sub_lo))
        z = z_ref[...]
        sz, dsz = _silu_grad(z)
        y = jnp.concatenate([p["y"] for p in pairs], axis=1)
        rstds, yns = _gnorm(y * sz)
        dout = dy_ref[...]
        dng_ref[...] += _colsum(dout * jnp.concatenate(yns, axis=1))
        dyn = dout * ng_ref[...]
        half = SSD_INNER // SSD_GROUPS
        dygs = []
        for g in range(SSD_GROUPS):
            dyn_g = dyn[:, half * g:half * (g + 1)]
            dygs.append(rstds[g] * (dyn_g - yns[g] * jnp.mean(dyn_g * yns[g], axis=-1, keepdims=True)))
        dyg = jnp.concatenate(dygs, axis=1)
        dyv = dyg * sz
        dz_ref[...] = (dyg * y * dsz).astype(_ACT)

        da_acc = jnp.zeros((q, LANE), F32)
        ddt = jnp.zeros((q, LANE), F32)
        dds = jnp.zeros((1, LANE), F32)
        dxs, dbs, dcs = [], [], []
        for g in range(SSD_GROUPS):
            b_mat, c_mat, g_mat = mats[g]
            dg_mat = jnp.zeros((q, q), F32)
            db = jnp.zeros((q, SSD_STATE), F32)
            dc = jnp.zeros((q, SSD_STATE), F32)
            for jj in range(ppg):
                j = g * ppg + jj
                ha = 2 * j
                p = pairs[j]
                hj = hin_ref[0, j]
                dyp = dyv[:, LANE * j:LANE * (j + 1)]
                dsum = _colsum(dyp * p["x"])
                dds = dds + put(ha, _rowsum(jnp.where(lo1, dsum, 0.0)), _rowsum(jnp.where(lo1, 0.0, dsum)))
                dx = dyp * p["dp"]
                dw = dyp * p["ea"]
                dc = dc + _dot(dw, hj)
                dh_yo = _dot_tn(dw, c_mat)
                ra, rb = halves(dyp * p["yo"])
                da_acc = da_acc + put(ha, ra, rb)
                dxd = jnp.zeros((q, LANE), F32)
                for idx in range(2):
                    dyh = jnp.where(lo, dyp, 0.0) if idx == 0 else jnp.where(lo, 0.0, dyp)
                    dm = _dot_nt(dyh, p["xd"])
                    dxd = dxd + _dot_tn(p["ms"][idx], dyh)
                    dg_mat = dg_mat + dm * p["ls"][idx]
                    t = dm * p["ms"][idx]
                    da_h = _rowsum(t) - _rowsum(t.T)
                    da_acc = da_acc + jnp.where(lane == ha + idx, da_h, 0.0)
                dhn = dh_ref[j]
                s = _rowsum(dhn * hj)
                sa = jnp.sum(jnp.where(sub_lo[:, :1], s, 0.0), keepdims=True)
                sb = jnp.sum(jnp.where(sub_lo[:, :1], 0.0, s), keepdims=True)
                cda, cdb = jnp.exp(alast[:, ha:ha + 1]), jnp.exp(alast[:, ha + 1:ha + 2])
                db = db + _dot(p["xd"] * p["e"], dhn)
                r = _dot_nt(b_mat, dhn)
                dxd = dxd + r * p["e"]
                qa, qb = halves(r * p["xd"] * p["e"])
                da_acc = da_acc - put(ha, qa, qb)
                tot_a = sa * cda + jnp.sum(qa, keepdims=True)
                tot_b = sb * cdb + jnp.sum(qb, keepdims=True)
                da_acc = da_acc + jnp.where(lastrow, put(ha, tot_a, tot_b), 0.0)
                dh_ref[j] = p["cd"] * dhn + dh_yo
                dx = dx + dxd * p["dtp"]
                ua, ub = halves(dxd * p["x"])
                ddt = ddt + put(ha, ua, ub)
                dxs.append(dx)
            dc = dc + _dot(dg_mat, b_mat)
            db = db + _dot_tn(dg_mat, c_mat)
            dbs.append(db)
            dcs.append(dc)
        r2 = lax.broadcasted_iota(jnp.int32, (q, q), 0)
        c2 = lax.broadcasted_iota(jnp.int32, (q, q), 1)
        dda = jnp.dot((c2 >= r2).astype(F32), da_acc, precision=_HI, preferred_element_type=F32)
        ddt = ddt + dda * a
        dalog_ref[...] += _colsum(dda * dt) * a
        ddsk_ref[...] += dds
        draw = jnp.where(lane < SSD_HEADS, ddt * _sigmoid(raw), 0.0)
        ddtb_ref[...] += _colsum(draw)
        dmisc_ref[...] = draw
        dconv = jnp.concatenate(dxs + dbs + dcs, axis=1) * _dsilu(conv)
        dcb_ref[...] += _colsum(dconv)
        nxt = carry_ref[...]
        dpre = jnp.zeros_like(dconv)
        for k in range(SSD_CONV):
            dcw_ref[k:k + 1, :] += _colsum(dconv * _shift_down(pre, halo, SSD_CONV - 1 - k))
            dpre = dpre + _shift_up(dconv, nxt, SSD_CONV - 1 - k) * cw_ref[k:k + 1, :]
        dpre_ref[...] = dpre.astype(_ACT)
        carry_ref[...] = dconv[:SUB]

    rev = lambda i: (nc - 1 - i, 0)
    vshape = lambda w, r=1: jax.ShapeDtypeStruct((r, w), F32)
    return pl.pallas_call(
        body, grid=(nc,),
        in_specs=_ssd_specs(nc, True) + [pl.BlockSpec((q, SSD_INNER), rev),
                                         pl.BlockSpec((1, npair, LANE, LANE), lambda i: (nc - 1 - i, 0, 0, 0))]
        + _ssd_param_specs(),
        out_specs=(pl.BlockSpec((q, SSD_XBC), rev), pl.BlockSpec((q, SSD_INNER), rev), pl.BlockSpec((q, LANE), rev),
                   _vec(SSD_XBC, SSD_CONV), _vec(SSD_XBC), _vec(LANE), _vec(LANE), _vec(LANE), _vec(SSD_INNER)),
        out_shape=(jax.ShapeDtypeStruct((S, SSD_XBC), _ACT), jax.ShapeDtypeStruct((S, SSD_INNER), _ACT),
                   jax.ShapeDtypeStruct((S, LANE), F32),
                   vshape(SSD_XBC, SSD_CONV), vshape(SSD_XBC), vshape(LANE), vshape(LANE), vshape(LANE),
                   vshape(SSD_INNER)),
        scratch_shapes=[pltpu.VMEM((npair, LANE, LANE), F32), pltpu.VMEM((SUB, SSD_XBC), F32)],
        compiler_params=_cp(1), name=name,
    )(proj, proj, proj, proj, dycat, hin, cw, cb, dtb, alog, dsk, ng)


def _rope(x, cosf, sina, sinb):
    return x * cosf + pltpu.roll(x, LANE - MLA_ROPE // 2, 1) * sina + pltpu.roll(x, MLA_ROPE // 2, 1) * sinb


def _rope_t(dy, cosf, sina, sinb):
    return dy * cosf + pltpu.roll(dy * sina, MLA_ROPE // 2, 1) + pltpu.roll(dy * sinb, LANE - MLA_ROPE // 2, 1)


def _rope_lanes(shape):
    lane = lax.broadcasted_iota(jnp.int32, shape, 1)
    return (lane >= ROPE_LANE) & (lane < ROPE_LANE + MLA_ROPE)


def _mla_prep_fwd(proj, cosf, sina, sinb, gq, gkv, wuq, wukv, *, name):
    S = proj.shape[0]
    ts = _tile(S, TS_ROW, SUB)
    hw = MLA_HEADS * LANE

    def body(cq_ref, ckv_ref, misc_ref, cos_ref, sa_ref, sb_ref, gq_ref, gkv_ref, wuq_ref, wukv_ref,
             q_ref, k_ref, v_ref, vt_ref):
        cosv, sav, sbv = cos_ref[...], sa_ref[...], sb_ref[...]
        cq = cq_ref[...]
        qn = cq * lax.rsqrt(jnp.mean(cq * cq, axis=-1, keepdims=True) + EPS) * gq_ref[...]
        qh = _dot(qn, wuq_ref[...])
        ckv = ckv_ref[...]
        kvn = ckv * lax.rsqrt(jnp.mean(ckv * ckv, axis=-1, keepdims=True) + EPS) * gkv_ref[...]
        kv = _dot(kvn, wukv_ref[...])
        kr = _rope(jnp.where(_rope_lanes((ts, LANE)), misc_ref[...], 0.0), cosv, sav, sbv)
        for h in range(MLA_HEADS):
            sl = slice(LANE * h, LANE * (h + 1))
            q_ref[:, sl] = (_rope(qh[:, sl], cosv, sav, sbv) * _Q_SCALE).astype(_ACT)
            k_ref[:, sl] = (kv[:, sl] + kr).astype(_ACT)
        v_ref[...] = kv[:, hw:].astype(_ACT)
        vt_ref[...] = kv[:, hw:].T.astype(_ACT)

    oshape = jax.ShapeDtypeStruct((S, hw), _ACT)
    return pl.pallas_call(
        body, grid=(S // ts,),
        in_specs=[_row(ts, MLA_QR, C_CQ // MLA_QR), _row(ts, MLA_KVR, C_CKV // MLA_KVR), _row(ts, LANE, C_MISC // LANE),
                  _row(ts, LANE), _row(ts, LANE), _row(ts, LANE), _vec(MLA_QR), _vec(MLA_KVR),
                  _vec(hw, MLA_QR), _vec(2 * hw, MLA_KVR)],
        out_specs=(_row(ts, hw),) * 3 + (pl.BlockSpec((hw, ts), lambda i: (0, i)),),
        out_shape=(oshape,) * 3 + (jax.ShapeDtypeStruct((hw, S), _ACT),), compiler_params=_cp(1), name=name,
    )(proj, proj, proj, cosf, sina, sinb, gq, gkv, wuq, wukv)


def _mla_prep_bwd(proj, dq, dk, dv, dmisc_ssd, cosf, sina, sinb, gq, gkv, wuq, wukv, *, name):
    S = proj.shape[0]
    ts = _tile(S, TS_ROW, SUB)
    hw = MLA_HEADS * LANE

    def body(cq_ref, ckv_ref, dq_ref, dk_ref, dv_ref, dms_ref, cos_ref, sa_ref, sb_ref, gq_ref, gkv_ref,
             wuq_ref, wukv_ref, dcq_ref, dckv_ref, dmisc_ref, dqh_ref, dkv_ref, qn_ref, kvn_ref, dgq_ref, dgkv_ref):
        i = pl.program_id(0)
        cosv, sav, sbv = cos_ref[...], sa_ref[...], sb_ref[...]

        @pl.when(i == 0)
        def _():
            dgq_ref[...] = jnp.zeros_like(dgq_ref)
            dgkv_ref[...] = jnp.zeros_like(dgkv_ref)

        dqh = jnp.concatenate([_rope_t(dq_ref[:, LANE * h:LANE * (h + 1)], cosv, sav, sbv)
                               for h in range(MLA_HEADS)], axis=1)
        dqh_ref[...] = dqh.astype(_ACT)
        dkv = jnp.concatenate([dk_ref[...], dv_ref[...]], axis=1)
        dkv_ref[...] = dkv.astype(_ACT)

        def norm_bwd(x, g, dn, dg_ref, n_ref):
            rstd = lax.rsqrt(jnp.mean(x * x, axis=-1, keepdims=True) + EPS)
            xhat = x * rstd
            n_ref[...] = (xhat * g).astype(_ACT)
            dg_ref[...] += _colsum(dn * xhat)
            dxh = dn * g
            return rstd * (dxh - xhat * jnp.mean(dxh * xhat, axis=-1, keepdims=True))

        dcq_ref[...] = norm_bwd(cq_ref[...], gq_ref[...], _dot_nt(dqh, wuq_ref[...]), dgq_ref, qn_ref).astype(_ACT)
        dckv_ref[...] = norm_bwd(ckv_ref[...], gkv_ref[...], _dot_nt(dkv, wukv_ref[...]), dgkv_ref, kvn_ref).astype(_ACT)
        dks = dk_ref[:, 0:LANE]
        for h in range(1, MLA_HEADS):
            dks = dks + dk_ref[:, LANE * h:LANE * (h + 1)]
        rl = _rope_lanes((ts, LANE))
        dkr = _rope_t(jnp.where(rl, dks, 0.0), cosv, sav, sbv)
        dmisc_ref[...] = (dms_ref[...] + jnp.where(rl, dkr, 0.0)).astype(_ACT)

    act = lambda w: jax.ShapeDtypeStruct((S, w), _ACT)
    return pl.pallas_call(
        body, grid=(S // ts,),
        in_specs=[_row(ts, MLA_QR, C_CQ // MLA_QR), _row(ts, MLA_KVR, C_CKV // MLA_KVR),
                  _row(ts, hw), _row(ts, hw), _row(ts, hw), _row(ts, LANE),
                  _row(ts, LANE), _row(ts, LANE), _row(ts, LANE), _vec(MLA_QR), _vec(MLA_KVR),
                  _vec(hw, MLA_QR), _vec(2 * hw, MLA_KVR)],
        out_specs=(_row(ts, MLA_QR), _row(ts, MLA_KVR), _row(ts, LANE), _row(ts, hw), _row(ts, 2 * hw),
                   _row(ts, MLA_QR), _row(ts, MLA_KVR), _vec(MLA_QR), _vec(MLA_KVR)),
        out_shape=(act(MLA_QR), act(MLA_KVR), act(LANE), act(hw), act(2 * hw), act(MLA_QR), act(MLA_KVR),
                   jax.ShapeDtypeStruct((1, MLA_QR), F32), jax.ShapeDtypeStruct((1, MLA_KVR), F32)),
        compiler_params=_cp(1), name=name,
    )(proj, proj, dq, dk, dv, dmisc_ssd, cosf, sina, sinb, gq, gkv, wuq, wukv)


_MLA_SCALE = 1.0 / math.sqrt(MLA_NOPE + MLA_ROPE)
_LOG2E = 1.4426950408889634
_Q_SCALE = _MLA_SCALE * _LOG2E
ATT_CHUNK = 1024


def _tri_grid(nq, by_key):
    if by_key:
        pairs = [(i, j) for j in range(nq) for i in range(j, nq)]
    else:
        pairs = [(i, j) for i in range(nq) for j in range(i + 1)]
    return jnp.asarray([p[0] for p in pairs], jnp.int32), jnp.asarray([p[1] for p in pairs], jnp.int32)


def _attn_fwd(q, k, vt, *, name):
    S = q.shape[0]
    tq = _tile(S, TQ_ATT)
    nq = S // tq
    itab, jtab = _tri_grid(nq, False)

    def body(it_ref, jt_ref, q_ref, k_ref, vt_ref, o_ref, lse_ref, lset_ref, m_ref, l_ref, acc_ref):
        t = pl.program_id(1)
        i, j = it_ref[t], jt_ref[t]

        @pl.when(j == 0)
        def _():
            m_ref[...] = jnp.full_like(m_ref, -jnp.inf)
            l_ref[...] = jnp.zeros_like(l_ref)
            acc_ref[...] = jnp.zeros_like(acc_ref)

        def step(diagonal):
            s = _dot_nt(k_ref[...], q_ref[...])
            if diagonal:
                kk = lax.broadcasted_iota(jnp.int32, (tq, tq), 0)
                s = jnp.where(kk <= lax.broadcasted_iota(jnp.int32, (tq, tq), 1), s, -jnp.inf)
            m_prev = m_ref[...]
            m_new = jnp.maximum(m_prev, jnp.max(s, axis=0, keepdims=True))
            p = jnp.exp2(s - m_new)
            alpha = jnp.exp2(m_prev - m_new)
            l_ref[...] = alpha * l_ref[...] + _colsum(p)
            acc_ref[...] = alpha * acc_ref[...] + _dot(vt_ref[...], p)
            m_ref[...] = m_new

        pl.when(j < i)(functools.partial(step, False))
        pl.when(j == i)(functools.partial(step, True))

        @pl.when(j == i)
        def _():
            o_ref[...] = (acc_ref[...] / l_ref[...]).T
            lse = m_ref[...] + jnp.log2(l_ref[...])
            lset_ref[...] = jnp.broadcast_to(lse, (SUB, tq))
            lse_ref[...] = jnp.broadcast_to(lse, (LANE, tq)).T

    qspec = pl.BlockSpec((tq, LANE), lambda h, t, it, jt: (it[t], h))
    kspec = pl.BlockSpec((tq, LANE), lambda h, t, it, jt: (jt[t], h))
    vtspec = pl.BlockSpec((LANE, tq), lambda h, t, it, jt: (h, jt[t]))
    oshape = jax.ShapeDtypeStruct((S, MLA_HEADS * LANE), F32)
    return pl.pallas_call(
        body,
        grid_spec=pltpu.PrefetchScalarGridSpec(
            num_scalar_prefetch=2, grid=(MLA_HEADS, itab.shape[0]), in_specs=[qspec, kspec, vtspec],
            out_specs=(qspec, qspec, pl.BlockSpec((SUB, tq), lambda h, t, it, jt: (h, it[t]))),
            scratch_shapes=[pltpu.VMEM((1, tq), F32), pltpu.VMEM((1, tq), F32), pltpu.VMEM((LANE, tq), F32)]),
        out_shape=(oshape, oshape, jax.ShapeDtypeStruct((MLA_HEADS * SUB, S), F32)),
        compiler_params=_cp(2), name=name)(itab, jtab, q, k, vt)


def _attn_bwd_dq(q, k, v, o, lse, dycat, *, name):
    S = q.shape[0]
    tq = _tile(S, TQ_ATT)
    nq = S // tq
    rc = min(ATT_CHUNK, tq)
    itab, jtab = _tri_grid(nq, False)

    def body(it_ref, jt_ref, q_ref, k_ref, v_ref, o_ref, lse_ref, do_ref, dq_ref, acc_ref):
        t = pl.program_id(1)
        i, j = it_ref[t], jt_ref[t]

        @pl.when(j == 0)
        def _():
            acc_ref[...] = jnp.zeros_like(acc_ref)

        def step(diagonal):
            kv, vv = k_ref[...], v_ref[...]
            for r in range(tq // rc):
                rows = slice(r * rc, (r + 1) * rc)
                s = _dot_nt(q_ref[rows, :], kv)
                if diagonal:
                    rr = r * rc + lax.broadcasted_iota(jnp.int32, (rc, tq), 0)
                    s = jnp.where(lax.broadcasted_iota(jnp.int32, (rc, tq), 1) <= rr, s, -jnp.inf)
                p = jnp.exp2(s - lse_ref[rows, 0:1])
                dov = do_ref[rows, :]
                delta = _rowsum(dov * o_ref[rows, :])
                ds = p * (_dot_nt(dov, vv) - delta)
                acc_ref[rows, :] += _dot(ds, kv)

        pl.when(j < i)(functools.partial(step, False))
        pl.when(j == i)(functools.partial(step, True))

        @pl.when(j == i)
        def _():
            dq_ref[...] = acc_ref[...] * _MLA_SCALE

    qspec = pl.BlockSpec((tq, LANE), lambda h, t, it, jt: (it[t], h))
    kspec = pl.BlockSpec((tq, LANE), lambda h, t, it, jt: (jt[t], h))
    dospec = pl.BlockSpec((tq, LANE), lambda h, t, it, jt: (it[t], SSD_INNER // LANE + h))
    return pl.pallas_call(
        body,
        grid_spec=pltpu.PrefetchScalarGridSpec(
            num_scalar_prefetch=2, grid=(MLA_HEADS, itab.shape[0]),
            in_specs=[qspec, kspec, kspec, qspec, qspec, dospec], out_specs=qspec,
            scratch_shapes=[pltpu.VMEM((tq, LANE), F32)]),
        out_shape=jax.ShapeDtypeStruct((S, MLA_HEADS * LANE), F32),
        compiler_params=_cp(2), name=name)(itab, jtab, q, k, v, o, lse, dycat)


def _attn_bwd_dkv(q, k, v, o, lset, dycat, *, name):
    S = q.shape[0]
    tq = _tile(S, TQ_ATT)
    nq = S // tq
    kc = min(ATT_CHUNK, tq)
    itab, jtab = _tri_grid(nq, True)

    def body(it_ref, jt_ref, q_ref, k_ref, v_ref, o_ref, lset_ref, do_ref, dk_ref, dv_ref, dk_acc, dv_acc):
        t = pl.program_id(1)
        i, j = it_ref[t], jt_ref[t]

        @pl.when(i == j)
        def _():
            dk_acc[...] = jnp.zeros_like(dk_acc)
            dv_acc[...] = jnp.zeros_like(dv_acc)

        def step(diagonal):
            qv, dov = q_ref[...], do_ref[...]
            delta = lax.dot_general(jnp.ones((SUB, LANE), F32), dov * o_ref[...], (((1,), (1,)), ((), ())),
                                    precision=_HI, preferred_element_type=F32)[0:1]
            lse = lset_ref[0:1, :]
            for c in range(tq // kc):
                rows = slice(c * kc, (c + 1) * kc)
                s = _dot_nt(k_ref[rows, :], qv)
                if diagonal:
                    kk = c * kc + lax.broadcasted_iota(jnp.int32, (kc, tq), 0)
                    s = jnp.where(kk <= lax.broadcasted_iota(jnp.int32, (kc, tq), 1), s, -jnp.inf)
                p = jnp.exp2(s - lse)
                dv_acc[rows, :] += _dot(p, dov)
                ds = p * (_dot_nt(v_ref[rows, :], dov) - delta)
                dk_acc[rows, :] += _dot(ds, qv)

        pl.when(i > j)(functools.partial(step, False))
        pl.when(i == j)(functools.partial(step, True))

        @pl.when(i == nq - 1)
        def _():
            dk_ref[...] = dk_acc[...] * (1.0 / _LOG2E)
            dv_ref[...] = dv_acc[...]

    qspec = pl.BlockSpec((tq, LANE), lambda h, t, it, jt: (it[t], h))
    kspec = pl.BlockSpec((tq, LANE), lambda h, t, it, jt: (jt[t], h))
    dospec = pl.BlockSpec((tq, LANE), lambda h, t, it, jt: (it[t], SSD_INNER // LANE + h))
    lspec = pl.BlockSpec((SUB, tq), lambda h, t, it, jt: (h, it[t]))
    oshape = jax.ShapeDtypeStruct((S, MLA_HEADS * LANE), F32)
    return pl.pallas_call(
        body,
        grid_spec=pltpu.PrefetchScalarGridSpec(
            num_scalar_prefetch=2, grid=(MLA_HEADS, itab.shape[0]),
            in_specs=[qspec, kspec, kspec, qspec, lspec, dospec], out_specs=(kspec, kspec),
            scratch_shapes=[pltpu.VMEM((tq, LANE), F32), pltpu.VMEM((tq, LANE), F32)]),
        out_shape=(oshape, oshape), compiler_params=_cp(2), name=name)(itab, jtab, q, k, v, o, lset, dycat)


_SWA_SCALE = 1.0 / math.sqrt(SWA_HD)
_SWA_KW = SWA_KV * LANE


def _swa_specs(S, ts, rev):
    n = S // ts
    t = (lambda i: n - 1 - i) if rev else (lambda i: i)
    hb = lambda i: jnp.maximum(t(i) * (ts // WINDOW) - 1, 0)
    return [
        pl.BlockSpec((ts, SWA_HEADS * LANE), lambda i: (t(i), C_SQ // (SWA_HEADS * LANE))),
        pl.BlockSpec((ts, _SWA_KW), lambda i: (t(i), C_SK // _SWA_KW)),
        pl.BlockSpec((WINDOW, _SWA_KW), lambda i: (hb(i), C_SK // _SWA_KW)),
        pl.BlockSpec((ts, _SWA_KW), lambda i: (t(i), C_SV // _SWA_KW)),
        pl.BlockSpec((WINDOW, _SWA_KW), lambda i: (hb(i), C_SV // _SWA_KW)),
    ]


def _swa_scores(qh, kk, t, b, ts):
    s = _dot_nt(qh, kk) * _SWA_SCALE
    row = lax.broadcasted_iota(jnp.int32, (WINDOW, 2 * WINDOW), 0)
    col = lax.broadcasted_iota(jnp.int32, (WINDOW, 2 * WINDOW), 1)
    rel = WINDOW + row - col
    kpos = t * ts + (b - 1) * WINDOW + col
    return jnp.where((rel >= 0) & (rel < WINDOW) & (kpos >= 0), s, -jnp.inf)


def _swa_fwd(proj, sinks, *, name):
    S = proj.shape[0]
    ts = _tile(S, TS_SWA)
    nb = ts // WINDOW

    def body(q_ref, k_ref, kh_ref, v_ref, vh_ref, sink_ref, o_ref, lse_ref):
        t = pl.program_id(0)
        kext = jnp.concatenate([kh_ref[...], k_ref[...]], axis=0)
        vext = jnp.concatenate([vh_ref[...], v_ref[...]], axis=0)
        for b in range(nb):
            rows = slice(WINDOW * b, WINDOW * (b + 1))
            for h in range(SWA_HEADS):
                kvl = slice(LANE * (h // (SWA_HEADS // SWA_KV)), LANE * (h // (SWA_HEADS // SWA_KV) + 1))
                hl = slice(LANE * h, LANE * (h + 1))
                kk = kext[WINDOW * b:WINDOW * (b + 2), kvl]
                vv = vext[WINDOW * b:WINDOW * (b + 2), kvl]
                s = _swa_scores(q_ref[rows, hl], kk, t, b, ts)
                sk = sink_ref[:, h:h + 1]
                m = jnp.maximum(jnp.max(s, axis=1, keepdims=True), sk)
                p = jnp.exp(s - m)
                den = _rowsum(p) + jnp.exp(sk - m)
                o_ref[rows, hl] = _dot(p, vv) / den
                lse_ref[rows, hl] = jnp.broadcast_to(m + jnp.log(den), (WINDOW, LANE))

    oshape = jax.ShapeDtypeStruct((S, SWA_HEADS * LANE), F32)
    ospec = pl.BlockSpec((ts, SWA_HEADS * LANE), lambda i: (i, 0))
    return pl.pallas_call(
        body, grid=(S // ts,), in_specs=_swa_specs(S, ts, False) + [_vec(LANE)], out_specs=(ospec, ospec),
        out_shape=(oshape, oshape), compiler_params=_cp(1), name=name)(proj, proj, proj, proj, proj, sinks)


def _swa_bwd(proj, o, lse, dycat, sinks, *, name):
    S = proj.shape[0]
    ts = _tile(S, TS_SWA)
    nb = ts // WINDOW
    n = S // ts
    grp = SWA_HEADS // SWA_KV

    def body(q_ref, k_ref, kh_ref, v_ref, vh_ref, o_ref, lse_ref, do_ref, sink_ref,
             dq_ref, dk_ref, dv_ref, dsink_ref, dk_carry, dv_carry):
        i = pl.program_id(0)
        t = n - 1 - i

        @pl.when(i == 0)
        def _():
            dk_carry[...] = jnp.zeros_like(dk_carry)
            dv_carry[...] = jnp.zeros_like(dv_carry)
            dsink_ref[...] = jnp.zeros_like(dsink_ref)

        kext = jnp.concatenate([kh_ref[...], k_ref[...]], axis=0)
        vext = jnp.concatenate([vh_ref[...], v_ref[...]], axis=0)
        lane1 = lax.broadcasted_iota(jnp.int32, (1, LANE), 1)
        dkb = [[jnp.zeros((WINDOW, LANE), F32) for _ in range(SWA_KV)] for _ in range(nb + 1)]
        dvb = [[jnp.zeros((WINDOW, LANE), F32) for _ in range(SWA_KV)] for _ in range(nb + 1)]
        dsink = jnp.zeros((1, LANE), F32)
        for b in range(nb):
            rows = slice(WINDOW * b, WINDOW * (b + 1))
            for h in range(SWA_HEADS):
                kvh = h // grp
                kvl = slice(LANE * kvh, LANE * (kvh + 1))
                hl = slice(LANE * h, LANE * (h + 1))
                kk = kext[WINDOW * b:WINDOW * (b + 2), kvl]
                vv = vext[WINDOW * b:WINDOW * (b + 2), kvl]
                qh = q_ref[rows, hl]
                lse_h = lse_ref[rows, LANE * h:LANE * h + 1]
                p = jnp.exp(_swa_scores(qh, kk, t, b, ts) - lse_h)
                doh = do_ref[rows, hl]
                delta = _rowsum(doh * o_ref[rows, hl])
                ds = p * (_dot_nt(doh, vv) - delta)
                sk = sink_ref[:, h:h + 1]
                dsink = dsink + jnp.where(lane1 == h, -jnp.sum(jnp.exp(sk - lse_h) * delta, keepdims=True), 0.0)
                dq_ref[rows, hl] = (_dot(ds, kk) * _SWA_SCALE).astype(_ACT)
                dkk = _dot_tn(ds, qh) * _SWA_SCALE
                dvv = _dot_tn(p, doh)
                dkb[b][kvh] = dkb[b][kvh] + dkk[:WINDOW]
                dkb[b + 1][kvh] = dkb[b + 1][kvh] + dkk[WINDOW:]
                dvb[b][kvh] = dvb[b][kvh] + dvv[:WINDOW]
                dvb[b + 1][kvh] = dvb[b + 1][kvh] + dvv[WINDOW:]
        dsink_ref[...] += dsink
        for dref, blocks, carry in ((dk_ref, dkb, dk_carry), (dv_ref, dvb, dv_carry)):
            old = carry[...]
            for b in range(1, nb + 1):
                blk = jnp.concatenate(blocks[b], axis=1)
                if b == nb:
                    blk = blk + old
                dref[WINDOW * (b - 1):WINDOW * b, :] = blk.astype(_ACT)
            carry[...] = jnp.concatenate(blocks[0], axis=1)

    hw = SWA_HEADS * LANE
    rev = lambda i: (n - 1 - i, 0)
    mix = lambda i: (n - 1 - i, (SSD_INNER + MLA_HEADS * LANE) // hw)
    return pl.pallas_call(
        body, grid=(n,),
        in_specs=_swa_specs(S, ts, True) + [pl.BlockSpec((ts, hw), rev), pl.BlockSpec((ts, hw), rev),
                                            pl.BlockSpec((ts, hw), mix), _vec(LANE)],
        out_specs=(pl.BlockSpec((ts, hw), rev), pl.BlockSpec((ts, _SWA_KW), rev), pl.BlockSpec((ts, _SWA_KW), rev),
                   _vec(LANE)),
        out_shape=(jax.ShapeDtypeStruct((S, hw), _ACT), jax.ShapeDtypeStruct((S, _SWA_KW), _ACT),
                   jax.ShapeDtypeStruct((S, _SWA_KW), _ACT), jax.ShapeDtypeStruct((1, LANE), F32)),
        scratch_shapes=[pltpu.VMEM((WINDOW, _SWA_KW), F32), pltpu.VMEM((WINDOW, _SWA_KW), F32)],
        compiler_params=_cp(1), name=name)(proj, proj, proj, proj, proj, o, lse, dycat, sinks)


def _exchange(arrays, *, scatter, name):
    n = len(arrays)

    def body(*refs):
        ins, outs = refs[:n], refs[n:2 * n]
        send_sems, recv_sems, loc_sems = refs[2 * n:]
        x, y, c = lax.axis_index("x"), lax.axis_index("y"), lax.axis_index("c")
        me = 4 * x + 2 * y + c

        def src(i, dest):
            return ins[i].at[dest] if scatter else ins[i]

        local = [pltpu.make_async_copy(src(i, me), outs[i].at[me], loc_sems.at[i]) for i in range(n)]
        for cp in local:
            cp.start()
        sends, recvs = [], []
        for k in range(1, NDEV):
            px = 1 - x if k & 4 else x
            py = 1 - y if k & 2 else y
            pc = 1 - c if k & 1 else c
            peer = 4 * px + 2 * py + pc
            for i in range(n):
                common = dict(send_sem=send_sems.at[i, k - 1], recv_sem=recv_sems.at[i, k - 1],
                              device_id=(px, py, pc), device_id_type=pl.DeviceIdType.MESH)
                sends.append(pltpu.make_async_remote_copy(src_ref=src(i, peer), dst_ref=outs[i].at[me], **common))
                recvs.append(pltpu.make_async_remote_copy(src_ref=src(i, peer), dst_ref=outs[i].at[peer], **common))
        for cp in sends:
            cp.start()
        for cp in recvs:
            cp.wait_recv()
        for cp in sends:
            cp.wait_send()
        for cp in local:
            cp.wait()

    hbm = pl.BlockSpec(memory_space=pl.ANY)
    out_shape = tuple(jax.ShapeDtypeStruct(a.shape if scatter else (NDEV,) + a.shape, a.dtype) for a in arrays)
    return pl.pallas_call(
        body, in_specs=[hbm] * n, out_specs=tuple([hbm] * n), out_shape=out_shape,
        scratch_shapes=[pltpu.SemaphoreType.DMA((n, NDEV - 1)), pltpu.SemaphoreType.DMA((n, NDEV - 1)),
                        pltpu.SemaphoreType.DMA((n,))],
        name=name)(*arrays)


def _adamw(w, m, v, parts, *, name):
    R, C = w.shape
    npart = parts.shape[0]
    cap = max(SUB, ((1 << 18) // C) // SUB * SUB)
    tr = _tile(R, cap, SUB)

    def body(w_ref, m_ref, v_ref, p_ref, g_ref, d_ref, mo_ref, vo_ref):
        g = p_ref[0]
        for k in range(1, npart):
            g = g + p_ref[k]
        mn = ADAM_B1 * m_ref[...] + (1.0 - ADAM_B1) * g
        vn = ADAM_B2 * v_ref[...] + (1.0 - ADAM_B2) * (g * g)
        m_hat = mn / (1.0 - ADAM_B1 ** ADAM_STEP)
        v_hat = vn / (1.0 - ADAM_B2 ** ADAM_STEP)
        g_ref[...] = g
        d_ref[...] = -ADAM_LR * (m_hat / (jnp.sqrt(v_hat) + ADAM_EPS) + ADAM_WD * w_ref[...])
        mo_ref[...] = mn
        vo_ref[...] = vn

    spec = pl.BlockSpec((tr, C), lambda i: (i, 0))
    oshape = jax.ShapeDtypeStruct((R, C), F32)
    return pl.pallas_call(
        body, grid=(R // tr,), in_specs=[spec] * 3 + [pl.BlockSpec((npart, tr, C), lambda i: (0, i, 0))],
        out_specs=(spec,) * 4, out_shape=(oshape,) * 4, compiler_params=_cp(1), name=name)(w, m, v, parts)


def _adamw_many(ws, ms, vs, landed, mine, me, *, name):
    n = len(ws)

    def body(me_ref, *refs):
        w_r, m_r, v_r, p_r, o_r = (refs[k * n:(k + 1) * n] for k in range(5))
        outs = refs[5 * n:]
        for i in range(n):
            own = o_r[i][...]
            g = jnp.where(me_ref[0] == 0, own, p_r[i][0])
            for k in range(1, NDEV):
                g = g + jnp.where(me_ref[0] == k, own, p_r[i][k])
            mn = ADAM_B1 * m_r[i][...] + (1.0 - ADAM_B1) * g
            vn = ADAM_B2 * v_r[i][...] + (1.0 - ADAM_B2) * (g * g)
            m_hat = mn / (1.0 - ADAM_B1 ** ADAM_STEP)
            v_hat = vn / (1.0 - ADAM_B2 ** ADAM_STEP)
            outs[4 * i][...] = g
            outs[4 * i + 1][...] = -ADAM_LR * (m_hat / (jnp.sqrt(v_hat) + ADAM_EPS) + ADAM_WD * w_r[i][...])
            outs[4 * i + 2][...] = mn
            outs[4 * i + 3][...] = vn

    vmem = pl.BlockSpec(memory_space=pltpu.VMEM)
    return pl.pallas_call(
        body, in_specs=[pl.BlockSpec(memory_space=pltpu.SMEM)] + [vmem] * (5 * n), out_specs=(vmem,) * (4 * n),
        out_shape=tuple(jax.ShapeDtypeStruct(w.shape, F32) for w in ws for _ in range(4)),
        name=name)(me, *ws, *ms, *vs, *landed, *mine)


def _adamw_layer(l, w, m, v, landed, mine, me, prev, *, name):
    L, R, C = w.shape
    npart = landed.shape[0]
    cap = max(2 * SUB, ((1 << 18) // C) // (2 * SUB) * (2 * SUB))
    tr = _tile(R, cap, 2 * SUB)
    nprev = 0 if prev is None else 4

    def body(me_ref, *refs):
        w_ref, m_ref, v_ref, p_ref, own_ref = refs[:5]
        g_ref, d_ref, mo_ref, vo_ref = refs[5 + nprev:]
        own = own_ref[...].astype(F32)
        g = jnp.where(me_ref[0] == 0, own, p_ref[0].astype(F32))
        for k in range(1, npart):
            g = g + jnp.where(me_ref[0] == k, own, p_ref[k].astype(F32))
        mn = ADAM_B1 * m_ref[...] + (1.0 - ADAM_B1) * g
        vn = ADAM_B2 * v_ref[...] + (1.0 - ADAM_B2) * (g * g)
        m_hat = mn / (1.0 - ADAM_B1 ** ADAM_STEP)
        v_hat = vn / (1.0 - ADAM_B2 ** ADAM_STEP)
        g_ref[...] = g
        d_ref[...] = -ADAM_LR * (m_hat / (jnp.sqrt(v_hat) + ADAM_EPS) + ADAM_WD * w_ref[...])
        mo_ref[...] = mn
        vo_ref[...] = vn

    spec = pl.BlockSpec((None, tr, C), lambda i, me_ref: (l, i, 0))
    oshape = jax.ShapeDtypeStruct((L, R, C), F32)
    return pl.pallas_call(
        body,
        grid_spec=pltpu.PrefetchScalarGridSpec(
            num_scalar_prefetch=1, grid=(R // tr,),
            in_specs=[spec] * 3 + [pl.BlockSpec((npart, tr, C), lambda i, me_ref: (0, i, 0)),
                                   pl.BlockSpec((None, tr, C), lambda i, me_ref: (me_ref[0], i, 0))]
            + [pl.BlockSpec(memory_space=pl.ANY)] * nprev,
            out_specs=(spec,) * 4),
        out_shape=(oshape,) * 4, input_output_aliases={6 + k: k for k in range(nprev)},
        compiler_params=_cp(1), name=name)(me, w, m, v, landed, mine, *(prev or ()))


_HBM = pl.BlockSpec(memory_space=pltpu.HBM)
_SEM = pl.BlockSpec(memory_space=pltpu.SEMAPHORE)
_EFFECT = pltpu.SideEffectType.DATAFLOW_SIDE_EFFECTING


def _peers():
    x, y, c = lax.axis_index("x"), lax.axis_index("y"), lax.axis_index("c")
    out = []
    for k in range(1, NDEV):
        px = 1 - x if k & 4 else x
        py = 1 - y if k & 2 else y
        pc = 1 - c if k & 1 else c
        out.append((k - 1, (px, py, pc), 4 * px + 2 * py + pc))
    return 4 * x + 2 * y + c, out


def _xchg_start(arrays, *, scatter, name):
    n = len(arrays)
    lands = [lax.empty(a.shape if scatter else (NDEV,) + a.shape, a.dtype) for a in arrays]

    def body(*refs):
        ins, lnd = refs[:n], refs[n:2 * n]
        send_sems, recv_sems = refs[2 * n], refs[2 * n + 1]
        token = refs[-1]
        me, peers = _peers()
        for k, dev, peer in peers:
            for i in range(n):
                pltpu.make_async_remote_copy(
                    src_ref=ins[i].at[peer] if scatter else ins[i], dst_ref=lnd[i].at[me],
                    send_sem=send_sems.at[i * (NDEV - 1) + k], recv_sem=recv_sems.at[i * (NDEV - 1) + k],
                    device_id=dev, device_id_type=pl.DeviceIdType.MESH).start()
        token[...] = jnp.zeros_like(token)

    sems = pltpu.SemaphoreType.DMA((n * (NDEV - 1),))
    res = pl.pallas_call(
        body, name=name,
        out_shape=(sems, sems) + tuple(pltpu.HBM(t.shape, t.dtype) for t in list(arrays) + lands)
        + (jax.ShapeDtypeStruct((SUB, LANE), F32),),
        in_specs=[_HBM] * (2 * n), out_specs=(_SEM, _SEM) + (_HBM,) * (2 * n) + (pl.BlockSpec(memory_space=pltpu.VMEM),),
        input_output_aliases={i: 2 + i for i in range(2 * n)},
        compiler_params=pltpu.CompilerParams(has_side_effects=_EFFECT),
    )(*[pltpu.with_memory_space_constraint(t, pltpu.HBM) for t in list(arrays) + lands])
    return dict(send=res[0], recv=res[1], thru=list(res[2:2 + 2 * n]), token=res[-1], scatter=scatter, n=n)


def _xchg_wait(handle, after, *, name):
    n, scatter = handle["n"], handle["scatter"]
    thru = handle["thru"]

    def body(*refs):
        ins, lnd = refs[:n], refs[n:2 * n]
        send_sems, recv_sems = refs[2 * n], refs[2 * n + 1]
        me, peers = _peers()
        for k, dev, peer in peers:
            for i in range(n):
                cp = pltpu.make_async_remote_copy(
                    src_ref=ins[i].at[peer] if scatter else ins[i], dst_ref=lnd[i].at[peer],
                    send_sem=send_sems.at[i * (NDEV - 1) + k], recv_sem=recv_sems.at[i * (NDEV - 1) + k],
                    device_id=dev, device_id_type=pl.DeviceIdType.MESH)
                cp.wait_send()
                cp.wait_recv()

    res = pl.pallas_call(
        body, name=name, out_shape=tuple(pltpu.HBM(t.shape, t.dtype) for t in thru),
        in_specs=[_HBM] * (2 * n) + [_SEM, _SEM, pl.BlockSpec(memory_space=pl.ANY)], out_specs=(_HBM,) * (2 * n),
        input_output_aliases={i: i for i in range(2 * n)},
        compiler_params=pltpu.CompilerParams(has_side_effects=_EFFECT),
    )(*thru, handle["send"], handle["recv"], after)
    return list(res[:n]), list(res[n:])


def _pad_heads(w, nh, hd, axis=-1):
    axis = axis % w.ndim
    shp = w.shape
    w = w.reshape(shp[:axis] + (nh, hd) + shp[axis + 1:])
    pads = [(0, 0)] * w.ndim
    pads[axis + 1] = (0, LANE - hd)
    return jnp.pad(w, pads).reshape(shp[:axis] + (nh * LANE,) + shp[axis + 1:])


def _unpad_heads(w, nh, hd, axis=-1):
    axis = axis % w.ndim
    shp = w.shape
    w = w.reshape(shp[:axis] + (nh, LANE) + shp[axis + 1:])
    w = lax.slice_in_dim(w, 0, hd, axis=axis + 1)
    return w.reshape(shp[:axis] + (nh * hd,) + shp[axis + 1:])


_O_DT = SSD_INNER + SSD_XBC
_O_CQ = _O_DT + SSD_HEADS
_O_CKV = _O_CQ + MLA_QR
_O_KR = _O_CKV + MLA_KVR
_O_SQ = _O_KR + MLA_ROPE
_O_SK = _O_SQ + SWA_HEADS * SWA_HD
_O_SV = _O_SK + SWA_KV * SWA_HD


def _w_in_to_padded(w, axis=-1):
    axis = axis % w.ndim
    cut = lambda a, b: lax.slice_in_dim(w, a, b, axis=axis)
    z, xbc, dt = cut(0, SSD_INNER), cut(SSD_INNER, _O_DT), cut(_O_DT, _O_CQ)
    cq, ckv, kr = cut(_O_CQ, _O_CKV), cut(_O_CKV, _O_KR), cut(_O_KR, _O_SQ)
    sq, sk, sv = cut(_O_SQ, _O_SK), cut(_O_SK, _O_SV), cut(_O_SV, D_IN)
    zeros = lambda n: jnp.zeros(w.shape[:axis] + (n,) + w.shape[axis + 1:], w.dtype)
    return jnp.concatenate([xbc, z, cq, ckv, dt, zeros(ROPE_LANE - SSD_HEADS), kr, zeros(LANE - ROPE_LANE - MLA_ROPE),
                            _pad_heads(sq, SWA_HEADS, SWA_HD, axis), _pad_heads(sk, SWA_KV, SWA_HD, axis),
                            _pad_heads(sv, SWA_KV, SWA_HD, axis)], axis=axis)


def _w_in_from_padded(g, axis=-1):
    axis = axis % g.ndim
    cut = lambda a, b: lax.slice_in_dim(g, a, b, axis=axis)
    xbc, z, cq, ckv = cut(C_XBC, C_Z), cut(C_Z, C_CQ), cut(C_CQ, C_CKV), cut(C_CKV, C_MISC)
    dt, kr = cut(C_MISC, C_MISC + SSD_HEADS), cut(C_MISC + ROPE_LANE, C_MISC + ROPE_LANE + MLA_ROPE)
    sq = _unpad_heads(cut(C_SQ, C_SK), SWA_HEADS, SWA_HD, axis)
    sk = _unpad_heads(cut(C_SK, C_SV), SWA_KV, SWA_HD, axis)
    sv = _unpad_heads(cut(C_SV, D_INP), SWA_KV, SWA_HD, axis)
    return jnp.concatenate([z, xbc, dt, cq, ckv, kr, sq, sk, sv], axis=axis)


def _w_out_to_padded(w):
    a = SSD_INNER
    b = a + MLA_HEADS * MLA_V
    return jnp.concatenate([w[..., :a, :], _pad_heads(w[..., a:b, :], MLA_HEADS, MLA_V, axis=-2),
                            _pad_heads(w[..., b:, :], SWA_HEADS, SWA_HD, axis=-2)], axis=-2)


def _w_out_from_padded(g):
    a = SSD_INNER
    b = a + MLA_HEADS * LANE
    return jnp.concatenate([g[..., :a, :], _unpad_heads(g[..., a:b, :], MLA_HEADS, MLA_V, axis=-2),
                            _unpad_heads(g[..., b:, :], SWA_HEADS, SWA_HD, axis=-2)], axis=-2)


def _w_ukv_to_padded(w):
    w4 = w.reshape(w.shape[:-1] + (MLA_HEADS, MLA_NOPE + MLA_V))
    flat = lambda t: t.reshape(w.shape[:-1] + (MLA_HEADS * t.shape[-1],))
    return jnp.concatenate([_pad_heads(flat(w4[..., :MLA_NOPE]), MLA_HEADS, MLA_NOPE),
                            _pad_heads(flat(w4[..., MLA_NOPE:]), MLA_HEADS, MLA_V)], axis=-1)


def _w_ukv_from_padded(g):
    hw = MLA_HEADS * LANE
    gk = _unpad_heads(g[..., :hw], MLA_HEADS, MLA_NOPE).reshape(g.shape[:-1] + (MLA_HEADS, MLA_NOPE))
    gv = _unpad_heads(g[..., hw:], MLA_HEADS, MLA_V).reshape(g.shape[:-1] + (MLA_HEADS, MLA_V))
    return jnp.concatenate([gk, gv], axis=-1).reshape(g.shape[:-1] + (MLA_HEADS * (MLA_NOPE + MLA_V),))


def _pad_lane(v):
    return jnp.pad(v, [(0, 0)] * (v.ndim - 1) + [(0, LANE - v.shape[-1])])


def _rope_tables(positions):
    inv_freq = ROPE_THETA ** (-jnp.arange(0, MLA_ROPE, 2, dtype=F32) / MLA_ROPE)
    ang = positions.astype(F32).reshape(-1, 1) * inv_freq
    cos, sin = jnp.cos(ang), jnp.sin(ang)
    S = ang.shape[0]
    one, zero = jnp.ones((S, ROPE_LANE), F32), jnp.zeros((S, ROPE_LANE), F32)
    tail1, tail0 = jnp.ones((S, LANE - ROPE_LANE - MLA_ROPE), F32), jnp.zeros((S, LANE - ROPE_LANE - MLA_ROPE), F32)
    z16 = jnp.zeros_like(sin)
    return (jnp.concatenate([one, cos, cos, tail1], axis=1), jnp.concatenate([zero, -sin, z16, tail0], axis=1),
            jnp.concatenate([zero, z16, sin, tail0], axis=1))


def _layer_fwd(l, x_in, f_prev, gate_prev, mod, P, tabs):
    sh1, sc1, g1, sh2, sc2, g2 = [mod[k:k + 1] for k in range(6)]
    tag = f"l{l}_"
    if f_prev is None:
        x0 = x_in
        h1 = _norm_fwd(x0, P["n1g"], sc1, sh1, name=tag + "norm1")
    else:
        x0, h1 = _norm_fwd(x_in, P["n1g"], sc1, sh1, f=f_prev, gate=gate_prev, name=tag + "norm1")
    proj = _mm(h1, P["w_in"], tb=True, name=tag + "proj")
    P.update(P.pop("mid")(proj))
    y_ssd, hin = _ssd_fwd(proj, P["ssd_cw"], P["ssd_cb"], P["dtb"], P["alog"], P["dsk"],
                          P["ssd_ng"], name=tag + "ssd")
    q, k, v, vt = _mla_prep_fwd(proj, *tabs, P["gq"], P["gkv"], P["w_uq"], P["w_ukv"], name=tag + "mla_prep")
    o_mla, lse_mla, lset_mla = _attn_fwd(q, k, vt, name=tag + "mla_attn")
    o_swa, lse_swa = _swa_fwd(proj, P["sinks"], name=tag + "swa")
    ycat = jnp.concatenate([y_ssd.astype(_ACT), o_mla.astype(_ACT), o_swa.astype(_ACT)], axis=1)
    y = _mm(ycat, P["w_out"], name=tag + "out")
    P.update(P.pop("late")(y))
    x1, h2 = _norm_fwd(x0, P["n2g"], sc2, sh2, f=y, gate=g1, name=tag + "norm2")
    up = _mm(h2, P["w_up"], tb=True, name=tag + "up")
    act = _ffn_act_fwd(up, P["fcw"], P["fcb"], name=tag + "ffn_act")
    f = _mm(act, P["w_down"], name=tag + "down")
    saved = dict(x0=x0, h1=h1, proj=proj, hin=hin, q=q, k=k, v=v, o_mla=o_mla, lse_mla=lse_mla, lset_mla=lset_mla, o_swa=o_swa,
                 lse_swa=lse_swa, ycat=ycat, y=y, x1=x1, h2=h2, up=up, act=act, f=f, mod=mod)
    return x1, f, g2, saved


def _layer_bwd(l, dxo, ffn_branch, sv, P, tabs, on_part, below):
    mod = sv["mod"]
    sh1, sc1, g1, sh2, sc2, g2 = [mod[k:k + 1] for k in range(6)]
    tag = f"l{l}_b_"
    G = {}
    df, dg2 = ffn_branch
    dact = _mm(df, P["w_down"], tb=True, name=tag + "dact")
    G["w_down"] = _mm(sv["act"], df, ta=True, out_dtype=_ACT, name=tag + "dw_down")
    dup, G["fcw"], G["fcb"] = _ffn_bwd(sv["up"], dact, P["fcw"], P["fcb"], name=tag + "ffn")
    dh2 = _mm(dup, P["w_up"], name=tag + "dh2")
    G["w_up"] = _mm(dup, sv["h2"], ta=True, out_dtype=_ACT, name=tag + "dw_up")
    token = on_part(l, "ffn", G)
    if token is not None:
        sc2 = sc2 + token
    dx1, G["n2g"], dsc2, dsh2, dy, dg1 = _norm_bwd(dh2, sv["x1"], dxo, P["n2g"], sc2, branch=(sv["y"], g1),
                                                   name=tag + "norm2")
    dycat = _mm(dy, P["w_out"], tb=True, name=tag + "dycat")
    G["w_out"] = _mm(sv["ycat"], dy, ta=True, out_dtype=_ACT, name=tag + "dw_out")
    token = on_part(l, "out", G)
    ssd_cb = P["ssd_cb"] if token is None else P["ssd_cb"] + token
    proj = sv["proj"]
    (dpre, dz, dmisc_ssd, G["ssd_cw"], G["ssd_cb"], G["dtb"], G["alog"], G["dsk"], G["ssd_ng"]) = _ssd_bwd(
        proj, dycat, sv["hin"], P["ssd_cw"], ssd_cb, P["dtb"], P["alog"], P["dsk"],
        P["ssd_ng"], name=tag + "ssd")
    att = (sv["q"], sv["k"], sv["v"], sv["o_mla"])
    dq = _attn_bwd_dq(*att, sv["lse_mla"], dycat, name=tag + "mla_dq")
    dk, dv = _attn_bwd_dkv(*att, sv["lset_mla"], dycat, name=tag + "mla_dkv")
    dcq, dckv, dmisc, dqh, dkv, qn, kvn, G["gq"], G["gkv"] = _mla_prep_bwd(
        proj, dq, dk, dv, dmisc_ssd, *tabs, P["gq"], P["gkv"], P["w_uq"], P["w_ukv"], name=tag + "mla_prep")
    G["w_uq"] = _mm(qn, dqh, ta=True, out_dtype=_ACT, name=tag + "dw_uq")
    G["w_ukv"] = _mm(kvn, dkv, ta=True, out_dtype=_ACT, name=tag + "dw_ukv")
    dsq, dsk_, dsv_, G["sinks"] = _swa_bwd(proj, sv["o_swa"], sv["lse_swa"], dycat, P["sinks"], name=tag + "swa")
    dproj = jnp.concatenate([dpre, dz, dcq, dckv, dmisc, dsq, dsk_, dsv_], axis=1)
    G["w_in"] = _mm(dproj, sv["h1"], ta=True, out_dtype=_ACT, name=tag + "dw_in")
    token = on_part(l, "mixer", G)
    if token is not None:
        sc1 = sc1 + token
    dh1 = _mm(dproj, P["w_in"], name=tag + "dh1")
    res = _norm_bwd(dh1, sv["x0"], dx1, P["n1g"], sc1, branch=below, name=tag + "norm1")
    dx0, G["n1g"], dsc1, dsh1 = res[:4]
    G["mod"] = jnp.concatenate([dsh1, dsc1, dg1, dsh2, dsc2, dg2], axis=0)
    return dx0, G, (res[4:] or None)


def _local_step(x, tgt, mods, get_params, tabs, final_g, on_grads, on_part):
    saved, params = [], []
    xin, f, gate = x, None, None
    for l in range(DEPTH):
        params.append(get_params(l, x if f is None else f))
        xin, f, gate, sv = _layer_fwd(l, xin, f, gate, mods[l], params[l], tabs)
        saved.append(sv)
    loss, dx, dfinal, df, dgate = _final_loss(xin, f, gate, final_g, tgt, name="final_loss")
    branch = (df, dgate)
    for l in reversed(range(DEPTH)):
        below = (saved[l - 1]["f"], saved[l - 1]["mod"][5:6]) if l > 0 else None
        dx, G, branch = _layer_bwd(l, dx, branch, saved[l], params[l], tabs, on_part, below)
        on_grads(l, G)
    return loss[0, 0], dx, dfinal


_WEIGHTS = ['ada_w', 'ada_b', 'norm1_g', 'norm2_g', 'w_in', 'ssd_conv_w', 'ssd_conv_b', 'ssd_dt_bias', 'ssd_a_log',
            'ssd_d', 'ssd_norm_g', 'mla_q_norm_g', 'mla_w_uq', 'mla_kv_norm_g', 'mla_w_ukv', 'swa_sinks', 'w_out',
            'ffn_w_up', 'ffn_conv_w', 'ffn_conv_b', 'ffn_w_down', 'final_norm_g']
_INPUTS = ['x', 'c', 'positions'] + _WEIGHTS + ['loss_target'] + ['m_' + n for n in _WEIGHTS] + ['v_' + n for n in _WEIGHTS]
_SMALL = [('ada_b', 'mod'), ('norm1_g', 'n1g'), ('norm2_g', 'n2g'), ('ssd_conv_b', 'ssd_cb'), ('ssd_dt_bias', 'dtb'),
          ('ssd_a_log', 'alog'), ('ssd_d', 'dsk'), ('ssd_norm_g', 'ssd_ng'), ('mla_q_norm_g', 'gq'),
          ('mla_kv_norm_g', 'gkv'), ('swa_sinks', 'sinks'), ('ffn_conv_b', 'fcb')]
_SHARDED = [('w_in', 'w_in', 2), ('ssd_conv_w', 'ssd_cw', 2), ('mla_w_uq', 'w_uq', 2), ('mla_w_ukv', 'w_ukv', 2),
            ('w_out', 'w_out', 1), ('ffn_w_up', 'w_up', 2), ('ffn_conv_w', 'fcw', 2), ('ffn_w_down', 'w_down', 1)]
_SHARDED_NAMES = [n for n, _, _ in _SHARDED]
_TRANSPOSED = ('w_in', 'ffn_w_up')


def _shard_major(g, axis):
    shp = g.shape
    g = g.reshape(shp[:axis] + (NDEV, shp[axis] // NDEV) + shp[axis + 1:])
    return jnp.moveaxis(g, axis, 0)


def _unshard(g, axis):
    g = jnp.moveaxis(g, 0, axis)
    shp = g.shape
    return g.reshape(shp[:axis] + (shp[axis] * shp[axis + 1],) + shp[axis + 2:])


def kernel(x, c, positions, ada_w, ada_b, norm1_g, norm2_g, w_in, ssd_conv_w, ssd_conv_b, ssd_dt_bias, ssd_a_log, ssd_d, ssd_norm_g, mla_q_norm_g, mla_w_uq, mla_kv_norm_g, mla_w_ukv, swa_sinks, w_out, ffn_w_up, ffn_conv_w, ffn_conv_b, ffn_w_down, final_norm_g, loss_target, m_ada_w, m_ada_b, m_norm1_g, m_norm2_g, m_w_in, m_ssd_conv_w, m_ssd_conv_b, m_ssd_dt_bias, m_ssd_a_log, m_ssd_d, m_ssd_norm_g, m_mla_q_norm_g, m_mla_w_uq, m_mla_kv_norm_g, m_mla_w_ukv, m_swa_sinks, m_w_out, m_ffn_w_up, m_ffn_conv_w, m_ffn_conv_b, m_ffn_w_down, m_final_norm_g, v_ada_w, v_ada_b, v_norm1_g, v_norm2_g, v_w_in, v_ssd_conv_w, v_ssd_conv_b, v_ssd_dt_bias, v_ssd_a_log, v_ssd_d, v_ssd_norm_g, v_mla_q_norm_g, v_mla_w_uq, v_mla_kv_norm_g, v_mla_w_ukv, v_swa_sinks, v_w_out, v_ffn_w_up, v_ffn_conv_w, v_ffn_conv_b, v_ffn_w_down, v_final_norm_g):
    a = dict(zip(_INPUTS, (x, c, positions, ada_w, ada_b, norm1_g, norm2_g, w_in, ssd_conv_w, ssd_conv_b, ssd_dt_bias, ssd_a_log, ssd_d, ssd_norm_g, mla_q_norm_g, mla_w_uq, mla_kv_norm_g, mla_w_ukv, swa_sinks, w_out, ffn_w_up, ffn_conv_w, ffn_conv_b, ffn_w_down, final_norm_g, loss_target, m_ada_w, m_ada_b, m_norm1_g, m_norm2_g, m_w_in, m_ssd_conv_w, m_ssd_conv_b, m_ssd_dt_bias, m_ssd_a_log, m_ssd_d, m_ssd_norm_g, m_mla_q_norm_g, m_mla_w_uq, m_mla_kv_norm_g, m_mla_w_ukv, m_swa_sinks, m_w_out, m_ffn_w_up, m_ffn_conv_w, m_ffn_conv_b, m_ffn_w_down, m_final_norm_g, v_ada_w, v_ada_b, v_norm1_g, v_norm2_g, v_w_in, v_ssd_conv_w, v_ssd_conv_b, v_ssd_dt_bias, v_ssd_a_log, v_ssd_d, v_ssd_norm_g, v_mla_q_norm_g, v_mla_w_uq, v_mla_kv_norm_g, v_mla_w_ukv, v_swa_sinks, v_w_out, v_ffn_w_up, v_ffn_conv_w, v_ffn_conv_b, v_ffn_w_down, v_final_norm_g)))
    axes = ("x", "y", "c")
    me = 4 * lax.axis_index("x") + 2 * lax.axis_index("y") + lax.axis_index("c")
    ncol = ada_w.shape[-1]

    kform = lambda n, t: jnp.swapaxes(t, -1, -2) if n in _TRANSPOSED else t
    mxu_names = ('w_in', 'mla_w_uq', 'mla_w_ukv', 'w_out', 'ffn_w_up', 'ffn_w_down')
    gather_groups = (("early", _SHARDED_NAMES[:4]), ("mid", _SHARDED_NAMES[4:5]), ("late", _SHARDED_NAMES[5:]))

    def own_of(src, names, l):
        return [kform(n, src[n][l]).astype(_MXU) if n in mxu_names else src[n][l] for n in names]

    first_gather = _xchg_start(own_of(a, gather_groups[0][1], 0), scatter=False, name="gather_start_early0")

    c_all = _exchange([c + first_gather["token"][0, 0]], scatter=False, name="gather_c")[0]
    c_act = _silu_call(c_all.reshape(NDEV, D), name="c_act")
    mod_part = jnp.stack([_mm(c_act, ada_w[l], name=f"mod{l}") for l in range(DEPTH)])
    mod_all = _exchange([mod_part], scatter=False, name="gather_mod")[0]
    mod_mine = lax.dynamic_index_in_dim(mod_all, me, axis=2, keepdims=False)
    mods = (jnp.moveaxis(mod_mine, 0, 1).reshape(DEPTH, 6 * D) + ada_b).reshape(DEPTH, 6, D)
    tabs = _rope_tables(positions)

    shard_of = {n: (key, 1 if n in _TRANSPOSED else ax) for n, key, ax in _SHARDED}
    mods, raw = lax.optimization_barrier((mods, {n: a[n] for n in _SHARDED_NAMES}))
    gathers, prev = [], first_gather["token"]
    for l in range(DEPTH):
        gathers.append({})
        for grp, names in gather_groups:
            if (l, grp) == (0, "early"):
                gathers[l][grp] = first_gather
                continue
            srcs, _ = lax.optimization_barrier((own_of(raw, names, l), prev))
            gathers[l][grp] = _xchg_start(srcs, scatter=False, name=f"gather_start_{grp}{l}")
            prev = gathers[l][grp]["token"]

    def place_own(landed, mine, in_place=()):
        slot = lambda t: lax.broadcasted_iota(jnp.int32, (NDEV,) + (1,) * (t.ndim - 1), 0)
        return [lax.dynamic_update_index_in_dim(t, o, me, 0) if k in in_place else jnp.where(slot(t) == me, o[None], t)
                for k, (t, o) in enumerate(zip(landed, mine))]

    as_is = ('ssd_conv_w', 'ffn_w_up', 'ffn_conv_w', 'ffn_w_down')

    def gathered(l, grp, after):
        names = dict(gather_groups)[grp]
        mine, landed = _xchg_wait(gathers[l][grp], after, name=f"gather_wait_{grp}{l}")
        full = place_own(landed, mine, [k for k, n in enumerate(names) if n in as_is])
        return {n: _unshard(g, shard_of[n][1] - 1) for n, g in zip(names, full)}

    def get_params(l, after):
        full = gathered(l, "early", mods if l == 0 else after)
        vec = lambda t: t[l].reshape(1, -1)

        def mid(after2):
            return dict(w_out=_w_out_to_padded(gathered(l, "mid", after2)['w_out']))

        def late(after2):
            rest = gathered(l, "late", after2)
            return dict(w_up=rest['ffn_w_up'], w_down=rest['ffn_w_down'], fcw=rest['ffn_conv_w'])

        return dict(
            w_in=_w_in_to_padded(full['w_in'], axis=0), w_uq=_pad_heads(full['mla_w_uq'], MLA_HEADS, MLA_NOPE + MLA_ROPE),
            w_ukv=_w_ukv_to_padded(full['mla_w_ukv']), ssd_cw=full['ssd_conv_w'], mid=mid, late=late,
            ssd_cb=vec(ssd_conv_b), dtb=vec(_pad_lane(ssd_dt_bias)), alog=vec(_pad_lane(ssd_a_log)),
            dsk=vec(_pad_lane(ssd_d)), ssd_ng=vec(ssd_norm_g), gq=vec(mla_q_norm_g), gkv=vec(mla_kv_norm_g),
            sinks=vec(_pad_lane(swa_sinks)), fcb=vec(ffn_conv_b), n1g=vec(norm1_g), n2g=vec(norm2_g))

    unpad = dict(w_in=functools.partial(_w_in_from_padded, axis=0), w_out=_w_out_from_padded, w_ukv=_w_ukv_from_padded,
                 w_uq=lambda g: _unpad_heads(g, MLA_HEADS, MLA_NOPE + MLA_ROPE))
    scatter_groups = (("ffn", _SHARDED_NAMES[5:]), ("out", _SHARDED_NAMES[4:5]), ("mixer", _SHARDED_NAMES[:4]))
    grads = [None] * DEPTH
    scatters = [dict() for _ in range(DEPTH)]

    def on_part(l, grp, G):
        parts = [_shard_major(unpad.get(shard_of[n][0], lambda g: g)(G[shard_of[n][0]]), shard_of[n][1] - 1).astype(_ACT)
                 for n in dict(scatter_groups)[grp]]
        scatters[l][grp] = _xchg_start(parts, scatter=True, name=f"scatter_start_{grp}{l}")
        return scatters[l][grp]["token"][0, 0]

    def on_grads(l, G):
        grads[l] = G

    mods = mods + sum(g[grp]["token"][0, 0] for g in gathers for grp, _ in gather_groups)
    loss, dx, dfinal = _local_step(x[0], loss_target[0], mods, get_params, tabs, final_norm_g.reshape(1, D),
                                   on_grads, on_part)
    loss = lax.psum(loss, axes)

    stack = lambda key: jnp.stack([grads[l][key] for l in range(DEPTH)])
    small_names = [n for n, _ in _SMALL] + ['final_norm_g']
    small_g = [stack(key).reshape(DEPTH, -1)[:, :a[name].shape[1]] for name, key in _SMALL] + [dfinal]
    small_gather = _xchg_start(small_g, scatter=False, name="gather_small_start")

    out_g, out_d, out_m, out_v = {}, {}, {}, {}
    chain = {name: None for name in _SHARDED_NAMES}
    me_arr = jnp.reshape(me, (1,)).astype(jnp.int32)
    after = small_gather["token"]
    for l in reversed(range(DEPTH)):
        for grp, names in scatter_groups:
            mine, landed = _xchg_wait(scatters[l][grp], after, name=f"scatter_wait_{grp}{l}")
            for name, own, got in zip(names, mine, landed):
                chain[name] = _adamw_layer(l, kform(name, a[name]), kform(name, a['m_' + name]),
                                           kform(name, a['v_' + name]), got, own, me_arr, chain[name],
                                           name=f"adamw_{name}{l}")
    for name in _SHARDED_NAMES:
        out_g[name], out_d[name], out_m[name], out_v[name] = [kform(name, t) for t in chain[name]]
    small_mine, small_landed = _xchg_wait(small_gather, chain[_SHARDED_NAMES[0]][0], name="gather_small_wait")
    row = lambda t: t.reshape(1, -1) if t.ndim == 1 else t
    res = _adamw_many([row(a[n]) for n in small_names], [row(a['m_' + n]) for n in small_names],
                      [row(a['v_' + n]) for n in small_names], small_landed, small_mine, me_arr, name="adamw_small")
    for i, n in enumerate(small_names):
        out_g[n], out_d[n], out_m[n], out_v[n] = [t.reshape(a[n].shape) for t in res[4 * i:4 * i + 4]]

    dmod_all = place_own(small_landed[:1], small_mine[:1])[0]
    dmod_mine = lax.dynamic_slice_in_dim(dmod_all, me * ncol, ncol, axis=2)
    g_ada = jnp.stack([_mm(c_act, dmod_mine[:, l], ta=True, name=f"dw_ada{l}") for l in range(DEPTH)])
    shp = ada_w.shape
    res = _adamw(*[t.reshape(-1, shp[-1]) for t in (ada_w, m_ada_w, v_ada_w)], g_ada.reshape(1, -1, shp[-1]),
                 name="adamw_ada_w")
    out_g['ada_w'], out_d['ada_w'], out_m['ada_w'], out_v['ada_w'] = [t.reshape(shp) for t in res]

    outs = [loss, dx[None]]
    for dct in (out_g, out_d, out_m, out_v):
        outs += [dct[n] for n in _WEIGHTS]
    return tuple(outs)
```

```python
import functools
import math

import jax
import jax.numpy as jnp
from jax import lax
from jax.experimental import pallas as pl
from jax.experimental.pallas import tpu as pltpu

F32 = jnp.float32
_MXU = jnp.bfloat16
_ACT = jnp.bfloat16
_HI = lax.Precision.HIGHEST
EPS = 1e-6
NDEV = 8
DEPTH = 4
D = 1024
LANE = 128
SUB = 8
VMEM_LIMIT = 56 * 1024 * 1024

SSD_INNER, SSD_STATE, SSD_HEADS, SSD_GROUPS, SSD_CHUNK, SSD_CONV = 512, 128, 8, 2, 128, 4
SSD_XBC = SSD_INNER + 2 * SSD_GROUPS * SSD_STATE
MLA_HEADS, MLA_NOPE, MLA_ROPE, MLA_V, MLA_QR, MLA_KVR = 4, 64, 32, 64, 256, 128
SWA_HEADS, SWA_KV, SWA_HD, WINDOW = 4, 2, 64, 128
D_FF, FFN_CONV = 2816, 3
D_IN = 2472
ROPE_THETA = 10000.0
C_XBC, C_Z, C_CQ, C_CKV, C_MISC, C_SQ, C_SK, C_SV, D_INP = 0, 1024, 1536, 1792, 1920, 2048, 2560, 2816, 3072
ROPE_LANE = 64
D_MIXP = 1536

ADAM_LR, ADAM_B1, ADAM_B2, ADAM_EPS, ADAM_WD, ADAM_STEP = 0.001, 0.9, 0.999, 1e-08, 0.01, 10

TS_ROW = 1024
TS_FFN = 256
TQ_ATT = 1024
TS_SWA = 512


def _tile(n, cap, q=LANE):
    best = None
    for t in range(q, min(n, cap) + 1, q):
        if n % t == 0:
            best = t
    return n if best is None else best


def _cp(ngrid):
    return pltpu.CompilerParams(dimension_semantics=("arbitrary",) * ngrid, vmem_limit_bytes=VMEM_LIMIT)


def _dot(a, b):
    return jnp.dot(a.astype(_MXU), b.astype(_MXU), preferred_element_type=F32)


def _dot_nt(a, b):
    return lax.dot_general(a.astype(_MXU), b.astype(_MXU), (((1,), (1,)), ((), ())), preferred_element_type=F32)


def _dot_tn(a, b):
    return jnp.dot(a.T.astype(_MXU), b.astype(_MXU), preferred_element_type=F32)


def _sigmoid(x):
    return 1.0 / (1.0 + jnp.exp(-x))


def _sigmoid_t(x):
    return 0.5 * jnp.tanh(0.5 * x) + 0.5


def _silu(x):
    return x * _sigmoid_t(x)


def _silu_grad(x):
    s = _sigmoid_t(x)
    return x * s, s * (1.0 + x * (1.0 - s))


def _dsilu(x):
    return _silu_grad(x)[1]


def _softplus(x):
    u = jnp.exp(-jnp.abs(x))
    w = 1.0 + u
    log1p = jnp.where(w == 1.0, u, jnp.log(w) * u / jnp.where(w == 1.0, 1.0, w - 1.0))
    return jnp.maximum(x, 0.0) + log1p


def _colsum(x):
    return jnp.sum(x, axis=0, keepdims=True)


def _rowsum(x):
    return jnp.sum(x, axis=1, keepdims=True)


def _shift_down(t, halo, j):
    if j == 0:
        return t
    n = t.shape[0]
    rolled = pltpu.roll(t, j, 0)
    row = lax.broadcasted_iota(jnp.int32, (SUB, t.shape[1]), 0)
    first = jnp.where(row < j, pltpu.roll(halo, j, 0), rolled[:SUB])
    return jnp.concatenate([first, rolled[SUB:]], axis=0) if n > SUB else first


def _shift_up(t, halo, j):
    if j == 0:
        return t
    n = t.shape[0]
    rolled = pltpu.roll(t, n - j, 0)
    row = lax.broadcasted_iota(jnp.int32, (SUB, t.shape[1]), 0)
    last = jnp.where(row >= SUB - j, pltpu.roll(halo, SUB - j, 0), rolled[n - SUB:])
    return jnp.concatenate([rolled[:n - SUB], last], axis=0) if n > SUB else last


def _mm(a, b, *, ta=False, tb=False, out_dtype=F32, name):
    if ta:
        K, M = a.shape
    else:
        M, K = a.shape
    if tb:
        N, K2 = b.shape
    else:
        K2, N = b.shape
    assert K == K2, (a.shape, b.shape, ta, tb)
    tk = _tile(K, 1536)
    nk = K // tk
    tm, tn = _tile(M, 2048 if nk == 1 else 1536), _tile(N, 1536 if nk == 1 else 1408)
    dn = (((0 if ta else 1,), (1 if tb else 0,)), ((), ()))

    def body(a_ref, b_ref, o_ref, *acc):
        part = lax.dot_general(a_ref[...].astype(_MXU), b_ref[...].astype(_MXU), dn, preferred_element_type=F32)
        if nk == 1:
            o_ref[...] = part.astype(out_dtype)
            return
        acc_ref, = acc
        k = pl.program_id(2)

        @pl.when(k == 0)
        def _():
            acc_ref[...] = part

        @pl.when(k > 0)
        def _():
            acc_ref[...] += part

        @pl.when(k == nk - 1)
        def _():
            o_ref[...] = acc_ref[...].astype(out_dtype)

    a_spec = pl.BlockSpec((tk, tm), lambda i, j, k: (k, i)) if ta else pl.BlockSpec((tm, tk), lambda i, j, k: (i, k))
    b_spec = pl.BlockSpec((tn, tk), lambda i, j, k: (j, k)) if tb else pl.BlockSpec((tk, tn), lambda i, j, k: (k, j))
    return pl.pallas_call(
        body, grid=(M // tm, N // tn, nk), in_specs=[a_spec, b_spec],
        out_specs=pl.BlockSpec((tm, tn), lambda i, j, k: (i, j)),
        out_shape=jax.ShapeDtypeStruct((M, N), out_dtype),
        scratch_shapes=[pltpu.VMEM((tm, tn), F32)] * (nk > 1), compiler_params=_cp(3), name=name)(a, b)


def _row(ts, w, col=0):
    return pl.BlockSpec((ts, w), lambda i: (i, col))


def _vec(w, r=1):
    return pl.BlockSpec((r, w), lambda i: (0, 0))


def _silu_call(x, name):
    def body(x_ref, o_ref):
        o_ref[...] = _silu(x_ref[...])
    return pl.pallas_call(body, out_shape=jax.ShapeDtypeStruct(x.shape, F32), name=name)(x)


def _norm_fwd(x, g, sc, sh, *, f=None, gate=None, name):
    S, dm = x.shape
    ts = _tile(S, TS_ROW, SUB)
    res = f is not None

    def body(*refs):
        if res:
            x_ref, f_ref, gate_ref, g_ref, sc_ref, sh_ref, xo_ref, h_ref = refs
            xv = x_ref[...] + gate_ref[...] * f_ref[...]
            xo_ref[...] = xv
        else:
            x_ref, g_ref, sc_ref, sh_ref, h_ref = refs
            xv = x_ref[...]
        rstd = lax.rsqrt(jnp.mean(xv * xv, axis=-1, keepdims=True) + EPS)
        h_ref[...] = ((xv * rstd) * g_ref[...] * (1.0 + sc_ref[...]) + sh_ref[...]).astype(_ACT)

    ins = [x] + ([f, gate] if res else []) + [g, sc, sh]
    in_specs = [_row(ts, dm)] + ([_row(ts, dm), _vec(dm)] if res else []) + [_vec(dm)] * 3
    h_shape = jax.ShapeDtypeStruct((S, dm), _ACT)
    if res:
        out_shape, out_specs = (jax.ShapeDtypeStruct((S, dm), F32), h_shape), (_row(ts, dm), _row(ts, dm))
    else:
        out_shape, out_specs = h_shape, _row(ts, dm)
    return pl.pallas_call(body, grid=(S // ts,), in_specs=in_specs, out_specs=out_specs, out_shape=out_shape,
                          compiler_params=_cp(1), name=name)(*ins)


def _norm_bwd(dh, x, dres, g, sc, *, branch=None, name):
    S, dm = x.shape
    ts = _tile(S, TS_ROW // 2, SUB)
    nb = 0 if branch is None else 2

    def body(*refs):
        dh_ref, x_ref, dres_ref, g_ref, sc_ref = refs[:5]
        dx_ref, dg_ref, dsc_ref, dsh_ref = refs[5 + nb:9 + nb]
        i = pl.program_id(0)
        xv = x_ref[...]
        dhv = dh_ref[...]
        rstd = lax.rsqrt(jnp.mean(xv * xv, axis=-1, keepdims=True) + EPS)
        xhat = xv * rstd
        hn = xhat * g_ref[...]
        dhn = dhv * (1.0 + sc_ref[...])
        dxh = dhn * g_ref[...]
        dx = dres_ref[...] + rstd * (dxh - xhat * jnp.mean(dxh * xhat, axis=-1, keepdims=True))
        dx_ref[...] = dx

        @pl.when(i == 0)
        def _():
            for r in refs[6 + nb:]:
                if r.shape[0] == 1:
                    r[...] = jnp.zeros_like(r)

        dg_ref[...] += _colsum(dhn * xhat)
        dsc_ref[...] += _colsum(dhv * hn)
        dsh_ref[...] += _colsum(dhv)
        if branch is not None:
            f_ref, gate_ref = refs[5:7]
            df_ref, dgate_ref = refs[9 + nb:]
            df_ref[...] = (gate_ref[...] * dx).astype(_ACT)
            dgate_ref[...] += _colsum(dx * f_ref[...])

    vshape = jax.ShapeDtypeStruct((1, dm), F32)
    extra_in = [] if branch is None else list(branch)
    return pl.pallas_call(
        body, grid=(S // ts,),
        in_specs=[_row(ts, dm)] * 3 + [_vec(dm)] * 2 + ([_row(ts, dm), _vec(dm)] if nb else []),
        out_specs=(_row(ts, dm), _vec(dm), _vec(dm), _vec(dm)) + ((_row(ts, dm), _vec(dm)) if nb else ()),
        out_shape=(jax.ShapeDtypeStruct((S, dm), F32), vshape, vshape, vshape)
        + ((jax.ShapeDtypeStruct((S, dm), _ACT), vshape) if nb else ()),
        compiler_params=_cp(1), name=name)(dh, x, dres, g, sc, *extra_in)


def _final_loss(x, f, gate, g, tgt, *, name):
    S, dm = x.shape
    ts = _tile(S, TS_ROW, SUB)

    def body(x_ref, f_ref, gate_ref, g_ref, t_ref, loss_ref, dx_ref, dg_ref, df_ref, dgate_ref):
        i = pl.program_id(0)
        fv = f_ref[...]
        xv = x_ref[...] + gate_ref[...] * fv
        rstd = lax.rsqrt(jnp.mean(xv * xv, axis=-1, keepdims=True) + EPS)
        xhat = xv * rstd
        err = xhat * g_ref[...] - t_ref[...]
        dy = err * (1.0 / dm)
        dxh = dy * g_ref[...]
        dx = rstd * (dxh - xhat * jnp.mean(dxh * xhat, axis=-1, keepdims=True))
        dx_ref[...] = dx
        df_ref[...] = (gate_ref[...] * dx).astype(_ACT)

        @pl.when(i == 0)
        def _():
            loss_ref[...] = jnp.zeros_like(loss_ref)
            dg_ref[...] = jnp.zeros_like(dg_ref)
            dgate_ref[...] = jnp.zeros_like(dgate_ref)

        loss_ref[...] += jnp.full((1, LANE), 0.5 * jnp.sum(jnp.mean(err * err, axis=-1, keepdims=True)), F32)
        dg_ref[...] += _colsum(dy * xhat)
        dgate_ref[...] += _colsum(dx * fv)

    return pl.pallas_call(
        body, grid=(S // ts,), in_specs=[_row(ts, dm), _row(ts, dm), _vec(dm), _vec(dm), _row(ts, dm)],
        out_specs=(_vec(LANE), _row(ts, dm), _vec(dm), _row(ts, dm), _vec(dm)),
        out_shape=(jax.ShapeDtypeStruct((1, LANE), F32), jax.ShapeDtypeStruct((S, dm), F32),
                   jax.ShapeDtypeStruct((1, dm), F32), jax.ShapeDtypeStruct((S, dm), _ACT),
                   jax.ShapeDtypeStruct((1, dm), F32)),
        compiler_params=_cp(1), name=name)(x, f, gate, g, tgt)


def _ffn_conv(t, halo, cw_ref, cb_ref):
    t1, t2 = _shift_down(t, halo, 1), _shift_down(t, halo, 2)
    return ((cb_ref[...] + t2 * cw_ref[0:1, :]) + t1 * cw_ref[1:2, :]) + t * cw_ref[2:3, :], t1, t2


def _prev_halo_spec(ts, w, col=0):
    return pl.BlockSpec((SUB, w), lambda i: (jnp.maximum(i * (ts // SUB) - 1, 0), col))


def _ffn_act_fwd(up, cw, cb, *, name):
    S, w2 = up.shape
    ff = w2 // 2
    ts = _tile(S, TS_FFN, SUB)

    def body(up_ref, halo_ref, cw_ref, cb_ref, act_ref):
        i = pl.program_id(0)
        t = up_ref[...]
        halo = jnp.where(i > 0, halo_ref[...], 0.0)
        u, _, _ = _ffn_conv(t, halo, cw_ref, cb_ref)
        act_ref[...] = (_silu(u[:, :ff]) * u[:, ff:]).astype(_ACT)

    return pl.pallas_call(
        body, grid=(S // ts,), in_specs=[_row(ts, w2), _prev_halo_spec(ts, w2), _vec(w2, FFN_CONV), _vec(w2)],
        out_specs=_row(ts, ff), out_shape=jax.ShapeDtypeStruct((S, ff), _ACT),
        compiler_params=_cp(1), name=name)(up, up, cw, cb)


def _ffn_bwd(up, dact, cw, cb, *, name):
    S, w2 = up.shape
    ff = w2 // 2
    ts = _tile(S, TS_FFN, SUB)
    n = S // ts

    def body(up_ref, halo_ref, dact_ref, cw_ref, cb_ref, dup_ref, dcw_ref, dcb_ref, carry_ref):
        i = pl.program_id(0)
        t_idx = n - 1 - i

        @pl.when(i == 0)
        def _():
            carry_ref[...] = jnp.zeros_like(carry_ref)
            dcw_ref[...] = jnp.zeros_like(dcw_ref)
            dcb_ref[...] = jnp.zeros_like(dcb_ref)

        t = up_ref[...]
        halo = jnp.where(t_idx > 0, halo_ref[...], 0.0)
        u, t1, t2 = _ffn_conv(t, halo, cw_ref, cb_ref)
        a, b = u[:, :ff], u[:, ff:]
        da = dact_ref[...]
        sa, dsa = _silu_grad(a)
        dv = jnp.concatenate([da * b * dsa, da * sa], axis=1)
        nxt = carry_ref[...]
        dup = (dv * cw_ref[2:3, :] + _shift_up(dv, nxt, 1) * cw_ref[1:2, :]) + _shift_up(dv, nxt, 2) * cw_ref[0:1, :]
        dup_ref[...] = dup.astype(_ACT)
        dcb_ref[...] += _colsum(dv)
        dcw_ref[2:3, :] += _colsum(dv * t)
        dcw_ref[1:2, :] += _colsum(dv * t1)
        dcw_ref[0:1, :] += _colsum(dv * t2)
        carry_ref[...] = dv[:SUB]

    rev = lambda w: pl.BlockSpec((ts, w), lambda i: (n - 1 - i, 0))
    halo_spec = pl.BlockSpec((SUB, w2), lambda i: (jnp.maximum((n - 1 - i) * (ts // SUB) - 1, 0), 0))
    return pl.pallas_call(
        body, grid=(n,), in_specs=[rev(w2), halo_spec, rev(ff), _vec(w2, FFN_CONV), _vec(w2)],
        out_specs=(rev(w2), _vec(w2, FFN_CONV), _vec(w2)),
        out_shape=(jax.ShapeDtypeStruct((S, w2), _ACT), jax.ShapeDtypeStruct((FFN_CONV, w2), F32),
                   jax.ShapeDtypeStruct((1, w2), F32)),
        scratch_shapes=[pltpu.VMEM((SUB, w2), F32)], compiler_params=_cp(1), name=name)(up, up, dact, cw, cb)


def _ssd_core(pre, halo, misc, cw_ref, cb_ref, dtb, alog):
    q = pre.shape[0]
    conv = cb_ref[...]
    for k in range(SSD_CONV):
        conv = conv + _shift_down(pre, halo, SSD_CONV - 1 - k) * cw_ref[k:k + 1, :]
    xbc = _silu(conv)
    raw = misc + dtb
    dt = _softplus(raw)
    a = -jnp.exp(alog)
    r = lax.broadcasted_iota(jnp.int32, (q, q), 0)
    c = lax.broadcasted_iota(jnp.int32, (q, q), 1)
    tri = r >= c
    acum = jnp.dot(tri.astype(F32), dt * a, precision=_HI, preferred_element_type=F32)
    return conv, xbc, raw, dt, a, acum, acum.T, tri


def _sel(v, j, lo):
    return jnp.where(lo, v[:, 2 * j:2 * j + 1], v[:, 2 * j + 1:2 * j + 2])


def _ssd_pair_fwd(xbc, dt, acum, acum_t, tri, dsk, g_mat, b_mat, c_mat, h_pair, j, lo, lo1, sub_lo):
    q = xbc.shape[0]
    x = xbc[:, LANE * j:LANE * (j + 1)]
    dtp = _sel(dt, j, lo)
    ap = _sel(acum, j, lo)
    xd = x * dtp
    ls, ms = [], []
    for h in (2 * j, 2 * j + 1):
        seg = acum[:, h:h + 1] - acum_t[h:h + 1, :]
        l_mat = jnp.exp(jnp.where(tri, seg, -jnp.inf))
        ls.append(l_mat)
        ms.append(g_mat * l_mat)
    yd = jnp.where(lo, _dot(ms[0], xd), _dot(ms[1], xd))
    ea = jnp.exp(ap)
    yo = _dot_nt(c_mat, h_pair) * ea
    dp = _sel(dsk, j, lo1)
    alast = acum[q - 1:q, :]
    e = jnp.exp(_sel(alast, j, lo1) - ap)
    cd = jnp.where(sub_lo, jnp.exp(alast[:, 2 * j:2 * j + 1]), jnp.exp(alast[:, 2 * j + 1:2 * j + 2]))
    return dict(x=x, dtp=dtp, ap=ap, xd=xd, ls=ls, ms=ms, ea=ea, yo=yo, dp=dp, e=e, cd=cd, y=yd + yo + x * dp)


def _gnorm(yg):
    half = SSD_INNER // SSD_GROUPS
    rstds, yns = [], []
    for g in range(SSD_GROUPS):
        part = yg[:, half * g:half * (g + 1)]
        rstd = lax.rsqrt(jnp.mean(part * part, axis=-1, keepdims=True) + EPS)
        rstds.append(rstd)
        yns.append(part * rstd)
    return rstds, yns


def _ssd_specs(nc, rev):
    q = SSD_CHUNK
    cidx = (lambda i: nc - 1 - i) if rev else (lambda i: i)
    return [
        pl.BlockSpec((q, SSD_XBC), lambda i: (cidx(i), C_XBC // SSD_XBC)),
        pl.BlockSpec((SUB, SSD_XBC), lambda i: (jnp.maximum(cidx(i) * (q // SUB) - 1, 0), C_XBC // SSD_XBC)),
        pl.BlockSpec((q, SSD_INNER), lambda i: (cidx(i), C_Z // SSD_INNER)),
        pl.BlockSpec((q, LANE), lambda i: (cidx(i), C_MISC // LANE)),
    ]


def _ssd_param_specs():
    return [_vec(SSD_XBC, SSD_CONV), _vec(SSD_XBC), _vec(LANE), _vec(LANE), _vec(LANE), _vec(SSD_INNER)]


def _ssd_fwd(proj, cw, cb, dtb, alog, dsk, ng, *, name):
    S = proj.shape[0]
    q = SSD_CHUNK
    nc = S // q
    npair = SSD_HEADS // 2

    def body(xbc_ref, halo_ref, z_ref, misc_ref, cw_ref, cb_ref, dtb_ref, alog_ref, dsk_ref, ng_ref,
             y_ref, hin_ref, h_ref):
        c = pl.program_id(0)

        @pl.when(c == 0)
        def _():
            h_ref[...] = jnp.zeros_like(h_ref)

        pre = xbc_ref[...]
        halo = jnp.where(c > 0, halo_ref[...], 0.0)
        conv, xbc, raw, dt, a, acum, acum_t, tri = _ssd_core(pre, halo, misc_ref[...], cw_ref, cb_ref,
                                                             dtb_ref[...], alog_ref[...])
        lo = lax.broadcasted_iota(jnp.int32, (q, LANE), 1) < LANE // 2
        lo1 = lo[:1]
        sub_lo = lax.broadcasted_iota(jnp.int32, (LANE, LANE), 0) < LANE // 2
        ys = []
        for g in range(SSD_GROUPS):
            b_mat = xbc[:, SSD_INNER + SSD_STATE * g:SSD_INNER + SSD_STATE * (g + 1)]
            c_mat = xbc[:, SSD_INNER + SSD_STATE * (SSD_GROUPS + g):SSD_INNER + SSD_STATE * (SSD_GROUPS + g + 1)]
            g_mat = _dot_nt(c_mat, b_mat)
            for jj in range(npair // SSD_GROUPS):
                j = g * (npair // SSD_GROUPS) + jj
                hj = h_ref[j]
                p = _ssd_pair_fwd(xbc, dt, acum, acum_t, tri, dsk_ref[...], g_mat, b_mat, c_mat, hj, j, lo, lo1, sub_lo)
                ys.append(p["y"])
                hin_ref[0, j] = hj
                h_ref[j] = p["cd"] * hj + _dot_tn(p["xd"] * p["e"], b_mat)
        yg = jnp.concatenate(ys, axis=1) * _silu(z_ref[...])
        _, yns = _gnorm(yg)
        y_ref[...] = jnp.concatenate(yns, axis=1) * ng_ref[...]

    return pl.pallas_call(
        body, grid=(nc,), in_specs=_ssd_specs(nc, False) + _ssd_param_specs(),
        out_specs=(pl.BlockSpec((q, SSD_INNER), lambda i: (i, 0)),
                   pl.BlockSpec((1, npair, LANE, LANE), lambda i: (i, 0, 0, 0))),
        out_shape=(jax.ShapeDtypeStruct((S, SSD_INNER), F32), jax.ShapeDtypeStruct((nc, npair, LANE, LANE), F32)),
        scratch_shapes=[pltpu.VMEM((npair, LANE, LANE), F32)], compiler_params=_cp(1), name=name,
    )(proj, proj, proj, proj, cw, cb, dtb, alog, dsk, ng)


def _ssd_bwd(proj, dycat, hin, cw, cb, dtb, alog, dsk, ng, *, name):
    S = proj.shape[0]
    q = SSD_CHUNK
    nc = S // q
    npair = SSD_HEADS // 2
    ppg = npair // SSD_GROUPS

    def body(xbc_ref, halo_ref, z_ref, misc_ref, dy_ref, hin_ref, cw_ref, cb_ref, dtb_ref, alog_ref, dsk_ref, ng_ref,
             dpre_ref, dz_ref, dmisc_ref, dcw_ref, dcb_ref, ddtb_ref, dalog_ref, ddsk_ref, dng_ref,
             dh_ref, carry_ref):
        i = pl.program_id(0)
        c = nc - 1 - i

        @pl.when(i == 0)
        def _():
            dh_ref[...] = jnp.zeros_like(dh_ref)
            carry_ref[...] = jnp.zeros_like(carry_ref)
            for r in (dcw_ref, dcb_ref, ddtb_ref, dalog_ref, ddsk_ref, dng_ref):
                r[...] = jnp.zeros_like(r)

        pre = xbc_ref[...]
        halo = jnp.where(c > 0, halo_ref[...], 0.0)
        conv, xbc, raw, dt, a, acum, acum_t, tri = _ssd_core(pre, halo, misc_ref[...], cw_ref, cb_ref,
                                                             dtb_ref[...], alog_ref[...])
        lane = lax.broadcasted_iota(jnp.int32, (q, LANE), 1)
        lane1 = lane[:1]
        rowi = lax.broadcasted_iota(jnp.int32, (q, LANE), 0)
        lastrow = rowi == q - 1
        lo = lane < LANE // 2
        lo1 = lo[:1]
        sub_lo = lax.broadcasted_iota(jnp.int32, (LANE, LANE), 0) < LANE // 2
        dsk = dsk_ref[...]
        alast = acum[q - 1:q, :]

        def halves(t):
            return _rowsum(jnp.where(lo, t, 0.0)), _rowsum(jnp.where(lo, 0.0, t))

        def put(ha, va, vb):
            ln = lane if va.shape[0] == q else lane1
            return jnp.where(ln == ha, va, 0.0) + jnp.where(ln == ha + 1, vb, 0.0)

        mats, pairs = [], []
        for g in range(SSD_GROUPS):
            b_mat = xbc[:, SSD_INNER + SSD_STATE * g:SSD_INNER + SSD_STATE * (g + 1)]
            c_mat = xbc[:, SSD_INNER + SSD_STATE * (SSD_GROUPS + g):SSD_INNER + SSD_STATE * (SSD_GROUPS + g + 1)]
            g_mat = _dot_nt(c_mat, b_mat)
            mats.append((b_mat, c_mat, g_mat))
            for jj in range(ppg):
                j = g * ppg + jj
                pairs.append(_ssd_pair_fwd(xbc, dt, acum, acum_t, tri, dsk, g_mat, b_mat, c_mat, hin_ref[0, j],
                                           j, lo, lo1, sub_lo))
        z = z_ref[...]
        sz, dsz = _silu_grad(z)
        y = jnp.concatenate([p["y"] for p in pairs], axis=1)
        rstds, yns = _gnorm(y * sz)
        dout = dy_ref[...]
        dng_ref[...] += _colsum(dout * jnp.concatenate(yns, axis=1))
        dyn = dout * ng_ref[...]
        half = SSD_INNER // SSD_GROUPS
        dygs = []
        for g in range(SSD_GROUPS):
            dyn_g = dyn[:, half * g:half * (g + 1)]
            dygs.append(rstds[g] * (dyn_g - yns[g] * jnp.mean(dyn_g * yns[g], axis=-1, keepdims=True)))
        dyg = jnp.concatenate(dygs, axis=1)
        dyv = dyg * sz
        dz_ref[...] = (dyg * y * dsz).astype(_ACT)

        da_acc = jnp.zeros((q, LANE), F32)
        ddt = jnp.zeros((q, LANE), F32)
        dds = jnp.zeros((1, LANE), F32)
        dxs, dbs, dcs = [], [], []
        for g in range(SSD_GROUPS):
            b_mat, c_mat, g_mat = mats[g]
            dg_mat = jnp.zeros((q, q), F32)
            db = jnp.zeros((q, SSD_STATE), F32)
            dc = jnp.zeros((q, SSD_STATE), F32)
            for jj in range(ppg):
                j = g * ppg + jj
                ha = 2 * j
                p = pairs[j]
                hj = hin_ref[0, j]
                dyp = dyv[:, LANE * j:LANE * (j + 1)]
                dsum = _colsum(dyp * p["x"])
                dds = dds + put(ha, _rowsum(jnp.where(lo1, dsum, 0.0)), _rowsum(jnp.where(lo1, 0.0, dsum)))
                dx = dyp * p["dp"]
                dw = dyp * p["ea"]
                dc = dc + _dot(dw, hj)
                dh_yo = _dot_tn(dw, c_mat)
                ra, rb = halves(dyp * p["yo"])
                da_acc = da_acc + put(ha, ra, rb)
                dxd = jnp.zeros((q, LANE), F32)
                for idx in range(2):
                    dyh = jnp.where(lo, dyp, 0.0) if idx == 0 else jnp.where(lo, 0.0, dyp)
                    dm = _dot_nt(dyh, p["xd"])
                    dxd = dxd + _dot_tn(p["ms"][idx], dyh)
                    dg_mat = dg_mat + dm * p["ls"][idx]
                    t = dm * p["ms"][idx]
                    da_h = _rowsum(t) - _rowsum(t.T)
                    da_acc = da_acc + jnp.where(lane == ha + idx, da_h, 0.0)
                dhn = dh_ref[j]
                s = _rowsum(dhn * hj)
                sa = jnp.sum(jnp.where(sub_lo[:, :1], s, 0.0), keepdims=True)
                sb = jnp.sum(jnp.where(sub_lo[:, :1], 0.0, s), keepdims=True)
                cda, cdb = jnp.exp(alast[:, ha:ha + 1]), jnp.exp(alast[:, ha + 1:ha + 2])
                db = db + _dot(p["xd"] * p["e"], dhn)
                r = _dot_nt(b_mat, dhn)
                dxd = dxd + r * p["e"]
                qa, qb = halves(r * p["xd"] * p["e"])
                da_acc = da_acc - put(ha, qa, qb)
                tot_a = sa * cda + jnp.sum(qa, keepdims=True)
                tot_b = sb * cdb + jnp.sum(qb, keepdims=True)
                da_acc = da_acc + jnp.where(lastrow, put(ha, tot_a, tot_b), 0.0)
                dh_ref[j] = p["cd"] * dhn + dh_yo
                dx = dx + dxd * p["dtp"]
                ua, ub = halves(dxd * p["x"])
                ddt = ddt + put(ha, ua, ub)
                dxs.append(dx)
            dc = dc + _dot(dg_mat, b_mat)
            db = db + _dot_tn(dg_mat, c_mat)
            dbs.append(db)
            dcs.append(dc)
        r2 = lax.broadcasted_iota(jnp.int32, (q, q), 0)
        c2 = lax.broadcasted_iota(jnp.int32, (q, q), 1)
        dda = jnp.dot((c2 >= r2).astype(F32), da_acc, precision=_HI, preferred_element_type=F32)
        ddt = ddt + dda * a
        dalog_ref[...] += _colsum(dda * dt) * a
        ddsk_ref[...] += dds
        draw = jnp.where(lane < SSD_HEADS, ddt * _sigmoid(raw), 0.0)
        ddtb_ref[...] += _colsum(draw)
        dmisc_ref[...] = draw
        dconv = jnp.concatenate(dxs + dbs + dcs, axis=1) * _dsilu(conv)
        dcb_ref[...] += _colsum(dconv)
        nxt = carry_ref[...]
        dpre = jnp.zeros_like(dconv)
        for k in range(SSD_CONV):
            dcw_ref[k:k + 1, :] += _colsum(dconv * _shift_down(pre, halo, SSD_CONV - 1 - k))
            dpre = dpre + _shift_up(dconv, nxt, SSD_CONV - 1 - k) * cw_ref[k:k + 1, :]
        dpre_ref[...] = dpre.astype(_ACT)
        carry_ref[...] = dconv[:SUB]

    rev = lambda i: (nc - 1 - i, 0)
    vshape = lambda w, r=1: jax.ShapeDtypeStruct((r, w), F32)
    return pl.pallas_call(
        body, grid=(nc,),
        in_specs=_ssd_specs(nc, True) + [pl.BlockSpec((q, SSD_INNER), rev),
                                         pl.BlockSpec((1, npair, LANE, LANE), lambda i: (nc - 1 - i, 0, 0, 0))]
        + _ssd_param_specs(),
        out_specs=(pl.BlockSpec((q, SSD_XBC), rev), pl.BlockSpec((q, SSD_INNER), rev), pl.BlockSpec((q, LANE), rev),
                   _vec(SSD_XBC, SSD_CONV), _vec(SSD_XBC), _vec(LANE), _vec(LANE), _vec(LANE), _vec(SSD_INNER)),
        out_shape=(jax.ShapeDtypeStruct((S, SSD_XBC), _ACT), jax.ShapeDtypeStruct((S, SSD_INNER), _ACT),
                   jax.ShapeDtypeStruct((S, LANE), F32),
                   vshape(SSD_XBC, SSD_CONV), vshape(SSD_XBC), vshape(LANE), vshape(LANE), vshape(LANE),
                   vshape(SSD_INNER)),
        scratch_shapes=[pltpu.VMEM((npair, LANE, LANE), F32), pltpu.VMEM((SUB, SSD_XBC), F32)],
        compiler_params=_cp(1), name=name,
    )(proj, proj, proj, proj, dycat, hin, cw, cb, dtb, alog, dsk, ng)


def _rope(x, cosf, sina, sinb):
    return x * cosf + pltpu.roll(x, LANE - MLA_ROPE // 2, 1) * sina + pltpu.roll(x, MLA_ROPE // 2, 1) * sinb


def _rope_t(dy, cosf, sina, sinb):
    return dy * cosf + pltpu.roll(dy * sina, MLA_ROPE // 2, 1) + pltpu.roll(dy * sinb, LANE - MLA_ROPE // 2, 1)


def _rope_lanes(shape):
    lane = lax.broadcasted_iota(jnp.int32, shape, 1)
    return (lane >= ROPE_LANE) & (lane < ROPE_LANE + MLA_ROPE)


def _mla_prep_fwd(proj, cosf, sina, sinb, gq, gkv, wuq, wukv, *, name):
    S = proj.shape[0]
    ts = _tile(S, TS_ROW, SUB)
    hw = MLA_HEADS * LANE

    def body(cq_ref, ckv_ref, misc_ref, cos_ref, sa_ref, sb_ref, gq_ref, gkv_ref, wuq_ref, wukv_ref,
             q_ref, k_ref, v_ref, vt_ref):
        cosv, sav, sbv = cos_ref[...], sa_ref[...], sb_ref[...]
        cq = cq_ref[...]
        qn = cq * lax.rsqrt(jnp.mean(cq * cq, axis=-1, keepdims=True) + EPS) * gq_ref[...]
        qh = _dot(qn, wuq_ref[...])
        ckv = ckv_ref[...]
        kvn = ckv * lax.rsqrt(jnp.mean(ckv * ckv, axis=-1, keepdims=True) + EPS) * gkv_ref[...]
        kv = _dot(kvn, wukv_ref[...])
        kr = _rope(jnp.where(_rope_lanes((ts, LANE)), misc_ref[...], 0.0), cosv, sav, sbv)
        for h in range(MLA_HEADS):
            sl = slice(LANE * h, LANE * (h + 1))
            q_ref[:, sl] = (_rope(qh[:, sl], cosv, sav, sbv) * _Q_SCALE).astype(_ACT)
            k_ref[:, sl] = (kv[:, sl] + kr).astype(_ACT)
        v_ref[...] = kv[:, hw:].astype(_ACT)
        vt_ref[...] = kv[:, hw:].T.astype(_ACT)

    oshape = jax.ShapeDtypeStruct((S, hw), _ACT)
    return pl.pallas_call(
        body, grid=(S // ts,),
        in_specs=[_row(ts, MLA_QR, C_CQ // MLA_QR), _row(ts, MLA_KVR, C_CKV // MLA_KVR), _row(ts, LANE, C_MISC // LANE),
                  _row(ts, LANE), _row(ts, LANE), _row(ts, LANE), _vec(MLA_QR), _vec(MLA_KVR),
                  _vec(hw, MLA_QR), _vec(2 * hw, MLA_KVR)],
        out_specs=(_row(ts, hw),) * 3 + (pl.BlockSpec((hw, ts), lambda i: (0, i)),),
        out_shape=(oshape,) * 3 + (jax.ShapeDtypeStruct((hw, S), _ACT),), compiler_params=_cp(1), name=name,
    )(proj, proj, proj, cosf, sina, sinb, gq, gkv, wuq, wukv)


def _mla_prep_bwd(proj, dq, dk, dv, dmisc_ssd, cosf, sina, sinb, gq, gkv, wuq, wukv, *, name):
    S = proj.shape[0]
    ts = _tile(S, TS_ROW, SUB)
    hw = MLA_HEADS * LANE

    def body(cq_ref, ckv_ref, dq_ref, dk_ref, dv_ref, dms_ref, cos_ref, sa_ref, sb_ref, gq_ref, gkv_ref,
             wuq_ref, wukv_ref, dcq_ref, dckv_ref, dmisc_ref, dqh_ref, dkv_ref, qn_ref, kvn_ref, dgq_ref, dgkv_ref):
        i = pl.program_id(0)
        cosv, sav, sbv = cos_ref[...], sa_ref[...], sb_ref[...]

        @pl.when(i == 0)
        def _():
            dgq_ref[...] = jnp.zeros_like(dgq_ref)
            dgkv_ref[...] = jnp.zeros_like(dgkv_ref)

        dqh = jnp.concatenate([_rope_t(dq_ref[:, LANE * h:LANE * (h + 1)], cosv, sav, sbv)
                               for h in range(MLA_HEADS)], axis=1)
        dqh_ref[...] = dqh.astype(_ACT)
        dkv = jnp.concatenate([dk_ref[...], dv_ref[...]], axis=1)
        dkv_ref[...] = dkv.astype(_ACT)

        def norm_bwd(x, g, dn, dg_ref, n_ref):
            rstd = lax.rsqrt(jnp.mean(x * x, axis=-1, keepdims=True) + EPS)
            xhat = x * rstd
            n_ref[...] = (xhat * g).astype(_ACT)
            dg_ref[...] += _colsum(dn * xhat)
            dxh = dn * g
            return rstd * (dxh - xhat * jnp.mean(dxh * xhat, axis=-1, keepdims=True))

        dcq_ref[...] = norm_bwd(cq_ref[...], gq_ref[...], _dot_nt(dqh, wuq_ref[...]), dgq_ref, qn_ref).astype(_ACT)
        dckv_ref[...] = norm_bwd(ckv_ref[...], gkv_ref[...], _dot_nt(dkv, wukv_ref[...]), dgkv_ref, kvn_ref).astype(_ACT)
        dks = dk_ref[:, 0:LANE]
        for h in range(1, MLA_HEADS):
            dks = dks + dk_ref[:, LANE * h:LANE * (h + 1)]
        rl = _rope_lanes((ts, LANE))
        dkr = _rope_t(jnp.where(rl, dks, 0.0), cosv, sav, sbv)
        dmisc_ref[...] = (dms_ref[...] + jnp.where(rl, dkr, 0.0)).astype(_ACT)

    act = lambda w: jax.ShapeDtypeStruct((S, w), _ACT)
    return pl.pallas_call(
        body, grid=(S // ts,),
        in_specs=[_row(ts, MLA_QR, C_CQ // MLA_QR), _row(ts, MLA_KVR, C_CKV // MLA_KVR),
                  _row(ts, hw), _row(ts, hw), _row(ts, hw), _row(ts, LANE),
                  _row(ts, LANE), _row(ts, LANE), _row(ts, LANE), _vec(MLA_QR), _vec(MLA_KVR),
                  _vec(hw, MLA_QR), _vec(2 * hw, MLA_KVR)],
        out_specs=(_row(ts, MLA_QR), _row(ts, MLA_KVR), _row(ts, LANE), _row(ts, hw), _row(ts, 2 * hw),
                   _row(ts, MLA_QR), _row(ts, MLA_KVR), _vec(MLA_QR), _vec(MLA_KVR)),
        out_shape=(act(MLA_QR), act(MLA_KVR), act(LANE), act(hw), act(2 * hw), act(MLA_QR), act(MLA_KVR),
                   jax.ShapeDtypeStruct((1, MLA_QR), F32), jax.ShapeDtypeStruct((1, MLA_KVR), F32)),
        compiler_params=_cp(1), name=name,
    )(proj, proj, dq, dk, dv, dmisc_ssd, cosf, sina, sinb, gq, gkv, wuq, wukv)


_MLA_SCALE = 1.0 / math.sqrt(MLA_NOPE + MLA_ROPE)
_LOG2E = 1.4426950408889634
_Q_SCALE = _MLA_SCALE * _LOG2E
ATT_CHUNK = 1024


def _tri_grid(nq, by_key):
    if by_key:
        pairs = [(i, j) for j in range(nq) for i in range(j, nq)]
    else:
        pairs = [(i, j) for i in range(nq) for j in range(i + 1)]
    return jnp.asarray([p[0] for p in pairs], jnp.int32), jnp.asarray([p[1] for p in pairs], jnp.int32)


def _attn_fwd(q, k, vt, *, name):
    S = q.shape[0]
    tq = _tile(S, TQ_ATT)
    nq = S // tq
    itab, jtab = _tri_grid(nq, False)

    def body(it_ref, jt_ref, q_ref, k_ref, vt_ref, o_ref, lse_ref, lset_ref, m_ref, l_ref, acc_ref):
        t = pl.program_id(1)
        i, j = it_ref[t], jt_ref[t]

        @pl.when(j == 0)
        def _():
            m_ref[...] = jnp.full_like(m_ref, -jnp.inf)
            l_ref[...] = jnp.zeros_like(l_ref)
            acc_ref[...] = jnp.zeros_like(acc_ref)

        def step(diagonal):
            s = _dot_nt(k_ref[...], q_ref[...])
            if diagonal:
                kk = lax.broadcasted_iota(jnp.int32, (tq, tq), 0)
                s = jnp.where(kk <= lax.broadcasted_iota(jnp.int32, (tq, tq), 1), s, -jnp.inf)
            m_prev = m_ref[...]
            m_new = jnp.maximum(m_prev, jnp.max(s, axis=0, keepdims=True))
            p = jnp.exp2(s - m_new)
            alpha = jnp.exp2(m_prev - m_new)
            l_ref[...] = alpha * l_ref[...] + _colsum(p)
            acc_ref[...] = alpha * acc_ref[...] + _dot(vt_ref[...], p)
            m_ref[...] = m_new

        pl.when(j < i)(functools.partial(step, False))
        pl.when(j == i)(functools.partial(step, True))

        @pl.when(j == i)
        def _():
            o_ref[...] = (acc_ref[...] / l_ref[...]).T
            lse = m_ref[...] + jnp.log2(l_ref[...])
            lset_ref[...] = jnp.broadcast_to(lse, (SUB, tq))
            lse_ref[...] = jnp.broadcast_to(lse, (LANE, tq)).T

    qspec = pl.BlockSpec((tq, LANE), lambda h, t, it, jt: (it[t], h))
    kspec = pl.BlockSpec((tq, LANE), lambda h, t, it, jt: (jt[t], h))
    vtspec = pl.BlockSpec((LANE, tq), lambda h, t, it, jt: (h, jt[t]))
    oshape = jax.ShapeDtypeStruct((S, MLA_HEADS * LANE), F32)
    return pl.pallas_call(
        body,
        grid_spec=pltpu.PrefetchScalarGridSpec(
            num_scalar_prefetch=2, grid=(MLA_HEADS, itab.shape[0]), in_specs=[qspec, kspec, vtspec],
            out_specs=(qspec, qspec, pl.BlockSpec((SUB, tq), lambda h, t, it, jt: (h, it[t]))),
            scratch_shapes=[pltpu.VMEM((1, tq), F32), pltpu.VMEM((1, tq), F32), pltpu.VMEM((LANE, tq), F32)]),
        out_shape=(oshape, oshape, jax.ShapeDtypeStruct((MLA_HEADS * SUB, S), F32)),
        compiler_params=_cp(2), name=name)(itab, jtab, q, k, vt)


def _attn_bwd_dq(q, k, v, o, lse, dycat, *, name):
    S = q.shape[0]
    tq = _tile(S, TQ_ATT)
    nq = S // tq
    rc = min(ATT_CHUNK, tq)
    itab, jtab = _tri_grid(nq, False)

    def body(it_ref, jt_ref, q_ref, k_ref, v_ref, o_ref, lse_ref, do_ref, dq_ref, acc_ref):
        t = pl.program_id(1)
        i, j = it_ref[t], jt_ref[t]

        @pl.when(j == 0)
        def _():
            acc_ref[...] = jnp.zeros_like(acc_ref)

        def step(diagonal):
            kv, vv = k_ref[...], v_ref[...]
            for r in range(tq // rc):
                rows = slice(r * rc, (r + 1) * rc)
                s = _dot_nt(q_ref[rows, :], kv)
                if diagonal:
                    rr = r * rc + lax.broadcasted_iota(jnp.int32, (rc, tq), 0)
                    s = jnp.where(lax.broadcasted_iota(jnp.int32, (rc, tq), 1) <= rr, s, -jnp.inf)
                p = jnp.exp2(s - lse_ref[rows, 0:1])
                dov = do_ref[rows, :]
                delta = _rowsum(dov * o_ref[rows, :])
                ds = p * (_dot_nt(dov, vv) - delta)
                acc_ref[rows, :] += _dot(ds, kv)

        pl.when(j < i)(functools.partial(step, False))
        pl.when(j == i)(functools.partial(step, True))

        @pl.when(j == i)
        def _():
            dq_ref[...] = acc_ref[...] * _MLA_SCALE

    qspec = pl.BlockSpec((tq, LANE), lambda h, t, it, jt: (it[t], h))
    kspec = pl.BlockSpec((tq, LANE), lambda h, t, it, jt: (jt[t], h))
    dospec = pl.BlockSpec((tq, LANE), lambda h, t, it, jt: (it[t], SSD_INNER // LANE + h))
    return pl.pallas_call(
        body,
        grid_spec=pltpu.PrefetchScalarGridSpec(
            num_scalar_prefetch=2, grid=(MLA_HEADS, itab.shape[0]),
            in_specs=[qspec, kspec, kspec, qspec, qspec, dospec], out_specs=qspec,
            scratch_shapes=[pltpu.VMEM((tq, LANE), F32)]),
        out_shape=jax.ShapeDtypeStruct((S, MLA_HEADS * LANE), F32),
        compiler_params=_cp(2), name=name)(itab, jtab, q, k, v, o, lse, dycat)


def _attn_bwd_dkv(q, k, v, o, lset, dycat, *, name):
    S = q.shape[0]
    tq = _tile(S, TQ_ATT)
    nq = S // tq
    kc = min(ATT_CHUNK, tq)
    itab, jtab = _tri_grid(nq, True)

    def body(it_ref, jt_ref, q_ref, k_ref, v_ref, o_ref, lset_ref, do_ref, dk_ref, dv_ref, dk_acc, dv_acc):
        t = pl.program_id(1)
        i, j = it_ref[t], jt_ref[t]

        @pl.when(i == j)
        def _():
            dk_acc[...] = jnp.zeros_like(dk_acc)
            dv_acc[...] = jnp.zeros_like(dv_acc)

        def step(diagonal):
            qv, dov = q_ref[...], do_ref[...]
            delta = lax.dot_general(jnp.ones((SUB, LANE), F32), dov * o_ref[...], (((1,), (1,)), ((), ())),
                                    precision=_HI, preferred_element_type=F32)[0:1]
            lse = lset_ref[0:1, :]
            for c in range(tq // kc):
                rows = slice(c * kc, (c + 1) * kc)
                s = _dot_nt(k_ref[rows, :], qv)
                if diagonal:
                    kk = c * kc + lax.broadcasted_iota(jnp.int32, (kc, tq), 0)
                    s = jnp.where(kk <= lax.broadcasted_iota(jnp.int32, (kc, tq), 1), s, -jnp.inf)
                p = jnp.exp2(s - lse)
                dv_acc[rows, :] += _dot(p, dov)
                ds = p * (_dot_nt(v_ref[rows, :], dov) - delta)
                dk_acc[rows, :] += _dot(ds, qv)

        pl.when(i > j)(functools.partial(step, False))
        pl.when(i == j)(functools.partial(step, True))

        @pl.when(i == nq - 1)
        def _():
            dk_ref[...] = dk_acc[...] * (1.0 / _LOG2E)
            dv_ref[...] = dv_acc[...]

    qspec = pl.BlockSpec((tq, LANE), lambda h, t, it, jt: (it[t], h))
    kspec = pl.BlockSpec((tq, LANE), lambda h, t, it, jt: (jt[t], h))
    dospec = pl.BlockSpec((tq, LANE), lambda h, t, it, jt: (it[t], SSD_INNER // LANE + h))
    lspec = pl.BlockSpec((SUB, tq), lambda h, t, it, jt: (h, it[t]))
    oshape = jax.ShapeDtypeStruct((S, MLA_HEADS * LANE), F32)
    return pl.pallas_call(
        body,
        grid_spec=pltpu.PrefetchScalarGridSpec(
            num_scalar_prefetch=2, grid=(MLA_HEADS, itab.shape[0]),
            in_specs=[qspec, kspec, kspec, qspec, lspec, dospec], out_specs=(kspec, kspec),
            scratch_shapes=[pltpu.VMEM((tq, LANE), F32), pltpu.VMEM((tq, LANE), F32)]),
        out_shape=(oshape, oshape), compiler_params=_cp(2), name=name)(itab, jtab, q, k, v, o, lset, dycat)


_SWA_SCALE = 1.0 / math.sqrt(SWA_HD)
_SWA_KW = SWA_KV * LANE


def _swa_specs(S, ts, rev):
    n = S // ts
    t = (lambda i: n - 1 - i) if rev else (lambda i: i)
    hb = lambda i: jnp.maximum(t(i) * (ts // WINDOW) - 1, 0)
    return [
        pl.BlockSpec((ts, SWA_HEADS * LANE), lambda i: (t(i), C_SQ // (SWA_HEADS * LANE))),
        pl.BlockSpec((ts, _SWA_KW), lambda i: (t(i), C_SK // _SWA_KW)),
        pl.BlockSpec((WINDOW, _SWA_KW), lambda i: (hb(i), C_SK // _SWA_KW)),
        pl.BlockSpec((ts, _SWA_KW), lambda i: (t(i), C_SV // _SWA_KW)),
        pl.BlockSpec((WINDOW, _SWA_KW), lambda i: (hb(i), C_SV // _SWA_KW)),
    ]


def _swa_scores(qh, kk, t, b, ts):
    s = _dot_nt(qh, kk) * _SWA_SCALE
    row = lax.broadcasted_iota(jnp.int32, (WINDOW, 2 * WINDOW), 0)
    col = lax.broadcasted_iota(jnp.int32, (WINDOW, 2 * WINDOW), 1)
    rel = WINDOW + row - col
    kpos = t * ts + (b - 1) * WINDOW + col
    return jnp.where((rel >= 0) & (rel < WINDOW) & (kpos >= 0), s, -jnp.inf)


def _swa_fwd(proj, sinks, *, name):
    S = proj.shape[0]
    ts = _tile(S, TS_SWA)
    nb = ts // WINDOW

    def body(q_ref, k_ref, kh_ref, v_ref, vh_ref, sink_ref, o_ref, lse_ref):
        t = pl.program_id(0)
        kext = jnp.concatenate([kh_ref[...], k_ref[...]], axis=0)
        vext = jnp.concatenate([vh_ref[...], v_ref[...]], axis=0)
        for b in range(nb):
            rows = slice(WINDOW * b, WINDOW * (b + 1))
            for h in range(SWA_HEADS):
                kvl = slice(LANE * (h // (SWA_HEADS // SWA_KV)), LANE * (h // (SWA_HEADS // SWA_KV) + 1))
                hl = slice(LANE * h, LANE * (h + 1))
                kk = kext[WINDOW * b:WINDOW * (b + 2), kvl]
                vv = vext[WINDOW * b:WINDOW * (b + 2), kvl]
                s = _swa_scores(q_ref[rows, hl], kk, t, b, ts)
                sk = sink_ref[:, h:h + 1]
                m = jnp.maximum(jnp.max(s, axis=1, keepdims=True), sk)
                p = jnp.exp(s - m)
                den = _rowsum(p) + jnp.exp(sk - m)
                o_ref[rows, hl] = _dot(p, vv) / den
                lse_ref[rows, hl] = jnp.broadcast_to(m + jnp.log(den), (WINDOW, LANE))

    oshape = jax.ShapeDtypeStruct((S, SWA_HEADS * LANE), F32)
    ospec = pl.BlockSpec((ts, SWA_HEADS * LANE), lambda i: (i, 0))
    return pl.pallas_call(
        body, grid=(S // ts,), in_specs=_swa_specs(S, ts, False) + [_vec(LANE)], out_specs=(ospec, ospec),
        out_shape=(oshape, oshape), compiler_params=_cp(1), name=name)(proj, proj, proj, proj, proj, sinks)


def _swa_bwd(proj, o, lse, dycat, sinks, *, name):
    S = proj.shape[0]
    ts = _tile(S, TS_SWA)
    nb = ts // WINDOW
    n = S // ts
    grp = SWA_HEADS // SWA_KV

    def body(q_ref, k_ref, kh_ref, v_ref, vh_ref, o_ref, lse_ref, do_ref, sink_ref,
             dq_ref, dk_ref, dv_ref, dsink_ref, dk_carry, dv_carry):
        i = pl.program_id(0)
        t = n - 1 - i

        @pl.when(i == 0)
        def _():
            dk_carry[...] = jnp.zeros_like(dk_carry)
            dv_carry[...] = jnp.zeros_like(dv_carry)
            dsink_ref[...] = jnp.zeros_like(dsink_ref)

        kext = jnp.concatenate([kh_ref[...], k_ref[...]], axis=0)
        vext = jnp.concatenate([vh_ref[...], v_ref[...]], axis=0)
        lane1 = lax.broadcasted_iota(jnp.int32, (1, LANE), 1)
        dkb = [[jnp.zeros((WINDOW, LANE), F32) for _ in range(SWA_KV)] for _ in range(nb + 1)]
        dvb = [[jnp.zeros((WINDOW, LANE), F32) for _ in range(SWA_KV)] for _ in range(nb + 1)]
        dsink = jnp.zeros((1, LANE), F32)
        for b in range(nb):
            rows = slice(WINDOW * b, WINDOW * (b + 1))
            for h in range(SWA_HEADS):
                kvh = h // grp
                kvl = slice(LANE * kvh, LANE * (kvh + 1))
                hl = slice(LANE * h, LANE * (h + 1))
                kk = kext[WINDOW * b:WINDOW * (b + 2), kvl]
                vv = vext[WINDOW * b:WINDOW * (b + 2), kvl]
                qh = q_ref[rows, hl]
                lse_h = lse_ref[rows, LANE * h:LANE * h + 1]
                p = jnp.exp(_swa_scores(qh, kk, t, b, ts) - lse_h)
                doh = do_ref[rows, hl]
                delta = _rowsum(doh * o_ref[rows, hl])
                ds = p * (_dot_nt(doh, vv) - delta)
                sk = sink_ref[:, h:h + 1]
                dsink = dsink + jnp.where(lane1 == h, -jnp.sum(jnp.exp(sk - lse_h) * delta, keepdims=True), 0.0)
                dq_ref[rows, hl] = (_dot(ds, kk) * _SWA_SCALE).astype(_ACT)
                dkk = _dot_tn(ds, qh) * _SWA_SCALE
                dvv = _dot_tn(p, doh)
                dkb[b][kvh] = dkb[b][kvh] + dkk[:WINDOW]
                dkb[b + 1][kvh] = dkb[b + 1][kvh] + dkk[WINDOW:]
                dvb[b][kvh] = dvb[b][kvh] + dvv[:WINDOW]
                dvb[b + 1][kvh] = dvb[b + 1][kvh] + dvv[WINDOW:]
        dsink_ref[...] += dsink
        for dref, blocks, carry in ((dk_ref, dkb, dk_carry), (dv_ref, dvb, dv_carry)):
            old = carry[...]
            for b in range(1, nb + 1):
                blk = jnp.concatenate(blocks[b], axis=1)
                if b == nb:
                    blk = blk + old
                dref[WINDOW * (b - 1):WINDOW * b, :] = blk.astype(_ACT)
            carry[...] = jnp.concatenate(blocks[0], axis=1)

    hw = SWA_HEADS * LANE
    rev = lambda i: (n - 1 - i, 0)
    mix = lambda i: (n - 1 - i, (SSD_INNER + MLA_HEADS * LANE) // hw)
    return pl.pallas_call(
        body, grid=(n,),
        in_specs=_swa_specs(S, ts, True) + [pl.BlockSpec((ts, hw), rev), pl.BlockSpec((ts, hw), rev),
                                            pl.BlockSpec((ts, hw), mix), _vec(LANE)],
        out_specs=(pl.BlockSpec((ts, hw), rev), pl.BlockSpec((ts, _SWA_KW), rev), pl.BlockSpec((ts, _SWA_KW), rev),
                   _vec(LANE)),
        out_shape=(jax.ShapeDtypeStruct((S, hw), _ACT), jax.ShapeDtypeStruct((S, _SWA_KW), _ACT),
                   jax.ShapeDtypeStruct((S, _SWA_KW), _ACT), jax.ShapeDtypeStruct((1, LANE), F32)),
        scratch_shapes=[pltpu.VMEM((WINDOW, _SWA_KW), F32), pltpu.VMEM((WINDOW, _SWA_KW), F32)],
        compiler_params=_cp(1), name=name)(proj, proj, proj, proj, proj, o, lse, dycat, sinks)


def _exchange(arrays, *, scatter, name):
    n = len(arrays)

    def body(*refs):
        ins, outs = refs[:n], refs[n:2 * n]
        send_sems, recv_sems, loc_sems = refs[2 * n:]
        x, y, c = lax.axis_index("x"), lax.axis_index("y"), lax.axis_index("c")
        me = 4 * x + 2 * y + c

        def src(i, dest):
            return ins[i].at[dest] if scatter else ins[i]

        local = [pltpu.make_async_copy(src(i, me), outs[i].at[me], loc_sems.at[i]) for i in range(n)]
        for cp in local:
            cp.start()
        sends, recvs = [], []
        for k in range(1, NDEV):
            px = 1 - x if k & 4 else x
            py = 1 - y if k & 2 else y
            pc = 1 - c if k & 1 else c
            peer = 4 * px + 2 * py + pc
            for i in range(n):
                common = dict(send_sem=send_sems.at[i, k - 1], recv_sem=recv_sems.at[i, k - 1],
                              device_id=(px, py, pc), device_id_type=pl.DeviceIdType.MESH)
                sends.append(pltpu.make_async_remote_copy(src_ref=src(i, peer), dst_ref=outs[i].at[me], **common))
                recvs.append(pltpu.make_async_remote_copy(src_ref=src(i, peer), dst_ref=outs[i].at[peer], **common))
        for cp in sends:
            cp.start()
        for cp in recvs:
            cp.wait_recv()
        for cp in sends:
            cp.wait_send()
        for cp in local:
            cp.wait()

    hbm = pl.BlockSpec(memory_space=pl.ANY)
    out_shape = tuple(jax.ShapeDtypeStruct(a.shape if scatter else (NDEV,) + a.shape, a.dtype) for a in arrays)
    return pl.pallas_call(
        body, in_specs=[hbm] * n, out_specs=tuple([hbm] * n), out_shape=out_shape,
        scratch_shapes=[pltpu.SemaphoreType.DMA((n, NDEV - 1)), pltpu.SemaphoreType.DMA((n, NDEV - 1)),
                        pltpu.SemaphoreType.DMA((n,))],
        name=name)(*arrays)


def _adamw(w, m, v, parts, *, name):
    R, C = w.shape
    npart = parts.shape[0]
    cap = max(SUB, ((1 << 18) // C) // SUB * SUB)
    tr = _tile(R, cap, SUB)

    def body(w_ref, m_ref, v_ref, p_ref, g_ref, d_ref, mo_ref, vo_ref):
        g = p_ref[0]
        for k in range(1, npart):
            g = g + p_ref[k]
        mn = ADAM_B1 * m_ref[...] + (1.0 - ADAM_B1) * g
        vn = ADAM_B2 * v_ref[...] + (1.0 - ADAM_B2) * (g * g)
        m_hat = mn / (1.0 - ADAM_B1 ** ADAM_STEP)
        v_hat = vn / (1.0 - ADAM_B2 ** ADAM_STEP)
        g_ref[...] = g
        d_ref[...] = -ADAM_LR * (m_hat / (jnp.sqrt(v_hat) + ADAM_EPS) + ADAM_WD * w_ref[...])
        mo_ref[...] = mn
        vo_ref[...] = vn

    spec = pl.BlockSpec((tr, C), lambda i: (i, 0))
    oshape = jax.ShapeDtypeStruct((R, C), F32)
    return pl.pallas_call(
        body, grid=(R // tr,), in_specs=[spec] * 3 + [pl.BlockSpec((npart, tr, C), lambda i: (0, i, 0))],
        out_specs=(spec,) * 4, out_shape=(oshape,) * 4, compiler_params=_cp(1), name=name)(w, m, v, parts)


def _adamw_many(ws, ms, vs, landed, mine, me, *, name):
    n = len(ws)

    def body(me_ref, *refs):
        w_r, m_r, v_r, p_r, o_r = (refs[k * n:(k + 1) * n] for k in range(5))
        outs = refs[5 * n:]
        for i in range(n):
            own = o_r[i][...]
            g = jnp.where(me_ref[0] == 0, own, p_r[i][0])
            for k in range(1, NDEV):
                g = g + jnp.where(me_ref[0] == k, own, p_r[i][k])
            mn = ADAM_B1 * m_r[i][...] + (1.0 - ADAM_B1) * g
            vn = ADAM_B2 * v_r[i][...] + (1.0 - ADAM_B2) * (g * g)
            m_hat = mn / (1.0 - ADAM_B1 ** ADAM_STEP)
            v_hat = vn / (1.0 - ADAM_B2 ** ADAM_STEP)
            outs[4 * i][...] = g
            outs[4 * i + 1][...] = -ADAM_LR * (m_hat / (jnp.sqrt(v_hat) + ADAM_EPS) + ADAM_WD * w_r[i][...])
            outs[4 * i + 2][...] = mn
            outs[4 * i + 3][...] = vn

    vmem = pl.BlockSpec(memory_space=pltpu.VMEM)
    return pl.pallas_call(
        body, in_specs=[pl.BlockSpec(memory_space=pltpu.SMEM)] + [vmem] * (5 * n), out_specs=(vmem,) * (4 * n),
        out_shape=tuple(jax.ShapeDtypeStruct(w.shape, F32) for w in ws for _ in range(4)),
        name=name)(me, *ws, *ms, *vs, *landed, *mine)


def _adamw_layer(l, w, m, v, landed, mine, me, prev, *, name):
    L, R, C = w.shape
    npart = landed.shape[0]
    cap = max(2 * SUB, ((1 << 18) // C) // (2 * SUB) * (2 * SUB))
    tr = _tile(R, cap, 2 * SUB)
    nprev = 0 if prev is None else 4

    def body(me_ref, *refs):
        w_ref, m_ref, v_ref, p_ref, own_ref = refs[:5]
        g_ref, d_ref, mo_ref, vo_ref = refs[5 + nprev:]
        own = own_ref[...].astype(F32)
        g = jnp.where(me_ref[0] == 0, own, p_ref[0].astype(F32))
        for k in range(1, npart):
            g = g + jnp.where(me_ref[0] == k, own, p_ref[k].astype(F32))
        mn = ADAM_B1 * m_ref[...] + (1.0 - ADAM_B1) * g
        vn = ADAM_B2 * v_ref[...] + (1.0 - ADAM_B2) * (g * g)
        m_hat = mn / (1.0 - ADAM_B1 ** ADAM_STEP)
        v_hat = vn / (1.0 - ADAM_B2 ** ADAM_STEP)
        g_ref[...] = g
        d_ref[...] = -ADAM_LR * (m_hat / (jnp.sqrt(v_hat) + ADAM_EPS) + ADAM_WD * w_ref[...])
        mo_ref[...] = mn
        vo_ref[...] = vn

    spec = pl.BlockSpec((None, tr, C), lambda i, me_ref: (l, i, 0))
    oshape = jax.ShapeDtypeStruct((L, R, C), F32)
    return pl.pallas_call(
        body,
        grid_spec=pltpu.PrefetchScalarGridSpec(
            num_scalar_prefetch=1, grid=(R // tr,),
            in_specs=[spec] * 3 + [pl.BlockSpec((npart, tr, C), lambda i, me_ref: (0, i, 0)),
                                   pl.BlockSpec((None, tr, C), lambda i, me_ref: (me_ref[0], i, 0))]
            + [pl.BlockSpec(memory_space=pl.ANY)] * nprev,
            out_specs=(spec,) * 4),
        out_shape=(oshape,) * 4, input_output_aliases={6 + k: k for k in range(nprev)},
        compiler_params=_cp(1), name=name)(me, w, m, v, landed, mine, *(prev or ()))


_HBM = pl.BlockSpec(memory_space=pltpu.HBM)
_SEM = pl.BlockSpec(memory_space=pltpu.SEMAPHORE)
_EFFECT = pltpu.SideEffectType.DATAFLOW_SIDE_EFFECTING


def _peers():
    x, y, c = lax.axis_index("x"), lax.axis_index("y"), lax.axis_index("c")
    out = []
    for k in range(1, NDEV):
        px = 1 - x if k & 4 else x
        py = 1 - y if k & 2 else y
        pc = 1 - c if k & 1 else c
        out.append((k - 1, (px, py, pc), 4 * px + 2 * py + pc))
    return 4 * x + 2 * y + c, out


def _xchg_start(arrays, *, scatter, name):
    n = len(arrays)
    lands = [lax.empty(a.shape if scatter else (NDEV,) + a.shape, a.dtype) for a in arrays]

    def body(*refs):
        ins, lnd = refs[:n], refs[n:2 * n]
        send_sems, recv_sems = refs[2 * n], refs[2 * n + 1]
        token = refs[-1]
        me, peers = _peers()
        for k, dev, peer in peers:
            for i in range(n):
                pltpu.make_async_remote_copy(
                    src_ref=ins[i].at[peer] if scatter else ins[i], dst_ref=lnd[i].at[me],
                    send_sem=send_sems.at[i * (NDEV - 1) + k], recv_sem=recv_sems.at[i * (NDEV - 1) + k],
                    device_id=dev, device_id_type=pl.DeviceIdType.MESH).start()
        token[...] = jnp.zeros_like(token)

    sems = pltpu.SemaphoreType.DMA((n * (NDEV - 1),))
    res = pl.pallas_call(
        body, name=name,
        out_shape=(sems, sems) + tuple(pltpu.HBM(t.shape, t.dtype) for t in list(arrays) + lands)
        + (jax.ShapeDtypeStruct((SUB, LANE), F32),),
        in_specs=[_HBM] * (2 * n), out_specs=(_SEM, _SEM) + (_HBM,) * (2 * n) + (pl.BlockSpec(memory_space=pltpu.VMEM),),
        input_output_aliases={i: 2 + i for i in range(2 * n)},
        compiler_params=pltpu.CompilerParams(has_side_effects=_EFFECT),
    )(*[pltpu.with_memory_space_constraint(t, pltpu.HBM) for t in list(arrays) + lands])
    return dict(send=res[0], recv=res[1], thru=list(res[2:2 + 2 * n]), token=res[-1], scatter=scatter, n=n)


def _xchg_wait(handle, after, *, name):
    n, scatter = handle["n"], handle["scatter"]
    thru = handle["thru"]

    def body(*refs):
        ins, lnd = refs[:n], refs[n:2 * n]
        send_sems, recv_sems = refs[2 * n], refs[2 * n + 1]
        me, peers = _peers()
        for k, dev, peer in peers:
            for i in range(n):
                cp = pltpu.make_async_remote_copy(
                    src_ref=ins[i].at[peer] if scatter else ins[i], dst_ref=lnd[i].at[peer],
                    send_sem=send_sems.at[i * (NDEV - 1) + k], recv_sem=recv_sems.at[i * (NDEV - 1) + k],
                    device_id=dev, device_id_type=pl.DeviceIdType.MESH)
                cp.wait_send()
                cp.wait_recv()

    res = pl.pallas_call(
        body, name=name, out_shape=tuple(pltpu.HBM(t.shape, t.dtype) for t in thru),
        in_specs=[_HBM] * (2 * n) + [_SEM, _SEM, pl.BlockSpec(memory_space=pl.ANY)], out_specs=(_HBM,) * (2 * n),
        input_output_aliases={i: i for i in range(2 * n)},
        compiler_params=pltpu.CompilerParams(has_side_effects=_EFFECT),
    )(*thru, handle["send"], handle["recv"], after)
    return list(res[:n]), list(res[n:])


def _pad_heads(w, nh, hd, axis=-1):
    axis = axis % w.ndim
    shp = w.shape
    w = w.reshape(shp[:axis] + (nh, hd) + shp[axis + 1:])
    pads = [(0, 0)] * w.ndim
    pads[axis + 1] = (0, LANE - hd)
    return jnp.pad(w, pads).reshape(shp[:axis] + (nh * LANE,) + shp[axis + 1:])


def _unpad_heads(w, nh, hd, axis=-1):
    axis = axis % w.ndim
    shp = w.shape
    w = w.reshape(shp[:axis] + (nh, LANE) + shp[axis + 1:])
    w = lax.slice_in_dim(w, 0, hd, axis=axis + 1)
    return w.reshape(shp[:axis] + (nh * hd,) + shp[axis + 1:])


_O_DT = SSD_INNER + SSD_XBC
_O_CQ = _O_DT + SSD_HEADS
_O_CKV = _O_CQ + MLA_QR
_O_KR = _O_CKV + MLA_KVR
_O_SQ = _O_KR + MLA_ROPE
_O_SK = _O_SQ + SWA_HEADS * SWA_HD
_O_SV = _O_SK + SWA_KV * SWA_HD


def _w_in_to_padded(w, axis=-1):
    axis = axis % w.ndim
    cut = lambda a, b: lax.slice_in_dim(w, a, b, axis=axis)
    z, xbc, dt = cut(0, SSD_INNER), cut(SSD_INNER, _O_DT), cut(_O_DT, _O_CQ)
    cq, ckv, kr = cut(_O_CQ, _O_CKV), cut(_O_CKV, _O_KR), cut(_O_KR, _O_SQ)
    sq, sk, sv = cut(_O_SQ, _O_SK), cut(_O_SK, _O_SV), cut(_O_SV, D_IN)
    zeros = lambda n: jnp.zeros(w.shape[:axis] + (n,) + w.shape[axis + 1:], w.dtype)
    return jnp.concatenate([xbc, z, cq, ckv, dt, zeros(ROPE_LANE - SSD_HEADS), kr, zeros(LANE - ROPE_LANE - MLA_ROPE),
                            _pad_heads(sq, SWA_HEADS, SWA_HD, axis), _pad_heads(sk, SWA_KV, SWA_HD, axis),
                            _pad_heads(sv, SWA_KV, SWA_HD, axis)], axis=axis)


def _w_in_from_padded(g, axis=-1):
    axis = axis % g.ndim
    cut = lambda a, b: lax.slice_in_dim(g, a, b, axis=axis)
    xbc, z, cq, ckv = cut(C_XBC, C_Z), cut(C_Z, C_CQ), cut(C_CQ, C_CKV), cut(C_CKV, C_MISC)
    dt, kr = cut(C_MISC, C_MISC + SSD_HEADS), cut(C_MISC + ROPE_LANE, C_MISC + ROPE_LANE + MLA_ROPE)
    sq = _unpad_heads(cut(C_SQ, C_SK), SWA_HEADS, SWA_HD, axis)
    sk = _unpad_heads(cut(C_SK, C_SV), SWA_KV, SWA_HD, axis)
    sv = _unpad_heads(cut(C_SV, D_INP), SWA_KV, SWA_HD, axis)
    return jnp.concatenate([z, xbc, dt, cq, ckv, kr, sq, sk, sv], axis=axis)


def _w_out_to_padded(w):
    a = SSD_INNER
    b = a + MLA_HEADS * MLA_V
    return jnp.concatenate([w[..., :a, :], _pad_heads(w[..., a:b, :], MLA_HEADS, MLA_V, axis=-2),
                            _pad_heads(w[..., b:, :], SWA_HEADS, SWA_HD, axis=-2)], axis=-2)


def _w_out_from_padded(g):
    a = SSD_INNER
    b = a + MLA_HEADS * LANE
    return jnp.concatenate([g[..., :a, :], _unpad_heads(g[..., a:b, :], MLA_HEADS, MLA_V, axis=-2),
                            _unpad_heads(g[..., b:, :], SWA_HEADS, SWA_HD, axis=-2)], axis=-2)


def _w_ukv_to_padded(w):
    w4 = w.reshape(w.shape[:-1] + (MLA_HEADS, MLA_NOPE + MLA_V))
    flat = lambda t: t.reshape(w.shape[:-1] + (MLA_HEADS * t.shape[-1],))
    return jnp.concatenate([_pad_heads(flat(w4[..., :MLA_NOPE]), MLA_HEADS, MLA_NOPE),
                            _pad_heads(flat(w4[..., MLA_NOPE:]), MLA_HEADS, MLA_V)], axis=-1)


def _w_ukv_from_padded(g):
    hw = MLA_HEADS * LANE
    gk = _unpad_heads(g[..., :hw], MLA_HEADS, MLA_NOPE).reshape(g.shape[:-1] + (MLA_HEADS, MLA_NOPE))
    gv = _unpad_heads(g[..., hw:], MLA_HEADS, MLA_V).reshape(g.shape[:-1] + (MLA_HEADS, MLA_V))
    return jnp.concatenate([gk, gv], axis=-1).reshape(g.shape[:-1] + (MLA_HEADS * (MLA_NOPE + MLA_V),))


def _pad_lane(v):
    return jnp.pad(v, [(0, 0)] * (v.ndim - 1) + [(0, LANE - v.shape[-1])])


def _rope_tables(positions):
    inv_freq = ROPE_THETA ** (-jnp.arange(0, MLA_ROPE, 2, dtype=F32) / MLA_ROPE)
    ang = positions.astype(F32).reshape(-1, 1) * inv_freq
    cos, sin = jnp.cos(ang), jnp.sin(ang)
    S = ang.shape[0]
    one, zero = jnp.ones((S, ROPE_LANE), F32), jnp.zeros((S, ROPE_LANE), F32)
    tail1, tail0 = jnp.ones((S, LANE - ROPE_LANE - MLA_ROPE), F32), jnp.zeros((S, LANE - ROPE_LANE - MLA_ROPE), F32)
    z16 = jnp.zeros_like(sin)
    return (jnp.concatenate([one, cos, cos, tail1], axis=1), jnp.concatenate([zero, -sin, z16, tail0], axis=1),
            jnp.concatenate([zero, z16, sin, tail0], axis=1))


def _layer_fwd(l, x_in, f_prev, gate_prev, mod, P, tabs):
    sh1, sc1, g1, sh2, sc2, g2 = [mod[k:k + 1] for k in range(6)]
    tag = f"l{l}_"
    if f_prev is None:
        x0 = x_in
        h1 = _norm_fwd(x0, P["n1g"], sc1, sh1, name=tag + "norm1")
    else:
        x0, h1 = _norm_fwd(x_in, P["n1g"], sc1, sh1, f=f_prev, gate=gate_prev, name=tag + "norm1")
    proj = _mm(h1, P["w_in"], tb=True, name=tag + "proj")
    P.update(P.pop("mid")(proj))
    y_ssd, hin = _ssd_fwd(proj, P["ssd_cw"], P["ssd_cb"], P["dtb"], P["alog"], P["dsk"],
                          P["ssd_ng"], name=tag + "ssd")
    q, k, v, vt = _mla_prep_fwd(proj, *tabs, P["gq"], P["gkv"], P["w_uq"], P["w_ukv"], name=tag + "mla_prep")
    o_mla, lse_mla, lset_mla = _attn_fwd(q, k, vt, name=tag + "mla_attn")
    o_swa, lse_swa = _swa_fwd(proj, P["sinks"], name=tag + "swa")
    ycat = jnp.concatenate([y_ssd.astype(_ACT), o_mla.astype(_ACT), o_swa.astype(_ACT)], axis=1)
    y = _mm(ycat, P["w_out"], name=tag + "out")
    P.update(P.pop("late")(y))
    x1, h2 = _norm_fwd(x0, P["n2g"], sc2, sh2, f=y, gate=g1, name=tag + "norm2")
    up = _mm(h2, P["w_up"], tb=True, name=tag + "up")
    act = _ffn_act_fwd(up, P["fcw"], P["fcb"], name=tag + "ffn_act")
    f = _mm(act, P["w_down"], name=tag + "down")
    saved = dict(x0=x0, h1=h1, proj=proj, hin=hin, q=q, k=k, v=v, o_mla=o_mla, lse_mla=lse_mla, lset_mla=lset_mla, o_swa=o_swa,
                 lse_swa=lse_swa, ycat=ycat, y=y, x1=x1, h2=h2, up=up, act=act, f=f, mod=mod)
    return x1, f, g2, saved


def _layer_bwd(l, dxo, ffn_branch, sv, P, tabs, on_part, below):
    mod = sv["mod"]
    sh1, sc1, g1, sh2, sc2, g2 = [mod[k:k + 1] for k in range(6)]
    tag = f"l{l}_b_"
    G = {}
    df, dg2 = ffn_branch
    dact = _mm(df, P["w_down"], tb=True, name=tag + "dact")
    G["w_down"] = _mm(sv["act"], df, ta=True, out_dtype=_ACT, name=tag + "dw_down")
    dup, G["fcw"], G["fcb"] = _ffn_bwd(sv["up"], dact, P["fcw"], P["fcb"], name=tag + "ffn")
    dh2 = _mm(dup, P["w_up"], name=tag + "dh2")
    G["w_up"] = _mm(dup, sv["h2"], ta=True, out_dtype=_ACT, name=tag + "dw_up")
    token = on_part(l, "ffn", G)
    if token is not None:
        sc2 = sc2 + token
    dx1, G["n2g"], dsc2, dsh2, dy, dg1 = _norm_bwd(dh2, sv["x1"], dxo, P["n2g"], sc2, branch=(sv["y"], g1),
                                                   name=tag + "norm2")
    dycat = _mm(dy, P["w_out"], tb=True, name=tag + "dycat")
    G["w_out"] = _mm(sv["ycat"], dy, ta=True, out_dtype=_ACT, name=tag + "dw_out")
    token = on_part(l, "out", G)
    ssd_cb = P["ssd_cb"] if token is None else P["ssd_cb"] + token
    proj = sv["proj"]
    (dpre, dz, dmisc_ssd, G["ssd_cw"], G["ssd_cb"], G["dtb"], G["alog"], G["dsk"], G["ssd_ng"]) = _ssd_bwd(
        proj, dycat, sv["hin"], P["ssd_cw"], ssd_cb, P["dtb"], P["alog"], P["dsk"],
        P["ssd_ng"], name=tag + "ssd")
    att = (sv["q"], sv["k"], sv["v"], sv["o_mla"])
    dq = _attn_bwd_dq(*att, sv["lse_mla"], dycat, name=tag + "mla_dq")
    dk, dv = _attn_bwd_dkv(*att, sv["lset_mla"], dycat, name=tag + "mla_dkv")
    dcq, dckv, dmisc, dqh, dkv, qn, kvn, G["gq"], G["gkv"] = _mla_prep_bwd(
        proj, dq, dk, dv, dmisc_ssd, *tabs, P["gq"], P["gkv"], P["w_uq"], P["w_ukv"], name=tag + "mla_prep")
    G["w_uq"] = _mm(qn, dqh, ta=True, out_dtype=_ACT, name=tag + "dw_uq")
    G["w_ukv"] = _mm(kvn, dkv, ta=True, out_dtype=_ACT, name=tag + "dw_ukv")
    dsq, dsk_, dsv_, G["sinks"] = _swa_bwd(proj, sv["o_swa"], sv["lse_swa"], dycat, P["sinks"], name=tag + "swa")
    dproj = jnp.concatenate([dpre, dz, dcq, dckv, dmisc, dsq, dsk_, dsv_], axis=1)
    G["w_in"] = _mm(dproj, sv["h1"], ta=True, out_dtype=_ACT, name=tag + "dw_in")
    token = on_part(l, "mixer", G)
    if token is not None:
        sc1 = sc1 + token
    dh1 = _mm(dproj, P["w_in"], name=tag + "dh1")
    res = _norm_bwd(dh1, sv["x0"], dx1, P["n1g"], sc1, branch=below, name=tag + "norm1")
    dx0, G["n1g"], dsc1, dsh1 = res[:4]
    G["mod"] = jnp.concatenate([dsh1, dsc1, dg1, dsh2, dsc2, dg2], axis=0)
    return dx0, G, (res[4:] or None)


def _local_step(x, tgt, mods, get_params, tabs, final_g, on_grads, on_part):
    saved, params = [], []
    xin, f, gate = x, None, None
    for l in range(DEPTH):
        params.append(get_params(l, x if f is None else f))
        xin, f, gate, sv = _layer_fwd(l, xin, f, gate, mods[l], params[l], tabs)
        saved.append(sv)
    loss, dx, dfinal, df, dgate = _final_loss(xin, f, gate, final_g, tgt, name="final_loss")
    branch = (df, dgate)
    for l in reversed(range(DEPTH)):
        below = (saved[l - 1]["f"], saved[l - 1]["mod"][5:6]) if l > 0 else None
        dx, G, branch = _layer_bwd(l, dx, branch, saved[l], params[l], tabs, on_part, below)
        on_grads(l, G)
    return loss[0, 0], dx, dfinal


_WEIGHTS = ['ada_w', 'ada_b', 'norm1_g', 'norm2_g', 'w_in', 'ssd_conv_w', 'ssd_conv_b', 'ssd_dt_bias', 'ssd_a_log',
            'ssd_d', 'ssd_norm_g', 'mla_q_norm_g', 'mla_w_uq', 'mla_kv_norm_g', 'mla_w_ukv', 'swa_sinks', 'w_out',
            'ffn_w_up', 'ffn_conv_w', 'ffn_conv_b', 'ffn_w_down', 'final_norm_g']
_INPUTS = ['x', 'c', 'positions'] + _WEIGHTS + ['loss_target'] + ['m_' + n for n in _WEIGHTS] + ['v_' + n for n in _WEIGHTS]
_SMALL = [('ada_b', 'mod'), ('norm1_g', 'n1g'), ('norm2_g', 'n2g'), ('ssd_conv_b', 'ssd_cb'), ('ssd_dt_bias', 'dtb'),
          ('ssd_a_log', 'alog'), ('ssd_d', 'dsk'), ('ssd_norm_g', 'ssd_ng'), ('mla_q_norm_g', 'gq'),
          ('mla_kv_norm_g', 'gkv'), ('swa_sinks', 'sinks'), ('ffn_conv_b', 'fcb')]
_SHARDED = [('w_in', 'w_in', 2), ('ssd_conv_w', 'ssd_cw', 2), ('mla_w_uq', 'w_uq', 2), ('mla_w_ukv', 'w_ukv', 2),
            ('w_out', 'w_out', 1), ('ffn_w_up', 'w_up', 2), ('ffn_conv_w', 'fcw', 2), ('ffn_w_down', 'w_down', 1)]
_SHARDED_NAMES = [n for n, _, _ in _SHARDED]
_TRANSPOSED = ('w_in', 'ffn_w_up')


def _shard_major(g, axis):
    shp = g.shape
    g = g.reshape(shp[:axis] + (NDEV, shp[axis] // NDEV) + shp[axis + 1:])
    return jnp.moveaxis(g, axis, 0)


def _unshard(g, axis):
    g = jnp.moveaxis(g, 0, axis)
    shp = g.shape
    return g.reshape(shp[:axis] + (shp[axis] * shp[axis + 1],) + shp[axis + 2:])


def kernel(x, c, positions, ada_w, ada_b, norm1_g, norm2_g, w_in, ssd_conv_w, ssd_conv_b, ssd_dt_bias, ssd_a_log, ssd_d, ssd_norm_g, mla_q_norm_g, mla_w_uq, mla_kv_norm_g, mla_w_ukv, swa_sinks, w_out, ffn_w_up, ffn_conv_w, ffn_conv_b, ffn_w_down, final_norm_g, loss_target, m_ada_w, m_ada_b, m_norm1_g, m_norm2_g, m_w_in, m_ssd_conv_w, m_ssd_conv_b, m_ssd_dt_bias, m_ssd_a_log, m_ssd_d, m_ssd_norm_g, m_mla_q_norm_g, m_mla_w_uq, m_mla_kv_norm_g, m_mla_w_ukv, m_swa_sinks, m_w_out, m_ffn_w_up, m_ffn_conv_w, m_ffn_conv_b, m_ffn_w_down, m_final_norm_g, v_ada_w, v_ada_b, v_norm1_g, v_norm2_g, v_w_in, v_ssd_conv_w, v_ssd_conv_b, v_ssd_dt_bias, v_ssd_a_log, v_ssd_d, v_ssd_norm_g, v_mla_q_norm_g, v_mla_w_uq, v_mla_kv_norm_g, v_mla_w_ukv, v_swa_sinks, v_w_out, v_ffn_w_up, v_ffn_conv_w, v_ffn_conv_b, v_ffn_w_down, v_final_norm_g):
    a = dict(zip(_INPUTS, (x, c, positions, ada_w, ada_b, norm1_g, norm2_g, w_in, ssd_conv_w, ssd_conv_b, ssd_dt_bias, ssd_a_log, ssd_d, ssd_norm_g, mla_q_norm_g, mla_w_uq, mla_kv_norm_g, mla_w_ukv, swa_sinks, w_out, ffn_w_up, ffn_conv_w, ffn_conv_b, ffn_w_down, final_norm_g, loss_target, m_ada_w, m_ada_b, m_norm1_g, m_norm2_g, m_w_in, m_ssd_conv_w, m_ssd_conv_b, m_ssd_dt_bias, m_ssd_a_log, m_ssd_d, m_ssd_norm_g, m_mla_q_norm_g, m_mla_w_uq, m_mla_kv_norm_g, m_mla_w_ukv, m_swa_sinks, m_w_out, m_ffn_w_up, m_ffn_conv_w, m_ffn_conv_b, m_ffn_w_down, m_final_norm_g, v_ada_w, v_ada_b, v_norm1_g, v_norm2_g, v_w_in, v_ssd_conv_w, v_ssd_conv_b, v_ssd_dt_bias, v_ssd_a_log, v_ssd_d, v_ssd_norm_g, v_mla_q_norm_g, v_mla_w_uq, v_mla_kv_norm_g, v_mla_w_ukv, v_swa_sinks, v_w_out, v_ffn_w_up, v_ffn_conv_w, v_ffn_conv_b, v_ffn_w_down, v_final_norm_g)))
    axes = ("x", "y", "c")
    me = 4 * lax.axis_index("x") + 2 * lax.axis_index("y") + lax.axis_index("c")
    ncol = ada_w.shape[-1]

    kform = lambda n, t: jnp.swapaxes(t, -1, -2) if n in _TRANSPOSED else t
    mxu_names = ('w_in', 'mla_w_uq', 'mla_w_ukv', 'w_out', 'ffn_w_up', 'ffn_w_down')
    gather_groups = (("early", _SHARDED_NAMES[:4]), ("mid", _SHARDED_NAMES[4:5]), ("late", _SHARDED_NAMES[5:]))

    def own_of(src, names, l):
        return [kform(n, src[n][l]).astype(_MXU) if n in mxu_names else src[n][l] for n in names]

    first_gather = _xchg_start(own_of(a, gather_groups[0][1], 0), scatter=False, name="gather_start_early0")

    c_all = _exchange([c + first_gather["token"][0, 0]], scatter=False, name="gather_c")[0]
    c_act = _silu_call(c_all.reshape(NDEV, D), name="c_act")
    mod_part = jnp.stack([_mm(c_act, ada_w[l], name=f"mod{l}") for l in range(DEPTH)])
    mod_all = _exchange([mod_part], scatter=False, name="gather_mod")[0]
    mod_mine = lax.dynamic_index_in_dim(mod_all, me, axis=2, keepdims=False)
    mods = (jnp.moveaxis(mod_mine, 0, 1).reshape(DEPTH, 6 * D) + ada_b).reshape(DEPTH, 6, D)
    tabs = _rope_tables(positions)

    shard_of = {n: (key, 1 if n in _TRANSPOSED else ax) for n, key, ax in _SHARDED}
    mods, raw = lax.optimization_barrier((mods, {n: a[n] for n in _SHARDED_NAMES}))
    groups_of = lambda l, groups: groups if l == 0 else (("all", sum((list(n) for _, n in groups), [])),)
    gathers, prev = [], first_gather["token"]
    for l in range(DEPTH):
        gathers.append({})
        for grp, names in groups_of(l, gather_groups):
            if (l, grp) == (0, "early"):
                gathers[l][grp] = first_gather
                continue
            srcs, _ = lax.optimization_barrier((own_of(raw, names, l), prev))
            gathers[l][grp] = _xchg_start(srcs, scatter=False, name=f"gather_start_{grp}{l}")
            prev = gathers[l][grp]["token"]

    def place_own(landed, mine, in_place=()):
        slot = lambda t: lax.broadcasted_iota(jnp.int32, (NDEV,) + (1,) * (t.ndim - 1), 0)
        return [lax.dynamic_update_index_in_dim(t, o, me, 0) if k in in_place else jnp.where(slot(t) == me, o[None], t)
                for k, (t, o) in enumerate(zip(landed, mine))]

    as_is = ('ssd_conv_w', 'ffn_w_up', 'ffn_conv_w', 'ffn_w_down')

    def gathered(l, grp, after):
        names = dict(groups_of(l, gather_groups))[grp]
        mine, landed = _xchg_wait(gathers[l][grp], after, name=f"gather_wait_{grp}{l}")
        full = place_own(landed, mine, [k for k, n in enumerate(names) if n in as_is])
        return {n: _unshard(g, shard_of[n][1] - 1) for n, g in zip(names, full)}

    def get_params(l, after):
        full = gathered(l, "early", mods) if l == 0 else gathered(l, "all", after)
        vec = lambda t: t[l].reshape(1, -1)

        def mid(after2):
            return dict(w_out=_w_out_to_padded((gathered(l, "mid", after2) if l == 0 else full)['w_out']))

        def late(after2):
            rest = gathered(l, "late", after2) if l == 0 else full
            return dict(w_up=rest['ffn_w_up'], w_down=rest['ffn_w_down'], fcw=rest['ffn_conv_w'])

        return dict(
            w_in=_w_in_to_padded(full['w_in'], axis=0), w_uq=_pad_heads(full['mla_w_uq'], MLA_HEADS, MLA_NOPE + MLA_ROPE),
            w_ukv=_w_ukv_to_padded(full['mla_w_ukv']), ssd_cw=full['ssd_conv_w'], mid=mid, late=late,
            ssd_cb=vec(ssd_conv_b), dtb=vec(_pad_lane(ssd_dt_bias)), alog=vec(_pad_lane(ssd_a_log)),
            dsk=vec(_pad_lane(ssd_d)), ssd_ng=vec(ssd_norm_g), gq=vec(mla_q_norm_g), gkv=vec(mla_kv_norm_g),
            sinks=vec(_pad_lane(swa_sinks)), fcb=vec(ffn_conv_b), n1g=vec(norm1_g), n2g=vec(norm2_g))

    unpad = dict(w_in=functools.partial(_w_in_from_padded, axis=0), w_out=_w_out_from_padded, w_ukv=_w_ukv_from_padded,
                 w_uq=lambda g: _unpad_heads(g, MLA_HEADS, MLA_NOPE + MLA_ROPE))
    scatter_groups = (("ffn", _SHARDED_NAMES[5:]), ("out", _SHARDED_NAMES[4:5]), ("mixer", _SHARDED_NAMES[:4]))
    grads = [None] * DEPTH
    scatters = [dict() for _ in range(DEPTH)]

    def on_part(l, grp, G):
        if l > 0:
            if grp != scatter_groups[-1][0]:
                return None
            grp = "all"
        parts = [_shard_major(unpad.get(shard_of[n][0], lambda g: g)(G[shard_of[n][0]]), shard_of[n][1] - 1).astype(_ACT)
                 for n in dict(groups_of(l, scatter_groups))[grp]]
        scatters[l][grp] = _xchg_start(parts, scatter=True, name=f"scatter_start_{grp}{l}")
        return scatters[l][grp]["token"][0, 0]

    def on_grads(l, G):
        grads[l] = G

    mods = mods + sum(h["token"][0, 0] for g in gathers for h in g.values())
    loss, dx, dfinal = _local_step(x[0], loss_target[0], mods, get_params, tabs, final_norm_g.reshape(1, D),
                                   on_grads, on_part)
    loss = lax.psum(loss, axes)

    stack = lambda key: jnp.stack([grads[l][key] for l in range(DEPTH)])
    small_names = [n for n, _ in _SMALL] + ['final_norm_g']
    small_g = [stack(key).reshape(DEPTH, -1)[:, :a[name].shape[1]] for name, key in _SMALL] + [dfinal]
    small_gather = _xchg_start(small_g, scatter=False, name="gather_small_start")

    out_g, out_d, out_m, out_v = {}, {}, {}, {}
    chain = {name: None for name in _SHARDED_NAMES}
    me_arr = jnp.reshape(me, (1,)).astype(jnp.int32)
    after = small_gather["token"]
    for l in reversed(range(DEPTH)):
        for grp, names in groups_of(l, scatter_groups):
            mine, landed = _xchg_wait(scatters[l][grp], after, name=f"scatter_wait_{grp}{l}")
            for name, own, got in zip(names, mine, landed):
                chain[name] = _adamw_layer(l, kform(name, a[name]), kform(name, a['m_' + name]),
                                           kform(name, a['v_' + name]), got, own, me_arr, chain[name],
                                           name=f"adamw_{name}{l}")
    for name in _SHARDED_NAMES:
        out_g[name], out_d[name], out_m[name], out_v[name] = [kform(name, t) for t in chain[name]]
    small_mine, small_landed = _xchg_wait(small_gather, chain[_SHARDED_NAMES[0]][0], name="gather_small_wait")
    row = lambda t: t.reshape(1, -1) if t.ndim == 1 else t
    res = _adamw_many([row(a[n]) for n in small_names], [row(a['m_' + n]) for n in small_names],
                      [row(a['v_' + n]) for n in small_names], small_landed, small_mine, me_arr, name="adamw_small")
    for i, n in enumerate(small_names):
        out_g[n], out_d[n], out_m[n], out_v[n] = [t.reshape(a[n].shape) for t in res[4 * i:4 * i + 4]]

    dmod_all = place_own(small_landed[:1], small_mine[:1])[0]
    dmod_mine = lax.dynamic_slice_in_dim(dmod_all, me * ncol, ncol, axis=2)
    g_ada = jnp.stack([_mm(c_act, dmod_mine[:, l], ta=True, name=f"dw_ada{l}") for l in range(DEPTH)])
    shp = ada_w.shape
    res = _adamw(*[t.reshape(-1, shp[-1]) for t in (ada_w, m_ada_w, v_ada_w)], g_ada.reshape(1, -1, shp[-1]),
                 name="adamw_ada_w")
    out_g['ada_w'], out_d['ada_w'], out_m['ada_w'], out_v['ada_w'] = [t.reshape(shp) for t in res]

    outs = [loss, dx[None]]
    for dct in (out_g, out_d, out_m, out_v):
        outs += [dct[n] for n in _WEIGHTS]
    return tuple(outs)
```

```python
import functools
import math

import jax
import jax.numpy as jnp
from jax import lax
from jax.experimental import pallas as pl
from jax.experimental.pallas import tpu as pltpu

F32 = jnp.float32
_MXU = jnp.bfloat16
_ACT = jnp.bfloat16
_HI = lax.Precision.HIGHEST
EPS = 1e-6
NDEV = 8
DEPTH = 4
D = 1024
LANE = 128
SUB = 8
VMEM_LIMIT = 56 * 1024 * 1024

SSD_INNER, SSD_STATE, SSD_HEADS, SSD_GROUPS, SSD_CHUNK, SSD_CONV = 512, 128, 8, 2, 128, 4
SSD_XBC = SSD_INNER + 2 * SSD_GROUPS * SSD_STATE
MLA_HEADS, MLA_NOPE, MLA_ROPE, MLA_V, MLA_QR, MLA_KVR = 4, 64, 32, 64, 256, 128
SWA_HEADS, SWA_KV, SWA_HD, WINDOW = 4, 2, 64, 128
D_FF, FFN_CONV = 2816, 3
D_IN = 2472
ROPE_THETA = 10000.0
C_XBC, C_Z, C_CQ, C_CKV, C_MISC, C_SQ, C_SK, C_SV, D_INP = 0, 1024, 1536, 1792, 1920, 2048, 2560, 2816, 3072
ROPE_LANE = 64
D_MIXP = 1536

ADAM_LR, ADAM_B1, ADAM_B2, ADAM_EPS, ADAM_WD, ADAM_STEP = 0.001, 0.9, 0.999, 1e-08, 0.01, 10

TS_ROW = 1024
TS_FFN = 256
TQ_ATT = 1024
TS_SWA = 512


def _tile(n, cap, q=LANE):
    best = None
    for t in range(q, min(n, cap) + 1, q):
        if n % t == 0:
            best = t
    return n if best is None else best


def _cp(ngrid):
    return pltpu.CompilerParams(dimension_semantics=("arbitrary",) * ngrid, vmem_limit_bytes=VMEM_LIMIT)


def _dot(a, b):
    return jnp.dot(a.astype(_MXU), b.astype(_MXU), preferred_element_type=F32)


def _dot_nt(a, b):
    return lax.dot_general(a.astype(_MXU), b.astype(_MXU), (((1,), (1,)), ((), ())), preferred_element_type=F32)


def _dot_tn(a, b):
    return jnp.dot(a.T.astype(_MXU), b.astype(_MXU), preferred_element_type=F32)


def _sigmoid(x):
    return 1.0 / (1.0 + jnp.exp(-x))


def _sigmoid_t(x):
    return 0.5 * jnp.tanh(0.5 * x) + 0.5


def _silu(x):
    return x * _sigmoid_t(x)


def _silu_grad(x):
    s = _sigmoid_t(x)
    return x * s, s * (1.0 + x * (1.0 - s))


def _dsilu(x):
    return _silu_grad(x)[1]


def _softplus(x):
    u = jnp.exp(-jnp.abs(x))
    w = 1.0 + u
    log1p = jnp.where(w == 1.0, u, jnp.log(w) * u / jnp.where(w == 1.0, 1.0, w - 1.0))
    return jnp.maximum(x, 0.0) + log1p


def _colsum(x):
    return jnp.sum(x, axis=0, keepdims=True)


def _rowsum(x):
    return jnp.sum(x, axis=1, keepdims=True)


def _shift_down(t, halo, j):
    if j == 0:
        return t
    n = t.shape[0]
    rolled = pltpu.roll(t, j, 0)
    row = lax.broadcasted_iota(jnp.int32, (SUB, t.shape[1]), 0)
    first = jnp.where(row < j, pltpu.roll(halo, j, 0), rolled[:SUB])
    return jnp.concatenate([first, rolled[SUB:]], axis=0) if n > SUB else first


def _shift_up(t, halo, j):
    if j == 0:
        return t
    n = t.shape[0]
    rolled = pltpu.roll(t, n - j, 0)
    row = lax.broadcasted_iota(jnp.int32, (SUB, t.shape[1]), 0)
    last = jnp.where(row >= SUB - j, pltpu.roll(halo, SUB - j, 0), rolled[n - SUB:])
    return jnp.concatenate([rolled[:n - SUB], last], axis=0) if n > SUB else last


def _mm(a, b, *, ta=False, tb=False, out_dtype=F32, name):
    if ta:
        K, M = a.shape
    else:
        M, K = a.shape
    if tb:
        N, K2 = b.shape
    else:
        K2, N = b.shape
    assert K == K2, (a.shape, b.shape, ta, tb)
    tk = _tile(K, 1536)
    nk = K // tk
    tm, tn = _tile(M, 2048 if nk == 1 else 1536), _tile(N, 1536 if nk == 1 else 1408)
    dn = (((0 if ta else 1,), (1 if tb else 0,)), ((), ()))

    def body(a_ref, b_ref, o_ref, *acc):
        part = lax.dot_general(a_ref[...].astype(_MXU), b_ref[...].astype(_MXU), dn, preferred_element_type=F32)
        if nk == 1:
            o_ref[...] = part.astype(out_dtype)
            return
        acc_ref, = acc
        k = pl.program_id(2)

        @pl.when(k == 0)
        def _():
            acc_ref[...] = part

        @pl.when(k > 0)
        def _():
            acc_ref[...] += part

        @pl.when(k == nk - 1)
        def _():
            o_ref[...] = acc_ref[...].astype(out_dtype)

    a_spec = pl.BlockSpec((tk, tm), lambda i, j, k: (k, i)) if ta else pl.BlockSpec((tm, tk), lambda i, j, k: (i, k))
    b_spec = pl.BlockSpec((tn, tk), lambda i, j, k: (j, k)) if tb else pl.BlockSpec((tk, tn), lambda i, j, k: (k, j))
    return pl.pallas_call(
        body, grid=(M // tm, N // tn, nk), in_specs=[a_spec, b_spec],
        out_specs=pl.BlockSpec((tm, tn), lambda i, j, k: (i, j)),
        out_shape=jax.ShapeDtypeStruct((M, N), out_dtype),
        scratch_shapes=[pltpu.VMEM((tm, tn), F32)] * (nk > 1), compiler_params=_cp(3), name=name)(a, b)


def _row(ts, w, col=0):
    return pl.BlockSpec((ts, w), lambda i: (i, col))


def _vec(w, r=1):
    return pl.BlockSpec((r, w), lambda i: (0, 0))


def _silu_call(x, name):
    def body(x_ref, o_ref):
        o_ref[...] = _silu(x_ref[...])
    return pl.pallas_call(body, out_shape=jax.ShapeDtypeStruct(x.shape, F32), name=name)(x)


def _norm_fwd(x, g, sc, sh, *, f=None, gate=None, name):
    S, dm = x.shape
    ts = _tile(S, TS_ROW, SUB)
    res = f is not None

    def body(*refs):
        if res:
            x_ref, f_ref, gate_ref, g_ref, sc_ref, sh_ref, xo_ref, h_ref = refs
            xv = x_ref[...] + gate_ref[...] * f_ref[...]
            xo_ref[...] = xv
        else:
            x_ref, g_ref, sc_ref, sh_ref, h_ref = refs
            xv = x_ref[...]
        rstd = lax.rsqrt(jnp.mean(xv * xv, axis=-1, keepdims=True) + EPS)
        h_ref[...] = ((xv * rstd) * g_ref[...] * (1.0 + sc_ref[...]) + sh_ref[...]).astype(_ACT)

    ins = [x] + ([f, gate] if res else []) + [g, sc, sh]
    in_specs = [_row(ts, dm)] + ([_row(ts, dm), _vec(dm)] if res else []) + [_vec(dm)] * 3
    h_shape = jax.ShapeDtypeStruct((S, dm), _ACT)
    if res:
        out_shape, out_specs = (jax.ShapeDtypeStruct((S, dm), F32), h_shape), (_row(ts, dm), _row(ts, dm))
    else:
        out_shape, out_specs = h_shape, _row(ts, dm)
    return pl.pallas_call(body, grid=(S // ts,), in_specs=in_specs, out_specs=out_specs, out_shape=out_shape,
                          compiler_params=_cp(1), name=name)(*ins)


def _norm_bwd(dh, x, dres, g, sc, *, branch=None, name):
    S, dm = x.shape
    ts = _tile(S, TS_ROW // 2, SUB)
    nb = 0 if branch is None else 2

    def body(*refs):
        dh_ref, x_ref, dres_ref, g_ref, sc_ref = refs[:5]
        dx_ref, dg_ref, dsc_ref, dsh_ref = refs[5 + nb:9 + nb]
        i = pl.program_id(0)
        xv = x_ref[...]
        dhv = dh_ref[...]
        rstd = lax.rsqrt(jnp.mean(xv * xv, axis=-1, keepdims=True) + EPS)
        xhat = xv * rstd
        hn = xhat * g_ref[...]
        dhn = dhv * (1.0 + sc_ref[...])
        dxh = dhn * g_ref[...]
        dx = dres_ref[...] + rstd * (dxh - xhat * jnp.mean(dxh * xhat, axis=-1, keepdims=True))
        dx_ref[...] = dx

        @pl.when(i == 0)
        def _():
            for r in refs[6 + nb:]:
                if r.shape[0] == 1:
                    r[...] = jnp.zeros_like(r)

        dg_ref[...] += _colsum(dhn * xhat)
        dsc_ref[...] += _colsum(dhv * hn)
        dsh_ref[...] += _colsum(dhv)
        if branch is not None:
            f_ref, gate_ref = refs[5:7]
            df_ref, dgate_ref = refs[9 + nb:]
            df_ref[...] = (gate_ref[...] * dx).astype(_ACT)
            dgate_ref[...] += _colsum(dx * f_ref[...])

    vshape = jax.ShapeDtypeStruct((1, dm), F32)
    extra_in = [] if branch is None else list(branch)
    return pl.pallas_call(
        body, grid=(S // ts,),
        in_specs=[_row(ts, dm)] * 3 + [_vec(dm)] * 2 + ([_row(ts, dm), _vec(dm)] if nb else []),
        out_specs=(_row(ts, dm), _vec(dm), _vec(dm), _vec(dm)) + ((_row(ts, dm), _vec(dm)) if nb else ()),
        out_shape=(jax.ShapeDtypeStruct((S, dm), F32), vshape, vshape, vshape)
        + ((jax.ShapeDtypeStruct((S, dm), _ACT), vshape) if nb else ()),
        compiler_params=_cp(1), name=name)(dh, x, dres, g, sc, *extra_in)


def _final_loss(x, f, gate, g, tgt, *, name):
    S, dm = x.shape
    ts = _tile(S, TS_ROW, SUB)

    def body(x_ref, f_ref, gate_ref, g_ref, t_ref, loss_ref, dx_ref, dg_ref, df_ref, dgate_ref):
        i = pl.program_id(0)
        fv = f_ref[...]
        xv = x_ref[...] + gate_ref[...] * fv
        rstd = lax.rsqrt(jnp.mean(xv * xv, axis=-1, keepdims=True) + EPS)
        xhat = xv * rstd
        err = xhat * g_ref[...] - t_ref[...]
        dy = err * (1.0 / dm)
        dxh = dy * g_ref[...]
        dx = rstd * (dxh - xhat * jnp.mean(dxh * xhat, axis=-1, keepdims=True))
        dx_ref[...] = dx
        df_ref[...] = (gate_ref[...] * dx).astype(_ACT)

        @pl.when(i == 0)
        def _():
            loss_ref[...] = jnp.zeros_like(loss_ref)
            dg_ref[...] = jnp.zeros_like(dg_ref)
            dgate_ref[...] = jnp.zeros_like(dgate_ref)

        loss_ref[...] += jnp.full((1, LANE), 0.5 * jnp.sum(jnp.mean(err * err, axis=-1, keepdims=True)), F32)
        dg_ref[...] += _colsum(dy * xhat)
        dgate_ref[...] += _colsum(dx * fv)

    return pl.pallas_call(
        body, grid=(S // ts,), in_specs=[_row(ts, dm), _row(ts, dm), _vec(dm), _vec(dm), _row(ts, dm)],
        out_specs=(_vec(LANE), _row(ts, dm), _vec(dm), _row(ts, dm), _vec(dm)),
        out_shape=(jax.ShapeDtypeStruct((1, LANE), F32), jax.ShapeDtypeStruct((S, dm), F32),
                   jax.ShapeDtypeStruct((1, dm), F32), jax.ShapeDtypeStruct((S, dm), _ACT),
                   jax.ShapeDtypeStruct((1, dm), F32)),
        compiler_params=_cp(1), name=name)(x, f, gate, g, tgt)


def _ffn_conv(t, halo, cw_ref, cb_ref):
    t1, t2 = _shift_down(t, halo, 1), _shift_down(t, halo, 2)
    return ((cb_ref[...] + t2 * cw_ref[0:1, :]) + t1 * cw_ref[1:2, :]) + t * cw_ref[2:3, :], t1, t2


def _prev_halo_spec(ts, w, col=0):
    return pl.BlockSpec((SUB, w), lambda i: (jnp.maximum(i * (ts // SUB) - 1, 0), col))


def _ffn_act_fwd(up, cw, cb, *, name):
    S, w2 = up.shape
    ff = w2 // 2
    ts = _tile(S, TS_FFN, SUB)

    def body(up_ref, halo_ref, cw_ref, cb_ref, act_ref):
        i = pl.program_id(0)
        t = up_ref[...]
        halo = jnp.where(i > 0, halo_ref[...], 0.0)
        u, _, _ = _ffn_conv(t, halo, cw_ref, cb_ref)
        act_ref[...] = (_silu(u[:, :ff]) * u[:, ff:]).astype(_ACT)

    return pl.pallas_call(
        body, grid=(S // ts,), in_specs=[_row(ts, w2), _prev_halo_spec(ts, w2), _vec(w2, FFN_CONV), _vec(w2)],
        out_specs=_row(ts, ff), out_shape=jax.ShapeDtypeStruct((S, ff), _ACT),
        compiler_params=_cp(1), name=name)(up, up, cw, cb)


def _ffn_bwd(up, dact, cw, cb, *, name):
    S, w2 = up.shape
    ff = w2 // 2
    ts = _tile(S, TS_FFN // 2, SUB)
    n = S // ts

    def body(up_ref, halo_ref, dact_ref, cw_ref, cb_ref, dup_ref, dcw_ref, dcb_ref, carry_ref):
        i = pl.program_id(0)
        t_idx = n - 1 - i

        @pl.when(i == 0)
        def _():
            carry_ref[...] = jnp.zeros_like(carry_ref)
            dcw_ref[...] = jnp.zeros_like(dcw_ref)
            dcb_ref[...] = jnp.zeros_like(dcb_ref)

        t = up_ref[...]
        halo = jnp.where(t_idx > 0, halo_ref[...], 0.0)
        u, t1, t2 = _ffn_conv(t, halo, cw_ref, cb_ref)
        a, b = u[:, :ff], u[:, ff:]
        da = dact_ref[...]
        sa, dsa = _silu_grad(a)
        dv = jnp.concatenate([da * b * dsa, da * sa], axis=1)
        nxt = carry_ref[...]
        dup = (dv * cw_ref[2:3, :] + _shift_up(dv, nxt, 1) * cw_ref[1:2, :]) + _shift_up(dv, nxt, 2) * cw_ref[0:1, :]
        dup_ref[...] = dup.astype(_ACT)
        dcb_ref[...] += _colsum(dv)
        dcw_ref[2:3, :] += _colsum(dv * t)
        dcw_ref[1:2, :] += _colsum(dv * t1)
        dcw_ref[0:1, :] += _colsum(dv * t2)
        carry_ref[...] = dv[:SUB]

    rev = lambda w: pl.BlockSpec((ts, w), lambda i: (n - 1 - i, 0))
    halo_spec = pl.BlockSpec((SUB, w2), lambda i: (jnp.maximum((n - 1 - i) * (ts // SUB) - 1, 0), 0))
    return pl.pallas_call(
        body, grid=(n,), in_specs=[rev(w2), halo_spec, rev(ff), _vec(w2, FFN_CONV), _vec(w2)],
        out_specs=(rev(w2), _vec(w2, FFN_CONV), _vec(w2)),
        out_shape=(jax.ShapeDtypeStruct((S, w2), _ACT), jax.ShapeDtypeStruct((FFN_CONV, w2), F32),
                   jax.ShapeDtypeStruct((1, w2), F32)),
        scratch_shapes=[pltpu.VMEM((SUB, w2), F32)], compiler_params=_cp(1), name=name)(up, up, dact, cw, cb)


def _ssd_core(pre, halo, misc, cw_ref, cb_ref, dtb, alog):
    q = pre.shape[0]
    conv = cb_ref[...]
    for k in range(SSD_CONV):
        conv = conv + _shift_down(pre, halo, SSD_CONV - 1 - k) * cw_ref[k:k + 1, :]
    xbc = _silu(conv)
    raw = misc + dtb
    dt = _softplus(raw)
    a = -jnp.exp(alog)
    r = lax.broadcasted_iota(jnp.int32, (q, q), 0)
    c = lax.broadcasted_iota(jnp.int32, (q, q), 1)
    tri = r >= c
    acum = jnp.dot(tri.astype(F32), dt * a, precision=_HI, preferred_element_type=F32)
    return conv, xbc, raw, dt, a, acum, acum.T, tri


def _sel(v, j, lo):
    return jnp.where(lo, v[:, 2 * j:2 * j + 1], v[:, 2 * j + 1:2 * j + 2])


def _ssd_pair_fwd(xbc, dt, acum, acum_t, tri, dsk, g_mat, b_mat, c_mat, h_pair, j, lo, lo1, sub_lo):
    q = xbc.shape[0]
    x = xbc[:, LANE * j:LANE * (j + 1)]
    dtp = _sel(dt, j, lo)
    ap = _sel(acum, j, lo)
    xd = x * dtp
    ls, ms = [], []
    for h in (2 * j, 2 * j + 1):
        seg = acum[:, h:h + 1] - acum_t[h:h + 1, :]
        l_mat = jnp.exp(jnp.where(tri, seg, -jnp.inf))
        ls.append(l_mat)
        ms.append(g_mat * l_mat)
    yd = jnp.where(lo, _dot(ms[0], xd), _dot(ms[1], xd))
    ea = jnp.exp(ap)
    yo = _dot_nt(c_mat, h_pair) * ea
    dp = _sel(dsk, j, lo1)
    alast = acum[q - 1:q, :]
    e = jnp.exp(_sel(alast, j, lo1) - ap)
    cd = jnp.where(sub_lo, jnp.exp(alast[:, 2 * j:2 * j + 1]), jnp.exp(alast[:, 2 * j + 1:2 * j + 2]))
    return dict(x=x, dtp=dtp, ap=ap, xd=xd, ls=ls, ms=ms, ea=ea, yo=yo, dp=dp, e=e, cd=cd, y=yd + yo + x * dp)


def _gnorm(yg):
    half = SSD_INNER // SSD_GROUPS
    rstds, yns = [], []
    for g in range(SSD_GROUPS):
        part = yg[:, half * g:half * (g + 1)]
        rstd = lax.rsqrt(jnp.mean(part * part, axis=-1, keepdims=True) + EPS)
        rstds.append(rstd)
        yns.append(part * rstd)
    return rstds, yns


def _ssd_specs(nc, rev):
    q = SSD_CHUNK
    cidx = (lambda i: nc - 1 - i) if rev else (lambda i: i)
    return [
        pl.BlockSpec((q, SSD_XBC), lambda i: (cidx(i), C_XBC // SSD_XBC)),
        pl.BlockSpec((SUB, SSD_XBC), lambda i: (jnp.maximum(cidx(i) * (q // SUB) - 1, 0), C_XBC // SSD_XBC)),
        pl.BlockSpec((q, SSD_INNER), lambda i: (cidx(i), C_Z // SSD_INNER)),
        pl.BlockSpec((q, LANE), lambda i: (cidx(i), C_MISC // LANE)),
    ]


def _ssd_param_specs():
    return [_vec(SSD_XBC, SSD_CONV), _vec(SSD_XBC), _vec(LANE), _vec(LANE), _vec(LANE), _vec(SSD_INNER)]


def _ssd_fwd(proj, cw, cb, dtb, alog, dsk, ng, *, name):
    S = proj.shape[0]
    q = SSD_CHUNK
    nc = S // q
    npair = SSD_HEADS // 2

    def body(xbc_ref, halo_ref, z_ref, misc_ref, cw_ref, cb_ref, dtb_ref, alog_ref, dsk_ref, ng_ref,
             y_ref, hin_ref, h_ref):
        c = pl.program_id(0)

        @pl.when(c == 0)
        def _():
            h_ref[...] = jnp.zeros_like(h_ref)

        pre = xbc_ref[...]
        halo = jnp.where(c > 0, halo_ref[...], 0.0)
        conv, xbc, raw, dt, a, acum, acum_t, tri = _ssd_core(pre, halo, misc_ref[...], cw_ref, cb_ref,
                                                             dtb_ref[...], alog_ref[...])
        lo = lax.broadcasted_iota(jnp.int32, (q, LANE), 1) < LANE // 2
        lo1 = lo[:1]
        sub_lo = lax.broadcasted_iota(jnp.int32, (LANE, LANE), 0) < LANE // 2
        ys = []
        for g in range(SSD_GROUPS):
            b_mat = xbc[:, SSD_INNER + SSD_STATE * g:SSD_INNER + SSD_STATE * (g + 1)]
            c_mat = xbc[:, SSD_INNER + SSD_STATE * (SSD_GROUPS + g):SSD_INNER + SSD_STATE * (SSD_GROUPS + g + 1)]
            g_mat = _dot_nt(c_mat, b_mat)
            for jj in range(npair // SSD_GROUPS):
                j = g * (npair // SSD_GROUPS) + jj
                hj = h_ref[j]
                p = _ssd_pair_fwd(xbc, dt, acum, acum_t, tri, dsk_ref[...], g_mat, b_mat, c_mat, hj, j, lo, lo1, sub_lo)
                ys.append(p["y"])
                hin_ref[0, j] = hj
                h_ref[j] = p["cd"] * hj + _dot_tn(p["xd"] * p["e"], b_mat)
        yg = jnp.concatenate(ys, axis=1) * _silu(z_ref[...])
        _, yns = _gnorm(yg)
        y_ref[...] = jnp.concatenate(yns, axis=1) * ng_ref[...]

    return pl.pallas_call(
        body, grid=(nc,), in_specs=_ssd_specs(nc, False) + _ssd_param_specs(),
        out_specs=(pl.BlockSpec((q, SSD_INNER), lambda i: (i, 0)),
                   pl.BlockSpec((1, npair, LANE, LANE), lambda i: (i, 0, 0, 0))),
        out_shape=(jax.ShapeDtypeStruct((S, SSD_INNER), F32), jax.ShapeDtypeStruct((nc, npair, LANE, LANE), F32)),
        scratch_shapes=[pltpu.VMEM((npair, LANE, LANE), F32)], compiler_params=_cp(1), name=name,
    )(proj, proj, proj, proj, cw, cb, dtb, alog, dsk, ng)


def _ssd_bwd(proj, dycat, hin, cw, cb, dtb, alog, dsk, ng, *, name):
    S = proj.shape[0]
    q = SSD_CHUNK
    nc = S // q
    npair = SSD_HEADS // 2
    ppg = npair // SSD_GROUPS

    def body(xbc_ref, halo_ref, z_ref, misc_ref, dy_ref, hin_ref, cw_ref, cb_ref, dtb_ref, alog_ref, dsk_ref, ng_ref,
             dpre_ref, dz_ref, dmisc_ref, dcw_ref, dcb_ref, ddtb_ref, dalog_ref, ddsk_ref, dng_ref,
             dh_ref, carry_ref):
        i = pl.program_id(0)
        c = nc - 1 - i

        @pl.when(i == 0)
        def _():
            dh_ref[...] = jnp.zeros_like(dh_ref)
            carry_ref[...] = jnp.zeros_like(carry_ref)
            for r in (dcw_ref, dcb_ref, ddtb_ref, dalog_ref, ddsk_ref, dng_ref):
                r[...] = jnp.zeros_like(r)

        pre = xbc_ref[...]
        halo = jnp.where(c > 0, halo_ref[...], 0.0)
        conv, xbc, raw, dt, a, acum, acum_t, tri = _ssd_core(pre, halo, misc_ref[...], cw_ref, cb_ref,
                                                             dtb_ref[...], alog_ref[...])
        lane = lax.broadcasted_iota(jnp.int32, (q, LANE), 1)
        lane1 = lane[:1]
        rowi = lax.broadcasted_iota(jnp.int32, (q, LANE), 0)
        lastrow = rowi == q - 1
        lo = lane < LANE // 2
        lo1 = lo[:1]
        sub_lo = lax.broadcasted_iota(jnp.int32, (LANE, LANE), 0) < LANE // 2
        dsk = dsk_ref[...]
        alast = acum[q - 1:q, :]

        def halves(t):
            return _rowsum(jnp.where(lo, t, 0.0)), _rowsum(jnp.where(lo, 0.0, t))

        def put(ha, va, vb):
            ln = lane if va.shape[0] == q else lane1
            return jnp.where(ln == ha, va, 0.0) + jnp.where(ln == ha + 1, vb, 0.0)

        mats, pairs = [], []
        for g in range(SSD_GROUPS):
            b_mat = xbc[:, SSD_INNER + SSD_STATE * g:SSD_INNER + SSD_STATE * (g + 1)]
            c_mat = xbc[:, SSD_INNER + SSD_STATE * (SSD_GROUPS + g):SSD_INNER + SSD_STATE * (SSD_GROUPS + g + 1)]
            g_mat = _dot_nt(c_mat, b_mat)
            mats.append((b_mat, c_mat, g_mat))
            for jj in range(ppg):
                j = g * ppg + jj
                pairs.append(_ssd_pair_fwd(xbc, dt, acum, acum_t, tri, dsk, g_mat, b_mat, c_mat, hin_ref[0, j],
                                           j, lo, lo1, sub_lo))
        z = z_ref[...]
        sz, dsz = _silu_grad(z)
        y = jnp.concatenate([p["y"] for p in pairs], axis=1)
        rstds, yns = _gnorm(y * sz)
        dout = dy_ref[...]
        dng_ref[...] += _colsum(dout * jnp.concatenate(yns, axis=1))
        dyn = dout * ng_ref[...]
        half = SSD_INNER // SSD_GROUPS
        dygs = []
        for g in range(SSD_GROUPS):
            dyn_g = dyn[:, half * g:half * (g + 1)]
            dygs.append(rstds[g] * (dyn_g - yns[g] * jnp.mean(dyn_g * yns[g], axis=-1, keepdims=True)))
        dyg = jnp.concatenate(dygs, axis=1)
        dyv = dyg * sz
        dz_ref[...] = (dyg * y * dsz).astype(_ACT)

        da_acc = jnp.zeros((q, LANE), F32)
        ddt = jnp.zeros((q, LANE), F32)
        dds = jnp.zeros((1, LANE), F32)
        dxs, dbs, dcs = [], [], []
        for g in range(SSD_GROUPS):
            b_mat, c_mat, g_mat = mats[g]
            dg_mat = jnp.zeros((q, q), F32)
            db = jnp.zeros((q, SSD_STATE), F32)
            dc = jnp.zeros((q, SSD_STATE), F32)
            for jj in range(ppg):
                j = g * ppg + jj
                ha = 2 * j
                p = pairs[j]
                hj = hin_ref[0, j]
                dyp = dyv[:, LANE * j:LANE * (j + 1)]
                dsum = _colsum(dyp * p["x"])
                dds = dds + put(ha, _rowsum(jnp.where(lo1, dsum, 0.0)), _rowsum(jnp.where(lo1, 0.0, dsum)))
                dx = dyp * p["dp"]
                dw = dyp * p["ea"]
                dc = dc + _dot(dw, hj)
                dh_yo = _dot_tn(dw, c_mat)
                ra, rb = halves(dyp * p["yo"])
                da_acc = da_acc + put(ha, ra, rb)
                dxd = jnp.zeros((q, LANE), F32)
                for idx in range(2):
                    dyh = jnp.where(lo, dyp, 0.0) if idx == 0 else jnp.where(lo, 0.0, dyp)
                    dm = _dot_nt(dyh, p["xd"])
                    dxd = dxd + _dot_tn(p["ms"][idx], dyh)
                    dg_mat = dg_mat + dm * p["ls"][idx]
                    t = dm * p["ms"][idx]
                    da_h = _rowsum(t) - _rowsum(t.T)
                    da_acc = da_acc + jnp.where(lane == ha + idx, da_h, 0.0)
                dhn = dh_ref[j]
                s = _rowsum(dhn * hj)
                sa = jnp.sum(jnp.where(sub_lo[:, :1], s, 0.0), keepdims=True)
                sb = jnp.sum(jnp.where(sub_lo[:, :1], 0.0, s), keepdims=True)
                cda, cdb = jnp.exp(alast[:, ha:ha + 1]), jnp.exp(alast[:, ha + 1:ha + 2])
                db = db + _dot(p["xd"] * p["e"], dhn)
                r = _dot_nt(b_mat, dhn)
                dxd = dxd + r * p["e"]
                qa, qb = halves(r * p["xd"] * p["e"])
                da_acc = da_acc - put(ha, qa, qb)
                tot_a = sa * cda + jnp.sum(qa, keepdims=True)
                tot_b = sb * cdb + jnp.sum(qb, keepdims=True)
                da_acc = da_acc + jnp.where(lastrow, put(ha, tot_a, tot_b), 0.0)
                dh_ref[j] = p["cd"] * dhn + dh_yo
                dx = dx + dxd * p["dtp"]
                ua, ub = halves(dxd * p["x"])
                ddt = ddt + put(ha, ua, ub)
                dxs.append(dx)
            dc = dc + _dot(dg_mat, b_mat)
            db = db + _dot_tn(dg_mat, c_mat)
            dbs.append(db)
            dcs.append(dc)
        r2 = lax.broadcasted_iota(jnp.int32, (q, q), 0)
        c2 = lax.broadcasted_iota(jnp.int32, (q, q), 1)
        dda = jnp.dot((c2 >= r2).astype(F32), da_acc, precision=_HI, preferred_element_type=F32)
        ddt = ddt + dda * a
        dalog_ref[...] += _colsum(dda * dt) * a
        ddsk_ref[...] += dds
        draw = jnp.where(lane < SSD_HEADS, ddt * _sigmoid(raw), 0.0)
        ddtb_ref[...] += _colsum(draw)
        dmisc_ref[...] = draw
        dconv = jnp.concatenate(dxs + dbs + dcs, axis=1) * _dsilu(conv)
        dcb_ref[...] += _colsum(dconv)
        nxt = carry_ref[...]
        dpre = jnp.zeros_like(dconv)
        for k in range(SSD_CONV):
            dcw_ref[k:k + 1, :] += _colsum(dconv * _shift_down(pre, halo, SSD_CONV - 1 - k))
            dpre = dpre + _shift_up(dconv, nxt, SSD_CONV - 1 - k) * cw_ref[k:k + 1, :]
        dpre_ref[...] = dpre.astype(_ACT)
        carry_ref[...] = dconv[:SUB]

    rev = lambda i: (nc - 1 - i, 0)
    vshape = lambda w, r=1: jax.ShapeDtypeStruct((r, w), F32)
    return pl.pallas_call(
        body, grid=(nc,),
        in_specs=_ssd_specs(nc, True) + [pl.BlockSpec((q, SSD_INNER), rev),
                                         pl.BlockSpec((1, npair, LANE, LANE), lambda i: (nc - 1 - i, 0, 0, 0))]
        + _ssd_param_specs(),
        out_specs=(pl.BlockSpec((q, SSD_XBC), rev), pl.BlockSpec((q, SSD_INNER), rev), pl.BlockSpec((q, LANE), rev),
                   _vec(SSD_XBC, SSD_CONV), _vec(SSD_XBC), _vec(LANE), _vec(LANE), _vec(LANE), _vec(SSD_INNER)),
        out_shape=(jax.ShapeDtypeStruct((S, SSD_XBC), _ACT), jax.ShapeDtypeStruct((S, SSD_INNER), _ACT),
                   jax.ShapeDtypeStruct((S, LANE), F32),
                   vshape(SSD_XBC, SSD_CONV), vshape(SSD_XBC), vshape(LANE), vshape(LANE), vshape(LANE),
                   vshape(SSD_INNER)),
        scratch_shapes=[pltpu.VMEM((npair, LANE, LANE), F32), pltpu.VMEM((SUB, SSD_XBC), F32)],
        compiler_params=_cp(1), name=name,
    )(proj, proj, proj, proj, dycat, hin, cw, cb, dtb, alog, dsk, ng)


def _rope(x, cosf, sina, sinb):
    return x * cosf + pltpu.roll(x, LANE - MLA_ROPE // 2, 1) * sina + pltpu.roll(x, MLA_ROPE // 2, 1) * sinb


def _rope_t(dy, cosf, sina, sinb):
    return dy * cosf + pltpu.roll(dy * sina, MLA_ROPE // 2, 1) + pltpu.roll(dy * sinb, LANE - MLA_ROPE // 2, 1)


def _rope_lanes(shape):
    lane = lax.broadcasted_iota(jnp.int32, shape, 1)
    return (lane >= ROPE_LANE) & (lane < ROPE_LANE + MLA_ROPE)


def _mla_prep_fwd(proj, cosf, sina, sinb, gq, gkv, wuq, wukv, *, name):
    S = proj.shape[0]
    ts = _tile(S, TS_ROW, SUB)
    hw = MLA_HEADS * LANE

    def body(cq_ref, ckv_ref, misc_ref, cos_ref, sa_ref, sb_ref, gq_ref, gkv_ref, wuq_ref, wukv_ref,
             q_ref, k_ref, v_ref, vt_ref):
        cosv, sav, sbv = cos_ref[...], sa_ref[...], sb_ref[...]
        cq = cq_ref[...]
        qn = cq * lax.rsqrt(jnp.mean(cq * cq, axis=-1, keepdims=True) + EPS) * gq_ref[...]
        qh = _dot(qn, wuq_ref[...])
        ckv = ckv_ref[...]
        kvn = ckv * lax.rsqrt(jnp.mean(ckv * ckv, axis=-1, keepdims=True) + EPS) * gkv_ref[...]
        kv = _dot(kvn, wukv_ref[...])
        kr = _rope(jnp.where(_rope_lanes((ts, LANE)), misc_ref[...], 0.0), cosv, sav, sbv)
        for h in range(MLA_HEADS):
            sl = slice(LANE * h, LANE * (h + 1))
            q_ref[:, sl] = (_rope(qh[:, sl], cosv, sav, sbv) * _Q_SCALE).astype(_ACT)
            k_ref[:, sl] = (kv[:, sl] + kr).astype(_ACT)
        v_ref[...] = kv[:, hw:].astype(_ACT)
        vt_ref[...] = kv[:, hw:].T.astype(_ACT)

    oshape = jax.ShapeDtypeStruct((S, hw), _ACT)
    return pl.pallas_call(
        body, grid=(S // ts,),
        in_specs=[_row(ts, MLA_QR, C_CQ // MLA_QR), _row(ts, MLA_KVR, C_CKV // MLA_KVR), _row(ts, LANE, C_MISC // LANE),
                  _row(ts, LANE), _row(ts, LANE), _row(ts, LANE), _vec(MLA_QR), _vec(MLA_KVR),
                  _vec(hw, MLA_QR), _vec(2 * hw, MLA_KVR)],
        out_specs=(_row(ts, hw),) * 3 + (pl.BlockSpec((hw, ts), lambda i: (0, i)),),
        out_shape=(oshape,) * 3 + (jax.ShapeDtypeStruct((hw, S), _ACT),), compiler_params=_cp(1), name=name,
    )(proj, proj, proj, cosf, sina, sinb, gq, gkv, wuq, wukv)


def _mla_prep_bwd(proj, dq, dk, dv, dmisc_ssd, cosf, sina, sinb, gq, gkv, wuq, wukv, *, name):
    S = proj.shape[0]
    ts = _tile(S, TS_ROW, SUB)
    hw = MLA_HEADS * LANE

    def body(cq_ref, ckv_ref, dq_ref, dk_ref, dv_ref, dms_ref, cos_ref, sa_ref, sb_ref, gq_ref, gkv_ref,
             wuq_ref, wukv_ref, dcq_ref, dckv_ref, dmisc_ref, dqh_ref, dkv_ref, qn_ref, kvn_ref, dgq_ref, dgkv_ref):
        i = pl.program_id(0)
        cosv, sav, sbv = cos_ref[...], sa_ref[...], sb_ref[...]

        @pl.when(i == 0)
        def _():
            dgq_ref[...] = jnp.zeros_like(dgq_ref)
            dgkv_ref[...] = jnp.zeros_like(dgkv_ref)

        dqh = jnp.concatenate([_rope_t(dq_ref[:, LANE * h:LANE * (h + 1)], cosv, sav, sbv)
                               for h in range(MLA_HEADS)], axis=1)
        dqh_ref[...] = dqh.astype(_ACT)
        dkv = jnp.concatenate([dk_ref[...], dv_ref[...]], axis=1)
        dkv_ref[...] = dkv.astype(_ACT)

        def norm_bwd(x, g, dn, dg_ref, n_ref):
            rstd = lax.rsqrt(jnp.mean(x * x, axis=-1, keepdims=True) + EPS)
            xhat = x * rstd
            n_ref[...] = (xhat * g).astype(_ACT)
            dg_ref[...] += _colsum(dn * xhat)
            dxh = dn * g
            return rstd * (dxh - xhat * jnp.mean(dxh * xhat, axis=-1, keepdims=True))

        dcq_ref[...] = norm_bwd(cq_ref[...], gq_ref[...], _dot_nt(dqh, wuq_ref[...]), dgq_ref, qn_ref).astype(_ACT)
        dckv_ref[...] = norm_bwd(ckv_ref[...], gkv_ref[...], _dot_nt(dkv, wukv_ref[...]), dgkv_ref, kvn_ref).astype(_ACT)
        dks = dk_ref[:, 0:LANE]
        for h in range(1, MLA_HEADS):
            dks = dks + dk_ref[:, LANE * h:LANE * (h + 1)]
        rl = _rope_lanes((ts, LANE))
        dkr = _rope_t(jnp.where(rl, dks, 0.0), cosv, sav, sbv)
        dmisc_ref[...] = (dms_ref[...] + jnp.where(rl, dkr, 0.0)).astype(_ACT)

    act = lambda w: jax.ShapeDtypeStruct((S, w), _ACT)
    return pl.pallas_call(
        body, grid=(S // ts,),
        in_specs=[_row(ts, MLA_QR, C_CQ // MLA_QR), _row(ts, MLA_KVR, C_CKV // MLA_KVR),
                  _row(ts, hw), _row(ts, hw), _row(ts, hw), _row(ts, LANE),
                  _row(ts, LANE), _row(ts, LANE), _row(ts, LANE), _vec(MLA_QR), _vec(MLA_KVR),
                  _vec(hw, MLA_QR), _vec(2 * hw, MLA_KVR)],
        out_specs=(_row(ts, MLA_QR), _row(ts, MLA_KVR), _row(ts, LANE), _row(ts, hw), _row(ts, 2 * hw),
                   _row(ts, MLA_QR), _row(ts, MLA_KVR), _vec(MLA_QR), _vec(MLA_KVR)),
        out_shape=(act(MLA_QR), act(MLA_KVR), act(LANE), act(hw), act(2 * hw), act(MLA_QR), act(MLA_KVR),
                   jax.ShapeDtypeStruct((1, MLA_QR), F32), jax.ShapeDtypeStruct((1, MLA_KVR), F32)),
        compiler_params=_cp(1), name=name,
    )(proj, proj, dq, dk, dv, dmisc_ssd, cosf, sina, sinb, gq, gkv, wuq, wukv)


_MLA_SCALE = 1.0 / math.sqrt(MLA_NOPE + MLA_ROPE)
_LOG2E = 1.4426950408889634
_Q_SCALE = _MLA_SCALE * _LOG2E
ATT_CHUNK = 1024


def _tri_grid(nq, by_key):
    if by_key:
        pairs = [(i, j) for j in range(nq) for i in range(j, nq)]
    else:
        pairs = [(i, j) for i in range(nq) for j in range(i + 1)]
    return jnp.asarray([p[0] for p in pairs], jnp.int32), jnp.asarray([p[1] for p in pairs], jnp.int32)


def _attn_fwd(q, k, vt, *, name):
    S = q.shape[0]
    tq = _tile(S, TQ_ATT)
    nq = S // tq
    itab, jtab = _tri_grid(nq, False)

    def body(it_ref, jt_ref, q_ref, k_ref, vt_ref, o_ref, lse_ref, lset_ref, m_ref, l_ref, acc_ref):
        t = pl.program_id(1)
        i, j = it_ref[t], jt_ref[t]

        @pl.when(j == 0)
        def _():
            m_ref[...] = jnp.full_like(m_ref, -jnp.inf)
            l_ref[...] = jnp.zeros_like(l_ref)
            acc_ref[...] = jnp.zeros_like(acc_ref)

        def step(diagonal):
            s = _dot_nt(k_ref[...], q_ref[...])
            if diagonal:
                kk = lax.broadcasted_iota(jnp.int32, (tq, tq), 0)
                s = jnp.where(kk <= lax.broadcasted_iota(jnp.int32, (tq, tq), 1), s, -jnp.inf)
            m_prev = m_ref[...]
            m_new = jnp.maximum(m_prev, jnp.max(s, axis=0, keepdims=True))
            p = jnp.exp2(s - m_new)
            alpha = jnp.exp2(m_prev - m_new)
            l_ref[...] = alpha * l_ref[...] + _colsum(p)
            acc_ref[...] = alpha * acc_ref[...] + _dot(vt_ref[...], p)
            m_ref[...] = m_new

        pl.when(j < i)(functools.partial(step, False))
        pl.when(j == i)(functools.partial(step, True))

        @pl.when(j == i)
        def _():
            o_ref[...] = (acc_ref[...] / l_ref[...]).T
            lse = m_ref[...] + jnp.log2(l_ref[...])
            lset_ref[...] = jnp.broadcast_to(lse, (SUB, tq))
            lse_ref[...] = jnp.broadcast_to(lse, (LANE, tq)).T

    qspec = pl.BlockSpec((tq, LANE), lambda h, t, it, jt: (it[t], h))
    kspec = pl.BlockSpec((tq, LANE), lambda h, t, it, jt: (jt[t], h))
    vtspec = pl.BlockSpec((LANE, tq), lambda h, t, it, jt: (h, jt[t]))
    oshape = jax.ShapeDtypeStruct((S, MLA_HEADS * LANE), F32)
    return pl.pallas_call(
        body,
        grid_spec=pltpu.PrefetchScalarGridSpec(
            num_scalar_prefetch=2, grid=(MLA_HEADS, itab.shape[0]), in_specs=[qspec, kspec, vtspec],
            out_specs=(qspec, qspec, pl.BlockSpec((SUB, tq), lambda h, t, it, jt: (h, it[t]))),
            scratch_shapes=[pltpu.VMEM((1, tq), F32), pltpu.VMEM((1, tq), F32), pltpu.VMEM((LANE, tq), F32)]),
        out_shape=(oshape, oshape, jax.ShapeDtypeStruct((MLA_HEADS * SUB, S), F32)),
        compiler_params=_cp(2), name=name)(itab, jtab, q, k, vt)


def _attn_bwd_dq(q, k, v, o, lse, dycat, *, name):
    S = q.shape[0]
    tq = _tile(S, TQ_ATT)
    nq = S // tq
    rc = min(ATT_CHUNK, tq)
    itab, jtab = _tri_grid(nq, False)

    def body(it_ref, jt_ref, q_ref, k_ref, v_ref, o_ref, lse_ref, do_ref, dq_ref, acc_ref):
        t = pl.program_id(1)
        i, j = it_ref[t], jt_ref[t]

        @pl.when(j == 0)
        def _():
            acc_ref[...] = jnp.zeros_like(acc_ref)

        def step(diagonal):
            kv, vv = k_ref[...], v_ref[...]
            for r in range(tq // rc):
                rows = slice(r * rc, (r + 1) * rc)
                s = _dot_nt(q_ref[rows, :], kv)
                if diagonal:
                    rr = r * rc + lax.broadcasted_iota(jnp.int32, (rc, tq), 0)
                    s = jnp.where(lax.broadcasted_iota(jnp.int32, (rc, tq), 1) <= rr, s, -jnp.inf)
                p = jnp.exp2(s - lse_ref[rows, 0:1])
                dov = do_ref[rows, :]
                delta = _rowsum(dov * o_ref[rows, :])
                ds = p * (_dot_nt(dov, vv) - delta)
                acc_ref[rows, :] += _dot(ds, kv)

        pl.when(j < i)(functools.partial(step, False))
        pl.when(j == i)(functools.partial(step, True))

        @pl.when(j == i)
        def _():
            dq_ref[...] = acc_ref[...] * _MLA_SCALE

    qspec = pl.BlockSpec((tq, LANE), lambda h, t, it, jt: (it[t], h))
    kspec = pl.BlockSpec((tq, LANE), lambda h, t, it, jt: (jt[t], h))
    dospec = pl.BlockSpec((tq, LANE), lambda h, t, it, jt: (it[t], SSD_INNER // LANE + h))
    return pl.pallas_call(
        body,
        grid_spec=pltpu.PrefetchScalarGridSpec(
            num_scalar_prefetch=2, grid=(MLA_HEADS, itab.shape[0]),
            in_specs=[qspec, kspec, kspec, qspec, qspec, dospec], out_specs=qspec,
            scratch_shapes=[pltpu.VMEM((tq, LANE), F32)]),
        out_shape=jax.ShapeDtypeStruct((S, MLA_HEADS * LANE), F32),
        compiler_params=_cp(2), name=name)(itab, jtab, q, k, v, o, lse, dycat)


def _attn_bwd_dkv(q, k, v, o, lset, dycat, *, name):
    S = q.shape[0]
    tq = _tile(S, TQ_ATT)
    nq = S // tq
    kc = min(ATT_CHUNK, tq)
    itab, jtab = _tri_grid(nq, True)

    def body(it_ref, jt_ref, q_ref, k_ref, v_ref, o_ref, lset_ref, do_ref, dk_ref, dv_ref, dk_acc, dv_acc):
        t = pl.program_id(1)
        i, j = it_ref[t], jt_ref[t]

        @pl.when(i == j)
        def _():
            dk_acc[...] = jnp.zeros_like(dk_acc)
            dv_acc[...] = jnp.zeros_like(dv_acc)

        def step(diagonal):
            qv, dov = q_ref[...], do_ref[...]
            delta = lax.dot_general(jnp.ones((SUB, LANE), F32), dov * o_ref[...], (((1,), (1,)), ((), ())),
                                    precision=_HI, preferred_element_type=F32)[0:1]
            lse = lset_ref[0:1, :]
            for c in range(tq // kc):
                rows = slice(c * kc, (c + 1) * kc)
                s = _dot_nt(k_ref[rows, :], qv)
                if diagonal:
                    kk = c * kc + lax.broadcasted_iota(jnp.int32, (kc, tq), 0)
                    s = jnp.where(kk <= lax.broadcasted_iota(jnp.int32, (kc, tq), 1), s, -jnp.inf)
                p = jnp.exp2(s - lse)
                dv_acc[rows, :] += _dot(p, dov)
                ds = p * (_dot_nt(v_ref[rows, :], dov) - delta)
                dk_acc[rows, :] += _dot(ds, qv)

        pl.when(i > j)(functools.partial(step, False))
        pl.when(i == j)(functools.partial(step, True))

        @pl.when(i == nq - 1)
        def _():
            dk_ref[...] = dk_acc[...] * (1.0 / _LOG2E)
            dv_ref[...] = dv_acc[...]

    qspec = pl.BlockSpec((tq, LANE), lambda h, t, it, jt: (it[t], h))
    kspec = pl.BlockSpec((tq, LANE), lambda h, t, it, jt: (jt[t], h))
    dospec = pl.BlockSpec((tq, LANE), lambda h, t, it, jt: (it[t], SSD_INNER // LANE + h))
    lspec = pl.BlockSpec((SUB, tq), lambda h, t, it, jt: (h, it[t]))
    oshape = jax.ShapeDtypeStruct((S, MLA_HEADS * LANE), F32)
    return pl.pallas_call(
        body,
        grid_spec=pltpu.PrefetchScalarGridSpec(
            num_scalar_prefetch=2, grid=(MLA_HEADS, itab.shape[0]),
            in_specs=[qspec, kspec, kspec, qspec, lspec, dospec], out_specs=(kspec, kspec),
            scratch_shapes=[pltpu.VMEM((tq, LANE), F32), pltpu.VMEM((tq, LANE), F32)]),
        out_shape=(oshape, oshape), compiler_params=_cp(2), name=name)(itab, jtab, q, k, v, o, lset, dycat)


_SWA_SCALE = 1.0 / math.sqrt(SWA_HD)
_SWA_KW = SWA_KV * LANE


def _swa_specs(S, ts, rev):
    n = S // ts
    t = (lambda i: n - 1 - i) if rev else (lambda i: i)
    hb = lambda i: jnp.maximum(t(i) * (ts // WINDOW) - 1, 0)
    return [
        pl.BlockSpec((ts, SWA_HEADS * LANE), lambda i: (t(i), C_SQ // (SWA_HEADS * LANE))),
        pl.BlockSpec((ts, _SWA_KW), lambda i: (t(i), C_SK // _SWA_KW)),
        pl.BlockSpec((WINDOW, _SWA_KW), lambda i: (hb(i), C_SK // _SWA_KW)),
        pl.BlockSpec((ts, _SWA_KW), lambda i: (t(i), C_SV // _SWA_KW)),
        pl.BlockSpec((WINDOW, _SWA_KW), lambda i: (hb(i), C_SV // _SWA_KW)),
    ]


def _swa_scores(qh, kk, t, b, ts):
    s = _dot_nt(qh, kk) * _SWA_SCALE
    row = lax.broadcasted_iota(jnp.int32, (WINDOW, 2 * WINDOW), 0)
    col = lax.broadcasted_iota(jnp.int32, (WINDOW, 2 * WINDOW), 1)
    rel = WINDOW + row - col
    kpos = t * ts + (b - 1) * WINDOW + col
    return jnp.where((rel >= 0) & (rel < WINDOW) & (kpos >= 0), s, -jnp.inf)


def _swa_fwd(proj, sinks, *, name):
    S = proj.shape[0]
    ts = _tile(S, TS_SWA)
    nb = ts // WINDOW

    def body(q_ref, k_ref, kh_ref, v_ref, vh_ref, sink_ref, o_ref, lse_ref):
        t = pl.program_id(0)
        kext = jnp.concatenate([kh_ref[...], k_ref[...]], axis=0)
        vext = jnp.concatenate([vh_ref[...], v_ref[...]], axis=0)
        for b in range(nb):
            rows = slice(WINDOW * b, WINDOW * (b + 1))
            for h in range(SWA_HEADS):
                kvl = slice(LANE * (h // (SWA_HEADS // SWA_KV)), LANE * (h // (SWA_HEADS // SWA_KV) + 1))
                hl = slice(LANE * h, LANE * (h + 1))
                kk = kext[WINDOW * b:WINDOW * (b + 2), kvl]
                vv = vext[WINDOW * b:WINDOW * (b + 2), kvl]
                s = _swa_scores(q_ref[rows, hl], kk, t, b, ts)
                sk = sink_ref[:, h:h + 1]
                m = jnp.maximum(jnp.max(s, axis=1, keepdims=True), sk)
                p = jnp.exp(s - m)
                den = _rowsum(p) + jnp.exp(sk - m)
                o_ref[rows, hl] = _dot(p, vv) / den
                lse_ref[rows, hl] = jnp.broadcast_to(m + jnp.log(den), (WINDOW, LANE))

    oshape = jax.ShapeDtypeStruct((S, SWA_HEADS * LANE), F32)
    ospec = pl.BlockSpec((ts, SWA_HEADS * LANE), lambda i: (i, 0))
    return pl.pallas_call(
        body, grid=(S // ts,), in_specs=_swa_specs(S, ts, False) + [_vec(LANE)], out_specs=(ospec, ospec),
        out_shape=(oshape, oshape), compiler_params=_cp(1), name=name)(proj, proj, proj, proj, proj, sinks)


def _swa_bwd(proj, o, lse, dycat, sinks, *, name):
    S = proj.shape[0]
    ts = _tile(S, TS_SWA)
    nb = ts // WINDOW
    n = S // ts
    grp = SWA_HEADS // SWA_KV

    def body(q_ref, k_ref, kh_ref, v_ref, vh_ref, o_ref, lse_ref, do_ref, sink_ref,
             dq_ref, dk_ref, dv_ref, dsink_ref, dk_carry, dv_carry):
        i = pl.program_id(0)
        t = n - 1 - i

        @pl.when(i == 0)
        def _():
            dk_carry[...] = jnp.zeros_like(dk_carry)
            dv_carry[...] = jnp.zeros_like(dv_carry)
            dsink_ref[...] = jnp.zeros_like(dsink_ref)

        kext = jnp.concatenate([kh_ref[...], k_ref[...]], axis=0)
        vext = jnp.concatenate([vh_ref[...], v_ref[...]], axis=0)
        lane1 = lax.broadcasted_iota(jnp.int32, (1, LANE), 1)
        dkb = [[jnp.zeros((WINDOW, LANE), F32) for _ in range(SWA_KV)] for _ in range(nb + 1)]
        dvb = [[jnp.zeros((WINDOW, LANE), F32) for _ in range(SWA_KV)] for _ in range(nb + 1)]
        dsink = jnp.zeros((1, LANE), F32)
        for b in range(nb):
            rows = slice(WINDOW * b, WINDOW * (b + 1))
            for h in range(SWA_HEADS):
                kvh = h // grp
                kvl = slice(LANE * kvh, LANE * (kvh + 1))
                hl = slice(LANE * h, LANE * (h + 1))
                kk = kext[WINDOW * b:WINDOW * (b + 2), kvl]
                vv = vext[WINDOW * b:WINDOW * (b + 2), kvl]
                qh = q_ref[rows, hl]
                lse_h = lse_ref[rows, LANE * h:LANE * h + 1]
                p = jnp.exp(_swa_scores(qh, kk, t, b, ts) - lse_h)
                doh = do_ref[rows, hl]
                delta = _rowsum(doh * o_ref[rows, hl])
                ds = p * (_dot_nt(doh, vv) - delta)
                sk = sink_ref[:, h:h + 1]
                dsink = dsink + jnp.where(lane1 == h, -jnp.sum(jnp.exp(sk - lse_h) * delta, keepdims=True), 0.0)
                dq_ref[rows, hl] = (_dot(ds, kk) * _SWA_SCALE).astype(_ACT)
                dkk = _dot_tn(ds, qh) * _SWA_SCALE
                dvv = _dot_tn(p, doh)
                dkb[b][kvh] = dkb[b][kvh] + dkk[:WINDOW]
                dkb[b + 1][kvh] = dkb[b + 1][kvh] + dkk[WINDOW:]
                dvb[b][kvh] = dvb[b][kvh] + dvv[:WINDOW]
                dvb[b + 1][kvh] = dvb[b + 1][kvh] + dvv[WINDOW:]
        dsink_ref[...] += dsink
        for dref, blocks, carry in ((dk_ref, dkb, dk_carry), (dv_ref, dvb, dv_carry)):
            old = carry[...]
            for b in range(1, nb + 1):
                blk = jnp.concatenate(blocks[b], axis=1)
                if b == nb:
                    blk = blk + old
                dref[WINDOW * (b - 1):WINDOW * b, :] = blk.astype(_ACT)
            carry[...] = jnp.concatenate(blocks[0], axis=1)

    hw = SWA_HEADS * LANE
    rev = lambda i: (n - 1 - i, 0)
    mix = lambda i: (n - 1 - i, (SSD_INNER + MLA_HEADS * LANE) // hw)
    return pl.pallas_call(
        body, grid=(n,),
        in_specs=_swa_specs(S, ts, True) + [pl.BlockSpec((ts, hw), rev), pl.BlockSpec((ts, hw), rev),
                                            pl.BlockSpec((ts, hw), mix), _vec(LANE)],
        out_specs=(pl.BlockSpec((ts, hw), rev), pl.BlockSpec((ts, _SWA_KW), rev), pl.BlockSpec((ts, _SWA_KW), rev),
                   _vec(LANE)),
        out_shape=(jax.ShapeDtypeStruct((S, hw), _ACT), jax.ShapeDtypeStruct((S, _SWA_KW), _ACT),
                   jax.ShapeDtypeStruct((S, _SWA_KW), _ACT), jax.ShapeDtypeStruct((1, LANE), F32)),
        scratch_shapes=[pltpu.VMEM((WINDOW, _SWA_KW), F32), pltpu.VMEM((WINDOW, _SWA_KW), F32)],
        compiler_params=_cp(1), name=name)(proj, proj, proj, proj, proj, o, lse, dycat, sinks)


def _exchange(arrays, *, scatter, name):
    n = len(arrays)

    def body(*refs):
        ins, outs = refs[:n], refs[n:2 * n]
        send_sems, recv_sems, loc_sems = refs[2 * n:]
        x, y, c = lax.axis_index("x"), lax.axis_index("y"), lax.axis_index("c")
        me = 4 * x + 2 * y + c

        def src(i, dest):
            return ins[i].at[dest] if scatter else ins[i]

        local = [pltpu.make_async_copy(src(i, me), outs[i].at[me], loc_sems.at[i]) for i in range(n)]
        for cp in local:
            cp.start()
        sends, recvs = [], []
        for k in range(1, NDEV):
            px = 1 - x if k & 4 else x
            py = 1 - y if k & 2 else y
            pc = 1 - c if k & 1 else c
            peer = 4 * px + 2 * py + pc
            for i in range(n):
                common = dict(send_sem=send_sems.at[i, k - 1], recv_sem=recv_sems.at[i, k - 1],
                              device_id=(px, py, pc), device_id_type=pl.DeviceIdType.MESH)
                sends.append(pltpu.make_async_remote_copy(src_ref=src(i, peer), dst_ref=outs[i].at[me], **common))
                recvs.append(pltpu.make_async_remote_copy(src_ref=src(i, peer), dst_ref=outs[i].at[peer], **common))
        for cp in sends:
            cp.start()
        for cp in recvs:
            cp.wait_recv()
        for cp in sends:
            cp.wait_send()
        for cp in local:
            cp.wait()

    hbm = pl.BlockSpec(memory_space=pl.ANY)
    out_shape = tuple(jax.ShapeDtypeStruct(a.shape if scatter else (NDEV,) + a.shape, a.dtype) for a in arrays)
    return pl.pallas_call(
        body, in_specs=[hbm] * n, out_specs=tuple([hbm] * n), out_shape=out_shape,
        scratch_shapes=[pltpu.SemaphoreType.DMA((n, NDEV - 1)), pltpu.SemaphoreType.DMA((n, NDEV - 1)),
                        pltpu.SemaphoreType.DMA((n,))],
        name=name)(*arrays)


def _adamw(w, m, v, parts, *, name):
    R, C = w.shape
    npart = parts.shape[0]
    cap = max(SUB, ((1 << 18) // C) // SUB * SUB)
    tr = _tile(R, cap, SUB)

    def body(w_ref, m_ref, v_ref, p_ref, g_ref, d_ref, mo_ref, vo_ref):
        g = p_ref[0]
        for k in range(1, npart):
            g = g + p_ref[k]
        mn = ADAM_B1 * m_ref[...] + (1.0 - ADAM_B1) * g
        vn = ADAM_B2 * v_ref[...] + (1.0 - ADAM_B2) * (g * g)
        m_hat = mn / (1.0 - ADAM_B1 ** ADAM_STEP)
        v_hat = vn / (1.0 - ADAM_B2 ** ADAM_STEP)
        g_ref[...] = g
        d_ref[...] = -ADAM_LR * (m_hat / (jnp.sqrt(v_hat) + ADAM_EPS) + ADAM_WD * w_ref[...])
        mo_ref[...] = mn
        vo_ref[...] = vn

    spec = pl.BlockSpec((tr, C), lambda i: (i, 0))
    oshape = jax.ShapeDtypeStruct((R, C), F32)
    return pl.pallas_call(
        body, grid=(R // tr,), in_specs=[spec] * 3 + [pl.BlockSpec((npart, tr, C), lambda i: (0, i, 0))],
        out_specs=(spec,) * 4, out_shape=(oshape,) * 4, compiler_params=_cp(1), name=name)(w, m, v, parts)


def _adamw_many(ws, ms, vs, landed, mine, me, *, name):
    n = len(ws)

    def body(me_ref, *refs):
        w_r, m_r, v_r, p_r, o_r = (refs[k * n:(k + 1) * n] for k in range(5))
        outs = refs[5 * n:]
        for i in range(n):
            own = o_r[i][...]
            g = jnp.where(me_ref[0] == 0, own, p_r[i][0])
            for k in range(1, NDEV):
                g = g + jnp.where(me_ref[0] == k, own, p_r[i][k])
            mn = ADAM_B1 * m_r[i][...] + (1.0 - ADAM_B1) * g
            vn = ADAM_B2 * v_r[i][...] + (1.0 - ADAM_B2) * (g * g)
            m_hat = mn / (1.0 - ADAM_B1 ** ADAM_STEP)
            v_hat = vn / (1.0 - ADAM_B2 ** ADAM_STEP)
            outs[4 * i][...] = g
            outs[4 * i + 1][...] = -ADAM_LR * (m_hat / (jnp.sqrt(v_hat) + ADAM_EPS) + ADAM_WD * w_r[i][...])
            outs[4 * i + 2][...] = mn
            outs[4 * i + 3][...] = vn

    vmem = pl.BlockSpec(memory_space=pltpu.VMEM)
    return pl.pallas_call(
        body, in_specs=[pl.BlockSpec(memory_space=pltpu.SMEM)] + [vmem] * (5 * n), out_specs=(vmem,) * (4 * n),
        out_shape=tuple(jax.ShapeDtypeStruct(w.shape, F32) for w in ws for _ in range(4)),
        name=name)(me, *ws, *ms, *vs, *landed, *mine)


def _adamw_layer(l, w, m, v, landed, mine, me, prev, *, name):
    L, R, C = w.shape
    npart = landed.shape[0]
    cap = max(2 * SUB, ((1 << 18) // C) // (2 * SUB) * (2 * SUB))
    tr = _tile(R, cap, 2 * SUB)
    nprev = 0 if prev is None else 4

    def body(me_ref, *refs):
        w_ref, m_ref, v_ref, p_ref, own_ref = refs[:5]
        g_ref, d_ref, mo_ref, vo_ref = refs[5 + nprev:]
        own = own_ref[...].astype(F32)
        g = jnp.where(me_ref[0] == 0, own, p_ref[0].astype(F32))
        for k in range(1, npart):
            g = g + jnp.where(me_ref[0] == k, own, p_ref[k].astype(F32))
        mn = ADAM_B1 * m_ref[...] + (1.0 - ADAM_B1) * g
        vn = ADAM_B2 * v_ref[...] + (1.0 - ADAM_B2) * (g * g)
        m_hat = mn / (1.0 - ADAM_B1 ** ADAM_STEP)
        v_hat = vn / (1.0 - ADAM_B2 ** ADAM_STEP)
        g_ref[...] = g
        d_ref[...] = -ADAM_LR * (m_hat / (jnp.sqrt(v_hat) + ADAM_EPS) + ADAM_WD * w_ref[...])
        mo_ref[...] = mn
        vo_ref[...] = vn

    spec = pl.BlockSpec((None, tr, C), lambda i, me_ref: (l, i, 0))
    oshape = jax.ShapeDtypeStruct((L, R, C), F32)
    return pl.pallas_call(
        body,
        grid_spec=pltpu.PrefetchScalarGridSpec(
            num_scalar_prefetch=1, grid=(R // tr,),
            in_specs=[spec] * 3 + [pl.BlockSpec((npart, tr, C), lambda i, me_ref: (0, i, 0)),
                                   pl.BlockSpec((None, tr, C), lambda i, me_ref: (me_ref[0], i, 0))]
            + [pl.BlockSpec(memory_space=pl.ANY)] * nprev,
            out_specs=(spec,) * 4),
        out_shape=(oshape,) * 4, input_output_aliases={6 + k: k for k in range(nprev)},
        compiler_params=_cp(1), name=name)(me, w, m, v, landed, mine, *(prev or ()))


_HBM = pl.BlockSpec(memory_space=pltpu.HBM)
_SEM = pl.BlockSpec(memory_space=pltpu.SEMAPHORE)
_EFFECT = pltpu.SideEffectType.DATAFLOW_SIDE_EFFECTING


def _peers():
    x, y, c = lax.axis_index("x"), lax.axis_index("y"), lax.axis_index("c")
    out = []
    for k in range(1, NDEV):
        px = 1 - x if k & 4 else x
        py = 1 - y if k & 2 else y
        pc = 1 - c if k & 1 else c
        out.append((k - 1, (px, py, pc), 4 * px + 2 * py + pc))
    return 4 * x + 2 * y + c, out


def _xchg_start(arrays, *, scatter, name):
    n = len(arrays)
    lands = [lax.empty(a.shape if scatter else (NDEV,) + a.shape, a.dtype) for a in arrays]

    def body(*refs):
        ins, lnd = refs[:n], refs[n:2 * n]
        send_sems, recv_sems = refs[2 * n], refs[2 * n + 1]
        token = refs[-1]
        me, peers = _peers()
        for k, dev, peer in peers:
            for i in range(n):
                pltpu.make_async_remote_copy(
                    src_ref=ins[i].at[peer] if scatter else ins[i], dst_ref=lnd[i].at[me],
                    send_sem=send_sems.at[i * (NDEV - 1) + k], recv_sem=recv_sems.at[i * (NDEV - 1) + k],
                    device_id=dev, device_id_type=pl.DeviceIdType.MESH).start()
        token[...] = jnp.zeros_like(token)

    sems = pltpu.SemaphoreType.DMA((n * (NDEV - 1),))
    res = pl.pallas_call(
        body, name=name,
        out_shape=(sems, sems) + tuple(pltpu.HBM(t.shape, t.dtype) for t in list(arrays) + lands)
        + (jax.ShapeDtypeStruct((SUB, LANE), F32),),
        in_specs=[_HBM] * (2 * n), out_specs=(_SEM, _SEM) + (_HBM,) * (2 * n) + (pl.BlockSpec(memory_space=pltpu.VMEM),),
        input_output_aliases={i: 2 + i for i in range(2 * n)},
        compiler_params=pltpu.CompilerParams(has_side_effects=_EFFECT),
    )(*[pltpu.with_memory_space_constraint(t, pltpu.HBM) for t in list(arrays) + lands])
    return dict(send=res[0], recv=res[1], thru=list(res[2:2 + 2 * n]), token=res[-1], scatter=scatter, n=n)


def _xchg_wait(handle, after, *, name):
    n, scatter = handle["n"], handle["scatter"]
    thru = handle["thru"]

    def body(*refs):
        ins, lnd = refs[:n], refs[n:2 * n]
        send_sems, recv_sems = refs[2 * n], refs[2 * n + 1]
        me, peers = _peers()
        for k, dev, peer in peers:
            for i in range(n):
                cp = pltpu.make_async_remote_copy(
                    src_ref=ins[i].at[peer] if scatter else ins[i], dst_ref=lnd[i].at[peer],
                    send_sem=send_sems.at[i * (NDEV - 1) + k], recv_sem=recv_sems.at[i * (NDEV - 1) + k],
                    device_id=dev, device_id_type=pl.DeviceIdType.MESH)
                cp.wait_send()
                cp.wait_recv()

    res = pl.pallas_call(
        body, name=name, out_shape=tuple(pltpu.HBM(t.shape, t.dtype) for t in thru),
        in_specs=[_HBM] * (2 * n) + [_SEM, _SEM, pl.BlockSpec(memory_space=pl.ANY)], out_specs=(_HBM,) * (2 * n),
        input_output_aliases={i: i for i in range(2 * n)},
        compiler_params=pltpu.CompilerParams(has_side_effects=_EFFECT),
    )(*thru, handle["send"], handle["recv"], after)
    return list(res[:n]), list(res[n:])


def _pad_heads(w, nh, hd, axis=-1):
    axis = axis % w.ndim
    shp = w.shape
    w = w.reshape(shp[:axis] + (nh, hd) + shp[axis + 1:])
    pads = [(0, 0)] * w.ndim
    pads[axis + 1] = (0, LANE - hd)
    return jnp.pad(w, pads).reshape(shp[:axis] + (nh * LANE,) + shp[axis + 1:])


def _unpad_heads(w, nh, hd, axis=-1):
    axis = axis % w.ndim
    shp = w.shape
    w = w.reshape(shp[:axis] + (nh, LANE) + shp[axis + 1:])
    w = lax.slice_in_dim(w, 0, hd, axis=axis + 1)
    return w.reshape(shp[:axis] + (nh * hd,) + shp[axis + 1:])


_O_DT = SSD_INNER + SSD_XBC
_O_CQ = _O_DT + SSD_HEADS
_O_CKV = _O_CQ + MLA_QR
_O_KR = _O_CKV + MLA_KVR
_O_SQ = _O_KR + MLA_ROPE
_O_SK = _O_SQ + SWA_HEADS * SWA_HD
_O_SV = _O_SK + SWA_KV * SWA_HD


def _w_in_to_padded(w, axis=-1):
    axis = axis % w.ndim
    cut = lambda a, b: lax.slice_in_dim(w, a, b, axis=axis)
    z, xbc, dt = cut(0, SSD_INNER), cut(SSD_INNER, _O_DT), cut(_O_DT, _O_CQ)
    cq, ckv, kr = cut(_O_CQ, _O_CKV), cut(_O_CKV, _O_KR), cut(_O_KR, _O_SQ)
    sq, sk, sv = cut(_O_SQ, _O_SK), cut(_O_SK, _O_SV), cut(_O_SV, D_IN)
    zeros = lambda n: jnp.zeros(w.shape[:axis] + (n,) + w.shape[axis + 1:], w.dtype)
    return jnp.concatenate([xbc, z, cq, ckv, dt, zeros(ROPE_LANE - SSD_HEADS), kr, zeros(LANE - ROPE_LANE - MLA_ROPE),
                            _pad_heads(sq, SWA_HEADS, SWA_HD, axis), _pad_heads(sk, SWA_KV, SWA_HD, axis),
                            _pad_heads(sv, SWA_KV, SWA_HD, axis)], axis=axis)


def _w_in_from_padded(g, axis=-1):
    axis = axis % g.ndim
    cut = lambda a, b: lax.slice_in_dim(g, a, b, axis=axis)
    xbc, z, cq, ckv = cut(C_XBC, C_Z), cut(C_Z, C_CQ), cut(C_CQ, C_CKV), cut(C_CKV, C_MISC)
    dt, kr = cut(C_MISC, C_MISC + SSD_HEADS), cut(C_MISC + ROPE_LANE, C_MISC + ROPE_LANE + MLA_ROPE)
    sq = _unpad_heads(cut(C_SQ, C_SK), SWA_HEADS, SWA_HD, axis)
    sk = _unpad_heads(cut(C_SK, C_SV), SWA_KV, SWA_HD, axis)
    sv = _unpad_heads(cut(C_SV, D_INP), SWA_KV, SWA_HD, axis)
    return jnp.concatenate([z, xbc, dt, cq, ckv, kr, sq, sk, sv], axis=axis)


def _w_out_to_padded(w):
    a = SSD_INNER
    b = a + MLA_HEADS * MLA_V
    return jnp.concatenate([w[..., :a, :], _pad_heads(w[..., a:b, :], MLA_HEADS, MLA_V, axis=-2),
                            _pad_heads(w[..., b:, :], SWA_HEADS, SWA_HD, axis=-2)], axis=-2)


def _w_out_from_padded(g):
    a = SSD_INNER
    b = a + MLA_HEADS * LANE
    return jnp.concatenate([g[..., :a, :], _unpad_heads(g[..., a:b, :], MLA_HEADS, MLA_V, axis=-2),
                            _unpad_heads(g[..., b:, :], SWA_HEADS, SWA_HD, axis=-2)], axis=-2)


def _w_ukv_to_padded(w):
    w4 = w.reshape(w.shape[:-1] + (MLA_HEADS, MLA_NOPE + MLA_V))
    flat = lambda t: t.reshape(w.shape[:-1] + (MLA_HEADS * t.shape[-1],))
    return jnp.concatenate([_pad_heads(flat(w4[..., :MLA_NOPE]), MLA_HEADS, MLA_NOPE),
                            _pad_heads(flat(w4[..., MLA_NOPE:]), MLA_HEADS, MLA_V)], axis=-1)


def _w_ukv_from_padded(g):
    hw = MLA_HEADS * LANE
    gk = _unpad_heads(g[..., :hw], MLA_HEADS, MLA_NOPE).reshape(g.shape[:-1] + (MLA_HEADS, MLA_NOPE))
    gv = _unpad_heads(g[..., hw:], MLA_HEADS, MLA_V).reshape(g.shape[:-1] + (MLA_HEADS, MLA_V))
    return jnp.concatenate([gk, gv], axis=-1).reshape(g.shape[:-1] + (MLA_HEADS * (MLA_NOPE + MLA_V),))


def _pad_lane(v):
    return jnp.pad(v, [(0, 0)] * (v.ndim - 1) + [(0, LANE - v.shape[-1])])


def _rope_tables(positions):
    inv_freq = ROPE_THETA ** (-jnp.arange(0, MLA_ROPE, 2, dtype=F32) / MLA_ROPE)
    ang = positions.astype(F32).reshape(-1, 1) * inv_freq
    cos, sin = jnp.cos(ang), jnp.sin(ang)
    S = ang.shape[0]
    one, zero = jnp.ones((S, ROPE_LANE), F32), jnp.zeros((S, ROPE_LANE), F32)
    tail1, tail0 = jnp.ones((S, LANE - ROPE_LANE - MLA_ROPE), F32), jnp.zeros((S, LANE - ROPE_LANE - MLA_ROPE), F32)
    z16 = jnp.zeros_like(sin)
    return (jnp.concatenate([one, cos, cos, tail1], axis=1), jnp.concatenate([zero, -sin, z16, tail0], axis=1),
            jnp.concatenate([zero, z16, sin, tail0], axis=1))


def _layer_fwd(l, x_in, f_prev, gate_prev, mod, P, tabs):
    sh1, sc1, g1, sh2, sc2, g2 = [mod[k:k + 1] for k in range(6)]
    tag = f"l{l}_"
    if f_prev is None:
        x0 = x_in
        h1 = _norm_fwd(x0, P["n1g"], sc1, sh1, name=tag + "norm1")
    else:
        x0, h1 = _norm_fwd(x_in, P["n1g"], sc1, sh1, f=f_prev, gate=gate_prev, name=tag + "norm1")
    proj = _mm(h1, P["w_in"], tb=True, name=tag + "proj")
    P.update(P.pop("mid")(proj))
    y_ssd, hin = _ssd_fwd(proj, P["ssd_cw"], P["ssd_cb"], P["dtb"], P["alog"], P["dsk"],
                          P["ssd_ng"], name=tag + "ssd")
    q, k, v, vt = _mla_prep_fwd(proj, *tabs, P["gq"], P["gkv"], P["w_uq"], P["w_ukv"], name=tag + "mla_prep")
    o_mla, lse_mla, lset_mla = _attn_fwd(q, k, vt, name=tag + "mla_attn")
    o_swa, lse_swa = _swa_fwd(proj, P["sinks"], name=tag + "swa")
    ycat = jnp.concatenate([y_ssd.astype(_ACT), o_mla.astype(_ACT), o_swa.astype(_ACT)], axis=1)
    y = _mm(ycat, P["w_out"], name=tag + "out")
    P.update(P.pop("late")(y))
    x1, h2 = _norm_fwd(x0, P["n2g"], sc2, sh2, f=y, gate=g1, name=tag + "norm2")
    up = _mm(h2, P["w_up"], tb=True, name=tag + "up")
    act = _ffn_act_fwd(up, P["fcw"], P["fcb"], name=tag + "ffn_act")
    f = _mm(act, P["w_down"], name=tag + "down")
    saved = dict(x0=x0, h1=h1, proj=proj, hin=hin, q=q, k=k, v=v, o_mla=o_mla, lse_mla=lse_mla, lset_mla=lset_mla, o_swa=o_swa,
                 lse_swa=lse_swa, ycat=ycat, y=y, x1=x1, h2=h2, up=up, act=act, f=f, mod=mod)
    return x1, f, g2, saved


def _layer_bwd(l, dxo, ffn_branch, sv, P, tabs, on_part, below):
    mod = sv["mod"]
    sh1, sc1, g1, sh2, sc2, g2 = [mod[k:k + 1] for k in range(6)]
    tag = f"l{l}_b_"
    G = {}
    df, dg2 = ffn_branch
    dact = _mm(df, P["w_down"], tb=True, name=tag + "dact")
    G["w_down"] = _mm(sv["act"], df, ta=True, out_dtype=_ACT, name=tag + "dw_down")
    dup, G["fcw"], G["fcb"] = _ffn_bwd(sv["up"], dact, P["fcw"], P["fcb"], name=tag + "ffn")
    dh2 = _mm(dup, P["w_up"], name=tag + "dh2")
    G["w_up"] = _mm(dup, sv["h2"], ta=True, out_dtype=_ACT, name=tag + "dw_up")
    token = on_part(l, "ffn", G)
    if token is not None:
        sc2 = sc2 + token
    dx1, G["n2g"], dsc2, dsh2, dy, dg1 = _norm_bwd(dh2, sv["x1"], dxo, P["n2g"], sc2, branch=(sv["y"], g1),
                                                   name=tag + "norm2")
    dycat = _mm(dy, P["w_out"], tb=True, name=tag + "dycat")
    G["w_out"] = _mm(sv["ycat"], dy, ta=True, out_dtype=_ACT, name=tag + "dw_out")
    token = on_part(l, "out", G)
    ssd_cb = P["ssd_cb"] if token is None else P["ssd_cb"] + token
    proj = sv["proj"]
    (dpre, dz, dmisc_ssd, G["ssd_cw"], G["ssd_cb"], G["dtb"], G["alog"], G["dsk"], G["ssd_ng"]) = _ssd_bwd(
        proj, dycat, sv["hin"], P["ssd_cw"], ssd_cb, P["dtb"], P["alog"], P["dsk"],
        P["ssd_ng"], name=tag + "ssd")
    att = (sv["q"], sv["k"], sv["v"], sv["o_mla"])
    dq = _attn_bwd_dq(*att, sv["lse_mla"], dycat, name=tag + "mla_dq")
    dk, dv = _attn_bwd_dkv(*att, sv["lset_mla"], dycat, name=tag + "mla_dkv")
    dcq, dckv, dmisc, dqh, dkv, qn, kvn, G["gq"], G["gkv"] = _mla_prep_bwd(
        proj, dq, dk, dv, dmisc_ssd, *tabs, P["gq"], P["gkv"], P["w_uq"], P["w_ukv"], name=tag + "mla_prep")
    G["w_uq"] = _mm(qn, dqh, ta=True, out_dtype=_ACT, name=tag + "dw_uq")
    G["w_ukv"] = _mm(kvn, dkv, ta=True, out_dtype=_ACT, name=tag + "dw_ukv")
    dsq, dsk_, dsv_, G["sinks"] = _swa_bwd(proj, sv["o_swa"], sv["lse_swa"], dycat, P["sinks"], name=tag + "swa")
    dproj = jnp.concatenate([dpre, dz, dcq, dckv, dmisc, dsq, dsk_, dsv_], axis=1)
    G["w_in"] = _mm(dproj, sv["h1"], ta=True, out_dtype=_ACT, name=tag + "dw_in")
    token = on_part(l, "mixer", G)
    if token is not None:
        sc1 = sc1 + token
    dh1 = _mm(dproj, P["w_in"], name=tag + "dh1")
    res = _norm_bwd(dh1, sv["x0"], dx1, P["n1g"], sc1, branch=below, name=tag + "norm1")
    dx0, G["n1g"], dsc1, dsh1 = res[:4]
    G["mod"] = jnp.concatenate([dsh1, dsc1, dg1, dsh2, dsc2, dg2], axis=0)
    return dx0, G, (res[4:] or None)


def _local_step(x, tgt, mods, get_params, tabs, final_g, on_grads, on_part):
    saved, params = [], []
    xin, f, gate = x, None, None
    for l in range(DEPTH):
        params.append(get_params(l, x if f is None else f))
        xin, f, gate, sv = _layer_fwd(l, xin, f, gate, mods[l], params[l], tabs)
        saved.append(sv)
    loss, dx, dfinal, df, dgate = _final_loss(xin, f, gate, final_g, tgt, name="final_loss")
    branch = (df, dgate)
    for l in reversed(range(DEPTH)):
        below = (saved[l - 1]["f"], saved[l - 1]["mod"][5:6]) if l > 0 else None
        dx, G, branch = _layer_bwd(l, dx, branch, saved[l], params[l], tabs, on_part, below)
        on_grads(l, G)
    return loss[0, 0], dx, dfinal


_WEIGHTS = ['ada_w', 'ada_b', 'norm1_g', 'norm2_g', 'w_in', 'ssd_conv_w', 'ssd_conv_b', 'ssd_dt_bias', 'ssd_a_log',
            'ssd_d', 'ssd_norm_g', 'mla_q_norm_g', 'mla_w_uq', 'mla_kv_norm_g', 'mla_w_ukv', 'swa_sinks', 'w_out',
            'ffn_w_up', 'ffn_conv_w', 'ffn_conv_b', 'ffn_w_down', 'final_norm_g']
_INPUTS = ['x', 'c', 'positions'] + _WEIGHTS + ['loss_target'] + ['m_' + n for n in _WEIGHTS] + ['v_' + n for n in _WEIGHTS]
_SMALL = [('ada_b', 'mod'), ('norm1_g', 'n1g'), ('norm2_g', 'n2g'), ('ssd_conv_b', 'ssd_cb'), ('ssd_dt_bias', 'dtb'),
          ('ssd_a_log', 'alog'), ('ssd_d', 'dsk'), ('ssd_norm_g', 'ssd_ng'), ('mla_q_norm_g', 'gq'),
          ('mla_kv_norm_g', 'gkv'), ('swa_sinks', 'sinks'), ('ffn_conv_b', 'fcb')]
_SHARDED = [('w_in', 'w_in', 2), ('ssd_conv_w', 'ssd_cw', 2), ('mla_w_uq', 'w_uq', 2), ('mla_w_ukv', 'w_ukv', 2),
            ('w_out', 'w_out', 1), ('ffn_w_up', 'w_up', 2), ('ffn_conv_w', 'fcw', 2), ('ffn_w_down', 'w_down', 1)]
_SHARDED_NAMES = [n for n, _, _ in _SHARDED]
_TRANSPOSED = ('w_in', 'ffn_w_up')


def _shard_major(g, axis):
    shp = g.shape
    g = g.reshape(shp[:axis] + (NDEV, shp[axis] // NDEV) + shp[axis + 1:])
    return jnp.moveaxis(g, axis, 0)


def _unshard(g, axis):
    g = jnp.moveaxis(g, 0, axis)
    shp = g.shape
    return g.reshape(shp[:axis] + (shp[axis] * shp[axis + 1],) + shp[axis + 2:])


def kernel(x, c, positions, ada_w, ada_b, norm1_g, norm2_g, w_in, ssd_conv_w, ssd_conv_b, ssd_dt_bias, ssd_a_log, ssd_d, ssd_norm_g, mla_q_norm_g, mla_w_uq, mla_kv_norm_g, mla_w_ukv, swa_sinks, w_out, ffn_w_up, ffn_conv_w, ffn_conv_b, ffn_w_down, final_norm_g, loss_target, m_ada_w, m_ada_b, m_norm1_g, m_norm2_g, m_w_in, m_ssd_conv_w, m_ssd_conv_b, m_ssd_dt_bias, m_ssd_a_log, m_ssd_d, m_ssd_norm_g, m_mla_q_norm_g, m_mla_w_uq, m_mla_kv_norm_g, m_mla_w_ukv, m_swa_sinks, m_w_out, m_ffn_w_up, m_ffn_conv_w, m_ffn_conv_b, m_ffn_w_down, m_final_norm_g, v_ada_w, v_ada_b, v_norm1_g, v_norm2_g, v_w_in, v_ssd_conv_w, v_ssd_conv_b, v_ssd_dt_bias, v_ssd_a_log, v_ssd_d, v_ssd_norm_g, v_mla_q_norm_g, v_mla_w_uq, v_mla_kv_norm_g, v_mla_w_ukv, v_swa_sinks, v_w_out, v_ffn_w_up, v_ffn_conv_w, v_ffn_conv_b, v_ffn_w_down, v_final_norm_g):
    a = dict(zip(_INPUTS, (x, c, positions, ada_w, ada_b, norm1_g, norm2_g, w_in, ssd_conv_w, ssd_conv_b, ssd_dt_bias, ssd_a_log, ssd_d, ssd_norm_g, mla_q_norm_g, mla_w_uq, mla_kv_norm_g, mla_w_ukv, swa_sinks, w_out, ffn_w_up, ffn_conv_w, ffn_conv_b, ffn_w_down, final_norm_g, loss_target, m_ada_w, m_ada_b, m_norm1_g, m_norm2_g, m_w_in, m_ssd_conv_w, m_ssd_conv_b, m_ssd_dt_bias, m_ssd_a_log, m_ssd_d, m_ssd_norm_g, m_mla_q_norm_g, m_mla_w_uq, m_mla_kv_norm_g, m_mla_w_ukv, m_swa_sinks, m_w_out, m_ffn_w_up, m_ffn_conv_w, m_ffn_conv_b, m_ffn_w_down, m_final_norm_g, v_ada_w, v_ada_b, v_norm1_g, v_norm2_g, v_w_in, v_ssd_conv_w, v_ssd_conv_b, v_ssd_dt_bias, v_ssd_a_log, v_ssd_d, v_ssd_norm_g, v_mla_q_norm_g, v_mla_w_uq, v_mla_kv_norm_g, v_mla_w_ukv, v_swa_sinks, v_w_out, v_ffn_w_up, v_ffn_conv_w, v_ffn_conv_b, v_ffn_w_down, v_final_norm_g)))
    axes = ("x", "y", "c")
    me = 4 * lax.axis_index("x") + 2 * lax.axis_index("y") + lax.axis_index("c")
    ncol = ada_w.shape[-1]

    kform = lambda n, t: jnp.swapaxes(t, -1, -2) if n in _TRANSPOSED else t
    mxu_names = ('w_in', 'mla_w_uq', 'mla_w_ukv', 'w_out', 'ffn_w_up', 'ffn_w_down')
    gather_groups = (("early", _SHARDED_NAMES[:4]), ("mid", _SHARDED_NAMES[4:5]), ("late", _SHARDED_NAMES[5:]))

    def own_of(src, names, l):
        return [kform(n, src[n][l]).astype(_MXU) if n in mxu_names else src[n][l] for n in names]

    first_gather = _xchg_start(own_of(a, gather_groups[0][1], 0), scatter=False, name="gather_start_early0")

    c_all = _exchange([c + first_gather["token"][0, 0]], scatter=False, name="gather_c")[0]
    c_act = _silu_call(c_all.reshape(NDEV, D), name="c_act")
    mod_part = jnp.stack([_mm(c_act, ada_w[l], name=f"mod{l}") for l in range(DEPTH)])
    mod_all = _exchange([mod_part], scatter=False, name="gather_mod")[0]
    mod_mine = lax.dynamic_index_in_dim(mod_all, me, axis=2, keepdims=False)
    mods = (jnp.moveaxis(mod_mine, 0, 1).reshape(DEPTH, 6 * D) + ada_b).reshape(DEPTH, 6, D)
    tabs = _rope_tables(positions)

    shard_of = {n: (key, 1 if n in _TRANSPOSED else ax) for n, key, ax in _SHARDED}
    mods, raw = lax.optimization_barrier((mods, {n: a[n] for n in _SHARDED_NAMES}))
    groups_of = lambda l, groups: groups if l == 0 else (("all", sum((list(n) for _, n in groups), [])),)
    gathers, prev = [], first_gather["token"]
    for l in range(DEPTH):
        gathers.append({})
        for grp, names in groups_of(l, gather_groups):
            if (l, grp) == (0, "early"):
                gathers[l][grp] = first_gather
                continue
            srcs, _ = lax.optimization_barrier((own_of(raw, names, l), prev))
            gathers[l][grp] = _xchg_start(srcs, scatter=False, name=f"gather_start_{grp}{l}")
            prev = gathers[l][grp]["token"]

    def place_own(landed, mine, in_place=()):
        slot = lambda t: lax.broadcasted_iota(jnp.int32, (NDEV,) + (1,) * (t.ndim - 1), 0)
        return [lax.dynamic_update_index_in_dim(t, o, me, 0) if k in in_place else jnp.where(slot(t) == me, o[None], t)
                for k, (t, o) in enumerate(zip(landed, mine))]

    as_is = ('ssd_conv_w', 'ffn_w_up', 'ffn_conv_w', 'ffn_w_down')

    def gathered(l, grp, after):
        names = dict(groups_of(l, gather_groups))[grp]
        mine, landed = _xchg_wait(gathers[l][grp], after, name=f"gather_wait_{grp}{l}")
        full = place_own(landed, mine, [k for k, n in enumerate(names) if n in as_is])
        return {n: _unshard(g, shard_of[n][1] - 1) for n, g in zip(names, full)}

    def get_params(l, after):
        full = gathered(l, "early", mods) if l == 0 else gathered(l, "all", after)
        vec = lambda t: t[l].reshape(1, -1)

        def mid(after2):
            return dict(w_out=_w_out_to_padded((gathered(l, "mid", after2) if l == 0 else full)['w_out']))

        def late(after2):
            rest = gathered(l, "late", after2) if l == 0 else full
            return dict(w_up=rest['ffn_w_up'], w_down=rest['ffn_w_down'], fcw=rest['ffn_conv_w'])

        return dict(
            w_in=_w_in_to_padded(full['w_in'], axis=0), w_uq=_pad_heads(full['mla_w_uq'], MLA_HEADS, MLA_NOPE + MLA_ROPE),
            w_ukv=_w_ukv_to_padded(full['mla_w_ukv']), ssd_cw=full['ssd_conv_w'], mid=mid, late=late,
            ssd_cb=vec(ssd_conv_b), dtb=vec(_pad_lane(ssd_dt_bias)), alog=vec(_pad_lane(ssd_a_log)),
            dsk=vec(_pad_lane(ssd_d)), ssd_ng=vec(ssd_norm_g), gq=vec(mla_q_norm_g), gkv=vec(mla_kv_norm_g),
            sinks=vec(_pad_lane(swa_sinks)), fcb=vec(ffn_conv_b), n1g=vec(norm1_g), n2g=vec(norm2_g))

    unpad = dict(w_in=functools.partial(_w_in_from_padded, axis=0), w_out=_w_out_from_padded, w_ukv=_w_ukv_from_padded,
                 w_uq=lambda g: _unpad_heads(g, MLA_HEADS, MLA_NOPE + MLA_ROPE))
    scatter_groups = (("ffn", _SHARDED_NAMES[5:]), ("out", _SHARDED_NAMES[4:5]), ("mixer", _SHARDED_NAMES[:4]))
    grads = [None] * DEPTH
    scatters = [dict() for _ in range(DEPTH)]

    def on_part(l, grp, G):
        if l > 0:
            if grp != scatter_groups[-1][0]:
                return None
            grp = "all"
        parts = [_shard_major(unpad.get(shard_of[n][0], lambda g: g)(G[shard_of[n][0]]), shard_of[n][1] - 1).astype(_ACT)
                 for n in dict(groups_of(l, scatter_groups))[grp]]
        scatters[l][grp] = _xchg_start(parts, scatter=True, name=f"scatter_start_{grp}{l}")
        return scatters[l][grp]["token"][0, 0]

    def on_grads(l, G):
        grads[l] = G

    mods = mods + sum(h["token"][0, 0] for g in gathers for h in g.values())
    loss, dx, dfinal = _local_step(x[0], loss_target[0], mods, get_params, tabs, final_norm_g.reshape(1, D),
                                   on_grads, on_part)
    loss = lax.psum(loss, axes)

    stack = lambda key: jnp.stack([grads[l][key] for l in range(DEPTH)])
    small_names = [n for n, _ in _SMALL] + ['final_norm_g']
    small_g = [stack(key).reshape(DEPTH, -1)[:, :a[name].shape[1]] for name, key in _SMALL] + [dfinal]
    small_gather = _xchg_start(small_g, scatter=False, name="gather_small_start")

    out_g, out_d, out_m, out_v = {}, {}, {}, {}
    chain = {name: None for name in _SHARDED_NAMES}
    me_arr = jnp.reshape(me, (1,)).astype(jnp.int32)
    after = small_gather["token"]
    for l in reversed(range(DEPTH)):
        for grp, names in groups_of(l, scatter_groups):
            mine, landed = _xchg_wait(scatters[l][grp], after, name=f"scatter_wait_{grp}{l}")
            for name, own, got in zip(names, mine, landed):
                chain[name] = _adamw_layer(l, kform(name, a[name]), kform(name, a['m_' + name]),
                                           kform(name, a['v_' + name]), got, own, me_arr, chain[name],
                                           name=f"adamw_{name}{l}")
    for name in _SHARDED_NAMES:
        out_g[name], out_d[name], out_m[name], out_v[name] = [kform(name, t) for t in chain[name]]
    small_mine, small_landed = _xchg_wait(small_gather, chain[_SHARDED_NAMES[0]][0], name="gather_small_wait")
    row = lambda t: t.reshape(1, -1) if t.ndim == 1 else t
    res = _adamw_many([row(a[n]) for n in small_names], [row(a['m_' + n]) for n in small_names],
                      [row(a['v_' + n]) for n in small_names], small_landed, small_mine, me_arr, name="adamw_small")
    for i, n in enumerate(small_names):
        out_g[n], out_d[n], out_m[n], out_v[n] = [t.reshape(a[n].shape) for t in res[4 * i:4 * i + 4]]

    dmod_all = place_own(small_landed[:1], small_mine[:1])[0]
    dmod_mine = lax.dynamic_slice_in_dim(dmod_all, me * ncol, ncol, axis=2)
    g_ada = jnp.stack([_mm(c_act, dmod_mine[:, l], ta=True, name=f"dw_ada{l}") for l in range(DEPTH)])
    shp = ada_w.shape
    res = _adamw(*[t.reshape(-1, shp[-1]) for t in (ada_w, m_ada_w, v_ada_w)], g_ada.reshape(1, -1, shp[-1]),
                 name="adamw_ada_w")
    out_g['ada_w'], out_d['ada_w'], out_m['ada_w'], out_v['ada_w'] = [t.reshape(shp) for t in res]

    outs = [loss, dx[None]]
    for dct in (out_g, out_d, out_m, out_v):
        outs += [dct[n] for n in _WEIGHTS]
    return tuple(outs)
```
